```python
import jax, jax.numpy as jnp
from jax import lax
import numpy as np

D_MODEL = 1024
BATCH = 8
SEQ = 8192
DEPTH = 2

EPS = 1e-6
ROPE_THETA = 10000.0
BLOCK = 128
MLA_HEADS = 4
MLA_Q_RANK = 256
MLA_KV_RANK = 128
MLA_NOPE = 128
MLA_ROPE = 64
MLA_V = 128
MLA_QK = MLA_NOPE + MLA_ROPE
MLA_WIDTH = MLA_HEADS * MLA_V
SWA_HEADS = 8
SWA_KV_HEADS = 2
SWA_HEAD_DIM = 64
WINDOW = 128
SWA_WIDTH = SWA_HEADS * SWA_HEAD_DIM
MIX_WIDTH = MLA_WIDTH + SWA_WIDTH
IN_SPLITS = (MLA_Q_RANK, MLA_KV_RANK, MLA_ROPE,
             SWA_HEADS * SWA_HEAD_DIM, SWA_KV_HEADS * SWA_HEAD_DIM, SWA_KV_HEADS * SWA_HEAD_DIM)
IN_COLS = sum(IN_SPLITS)
D_FF = 2816

kernel_name = "hybrid_mla_swa_sink_macaron"


def rmsnorm(t, g):
    tf = t.astype(jnp.float32)
    out = tf * lax.rsqrt(jnp.mean(tf * tf, axis=-1, keepdims=True) + EPS)
    return (out * g.astype(jnp.float32)).astype(t.dtype)


def swiglu(t, w_gate, w_up, w_down):
    return (jax.nn.silu(t @ w_gate) * (t @ w_up)) @ w_down


def rope_table(seq, dim):
    pos = jnp.arange(seq, dtype=jnp.float32)
    inv = 1.0 / (ROPE_THETA ** (jnp.arange(0, dim, 2, dtype=jnp.float32) / dim))
    ang = pos[:, None] * inv[None, :]
    return jnp.cos(ang), jnp.sin(ang)


def apply_rope(t, cos, sin):
    half = t.shape[-1] // 2
    t1 = t[..., :half].astype(jnp.float32)
    t2 = t[..., half:].astype(jnp.float32)
    c = cos[None, :, None, :]
    s = sin[None, :, None, :]
    return jnp.concatenate([t1 * c - t2 * s, t2 * c + t1 * s], axis=-1).astype(t.dtype)


def dense_causal_attention(q, k, v, scale):
    B, S, H, Dq = q.shape
    nb = S // BLOCK
    qb = q.reshape(B, nb, BLOCK, H, Dq).transpose(1, 0, 2, 3, 4)
    k_pos = jnp.arange(S)

    def one_block(args):
        q_blk, i = args
        s = jnp.einsum('bqhd,bkhd->bhqk', q_blk, k, preferred_element_type=jnp.float32) * scale
        q_pos = i * BLOCK + jnp.arange(BLOCK)
        mask = k_pos[None, :] <= q_pos[:, None]
        s = jnp.where(mask[None, None], s, -jnp.inf)
        p = jax.nn.softmax(s, axis=-1)
        return jnp.einsum('bhqk,bkhd->bqhd', p.astype(v.dtype), v)

    out = lax.map(one_block, (qb, jnp.arange(nb)))
    return out.transpose(1, 0, 2, 3, 4).reshape(B, S, H, v.shape[-1])


def sliding_window_sink_attention(q, k, v, sinks, scale):
    B, S, H, D = q.shape
    KV = k.shape[2]
    G = H // KV
    nb = S // BLOCK
    qb = q.reshape(B, nb, BLOCK, KV, G, D)

    def band(t):
        tb = t.reshape(B, nb, BLOCK, KV, D)
        prev = jnp.pad(tb[:, :-1], ((0, 0), (1, 0), (0, 0), (0, 0), (0, 0)))
        return jnp.concatenate([prev, tb], axis=2)

    kb, vb = band(k), band(v)
    s = jnp.einsum('bnqcgd,bnkcd->bncgqk', qb, kb, preferred_element_type=jnp.float32) * scale
    q_rel = jnp.arange(BLOCK)[:, None] + BLOCK
    k_rel = jnp.arange(2 * BLOCK)[None, :]
    dist = q_rel - k_rel
    in_window = (dist >= 0) & (dist < WINDOW)
    k_abs = jnp.arange(nb)[:, None, None] * BLOCK + k_rel[None] - BLOCK
    valid = in_window[None] & (k_abs >= 0)
    s = jnp.where(valid[None, :, None, None], s, -jnp.inf)
    sink = sinks.astype(jnp.float32).reshape(KV, G)[None, None, :, :, None, None]
    m = jnp.maximum(jnp.max(s, axis=-1, keepdims=True), sink)
    e = jnp.exp(s - m)
    p = e / (jnp.sum(e, axis=-1, keepdims=True) + jnp.exp(sink - m))
    out = jnp.einsum('bncgqk,bnkcd->bnqcgd', p.astype(v.dtype), vb)
    return out.reshape(B, S, H, D)


def _fwd_setup_inputs(seed: int = 0) -> dict:
    key = jax.random.key(seed)
    ks = iter(jax.random.split(key, 32))

    def w(shape, fan_in):
        return jax.random.normal(next(ks), shape, jnp.float32) * (fan_in ** -0.5)

    def gain(shape):
        return 1.0 + 0.02 * jax.random.normal(next(ks), shape, jnp.float32)

    L = DEPTH
    return {
        "x": jax.random.normal(next(ks), (BATCH, SEQ, D_MODEL), jnp.float32),
        "ffn1_norm": gain((L, D_MODEL)),
        "ffn1_w_gate": w((L, D_MODEL, D_FF), D_MODEL),
        "ffn1_w_up": w((L, D_MODEL, D_FF), D_MODEL),
        "ffn1_w_down": w((L, D_FF, D_MODEL), D_FF),
        "mix_norm": gain((L, D_MODEL)),
        "w_in": w((L, D_MODEL, IN_COLS), D_MODEL),
        "mla_q_a_norm": gain((L, MLA_Q_RANK)),
        "mla_w_q_b": w((L, MLA_Q_RANK, MLA_HEADS * MLA_QK), MLA_Q_RANK),
        "mla_kv_a_norm": gain((L, MLA_KV_RANK)),
        "mla_w_kv_b": w((L, MLA_KV_RANK, MLA_HEADS * (MLA_NOPE + MLA_V)), MLA_KV_RANK),
        "mla_q_norm": gain((L, MLA_QK)),
        "mla_k_norm": gain((L, MLA_QK)),
        "swa_q_norm": gain((L, SWA_HEAD_DIM)),
        "swa_k_norm": gain((L, SWA_HEAD_DIM)),
        "swa_sinks": 0.5 * jax.random.normal(next(ks), (L, SWA_HEADS), jnp.float32),
        "mla_out_norm": gain((L, MLA_WIDTH)),
        "swa_out_norm": gain((L, SWA_WIDTH)),
        "w_o": w((L, MIX_WIDTH, D_MODEL), MIX_WIDTH),
        "ffn2_norm": gain((L, D_MODEL)),
        "ffn2_w_gate": w((L, D_MODEL, D_FF), D_MODEL),
        "ffn2_w_up": w((L, D_MODEL, D_FF), D_MODEL),
        "ffn2_w_down": w((L, D_FF, D_MODEL), D_FF),
    }


def _fwd_reference(x, ffn1_norm, ffn1_w_gate, ffn1_w_up, ffn1_w_down, mix_norm, w_in,
              mla_q_a_norm, mla_w_q_b, mla_kv_a_norm, mla_w_kv_b, mla_q_norm, mla_k_norm,
              swa_q_norm, swa_k_norm, swa_sinks, mla_out_norm, swa_out_norm, w_o,
              ffn2_norm, ffn2_w_gate, ffn2_w_up, ffn2_w_down):
    B, S, _ = x.shape
    cos, sin = rope_table(S, MLA_ROPE)
    split_idx = np.cumsum(IN_SPLITS)[:-1].tolist()
    mla_scale = MLA_QK ** -0.5
    swa_scale = SWA_HEAD_DIM ** -0.5

    for l in range(DEPTH):
        x = x + 0.5 * swiglu(rmsnorm(x, ffn1_norm[l]), ffn1_w_gate[l], ffn1_w_up[l], ffn1_w_down[l])

        h = rmsnorm(x, mix_norm[l])
        c_q, c_kv, k_pe, q_s, k_s, v_s = jnp.split(h @ w_in[l], split_idx, axis=-1)

        q_a = (rmsnorm(c_q, mla_q_a_norm[l]) @ mla_w_q_b[l]).reshape(B, S, MLA_HEADS, MLA_QK)
        kv_a = (rmsnorm(c_kv, mla_kv_a_norm[l]) @ mla_w_kv_b[l]).reshape(B, S, MLA_HEADS, MLA_NOPE + MLA_V)
        k_nope, v_a = kv_a[..., :MLA_NOPE], kv_a[..., MLA_NOPE:]
        k_a = jnp.concatenate(
            [k_nope, jnp.broadcast_to(k_pe[:, :, None, :], (B, S, MLA_HEADS, MLA_ROPE))], axis=-1)
        q_a = rmsnorm(q_a, mla_q_norm[l])
        k_a = rmsnorm(k_a, mla_k_norm[l])
        q_a = jnp.concatenate([q_a[..., :MLA_NOPE], apply_rope(q_a[..., MLA_NOPE:], cos, sin)], axis=-1)
        k_a = jnp.concatenate([k_a[..., :MLA_NOPE], apply_rope(k_a[..., MLA_NOPE:], cos, sin)], axis=-1)
        out_a = dense_causal_attention(q_a, k_a, v_a, mla_scale).reshape(B, S, MLA_WIDTH)

        q_b = rmsnorm(q_s.reshape(B, S, SWA_HEADS, SWA_HEAD_DIM), swa_q_norm[l])
        k_b = rmsnorm(k_s.reshape(B, S, SWA_KV_HEADS, SWA_HEAD_DIM), swa_k_norm[l])
        v_b = v_s.reshape(B, S, SWA_KV_HEADS, SWA_HEAD_DIM)
        q_b = apply_rope(q_b, cos, sin)
        k_b = apply_rope(k_b, cos, sin)
        out_b = sliding_window_sink_attention(q_b, k_b, v_b, swa_sinks[l], swa_scale).reshape(B, S, SWA_WIDTH)

        mixed = jnp.concatenate([rmsnorm(out_a, mla_out_norm[l]), rmsnorm(out_b, swa_out_norm[l])], axis=-1)
        x = x + mixed @ w_o[l]

        x = x + 0.5 * swiglu(rmsnorm(x, ffn2_norm[l]), ffn2_w_gate[l], ffn2_w_up[l], ffn2_w_down[l])
    return x


import jax as _jax
import jax.numpy as _jnp

TWIN_FORMAT = 'train_step'
FWD_PARAMS = ['x', 'ffn1_norm', 'ffn1_w_gate', 'ffn1_w_up', 'ffn1_w_down', 'mix_norm', 'w_in', 'mla_q_a_norm', 'mla_w_q_b', 'mla_kv_a_norm', 'mla_w_kv_b', 'mla_q_norm', 'mla_k_norm', 'swa_q_norm', 'swa_k_norm', 'swa_sinks', 'mla_out_norm', 'swa_out_norm', 'w_o', 'ffn2_norm', 'ffn2_w_gate', 'ffn2_w_up', 'ffn2_w_down']
TWIN_WEIGHTS = ['ffn1_norm', 'ffn1_w_gate', 'ffn1_w_up', 'ffn1_w_down', 'mix_norm', 'w_in', 'mla_q_a_norm', 'mla_w_q_b', 'mla_kv_a_norm', 'mla_w_kv_b', 'mla_q_norm', 'mla_k_norm', 'swa_q_norm', 'swa_k_norm', 'swa_sinks', 'mla_out_norm', 'swa_out_norm', 'w_o', 'ffn2_norm', 'ffn2_w_gate', 'ffn2_w_up', 'ffn2_w_down']
TWIN_DIFF_INPUT = 'x'
TWIN_INPUTS = ['x', 'ffn1_norm', 'ffn1_w_gate', 'ffn1_w_up', 'ffn1_w_down', 'mix_norm', 'w_in', 'mla_q_a_norm', 'mla_w_q_b', 'mla_kv_a_norm', 'mla_w_kv_b', 'mla_q_norm', 'mla_k_norm', 'swa_q_norm', 'swa_k_norm', 'swa_sinks', 'mla_out_norm', 'swa_out_norm', 'w_o', 'ffn2_norm', 'ffn2_w_gate', 'ffn2_w_up', 'ffn2_w_down', 'loss_target', 'm_ffn1_norm', 'm_ffn1_w_gate', 'm_ffn1_w_up', 'm_ffn1_w_down', 'm_mix_norm', 'm_w_in', 'm_mla_q_a_norm', 'm_mla_w_q_b', 'm_mla_kv_a_norm', 'm_mla_w_kv_b', 'm_mla_q_norm', 'm_mla_k_norm', 'm_swa_q_norm', 'm_swa_k_norm', 'm_swa_sinks', 'm_mla_out_norm', 'm_swa_out_norm', 'm_w_o', 'm_ffn2_norm', 'm_ffn2_w_gate', 'm_ffn2_w_up', 'm_ffn2_w_down', 'v_ffn1_norm', 'v_ffn1_w_gate', 'v_ffn1_w_up', 'v_ffn1_w_down', 'v_mix_norm', 'v_w_in', 'v_mla_q_a_norm', 'v_mla_w_q_b', 'v_mla_kv_a_norm', 'v_mla_w_kv_b', 'v_mla_q_norm', 'v_mla_k_norm', 'v_swa_q_norm', 'v_swa_k_norm', 'v_swa_sinks', 'v_mla_out_norm', 'v_swa_out_norm', 'v_w_o', 'v_ffn2_norm', 'v_ffn2_w_gate', 'v_ffn2_w_up', 'v_ffn2_w_down']
TWIN_OUTPUTS = ['loss', 'grad_x', 'grad_ffn1_norm', 'grad_ffn1_w_gate', 'grad_ffn1_w_up', 'grad_ffn1_w_down', 'grad_mix_norm', 'grad_w_in', 'grad_mla_q_a_norm', 'grad_mla_w_q_b', 'grad_mla_kv_a_norm', 'grad_mla_w_kv_b', 'grad_mla_q_norm', 'grad_mla_k_norm', 'grad_swa_q_norm', 'grad_swa_k_norm', 'grad_swa_sinks', 'grad_mla_out_norm', 'grad_swa_out_norm', 'grad_w_o', 'grad_ffn2_norm', 'grad_ffn2_w_gate', 'grad_ffn2_w_up', 'grad_ffn2_w_down', 'delta_ffn1_norm', 'delta_ffn1_w_gate', 'delta_ffn1_w_up', 'delta_ffn1_w_down', 'delta_mix_norm', 'delta_w_in', 'delta_mla_q_a_norm', 'delta_mla_w_q_b', 'delta_mla_kv_a_norm', 'delta_mla_w_kv_b', 'delta_mla_q_norm', 'delta_mla_k_norm', 'delta_swa_q_norm', 'delta_swa_k_norm', 'delta_swa_sinks', 'delta_mla_out_norm', 'delta_swa_out_norm', 'delta_w_o', 'delta_ffn2_norm', 'delta_ffn2_w_gate', 'delta_ffn2_w_up', 'delta_ffn2_w_down', 'new_m_ffn1_norm', 'new_m_ffn1_w_gate', 'new_m_ffn1_w_up', 'new_m_ffn1_w_down', 'new_m_mix_norm', 'new_m_w_in', 'new_m_mla_q_a_norm', 'new_m_mla_w_q_b', 'new_m_mla_kv_a_norm', 'new_m_mla_w_kv_b', 'new_m_mla_q_norm', 'new_m_mla_k_norm', 'new_m_swa_q_norm', 'new_m_swa_k_norm', 'new_m_swa_sinks', 'new_m_mla_out_norm', 'new_m_swa_out_norm', 'new_m_w_o', 'new_m_ffn2_norm', 'new_m_ffn2_w_gate', 'new_m_ffn2_w_up', 'new_m_ffn2_w_down', 'new_v_ffn1_norm', 'new_v_ffn1_w_gate', 'new_v_ffn1_w_up', 'new_v_ffn1_w_down', 'new_v_mix_norm', 'new_v_w_in', 'new_v_mla_q_a_norm', 'new_v_mla_w_q_b', 'new_v_mla_kv_a_norm', 'new_v_mla_w_kv_b', 'new_v_mla_q_norm', 'new_v_mla_k_norm', 'new_v_swa_q_norm', 'new_v_swa_k_norm', 'new_v_swa_sinks', 'new_v_mla_out_norm', 'new_v_swa_out_norm', 'new_v_w_o', 'new_v_ffn2_norm', 'new_v_ffn2_w_gate', 'new_v_ffn2_w_up', 'new_v_ffn2_w_down']
TWIN_LEAF_KINDS = {'loss': 'loss', 'grad_x': 'grad_x', 'grad_ffn1_norm': 'grad_w', 'grad_ffn1_w_gate': 'grad_w', 'grad_ffn1_w_up': 'grad_w', 'grad_ffn1_w_down': 'grad_w', 'grad_mix_norm': 'grad_w', 'grad_w_in': 'grad_w', 'grad_mla_q_a_norm': 'grad_w', 'grad_mla_w_q_b': 'grad_w', 'grad_mla_kv_a_norm': 'grad_w', 'grad_mla_w_kv_b': 'grad_w', 'grad_mla_q_norm': 'grad_w', 'grad_mla_k_norm': 'grad_w', 'grad_swa_q_norm': 'grad_w', 'grad_swa_k_norm': 'grad_w', 'grad_swa_sinks': 'grad_w', 'grad_mla_out_norm': 'grad_w', 'grad_swa_out_norm': 'grad_w', 'grad_w_o': 'grad_w', 'grad_ffn2_norm': 'grad_w', 'grad_ffn2_w_gate': 'grad_w', 'grad_ffn2_w_up': 'grad_w', 'grad_ffn2_w_down': 'grad_w', 'delta_ffn1_norm': 'delta_w', 'delta_ffn1_w_gate': 'delta_w', 'delta_ffn1_w_up': 'delta_w', 'delta_ffn1_w_down': 'delta_w', 'delta_mix_norm': 'delta_w', 'delta_w_in': 'delta_w', 'delta_mla_q_a_norm': 'delta_w', 'delta_mla_w_q_b': 'delta_w', 'delta_mla_kv_a_norm': 'delta_w', 'delta_mla_w_kv_b': 'delta_w', 'delta_mla_q_norm': 'delta_w', 'delta_mla_k_norm': 'delta_w', 'delta_swa_q_norm': 'delta_w', 'delta_swa_k_norm': 'delta_w', 'delta_swa_sinks': 'delta_w', 'delta_mla_out_norm': 'delta_w', 'delta_swa_out_norm': 'delta_w', 'delta_w_o': 'delta_w', 'delta_ffn2_norm': 'delta_w', 'delta_ffn2_w_gate': 'delta_w', 'delta_ffn2_w_up': 'delta_w', 'delta_ffn2_w_down': 'delta_w', 'new_m_ffn1_norm': 'new_m', 'new_m_ffn1_w_gate': 'new_m', 'new_m_ffn1_w_up': 'new_m', 'new_m_ffn1_w_down': 'new_m', 'new_m_mix_norm': 'new_m', 'new_m_w_in': 'new_m', 'new_m_mla_q_a_norm': 'new_m', 'new_m_mla_w_q_b': 'new_m', 'new_m_mla_kv_a_norm': 'new_m', 'new_m_mla_w_kv_b': 'new_m', 'new_m_mla_q_norm': 'new_m', 'new_m_mla_k_norm': 'new_m', 'new_m_swa_q_norm': 'new_m', 'new_m_swa_k_norm': 'new_m', 'new_m_swa_sinks': 'new_m', 'new_m_mla_out_norm': 'new_m', 'new_m_swa_out_norm': 'new_m', 'new_m_w_o': 'new_m', 'new_m_ffn2_norm': 'new_m', 'new_m_ffn2_w_gate': 'new_m', 'new_m_ffn2_w_up': 'new_m', 'new_m_ffn2_w_down': 'new_m', 'new_v_ffn1_norm': 'new_v', 'new_v_ffn1_w_gate': 'new_v', 'new_v_ffn1_w_up': 'new_v', 'new_v_ffn1_w_down': 'new_v', 'new_v_mix_norm': 'new_v', 'new_v_w_in': 'new_v', 'new_v_mla_q_a_norm': 'new_v', 'new_v_mla_w_q_b': 'new_v', 'new_v_mla_kv_a_norm': 'new_v', 'new_v_mla_w_kv_b': 'new_v', 'new_v_mla_q_norm': 'new_v', 'new_v_mla_k_norm': 'new_v', 'new_v_swa_q_norm': 'new_v', 'new_v_swa_k_norm': 'new_v', 'new_v_swa_sinks': 'new_v', 'new_v_mla_out_norm': 'new_v', 'new_v_swa_out_norm': 'new_v', 'new_v_w_o': 'new_v', 'new_v_ffn2_norm': 'new_v', 'new_v_ffn2_w_gate': 'new_v', 'new_v_ffn2_w_up': 'new_v', 'new_v_ffn2_w_down': 'new_v'}


def _forward(args):
    return _fwd_reference(*[args[k] for k in FWD_PARAMS])


def _output_shape():
    def fwd():
        inp = _fwd_setup_inputs(0)
        return _fwd_reference(*[inp[k] for k in FWD_PARAMS])
    out = _jax.eval_shape(fwd)
    return out.shape, out.dtype

N_MICROBATCH = 1
ADAM_LR = 0.001
ADAM_B1 = 0.9
ADAM_B2 = 0.999
ADAM_EPS = 1e-08
ADAM_WD = 0.01
ADAM_STEP = 10
PER_EXAMPLE_BATCH_AXIS = {'x': 0, 'loss_target': 0}
SHARED_INPUTS = []
_WEIGHT_DTYPES = {'ffn1_norm': _jnp.float32, 'ffn1_w_gate': _jnp.float32, 'ffn1_w_up': _jnp.float32, 'ffn1_w_down': _jnp.float32, 'mix_norm': _jnp.float32, 'w_in': _jnp.float32, 'mla_q_a_norm': _jnp.float32, 'mla_w_q_b': _jnp.float32, 'mla_kv_a_norm': _jnp.float32, 'mla_w_kv_b': _jnp.float32, 'mla_q_norm': _jnp.float32, 'mla_k_norm': _jnp.float32, 'swa_q_norm': _jnp.float32, 'swa_k_norm': _jnp.float32, 'swa_sinks': _jnp.float32, 'mla_out_norm': _jnp.float32, 'swa_out_norm': _jnp.float32, 'w_o': _jnp.float32, 'ffn2_norm': _jnp.float32, 'ffn2_w_gate': _jnp.float32, 'ffn2_w_up': _jnp.float32, 'ffn2_w_down': _jnp.float32}
MOMENT_SCALE = {'ffn1_norm': 1.315598e+01, 'ffn1_w_gate': 1.449911e+00, 'ffn1_w_up': 1.470671e+00, 'ffn1_w_down': 2.523655e+00, 'mix_norm': 8.711135e+00, 'w_in': 8.381857e+00, 'mla_q_a_norm': 3.268422e+00, 'mla_w_q_b': 1.326058e+00, 'mla_kv_a_norm': 2.797324e+01, 'mla_w_kv_b': 9.764707e+00, 'mla_q_norm': 3.334676e+00, 'mla_k_norm': 3.515298e+00, 'swa_q_norm': 3.537688e+00, 'swa_k_norm': 3.129922e+00, 'swa_sinks': 7.240657e-01, 'mla_out_norm': 6.670389e+01, 'swa_out_norm': 6.122633e+01, 'w_o': 1.061246e+01, 'ffn2_norm': 1.243232e+01, 'ffn2_w_gate': 4.182857e-01, 'ffn2_w_up': 5.720443e-01, 'ffn2_w_down': 9.368700e-01}


def _to_microbatches(a, axis):
    t = _jnp.moveaxis(a, axis, 0)
    t = t.reshape((N_MICROBATCH, t.shape[0] // N_MICROBATCH) + t.shape[1:])
    return _jnp.moveaxis(t, 1, axis + 1)


def setup_inputs(seed: int = 0) -> dict:
    inp = _fwd_setup_inputs(seed)
    key = _jax.random.fold_in(_jax.random.key(seed), 7919)
    shape, _ = _output_shape()
    out = dict(inp)
    out["loss_target"] = _jax.random.normal(_jax.random.fold_in(key, 0), shape, _jnp.float32)
    for i, name in enumerate(TWIN_WEIGHTS):
        w = inp[name].astype(_jnp.float32)
        if MOMENT_SCALE is None:
            s = _jnp.sqrt(_jnp.mean(_jnp.square(w)) + 1e-30)
        else:
            s = MOMENT_SCALE[name]
        km, kv = _jax.random.split(_jax.random.fold_in(key, i + 1))
        out[name] = w
        out["m_" + name] = s * _jax.random.normal(km, w.shape, _jnp.float32)
        out["v_" + name] = (s * s) * _jax.random.uniform(kv, w.shape, _jnp.float32, 0.5, 1.5)
    if N_MICROBATCH > 1:
        for name, axis in PER_EXAMPLE_BATCH_AXIS.items():
            out[name] = _to_microbatches(out[name], axis)
    return {'x': out['x'], 'ffn1_norm': out['ffn1_norm'], 'ffn1_w_gate': out['ffn1_w_gate'], 'ffn1_w_up': out['ffn1_w_up'], 'ffn1_w_down': out['ffn1_w_down'], 'mix_norm': out['mix_norm'], 'w_in': out['w_in'], 'mla_q_a_norm': out['mla_q_a_norm'], 'mla_w_q_b': out['mla_w_q_b'], 'mla_kv_a_norm': out['mla_kv_a_norm'], 'mla_w_kv_b': out['mla_w_kv_b'], 'mla_q_norm': out['mla_q_norm'], 'mla_k_norm': out['mla_k_norm'], 'swa_q_norm': out['swa_q_norm'], 'swa_k_norm': out['swa_k_norm'], 'swa_sinks': out['swa_sinks'], 'mla_out_norm': out['mla_out_norm'], 'swa_out_norm': out['swa_out_norm'], 'w_o': out['w_o'], 'ffn2_norm': out['ffn2_norm'], 'ffn2_w_gate': out['ffn2_w_gate'], 'ffn2_w_up': out['ffn2_w_up'], 'ffn2_w_down': out['ffn2_w_down'], 'loss_target': out['loss_target'], 'm_ffn1_norm': out['m_ffn1_norm'], 'm_ffn1_w_gate': out['m_ffn1_w_gate'], 'm_ffn1_w_up': out['m_ffn1_w_up'], 'm_ffn1_w_down': out['m_ffn1_w_down'], 'm_mix_norm': out['m_mix_norm'], 'm_w_in': out['m_w_in'], 'm_mla_q_a_norm': out['m_mla_q_a_norm'], 'm_mla_w_q_b': out['m_mla_w_q_b'], 'm_mla_kv_a_norm': out['m_mla_kv_a_norm'], 'm_mla_w_kv_b': out['m_mla_w_kv_b'], 'm_mla_q_norm': out['m_mla_q_norm'], 'm_mla_k_norm': out['m_mla_k_norm'], 'm_swa_q_norm': out['m_swa_q_norm'], 'm_swa_k_norm': out['m_swa_k_norm'], 'm_swa_sinks': out['m_swa_sinks'], 'm_mla_out_norm': out['m_mla_out_norm'], 'm_swa_out_norm': out['m_swa_out_norm'], 'm_w_o': out['m_w_o'], 'm_ffn2_norm': out['m_ffn2_norm'], 'm_ffn2_w_gate': out['m_ffn2_w_gate'], 'm_ffn2_w_up': out['m_ffn2_w_up'], 'm_ffn2_w_down': out['m_ffn2_w_down'], 'v_ffn1_norm': out['v_ffn1_norm'], 'v_ffn1_w_gate': out['v_ffn1_w_gate'], 'v_ffn1_w_up': out['v_ffn1_w_up'], 'v_ffn1_w_down': out['v_ffn1_w_down'], 'v_mix_norm': out['v_mix_norm'], 'v_w_in': out['v_w_in'], 'v_mla_q_a_norm': out['v_mla_q_a_norm'], 'v_mla_w_q_b': out['v_mla_w_q_b'], 'v_mla_kv_a_norm': out['v_mla_kv_a_norm'], 'v_mla_w_kv_b': out['v_mla_w_kv_b'], 'v_mla_q_norm': out['v_mla_q_norm'], 'v_mla_k_norm': out['v_mla_k_norm'], 'v_swa_q_norm': out['v_swa_q_norm'], 'v_swa_k_norm': out['v_swa_k_norm'], 'v_swa_sinks': out['v_swa_sinks'], 'v_mla_out_norm': out['v_mla_out_norm'], 'v_swa_out_norm': out['v_swa_out_norm'], 'v_w_o': out['v_w_o'], 'v_ffn2_norm': out['v_ffn2_norm'], 'v_ffn2_w_gate': out['v_ffn2_w_gate'], 'v_ffn2_w_up': out['v_ffn2_w_up'], 'v_ffn2_w_down': out['v_ffn2_w_down']}


def _loss(weights, diff, rest, loss_target):
    with _jax.named_scope("forward"):
        args = {**rest, TWIN_DIFF_INPUT: diff, **{k: w.astype(_WEIGHT_DTYPES[k]) for k, w in weights.items()}}
        y = _forward(args)
    with _jax.named_scope("loss_head"):
        err = _jnp.square(y.astype(_jnp.float32) - loss_target)
        return 0.5 * _jnp.sum(_jnp.mean(err, axis=-1)) if err.ndim else 0.5 * err


def _adamw(w, g, m, v):
    m = ADAM_B1 * m + (1.0 - ADAM_B1) * g
    v = ADAM_B2 * v + (1.0 - ADAM_B2) * _jnp.square(g)
    m_hat = m / (1.0 - ADAM_B1 ** ADAM_STEP)
    v_hat = v / (1.0 - ADAM_B2 ** ADAM_STEP)
    delta = -ADAM_LR * (m_hat / (_jnp.sqrt(v_hat) + ADAM_EPS) + ADAM_WD * w)
    return delta, m, v


def reference(x, ffn1_norm, ffn1_w_gate, ffn1_w_up, ffn1_w_down, mix_norm, w_in, mla_q_a_norm, mla_w_q_b, mla_kv_a_norm, mla_w_kv_b, mla_q_norm, mla_k_norm, swa_q_norm, swa_k_norm, swa_sinks, mla_out_norm, swa_out_norm, w_o, ffn2_norm, ffn2_w_gate, ffn2_w_up, ffn2_w_down, loss_target, m_ffn1_norm, m_ffn1_w_gate, m_ffn1_w_up, m_ffn1_w_down, m_mix_norm, m_w_in, m_mla_q_a_norm, m_mla_w_q_b, m_mla_kv_a_norm, m_mla_w_kv_b, m_mla_q_norm, m_mla_k_norm, m_swa_q_norm, m_swa_k_norm, m_swa_sinks, m_mla_out_norm, m_swa_out_norm, m_w_o, m_ffn2_norm, m_ffn2_w_gate, m_ffn2_w_up, m_ffn2_w_down, v_ffn1_norm, v_ffn1_w_gate, v_ffn1_w_up, v_ffn1_w_down, v_mix_norm, v_w_in, v_mla_q_a_norm, v_mla_w_q_b, v_mla_kv_a_norm, v_mla_w_kv_b, v_mla_q_norm, v_mla_k_norm, v_swa_q_norm, v_swa_k_norm, v_swa_sinks, v_mla_out_norm, v_swa_out_norm, v_w_o, v_ffn2_norm, v_ffn2_w_gate, v_ffn2_w_up, v_ffn2_w_down):
    given = dict(x=x, ffn1_norm=ffn1_norm, ffn1_w_gate=ffn1_w_gate, ffn1_w_up=ffn1_w_up, ffn1_w_down=ffn1_w_down, mix_norm=mix_norm, w_in=w_in, mla_q_a_norm=mla_q_a_norm, mla_w_q_b=mla_w_q_b, mla_kv_a_norm=mla_kv_a_norm, mla_w_kv_b=mla_w_kv_b, mla_q_norm=mla_q_norm, mla_k_norm=mla_k_norm, swa_q_norm=swa_q_norm, swa_k_norm=swa_k_norm, swa_sinks=swa_sinks, mla_out_norm=mla_out_norm, swa_out_norm=swa_out_norm, w_o=w_o, ffn2_norm=ffn2_norm, ffn2_w_gate=ffn2_w_gate, ffn2_w_up=ffn2_w_up, ffn2_w_down=ffn2_w_down, loss_target=loss_target, m_ffn1_norm=m_ffn1_norm, m_ffn1_w_gate=m_ffn1_w_gate, m_ffn1_w_up=m_ffn1_w_up, m_ffn1_w_down=m_ffn1_w_down, m_mix_norm=m_mix_norm, m_w_in=m_w_in, m_mla_q_a_norm=m_mla_q_a_norm, m_mla_w_q_b=m_mla_w_q_b, m_mla_kv_a_norm=m_mla_kv_a_norm, m_mla_w_kv_b=m_mla_w_kv_b, m_mla_q_norm=m_mla_q_norm, m_mla_k_norm=m_mla_k_norm, m_swa_q_norm=m_swa_q_norm, m_swa_k_norm=m_swa_k_norm, m_swa_sinks=m_swa_sinks, m_mla_out_norm=m_mla_out_norm, m_swa_out_norm=m_swa_out_norm, m_w_o=m_w_o, m_ffn2_norm=m_ffn2_norm, m_ffn2_w_gate=m_ffn2_w_gate, m_ffn2_w_up=m_ffn2_w_up, m_ffn2_w_down=m_ffn2_w_down, v_ffn1_norm=v_ffn1_norm, v_ffn1_w_gate=v_ffn1_w_gate, v_ffn1_w_up=v_ffn1_w_up, v_ffn1_w_down=v_ffn1_w_down, v_mix_norm=v_mix_norm, v_w_in=v_w_in, v_mla_q_a_norm=v_mla_q_a_norm, v_mla_w_q_b=v_mla_w_q_b, v_mla_kv_a_norm=v_mla_kv_a_norm, v_mla_w_kv_b=v_mla_w_kv_b, v_mla_q_norm=v_mla_q_norm, v_mla_k_norm=v_mla_k_norm, v_swa_q_norm=v_swa_q_norm, v_swa_k_norm=v_swa_k_norm, v_swa_sinks=v_swa_sinks, v_mla_out_norm=v_mla_out_norm, v_swa_out_norm=v_swa_out_norm, v_w_o=v_w_o, v_ffn2_norm=v_ffn2_norm, v_ffn2_w_gate=v_ffn2_w_gate, v_ffn2_w_up=v_ffn2_w_up, v_ffn2_w_down=v_ffn2_w_down)
    weights = {n: given[n] for n in TWIN_WEIGHTS}
    shared = {n: given[n] for n in SHARED_INPUTS}
    per_example = {n: given[n] for n in ['x']}
    grad_fn = _jax.value_and_grad(_loss, argnums=(0, 1))

    def one_microbatch(ex, loss_target):
        ex = dict(ex)
        diff = ex.pop(TWIN_DIFF_INPUT)
        return grad_fn(weights, diff, {**shared, **ex}, loss_target)

    if N_MICROBATCH == 1:
        loss, (grad_w, grad_x) = one_microbatch(per_example, given["loss_target"])
    else:
        def body(carry, xs):
            loss_sum, grad_sum = carry
            l_k, (gw_k, gx_k) = one_microbatch(xs[0], xs[1])
            with _jax.named_scope("update"):
                return (loss_sum + l_k, _jax.tree.map(_jnp.add, grad_sum, gw_k)), gx_k

        init = (_jnp.zeros((), _jnp.float32), _jax.tree.map(_jnp.zeros_like, weights))
        (loss, grad_w), grad_x = _jax.lax.scan(body, init, (per_example, given["loss_target"]))
    with _jax.named_scope("update"):
        delta_w, new_m, new_v = {}, {}, {}
        for n in TWIN_WEIGHTS:
            delta_w[n], new_m[n], new_v[n] = _adamw(weights[n], grad_w[n], given["m_" + n], given["v_" + n])
    return (loss, grad_x, *[grad_w[n] for n in TWIN_WEIGHTS], *[delta_w[n] for n in TWIN_WEIGHTS],
            *[new_m[n] for n in TWIN_WEIGHTS], *[new_v[n] for n in TWIN_WEIGHTS])
```

```python
import functools

import jax
import jax.numpy as jnp
from jax import lax
from jax.experimental import pallas as pl
from jax.experimental.pallas import tpu as pltpu

N_DEV = 8
EPS = 1e-6
ROPE_THETA = 10000.0
MLA_HEADS = 4
MLA_Q_RANK = 256
MLA_KV_RANK = 128
MLA_NOPE = 128
MLA_ROPE = 64
MLA_V = 128
MLA_QK = MLA_NOPE + MLA_ROPE
SWA_HEADS = 8
SWA_KV = 2
SWA_GROUP = SWA_HEADS // SWA_KV
SWA_D = 64
SWA_BLOCK = 128
ADAM_LR = 0.001
ADAM_B1 = 0.9
ADAM_B2 = 0.999
ADAM_EPS = 1e-08
ADAM_WD = 0.01
ADAM_STEP = 10

LANES = 128
VMEM_LIMIT = 56 * 1024 * 1024

_MXU = jnp.bfloat16
_TB = 256
_BQ = 512
_TK = 512
_RB = 2048

BIG = ("ffn1_w_gate", "ffn1_w_up", "ffn1_w_down", "w_in", "mla_w_q_b", "mla_w_kv_b", "w_o",
       "ffn2_w_gate", "ffn2_w_up", "ffn2_w_down")
ROW_SHARDED = ("ffn1_w_down", "w_o", "ffn2_w_down")
SMALL = ("ffn1_norm", "mix_norm", "mla_q_a_norm", "mla_kv_a_norm", "mla_q_norm", "mla_k_norm",
         "swa_q_norm", "swa_k_norm", "swa_sinks", "mla_out_norm", "swa_out_norm", "ffn2_norm")
WEIGHTS = ("ffn1_norm", "ffn1_w_gate", "ffn1_w_up", "ffn1_w_down", "mix_norm", "w_in", "mla_q_a_norm",
           "mla_w_q_b", "mla_kv_a_norm", "mla_w_kv_b", "mla_q_norm", "mla_k_norm", "swa_q_norm",
           "swa_k_norm", "swa_sinks", "mla_out_norm", "swa_out_norm", "w_o", "ffn2_norm",
           "ffn2_w_gate", "ffn2_w_up", "ffn2_w_down")
MESH_AXES = ("x", "y", "c")
MESH = pl.DeviceIdType.MESH


def _f32(t):
    return t.astype(jnp.float32)


def _mx(t):
    return t.astype(_MXU)


def _dot(a, b):
    return jnp.dot(a, b, preferred_element_type=jnp.float32)


def _dot_nt(a, b):
    return lax.dot_general(a, b, (((1,), (1,)), ((), ())), preferred_element_type=jnp.float32)


def _dot_tn(a, b):
    return lax.dot_general(a, b, (((0,), (0,)), ((), ())), preferred_element_type=jnp.float32)


def _rsq(ss, n):
    return lax.rsqrt(ss * (1.0 / n) + EPS)


def _sumsq(t):
    return jnp.sum(t * t, axis=-1, keepdims=True)


def _rowsum(t):
    return jnp.sum(t, axis=-1, keepdims=True)


def _colsum(t):
    return jnp.sum(t, axis=0, keepdims=True)


def _swap_halves(t, p):
    hi = t.astype(jnp.bfloat16)
    r1 = t - _f32(hi)
    mid = r1.astype(jnp.bfloat16)
    lo = (r1 - _f32(mid)).astype(jnp.bfloat16)
    pb = p.astype(jnp.bfloat16)
    return _dot(hi, pb) + _dot(mid, pb) + _dot(lo, pb)


def _rope(t, cos, sin_signed, p):
    return t * cos + _swap_halves(t, p) * sin_signed


def _rope_bwd(d, cos, sin_signed, p):
    return d * cos + _swap_halves(d * sin_signed, p)


def _cparams(semantics):
    return pltpu.CompilerParams(dimension_semantics=semantics, vmem_limit_bytes=VMEM_LIMIT)


def _const(shape):
    nd = len(shape)
    return pl.BlockSpec(shape, lambda *_: (0,) * nd, pipeline_mode=pl.Buffered(1))


def _rows(tb, width):
    return pl.BlockSpec((tb, width), lambda i: (i, 0))


def _heads_rows(h, tb, width):
    return pl.BlockSpec((h, tb, width), lambda i: (0, i, 0))


def _sds(shape, dtype):
    return jax.ShapeDtypeStruct(shape, dtype)


def _position():
    return lax.axis_index("x"), lax.axis_index("y"), lax.axis_index("c")


def _all_gather(xp):
    rows_n = xp.shape[0]

    def body(x_ref, out_ref, send_sems, recv_sems, local_sem):
        x, y, c = _position()
        me, sibling = (x, y, c), (x, y, 1 - c)
        chips = [(1 - x, y), (x, 1 - y), (1 - x, 1 - y)]

        def rows(px, py, pc):
            return out_ref.at[4 * px + 2 * py + pc]

        def copy(k, block, to, src=None):
            return pltpu.make_async_remote_copy(
                src_ref=rows(*block) if src is None else src, dst_ref=rows(*block),
                send_sem=send_sems.at[k], recv_sem=recv_sems.at[k], device_id=to, device_id_type=MESH)

        mine = pltpu.make_async_copy(x_ref, rows(*me), local_sem)
        mine.start()
        first = [copy(0, me, sibling, src=x_ref)]
        first += [copy(1 + j, me, (*chip, c), src=x_ref) for j, chip in enumerate(chips)]
        for cp in first:
            cp.start()
        passed = [copy(4 + j, (*chip, c), sibling) for j, chip in enumerate(chips)]
        for j, chip in enumerate(chips):
            copy(1 + j, (*chip, c), me).wait_recv()
            passed[j].start()
        copy(0, sibling, me).wait_recv()
        for j, chip in enumerate(chips):
            copy(4 + j, (*chip, 1 - c), me).wait_recv()
        for cp in first + passed:
            cp.wait_send()
        mine.wait()

    return pl.pallas_call(
        body, name="ag_weights",
        out_shape=_sds((N_DEV, rows_n, LANES), xp.dtype),
        in_specs=[pl.BlockSpec(memory_space=pl.ANY)],
        out_specs=pl.BlockSpec(memory_space=pl.ANY),
        scratch_shapes=[pltpu.SemaphoreType.DMA((7,)), pltpu.SemaphoreType.DMA((7,)), pltpu.SemaphoreType.DMA],
    )(xp)


def _relations(x, y):
    return [(x, y), (1 - x, y), (x, 1 - y), (1 - x, 1 - y)]


def _rs_sibling_exchange(g):
    rows_n = g.shape[1]

    def body(g_ref, out_ref, send_sems, recv_sems):
        x, y, c = _position()
        copies = []
        for k, (px, py) in enumerate(_relations(x, y)):
            copies.append(pltpu.make_async_remote_copy(
                src_ref=g_ref.at[4 * px + 2 * py + (1 - c)], dst_ref=out_ref.at[k],
                send_sem=send_sems.at[k], recv_sem=recv_sems.at[k], device_id=(x, y, 1 - c), device_id_type=MESH))
        for cp in copies:
            cp.start()
        for cp in copies:
            cp.wait()

    return pl.pallas_call(
        body, name="rs_sibling_exchange",
        out_shape=_sds((4, rows_n, LANES), g.dtype),
        in_specs=[pl.BlockSpec(memory_space=pl.ANY)],
        out_specs=pl.BlockSpec(memory_space=pl.ANY),
        scratch_shapes=[pltpu.SemaphoreType.DMA((4,)), pltpu.SemaphoreType.DMA((4,))],
    )(g)


def _rs_chip_sums(g, sib, dest_idx):
    rows_n = g.shape[1]
    rb = min(_RB, rows_n)

    def body(idx_ref, g_ref, s_ref, own_ref, wire_ref):
        k = pl.program_id(1)
        total = g_ref[0] + s_ref[0]
        wire_ref[0] = total.astype(wire_ref.dtype)

        @pl.when(k == 0)
        def _():
            own_ref[...] = total

    return pl.pallas_call(
        body, name="rs_chip_sums",
        grid_spec=pltpu.PrefetchScalarGridSpec(
            num_scalar_prefetch=1, grid=(rows_n // rb, 4),
            in_specs=[pl.BlockSpec((1, rb, LANES), lambda r, k, idx: (idx[k], r, 0)),
                      pl.BlockSpec((1, rb, LANES), lambda r, k, idx: (k, r, 0))],
            out_specs=[pl.BlockSpec((rb, LANES), lambda r, k, idx: (r, 0)),
                       pl.BlockSpec((1, rb, LANES), lambda r, k, idx: (k, r, 0))]),
        out_shape=[_sds((rows_n, LANES), jnp.float32), _sds((4, rows_n, LANES), _MXU)],
        compiler_params=_cparams(("parallel", "arbitrary")),
    )(dest_idx, g, sib)


def _rs_chip_exchange(wire):
    rows_n = wire.shape[1]

    def body(w_ref, out_ref, send_sems, recv_sems):
        x, y, c = _position()
        copies = []
        for k, (px, py) in enumerate(_relations(x, y)[1:]):
            copies.append(pltpu.make_async_remote_copy(
                src_ref=w_ref.at[k + 1], dst_ref=out_ref.at[k],
                send_sem=send_sems.at[k], recv_sem=recv_sems.at[k], device_id=(px, py, c), device_id_type=MESH))
        for cp in copies:
            cp.start()
        for cp in copies:
            cp.wait()

    return pl.pallas_call(
        body, name="rs_chip_exchange",
        out_shape=_sds((3, rows_n, LANES), wire.dtype),
        in_specs=[pl.BlockSpec(memory_space=pl.ANY)],
        out_specs=pl.BlockSpec(memory_space=pl.ANY),
        scratch_shapes=[pltpu.SemaphoreType.DMA((3,)), pltpu.SemaphoreType.DMA((3,))],
    )(wire)


def _all_reduce_small(v):
    rows_n = v.shape[0]

    def body(v_ref, out_ref, buf, send_sems, recv_sems):
        x, y, c = _position()
        me = 4 * x + 2 * y + c
        buf[me] = v_ref[...]
        copies = []
        for k in range(1, N_DEV):
            px = 1 - x if k & 4 else x
            py = 1 - y if k & 2 else y
            pc = 1 - c if k & 1 else c
            copies.append(pltpu.make_async_remote_copy(
                src_ref=v_ref, dst_ref=buf.at[me],
                send_sem=send_sems.at[k - 1], recv_sem=recv_sems.at[k - 1], device_id=(px, py, pc), device_id_type=MESH))
        for cp in copies:
            cp.start()
        for cp in copies:
            cp.wait()
        total = buf[0]
        for d in range(1, N_DEV):
            total = total + buf[d]
        out_ref[...] = total

    return pl.pallas_call(
        body, name="ar_small",
        out_shape=_sds((rows_n, LANES), jnp.float32),
        in_specs=[pl.BlockSpec(memory_space=pltpu.VMEM)],
        out_specs=pl.BlockSpec(memory_space=pltpu.VMEM),
        scratch_shapes=[pltpu.VMEM((N_DEV, rows_n, LANES), jnp.float32),
                        pltpu.SemaphoreType.DMA((N_DEV - 1,)), pltpu.SemaphoreType.DMA((N_DEV - 1,))],
    )(v)


def _adamw_math(w, g, m, v):
    m = ADAM_B1 * m + (1.0 - ADAM_B1) * g
    v = ADAM_B2 * v + (1.0 - ADAM_B2) * (g * g)
    m_hat = m / (1.0 - ADAM_B1 ** ADAM_STEP)
    v_hat = v / (1.0 - ADAM_B2 ** ADAM_STEP)
    delta = -ADAM_LR * (m_hat / (jnp.sqrt(v_hat) + ADAM_EPS) + ADAM_WD * w)
    return delta, m, v


def _adamw_reduced(own, recv, w, m, v):
    rows_n = own.shape[0]
    rb = min(_RB, rows_n)

    def body(own_ref, r0, r1, r2, w_ref, m_ref, v_ref, g_out, d_out, m_out, v_out):
        g = ((own_ref[...] + _f32(r0[0])) + _f32(r1[0])) + _f32(r2[0])
        delta, m2, v2 = _adamw_math(w_ref[...], g, m_ref[...], v_ref[...])
        g_out[...] = g
        d_out[...] = delta
        m_out[...] = m2
        v_out[...] = v2

    row = pl.BlockSpec((rb, LANES), lambda r: (r, 0))

    def slot(k):
        return pl.BlockSpec((1, rb, LANES), lambda r: (k, r, 0))

    return pl.pallas_call(
        body, name="adamw_big", grid=(rows_n // rb,),
        in_specs=[row, slot(0), slot(1), slot(2), row, row, row],
        out_specs=[row, row, row, row],
        out_shape=[_sds((rows_n, LANES), jnp.float32)] * 4,
        compiler_params=_cparams(("parallel",)),
    )(own, recv, recv, recv, w, m, v)


def _adamw_small(g, w, m, v):
    def body(g_ref, w_ref, m_ref, v_ref, d_out, m_out, v_out):
        delta, m2, v2 = _adamw_math(w_ref[...], g_ref[...], m_ref[...], v_ref[...])
        d_out[...] = delta
        m_out[...] = m2
        v_out[...] = v2

    vm = pl.BlockSpec(memory_space=pltpu.VMEM)
    return pl.pallas_call(
        body, name="adamw_small",
        in_specs=[vm] * 4, out_specs=[vm] * 3,
        out_shape=[_sds(g.shape, jnp.float32)] * 3,
    )(g, w, m, v)


def _f_chunk(f):
    for cand in (1408, 1024, 512, 256, 128):
        if f % cand == 0:
            return cand
    return f


def _ffn_fwd(x, gain, wg, wu, wd):
    t, d = x.shape
    f = wg.shape[1]
    tb = min(_TB, t)
    fc = _f_chunk(f)

    def body(x_ref, g_ref, wg_ref, wu_ref, wd_ref, xo_ref, a_ref, u_ref):
        xv = x_ref[...]
        hb = _mx(xv * _rsq(_sumsq(xv), d) * g_ref[...])
        y = jnp.zeros((tb, d), jnp.float32)
        for c0 in range(0, f, fc):
            a = _dot(hb, wg_ref[:, c0:c0 + fc])
            u = _dot(hb, wu_ref[:, c0:c0 + fc])
            a_ref[:, c0:c0 + fc] = a.astype(a_ref.dtype)
            u_ref[:, c0:c0 + fc] = u.astype(u_ref.dtype)
            s = a * jax.nn.sigmoid(a) * u
            y = y + _dot(_mx(s), wd_ref[c0:c0 + fc, :])
        xo_ref[...] = xv + 0.5 * y

    return pl.pallas_call(
        body, name="ffn_fwd", grid=(t // tb,),
        in_specs=[_rows(tb, d), _const((1, d)), _const((d, f)), _const((d, f)), _const((f, d))],
        out_specs=[_rows(tb, d), _rows(tb, f), _rows(tb, f)],
        out_shape=[_sds((t, d), jnp.float32), _sds((t, f), _MXU), _sds((t, f), _MXU)],
        compiler_params=_cparams(("parallel",)),
    )(x, gain, wg, wu, wd)


def _ffn_dgrad(x, gain, dxo, a, u, wg, wu, wd):
    t, d = x.shape
    f = wg.shape[1]
    tb = min(_TB, t)
    fc = _f_chunk(f)

    def body(x_ref, g_ref, dxo_ref, a_ref, u_ref, wg_ref, wu_ref, wd_ref,
             dxi_ref, da_ref, du_ref, s_ref, h_ref, dy_ref, dg_ref):
        xv = x_ref[...]
        gv = g_ref[...]
        r = _rsq(_sumsq(xv), d)
        xhat = xv * r
        h_ref[...] = _mx(xhat * gv)
        dxo = dxo_ref[...]
        dyb = _mx(0.5 * dxo)
        dy_ref[...] = dyb
        dh = jnp.zeros((tb, d), jnp.float32)
        for c0 in range(0, f, fc):
            ds = _dot_nt(dyb, wd_ref[c0:c0 + fc, :])
            av = _f32(a_ref[:, c0:c0 + fc])
            uv = _f32(u_ref[:, c0:c0 + fc])
            sig = jax.nn.sigmoid(av)
            silu = av * sig
            da = _mx(ds * uv * (sig * (1.0 + av * (1.0 - sig))))
            du = _mx(ds * silu)
            s_ref[:, c0:c0 + fc] = _mx(silu * uv)
            da_ref[:, c0:c0 + fc] = da
            du_ref[:, c0:c0 + fc] = du
            dh = dh + _dot_nt(da, wg_ref[:, c0:c0 + fc]) + _dot_nt(du, wu_ref[:, c0:c0 + fc])

        @pl.when(pl.program_id(0) == 0)
        def _():
            dg_ref[...] = jnp.zeros_like(dg_ref)

        dg_ref[...] += _colsum(dh * xhat)
        dn = dh * gv
        dxi_ref[...] = dxo + r * (dn - xhat * (_rowsum(dn * xhat) * (1.0 / d)))

    return pl.pallas_call(
        body, name="ffn_dgrad", grid=(t // tb,),
        in_specs=[_rows(tb, d), _const((1, d)), _rows(tb, d), _rows(tb, f), _rows(tb, f),
                  _const((d, f)), _const((d, f)), _const((f, d))],
        out_specs=[_rows(tb, d), _rows(tb, f), _rows(tb, f), _rows(tb, f), _rows(tb, d), _rows(tb, d),
                   pl.BlockSpec((1, d), lambda i: (0, 0))],
        out_shape=[_sds((t, d), jnp.float32), _sds((t, f), _MXU), _sds((t, f), _MXU), _sds((t, f), _MXU),
                   _sds((t, d), _MXU), _sds((t, d), _MXU), _sds((1, d), jnp.float32)],
        compiler_params=_cparams(("arbitrary",)),
    )(x, gain, dxo, a, u, wg, wu, wd)


def _tn_matmul(a, b, name):
    g_n, t, m = a.shape
    n = b.shape[1]
    tk = min(_TK, t)
    tn = n
    while m * tn * 4 > 12 * 1024 * 1024 and tn % 256 == 0:
        tn //= 2
    steps = t // tk

    def body(a_ref, b_ref, o_ref):
        @pl.when(pl.program_id(2) == 0)
        def _():
            o_ref[...] = jnp.zeros_like(o_ref)

        o_ref[0] += _dot_tn(a_ref[0], b_ref[...])

    return pl.pallas_call(
        body, name=name, grid=(g_n, n // tn, steps),
        in_specs=[pl.BlockSpec((1, tk, m), lambda g, j, k: (g, k, 0)),
                  pl.BlockSpec((tk, tn), lambda g, j, k: (k, j))],
        out_specs=pl.BlockSpec((1, m, tn), lambda g, j, k: (g, 0, j)),
        out_shape=_sds((g_n, m, n), jnp.float32),
        compiler_params=_cparams(("parallel", "parallel", "arbitrary")),
    )(a, b)


PREP_WEIGHTS = ("mix_g", "w_cq", "w_ckv", "w_kpe", "w_qs", "w_ks", "w_vs", "g_qa", "wqb_n", "wqb_r", "g_kva", "w_kvb",
                "gq_n", "gq_r", "gk_n", "gk_r", "g_sq", "g_sk", "perm")


def _prep_specs(p):
    return [_const(p[n].shape) for n in PREP_WEIGHTS]


def _prep_fwd(x, cos, sin_s, p):
    t, d = x.shape
    tb = min(_TB, t)

    def body(x_ref, cos_ref, sin_ref, mix_g, w_cq, w_ckv, w_kpe, w_qs, w_ks, w_vs, g_qa, wqb_n, wqb_r, g_kva, w_kvb,
             gq_n, gq_r, gk_n, gk_r, g_sq, g_sk, perm, qa_ref, ka_ref, va_ref, qb_ref, kb_ref, vb_ref):
        xv = x_ref[...]
        cos_v, sin_v, pm = cos_ref[...], sin_ref[...], perm[...]
        hb = _mx(xv * _rsq(_sumsq(xv), d) * mix_g[...])
        cq = _dot(hb, w_cq[...])
        cqn = _mx(cq * _rsq(_sumsq(cq), MLA_Q_RANK) * g_qa[...])
        for h in range(MLA_HEADS):
            qn = _dot(cqn, wqb_n[h])
            qr = _dot(cqn, wqb_r[h])
            rh = _rsq(_sumsq(qn) + _sumsq(qr), MLA_QK)
            qa_ref[h, :, 0:MLA_NOPE] = (qn * rh * gq_n[...]).astype(qa_ref.dtype)
            qa_ref[h, :, MLA_NOPE:MLA_QK] = _rope(qr * rh * gq_r[...], cos_v, sin_v, pm).astype(qa_ref.dtype)
        ckv = _dot(hb, w_ckv[...])
        ckvn = _mx(ckv * _rsq(_sumsq(ckv), MLA_KV_RANK) * g_kva[...])
        kpe = _dot(hb, w_kpe[...])
        ss_pe = _sumsq(kpe)
        for h in range(MLA_HEADS):
            c0 = h * (MLA_NOPE + MLA_V)
            kn = _dot(ckvn, w_kvb[:, c0:c0 + MLA_NOPE])
            vv = _dot(ckvn, w_kvb[:, c0 + MLA_NOPE:c0 + MLA_NOPE + MLA_V])
            rh = _rsq(_sumsq(kn) + ss_pe, MLA_QK)
            ka_ref[h, :, 0:MLA_NOPE] = (kn * rh * gk_n[...]).astype(ka_ref.dtype)
            ka_ref[h, :, MLA_NOPE:MLA_QK] = _rope(kpe * rh * gk_r[...], cos_v, sin_v, pm).astype(ka_ref.dtype)
            va_ref[h] = vv.astype(va_ref.dtype)
        for h in range(SWA_HEADS):
            qs = _dot(hb, w_qs[h])
            qb_ref[h] = _rope(qs * _rsq(_sumsq(qs), SWA_D) * g_sq[...], cos_v, sin_v, pm).astype(qb_ref.dtype)
        for c in range(SWA_KV):
            ks = _dot(hb, w_ks[c])
            kb_ref[c] = _rope(ks * _rsq(_sumsq(ks), SWA_D) * g_sk[...], cos_v, sin_v, pm).astype(kb_ref.dtype)
            vb_ref[c] = _dot(hb, w_vs[c]).astype(vb_ref.dtype)

    return pl.pallas_call(
        body, name="prep_fwd", grid=(t // tb,),
        in_specs=[_rows(tb, d), _rows(tb, MLA_ROPE), _rows(tb, MLA_ROPE)] + _prep_specs(p),
        out_specs=[_heads_rows(MLA_HEADS, tb, MLA_QK), _heads_rows(MLA_HEADS, tb, MLA_QK),
                   _heads_rows(MLA_HEADS, tb, MLA_V), _heads_rows(SWA_HEADS, tb, SWA_D),
                   _heads_rows(SWA_KV, tb, SWA_D), _heads_rows(SWA_KV, tb, SWA_D)],
        out_shape=[_sds((MLA_HEADS, t, MLA_QK), _MXU), _sds((MLA_HEADS, t, MLA_QK), _MXU),
                   _sds((MLA_HEADS, t, MLA_V), _MXU), _sds((SWA_HEADS, t, SWA_D), _MXU),
                   _sds((SWA_KV, t, SWA_D), _MXU), _sds((SWA_KV, t, SWA_D), _MXU)],
        compiler_params=_cparams(("parallel",)),
    )(x, cos, sin_s, *[p[n] for n in PREP_WEIGHTS])


PREP_GRADS = ("mix_g", "w_cq", "w_ckv", "w_kpe", "w_qs", "w_ks", "w_vs", "g_qa", "wqb_n", "wqb_r", "g_kva", "w_kvb",
              "gq_n", "gq_r", "gk_n", "gk_r", "g_sq", "g_sk")


def _prep_bwd(x, dxin, cos, sin_s, p, dqa, dka, dva, dqb, dkb, dvb):
    t, d = x.shape
    tb = min(_TB, t)
    n_w = len(PREP_WEIGHTS)
    n_g = len(PREP_GRADS)

    def norm_bwd(dn_list, xh_list, r, n):
        c = sum(_rowsum(dn * xh) for dn, xh in zip(dn_list, xh_list)) * (1.0 / n)
        return [r * (dn - xh * c) for dn, xh in zip(dn_list, xh_list)]

    def body(*refs):
        x_ref, dxin_ref, cos_ref, sin_ref = refs[:4]
        (mix_g, w_cq, w_ckv, w_kpe, w_qs, w_ks, w_vs, g_qa, wqb_n, wqb_r, g_kva, w_kvb,
         gq_n, gq_r, gk_n, gk_r, g_sq, g_sk, perm) = refs[4:4 + n_w]
        dqa_ref, dka_ref, dva_ref, dqb_ref, dkb_ref, dvb_ref = refs[4 + n_w:10 + n_w]
        dx_ref = refs[10 + n_w]
        grads = dict(zip(PREP_GRADS, refs[11 + n_w:11 + n_w + n_g]))

        @pl.when(pl.program_id(0) == 0)
        def _():
            for ref in grads.values():
                ref[...] = jnp.zeros_like(ref)

        xv = x_ref[...]
        cos_v, sin_v, pm = cos_ref[...], sin_ref[...], perm[...]
        r0 = _rsq(_sumsq(xv), d)
        xhat = xv * r0
        hb = _mx(xhat * mix_g[...])
        dh = jnp.zeros((tb, d), jnp.float32)

        cq = _dot(hb, w_cq[...])
        rq = _rsq(_sumsq(cq), MLA_Q_RANK)
        cqh = cq * rq
        cqn = _mx(cqh * g_qa[...])
        dcqn = jnp.zeros((tb, MLA_Q_RANK), jnp.float32)
        for h in range(MLA_HEADS):
            qn = _dot(cqn, wqb_n[h])
            qr = _dot(cqn, wqb_r[h])
            rh = _rsq(_sumsq(qn) + _sumsq(qr), MLA_QK)
            xh_n, xh_r = qn * rh, qr * rh
            dy_n = dqa_ref[h, :, 0:MLA_NOPE]
            dy_r = _rope_bwd(dqa_ref[h, :, MLA_NOPE:MLA_QK], cos_v, sin_v, pm)
            grads["gq_n"][...] += _colsum(dy_n * xh_n)
            grads["gq_r"][...] += _colsum(dy_r * xh_r)
            dqn, dqr = norm_bwd([dy_n * gq_n[...], dy_r * gq_r[...]], [xh_n, xh_r], rh, MLA_QK)
            dqn, dqr = _mx(dqn), _mx(dqr)
            grads["wqb_n"][h] += _dot_tn(cqn, dqn)
            grads["wqb_r"][h] += _dot_tn(cqn, dqr)
            dcqn = dcqn + _dot_nt(dqn, wqb_n[h]) + _dot_nt(dqr, wqb_r[h])
        grads["g_qa"][...] += _colsum(dcqn * cqh)
        (dcq,) = norm_bwd([dcqn * g_qa[...]], [cqh], rq, MLA_Q_RANK)
        dcq = _mx(dcq)
        grads["w_cq"][...] += _dot_tn(hb, dcq)
        dh = dh + _dot_nt(dcq, w_cq[...])

        ckv = _dot(hb, w_ckv[...])
        rkv = _rsq(_sumsq(ckv), MLA_KV_RANK)
        ckvh = ckv * rkv
        ckvn = _mx(ckvh * g_kva[...])
        kpe = _dot(hb, w_kpe[...])
        ss_pe = _sumsq(kpe)
        dckvn = jnp.zeros((tb, MLA_KV_RANK), jnp.float32)
        dkpe = jnp.zeros((tb, MLA_ROPE), jnp.float32)
        for h in range(MLA_HEADS):
            c0 = h * (MLA_NOPE + MLA_V)
            c1 = c0 + MLA_NOPE
            kn = _dot(ckvn, w_kvb[:, c0:c1])
            rh = _rsq(_sumsq(kn) + ss_pe, MLA_QK)
            xh_n, xh_r = kn * rh, kpe * rh
            dy_n = dka_ref[h, :, 0:MLA_NOPE]
            dy_r = _rope_bwd(dka_ref[h, :, MLA_NOPE:MLA_QK], cos_v, sin_v, pm)
            grads["gk_n"][...] += _colsum(dy_n * xh_n)
            grads["gk_r"][...] += _colsum(dy_r * xh_r)
            dkn, dkr = norm_bwd([dy_n * gk_n[...], dy_r * gk_r[...]], [xh_n, xh_r], rh, MLA_QK)
            dkpe = dkpe + dkr
            dkn = _mx(dkn)
            dvv = _mx(dva_ref[h])
            grads["w_kvb"][:, c0:c1] += _dot_tn(ckvn, dkn)
            grads["w_kvb"][:, c1:c1 + MLA_V] += _dot_tn(ckvn, dvv)
            dckvn = dckvn + _dot_nt(dkn, w_kvb[:, c0:c1]) + _dot_nt(dvv, w_kvb[:, c1:c1 + MLA_V])
        grads["g_kva"][...] += _colsum(dckvn * ckvh)
        (dckv,) = norm_bwd([dckvn * g_kva[...]], [ckvh], rkv, MLA_KV_RANK)
        dckv = _mx(dckv)
        grads["w_ckv"][...] += _dot_tn(hb, dckv)
        dh = dh + _dot_nt(dckv, w_ckv[...])
        dkpe = _mx(dkpe)
        grads["w_kpe"][...] += _dot_tn(hb, dkpe)
        dh = dh + _dot_nt(dkpe, w_kpe[...])

        def swa_head(w_ref, g_ref, dy, gname, wname, idx, dh):
            ws = w_ref[idx]
            s = _dot(hb, ws)
            rh = _rsq(_sumsq(s), SWA_D)
            xh = s * rh
            dpre = _rope_bwd(dy, cos_v, sin_v, pm)
            grads[gname][...] += _colsum(dpre * xh)
            (ds,) = norm_bwd([dpre * g_ref[...]], [xh], rh, SWA_D)
            ds = _mx(ds)
            grads[wname][idx] += _dot_tn(hb, ds)
            return dh + _dot_nt(ds, ws)

        for h in range(SWA_HEADS):
            dh = swa_head(w_qs, g_sq, dqb_ref[h], "g_sq", "w_qs", h, dh)
        for c in range(SWA_KV):
            dh = swa_head(w_ks, g_sk, dkb_ref[c], "g_sk", "w_ks", c, dh)
            dvs = _mx(dvb_ref[c])
            grads["w_vs"][c] += _dot_tn(hb, dvs)
            dh = dh + _dot_nt(dvs, w_vs[c])

        grads["mix_g"][...] += _colsum(dh * xhat)
        (dxv,) = norm_bwd([dh * mix_g[...]], [xhat], r0, d)
        dx_ref[...] = dxin_ref[...] + dxv

    def acc_spec(shape):
        nd = len(shape)
        return pl.BlockSpec(shape, lambda i: (0,) * nd)

    grad_shapes = [p[n].shape for n in PREP_GRADS]
    return pl.pallas_call(
        body, name="prep_bwd", grid=(t // tb,),
        in_specs=[_rows(tb, d), _rows(tb, d), _rows(tb, MLA_ROPE), _rows(tb, MLA_ROPE)] + _prep_specs(p) + [
            _heads_rows(MLA_HEADS, tb, MLA_QK), _heads_rows(MLA_HEADS, tb, MLA_QK), _heads_rows(MLA_HEADS, tb, MLA_V),
            _heads_rows(SWA_HEADS, tb, SWA_D), _heads_rows(SWA_KV, tb, SWA_D), _heads_rows(SWA_KV, tb, SWA_D)],
        out_specs=[_rows(tb, d)] + [acc_spec(s) for s in grad_shapes],
        out_shape=[_sds((t, d), jnp.float32)] + [_sds(s, jnp.float32) for s in grad_shapes],
        compiler_params=_cparams(("arbitrary",)),
    )(x, dxin, cos, sin_s, *[p[n] for n in PREP_WEIGHTS], dqa, dka, dva, dqb, dkb, dvb)


NEG = -1e30


def _mla_fwd(q, k, v):
    hn, t, dq = q.shape
    dv = v.shape[2]
    bq = min(_BQ, t)
    scale = MLA_QK ** -0.5

    def body(q_ref, k_ref, v_ref, o_ref, l_ref):
        i = pl.program_id(1)
        qv = q_ref[0]

        def step(j, carry, masked):
            m, l, acc = carry
            start = pl.multiple_of(j * bq, bq)
            kv = k_ref[0, pl.ds(start, bq), :]
            vv = v_ref[0, pl.ds(start, bq), :]
            s = _dot_nt(qv, kv) * scale
            if masked:
                row = lax.broadcasted_iota(jnp.int32, (bq, bq), 0)
                col = lax.broadcasted_iota(jnp.int32, (bq, bq), 1)
                s = jnp.where(col <= row, s, NEG)
            m_new = jnp.maximum(m, jnp.max(s, axis=-1, keepdims=True))
            alpha = jnp.exp(m - m_new)
            pv = jnp.exp(s - m_new)
            l = alpha * l + _rowsum(pv)
            acc = alpha * acc + _dot(_mx(pv), vv)
            return m_new, l, acc

        init = (jnp.full((bq, 1), NEG, jnp.float32), jnp.zeros((bq, 1), jnp.float32), jnp.zeros((bq, dv), jnp.float32))
        carry = lax.fori_loop(0, i, lambda j, c: step(j, c, False), init)
        m, l, acc = step(i, carry, True)
        o_ref[0] = acc / l
        l_ref[0] = jnp.broadcast_to(m + jnp.log(l), (bq, LANES))

    return pl.pallas_call(
        body, name="mla_fwd", grid=(hn, t // bq),
        in_specs=[pl.BlockSpec((1, bq, dq), lambda h, i: (h, i, 0)),
                  pl.BlockSpec((1, t, dq), lambda h, i: (h, 0, 0)),
                  pl.BlockSpec((1, t, dv), lambda h, i: (h, 0, 0))],
        out_specs=[pl.BlockSpec((1, bq, dv), lambda h, i: (h, i, 0)),
                   pl.BlockSpec((1, bq, LANES), lambda h, i: (h, i, 0))],
        out_shape=[_sds((hn, t, dv), jnp.float32), _sds((hn, t, LANES), jnp.float32)],
        compiler_params=_cparams(("parallel", "arbitrary")),
    )(q, k, v)


def _mla_bwd(q, k, v, do, lse_rows, dsum_rows):
    hn, t, dq_w = q.shape
    dv_w = v.shape[2]
    bq = min(_BQ, t)
    nb = t // bq
    scale = MLA_QK ** -0.5

    def body(q_ref, do_ref, l_ref, d_ref, k_ref, v_ref, dq_ref, dk_ref, dv_ref):
        j = pl.program_id(1)

        @pl.when(j == 0)
        def _():
            dq_ref[...] = jnp.zeros_like(dq_ref)

        kv = k_ref[0]
        vv = v_ref[0]

        def step(i, carry, masked):
            dk, dv = carry
            start = pl.multiple_of(i * bq, bq)
            qv = q_ref[0, pl.ds(start, bq), :]
            dov = do_ref[0, pl.ds(start, bq), :]
            s_t = _dot_nt(kv, qv) * scale
            p_t = jnp.exp(s_t - l_ref[0, i])
            if masked:
                row = lax.broadcasted_iota(jnp.int32, (bq, bq), 0)
                col = lax.broadcasted_iota(jnp.int32, (bq, bq), 1)
                p_t = jnp.where(row <= col, p_t, 0.0)
            dp_t = _dot_nt(vv, dov)
            ds_t = _mx(p_t * (dp_t - d_ref[0, i]) * scale)
            dv = dv + _dot(_mx(p_t), dov)
            dk = dk + _dot(ds_t, qv)
            dq_ref[0, pl.ds(start, bq), :] += _dot_tn(ds_t, kv)
            return dk, dv

        init = (jnp.zeros((bq, dq_w), jnp.float32), jnp.zeros((bq, dv_w), jnp.float32))
        carry = step(j, init, True)
        dk, dv = lax.fori_loop(j + 1, nb, lambda i, c: step(i, c, False), carry)
        dk_ref[0] = dk
        dv_ref[0] = dv

    return pl.pallas_call(
        body, name="mla_bwd", grid=(hn, nb),
        in_specs=[pl.BlockSpec((1, t, dq_w), lambda h, j: (h, 0, 0)),
                  pl.BlockSpec((1, t, dv_w), lambda h, j: (h, 0, 0)),
                  pl.BlockSpec((1, nb, 1, bq), lambda h, j: (h, 0, 0, 0)),
                  pl.BlockSpec((1, nb, 1, bq), lambda h, j: (h, 0, 0, 0)),
                  pl.BlockSpec((1, bq, dq_w), lambda h, j: (h, j, 0)),
                  pl.BlockSpec((1, bq, dv_w), lambda h, j: (h, j, 0))],
        out_specs=[pl.BlockSpec((1, t, dq_w), lambda h, j: (h, 0, 0)),
                   pl.BlockSpec((1, bq, dq_w), lambda h, j: (h, j, 0)),
                   pl.BlockSpec((1, bq, dv_w), lambda h, j: (h, j, 0))],
        out_shape=[_sds((hn, t, dq_w), jnp.float32), _sds((hn, t, dq_w), jnp.float32), _sds((hn, t, dv_w), jnp.float32)],
        compiler_params=_cparams(("parallel", "arbitrary")),
    )(q, do, lse_rows, dsum_rows, k, v)


def _swa_masks(n):
    row = lax.broadcasted_iota(jnp.int32, (SWA_BLOCK, SWA_BLOCK), 0)
    col = lax.broadcasted_iota(jnp.int32, (SWA_BLOCK, SWA_BLOCK), 1)
    return col <= row, jnp.logical_and(col > row, n > 0)


def _swa_fwd(q, k, v, sinks):
    hn, t, dh = q.shape
    nb = t // SWA_BLOCK
    scale = SWA_D ** -0.5

    def body(q_ref, kc_ref, kp_ref, vc_ref, vp_ref, s_ref, o_ref, l_ref):
        n = pl.program_id(1)
        m_cur, m_prev = _swa_masks(n)
        kc, kp, vc, vp = kc_ref[0], kp_ref[0], vc_ref[0], vp_ref[0]
        for g in range(SWA_GROUP):
            qv = q_ref[g]
            sink = s_ref[g][:, 0:1]
            s_c = jnp.where(m_cur, _dot_nt(qv, kc) * scale, NEG)
            s_p = jnp.where(m_prev, _dot_nt(qv, kp) * scale, NEG)
            m = jnp.maximum(jnp.maximum(jnp.max(s_c, axis=-1, keepdims=True), jnp.max(s_p, axis=-1, keepdims=True)), sink)
            e_c = jnp.exp(s_c - m)
            e_p = jnp.exp(s_p - m)
            denom = _rowsum(e_c) + _rowsum(e_p) + jnp.exp(sink - m)
            inv = 1.0 / denom
            o_ref[g] = _dot(_mx(e_c * inv), vc) + _dot(_mx(e_p * inv), vp)
            l_ref[g] = jnp.broadcast_to(m + jnp.log(denom), (SWA_BLOCK, LANES))

    cur = lambda c, n: (c, n, 0)
    prev = lambda c, n: (c, jnp.maximum(n - 1, 0), 0)
    return pl.pallas_call(
        body, name="swa_fwd", grid=(SWA_KV, nb),
        in_specs=[pl.BlockSpec((SWA_GROUP, SWA_BLOCK, dh), cur),
                  pl.BlockSpec((1, SWA_BLOCK, dh), cur), pl.BlockSpec((1, SWA_BLOCK, dh), prev),
                  pl.BlockSpec((1, SWA_BLOCK, dh), cur), pl.BlockSpec((1, SWA_BLOCK, dh), prev),
                  pl.BlockSpec((SWA_GROUP, 1, LANES), lambda c, n: (c, 0, 0))],
        out_specs=[pl.BlockSpec((SWA_GROUP, SWA_BLOCK, dh), cur), pl.BlockSpec((SWA_GROUP, SWA_BLOCK, LANES), cur)],
        out_shape=[_sds((hn, t, dh), jnp.float32), _sds((hn, t, LANES), jnp.float32)],
        compiler_params=_cparams(("parallel", "arbitrary")),
    )(q, k, k, v, v, sinks)


def _swa_bwd(q, k, v, sinks, do, lse, dsum):
    hn, t, dh = q.shape
    nb = t // SWA_BLOCK
    scale = SWA_D ** -0.5

    def body(q_ref, kc_ref, kp_ref, vc_ref, vp_ref, s_ref, do_ref, l_ref, d_ref,
             dq_ref, dk_ref, dv_ref, ds_ref, dk_carry, dv_carry):
        n = pl.program_id(1)

        @pl.when(n == 0)
        def _():
            ds_ref[...] = jnp.zeros_like(ds_ref)

        @pl.when(n < nb)
        def _():
            m_cur, m_prev = _swa_masks(n)
            kc, kp, vc, vp = kc_ref[0], kp_ref[0], vc_ref[0], vp_ref[0]
            dkc = jnp.zeros((SWA_BLOCK, dh), jnp.float32)
            dvc = jnp.zeros((SWA_BLOCK, dh), jnp.float32)
            dkp = jnp.zeros((SWA_BLOCK, dh), jnp.float32)
            dvp = jnp.zeros((SWA_BLOCK, dh), jnp.float32)
            for g in range(SWA_GROUP):
                qv = q_ref[g]
                dov = _mx(do_ref[g])
                lv = l_ref[g][:, 0:1]
                dv_ = d_ref[g][:, 0:1]
                sink = s_ref[g][:, 0:1]
                p_c = jnp.where(m_cur, jnp.exp(_dot_nt(qv, kc) * scale - lv), 0.0)
                p_p = jnp.where(m_prev, jnp.exp(_dot_nt(qv, kp) * scale - lv), 0.0)
                ds_c = _mx(p_c * (_dot_nt(dov, vc) - dv_) * scale)
                ds_p = _mx(p_p * (_dot_nt(dov, vp) - dv_) * scale)
                dq_ref[g] = _dot(ds_c, kc) + _dot(ds_p, kp)
                dkc = dkc + _dot_tn(ds_c, qv)
                dkp = dkp + _dot_tn(ds_p, qv)
                dvc = dvc + _dot_tn(_mx(p_c), dov)
                dvp = dvp + _dot_tn(_mx(p_p), dov)
                ds_ref[g] += -jnp.sum(jnp.exp(sink - lv) * dv_)

            @pl.when(n > 0)
            def _():
                dk_ref[0] = dk_carry[...] + dkp
                dv_ref[0] = dv_carry[...] + dvp

            dk_carry[...] = dkc
            dv_carry[...] = dvc

        @pl.when(n == nb)
        def _():
            dk_ref[0] = dk_carry[...]
            dv_ref[0] = dv_carry[...]

    last = nb - 1
    cur = lambda c, n: (c, jnp.minimum(n, last), 0)
    prev = lambda c, n: (c, jnp.maximum(jnp.minimum(n, last) - 1, 0), 0)
    out_prev = lambda c, n: (c, jnp.maximum(n - 1, 0), 0)
    qblk = pl.BlockSpec((SWA_GROUP, SWA_BLOCK, dh), cur)
    stat = pl.BlockSpec((SWA_GROUP, SWA_BLOCK, LANES), cur)
    sink_spec = pl.BlockSpec((SWA_GROUP, 1, LANES), lambda c, n: (c, 0, 0))
    return pl.pallas_call(
        body, name="swa_bwd", grid=(SWA_KV, nb + 1),
        in_specs=[qblk, pl.BlockSpec((1, SWA_BLOCK, dh), cur), pl.BlockSpec((1, SWA_BLOCK, dh), prev),
                  pl.BlockSpec((1, SWA_BLOCK, dh), cur), pl.BlockSpec((1, SWA_BLOCK, dh), prev),
                  sink_spec, qblk, stat, stat],
        out_specs=[qblk, pl.BlockSpec((1, SWA_BLOCK, dh), out_prev), pl.BlockSpec((1, SWA_BLOCK, dh), out_prev), sink_spec],
        out_shape=[_sds((hn, t, dh), jnp.float32), _sds((SWA_KV, t, dh), jnp.float32), _sds((SWA_KV, t, dh), jnp.float32),
                   _sds((hn, 1, LANES), jnp.float32)],
        scratch_shapes=[pltpu.VMEM((SWA_BLOCK, dh), jnp.float32), pltpu.VMEM((SWA_BLOCK, dh), jnp.float32)],
        compiler_params=_cparams(("parallel", "arbitrary")),
    )(q, k, k, v, v, sinks, do, lse, dsum)


def _mix_out_fwd(x, oa, ob, ga, gb, wo_a, wo_b):
    t, d = x.shape
    tb = min(_TB, t)
    wa = MLA_HEADS * MLA_V
    wb = SWA_HEADS * SWA_D

    def body(x_ref, oa_ref, ob_ref, ga_ref, gb_ref, woa_ref, wob_ref, xo_ref):
        ra = _rsq(sum(_sumsq(oa_ref[h]) for h in range(MLA_HEADS)), wa)
        rb = _rsq(sum(_sumsq(ob_ref[h]) for h in range(SWA_HEADS)), wb)
        y = jnp.zeros((tb, d), jnp.float32)
        for h in range(MLA_HEADS):
            y = y + _dot(_mx(oa_ref[h] * ra * ga_ref[h]), woa_ref[h])
        for h in range(SWA_HEADS):
            y = y + _dot(_mx(ob_ref[h] * rb * gb_ref[h]), wob_ref[h])
        xo_ref[...] = x_ref[...] + y

    return pl.pallas_call(
        body, name="mix_out_fwd", grid=(t // tb,),
        in_specs=[_rows(tb, d), _heads_rows(MLA_HEADS, tb, MLA_V), _heads_rows(SWA_HEADS, tb, SWA_D),
                  _const(ga.shape), _const(gb.shape), _const(wo_a.shape), _const(wo_b.shape)],
        out_specs=_rows(tb, d),
        out_shape=_sds((t, d), jnp.float32),
        compiler_params=_cparams(("parallel",)),
    )(x, oa, ob, ga, gb, wo_a, wo_b)


def _mix_out_bwd(dx, oa, ob, ga, gb, wo_a, wo_b):
    t, d = dx.shape
    tb = min(_TB, t)
    wa = MLA_HEADS * MLA_V
    wb = SWA_HEADS * SWA_D

    def group(o_ref, g_ref, w_ref, dyb, heads, width, do_ref, dsum_ref, n_ref, dg_ref):
        r = _rsq(sum(_sumsq(o_ref[h]) for h in range(heads)), width)
        xh, dn = [], []
        for h in range(heads):
            xh.append(o_ref[h] * r)
            n_ref[h] = _mx(xh[h] * g_ref[h])
            dm = _dot_nt(dyb, w_ref[h])
            dg_ref[h] += _colsum(dm * xh[h])
            dn.append(dm * g_ref[h])
        c = sum(_rowsum(dn[h] * xh[h]) for h in range(heads)) * (1.0 / width)
        for h in range(heads):
            do = r * (dn[h] - xh[h] * c)
            do_ref[h] = do.astype(do_ref.dtype)
            dsum_ref[h] = jnp.broadcast_to(_rowsum(do * o_ref[h]), (tb, LANES))

    def body(dx_ref, oa_ref, ob_ref, ga_ref, gb_ref, woa_ref, wob_ref,
             doa_ref, dsa_ref, dob_ref, dsb_ref, na_ref, nb_ref, dy_ref, dga_ref, dgb_ref):
        @pl.when(pl.program_id(0) == 0)
        def _():
            dga_ref[...] = jnp.zeros_like(dga_ref)
            dgb_ref[...] = jnp.zeros_like(dgb_ref)

        dyb = _mx(dx_ref[...])
        dy_ref[...] = dyb
        group(oa_ref, ga_ref, woa_ref, dyb, MLA_HEADS, wa, doa_ref, dsa_ref, na_ref, dga_ref)
        group(ob_ref, gb_ref, wob_ref, dyb, SWA_HEADS, wb, dob_ref, dsb_ref, nb_ref, dgb_ref)

    ha = _heads_rows(MLA_HEADS, tb, MLA_V)
    hb = _heads_rows(SWA_HEADS, tb, SWA_D)
    hbl = _heads_rows(SWA_HEADS, tb, LANES)
    return pl.pallas_call(
        body, name="mix_out_bwd", grid=(t // tb,),
        in_specs=[_rows(tb, d), ha, hb, _const(ga.shape), _const(gb.shape), _const(wo_a.shape), _const(wo_b.shape)],
        out_specs=[ha, ha, hb, hbl, ha, hb, _rows(tb, d),
                   pl.BlockSpec(ga.shape, lambda i: (0, 0, 0)), pl.BlockSpec(gb.shape, lambda i: (0, 0, 0))],
        out_shape=[_sds((MLA_HEADS, t, MLA_V), _MXU), _sds((MLA_HEADS, t, LANES), jnp.float32),
                   _sds((SWA_HEADS, t, SWA_D), jnp.float32), _sds((SWA_HEADS, t, LANES), jnp.float32),
                   _sds((MLA_HEADS, t, MLA_V), _MXU), _sds((SWA_HEADS, t, SWA_D), _MXU), _sds((t, d), _MXU),
                   _sds(ga.shape, jnp.float32), _sds(gb.shape, jnp.float32)],
        compiler_params=_cparams(("arbitrary",)),
    )(dx, oa, ob, ga, gb, wo_a, wo_b)


def _loss_head(y, target):
    t, d = y.shape
    tb = min(_TB, t)

    def body(y_ref, t_ref, dy_ref, acc_ref):
        @pl.when(pl.program_id(0) == 0)
        def _():
            acc_ref[...] = jnp.zeros_like(acc_ref)

        err = y_ref[...] - t_ref[...]
        dy_ref[...] = err * (1.0 / d)
        acc_ref[...] += jnp.sum(err * err)

    return pl.pallas_call(
        body, name="loss_head", grid=(t // tb,),
        in_specs=[_rows(tb, d), _rows(tb, d)],
        out_specs=[_rows(tb, d), pl.BlockSpec((8, LANES), lambda i: (0, 0))],
        out_shape=[_sds((t, d), jnp.float32), _sds((8, LANES), jnp.float32)],
        compiler_params=_cparams(("arbitrary",)),
    )(y, target)


def _pack_local(arrays):
    return jnp.concatenate([arrays[n].reshape(-1, LANES) for n in BIG], axis=0)


def _unpack_local(packed, like):
    out, off = {}, 0
    for n in BIG:
        rows_n = like[n].size // LANES
        out[n] = packed[off:off + rows_n].reshape(like[n].shape)
        off += rows_n
    return out


def _unpack_gathered(gathered, like):
    depth = like[BIG[0]].shape[0]
    layers = [{} for _ in range(depth)]
    off = 0
    for n in BIG:
        _, a, b = like[n].shape
        rows_n = like[n].size // LANES
        seg = gathered[:, off:off + rows_n].reshape(N_DEV, depth, a, b)
        off += rows_n
        for l in range(depth):
            if n in ROW_SHARDED:
                layers[l][n] = seg[:, l].reshape(N_DEV * a, b)
            else:
                layers[l][n] = jnp.transpose(seg[:, l], (1, 0, 2)).reshape(a, N_DEV * b)
    return layers


def _pack_by_destination(layer_grads):
    depth = len(layer_grads)
    parts = []
    for n in BIG:
        per_layer = []
        for l in range(depth):
            g = layer_grads[l][n]
            if n in ROW_SHARDED:
                per_layer.append(g.reshape(N_DEV, -1, LANES))
            else:
                a, nb = g.shape
                per_layer.append(jnp.transpose(g.reshape(a, N_DEV, nb // N_DEV), (1, 0, 2)).reshape(N_DEV, -1, LANES))
        parts.append(jnp.concatenate(per_layer, axis=1))
    return jnp.concatenate(parts, axis=1)


def _small_rows(n_elems):
    return -(-n_elems // LANES)


def _pack_small(arrays):
    parts = []
    for n in SMALL:
        v = arrays[n]
        depth, width = v.shape
        padded = _small_rows(width) * LANES
        parts.append(jnp.pad(v, ((0, 0), (0, padded - width))).reshape(-1, LANES))
    packed = jnp.concatenate(parts, axis=0)
    return jnp.pad(packed, ((0, (-packed.shape[0]) % 8), (0, 0)))


def _unpack_small(packed, like):
    out, off = {}, 0
    for n in SMALL:
        depth, width = like[n].shape
        rows_n = _small_rows(width)
        seg = packed[off:off + depth * rows_n].reshape(depth, rows_n * LANES)
        out[n] = seg[:, :width]
        off += depth * rows_n
    return out


def _rope_tables(t):
    pos = jnp.arange(t, dtype=jnp.float32)
    inv = 1.0 / (ROPE_THETA ** (jnp.arange(0, MLA_ROPE, 2, dtype=jnp.float32) / MLA_ROPE))
    ang = pos[:, None] * inv[None, :]
    cos, sin = jnp.cos(ang), jnp.sin(ang)
    half = MLA_ROPE // 2
    idx = jnp.arange(MLA_ROPE)
    perm = (idx[None, :] == (idx[:, None] + half) % MLA_ROPE).astype(jnp.float32)
    return jnp.concatenate([cos, cos], axis=1), jnp.concatenate([-sin, sin], axis=1), perm


def _layer_params(full, small, l, perm):
    d = full["w_in"].shape[0]
    w_in = full["w_in"]
    o = [0, MLA_Q_RANK, MLA_Q_RANK + MLA_KV_RANK, MLA_Q_RANK + MLA_KV_RANK + MLA_ROPE]
    o.append(o[3] + SWA_HEADS * SWA_D)
    o.append(o[4] + SWA_KV * SWA_D)
    o.append(o[5] + SWA_KV * SWA_D)
    heads = lambda w, n: jnp.transpose(w.reshape(d, n, SWA_D), (1, 0, 2))
    wqb = jnp.transpose(full["mla_w_q_b"].reshape(MLA_Q_RANK, MLA_HEADS, MLA_QK), (1, 0, 2))
    row = lambda name: small[name][l][None, :]
    prep = {
        "mix_g": row("mix_norm"),
        "w_cq": w_in[:, o[0]:o[1]], "w_ckv": w_in[:, o[1]:o[2]], "w_kpe": w_in[:, o[2]:o[3]],
        "w_qs": heads(w_in[:, o[3]:o[4]], SWA_HEADS), "w_ks": heads(w_in[:, o[4]:o[5]], SWA_KV),
        "w_vs": heads(w_in[:, o[5]:o[6]], SWA_KV),
        "g_qa": row("mla_q_a_norm"), "wqb_n": wqb[:, :, :MLA_NOPE], "wqb_r": wqb[:, :, MLA_NOPE:],
        "g_kva": row("mla_kv_a_norm"), "w_kvb": full["mla_w_kv_b"],
        "gq_n": row("mla_q_norm")[:, :MLA_NOPE], "gq_r": row("mla_q_norm")[:, MLA_NOPE:],
        "gk_n": row("mla_k_norm")[:, :MLA_NOPE], "gk_r": row("mla_k_norm")[:, MLA_NOPE:],
        "g_sq": row("swa_q_norm"), "g_sk": row("swa_k_norm"), "perm": perm,
    }
    return {
        "prep": prep,
        "ffn1": (row("ffn1_norm"), full["ffn1_w_gate"], full["ffn1_w_up"], full["ffn1_w_down"]),
        "ffn2": (row("ffn2_norm"), full["ffn2_w_gate"], full["ffn2_w_up"], full["ffn2_w_down"]),
        "sinks": jnp.broadcast_to(small["swa_sinks"][l][:, None, None], (SWA_HEADS, 1, LANES)),
        "ga": small["mla_out_norm"][l].reshape(MLA_HEADS, 1, MLA_V),
        "gb": small["swa_out_norm"][l].reshape(SWA_HEADS, 1, SWA_D),
        "wo_a": full["w_o"][:MLA_HEADS * MLA_V].reshape(MLA_HEADS, MLA_V, d),
        "wo_b": full["w_o"][MLA_HEADS * MLA_V:].reshape(SWA_HEADS, SWA_D, d),
    }


def _ffn_backward(x_in, dxo, a, u, params, tag):
    gain, wg, wu, wd = params
    dxi, da, du, s, h, dy, dg = _ffn_dgrad(x_in, gain, dxo, a, u, wg, wu, wd)
    dwg = _tn_matmul(h[None], da, "wgrad_" + tag + "_gate")[0]
    dwu = _tn_matmul(h[None], du, "wgrad_" + tag + "_up")[0]
    dwd = _tn_matmul(s[None], dy, "wgrad_" + tag + "_down")[0]
    return dxi, dg[0], dwg, dwu, dwd


def kernel(x, ffn1_norm, ffn1_w_gate, ffn1_w_up, ffn1_w_down, mix_norm, w_in, mla_q_a_norm, mla_w_q_b, mla_kv_a_norm, mla_w_kv_b, mla_q_norm, mla_k_norm, swa_q_norm, swa_k_norm, swa_sinks, mla_out_norm, swa_out_norm, w_o, ffn2_norm, ffn2_w_gate, ffn2_w_up, ffn2_w_down, loss_target, m_ffn1_norm, m_ffn1_w_gate, m_ffn1_w_up, m_ffn1_w_down, m_mix_norm, m_w_in, m_mla_q_a_norm, m_mla_w_q_b, m_mla_kv_a_norm, m_mla_w_kv_b, m_mla_q_norm, m_mla_k_norm, m_swa_q_norm, m_swa_k_norm, m_swa_sinks, m_mla_out_norm, m_swa_out_norm, m_w_o, m_ffn2_norm, m_ffn2_w_gate, m_ffn2_w_up, m_ffn2_w_down, v_ffn1_norm, v_ffn1_w_gate, v_ffn1_w_up, v_ffn1_w_down, v_mix_norm, v_w_in, v_mla_q_a_norm, v_mla_w_q_b, v_mla_kv_a_norm, v_mla_w_kv_b, v_mla_q_norm, v_mla_k_norm, v_swa_q_norm, v_swa_k_norm, v_swa_sinks, v_mla_out_norm, v_swa_out_norm, v_w_o, v_ffn2_norm, v_ffn2_w_gate, v_ffn2_w_up, v_ffn2_w_down):
    local = dict(locals())
    w = {n: local[n] for n in WEIGHTS}
    m = {n: local["m_" + n] for n in WEIGHTS}
    v = {n: local["v_" + n] for n in WEIGHTS}
    depth = ffn1_norm.shape[0]
    t, d = x.shape[-2], x.shape[-1]
    x2d = x.reshape(t, d)
    target = loss_target.reshape(t, d)
    bq = min(_BQ, t)

    gathered = _all_gather(_mx(_pack_local(w)))
    full = _unpack_gathered(gathered, w)
    cos, sin_s, perm = _rope_tables(t)
    params = [_layer_params(full[l], w, l, perm) for l in range(depth)]

    saved = []
    xc = x2d
    for l in range(depth):
        pr = params[l]
        x0 = xc
        x1, a1, u1 = _ffn_fwd(x0, *pr["ffn1"])
        qa, ka, va, qb, kb, vb = _prep_fwd(x1, cos, sin_s, pr["prep"])
        oa, lse_a = _mla_fwd(qa, ka, va)
        ob, lse_b = _swa_fwd(qb, kb, vb, pr["sinks"])
        x2 = _mix_out_fwd(x1, oa, ob, pr["ga"], pr["gb"], pr["wo_a"], pr["wo_b"])
        x3, a2, u2 = _ffn_fwd(x2, *pr["ffn2"])
        saved.append((x0, a1, u1, x1, qa, ka, va, qb, kb, vb, oa, lse_a, ob, lse_b, x2, a2, u2))
        xc = x3

    dx, sq_err = _loss_head(xc, target)
    loss = lax.psum(0.5 / d * sq_err[0, 0], MESH_AXES)

    big_grads = [None] * depth
    small_grads = {n: [None] * depth for n in SMALL}
    for l in reversed(range(depth)):
        pr = params[l]
        x0, a1, u1, x1, qa, ka, va, qb, kb, vb, oa, lse_a, ob, lse_b, x2, a2, u2 = saved[l]
        g = {}
        dx, small_grads["ffn2_norm"][l], g["ffn2_w_gate"], g["ffn2_w_up"], g["ffn2_w_down"] = _ffn_backward(
            x2, dx, a2, u2, pr["ffn2"], "ffn2")

        doa, dsum_a, dob, dsum_b, na, nb_, dyb, dga, dgb = _mix_out_bwd(
            dx, oa, ob, pr["ga"], pr["gb"], pr["wo_a"], pr["wo_b"])
        small_grads["mla_out_norm"][l] = dga.reshape(-1)
        small_grads["swa_out_norm"][l] = dgb.reshape(-1)
        g["w_o"] = jnp.concatenate([_tn_matmul(na, dyb, "wgrad_wo_a").reshape(-1, d),
                                    _tn_matmul(nb_, dyb, "wgrad_wo_b").reshape(-1, d)], axis=0)

        rows_of = lambda s: s[:, :, 0].reshape(MLA_HEADS, t // bq, 1, bq)
        dqa, dka, dva = _mla_bwd(qa, ka, va, doa, rows_of(lse_a), rows_of(dsum_a))
        dqb, dkb, dvb, dsinks = _swa_bwd(qb, kb, vb, pr["sinks"], dob, lse_b, dsum_b)
        small_grads["swa_sinks"][l] = dsinks[:, 0, 0]

        outs = _prep_bwd(x1, dx, cos, sin_s, pr["prep"], dqa, dka, dva, dqb, dkb, dvb)
        dx = outs[0]
        pg = dict(zip(PREP_GRADS, outs[1:]))
        unheads = lambda wg: jnp.transpose(wg, (1, 0, 2)).reshape(d, -1)
        g["w_in"] = jnp.concatenate([pg["w_cq"], pg["w_ckv"], pg["w_kpe"], unheads(pg["w_qs"]), unheads(pg["w_ks"]),
                                     unheads(pg["w_vs"])], axis=1)
        g["mla_w_q_b"] = jnp.transpose(jnp.concatenate([pg["wqb_n"], pg["wqb_r"]], axis=2), (1, 0, 2)).reshape(MLA_Q_RANK, -1)
        g["mla_w_kv_b"] = pg["w_kvb"]
        small_grads["mix_norm"][l] = pg["mix_g"][0]
        small_grads["mla_q_a_norm"][l] = pg["g_qa"][0]
        small_grads["mla_kv_a_norm"][l] = pg["g_kva"][0]
        small_grads["mla_q_norm"][l] = jnp.concatenate([pg["gq_n"][0], pg["gq_r"][0]])
        small_grads["mla_k_norm"][l] = jnp.concatenate([pg["gk_n"][0], pg["gk_r"][0]])
        small_grads["swa_q_norm"][l] = pg["g_sq"][0]
        small_grads["swa_k_norm"][l] = pg["g_sk"][0]

        dx, small_grads["ffn1_norm"][l], g["ffn1_w_gate"], g["ffn1_w_up"], g["ffn1_w_down"] = _ffn_backward(
            x0, dx, a1, u1, pr["ffn1"], "ffn1")
        big_grads[l] = g

    by_dest = _pack_by_destination(big_grads)
    x_i, y_i, c_i = _position()
    dest_idx = jnp.stack([4 * px + 2 * py + c_i for px, py in _relations(x_i, y_i)]).astype(jnp.int32)
    sib = _rs_sibling_exchange(by_dest)
    own, wire = _rs_chip_sums(by_dest, sib, dest_idx)
    recv = _rs_chip_exchange(wire)
    g_p, d_p, m_p, v_p = _adamw_reduced(own, recv, _pack_local(w), _pack_local(m), _pack_local(v))
    grad_big, delta_big, new_m_big, new_v_big = (_unpack_local(a, w) for a in (g_p, d_p, m_p, v_p))

    small_partial = _pack_small({n: jnp.stack(small_grads[n]) for n in SMALL})
    g_s = _all_reduce_small(small_partial)
    d_s, m_s, v_s = _adamw_small(g_s, _pack_small(w), _pack_small(m), _pack_small(v))
    grad_small, delta_small, new_m_small, new_v_small = (_unpack_small(a, w) for a in (g_s, d_s, m_s, v_s))

    def ordered(big, small):
        return [big[n] if n in big else small[n] for n in WEIGHTS]

    return (loss, dx.reshape(x.shape), *ordered(grad_big, grad_small), *ordered(delta_big, delta_small),
            *ordered(new_m_big, new_m_small), *ordered(new_v_big, new_v_small))
```

```python
import jax
import jax.numpy as jnp
from jax import lax
from jax.experimental import pallas as pl
from jax.experimental.pallas import tpu as pltpu

N_DEV = 8
EPS = 1e-6
ROPE_THETA = 10000.0
MLA_HEADS = 4
MLA_Q_RANK = 256
MLA_KV_RANK = 128
MLA_NOPE = 128
MLA_ROPE = 64
MLA_V = 128
MLA_QK = MLA_NOPE + MLA_ROPE
MLA_QK_PAD = 256
SWA_HEADS = 8
SWA_KV = 2
SWA_GROUP = SWA_HEADS // SWA_KV
SWA_D = 64
SWA_BLOCK = 128
ADAM_LR = 0.001
ADAM_B1 = 0.9
ADAM_B2 = 0.999
ADAM_EPS = 1e-08
ADAM_WD = 0.01
ADAM_STEP = 10

LANES = 128
HALF = LANES // 2
SWA_PAIRS = SWA_HEADS // 2
W_IN_COLS = (MLA_Q_RANK + MLA_KV_RANK + MLA_ROPE, SWA_HEADS * SWA_D + 2 * SWA_KV * SWA_D)
VMEM_LIMIT = 56 * 1024 * 1024

_MXU = jnp.bfloat16
_TB = 256
_BQ = 512
_STRIP = 32
_TK = 512
_RB = 2048
_SWA_STEP = 4

BIG = ("ffn1_w_gate", "ffn1_w_up", "ffn1_w_down", "w_in", "mla_w_q_b", "mla_w_kv_b", "w_o",
       "ffn2_w_gate", "ffn2_w_up", "ffn2_w_down")
ROW_SHARDED = ("ffn1_w_down", "w_o", "ffn2_w_down")
SMALL = ("ffn1_norm", "mix_norm", "mla_q_a_norm", "mla_kv_a_norm", "mla_q_norm", "mla_k_norm",
         "swa_q_norm", "swa_k_norm", "swa_sinks", "mla_out_norm", "swa_out_norm", "ffn2_norm")
WEIGHTS = ("ffn1_norm", "ffn1_w_gate", "ffn1_w_up", "ffn1_w_down", "mix_norm", "w_in", "mla_q_a_norm",
           "mla_w_q_b", "mla_kv_a_norm", "mla_w_kv_b", "mla_q_norm", "mla_k_norm", "swa_q_norm",
           "swa_k_norm", "swa_sinks", "mla_out_norm", "swa_out_norm", "w_o", "ffn2_norm",
           "ffn2_w_gate", "ffn2_w_up", "ffn2_w_down")
MESH_AXES = ("x", "y", "c")
MESH = pl.DeviceIdType.MESH
NEG = -1e30


def _f32(t):
    return t.astype(jnp.float32)


def _mx(t):
    return t.astype(_MXU)


def _dot(a, b):
    return jnp.dot(a, b, preferred_element_type=jnp.float32)


def _dot_nt(a, b):
    return lax.dot_general(a, b, (((1,), (1,)), ((), ())), preferred_element_type=jnp.float32)


def _dot_tn(a, b):
    return lax.dot_general(a, b, (((0,), (0,)), ((), ())), preferred_element_type=jnp.float32)


def _rsq(ss, n):
    return lax.rsqrt(ss * (1.0 / n) + EPS)


def _sumsq(t):
    return jnp.sum(t * t, axis=-1, keepdims=True)


def _rowsum(t):
    return jnp.sum(t, axis=-1, keepdims=True)


def _rowmax(t):
    return jnp.max(t, axis=-1, keepdims=True)


def _colsum(t):
    return jnp.sum(t, axis=0, keepdims=True)


def _lane():
    return lax.broadcasted_iota(jnp.int32, (1, LANES), 1)


def _low_half():
    return _lane() < HALF


def _swap32(t):
    return jnp.where((_lane() & 32) == 0, pltpu.roll(t, 96, 1), pltpu.roll(t, 32, 1))


def _rope(t, cos, sin_signed):
    return t * cos + _swap32(t) * sin_signed


def _rope_bwd(d, cos, sin_signed):
    return d * cos + _swap32(d * sin_signed)


def _half_sums(t):
    low = _low_half()
    return jnp.where(low, _rowsum(jnp.where(low, t, 0.0)), _rowsum(jnp.where(low, 0.0, t)))


def _dup_halves(pair):
    low = _low_half()
    swapped = pltpu.roll(pair, HALF, 1)
    return jnp.where(low, pair, swapped), jnp.where(low, swapped, pair)


def _undup_halves(d0, d1):
    return jnp.where(_low_half(), d0 + pltpu.roll(d0, HALF, 1), d1 + pltpu.roll(d1, HALF, 1))


def _pick_halves(a, b):
    return jnp.where(_low_half(), a, b)


def _norm_bwd(dn_list, xh_list, r, n):
    c = sum(_rowsum(dn * xh) for dn, xh in zip(dn_list, xh_list)) * (1.0 / n)
    return [r * (dn - xh * c) for dn, xh in zip(dn_list, xh_list)]


def _cparams(semantics):
    return pltpu.CompilerParams(dimension_semantics=semantics, vmem_limit_bytes=VMEM_LIMIT)


def _const(shape):
    nd = len(shape)
    return pl.BlockSpec(shape, lambda *_: (0,) * nd, pipeline_mode=pl.Buffered(1))


def _acc(shape):
    nd = len(shape)
    return pl.BlockSpec(shape, lambda *_: (0,) * nd)


def _rows(tb, width):
    return pl.BlockSpec((tb, width), lambda i: (i, 0))


def _heads_rows(h, tb, width):
    return pl.BlockSpec((h, tb, width), lambda i: (0, i, 0))


def _sds(shape, dtype):
    return jax.ShapeDtypeStruct(shape, dtype)


def _position():
    return lax.axis_index("x"), lax.axis_index("y"), lax.axis_index("c")


def _all_gather(xp):
    rows_n = xp.shape[0]

    def body(x_ref, out_ref, send_sems, recv_sems, local_sem):
        x, y, c = _position()
        me, sibling = (x, y, c), (x, y, 1 - c)
        chips = [(1 - x, y), (x, 1 - y), (1 - x, 1 - y)]

        def rows(px, py, pc):
            return out_ref.at[4 * px + 2 * py + pc]

        def copy(k, block, to, src=None):
            return pltpu.make_async_remote_copy(
                src_ref=rows(*block) if src is None else src, dst_ref=rows(*block),
                send_sem=send_sems.at[k], recv_sem=recv_sems.at[k], device_id=to, device_id_type=MESH)

        mine = pltpu.make_async_copy(x_ref, rows(*me), local_sem)
        mine.start()
        first = [copy(0, me, sibling, src=x_ref)]
        first += [copy(1 + j, me, (*chip, c), src=x_ref) for j, chip in enumerate(chips)]
        for cp in first:
            cp.start()
        passed = [copy(4 + j, (*chip, c), sibling) for j, chip in enumerate(chips)]
        for j, chip in enumerate(chips):
            copy(1 + j, (*chip, c), me).wait_recv()
            passed[j].start()
        copy(0, sibling, me).wait_recv()
        for j, chip in enumerate(chips):
            copy(4 + j, (*chip, 1 - c), me).wait_recv()
        for cp in first + passed:
            cp.wait_send()
        mine.wait()

    return pl.pallas_call(
        body, name="ag_weights",
        out_shape=_sds((N_DEV, rows_n, LANES), xp.dtype),
        in_specs=[pl.BlockSpec(memory_space=pl.ANY)],
        out_specs=pl.BlockSpec(memory_space=pl.ANY),
        scratch_shapes=[pltpu.SemaphoreType.DMA((7,)), pltpu.SemaphoreType.DMA((7,)), pltpu.SemaphoreType.DMA],
    )(xp)


def _relations(x, y):
    return [(x, y), (1 - x, y), (x, 1 - y), (1 - x, 1 - y)]


def _rs_sibling_exchange(g):
    rows_n = g.shape[1]

    def body(g_ref, out_ref, send_sems, recv_sems):
        x, y, c = _position()
        copies = []
        for k, (px, py) in enumerate(_relations(x, y)):
            copies.append(pltpu.make_async_remote_copy(
                src_ref=g_ref.at[4 * px + 2 * py + (1 - c)], dst_ref=out_ref.at[k],
                send_sem=send_sems.at[k], recv_sem=recv_sems.at[k], device_id=(x, y, 1 - c), device_id_type=MESH))
        for cp in copies:
            cp.start()
        for cp in copies:
            cp.wait()

    return pl.pallas_call(
        body, name="rs_sibling_exchange",
        out_shape=_sds((4, rows_n, LANES), g.dtype),
        in_specs=[pl.BlockSpec(memory_space=pl.ANY)],
        out_specs=pl.BlockSpec(memory_space=pl.ANY),
        scratch_shapes=[pltpu.SemaphoreType.DMA((4,)), pltpu.SemaphoreType.DMA((4,))],
    )(g)


def _rs_chip_sums(g, sib, dest_idx):
    rows_n = g.shape[1]
    rb = min(_RB, rows_n)

    def body(idx_ref, g_ref, s_ref, own_ref, wire_ref):
        k = pl.program_id(1)
        total = g_ref[0] + s_ref[0]
        wire_ref[0] = total.astype(wire_ref.dtype)

        @pl.when(k == 0)
        def _():
            own_ref[...] = total

    return pl.pallas_call(
        body, name="rs_chip_sums",
        grid_spec=pltpu.PrefetchScalarGridSpec(
            num_scalar_prefetch=1, grid=(rows_n // rb, 4),
            in_specs=[pl.BlockSpec((1, rb, LANES), lambda r, k, idx: (idx[k], r, 0)),
                      pl.BlockSpec((1, rb, LANES), lambda r, k, idx: (k, r, 0))],
            out_specs=[pl.BlockSpec((rb, LANES), lambda r, k, idx: (r, 0)),
                       pl.BlockSpec((1, rb, LANES), lambda r, k, idx: (k, r, 0))]),
        out_shape=[_sds((rows_n, LANES), jnp.float32), _sds((4, rows_n, LANES), _MXU)],
        compiler_params=_cparams(("parallel", "arbitrary")),
    )(dest_idx, g, sib)


def _rs_chip_exchange(wire):
    rows_n = wire.shape[1]

    def body(w_ref, out_ref, send_sems, recv_sems):
        x, y, c = _position()
        copies = []
        for k, (px, py) in enumerate(_relations(x, y)[1:]):
            copies.append(pltpu.make_async_remote_copy(
                src_ref=w_ref.at[k + 1], dst_ref=out_ref.at[k],
                send_sem=send_sems.at[k], recv_sem=recv_sems.at[k], device_id=(px, py, c), device_id_type=MESH))
        for cp in copies:
            cp.start()
        for cp in copies:
            cp.wait()

    return pl.pallas_call(
        body, name="rs_chip_exchange",
        out_shape=_sds((3, rows_n, LANES), wire.dtype),
        in_specs=[pl.BlockSpec(memory_space=pl.ANY)],
        out_specs=pl.BlockSpec(memory_space=pl.ANY),
        scratch_shapes=[pltpu.SemaphoreType.DMA((3,)), pltpu.SemaphoreType.DMA((3,))],
    )(wire)


def _all_reduce_small(v):
    rows_n = v.shape[0]

    def body(v_ref, out_ref, buf, send_sems, recv_sems):
        x, y, c = _position()
        me = 4 * x + 2 * y + c
        buf[me] = v_ref[...]
        copies = []
        for k in range(1, N_DEV):
            px = 1 - x if k & 4 else x
            py = 1 - y if k & 2 else y
            pc = 1 - c if k & 1 else c
            copies.append(pltpu.make_async_remote_copy(
                src_ref=v_ref, dst_ref=buf.at[me],
                send_sem=send_sems.at[k - 1], recv_sem=recv_sems.at[k - 1], device_id=(px, py, pc), device_id_type=MESH))
        for cp in copies:
            cp.start()
        for cp in copies:
            cp.wait()
        total = buf[0]
        for d in range(1, N_DEV):
            total = total + buf[d]
        out_ref[...] = total

    return pl.pallas_call(
        body, name="ar_small",
        out_shape=_sds((rows_n, LANES), jnp.float32),
        in_specs=[pl.BlockSpec(memory_space=pltpu.VMEM)],
        out_specs=pl.BlockSpec(memory_space=pltpu.VMEM),
        scratch_shapes=[pltpu.VMEM((N_DEV, rows_n, LANES), jnp.float32),
                        pltpu.SemaphoreType.DMA((N_DEV - 1,)), pltpu.SemaphoreType.DMA((N_DEV - 1,))],
    )(v)


def _adamw_math(w, g, m, v):
    m = ADAM_B1 * m + (1.0 - ADAM_B1) * g
    v = ADAM_B2 * v + (1.0 - ADAM_B2) * (g * g)
    m_hat = m / (1.0 - ADAM_B1 ** ADAM_STEP)
    v_hat = v / (1.0 - ADAM_B2 ** ADAM_STEP)
    delta = -ADAM_LR * (m_hat / (jnp.sqrt(v_hat) + ADAM_EPS) + ADAM_WD * w)
    return delta, m, v


def _adamw_reduced(own, recv, w, m, v):
    rows_n = own.shape[0]
    rb = min(_RB, rows_n)

    def body(own_ref, r0, r1, r2, w_ref, m_ref, v_ref, g_out, d_out, m_out, v_out):
        g = ((own_ref[...] + _f32(r0[0])) + _f32(r1[0])) + _f32(r2[0])
        delta, m2, v2 = _adamw_math(w_ref[...], g, m_ref[...], v_ref[...])
        g_out[...] = g
        d_out[...] = delta
        m_out[...] = m2
        v_out[...] = v2

    row = pl.BlockSpec((rb, LANES), lambda r: (r, 0))

    def slot(k):
        return pl.BlockSpec((1, rb, LANES), lambda r: (k, r, 0))

    return pl.pallas_call(
        body, name="adamw_big", grid=(rows_n // rb,),
        in_specs=[row, slot(0), slot(1), slot(2), row, row, row],
        out_specs=[row, row, row, row],
        out_shape=[_sds((rows_n, LANES), jnp.float32)] * 4,
        compiler_params=_cparams(("parallel",)),
    )(own, recv, recv, recv, w, m, v)


def _adamw_small(g, w, m, v):
    def body(g_ref, w_ref, m_ref, v_ref, d_out, m_out, v_out):
        delta, m2, v2 = _adamw_math(w_ref[...], g_ref[...], m_ref[...], v_ref[...])
        d_out[...] = delta
        m_out[...] = m2
        v_out[...] = v2

    vm = pl.BlockSpec(memory_space=pltpu.VMEM)
    return pl.pallas_call(
        body, name="adamw_small",
        in_specs=[vm] * 4, out_specs=[vm] * 3,
        out_shape=[_sds(g.shape, jnp.float32)] * 3,
    )(g, w, m, v)


def _f_chunk(f):
    for cand in (1408, 1024, 512, 256, 128):
        if f % cand == 0:
            return cand
    return f


def _ffn_fwd(x, gain, wg, wu, wd):
    t, d = x.shape
    f = wg.shape[1]
    tb = min(_TB, t)
    fc = _f_chunk(f)

    def body(x_ref, g_ref, wg_ref, wu_ref, wd_ref, xo_ref, a_ref, u_ref):
        xv = x_ref[...]
        hb = _mx(xv * _rsq(_sumsq(xv), d) * g_ref[...])
        y = jnp.zeros((tb, d), jnp.float32)
        for c0 in range(0, f, fc):
            a = _dot(hb, wg_ref[:, c0:c0 + fc])
            u = _dot(hb, wu_ref[:, c0:c0 + fc])
            a_ref[:, c0:c0 + fc] = a.astype(a_ref.dtype)
            u_ref[:, c0:c0 + fc] = u.astype(u_ref.dtype)
            s = a * jax.nn.sigmoid(a) * u
            y = y + _dot(_mx(s), wd_ref[c0:c0 + fc, :])
        xo_ref[...] = xv + 0.5 * y

    return pl.pallas_call(
        body, name="ffn_fwd", grid=(t // tb,),
        in_specs=[_rows(tb, d), _const((1, d)), _const((d, f)), _const((d, f)), _const((f, d))],
        out_specs=[_rows(tb, d), _rows(tb, f), _rows(tb, f)],
        out_shape=[_sds((t, d), jnp.float32), _sds((t, f), _MXU), _sds((t, f), _MXU)],
        compiler_params=_cparams(("parallel",)),
    )(x, gain, wg, wu, wd)


def _ffn_dgrad(x, gain, dxo, a, u, wg, wu, wd):
    t, d = x.shape
    f = wg.shape[1]
    tb = min(_TB, t)
    fc = _f_chunk(f)

    def body(x_ref, g_ref, dxo_ref, a_ref, u_ref, wg_ref, wu_ref, wd_ref,
             dxi_ref, da_ref, du_ref, s_ref, h_ref, dy_ref, dg_ref):
        xv = x_ref[...]
        gv = g_ref[...]
        r = _rsq(_sumsq(xv), d)
        xhat = xv * r
        h_ref[...] = _mx(xhat * gv)
        dxo = dxo_ref[...]
        dyb = _mx(0.5 * dxo)
        dy_ref[...] = dyb
        dh = jnp.zeros((tb, d), jnp.float32)
        for c0 in range(0, f, fc):
            ds = _dot_nt(dyb, wd_ref[c0:c0 + fc, :])
            av = _f32(a_ref[:, c0:c0 + fc])
            uv = _f32(u_ref[:, c0:c0 + fc])
            sig = jax.nn.sigmoid(av)
            silu = av * sig
            da = _mx(ds * uv * (sig * (1.0 + av * (1.0 - sig))))
            du = _mx(ds * silu)
            s_ref[:, c0:c0 + fc] = _mx(silu * uv)
            da_ref[:, c0:c0 + fc] = da
            du_ref[:, c0:c0 + fc] = du
            dh = dh + _dot_nt(da, wg_ref[:, c0:c0 + fc]) + _dot_nt(du, wu_ref[:, c0:c0 + fc])

        @pl.when(pl.program_id(0) == 0)
        def _():
            dg_ref[...] = jnp.zeros_like(dg_ref)

        dg_ref[...] += _colsum(dh * xhat)
        dn = dh * gv
        dxi_ref[...] = dxo + r * (dn - xhat * (_rowsum(dn * xhat) * (1.0 / d)))

    return pl.pallas_call(
        body, name="ffn_dgrad", grid=(t // tb,),
        in_specs=[_rows(tb, d), _const((1, d)), _rows(tb, d), _rows(tb, f), _rows(tb, f),
                  _const((d, f)), _const((d, f)), _const((f, d))],
        out_specs=[_rows(tb, d), _rows(tb, f), _rows(tb, f), _rows(tb, f), _rows(tb, d), _rows(tb, d),
                   _acc((1, d))],
        out_shape=[_sds((t, d), jnp.float32), _sds((t, f), _MXU), _sds((t, f), _MXU), _sds((t, f), _MXU),
                   _sds((t, d), _MXU), _sds((t, d), _MXU), _sds((1, d), jnp.float32)],
        compiler_params=_cparams(("arbitrary",)),
    )(x, gain, dxo, a, u, wg, wu, wd)


def _tn_matmul(a, b, name):
    g_n, t, m = a.shape
    n = b.shape[1]
    tk = min(_TK, t)
    tn = n
    while m * tn * 4 > 12 * 1024 * 1024 and tn % 256 == 0:
        tn //= 2

    def body(a_ref, b_ref, o_ref):
        @pl.when(pl.program_id(2) == 0)
        def _():
            o_ref[...] = jnp.zeros_like(o_ref)

        o_ref[0] += _dot_tn(a_ref[0], b_ref[...])

    return pl.pallas_call(
        body, name=name, grid=(g_n, n // tn, t // tk),
        in_specs=[pl.BlockSpec((1, tk, m), lambda g, j, k: (g, k, 0)),
                  pl.BlockSpec((tk, tn), lambda g, j, k: (k, j))],
        out_specs=pl.BlockSpec((1, m, tn), lambda g, j, k: (g, 0, j)),
        out_shape=_sds((g_n, m, n), jnp.float32),
        compiler_params=_cparams(("parallel", "parallel", "arbitrary")),
    )(a, b)


PREP_WEIGHTS = ("mix_g", "w_in", "g_qa", "wqb", "g_kva", "w_kvb", "gq_n", "gq_r", "gk_n", "gk_r", "g_sq", "g_sk")
C_CQ, C_CKV, C_KPE, C_QS = 0, MLA_Q_RANK, MLA_Q_RANK + MLA_KV_RANK, MLA_Q_RANK + MLA_KV_RANK + LANES
C_KS = C_QS + SWA_HEADS * SWA_D
C_VS = C_KS + LANES
W_IN_PACKED = C_VS + LANES


def _prep_specs(p):
    return [_const(p[n].shape) for n in PREP_WEIGHTS]


def _pair_norm_rope(t, gain, cos, sin_s):
    return _rope(t * _rsq(_half_sums(t * t), SWA_D) * gain, cos, sin_s)


def _prep_fwd(x, cos, sin_s, p):
    t, d = x.shape
    tb = min(_TB, t)

    def body(x_ref, cos_ref, sin_ref, mix_g, w_in, g_qa, wqb, g_kva, w_kvb, gq_n, gq_r, gk_n, gk_r, g_sq, g_sk,
             qa_ref, ka_ref, va_ref, qb_ref, kb_ref, vb_ref):
        xv = x_ref[...]
        cos_v, sin_v = cos_ref[...], sin_ref[...]
        hb = _mx(xv * _rsq(_sumsq(xv), d) * mix_g[...])
        proj = _dot(hb, w_in[...])
        cq = proj[:, C_CQ:C_CKV]
        cqn = _mx(cq * _rsq(_sumsq(cq), MLA_Q_RANK) * g_qa[...])
        for h in range(MLA_HEADS):
            qh = _dot(cqn, wqb[h])
            qn, qr = qh[:, :MLA_NOPE], qh[:, MLA_NOPE:]
            rh = _rsq(_sumsq(qn) + _sumsq(qr), MLA_QK)
            qa_ref[h, :, 0:MLA_NOPE] = (qn * rh * gq_n[...]).astype(qa_ref.dtype)
            qa_ref[h, :, MLA_NOPE:MLA_QK_PAD] = _rope(qr * rh * gq_r[...], cos_v, sin_v).astype(qa_ref.dtype)
        ckv = proj[:, C_CKV:C_KPE]
        ckvn = _mx(ckv * _rsq(_sumsq(ckv), MLA_KV_RANK) * g_kva[...])
        kpe = proj[:, C_KPE:C_QS]
        ss_pe = _sumsq(kpe)
        kv = _dot(ckvn, w_kvb[...])
        for h in range(MLA_HEADS):
            c0 = h * (MLA_NOPE + MLA_V)
            kn = kv[:, c0:c0 + MLA_NOPE]
            rh = _rsq(_sumsq(kn) + ss_pe, MLA_QK)
            ka_ref[h, :, 0:MLA_NOPE] = (kn * rh * gk_n[...]).astype(ka_ref.dtype)
            ka_ref[h, :, MLA_NOPE:MLA_QK_PAD] = _rope(kpe * rh * gk_r[...], cos_v, sin_v).astype(ka_ref.dtype)
            va_ref[h] = kv[:, c0 + MLA_NOPE:c0 + MLA_NOPE + MLA_V].astype(va_ref.dtype)
        for j in range(SWA_PAIRS):
            c0 = C_QS + j * LANES
            qb_ref[j] = _pair_norm_rope(proj[:, c0:c0 + LANES], g_sq[...], cos_v, sin_v).astype(qb_ref.dtype)
        k0, k1 = _dup_halves(_pair_norm_rope(proj[:, C_KS:C_VS], g_sk[...], cos_v, sin_v))
        kb_ref[0] = k0.astype(kb_ref.dtype)
        kb_ref[1] = k1.astype(kb_ref.dtype)
        v0, v1 = _dup_halves(proj[:, C_VS:W_IN_PACKED])
        vb_ref[0] = v0.astype(vb_ref.dtype)
        vb_ref[1] = v1.astype(vb_ref.dtype)

    return pl.pallas_call(
        body, name="prep_fwd", grid=(t // tb,),
        in_specs=[_rows(tb, d), _rows(tb, LANES), _rows(tb, LANES)] + _prep_specs(p),
        out_specs=[_heads_rows(MLA_HEADS, tb, MLA_QK_PAD), _heads_rows(MLA_HEADS, tb, MLA_QK_PAD),
                   _heads_rows(MLA_HEADS, tb, MLA_V), _heads_rows(SWA_PAIRS, tb, LANES),
                   _heads_rows(SWA_KV, tb, LANES), _heads_rows(SWA_KV, tb, LANES)],
        out_shape=[_sds((MLA_HEADS, t, MLA_QK_PAD), _MXU), _sds((MLA_HEADS, t, MLA_QK_PAD), _MXU),
                   _sds((MLA_HEADS, t, MLA_V), _MXU), _sds((SWA_PAIRS, t, LANES), _MXU),
                   _sds((SWA_KV, t, LANES), _MXU), _sds((SWA_KV, t, LANES), _MXU)],
        compiler_params=_cparams(("parallel",)),
    )(x, cos, sin_s, *[p[n] for n in PREP_WEIGHTS])


def _prep_bwd(x, dxin, cos, sin_s, p, dqa, dka, dva, dqb, dkb, dvb):
    t, d = x.shape
    tb = min(_TB, t)
    n_w = len(PREP_WEIGHTS)

    def body(*refs):
        x_ref, dxin_ref, cos_ref, sin_ref = refs[:4]
        mix_g, w_in, g_qa, wqb, g_kva, w_kvb, gq_n, gq_r, gk_n, gk_r, g_sq, g_sk = refs[4:4 + n_w]
        dqa_ref, dka_ref, dva_ref, dqb_ref, dkb_ref, dvb_ref = refs[4 + n_w:10 + n_w]
        dx_ref = refs[10 + n_w]
        grads = dict(zip(PREP_WEIGHTS, refs[11 + n_w:11 + 2 * n_w]))
        dproj_ref, dkv_ref, dqh_ref = refs[11 + 2 * n_w:]

        @pl.when(pl.program_id(0) == 0)
        def _():
            for ref in grads.values():
                ref[...] = jnp.zeros_like(ref)

        xv = x_ref[...]
        cos_v, sin_v = cos_ref[...], sin_ref[...]
        r0 = _rsq(_sumsq(xv), d)
        xhat = xv * r0
        hb = _mx(xhat * mix_g[...])
        proj = _dot(hb, w_in[...])

        cq = proj[:, C_CQ:C_CKV]
        rq = _rsq(_sumsq(cq), MLA_Q_RANK)
        cqh = cq * rq
        cqn = _mx(cqh * g_qa[...])
        dcqn = jnp.zeros((tb, MLA_Q_RANK), jnp.float32)
        for h in range(MLA_HEADS):
            qh = _dot(cqn, wqb[h])
            qn, qr = qh[:, :MLA_NOPE], qh[:, MLA_NOPE:]
            rh = _rsq(_sumsq(qn) + _sumsq(qr), MLA_QK)
            xh_n, xh_r = qn * rh, qr * rh
            dy_n = dqa_ref[h, :, 0:MLA_NOPE]
            dy_r = _rope_bwd(dqa_ref[h, :, MLA_NOPE:MLA_QK_PAD], cos_v, sin_v)
            grads["gq_n"][...] += _colsum(dy_n * xh_n)
            grads["gq_r"][...] += _colsum(dy_r * xh_r)
            dqn, dqr = _norm_bwd([dy_n * gq_n[...], dy_r * gq_r[...]], [xh_n, xh_r], rh, MLA_QK)
            dqh_ref[:, 0:MLA_NOPE] = _mx(dqn)
            dqh_ref[:, MLA_NOPE:MLA_QK_PAD] = _mx(dqr)
            dqh = dqh_ref[...]
            grads["wqb"][h] += _dot_tn(cqn, dqh)
            dcqn = dcqn + _dot_nt(dqh, wqb[h])
        grads["g_qa"][...] += _colsum(dcqn * cqh)
        (dcq,) = _norm_bwd([dcqn * g_qa[...]], [cqh], rq, MLA_Q_RANK)
        dproj_ref[:, C_CQ:C_CKV] = _mx(dcq)

        ckv = proj[:, C_CKV:C_KPE]
        rkv = _rsq(_sumsq(ckv), MLA_KV_RANK)
        ckvh = ckv * rkv
        ckvn = _mx(ckvh * g_kva[...])
        kpe = proj[:, C_KPE:C_QS]
        ss_pe = _sumsq(kpe)
        kv = _dot(ckvn, w_kvb[...])
        dkpe = jnp.zeros((tb, LANES), jnp.float32)
        for h in range(MLA_HEADS):
            c0 = h * (MLA_NOPE + MLA_V)
            c1 = c0 + MLA_NOPE
            kn = kv[:, c0:c1]
            rh = _rsq(_sumsq(kn) + ss_pe, MLA_QK)
            xh_n, xh_r = kn * rh, kpe * rh
            dy_n = dka_ref[h, :, 0:MLA_NOPE]
            dy_r = _rope_bwd(dka_ref[h, :, MLA_NOPE:MLA_QK_PAD], cos_v, sin_v)
            grads["gk_n"][...] += _colsum(dy_n * xh_n)
            grads["gk_r"][...] += _colsum(dy_r * xh_r)
            dkn, dkr = _norm_bwd([dy_n * gk_n[...], dy_r * gk_r[...]], [xh_n, xh_r], rh, MLA_QK)
            dkpe = dkpe + dkr
            dkv_ref[:, c0:c1] = _mx(dkn)
            dkv_ref[:, c1:c1 + MLA_V] = _mx(dva_ref[h])
        dkv = dkv_ref[...]
        grads["w_kvb"][...] += _dot_tn(ckvn, dkv)
        dckvn = _dot_nt(dkv, w_kvb[...])
        grads["g_kva"][...] += _colsum(dckvn * ckvh)
        (dckv,) = _norm_bwd([dckvn * g_kva[...]], [ckvh], rkv, MLA_KV_RANK)
        dproj_ref[:, C_CKV:C_KPE] = _mx(dckv)
        dproj_ref[:, C_KPE:C_QS] = _mx(dkpe)

        def pair_bwd(tv, dy, g_ref, gname):
            r = _rsq(_half_sums(tv * tv), SWA_D)
            xh = tv * r
            dpre = _rope_bwd(dy, cos_v, sin_v)
            grads[gname][...] += _colsum(dpre * xh)
            dn = dpre * g_ref[...]
            return r * (dn - xh * (_half_sums(dn * xh) * (1.0 / SWA_D)))

        for j in range(SWA_PAIRS):
            c0 = C_QS + j * LANES
            dproj_ref[:, c0:c0 + LANES] = _mx(pair_bwd(proj[:, c0:c0 + LANES], dqb_ref[j], g_sq, "g_sq"))
        dproj_ref[:, C_KS:C_VS] = _mx(pair_bwd(proj[:, C_KS:C_VS], _undup_halves(dkb_ref[0], dkb_ref[1]), g_sk, "g_sk"))
        dproj_ref[:, C_VS:W_IN_PACKED] = _mx(_undup_halves(dvb_ref[0], dvb_ref[1]))

        dproj = dproj_ref[...]
        grads["w_in"][...] += _dot_tn(hb, dproj)
        dh = _dot_nt(dproj, w_in[...])
        grads["mix_g"][...] += _colsum(dh * xhat)
        (dxv,) = _norm_bwd([dh * mix_g[...]], [xhat], r0, d)
        dx_ref[...] = dxin_ref[...] + dxv

    grad_shapes = [p[n].shape for n in PREP_WEIGHTS]
    return pl.pallas_call(
        body, name="prep_bwd", grid=(t // tb,),
        in_specs=[_rows(tb, d), _rows(tb, d), _rows(tb, LANES), _rows(tb, LANES)] + _prep_specs(p) + [
            _heads_rows(MLA_HEADS, tb, MLA_QK_PAD), _heads_rows(MLA_HEADS, tb, MLA_QK_PAD),
            _heads_rows(MLA_HEADS, tb, MLA_V), _heads_rows(SWA_PAIRS, tb, LANES),
            _heads_rows(SWA_KV, tb, LANES), _heads_rows(SWA_KV, tb, LANES)],
        out_specs=[_rows(tb, d)] + [_acc(s) for s in grad_shapes],
        out_shape=[_sds((t, d), jnp.float32)] + [_sds(s, jnp.float32) for s in grad_shapes],
        scratch_shapes=[pltpu.VMEM((tb, W_IN_PACKED), _MXU), pltpu.VMEM((tb, MLA_HEADS * (MLA_NOPE + MLA_V)), _MXU),
                        pltpu.VMEM((tb, MLA_QK_PAD), _MXU)],
        compiler_params=_cparams(("arbitrary",)),
    )(x, dxin, cos, sin_s, *[p[n] for n in PREP_WEIGHTS], dqa, dka, dva, dqb, dkb, dvb)


def _strips(n):
    step = min(_STRIP, n)
    return [slice(r, r + step) for r in range(0, n, step)]


def _mla_fwd(q, k, v):
    hn, t, dq = q.shape
    dv = v.shape[2]
    bq = min(_BQ, t)
    scale = MLA_QK ** -0.5

    def body(q_ref, k_ref, v_ref, o_ref, l_ref):
        i = pl.program_id(1)
        qv = q_ref[0]

        def step(j, carry, masked):
            m, l, acc = carry
            start = pl.multiple_of(j * bq, bq)
            s = _dot_nt(qv, k_ref[0, pl.ds(start, bq), :]) * scale
            if masked:
                row = lax.broadcasted_iota(jnp.int32, (bq, bq), 0)
                col = lax.broadcasted_iota(jnp.int32, (bq, bq), 1)
                s = jnp.where(col <= row, s, NEG)
            m_new = jnp.maximum(m, _rowmax(s))
            alpha = jnp.exp(m - m_new)
            pv = jnp.exp(s - m_new)
            l = alpha * l + _rowsum(pv)
            acc = alpha * acc + _dot(_mx(pv), v_ref[0, pl.ds(start, bq), :])
            return m_new, l, acc

        init = (jnp.full((bq, 1), NEG, jnp.float32), jnp.zeros((bq, 1), jnp.float32), jnp.zeros((bq, dv), jnp.float32))
        carry = lax.fori_loop(0, i, lambda j, c: step(j, c, False), init)
        m, l, acc = step(i, carry, True)
        o_ref[0] = acc / l
        l_ref[0] = jnp.broadcast_to(m + jnp.log(l), (bq, LANES))

    return pl.pallas_call(
        body, name="mla_fwd", grid=(hn, t // bq),
        in_specs=[pl.BlockSpec((1, bq, dq), lambda h, i: (h, i, 0)),
                  pl.BlockSpec((1, t, dq), lambda h, i: (h, 0, 0)),
                  pl.BlockSpec((1, t, dv), lambda h, i: (h, 0, 0))],
        out_specs=[pl.BlockSpec((1, bq, dv), lambda h, i: (h, i, 0)),
                   pl.BlockSpec((1, bq, LANES), lambda h, i: (h, i, 0))],
        out_shape=[_sds((hn, t, dv), jnp.float32), _sds((hn, t, LANES), jnp.float32)],
        compiler_params=_cparams(("parallel", "arbitrary")),
    )(q, k, v)


def _mla_bwd(q, k, v, do, lse_rows, dsum_rows):
    hn, t, dq_w = q.shape
    dv_w = v.shape[2]
    bq = min(_BQ, t)
    nb = t // bq
    scale = MLA_QK ** -0.5

    def body(q_ref, do_ref, l_ref, d_ref, k_ref, v_ref, dq_ref, dk_ref, dv_ref, st_scr, dpt_scr, p_scr, ds_scr):
        j = pl.program_id(1)

        @pl.when(j == 0)
        def _():
            dq_ref[...] = jnp.zeros_like(dq_ref)

        kv = k_ref[0]
        vv = v_ref[0]
        dk_ref[0] = jnp.zeros((bq, dq_w), jnp.float32)
        dv_ref[0] = jnp.zeros((bq, dv_w), jnp.float32)

        def tile(i, masked):
            start = pl.multiple_of(i * bq, bq)
            qv = q_ref[0, pl.ds(start, bq), :]
            dov = do_ref[0, pl.ds(start, bq), :]
            st_scr[...] = _dot_nt(kv, qv)
            dpt_scr[...] = _dot_nt(vv, dov)
            lse = l_ref[0, i]
            dsum = d_ref[0, i]
            for rows in _strips(bq):
                pt = jnp.exp(st_scr[rows, :] * scale - lse)
                if masked:
                    n_rows = rows.stop - rows.start
                    row = lax.broadcasted_iota(jnp.int32, (n_rows, bq), 0) + rows.start
                    col = lax.broadcasted_iota(jnp.int32, (n_rows, bq), 1)
                    pt = jnp.where(row <= col, pt, 0.0)
                p_scr[rows, :] = _mx(pt)
                ds_scr[rows, :] = _mx(pt * (dpt_scr[rows, :] - dsum) * scale)
            ds_t = ds_scr[...]
            dv_ref[0] += _dot(p_scr[...], dov)
            dk_ref[0] += _dot(ds_t, qv)
            dq_ref[0, pl.ds(start, bq), :] += _dot_tn(ds_t, kv)

        def loop_body(i, carry):
            tile(i, False)
            return carry

        tile(j, True)
        lax.fori_loop(j + 1, nb, loop_body, 0)

    return pl.pallas_call(
        body, name="mla_bwd", grid=(hn, nb),
        in_specs=[pl.BlockSpec((1, t, dq_w), lambda h, j: (h, 0, 0)),
                  pl.BlockSpec((1, t, dv_w), lambda h, j: (h, 0, 0)),
                  pl.BlockSpec((1, nb, 1, bq), lambda h, j: (h, 0, 0, 0)),
                  pl.BlockSpec((1, nb, 1, bq), lambda h, j: (h, 0, 0, 0)),
                  pl.BlockSpec((1, bq, dq_w), lambda h, j: (h, j, 0)),
                  pl.BlockSpec((1, bq, dv_w), lambda h, j: (h, j, 0))],
        out_specs=[pl.BlockSpec((1, t, dq_w), lambda h, j: (h, 0, 0)),
                   pl.BlockSpec((1, bq, dq_w), lambda h, j: (h, j, 0)),
                   pl.BlockSpec((1, bq, dv_w), lambda h, j: (h, j, 0))],
        out_shape=[_sds((hn, t, dq_w), jnp.float32), _sds((hn, t, dq_w), jnp.float32), _sds((hn, t, dv_w), jnp.float32)],
        scratch_shapes=[pltpu.VMEM((bq, bq), jnp.float32), pltpu.VMEM((bq, bq), jnp.float32),
                        pltpu.VMEM((bq, bq), _MXU), pltpu.VMEM((bq, bq), _MXU)],
        compiler_params=_cparams(("parallel", "arbitrary")),
    )(q, do, lse_rows, dsum_rows, k, v)


STACK = SWA_GROUP * SWA_BLOCK


def _swa_stack(ref, c, rows):
    low = _low_half()
    parts = []
    for g in range(SWA_GROUP):
        tv = ref[SWA_GROUP // 2 * c + g // 2, rows, :]
        keep = low if g % 2 == 0 else jnp.logical_not(low)
        parts.append(_mx(jnp.where(keep, tv, jnp.zeros_like(tv))))
    return jnp.concatenate(parts, axis=0)


def _swa_cols(ref, c, rows):
    return jnp.concatenate([ref[SWA_GROUP * c + g, rows, 0:1] for g in range(SWA_GROUP)], axis=0)


def _swa_sink_col(s_ref, c):
    return jnp.concatenate([jnp.broadcast_to(s_ref[SWA_GROUP * c + g][:, 0:1], (SWA_BLOCK, 1))
                            for g in range(SWA_GROUP)], axis=0)


def _swa_band_masks():
    row = lax.broadcasted_iota(jnp.int32, (STACK, SWA_BLOCK), 0) & (SWA_BLOCK - 1)
    col = lax.broadcasted_iota(jnp.int32, (STACK, SWA_BLOCK), 1)
    return col <= row, col > row


def _swa_unstack_pairs(ref, c, rows, stacked):
    for pr in range(SWA_GROUP // 2):
        r0 = 2 * pr * SWA_BLOCK
        ref[SWA_GROUP // 2 * c + pr, rows, :] = _pick_halves(stacked[r0:r0 + SWA_BLOCK], stacked[r0 + SWA_BLOCK:r0 + 2 * SWA_BLOCK])


def _swa_blocks(t):
    nblk = t // SWA_BLOCK
    bps = min(_SWA_STEP, nblk)
    return nblk, bps, bps * SWA_BLOCK


def _swa_fwd(q, k, v, sinks):
    _, t, _ = q.shape
    nblk, bps, sb = _swa_blocks(t)
    scale = SWA_D ** -0.5

    def body(q_ref, k_ref, kp_ref, v_ref, vp_ref, s_ref, o_ref, l_ref):
        n = pl.program_id(0)
        m_cur, m_prev = _swa_band_masks()
        for c in range(SWA_KV):
            sink = _swa_sink_col(s_ref, c)
            for b in range(bps):
                rows = slice(b * SWA_BLOCK, (b + 1) * SWA_BLOCK)
                kc, vc = k_ref[c, rows, :], v_ref[c, rows, :]
                if b == 0:
                    kp, vp, mp = kp_ref[c], vp_ref[c], jnp.logical_and(m_prev, n > 0)
                else:
                    before = slice((b - 1) * SWA_BLOCK, b * SWA_BLOCK)
                    kp, vp, mp = k_ref[c, before, :], v_ref[c, before, :], m_prev
                qs = _swa_stack(q_ref, c, rows)
                s_c = jnp.where(m_cur, _dot_nt(qs, kc) * scale, NEG)
                s_p = jnp.where(mp, _dot_nt(qs, kp) * scale, NEG)
                m = jnp.maximum(jnp.maximum(_rowmax(s_c), _rowmax(s_p)), sink)
                e_c = jnp.exp(s_c - m)
                e_p = jnp.exp(s_p - m)
                denom = _rowsum(e_c) + _rowsum(e_p) + jnp.exp(sink - m)
                inv = 1.0 / denom
                o = _dot(_mx(e_c * inv), vc) + _dot(_mx(e_p * inv), vp)
                lse = m + jnp.log(denom)
                for g in range(SWA_GROUP):
                    l_ref[SWA_GROUP * c + g, rows, :] = jnp.broadcast_to(
                        lse[g * SWA_BLOCK:(g + 1) * SWA_BLOCK], (SWA_BLOCK, LANES))
                _swa_unstack_pairs(o_ref, c, rows, o)

    main = lambda n: (0, n, 0)
    prev = lambda n: (0, jnp.maximum(n * bps - 1, 0), 0)
    return pl.pallas_call(
        body, name="swa_fwd", grid=(nblk // bps,),
        in_specs=[pl.BlockSpec((SWA_PAIRS, sb, LANES), main),
                  pl.BlockSpec((SWA_KV, sb, LANES), main), pl.BlockSpec((SWA_KV, SWA_BLOCK, LANES), prev),
                  pl.BlockSpec((SWA_KV, sb, LANES), main), pl.BlockSpec((SWA_KV, SWA_BLOCK, LANES), prev),
                  _const((SWA_HEADS, 1, LANES))],
        out_specs=[pl.BlockSpec((SWA_PAIRS, sb, LANES), main), pl.BlockSpec((SWA_HEADS, sb, LANES), main)],
        out_shape=[_sds((SWA_PAIRS, t, LANES), jnp.float32), _sds((SWA_HEADS, t, LANES), jnp.float32)],
        compiler_params=_cparams(("parallel",)),
    )(q, k, k, v, v, sinks)


def _swa_bwd(q, k, v, sinks, do, lse, dsum):
    _, t, _ = q.shape
    nblk, bps, sb = _swa_blocks(t)
    steps = nblk // bps
    scale = SWA_D ** -0.5

    def body(q_ref, k_ref, kp_ref, v_ref, vp_ref, s_ref, do_ref, l_ref, d_ref, qn_ref, don_ref, ln_ref, dn_ref,
             dq_ref, dk_ref, dv_ref, ds_ref):
        n = pl.program_id(0)

        @pl.when(n == 0)
        def _():
            ds_ref[...] = jnp.zeros_like(ds_ref)

        m_cur, m_prev = _swa_band_masks()
        everything = slice(0, SWA_BLOCK)

        def probs(qs, keys, mask, lcol):
            return jnp.where(mask, jnp.exp(_dot_nt(qs, keys) * scale - lcol), 0.0)

        def dscores(pm, dos, vals, dcol):
            return _mx(pm * (_dot_nt(dos, vals) - dcol) * scale)

        for c in range(SWA_KV):
            sink = _swa_sink_col(s_ref, c)
            dk_acc = [jnp.zeros((SWA_BLOCK, LANES), jnp.float32) for _ in range(bps)]
            dv_acc = [jnp.zeros((SWA_BLOCK, LANES), jnp.float32) for _ in range(bps)]
            for b in range(bps):
                rows = slice(b * SWA_BLOCK, (b + 1) * SWA_BLOCK)
                kc, vc = k_ref[c, rows, :], v_ref[c, rows, :]
                if b == 0:
                    kp, vp, mp = kp_ref[c], vp_ref[c], jnp.logical_and(m_prev, n > 0)
                else:
                    before = slice((b - 1) * SWA_BLOCK, b * SWA_BLOCK)
                    kp, vp, mp = k_ref[c, before, :], v_ref[c, before, :], m_prev
                qs = _swa_stack(q_ref, c, rows)
                dos = _swa_stack(do_ref, c, rows)
                lcol = _swa_cols(l_ref, c, rows)
                dcol = _swa_cols(d_ref, c, rows)
                p_c = probs(qs, kc, m_cur, lcol)
                p_p = probs(qs, kp, mp, lcol)
                ds_c = dscores(p_c, dos, vc, dcol)
                ds_p = dscores(p_p, dos, vp, dcol)
                _swa_unstack_pairs(dq_ref, c, rows, _dot(ds_c, kc) + _dot(ds_p, kp))
                dk_acc[b] = dk_acc[b] + _dot_tn(ds_c, qs)
                dv_acc[b] = dv_acc[b] + _dot_tn(_mx(p_c), dos)
                if b > 0:
                    dk_acc[b - 1] = dk_acc[b - 1] + _dot_tn(ds_p, qs)
                    dv_acc[b - 1] = dv_acc[b - 1] + _dot_tn(_mx(p_p), dos)
                p_sink = jnp.exp(sink - lcol) * dcol
                for g in range(SWA_GROUP):
                    ds_ref[SWA_GROUP * c + g] += -jnp.sum(p_sink[g * SWA_BLOCK:(g + 1) * SWA_BLOCK])
            tail = slice((bps - 1) * SWA_BLOCK, bps * SWA_BLOCK)
            kc, vc = k_ref[c, tail, :], v_ref[c, tail, :]
            qs = _swa_stack(qn_ref, c, everything)
            dos = _swa_stack(don_ref, c, everything)
            lcol = _swa_cols(ln_ref, c, everything)
            dcol = _swa_cols(dn_ref, c, everything)
            p_p = probs(qs, kc, jnp.logical_and(m_prev, n < steps - 1), lcol)
            ds_p = dscores(p_p, dos, vc, dcol)
            dk_acc[bps - 1] = dk_acc[bps - 1] + _dot_tn(ds_p, qs)
            dv_acc[bps - 1] = dv_acc[bps - 1] + _dot_tn(_mx(p_p), dos)
            for b in range(bps):
                rows = slice(b * SWA_BLOCK, (b + 1) * SWA_BLOCK)
                dk_ref[c, rows, :] = dk_acc[b]
                dv_ref[c, rows, :] = dv_acc[b]

    main = lambda n: (0, n, 0)
    prev = lambda n: (0, jnp.maximum(n * bps - 1, 0), 0)
    nxt = lambda n: (0, jnp.minimum((n + 1) * bps, nblk - 1), 0)
    pairs = pl.BlockSpec((SWA_PAIRS, sb, LANES), main)
    kvs = pl.BlockSpec((SWA_KV, sb, LANES), main)
    kv_prev = pl.BlockSpec((SWA_KV, SWA_BLOCK, LANES), prev)
    stats = pl.BlockSpec((SWA_HEADS, sb, LANES), main)
    pairs_next = pl.BlockSpec((SWA_PAIRS, SWA_BLOCK, LANES), nxt)
    stats_next = pl.BlockSpec((SWA_HEADS, SWA_BLOCK, LANES), nxt)
    return pl.pallas_call(
        body, name="swa_bwd", grid=(steps,),
        in_specs=[pairs, kvs, kv_prev, kvs, kv_prev, _const((SWA_HEADS, 1, LANES)), pairs, stats, stats,
                  pairs_next, pairs_next, stats_next, stats_next],
        out_specs=[pairs, kvs, kvs, _acc((SWA_HEADS, 1, LANES))],
        out_shape=[_sds((SWA_PAIRS, t, LANES), jnp.float32), _sds((SWA_KV, t, LANES), jnp.float32),
                   _sds((SWA_KV, t, LANES), jnp.float32), _sds((SWA_HEADS, 1, LANES), jnp.float32)],
        compiler_params=_cparams(("arbitrary",)),
    )(q, k, k, v, v, sinks, do, lse, dsum, q, do, lse, dsum)


MIX_SLABS = 4
MIX_WIDTH = MIX_SLABS * LANES


def _mix_out_fwd(x, oa, ob, ga, gb, wo_a, wo_b):
    t, d = x.shape
    tb = min(_TB, t)

    def body(x_ref, oa_ref, ob_ref, ga_ref, gb_ref, woa_ref, wob_ref, xo_ref):
        y = x_ref[...]
        for o_ref, g_ref, w_ref in ((oa_ref, ga_ref, woa_ref), (ob_ref, gb_ref, wob_ref)):
            r = _rsq(sum(_sumsq(o_ref[h]) for h in range(MIX_SLABS)), MIX_WIDTH)
            for h in range(MIX_SLABS):
                y = y + _dot(_mx(o_ref[h] * r * g_ref[h]), w_ref[h])
        xo_ref[...] = y

    slab = _heads_rows(MIX_SLABS, tb, LANES)
    return pl.pallas_call(
        body, name="mix_out_fwd", grid=(t // tb,),
        in_specs=[_rows(tb, d), slab, slab, _const(ga.shape), _const(gb.shape), _const(wo_a.shape), _const(wo_b.shape)],
        out_specs=_rows(tb, d),
        out_shape=_sds((t, d), jnp.float32),
        compiler_params=_cparams(("parallel",)),
    )(x, oa, ob, ga, gb, wo_a, wo_b)


def _mix_out_bwd(dx, oa, ob, ga, gb, wo_a, wo_b):
    t, d = dx.shape
    tb = min(_TB, t)

    def group(o_ref, g_ref, w_ref, dyb, do_ref, n_ref, dg_ref):
        r = _rsq(sum(_sumsq(o_ref[h]) for h in range(MIX_SLABS)), MIX_WIDTH)
        xh, dn = [], []
        for h in range(MIX_SLABS):
            xh.append(o_ref[h] * r)
            n_ref[h] = _mx(xh[h] * g_ref[h])
            dm = _dot_nt(dyb, w_ref[h])
            dg_ref[h] += _colsum(dm * xh[h])
            dn.append(dm * g_ref[h])
        c = sum(_rowsum(dn[h] * xh[h]) for h in range(MIX_SLABS)) * (1.0 / MIX_WIDTH)
        prods = []
        for h in range(MIX_SLABS):
            do = r * (dn[h] - xh[h] * c)
            do_ref[h] = do.astype(do_ref.dtype)
            prods.append(do * o_ref[h])
        return prods

    def body(dx_ref, oa_ref, ob_ref, ga_ref, gb_ref, woa_ref, wob_ref,
             doa_ref, dsa_ref, dob_ref, dsb_ref, na_ref, nb_ref, dy_ref, dga_ref, dgb_ref):
        @pl.when(pl.program_id(0) == 0)
        def _():
            dga_ref[...] = jnp.zeros_like(dga_ref)
            dgb_ref[...] = jnp.zeros_like(dgb_ref)

        dyb = _mx(dx_ref[...])
        dy_ref[...] = dyb
        for h, pr in enumerate(group(oa_ref, ga_ref, woa_ref, dyb, doa_ref, na_ref, dga_ref)):
            dsa_ref[h] = jnp.broadcast_to(_rowsum(pr), (tb, LANES))
        low = _low_half()
        for j, pr in enumerate(group(ob_ref, gb_ref, wob_ref, dyb, dob_ref, nb_ref, dgb_ref)):
            dsb_ref[2 * j] = jnp.broadcast_to(_rowsum(jnp.where(low, pr, 0.0)), (tb, LANES))
            dsb_ref[2 * j + 1] = jnp.broadcast_to(_rowsum(jnp.where(low, 0.0, pr)), (tb, LANES))

    slab = _heads_rows(MIX_SLABS, tb, LANES)
    return pl.pallas_call(
        body, name="mix_out_bwd", grid=(t // tb,),
        in_specs=[_rows(tb, d), slab, slab, _const(ga.shape), _const(gb.shape), _const(wo_a.shape), _const(wo_b.shape)],
        out_specs=[slab, slab, slab, _heads_rows(SWA_HEADS, tb, LANES), slab, slab, _rows(tb, d),
                   _acc(ga.shape), _acc(gb.shape)],
        out_shape=[_sds((MIX_SLABS, t, LANES), _MXU), _sds((MIX_SLABS, t, LANES), jnp.float32),
                   _sds((MIX_SLABS, t, LANES), jnp.float32), _sds((SWA_HEADS, t, LANES), jnp.float32),
                   _sds((MIX_SLABS, t, LANES), _MXU), _sds((MIX_SLABS, t, LANES), _MXU), _sds((t, d), _MXU),
                   _sds(ga.shape, jnp.float32), _sds(gb.shape, jnp.float32)],
        compiler_params=_cparams(("arbitrary",)),
    )(dx, oa, ob, ga, gb, wo_a, wo_b)


def _loss_head(y, target):
    t, d = y.shape
    tb = min(_TB, t)

    def body(y_ref, t_ref, dy_ref, acc_ref):
        @pl.when(pl.program_id(0) == 0)
        def _():
            acc_ref[...] = jnp.zeros_like(acc_ref)

        err = y_ref[...] - t_ref[...]
        dy_ref[...] = err * (1.0 / d)
        acc_ref[...] += jnp.sum(err * err)

    return pl.pallas_call(
        body, name="loss_head", grid=(t // tb,),
        in_specs=[_rows(tb, d), _rows(tb, d)],
        out_specs=[_rows(tb, d), _acc((8, LANES))],
        out_shape=[_sds((t, d), jnp.float32), _sds((8, LANES), jnp.float32)],
        compiler_params=_cparams(("arbitrary",)),
    )(y, target)


def _pack_local(arrays):
    return jnp.concatenate([arrays[n].reshape(-1, LANES) for n in BIG], axis=0)


def _unpack_local(packed, like):
    out, off = {}, 0
    for n in BIG:
        rows_n = like[n].size // LANES
        out[n] = packed[off:off + rows_n].reshape(like[n].shape)
        off += rows_n
    return out


def _unpack_gathered(gathered, like):
    depth = like[BIG[0]].shape[0]
    layers = [{} for _ in range(depth)]
    off = 0
    for n in BIG:
        _, a, b = like[n].shape
        rows_n = like[n].size // LANES
        seg = gathered[:, off:off + rows_n].reshape(N_DEV, depth, a, b)
        off += rows_n
        for l in range(depth):
            if n in ROW_SHARDED:
                layers[l][n] = seg[:, l].reshape(N_DEV * a, b)
            else:
                layers[l][n] = jnp.transpose(seg[:, l], (1, 0, 2)).reshape(a, N_DEV * b)
    return layers


def _pack_by_destination(layer_grads):
    depth = len(layer_grads)
    parts = []
    for n in BIG:
        per_layer = []
        for l in range(depth):
            g = layer_grads[l][n]
            if n in ROW_SHARDED:
                per_layer.append(g.reshape(N_DEV, -1, LANES))
            else:
                a, nb = g.shape
                per_layer.append(jnp.transpose(g.reshape(a, N_DEV, nb // N_DEV), (1, 0, 2)).reshape(N_DEV, -1, LANES))
        parts.append(jnp.concatenate(per_layer, axis=1))
    return jnp.concatenate(parts, axis=1)


def _small_rows(n_elems):
    return -(-n_elems // LANES)


def _pack_small(arrays):
    parts = []
    for n in SMALL:
        v = arrays[n]
        depth, width = v.shape
        padded = _small_rows(width) * LANES
        parts.append(jnp.pad(v, ((0, 0), (0, padded - width))).reshape(-1, LANES))
    packed = jnp.concatenate(parts, axis=0)
    return jnp.pad(packed, ((0, (-packed.shape[0]) % 8), (0, 0)))


def _unpack_small(packed, like):
    out, off = {}, 0
    for n in SMALL:
        depth, width = like[n].shape
        rows_n = _small_rows(width)
        seg = packed[off:off + depth * rows_n].reshape(depth, rows_n * LANES)
        out[n] = seg[:, :width]
        off += depth * rows_n
    return out


def _rope_tables(t):
    pos = jnp.arange(t, dtype=jnp.float32)
    inv = 1.0 / (ROPE_THETA ** (jnp.arange(0, MLA_ROPE, 2, dtype=jnp.float32) / MLA_ROPE))
    ang = pos[:, None] * inv[None, :]
    cos, sin = jnp.cos(ang), jnp.sin(ang)
    return jnp.concatenate([cos, cos, cos, cos], axis=1), jnp.concatenate([-sin, sin, -sin, sin], axis=1)


def _pad_lanes(a, width):
    return jnp.pad(a, [(0, 0)] * (a.ndim - 1) + [(0, width - a.shape[-1])])


def _layer_params(full, small, l):
    d = full["w_in"].shape[0]
    w_in = full["w_in"]
    mla_cols = W_IN_COLS[0]
    w_in_p = jnp.concatenate([w_in[:, :mla_cols], jnp.zeros((d, LANES - MLA_ROPE), w_in.dtype), w_in[:, mla_cols:]], axis=1)
    wqb = jnp.transpose(full["mla_w_q_b"].reshape(MLA_Q_RANK, MLA_HEADS, MLA_QK), (1, 0, 2))
    row = lambda name: small[name][l][None, :]
    twice = lambda g: jnp.concatenate([g, g], axis=1)
    prep = {
        "mix_g": row("mix_norm"), "w_in": w_in_p,
        "g_qa": row("mla_q_a_norm"), "wqb": _pad_lanes(wqb, MLA_QK_PAD),
        "g_kva": row("mla_kv_a_norm"), "w_kvb": full["mla_w_kv_b"],
        "gq_n": row("mla_q_norm")[:, :MLA_NOPE], "gq_r": _pad_lanes(row("mla_q_norm")[:, MLA_NOPE:], LANES),
        "gk_n": row("mla_k_norm")[:, :MLA_NOPE], "gk_r": _pad_lanes(row("mla_k_norm")[:, MLA_NOPE:], LANES),
        "g_sq": twice(row("swa_q_norm")), "g_sk": twice(row("swa_k_norm")),
    }
    return {
        "prep": prep,
        "ffn1": (row("ffn1_norm"), full["ffn1_w_gate"], full["ffn1_w_up"], full["ffn1_w_down"]),
        "ffn2": (row("ffn2_norm"), full["ffn2_w_gate"], full["ffn2_w_up"], full["ffn2_w_down"]),
        "sinks": jnp.broadcast_to(small["swa_sinks"][l][:, None, None], (SWA_HEADS, 1, LANES)),
        "ga": small["mla_out_norm"][l].reshape(MIX_SLABS, 1, LANES),
        "gb": small["swa_out_norm"][l].reshape(MIX_SLABS, 1, LANES),
        "wo_a": full["w_o"][:MIX_WIDTH].reshape(MIX_SLABS, LANES, d),
        "wo_b": full["w_o"][MIX_WIDTH:].reshape(MIX_SLABS, LANES, d),
    }


def _ffn_backward(x_in, dxo, a, u, params, tag):
    gain, wg, wu, wd = params
    dxi, da, du, s, h, dy, dg = _ffn_dgrad(x_in, gain, dxo, a, u, wg, wu, wd)
    dwg = _tn_matmul(h[None], da, "wgrad_" + tag + "_gate")[0]
    dwu = _tn_matmul(h[None], du, "wgrad_" + tag + "_up")[0]
    dwd = _tn_matmul(s[None], dy, "wgrad_" + tag + "_down")[0]
    return dxi, dg[0], dwg, dwu, dwd


def kernel(x, ffn1_norm, ffn1_w_gate, ffn1_w_up, ffn1_w_down, mix_norm, w_in, mla_q_a_norm, mla_w_q_b, mla_kv_a_norm, mla_w_kv_b, mla_q_norm, mla_k_norm, swa_q_norm, swa_k_norm, swa_sinks, mla_out_norm, swa_out_norm, w_o, ffn2_norm, ffn2_w_gate, ffn2_w_up, ffn2_w_down, loss_target, m_ffn1_norm, m_ffn1_w_gate, m_ffn1_w_up, m_ffn1_w_down, m_mix_norm, m_w_in, m_mla_q_a_norm, m_mla_w_q_b, m_mla_kv_a_norm, m_mla_w_kv_b, m_mla_q_norm, m_mla_k_norm, m_swa_q_norm, m_swa_k_norm, m_swa_sinks, m_mla_out_norm, m_swa_out_norm, m_w_o, m_ffn2_norm, m_ffn2_w_gate, m_ffn2_w_up, m_ffn2_w_down, v_ffn1_norm, v_ffn1_w_gate, v_ffn1_w_up, v_ffn1_w_down, v_mix_norm, v_w_in, v_mla_q_a_norm, v_mla_w_q_b, v_mla_kv_a_norm, v_mla_w_kv_b, v_mla_q_norm, v_mla_k_norm, v_swa_q_norm, v_swa_k_norm, v_swa_sinks, v_mla_out_norm, v_swa_out_norm, v_w_o, v_ffn2_norm, v_ffn2_w_gate, v_ffn2_w_up, v_ffn2_w_down):
    local = dict(locals())
    w = {n: local[n] for n in WEIGHTS}
    m = {n: local["m_" + n] for n in WEIGHTS}
    v = {n: local["v_" + n] for n in WEIGHTS}
    depth = ffn1_norm.shape[0]
    t, d = x.shape[-2], x.shape[-1]
    x2d = x.reshape(t, d)
    target = loss_target.reshape(t, d)
    bq = min(_BQ, t)

    gathered = _all_gather(_mx(_pack_local(w)))
    full = _unpack_gathered(gathered, w)
    cos, sin_s = _rope_tables(t)
    params = [_layer_params(full[l], w, l) for l in range(depth)]

    saved = []
    xc = x2d
    for l in range(depth):
        pr = params[l]
        x0 = xc
        x1, a1, u1 = _ffn_fwd(x0, *pr["ffn1"])
        qa, ka, va, qb, kb, vb = _prep_fwd(x1, cos, sin_s, pr["prep"])
        oa, lse_a = _mla_fwd(qa, ka, va)
        ob, lse_b = _swa_fwd(qb, kb, vb, pr["sinks"])
        x2 = _mix_out_fwd(x1, oa, ob, pr["ga"], pr["gb"], pr["wo_a"], pr["wo_b"])
        x3, a2, u2 = _ffn_fwd(x2, *pr["ffn2"])
        saved.append((x0, a1, u1, x1, qa, ka, va, qb, kb, vb, oa, lse_a, ob, lse_b, x2, a2, u2))
        xc = x3

    dx, sq_err = _loss_head(xc, target)
    loss = lax.psum(0.5 / d * sq_err[0, 0], MESH_AXES)

    big_grads = [None] * depth
    small_grads = {n: [None] * depth for n in SMALL}
    for l in reversed(range(depth)):
        pr = params[l]
        x0, a1, u1, x1, qa, ka, va, qb, kb, vb, oa, lse_a, ob, lse_b, x2, a2, u2 = saved[l]
        g = {}
        dx, small_grads["ffn2_norm"][l], g["ffn2_w_gate"], g["ffn2_w_up"], g["ffn2_w_down"] = _ffn_backward(
            x2, dx, a2, u2, pr["ffn2"], "ffn2")

        doa, dsum_a, dob, dsum_b, na, nb_, dyb, dga, dgb = _mix_out_bwd(
            dx, oa, ob, pr["ga"], pr["gb"], pr["wo_a"], pr["wo_b"])
        small_grads["mla_out_norm"][l] = dga.reshape(-1)
        small_grads["swa_out_norm"][l] = dgb.reshape(-1)
        g["w_o"] = jnp.concatenate([_tn_matmul(na, dyb, "wgrad_wo_a").reshape(-1, d),
                                    _tn_matmul(nb_, dyb, "wgrad_wo_b").reshape(-1, d)], axis=0)

        rows_of = lambda s: s[:, :, 0].reshape(MLA_HEADS, t // bq, 1, bq)
        dqa, dka, dva = _mla_bwd(qa, ka, va, doa, rows_of(lse_a), rows_of(dsum_a))
        dqb, dkb, dvb, dsinks = _swa_bwd(qb, kb, vb, pr["sinks"], dob, lse_b, dsum_b)
        small_grads["swa_sinks"][l] = dsinks[:, 0, 0]

        outs = _prep_bwd(x1, dx, cos, sin_s, pr["prep"], dqa, dka, dva, dqb, dkb, dvb)
        dx = outs[0]
        pg = dict(zip(PREP_WEIGHTS, outs[1:]))
        mla_cols = W_IN_COLS[0]
        g["w_in"] = jnp.concatenate([pg["w_in"][:, :mla_cols], pg["w_in"][:, C_QS:]], axis=1)
        g["mla_w_q_b"] = jnp.transpose(pg["wqb"][:, :, :MLA_QK], (1, 0, 2)).reshape(MLA_Q_RANK, -1)
        g["mla_w_kv_b"] = pg["w_kvb"]
        fold = lambda gg: gg[0, :HALF] + gg[0, HALF:]
        small_grads["mix_norm"][l] = pg["mix_g"][0]
        small_grads["mla_q_a_norm"][l] = pg["g_qa"][0]
        small_grads["mla_kv_a_norm"][l] = pg["g_kva"][0]
        small_grads["mla_q_norm"][l] = jnp.concatenate([pg["gq_n"][0], pg["gq_r"][0, :MLA_ROPE]])
        small_grads["mla_k_norm"][l] = jnp.concatenate([pg["gk_n"][0], pg["gk_r"][0, :MLA_ROPE]])
        small_grads["swa_q_norm"][l] = fold(pg["g_sq"])
        small_grads["swa_k_norm"][l] = fold(pg["g_sk"])

        dx, small_grads["ffn1_norm"][l], g["ffn1_w_gate"], g["ffn1_w_up"], g["ffn1_w_down"] = _ffn_backward(
            x0, dx, a1, u1, pr["ffn1"], "ffn1")
        big_grads[l] = g

    by_dest = _pack_by_destination(big_grads)
    x_i, y_i, c_i = _position()
    dest_idx = jnp.stack([4 * px + 2 * py + c_i for px, py in _relations(x_i, y_i)]).astype(jnp.int32)
    sib = _rs_sibling_exchange(by_dest)
    own, wire = _rs_chip_sums(by_dest, sib, dest_idx)
    recv = _rs_chip_exchange(wire)
    g_p, d_p, m_p, v_p = _adamw_reduced(own, recv, _pack_local(w), _pack_local(m), _pack_local(v))
    grad_big, delta_big, new_m_big, new_v_big = (_unpack_local(a, w) for a in (g_p, d_p, m_p, v_p))

    small_partial = _pack_small({n: jnp.stack(small_grads[n]) for n in SMALL})
    g_s = _all_reduce_small(small_partial)
    d_s, m_s, v_s = _adamw_small(g_s, _pack_small(w), _pack_small(m), _pack_small(v))
    grad_small, delta_small, new_m_small, new_v_small = (_unpack_small(a, w) for a in (g_s, d_s, m_s, v_s))

    def ordered(big, small):
        return [big[n] if n in big else small[n] for n in WEIGHTS]

    return (loss, dx.reshape(x.shape), *ordered(grad_big, grad_small), *ordered(delta_big, delta_small),
            *ordered(new_m_big, new_m_small), *ordered(new_v_big, new_v_small))
```

```python
import jax
import jax.numpy as jnp
from jax import lax
from jax.experimental import pallas as pl
from jax.experimental.pallas import tpu as pltpu

N_DEV = 8
EPS = 1e-6
ROPE_THETA = 10000.0
MLA_HEADS = 4
MLA_Q_RANK = 256
MLA_KV_RANK = 128
MLA_NOPE = 128
MLA_ROPE = 64
MLA_V = 128
MLA_QK = MLA_NOPE + MLA_ROPE
MLA_QK_PAD = 256
SWA_HEADS = 8
SWA_KV = 2
SWA_GROUP = SWA_HEADS // SWA_KV
SWA_D = 64
SWA_BLOCK = 128
ADAM_LR = 0.001
ADAM_B1 = 0.9
ADAM_B2 = 0.999
ADAM_EPS = 1e-08
ADAM_WD = 0.01
ADAM_STEP = 10

LANES = 128
HALF = LANES // 2
SWA_PAIRS = SWA_HEADS // 2
W_IN_COLS = (MLA_Q_RANK + MLA_KV_RANK + MLA_ROPE, SWA_HEADS * SWA_D + 2 * SWA_KV * SWA_D)
VMEM_LIMIT = 56 * 1024 * 1024

_MXU = jnp.bfloat16
_TB = 256
_BQ = 512
_STRIP = 32
_TK = 512
_SWA_STEP = 4
RS_ROW_BLOCKS = 8

BIG = ("ffn1_w_gate", "ffn1_w_up", "ffn1_w_down", "w_in", "mla_w_q_b", "mla_w_kv_b", "w_o",
       "ffn2_w_gate", "ffn2_w_up", "ffn2_w_down")
ROW_SHARDED = ("ffn1_w_down", "w_o", "ffn2_w_down")
SMALL = ("ffn1_norm", "mix_norm", "mla_q_a_norm", "mla_kv_a_norm", "mla_q_norm", "mla_k_norm",
         "swa_q_norm", "swa_k_norm", "swa_sinks", "mla_out_norm", "swa_out_norm", "ffn2_norm")
WEIGHTS = ("ffn1_norm", "ffn1_w_gate", "ffn1_w_up", "ffn1_w_down", "mix_norm", "w_in", "mla_q_a_norm",
           "mla_w_q_b", "mla_kv_a_norm", "mla_w_kv_b", "mla_q_norm", "mla_k_norm", "swa_q_norm",
           "swa_k_norm", "swa_sinks", "mla_out_norm", "swa_out_norm", "w_o", "ffn2_norm",
           "ffn2_w_gate", "ffn2_w_up", "ffn2_w_down")
MESH_AXES = ("x", "y", "c")
MESH = pl.DeviceIdType.MESH
NEG = -1e30


def _f32(t):
    return t.astype(jnp.float32)


def _mx(t):
    return t.astype(_MXU)


def _dot(a, b):
    return jnp.dot(a, b, preferred_element_type=jnp.float32)


def _dot_nt(a, b):
    return lax.dot_general(a, b, (((1,), (1,)), ((), ())), preferred_element_type=jnp.float32)


def _dot_tn(a, b):
    return lax.dot_general(a, b, (((0,), (0,)), ((), ())), preferred_element_type=jnp.float32)


def _rsq(ss, n):
    return lax.rsqrt(ss * (1.0 / n) + EPS)


def _sumsq(t):
    return jnp.sum(t * t, axis=-1, keepdims=True)


def _rowsum(t):
    return jnp.sum(t, axis=-1, keepdims=True)


def _rowmax(t):
    return jnp.max(t, axis=-1, keepdims=True)


def _colsum(t):
    return jnp.sum(t, axis=0, keepdims=True)


def _lane():
    return lax.broadcasted_iota(jnp.int32, (1, LANES), 1)


def _low_half():
    return _lane() < HALF


def _swap32(t):
    return jnp.where((_lane() & 32) == 0, pltpu.roll(t, 96, 1), pltpu.roll(t, 32, 1))


def _rope(t, cos, sin_signed):
    return t * cos + _swap32(t) * sin_signed


def _rope_bwd(d, cos, sin_signed):
    return d * cos + _swap32(d * sin_signed)


def _half_sums(t):
    low = _low_half()
    return jnp.where(low, _rowsum(jnp.where(low, t, 0.0)), _rowsum(jnp.where(low, 0.0, t)))


def _dup_halves(pair):
    low = _low_half()
    swapped = pltpu.roll(pair, HALF, 1)
    return jnp.where(low, pair, swapped), jnp.where(low, swapped, pair)


def _undup_halves(d0, d1):
    return jnp.where(_low_half(), d0 + pltpu.roll(d0, HALF, 1), d1 + pltpu.roll(d1, HALF, 1))


def _pick_halves(a, b):
    return jnp.where(_low_half(), a, b)


def _norm_bwd(dn_list, xh_list, r, n):
    c = sum(_rowsum(dn * xh) for dn, xh in zip(dn_list, xh_list)) * (1.0 / n)
    return [r * (dn - xh * c) for dn, xh in zip(dn_list, xh_list)]


def _cparams(semantics):
    return pltpu.CompilerParams(dimension_semantics=semantics, vmem_limit_bytes=VMEM_LIMIT)


def _const(shape):
    nd = len(shape)
    return pl.BlockSpec(shape, lambda *_: (0,) * nd, pipeline_mode=pl.Buffered(1))


def _acc(shape):
    nd = len(shape)
    return pl.BlockSpec(shape, lambda *_: (0,) * nd)


def _rows(tb, width):
    return pl.BlockSpec((tb, width), lambda i: (i, 0))


def _heads_rows(h, tb, width):
    return pl.BlockSpec((h, tb, width), lambda i: (0, i, 0))


def _sds(shape, dtype):
    return jax.ShapeDtypeStruct(shape, dtype)


def _position():
    return lax.axis_index("x"), lax.axis_index("y"), lax.axis_index("c")


def _all_gather(xp):
    def body(x_ref, out_ref, send_sems, recv_sems, local_sem):
        x, y, c = _position()
        me, sibling = (x, y, c), (x, y, 1 - c)
        chips = [(1 - x, y), (x, 1 - y), (1 - x, 1 - y)]

        def rows(px, py, pc):
            return out_ref.at[4 * px + 2 * py + pc]

        def copy(k, block, to, src=None):
            return pltpu.make_async_remote_copy(
                src_ref=rows(*block) if src is None else src, dst_ref=rows(*block),
                send_sem=send_sems.at[k], recv_sem=recv_sems.at[k], device_id=to, device_id_type=MESH)

        mine = pltpu.make_async_copy(x_ref, rows(*me), local_sem)
        mine.start()
        first = [copy(0, me, sibling, src=x_ref)]
        first += [copy(1 + j, me, (*chip, c), src=x_ref) for j, chip in enumerate(chips)]
        for cp in first:
            cp.start()
        passed = [copy(4 + j, (*chip, c), sibling) for j, chip in enumerate(chips)]
        for j, chip in enumerate(chips):
            copy(1 + j, (*chip, c), me).wait_recv()
            passed[j].start()
        copy(0, sibling, me).wait_recv()
        for j, chip in enumerate(chips):
            copy(4 + j, (*chip, 1 - c), me).wait_recv()
        for cp in first + passed:
            cp.wait_send()
        mine.wait()

    return pl.pallas_call(
        body, name="ag_weights",
        out_shape=_sds((N_DEV,) + xp.shape, xp.dtype),
        in_specs=[pl.BlockSpec(memory_space=pl.ANY)],
        out_specs=pl.BlockSpec(memory_space=pl.ANY),
        scratch_shapes=[pltpu.SemaphoreType.DMA((7,)), pltpu.SemaphoreType.DMA((7,)), pltpu.SemaphoreType.DMA],
    )(xp)


def _relations(x, y):
    return [(x, y), (1 - x, y), (x, 1 - y), (1 - x, 1 - y)]


def _rs_sibling_exchange(g):
    def body(g_ref, out_ref, send_sems, recv_sems):
        x, y, c = _position()
        copies = []
        for k, (px, py) in enumerate(_relations(x, y)):
            copies.append(pltpu.make_async_remote_copy(
                src_ref=g_ref.at[4 * px + 2 * py + (1 - c)], dst_ref=out_ref.at[k],
                send_sem=send_sems.at[k], recv_sem=recv_sems.at[k], device_id=(x, y, 1 - c), device_id_type=MESH))
        for cp in copies:
            cp.start()
        for cp in copies:
            cp.wait()

    return pl.pallas_call(
        body, name="rs_sibling_exchange",
        out_shape=_sds((4,) + g.shape[1:], g.dtype),
        in_specs=[pl.BlockSpec(memory_space=pl.ANY)],
        out_specs=pl.BlockSpec(memory_space=pl.ANY),
        scratch_shapes=[pltpu.SemaphoreType.DMA((4,)), pltpu.SemaphoreType.DMA((4,))],
    )(g)


def _rs_chip_sums(g, sib, dest_idx):
    _, rows_n, width = g.shape
    rb = rows_n // RS_ROW_BLOCKS

    def body(idx_ref, g_ref, s_ref, own_ref, wire_ref):
        k = pl.program_id(1)
        total = g_ref[0] + s_ref[0]
        wire_ref[0] = total.astype(wire_ref.dtype)

        @pl.when(k == 0)
        def _():
            own_ref[...] = total

    return pl.pallas_call(
        body, name="rs_chip_sums",
        grid_spec=pltpu.PrefetchScalarGridSpec(
            num_scalar_prefetch=1, grid=(rows_n // rb, 4),
            in_specs=[pl.BlockSpec((1, rb, width), lambda r, k, idx: (idx[k], r, 0)),
                      pl.BlockSpec((1, rb, width), lambda r, k, idx: (k, r, 0))],
            out_specs=[pl.BlockSpec((rb, width), lambda r, k, idx: (r, 0)),
                       pl.BlockSpec((1, rb, width), lambda r, k, idx: (k, r, 0))]),
        out_shape=[_sds((rows_n, width), jnp.float32), _sds((4, rows_n, width), _MXU)],
        compiler_params=_cparams(("parallel", "arbitrary")),
    )(dest_idx, g, sib)


def _rs_chip_exchange(wire):
    def body(w_ref, out_ref, send_sems, recv_sems):
        x, y, c = _position()
        copies = []
        for k, (px, py) in enumerate(_relations(x, y)[1:]):
            copies.append(pltpu.make_async_remote_copy(
                src_ref=w_ref.at[k + 1], dst_ref=out_ref.at[k],
                send_sem=send_sems.at[k], recv_sem=recv_sems.at[k], device_id=(px, py, c), device_id_type=MESH))
        for cp in copies:
            cp.start()
        for cp in copies:
            cp.wait()

    return pl.pallas_call(
        body, name="rs_chip_exchange",
        out_shape=_sds((3,) + wire.shape[1:], wire.dtype),
        in_specs=[pl.BlockSpec(memory_space=pl.ANY)],
        out_specs=pl.BlockSpec(memory_space=pl.ANY),
        scratch_shapes=[pltpu.SemaphoreType.DMA((3,)), pltpu.SemaphoreType.DMA((3,))],
    )(wire)


def _all_reduce_small(v):
    rows_n = v.shape[0]

    def body(v_ref, out_ref, buf, send_sems, recv_sems):
        x, y, c = _position()
        me = 4 * x + 2 * y + c
        buf[me] = v_ref[...]
        copies = []
        for k in range(1, N_DEV):
            px = 1 - x if k & 4 else x
            py = 1 - y if k & 2 else y
            pc = 1 - c if k & 1 else c
            copies.append(pltpu.make_async_remote_copy(
                src_ref=v_ref, dst_ref=buf.at[me],
                send_sem=send_sems.at[k - 1], recv_sem=recv_sems.at[k - 1], device_id=(px, py, pc), device_id_type=MESH))
        for cp in copies:
            cp.start()
        for cp in copies:
            cp.wait()
        total = buf[0]
        for d in range(1, N_DEV):
            total = total + buf[d]
        out_ref[...] = total

    return pl.pallas_call(
        body, name="ar_small",
        out_shape=_sds((rows_n, LANES), jnp.float32),
        in_specs=[pl.BlockSpec(memory_space=pltpu.VMEM)],
        out_specs=pl.BlockSpec(memory_space=pltpu.VMEM),
        scratch_shapes=[pltpu.VMEM((N_DEV, rows_n, LANES), jnp.float32),
                        pltpu.SemaphoreType.DMA((N_DEV - 1,)), pltpu.SemaphoreType.DMA((N_DEV - 1,))],
    )(v)


def _adamw_math(w, g, m, v):
    m = ADAM_B1 * m + (1.0 - ADAM_B1) * g
    v = ADAM_B2 * v + (1.0 - ADAM_B2) * (g * g)
    m_hat = m / (1.0 - ADAM_B1 ** ADAM_STEP)
    v_hat = v / (1.0 - ADAM_B2 ** ADAM_STEP)
    delta = -ADAM_LR * (m_hat / (jnp.sqrt(v_hat) + ADAM_EPS) + ADAM_WD * w)
    return delta, m, v


def _rs_sum(own, recv):
    rows_n, width = own.shape
    rb = rows_n // RS_ROW_BLOCKS

    def body(own_ref, r0, r1, r2, g_out):
        g_out[...] = ((own_ref[...] + _f32(r0[0])) + _f32(r1[0])) + _f32(r2[0])

    row = pl.BlockSpec((rb, width), lambda r: (r, 0))

    def slot(k):
        return pl.BlockSpec((1, rb, width), lambda r: (k, r, 0))

    return pl.pallas_call(
        body, name="rs_sum", grid=(RS_ROW_BLOCKS,),
        in_specs=[row, slot(0), slot(1), slot(2)], out_specs=row,
        out_shape=_sds((rows_n, width), jnp.float32),
        compiler_params=_cparams(("parallel",)),
    )(own, recv, recv, recv)


def _adamw(g, w, m, v, name):
    depth, a, b = w.shape

    def body(g_ref, w_ref, m_ref, v_ref, d_out, m_out, v_out):
        delta, m2, v2 = _adamw_math(w_ref[...], g_ref[...], m_ref[...], v_ref[...])
        d_out[...] = delta
        m_out[...] = m2
        v_out[...] = v2

    layer = pl.BlockSpec((1, a, b), lambda l: (l, 0, 0))
    return pl.pallas_call(
        body, name="adamw_" + name, grid=(depth,),
        in_specs=[layer] * 4, out_specs=[layer] * 3,
        out_shape=[_sds(w.shape, jnp.float32)] * 3,
        compiler_params=_cparams(("parallel",)),
    )(g, w, m, v)


def _adamw_small(g, w, m, v):
    def body(g_ref, w_ref, m_ref, v_ref, d_out, m_out, v_out):
        delta, m2, v2 = _adamw_math(w_ref[...], g_ref[...], m_ref[...], v_ref[...])
        d_out[...] = delta
        m_out[...] = m2
        v_out[...] = v2

    vm = pl.BlockSpec(memory_space=pltpu.VMEM)
    return pl.pallas_call(
        body, name="adamw_small",
        in_specs=[vm] * 4, out_specs=[vm] * 3,
        out_shape=[_sds(g.shape, jnp.float32)] * 3,
    )(g, w, m, v)


def _f_chunk(f):
    for cand in (1408, 1024, 512, 256, 128):
        if f % cand == 0:
            return cand
    return f


def _ffn_fwd(x, gain, wg, wu, wd):
    t, d = x.shape
    f = wg.shape[0]
    tb = min(_TB, t)
    fc = _f_chunk(f)

    def body(x_ref, g_ref, wg_ref, wu_ref, wd_ref, xo_ref, a_ref, u_ref):
        xv = x_ref[...]
        hb = _mx(xv * _rsq(_sumsq(xv), d) * g_ref[...])
        y = jnp.zeros((tb, d), jnp.float32)
        for c0 in range(0, f, fc):
            a = _dot_nt(hb, wg_ref[c0:c0 + fc, :])
            u = _dot_nt(hb, wu_ref[c0:c0 + fc, :])
            a_ref[:, c0:c0 + fc] = a.astype(a_ref.dtype)
            u_ref[:, c0:c0 + fc] = u.astype(u_ref.dtype)
            s = a * jax.nn.sigmoid(a) * u
            y = y + _dot(_mx(s), wd_ref[c0:c0 + fc, :])
        xo_ref[...] = xv + 0.5 * y

    return pl.pallas_call(
        body, name="ffn_fwd", grid=(t // tb,),
        in_specs=[_rows(tb, d), _const((1, d)), _const((f, d)), _const((f, d)), _const((f, d))],
        out_specs=[_rows(tb, d), _rows(tb, f), _rows(tb, f)],
        out_shape=[_sds((t, d), jnp.float32), _sds((t, f), _MXU), _sds((t, f), _MXU)],
        compiler_params=_cparams(("parallel",)),
    )(x, gain, wg, wu, wd)


def _ffn_dgrad(x, gain, dxo, a, u, wg, wu, wd):
    t, d = x.shape
    f = wg.shape[0]
    tb = min(_TB, t)
    fc = _f_chunk(f)

    def body(x_ref, g_ref, dxo_ref, a_ref, u_ref, wg_ref, wu_ref, wd_ref,
             dxi_ref, da_ref, du_ref, s_ref, h_ref, dy_ref, dg_ref):
        xv = x_ref[...]
        gv = g_ref[...]
        r = _rsq(_sumsq(xv), d)
        xhat = xv * r
        h_ref[...] = _mx(xhat * gv)
        dxo = dxo_ref[...]
        dyb = _mx(0.5 * dxo)
        dy_ref[...] = dyb
        dh = jnp.zeros((tb, d), jnp.float32)
        for c0 in range(0, f, fc):
            ds = _dot_nt(dyb, wd_ref[c0:c0 + fc, :])
            av = _f32(a_ref[:, c0:c0 + fc])
            uv = _f32(u_ref[:, c0:c0 + fc])
            sig = jax.nn.sigmoid(av)
            silu = av * sig
            da = _mx(ds * uv * (sig * (1.0 + av * (1.0 - sig))))
            du = _mx(ds * silu)
            s_ref[:, c0:c0 + fc] = _mx(silu * uv)
            da_ref[:, c0:c0 + fc] = da
            du_ref[:, c0:c0 + fc] = du
            dh = dh + _dot(da, wg_ref[c0:c0 + fc, :]) + _dot(du, wu_ref[c0:c0 + fc, :])

        @pl.when(pl.program_id(0) == 0)
        def _():
            dg_ref[...] = jnp.zeros_like(dg_ref)

        dg_ref[...] += _colsum(dh * xhat)
        dn = dh * gv
        dxi_ref[...] = dxo + r * (dn - xhat * (_rowsum(dn * xhat) * (1.0 / d)))

    return pl.pallas_call(
        body, name="ffn_dgrad", grid=(t // tb,),
        in_specs=[_rows(tb, d), _const((1, d)), _rows(tb, d), _rows(tb, f), _rows(tb, f),
                  _const((f, d)), _const((f, d)), _const((f, d))],
        out_specs=[_rows(tb, d), _rows(tb, f), _rows(tb, f), _rows(tb, f), _rows(tb, d), _rows(tb, d),
                   _acc((1, d))],
        out_shape=[_sds((t, d), jnp.float32), _sds((t, f), _MXU), _sds((t, f), _MXU), _sds((t, f), _MXU),
                   _sds((t, d), _MXU), _sds((t, d), _MXU), _sds((1, d), jnp.float32)],
        compiler_params=_cparams(("arbitrary",)),
    )(x, gain, dxo, a, u, wg, wu, wd)


def _tn_matmul(a, b, name):
    g_n, t, m = a.shape
    n = b.shape[1]
    tk = min(_TK, t)
    tn = n
    while m * tn * 4 > 12 * 1024 * 1024 and tn % 256 == 0:
        tn //= 2

    def body(a_ref, b_ref, o_ref):
        @pl.when(pl.program_id(2) == 0)
        def _():
            o_ref[...] = jnp.zeros_like(o_ref)

        o_ref[0] += _dot_tn(a_ref[0], b_ref[...])

    return pl.pallas_call(
        body, name=name, grid=(g_n, n // tn, t // tk),
        in_specs=[pl.BlockSpec((1, tk, m), lambda g, j, k: (g, k, 0)),
                  pl.BlockSpec((tk, tn), lambda g, j, k: (k, j))],
        out_specs=pl.BlockSpec((1, m, tn), lambda g, j, k: (g, 0, j)),
        out_shape=_sds((g_n, m, n), jnp.float32),
        compiler_params=_cparams(("parallel", "parallel", "arbitrary")),
    )(a, b)


PREP_WEIGHTS = ("mix_g", "w_in", "g_qa", "wqb", "g_kva", "w_kvb", "gq_n", "gq_r", "gk_n", "gk_r", "g_sq", "g_sk")
C_CQ, C_CKV, C_KPE, C_QS = 0, MLA_Q_RANK, MLA_Q_RANK + MLA_KV_RANK, MLA_Q_RANK + MLA_KV_RANK + LANES
C_KS = C_QS + SWA_HEADS * SWA_D
C_VS = C_KS + LANES
W_IN_PACKED = C_VS + LANES


def _prep_specs(p):
    return [_const(p[n].shape) for n in PREP_WEIGHTS]


def _pair_norm_rope(t, gain, cos, sin_s):
    return _rope(t * _rsq(_half_sums(t * t), SWA_D) * gain, cos, sin_s)


def _prep_fwd(x, cos, sin_s, p):
    t, d = x.shape
    tb = min(_TB, t)

    def body(x_ref, cos_ref, sin_ref, mix_g, w_in, g_qa, wqb, g_kva, w_kvb, gq_n, gq_r, gk_n, gk_r, g_sq, g_sk,
             qa_ref, ka_ref, va_ref, qb_ref, kb_ref, vb_ref):
        xv = x_ref[...]
        cos_v, sin_v = cos_ref[...], sin_ref[...]
        hb = _mx(xv * _rsq(_sumsq(xv), d) * mix_g[...])
        proj = _dot_nt(hb, w_in[...])
        cq = proj[:, C_CQ:C_CKV]
        cqn = _mx(cq * _rsq(_sumsq(cq), MLA_Q_RANK) * g_qa[...])
        for h in range(MLA_HEADS):
            qh = _dot_nt(cqn, wqb[h])
            qn, qr = qh[:, :MLA_NOPE], qh[:, MLA_NOPE:]
            rh = _rsq(_sumsq(qn) + _sumsq(qr), MLA_QK)
            qa_ref[h, :, 0:MLA_NOPE] = (qn * rh * gq_n[...]).astype(qa_ref.dtype)
            qa_ref[h, :, MLA_NOPE:MLA_QK_PAD] = _rope(qr * rh * gq_r[...], cos_v, sin_v).astype(qa_ref.dtype)
        ckv = proj[:, C_CKV:C_KPE]
        ckvn = _mx(ckv * _rsq(_sumsq(ckv), MLA_KV_RANK) * g_kva[...])
        kpe = proj[:, C_KPE:C_QS]
        ss_pe = _sumsq(kpe)
        kv = _dot_nt(ckvn, w_kvb[...])
        for h in range(MLA_HEADS):
            c0 = h * (MLA_NOPE + MLA_V)
            kn = kv[:, c0:c0 + MLA_NOPE]
            rh = _rsq(_sumsq(kn) + ss_pe, MLA_QK)
            ka_ref[h, :, 0:MLA_NOPE] = (kn * rh * gk_n[...]).astype(ka_ref.dtype)
            ka_ref[h, :, MLA_NOPE:MLA_QK_PAD] = _rope(kpe * rh * gk_r[...], cos_v, sin_v).astype(ka_ref.dtype)
            va_ref[h] = kv[:, c0 + MLA_NOPE:c0 + MLA_NOPE + MLA_V].astype(va_ref.dtype)
        for j in range(SWA_PAIRS):
            c0 = C_QS + j * LANES
            qb_ref[j] = _pair_norm_rope(proj[:, c0:c0 + LANES], g_sq[...], cos_v, sin_v).astype(qb_ref.dtype)
        k0, k1 = _dup_halves(_pair_norm_rope(proj[:, C_KS:C_VS], g_sk[...], cos_v, sin_v))
        kb_ref[0] = k0.astype(kb_ref.dtype)
        kb_ref[1] = k1.astype(kb_ref.dtype)
        v0, v1 = _dup_halves(proj[:, C_VS:W_IN_PACKED])
        vb_ref[0] = v0.astype(vb_ref.dtype)
        vb_ref[1] = v1.astype(vb_ref.dtype)

    return pl.pallas_call(
        body, name="prep_fwd", grid=(t // tb,),
        in_specs=[_rows(tb, d), _rows(tb, LANES), _rows(tb, LANES)] + _prep_specs(p),
        out_specs=[_heads_rows(MLA_HEADS, tb, MLA_QK_PAD), _heads_rows(MLA_HEADS, tb, MLA_QK_PAD),
                   _heads_rows(MLA_HEADS, tb, MLA_V), _heads_rows(SWA_PAIRS, tb, LANES),
                   _heads_rows(SWA_KV, tb, LANES), _heads_rows(SWA_KV, tb, LANES)],
        out_shape=[_sds((MLA_HEADS, t, MLA_QK_PAD), _MXU), _sds((MLA_HEADS, t, MLA_QK_PAD), _MXU),
                   _sds((MLA_HEADS, t, MLA_V), _MXU), _sds((SWA_PAIRS, t, LANES), _MXU),
                   _sds((SWA_KV, t, LANES), _MXU), _sds((SWA_KV, t, LANES), _MXU)],
        compiler_params=_cparams(("parallel",)),
    )(x, cos, sin_s, *[p[n] for n in PREP_WEIGHTS])


def _prep_bwd(x, dxin, cos, sin_s, p, dqa, dka, dva, dqb, dkb, dvb):
    t, d = x.shape
    tb = min(_TB, t)
    n_w = len(PREP_WEIGHTS)

    def body(*refs):
        x_ref, dxin_ref, cos_ref, sin_ref = refs[:4]
        mix_g, w_in, g_qa, wqb, g_kva, w_kvb, gq_n, gq_r, gk_n, gk_r, g_sq, g_sk = refs[4:4 + n_w]
        dqa_ref, dka_ref, dva_ref, dqb_ref, dkb_ref, dvb_ref = refs[4 + n_w:10 + n_w]
        dx_ref = refs[10 + n_w]
        grads = dict(zip(PREP_WEIGHTS, refs[11 + n_w:11 + 2 * n_w]))
        dproj_ref, dkv_ref, dqh_ref = refs[11 + 2 * n_w:]

        @pl.when(pl.program_id(0) == 0)
        def _():
            for ref in grads.values():
                ref[...] = jnp.zeros_like(ref)

        xv = x_ref[...]
        cos_v, sin_v = cos_ref[...], sin_ref[...]
        r0 = _rsq(_sumsq(xv), d)
        xhat = xv * r0
        hb = _mx(xhat * mix_g[...])
        proj = _dot_nt(hb, w_in[...])

        cq = proj[:, C_CQ:C_CKV]
        rq = _rsq(_sumsq(cq), MLA_Q_RANK)
        cqh = cq * rq
        cqn = _mx(cqh * g_qa[...])
        dcqn = jnp.zeros((tb, MLA_Q_RANK), jnp.float32)
        for h in range(MLA_HEADS):
            qh = _dot_nt(cqn, wqb[h])
            qn, qr = qh[:, :MLA_NOPE], qh[:, MLA_NOPE:]
            rh = _rsq(_sumsq(qn) + _sumsq(qr), MLA_QK)
            xh_n, xh_r = qn * rh, qr * rh
            dy_n = dqa_ref[h, :, 0:MLA_NOPE]
            dy_r = _rope_bwd(dqa_ref[h, :, MLA_NOPE:MLA_QK_PAD], cos_v, sin_v)
            grads["gq_n"][...] += _colsum(dy_n * xh_n)
            grads["gq_r"][...] += _colsum(dy_r * xh_r)
            dqn, dqr = _norm_bwd([dy_n * gq_n[...], dy_r * gq_r[...]], [xh_n, xh_r], rh, MLA_QK)
            dqh_ref[:, 0:MLA_NOPE] = _mx(dqn)
            dqh_ref[:, MLA_NOPE:MLA_QK_PAD] = _mx(dqr)
            dqh = dqh_ref[...]
            grads["wqb"][h] += _dot_tn(dqh, cqn)
            dcqn = dcqn + _dot(dqh, wqb[h])
        grads["g_qa"][...] += _colsum(dcqn * cqh)
        (dcq,) = _norm_bwd([dcqn * g_qa[...]], [cqh], rq, MLA_Q_RANK)
        dproj_ref[:, C_CQ:C_CKV] = _mx(dcq)

        ckv = proj[:, C_CKV:C_KPE]
        rkv = _rsq(_sumsq(ckv), MLA_KV_RANK)
        ckvh = ckv * rkv
        ckvn = _mx(ckvh * g_kva[...])
        kpe = proj[:, C_KPE:C_QS]
        ss_pe = _sumsq(kpe)
        kv = _dot_nt(ckvn, w_kvb[...])
        dkpe = jnp.zeros((tb, LANES), jnp.float32)
        for h in range(MLA_HEADS):
            c0 = h * (MLA_NOPE + MLA_V)
            c1 = c0 + MLA_NOPE
            kn = kv[:, c0:c1]
            rh = _rsq(_sumsq(kn) + ss_pe, MLA_QK)
            xh_n, xh_r = kn * rh, kpe * rh
            dy_n = dka_ref[h, :, 0:MLA_NOPE]
            dy_r = _rope_bwd(dka_ref[h, :, MLA_NOPE:MLA_QK_PAD], cos_v, sin_v)
            grads["gk_n"][...] += _colsum(dy_n * xh_n)
            grads["gk_r"][...] += _colsum(dy_r * xh_r)
            dkn, dkr = _norm_bwd([dy_n * gk_n[...], dy_r * gk_r[...]], [xh_n, xh_r], rh, MLA_QK)
            dkpe = dkpe + dkr
            dkv_ref[:, c0:c1] = _mx(dkn)
            dkv_ref[:, c1:c1 + MLA_V] = _mx(dva_ref[h])
        dkv = dkv_ref[...]
        grads["w_kvb"][...] += _dot_tn(dkv, ckvn)
        dckvn = _dot(dkv, w_kvb[...])
        grads["g_kva"][...] += _colsum(dckvn * ckvh)
        (dckv,) = _norm_bwd([dckvn * g_kva[...]], [ckvh], rkv, MLA_KV_RANK)
        dproj_ref[:, C_CKV:C_KPE] = _mx(dckv)
        dproj_ref[:, C_KPE:C_QS] = _mx(dkpe)

        def pair_bwd(tv, dy, g_ref, gname):
            r = _rsq(_half_sums(tv * tv), SWA_D)
            xh = tv * r
            dpre = _rope_bwd(dy, cos_v, sin_v)
            grads[gname][...] += _colsum(dpre * xh)
            dn = dpre * g_ref[...]
            return r * (dn - xh * (_half_sums(dn * xh) * (1.0 / SWA_D)))

        for j in range(SWA_PAIRS):
            c0 = C_QS + j * LANES
            dproj_ref[:, c0:c0 + LANES] = _mx(pair_bwd(proj[:, c0:c0 + LANES], dqb_ref[j], g_sq, "g_sq"))
        dproj_ref[:, C_KS:C_VS] = _mx(pair_bwd(proj[:, C_KS:C_VS], _undup_halves(dkb_ref[0], dkb_ref[1]), g_sk, "g_sk"))
        dproj_ref[:, C_VS:W_IN_PACKED] = _mx(_undup_halves(dvb_ref[0], dvb_ref[1]))

        dproj = dproj_ref[...]
        grads["w_in"][...] += _dot_tn(dproj, hb)
        dh = _dot(dproj, w_in[...])
        grads["mix_g"][...] += _colsum(dh * xhat)
        (dxv,) = _norm_bwd([dh * mix_g[...]], [xhat], r0, d)
        dx_ref[...] = dxin_ref[...] + dxv

    grad_shapes = [p[n].shape for n in PREP_WEIGHTS]
    return pl.pallas_call(
        body, name="prep_bwd", grid=(t // tb,),
        in_specs=[_rows(tb, d), _rows(tb, d), _rows(tb, LANES), _rows(tb, LANES)] + _prep_specs(p) + [
            _heads_rows(MLA_HEADS, tb, MLA_QK_PAD), _heads_rows(MLA_HEADS, tb, MLA_QK_PAD),
            _heads_rows(MLA_HEADS, tb, MLA_V), _heads_rows(SWA_PAIRS, tb, LANES),
            _heads_rows(SWA_KV, tb, LANES), _heads_rows(SWA_KV, tb, LANES)],
        out_specs=[_rows(tb, d)] + [_acc(s) for s in grad_shapes],
        out_shape=[_sds((t, d), jnp.float32)] + [_sds(s, jnp.float32) for s in grad_shapes],
        scratch_shapes=[pltpu.VMEM((tb, W_IN_PACKED), _MXU), pltpu.VMEM((tb, MLA_HEADS * (MLA_NOPE + MLA_V)), _MXU),
                        pltpu.VMEM((tb, MLA_QK_PAD), _MXU)],
        compiler_params=_cparams(("arbitrary",)),
    )(x, dxin, cos, sin_s, *[p[n] for n in PREP_WEIGHTS], dqa, dka, dva, dqb, dkb, dvb)


def _strips(n):
    step = min(_STRIP, n)
    return [slice(r, r + step) for r in range(0, n, step)]


def _mla_fwd(q, k, v):
    hn, t, dq = q.shape
    dv = v.shape[2]
    bq = min(_BQ, t)
    scale = MLA_QK ** -0.5

    def body(q_ref, k_ref, v_ref, o_ref, l_ref):
        i = pl.program_id(1)
        qv = q_ref[0]

        def step(j, carry, masked):
            m, l, acc = carry
            start = pl.multiple_of(j * bq, bq)
            s = _dot_nt(qv, k_ref[0, pl.ds(start, bq), :]) * scale
            if masked:
                row = lax.broadcasted_iota(jnp.int32, (bq, bq), 0)
                col = lax.broadcasted_iota(jnp.int32, (bq, bq), 1)
                s = jnp.where(col <= row, s, NEG)
            m_new = jnp.maximum(m, _rowmax(s))
            alpha = jnp.exp(m - m_new)
            pv = jnp.exp(s - m_new)
            l = alpha * l + _rowsum(pv)
            acc = alpha * acc + _dot(_mx(pv), v_ref[0, pl.ds(start, bq), :])
            return m_new, l, acc

        init = (jnp.full((bq, 1), NEG, jnp.float32), jnp.zeros((bq, 1), jnp.float32), jnp.zeros((bq, dv), jnp.float32))
        carry = lax.fori_loop(0, i, lambda j, c: step(j, c, False), init)
        m, l, acc = step(i, carry, True)
        o_ref[0] = acc / l
        l_ref[0] = jnp.broadcast_to(m + jnp.log(l), (bq, LANES))

    return pl.pallas_call(
        body, name="mla_fwd", grid=(hn, t // bq),
        in_specs=[pl.BlockSpec((1, bq, dq), lambda h, i: (h, i, 0)),
                  pl.BlockSpec((1, t, dq), lambda h, i: (h, 0, 0)),
                  pl.BlockSpec((1, t, dv), lambda h, i: (h, 0, 0))],
        out_specs=[pl.BlockSpec((1, bq, dv), lambda h, i: (h, i, 0)),
                   pl.BlockSpec((1, bq, LANES), lambda h, i: (h, i, 0))],
        out_shape=[_sds((hn, t, dv), jnp.float32), _sds((hn, t, LANES), jnp.float32)],
        compiler_params=_cparams(("parallel", "arbitrary")),
    )(q, k, v)


def _mla_bwd(q, k, v, do, lse_rows, dsum_rows):
    hn, t, dq_w = q.shape
    dv_w = v.shape[2]
    bq = min(_BQ, t)
    nb = t // bq
    scale = MLA_QK ** -0.5

    def body(q_ref, do_ref, l_ref, d_ref, k_ref, v_ref, dq_ref, dk_ref, dv_ref, st_scr, dpt_scr, p_scr, ds_scr):
        j = pl.program_id(1)

        @pl.when(j == 0)
        def _():
            dq_ref[...] = jnp.zeros_like(dq_ref)

        kv = k_ref[0]
        vv = v_ref[0]
        dk_ref[0] = jnp.zeros((bq, dq_w), jnp.float32)
        dv_ref[0] = jnp.zeros((bq, dv_w), jnp.float32)

        def tile(i, masked):
            start = pl.multiple_of(i * bq, bq)
            qv = q_ref[0, pl.ds(start, bq), :]
            dov = do_ref[0, pl.ds(start, bq), :]
            st_scr[...] = _dot_nt(kv, qv)
            dpt_scr[...] = _dot_nt(vv, dov)
            lse = l_ref[0, i]
            dsum = d_ref[0, i]
            for rows in _strips(bq):
                pt = jnp.exp(st_scr[rows, :] * scale - lse)
                if masked:
                    n_rows = rows.stop - rows.start
                    row = lax.broadcasted_iota(jnp.int32, (n_rows, bq), 0) + rows.start
                    col = lax.broadcasted_iota(jnp.int32, (n_rows, bq), 1)
                    pt = jnp.where(row <= col, pt, 0.0)
                p_scr[rows, :] = _mx(pt)
                ds_scr[rows, :] = _mx(pt * (dpt_scr[rows, :] - dsum) * scale)
            ds_t = ds_scr[...]
            dv_ref[0] += _dot(p_scr[...], dov)
            dk_ref[0] += _dot(ds_t, qv)
            dq_ref[0, pl.ds(start, bq), :] += _dot_tn(ds_t, kv)

        def loop_body(i, carry):
            tile(i, False)
            return carry

        tile(j, True)
        lax.fori_loop(j + 1, nb, loop_body, 0)

    return pl.pallas_call(
        body, name="mla_bwd", grid=(hn, nb),
        in_specs=[pl.BlockSpec((1, t, dq_w), lambda h, j: (h, 0, 0)),
                  pl.BlockSpec((1, t, dv_w), lambda h, j: (h, 0, 0)),
                  pl.BlockSpec((1, nb, 1, bq), lambda h, j: (h, 0, 0, 0)),
                  pl.BlockSpec((1, nb, 1, bq), lambda h, j: (h, 0, 0, 0)),
                  pl.BlockSpec((1, bq, dq_w), lambda h, j: (h, j, 0)),
                  pl.BlockSpec((1, bq, dv_w), lambda h, j: (h, j, 0))],
        out_specs=[pl.BlockSpec((1, t, dq_w), lambda h, j: (h, 0, 0)),
                   pl.BlockSpec((1, bq, dq_w), lambda h, j: (h, j, 0)),
                   pl.BlockSpec((1, bq, dv_w), lambda h, j: (h, j, 0))],
        out_shape=[_sds((hn, t, dq_w), jnp.float32), _sds((hn, t, dq_w), jnp.float32), _sds((hn, t, dv_w), jnp.float32)],
        scratch_shapes=[pltpu.VMEM((bq, bq), jnp.float32), pltpu.VMEM((bq, bq), jnp.float32),
                        pltpu.VMEM((bq, bq), _MXU), pltpu.VMEM((bq, bq), _MXU)],
        compiler_params=_cparams(("parallel", "arbitrary")),
    )(q, do, lse_rows, dsum_rows, k, v)


STACK = SWA_GROUP * SWA_BLOCK


def _swa_stack(ref, c, rows):
    low = _low_half()
    parts = []
    for g in range(SWA_GROUP):
        tv = ref[SWA_GROUP // 2 * c + g // 2, rows, :]
        keep = low if g % 2 == 0 else jnp.logical_not(low)
        parts.append(_mx(jnp.where(keep, tv, jnp.zeros_like(tv))))
    return jnp.concatenate(parts, axis=0)


def _swa_cols(ref, c, rows):
    return jnp.concatenate([ref[SWA_GROUP * c + g, rows, 0:1] for g in range(SWA_GROUP)], axis=0)


def _swa_sink_col(s_ref, c):
    return jnp.concatenate([jnp.broadcast_to(s_ref[SWA_GROUP * c + g][:, 0:1], (SWA_BLOCK, 1))
                            for g in range(SWA_GROUP)], axis=0)


def _swa_band_masks():
    row = lax.broadcasted_iota(jnp.int32, (STACK, SWA_BLOCK), 0) & (SWA_BLOCK - 1)
    col = lax.broadcasted_iota(jnp.int32, (STACK, SWA_BLOCK), 1)
    return col <= row, col > row


def _swa_unstack_pairs(ref, c, rows, stacked):
    for pr in range(SWA_GROUP // 2):
        r0 = 2 * pr * SWA_BLOCK
        ref[SWA_GROUP // 2 * c + pr, rows, :] = _pick_halves(stacked[r0:r0 + SWA_BLOCK], stacked[r0 + SWA_BLOCK:r0 + 2 * SWA_BLOCK])


def _swa_blocks(t):
    nblk = t // SWA_BLOCK
    bps = min(_SWA_STEP, nblk)
    return nblk, bps, bps * SWA_BLOCK


def _swa_fwd(q, k, v, sinks):
    _, t, _ = q.shape
    nblk, bps, sb = _swa_blocks(t)
    scale = SWA_D ** -0.5

    def body(q_ref, k_ref, kp_ref, v_ref, vp_ref, s_ref, o_ref, l_ref):
        n = pl.program_id(0)
        m_cur, m_prev = _swa_band_masks()
        for c in range(SWA_KV):
            sink = _swa_sink_col(s_ref, c)
            for b in range(bps):
                rows = slice(b * SWA_BLOCK, (b + 1) * SWA_BLOCK)
                kc, vc = k_ref[c, rows, :], v_ref[c, rows, :]
                if b == 0:
                    kp, vp, mp = kp_ref[c], vp_ref[c], jnp.logical_and(m_prev, n > 0)
                else:
                    before = slice((b - 1) * SWA_BLOCK, b * SWA_BLOCK)
                    kp, vp, mp = k_ref[c, before, :], v_ref[c, before, :], m_prev
                qs = _swa_stack(q_ref, c, rows)
                s_c = jnp.where(m_cur, _dot_nt(qs, kc) * scale, NEG)
                s_p = jnp.where(mp, _dot_nt(qs, kp) * scale, NEG)
                m = jnp.maximum(jnp.maximum(_rowmax(s_c), _rowmax(s_p)), sink)
                e_c = jnp.exp(s_c - m)
                e_p = jnp.exp(s_p - m)
                denom = _rowsum(e_c) + _rowsum(e_p) + jnp.exp(sink - m)
                inv = 1.0 / denom
                o = _dot(_mx(e_c * inv), vc) + _dot(_mx(e_p * inv), vp)
                lse = m + jnp.log(denom)
                for g in range(SWA_GROUP):
                    l_ref[SWA_GROUP * c + g, rows, :] = jnp.broadcast_to(
                        lse[g * SWA_BLOCK:(g + 1) * SWA_BLOCK], (SWA_BLOCK, LANES))
                _swa_unstack_pairs(o_ref, c, rows, o)

    main = lambda n: (0, n, 0)
    prev = lambda n: (0, jnp.maximum(n * bps - 1, 0), 0)
    return pl.pallas_call(
        body, name="swa_fwd", grid=(nblk // bps,),
        in_specs=[pl.BlockSpec((SWA_PAIRS, sb, LANES), main),
                  pl.BlockSpec((SWA_KV, sb, LANES), main), pl.BlockSpec((SWA_KV, SWA_BLOCK, LANES), prev),
                  pl.BlockSpec((SWA_KV, sb, LANES), main), pl.BlockSpec((SWA_KV, SWA_BLOCK, LANES), prev),
                  _const((SWA_HEADS, 1, LANES))],
        out_specs=[pl.BlockSpec((SWA_PAIRS, sb, LANES), main), pl.BlockSpec((SWA_HEADS, sb, LANES), main)],
        out_shape=[_sds((SWA_PAIRS, t, LANES), jnp.float32), _sds((SWA_HEADS, t, LANES), jnp.float32)],
        compiler_params=_cparams(("parallel",)),
    )(q, k, k, v, v, sinks)


def _swa_bwd(q, k, v, sinks, do, lse, dsum):
    _, t, _ = q.shape
    nblk, bps, sb = _swa_blocks(t)
    steps = nblk // bps
    scale = SWA_D ** -0.5

    def body(q_ref, k_ref, kp_ref, v_ref, vp_ref, s_ref, do_ref, l_ref, d_ref, qn_ref, don_ref, ln_ref, dn_ref,
             dq_ref, dk_ref, dv_ref, ds_ref):
        n = pl.program_id(0)

        @pl.when(n == 0)
        def _():
            ds_ref[...] = jnp.zeros_like(ds_ref)

        m_cur, m_prev = _swa_band_masks()
        everything = slice(0, SWA_BLOCK)

        def probs(qs, keys, mask, lcol):
            return jnp.where(mask, jnp.exp(_dot_nt(qs, keys) * scale - lcol), 0.0)

        def dscores(pm, dos, vals, dcol):
            return _mx(pm * (_dot_nt(dos, vals) - dcol) * scale)

        for c in range(SWA_KV):
            sink = _swa_sink_col(s_ref, c)
            dk_acc = [jnp.zeros((SWA_BLOCK, LANES), jnp.float32) for _ in range(bps)]
            dv_acc = [jnp.zeros((SWA_BLOCK, LANES), jnp.float32) for _ in range(bps)]
            for b in range(bps):
                rows = slice(b * SWA_BLOCK, (b + 1) * SWA_BLOCK)
                kc, vc = k_ref[c, rows, :], v_ref[c, rows, :]
                if b == 0:
                    kp, vp, mp = kp_ref[c], vp_ref[c], jnp.logical_and(m_prev, n > 0)
                else:
                    before = slice((b - 1) * SWA_BLOCK, b * SWA_BLOCK)
                    kp, vp, mp = k_ref[c, before, :], v_ref[c, before, :], m_prev
                qs = _swa_stack(q_ref, c, rows)
                dos = _swa_stack(do_ref, c, rows)
                lcol = _swa_cols(l_ref, c, rows)
                dcol = _swa_cols(d_ref, c, rows)
                p_c = probs(qs, kc, m_cur, lcol)
                p_p = probs(qs, kp, mp, lcol)
                ds_c = dscores(p_c, dos, vc, dcol)
                ds_p = dscores(p_p, dos, vp, dcol)
                _swa_unstack_pairs(dq_ref, c, rows, _dot(ds_c, kc) + _dot(ds_p, kp))
                dk_acc[b] = dk_acc[b] + _dot_tn(ds_c, qs)
                dv_acc[b] = dv_acc[b] + _dot_tn(_mx(p_c), dos)
                if b > 0:
                    dk_acc[b - 1] = dk_acc[b - 1] + _dot_tn(ds_p, qs)
                    dv_acc[b - 1] = dv_acc[b - 1] + _dot_tn(_mx(p_p), dos)
                p_sink = jnp.exp(sink - lcol) * dcol
                for g in range(SWA_GROUP):
                    ds_ref[SWA_GROUP * c + g] += -jnp.sum(p_sink[g * SWA_BLOCK:(g + 1) * SWA_BLOCK])
            tail = slice((bps - 1) * SWA_BLOCK, bps * SWA_BLOCK)
            kc, vc = k_ref[c, tail, :], v_ref[c, tail, :]
            qs = _swa_stack(qn_ref, c, everything)
            dos = _swa_stack(don_ref, c, everything)
            lcol = _swa_cols(ln_ref, c, everything)
            dcol = _swa_cols(dn_ref, c, everything)
            p_p = probs(qs, kc, jnp.logical_and(m_prev, n < steps - 1), lcol)
            ds_p = dscores(p_p, dos, vc, dcol)
            dk_acc[bps - 1] = dk_acc[bps - 1] + _dot_tn(ds_p, qs)
            dv_acc[bps - 1] = dv_acc[bps - 1] + _dot_tn(_mx(p_p), dos)
            for b in range(bps):
                rows = slice(b * SWA_BLOCK, (b + 1) * SWA_BLOCK)
                dk_ref[c, rows, :] = dk_acc[b]
                dv_ref[c, rows, :] = dv_acc[b]

    main = lambda n: (0, n, 0)
    prev = lambda n: (0, jnp.maximum(n * bps - 1, 0), 0)
    nxt = lambda n: (0, jnp.minimum((n + 1) * bps, nblk - 1), 0)
    pairs = pl.BlockSpec((SWA_PAIRS, sb, LANES), main)
    kvs = pl.BlockSpec((SWA_KV, sb, LANES), main)
    kv_prev = pl.BlockSpec((SWA_KV, SWA_BLOCK, LANES), prev)
    stats = pl.BlockSpec((SWA_HEADS, sb, LANES), main)
    pairs_next = pl.BlockSpec((SWA_PAIRS, SWA_BLOCK, LANES), nxt)
    stats_next = pl.BlockSpec((SWA_HEADS, SWA_BLOCK, LANES), nxt)
    return pl.pallas_call(
        body, name="swa_bwd", grid=(steps,),
        in_specs=[pairs, kvs, kv_prev, kvs, kv_prev, _const((SWA_HEADS, 1, LANES)), pairs, stats, stats,
                  pairs_next, pairs_next, stats_next, stats_next],
        out_specs=[pairs, kvs, kvs, _acc((SWA_HEADS, 1, LANES))],
        out_shape=[_sds((SWA_PAIRS, t, LANES), jnp.float32), _sds((SWA_KV, t, LANES), jnp.float32),
                   _sds((SWA_KV, t, LANES), jnp.float32), _sds((SWA_HEADS, 1, LANES), jnp.float32)],
        compiler_params=_cparams(("arbitrary",)),
    )(q, k, k, v, v, sinks, do, lse, dsum, q, do, lse, dsum)


MIX_SLABS = 4
MIX_WIDTH = MIX_SLABS * LANES


def _mix_out_fwd(x, oa, ob, ga, gb, wo_a, wo_b):
    t, d = x.shape
    tb = min(_TB, t)

    def body(x_ref, oa_ref, ob_ref, ga_ref, gb_ref, woa_ref, wob_ref, xo_ref):
        y = x_ref[...]
        for o_ref, g_ref, w_ref in ((oa_ref, ga_ref, woa_ref), (ob_ref, gb_ref, wob_ref)):
            r = _rsq(sum(_sumsq(o_ref[h]) for h in range(MIX_SLABS)), MIX_WIDTH)
            for h in range(MIX_SLABS):
                y = y + _dot(_mx(o_ref[h] * r * g_ref[h]), w_ref[h])
        xo_ref[...] = y

    slab = _heads_rows(MIX_SLABS, tb, LANES)
    return pl.pallas_call(
        body, name="mix_out_fwd", grid=(t // tb,),
        in_specs=[_rows(tb, d), slab, slab, _const(ga.shape), _const(gb.shape), _const(wo_a.shape), _const(wo_b.shape)],
        out_specs=_rows(tb, d),
        out_shape=_sds((t, d), jnp.float32),
        compiler_params=_cparams(("parallel",)),
    )(x, oa, ob, ga, gb, wo_a, wo_b)


def _mix_out_bwd(dx, oa, ob, ga, gb, wo_a, wo_b):
    t, d = dx.shape
    tb = min(_TB, t)

    def group(o_ref, g_ref, w_ref, dyb, do_ref, n_ref, dg_ref):
        r = _rsq(sum(_sumsq(o_ref[h]) for h in range(MIX_SLABS)), MIX_WIDTH)
        xh, dn = [], []
        for h in range(MIX_SLABS):
            xh.append(o_ref[h] * r)
            n_ref[h] = _mx(xh[h] * g_ref[h])
            dm = _dot_nt(dyb, w_ref[h])
            dg_ref[h] += _colsum(dm * xh[h])
            dn.append(dm * g_ref[h])
        c = sum(_rowsum(dn[h] * xh[h]) for h in range(MIX_SLABS)) * (1.0 / MIX_WIDTH)
        prods = []
        for h in range(MIX_SLABS):
            do = r * (dn[h] - xh[h] * c)
            do_ref[h] = do.astype(do_ref.dtype)
            prods.append(do * o_ref[h])
        return prods

    def body(dx_ref, oa_ref, ob_ref, ga_ref, gb_ref, woa_ref, wob_ref,
             doa_ref, dsa_ref, dob_ref, dsb_ref, na_ref, nb_ref, dy_ref, dga_ref, dgb_ref):
        @pl.when(pl.program_id(0) == 0)
        def _():
            dga_ref[...] = jnp.zeros_like(dga_ref)
            dgb_ref[...] = jnp.zeros_like(dgb_ref)

        dyb = _mx(dx_ref[...])
        dy_ref[...] = dyb
        for h, pr in enumerate(group(oa_ref, ga_ref, woa_ref, dyb, doa_ref, na_ref, dga_ref)):
            dsa_ref[h] = jnp.broadcast_to(_rowsum(pr), (tb, LANES))
        low = _low_half()
        for j, pr in enumerate(group(ob_ref, gb_ref, wob_ref, dyb, dob_ref, nb_ref, dgb_ref)):
            dsb_ref[2 * j] = jnp.broadcast_to(_rowsum(jnp.where(low, pr, 0.0)), (tb, LANES))
            dsb_ref[2 * j + 1] = jnp.broadcast_to(_rowsum(jnp.where(low, 0.0, pr)), (tb, LANES))

    slab = _heads_rows(MIX_SLABS, tb, LANES)
    return pl.pallas_call(
        body, name="mix_out_bwd", grid=(t // tb,),
        in_specs=[_rows(tb, d), slab, slab, _const(ga.shape), _const(gb.shape), _const(wo_a.shape), _const(wo_b.shape)],
        out_specs=[slab, slab, slab, _heads_rows(SWA_HEADS, tb, LANES), slab, slab, _rows(tb, d),
                   _acc(ga.shape), _acc(gb.shape)],
        out_shape=[_sds((MIX_SLABS, t, LANES), _MXU), _sds((MIX_SLABS, t, LANES), jnp.float32),
                   _sds((MIX_SLABS, t, LANES), jnp.float32), _sds((SWA_HEADS, t, LANES), jnp.float32),
                   _sds((MIX_SLABS, t, LANES), _MXU), _sds((MIX_SLABS, t, LANES), _MXU), _sds((t, d), _MXU),
                   _sds(ga.shape, jnp.float32), _sds(gb.shape, jnp.float32)],
        compiler_params=_cparams(("arbitrary",)),
    )(dx, oa, ob, ga, gb, wo_a, wo_b)


def _loss_head(y, target):
    t, d = y.shape
    tb = min(_TB, t)

    def body(y_ref, t_ref, dy_ref, acc_ref):
        @pl.when(pl.program_id(0) == 0)
        def _():
            acc_ref[...] = jnp.zeros_like(acc_ref)

        err = y_ref[...] - t_ref[...]
        dy_ref[...] = err * (1.0 / d)
        acc_ref[...] += jnp.sum(err * err)

    return pl.pallas_call(
        body, name="loss_head", grid=(t // tb,),
        in_specs=[_rows(tb, d), _rows(tb, d)],
        out_specs=[_rows(tb, d), _acc((8, LANES))],
        out_shape=[_sds((t, d), jnp.float32), _sds((8, LANES), jnp.float32)],
        compiler_params=_cparams(("arbitrary",)),
    )(y, target)


def _is_transposed(name):
    return name not in ROW_SHARDED


def _pack_layer(shards, l, width):
    rows = [(shards[n][l].T if _is_transposed(n) else shards[n][l]).reshape(-1, width) for n in BIG]
    return jnp.concatenate(rows, axis=0)


def _full_shape(like, name):
    _, a, b = like[name].shape
    return (N_DEV * b, a) if _is_transposed(name) else (N_DEV * a, b)


def _unpack_full(gathered, like):
    out, off = {}, 0
    for n in BIG:
        rows_n = like[n][0].size // gathered.shape[-1]
        out[n] = gathered[:, off:off + rows_n].reshape(_full_shape(like, n))
        off += rows_n
    return out


def _pack_by_destination(grads, width):
    return jnp.concatenate([grads[n].reshape(N_DEV, -1, width) for n in BIG], axis=1)


def _unpack_shards(packed, like):
    out, off = {}, 0
    for n in BIG:
        _, a, b = like[n].shape
        rows_n = a * b // packed.shape[-1]
        seg = packed[off:off + rows_n]
        out[n] = seg.reshape(b, a).T if _is_transposed(n) else seg.reshape(a, b)
        off += rows_n
    return out


def _small_rows(n_elems):
    return -(-n_elems // LANES)


def _pack_small(arrays):
    parts = []
    for n in SMALL:
        v = arrays[n]
        depth, width = v.shape
        padded = _small_rows(width) * LANES
        parts.append(jnp.pad(v, ((0, 0), (0, padded - width))).reshape(-1, LANES))
    packed = jnp.concatenate(parts, axis=0)
    return jnp.pad(packed, ((0, (-packed.shape[0]) % 8), (0, 0)))


def _unpack_small(packed, like):
    out, off = {}, 0
    for n in SMALL:
        depth, width = like[n].shape
        rows_n = _small_rows(width)
        seg = packed[off:off + depth * rows_n].reshape(depth, rows_n * LANES)
        out[n] = seg[:, :width]
        off += depth * rows_n
    return out


def _rope_tables(t):
    pos = jnp.arange(t, dtype=jnp.float32)
    inv = 1.0 / (ROPE_THETA ** (jnp.arange(0, MLA_ROPE, 2, dtype=jnp.float32) / MLA_ROPE))
    ang = pos[:, None] * inv[None, :]
    cos, sin = jnp.cos(ang), jnp.sin(ang)
    return jnp.concatenate([cos, cos, cos, cos], axis=1), jnp.concatenate([-sin, sin, -sin, sin], axis=1)


def _pad_lanes(a, width):
    return jnp.pad(a, [(0, 0)] * (a.ndim - 1) + [(0, width - a.shape[-1])])


def _layer_params(full, small, l):
    w_in = full["w_in"]
    d = w_in.shape[1]
    mla_rows = W_IN_COLS[0]
    w_in_p = jnp.concatenate([w_in[:mla_rows], jnp.zeros((LANES - MLA_ROPE, d), w_in.dtype), w_in[mla_rows:]], axis=0)
    wqb = full["mla_w_q_b"].reshape(MLA_HEADS, MLA_QK, MLA_Q_RANK)
    wqb = jnp.pad(wqb, ((0, 0), (0, MLA_QK_PAD - MLA_QK), (0, 0)))
    row = lambda name: small[name][l][None, :]
    twice = lambda g: jnp.concatenate([g, g], axis=1)
    prep = {
        "mix_g": row("mix_norm"), "w_in": w_in_p,
        "g_qa": row("mla_q_a_norm"), "wqb": wqb,
        "g_kva": row("mla_kv_a_norm"), "w_kvb": full["mla_w_kv_b"],
        "gq_n": row("mla_q_norm")[:, :MLA_NOPE], "gq_r": _pad_lanes(row("mla_q_norm")[:, MLA_NOPE:], LANES),
        "gk_n": row("mla_k_norm")[:, :MLA_NOPE], "gk_r": _pad_lanes(row("mla_k_norm")[:, MLA_NOPE:], LANES),
        "g_sq": twice(row("swa_q_norm")), "g_sk": twice(row("swa_k_norm")),
    }
    return {
        "prep": prep,
        "ffn1": (row("ffn1_norm"), full["ffn1_w_gate"], full["ffn1_w_up"], full["ffn1_w_down"]),
        "ffn2": (row("ffn2_norm"), full["ffn2_w_gate"], full["ffn2_w_up"], full["ffn2_w_down"]),
        "sinks": jnp.broadcast_to(small["swa_sinks"][l][:, None, None], (SWA_HEADS, 1, LANES)),
        "ga": small["mla_out_norm"][l].reshape(MIX_SLABS, 1, LANES),
        "gb": small["swa_out_norm"][l].reshape(MIX_SLABS, 1, LANES),
        "wo_a": full["w_o"][:MIX_WIDTH].reshape(MIX_SLABS, LANES, d),
        "wo_b": full["w_o"][MIX_WIDTH:].reshape(MIX_SLABS, LANES, d),
    }


def _ffn_backward(x_in, dxo, a, u, params, tag):
    gain, wg, wu, wd = params
    dxi, da, du, s, h, dy, dg = _ffn_dgrad(x_in, gain, dxo, a, u, wg, wu, wd)
    dwg = _tn_matmul(da[None], h, "wgrad_" + tag + "_gate")[0]
    dwu = _tn_matmul(du[None], h, "wgrad_" + tag + "_up")[0]
    dwd = _tn_matmul(s[None], dy, "wgrad_" + tag + "_down")[0]
    return dxi, dg[0], dwg, dwu, dwd


def kernel(x, ffn1_norm, ffn1_w_gate, ffn1_w_up, ffn1_w_down, mix_norm, w_in, mla_q_a_norm, mla_w_q_b, mla_kv_a_norm, mla_w_kv_b, mla_q_norm, mla_k_norm, swa_q_norm, swa_k_norm, swa_sinks, mla_out_norm, swa_out_norm, w_o, ffn2_norm, ffn2_w_gate, ffn2_w_up, ffn2_w_down, loss_target, m_ffn1_norm, m_ffn1_w_gate, m_ffn1_w_up, m_ffn1_w_down, m_mix_norm, m_w_in, m_mla_q_a_norm, m_mla_w_q_b, m_mla_kv_a_norm, m_mla_w_kv_b, m_mla_q_norm, m_mla_k_norm, m_swa_q_norm, m_swa_k_norm, m_swa_sinks, m_mla_out_norm, m_swa_out_norm, m_w_o, m_ffn2_norm, m_ffn2_w_gate, m_ffn2_w_up, m_ffn2_w_down, v_ffn1_norm, v_ffn1_w_gate, v_ffn1_w_up, v_ffn1_w_down, v_mix_norm, v_w_in, v_mla_q_a_norm, v_mla_w_q_b, v_mla_kv_a_norm, v_mla_w_kv_b, v_mla_q_norm, v_mla_k_norm, v_swa_q_norm, v_swa_k_norm, v_swa_sinks, v_mla_out_norm, v_swa_out_norm, v_w_o, v_ffn2_norm, v_ffn2_w_gate, v_ffn2_w_up, v_ffn2_w_down):
    local = dict(locals())
    w = {n: local[n] for n in WEIGHTS}
    m = {n: local["m_" + n] for n in WEIGHTS}
    v = {n: local["v_" + n] for n in WEIGHTS}
    depth = ffn1_norm.shape[0]
    t, d = x.shape[-2], x.shape[-1]
    x2d = x.reshape(t, d)
    target = loss_target.reshape(t, d)
    bq = min(_BQ, t)

    big = {n: w[n] for n in BIG}
    full = [_unpack_full(_all_gather(_mx(_pack_layer(big, l, d))), big) for l in range(depth)]
    cos, sin_s = _rope_tables(t)
    params = [_layer_params(full[l], w, l) for l in range(depth)]
    x_i, y_i, c_i = _position()
    dest_idx = jnp.stack([4 * px + 2 * py + c_i for px, py in _relations(x_i, y_i)]).astype(jnp.int32)

    saved = []
    xc = x2d
    for l in range(depth):
        pr = params[l]
        x0 = xc
        x1, a1, u1 = _ffn_fwd(x0, *pr["ffn1"])
        qa, ka, va, qb, kb, vb = _prep_fwd(x1, cos, sin_s, pr["prep"])
        oa, lse_a = _mla_fwd(qa, ka, va)
        ob, lse_b = _swa_fwd(qb, kb, vb, pr["sinks"])
        x2 = _mix_out_fwd(x1, oa, ob, pr["ga"], pr["gb"], pr["wo_a"], pr["wo_b"])
        x3, a2, u2 = _ffn_fwd(x2, *pr["ffn2"])
        saved.append((x0, a1, u1, x1, qa, ka, va, qb, kb, vb, oa, lse_a, ob, lse_b, x2, a2, u2))
        xc = x3

    dx, sq_err = _loss_head(xc, target)
    loss = lax.psum(0.5 / d * sq_err[0, 0], MESH_AXES)

    grad_shards = [None] * depth
    small_grads = {n: [None] * depth for n in SMALL}
    for l in reversed(range(depth)):
        pr = params[l]
        x0, a1, u1, x1, qa, ka, va, qb, kb, vb, oa, lse_a, ob, lse_b, x2, a2, u2 = saved[l]
        g = {}
        dx, small_grads["ffn2_norm"][l], g["ffn2_w_gate"], g["ffn2_w_up"], g["ffn2_w_down"] = _ffn_backward(
            x2, dx, a2, u2, pr["ffn2"], "ffn2")

        doa, dsum_a, dob, dsum_b, na, nb_, dyb, dga, dgb = _mix_out_bwd(
            dx, oa, ob, pr["ga"], pr["gb"], pr["wo_a"], pr["wo_b"])
        small_grads["mla_out_norm"][l] = dga.reshape(-1)
        small_grads["swa_out_norm"][l] = dgb.reshape(-1)
        g["w_o"] = jnp.concatenate([_tn_matmul(na, dyb, "wgrad_wo_a").reshape(-1, d),
                                    _tn_matmul(nb_, dyb, "wgrad_wo_b").reshape(-1, d)], axis=0)

        rows_of = lambda s: s[:, :, 0].reshape(MLA_HEADS, t // bq, 1, bq)
        dqa, dka, dva = _mla_bwd(qa, ka, va, doa, rows_of(lse_a), rows_of(dsum_a))
        dqb, dkb, dvb, dsinks = _swa_bwd(qb, kb, vb, pr["sinks"], dob, lse_b, dsum_b)
        small_grads["swa_sinks"][l] = dsinks[:, 0, 0]

        outs = _prep_bwd(x1, dx, cos, sin_s, pr["prep"], dqa, dka, dva, dqb, dkb, dvb)
        dx = outs[0]
        pg = dict(zip(PREP_WEIGHTS, outs[1:]))
        g["w_in"] = jnp.concatenate([pg["w_in"][:W_IN_COLS[0]], pg["w_in"][C_QS:]], axis=0)
        g["mla_w_q_b"] = pg["wqb"][:, :MLA_QK].reshape(MLA_HEADS * MLA_QK, MLA_Q_RANK)
        g["mla_w_kv_b"] = pg["w_kvb"]
        fold = lambda gg: gg[0, :HALF] + gg[0, HALF:]
        small_grads["mix_norm"][l] = pg["mix_g"][0]
        small_grads["mla_q_a_norm"][l] = pg["g_qa"][0]
        small_grads["mla_kv_a_norm"][l] = pg["g_kva"][0]
        small_grads["mla_q_norm"][l] = jnp.concatenate([pg["gq_n"][0], pg["gq_r"][0, :MLA_ROPE]])
        small_grads["mla_k_norm"][l] = jnp.concatenate([pg["gk_n"][0], pg["gk_r"][0, :MLA_ROPE]])
        small_grads["swa_q_norm"][l] = fold(pg["g_sq"])
        small_grads["swa_k_norm"][l] = fold(pg["g_sk"])

        dx, small_grads["ffn1_norm"][l], g["ffn1_w_gate"], g["ffn1_w_up"], g["ffn1_w_down"] = _ffn_backward(
            x0, dx, a1, u1, pr["ffn1"], "ffn1")

        by_dest = _pack_by_destination(g, d)
        own, wire = _rs_chip_sums(by_dest, _rs_sibling_exchange(by_dest), dest_idx)
        grad_shards[l] = _unpack_shards(_rs_sum(own, _rs_chip_exchange(wire)), big)

    grad_big, delta_big, new_m_big, new_v_big = {}, {}, {}, {}
    for n in BIG:
        grad_big[n] = jnp.stack([grad_shards[l][n] for l in range(depth)])
        delta_big[n], new_m_big[n], new_v_big[n] = _adamw(grad_big[n], w[n], m[n], v[n], n)

    small_partial = _pack_small({n: jnp.stack(small_grads[n]) for n in SMALL})
    g_s = _all_reduce_small(small_partial)
    d_s, m_s, v_s = _adamw_small(g_s, _pack_small(w), _pack_small(m), _pack_small(v))
    grad_small, delta_small, new_m_small, new_v_small = (_unpack_small(a, w) for a in (g_s, d_s, m_s, v_s))

    def ordered(big, small):
        return [big[n] if n in big else small[n] for n in WEIGHTS]

    return (loss, dx.reshape(x.shape), *ordered(grad_big, grad_small), *ordered(delta_big, delta_small),
            *ordered(new_m_big, new_m_small), *ordered(new_v_big, new_v_small))
```

```python
import jax
import jax.numpy as jnp
from jax import lax
from jax.experimental import pallas as pl
from jax.experimental.pallas import tpu as pltpu

N_DEV = 8
EPS = 1e-6
ROPE_THETA = 10000.0
MLA_HEADS = 4
MLA_Q_RANK = 256
MLA_KV_RANK = 128
MLA_NOPE = 128
MLA_ROPE = 64
MLA_V = 128
MLA_QK = MLA_NOPE + MLA_ROPE
MLA_QK_PAD = 256
SWA_HEADS = 8
SWA_KV = 2
SWA_GROUP = SWA_HEADS // SWA_KV
SWA_D = 64
SWA_BLOCK = 128
ADAM_LR = 0.001
ADAM_B1 = 0.9
ADAM_B2 = 0.999
ADAM_EPS = 1e-08
ADAM_WD = 0.01
ADAM_STEP = 10

LANES = 128
HALF = LANES // 2
SWA_PAIRS = SWA_HEADS // 2
W_IN_COLS = (MLA_Q_RANK + MLA_KV_RANK + MLA_ROPE, SWA_HEADS * SWA_D + 2 * SWA_KV * SWA_D)
VMEM_LIMIT = 56 * 1024 * 1024

_MXU = jnp.bfloat16
_TB = 256
_BQ = 512
_STRIP = 32
_TK = 512
_STREAMS = 4
_SWA_STEP = 4
RS_ROW_BLOCKS = 8

BIG = ("ffn1_w_gate", "ffn1_w_up", "ffn1_w_down", "w_in", "mla_w_q_b", "mla_w_kv_b", "w_o",
       "ffn2_w_gate", "ffn2_w_up", "ffn2_w_down")
ROW_SHARDED = ("ffn1_w_down", "w_o", "ffn2_w_down")
SMALL = ("ffn1_norm", "mix_norm", "mla_q_a_norm", "mla_kv_a_norm", "mla_q_norm", "mla_k_norm",
         "swa_q_norm", "swa_k_norm", "swa_sinks", "mla_out_norm", "swa_out_norm", "ffn2_norm")
WEIGHTS = ("ffn1_norm", "ffn1_w_gate", "ffn1_w_up", "ffn1_w_down", "mix_norm", "w_in", "mla_q_a_norm",
           "mla_w_q_b", "mla_kv_a_norm", "mla_w_kv_b", "mla_q_norm", "mla_k_norm", "swa_q_norm",
           "swa_k_norm", "swa_sinks", "mla_out_norm", "swa_out_norm", "w_o", "ffn2_norm",
           "ffn2_w_gate", "ffn2_w_up", "ffn2_w_down")
MESH_AXES = ("x", "y", "c")
MESH = pl.DeviceIdType.MESH
NEG = -1e30


def _f32(t):
    return t.astype(jnp.float32)


def _mx(t):
    return t.astype(_MXU)


def _dot(a, b):
    return jnp.dot(a, b, preferred_element_type=jnp.float32)


def _dot_nt(a, b):
    return lax.dot_general(a, b, (((1,), (1,)), ((), ())), preferred_element_type=jnp.float32)


def _dot_tn(a, b):
    return lax.dot_general(a, b, (((0,), (0,)), ((), ())), preferred_element_type=jnp.float32)


def _rsq(ss, n):
    return lax.rsqrt(ss * (1.0 / n) + EPS)


def _sumsq(t):
    return jnp.sum(t * t, axis=-1, keepdims=True)


def _rowsum(t):
    return jnp.sum(t, axis=-1, keepdims=True)


def _rowmax(t):
    return jnp.max(t, axis=-1, keepdims=True)


def _colsum(t):
    return jnp.sum(t, axis=0, keepdims=True)


def _lane():
    return lax.broadcasted_iota(jnp.int32, (1, LANES), 1)


def _low_half():
    return _lane() < HALF


def _swap32(t):
    return jnp.where((_lane() & 32) == 0, pltpu.roll(t, 96, 1), pltpu.roll(t, 32, 1))


def _rope(t, cos, sin_signed):
    return t * cos + _swap32(t) * sin_signed


def _rope_bwd(d, cos, sin_signed):
    return d * cos + _swap32(d * sin_signed)


def _half_sums(t):
    low = _low_half()
    return jnp.where(low, _rowsum(jnp.where(low, t, 0.0)), _rowsum(jnp.where(low, 0.0, t)))


def _dup_halves(pair):
    low = _low_half()
    swapped = pltpu.roll(pair, HALF, 1)
    return jnp.where(low, pair, swapped), jnp.where(low, swapped, pair)


def _undup_halves(d0, d1):
    return jnp.where(_low_half(), d0 + pltpu.roll(d0, HALF, 1), d1 + pltpu.roll(d1, HALF, 1))


def _pick_halves(a, b):
    return jnp.where(_low_half(), a, b)


def _norm_bwd(dn_list, xh_list, r, n):
    c = sum(_rowsum(dn * xh) for dn, xh in zip(dn_list, xh_list)) * (1.0 / n)
    return [r * (dn - xh * c) for dn, xh in zip(dn_list, xh_list)]


def _cparams(semantics):
    return pltpu.CompilerParams(dimension_semantics=semantics, vmem_limit_bytes=VMEM_LIMIT)


def _const(shape):
    nd = len(shape)
    return pl.BlockSpec(shape, lambda *_: (0,) * nd, pipeline_mode=pl.Buffered(1))


def _acc(shape):
    nd = len(shape)
    return pl.BlockSpec(shape, lambda *_: (0,) * nd)


def _rows(tb, width):
    return pl.BlockSpec((tb, width), lambda i: (i, 0))


def _heads_rows(h, tb, width):
    return pl.BlockSpec((h, tb, width), lambda i: (0, i, 0))


def _sds(shape, dtype):
    return jax.ShapeDtypeStruct(shape, dtype)


def _position():
    return lax.axis_index("x"), lax.axis_index("y"), lax.axis_index("c")


def _all_gather(xp):
    def body(x_ref, out_ref, send_sems, recv_sems, local_sem):
        x, y, c = _position()
        me, sibling = (x, y, c), (x, y, 1 - c)
        chips = [(1 - x, y), (x, 1 - y), (1 - x, 1 - y)]

        def rows(px, py, pc):
            return out_ref.at[4 * px + 2 * py + pc]

        def copy(k, block, to, src=None):
            return pltpu.make_async_remote_copy(
                src_ref=rows(*block) if src is None else src, dst_ref=rows(*block),
                send_sem=send_sems.at[k], recv_sem=recv_sems.at[k], device_id=to, device_id_type=MESH)

        mine = pltpu.make_async_copy(x_ref, rows(*me), local_sem)
        mine.start()
        first = [copy(0, me, sibling, src=x_ref)]
        first += [copy(1 + j, me, (*chip, c), src=x_ref) for j, chip in enumerate(chips)]
        for cp in first:
            cp.start()
        passed = [copy(4 + j, (*chip, c), sibling) for j, chip in enumerate(chips)]
        for j, chip in enumerate(chips):
            copy(1 + j, (*chip, c), me).wait_recv()
            passed[j].start()
        copy(0, sibling, me).wait_recv()
        for j, chip in enumerate(chips):
            copy(4 + j, (*chip, 1 - c), me).wait_recv()
        for cp in first + passed:
            cp.wait_send()
        mine.wait()

    return pl.pallas_call(
        body, name="ag_weights",
        out_shape=_sds((N_DEV,) + xp.shape, xp.dtype),
        in_specs=[pl.BlockSpec(memory_space=pl.ANY)],
        out_specs=pl.BlockSpec(memory_space=pl.ANY),
        scratch_shapes=[pltpu.SemaphoreType.DMA((7,)), pltpu.SemaphoreType.DMA((7,)), pltpu.SemaphoreType.DMA],
    )(xp)


def _relations(x, y):
    return [(x, y), (1 - x, y), (x, 1 - y), (1 - x, 1 - y)]


def _rs_sibling_exchange(g):
    def body(g_ref, out_ref, send_sems, recv_sems):
        x, y, c = _position()
        copies = []
        for k, (px, py) in enumerate(_relations(x, y)):
            copies.append(pltpu.make_async_remote_copy(
                src_ref=g_ref.at[4 * px + 2 * py + (1 - c)], dst_ref=out_ref.at[k],
                send_sem=send_sems.at[k], recv_sem=recv_sems.at[k], device_id=(x, y, 1 - c), device_id_type=MESH))
        for cp in copies:
            cp.start()
        for cp in copies:
            cp.wait()

    return pl.pallas_call(
        body, name="rs_sibling_exchange",
        out_shape=_sds((4,) + g.shape[1:], g.dtype),
        in_specs=[pl.BlockSpec(memory_space=pl.ANY)],
        out_specs=pl.BlockSpec(memory_space=pl.ANY),
        scratch_shapes=[pltpu.SemaphoreType.DMA((4,)), pltpu.SemaphoreType.DMA((4,))],
    )(g)


def _rs_chip_sums(g, sib, dest_idx):
    _, rows_n, width = g.shape
    rb = rows_n // RS_ROW_BLOCKS

    def body(idx_ref, g_ref, s_ref, own_ref, wire_ref):
        k = pl.program_id(1)
        total = g_ref[0] + s_ref[0]
        wire_ref[0] = total.astype(wire_ref.dtype)

        @pl.when(k == 0)
        def _():
            own_ref[...] = total

    return pl.pallas_call(
        body, name="rs_chip_sums",
        grid_spec=pltpu.PrefetchScalarGridSpec(
            num_scalar_prefetch=1, grid=(rows_n // rb, 4),
            in_specs=[pl.BlockSpec((1, rb, width), lambda r, k, idx: (idx[k], r, 0)),
                      pl.BlockSpec((1, rb, width), lambda r, k, idx: (k, r, 0))],
            out_specs=[pl.BlockSpec((rb, width), lambda r, k, idx: (r, 0)),
                       pl.BlockSpec((1, rb, width), lambda r, k, idx: (k, r, 0))]),
        out_shape=[_sds((rows_n, width), jnp.float32), _sds((4, rows_n, width), _MXU)],
        compiler_params=_cparams(("parallel", "arbitrary")),
    )(dest_idx, g, sib)


def _rs_chip_exchange(wire):
    def body(w_ref, out_ref, send_sems, recv_sems):
        x, y, c = _position()
        copies = []
        for k, (px, py) in enumerate(_relations(x, y)[1:]):
            copies.append(pltpu.make_async_remote_copy(
                src_ref=w_ref.at[k + 1], dst_ref=out_ref.at[k],
                send_sem=send_sems.at[k], recv_sem=recv_sems.at[k], device_id=(px, py, c), device_id_type=MESH))
        for cp in copies:
            cp.start()
        for cp in copies:
            cp.wait()

    return pl.pallas_call(
        body, name="rs_chip_exchange",
        out_shape=_sds((3,) + wire.shape[1:], wire.dtype),
        in_specs=[pl.BlockSpec(memory_space=pl.ANY)],
        out_specs=pl.BlockSpec(memory_space=pl.ANY),
        scratch_shapes=[pltpu.SemaphoreType.DMA((3,)), pltpu.SemaphoreType.DMA((3,))],
    )(wire)


def _remote(src, dst, send_sem, recv_sem, device):
    return pltpu.make_async_remote_copy(src_ref=src, dst_ref=dst, send_sem=send_sem, recv_sem=recv_sem,
                                        device_id=device, device_id_type=MESH)


def _side_gather(xp):
    def make(ins, outs, send, recv, local):
        (x_ref,), (out_ref,) = ins, outs
        x, y, c = _position()
        me = 4 * x + 2 * y + c
        copies = [pltpu.make_async_copy(x_ref, out_ref.at[me], local.at[0])]
        for k in range(1, N_DEV):
            peer = (1 - x if k & 4 else x, 1 - y if k & 2 else y, 1 - c if k & 1 else c)
            copies.append(_remote(x_ref, out_ref.at[me], send.at[k - 1], recv.at[k - 1], peer))
        return copies

    return [xp], [_sds((N_DEV,) + xp.shape, xp.dtype)], N_DEV - 1, make


def _side_sibling(g):
    def make(ins, outs, send, recv, local):
        (g_ref,), (out_ref,) = ins, outs
        x, y, c = _position()
        return [_remote(g_ref.at[4 * px + 2 * py + (1 - c)], out_ref.at[k], send.at[k], recv.at[k], (x, y, 1 - c))
                for k, (px, py) in enumerate(_relations(x, y))]

    return [g], [_sds((4,) + g.shape[1:], g.dtype)], 4, make


def _side_chips(wire):
    def make(ins, outs, send, recv, local):
        (w_ref,), (out_ref,) = ins, outs
        x, y, c = _position()
        return [_remote(w_ref.at[k + 1], out_ref.at[k], send.at[k], recv.at[k], (px, py, c))
                for k, (px, py) in enumerate(_relations(x, y)[1:])]

    return [wire], [_sds((3,) + wire.shape[1:], wire.dtype)], 3, make


def _call(body, args, side, *, name, grid, in_specs, out_specs, out_shape, scratch_shapes=(), semantics):
    in_specs, out_specs, out_shape = list(in_specs), list(out_specs), list(out_shape)
    n_in, n_out, n_scr = len(in_specs), len(out_specs), len(scratch_shapes)
    if side is None:
        outs = pl.pallas_call(body, name=name, grid=grid, in_specs=in_specs, out_specs=out_specs, out_shape=out_shape,
                              scratch_shapes=list(scratch_shapes), compiler_params=_cparams(semantics))(*args)
        return list(outs), []
    arrays, shapes, n_remote, make = side
    n_side_in, n_side_out = len(arrays), len(shapes)
    hbm = pl.BlockSpec(memory_space=pl.ANY)

    def with_copies(*refs):
        main_in, refs = refs[:n_in], refs[n_in:]
        side_in, refs = refs[:n_side_in], refs[n_side_in:]
        main_out, refs = refs[:n_out], refs[n_out:]
        side_out, refs = refs[:n_side_out], refs[n_side_out:]
        main_scr, (send, recv, local) = refs[:n_scr], refs[n_scr:]
        copies = make(side_in, side_out, send, recv, local)
        ids = [pl.program_id(a) for a in range(len(grid))]
        first, last = ids[0] == 0, ids[0] == grid[0] - 1
        for i, size in zip(ids[1:], grid[1:]):
            first, last = jnp.logical_and(first, i == 0), jnp.logical_and(last, i == size - 1)

        @pl.when(first)
        def _():
            for cp in copies:
                cp.start()

        body(*main_in, *main_out, *main_scr)

        @pl.when(last)
        def _():
            for cp in copies:
                cp.wait()

    dma = pltpu.SemaphoreType.DMA
    outs = pl.pallas_call(
        with_copies, name=name, grid=grid, in_specs=in_specs + [hbm] * n_side_in,
        out_specs=out_specs + [hbm] * n_side_out, out_shape=out_shape + list(shapes),
        scratch_shapes=list(scratch_shapes) + [dma((n_remote,)), dma((n_remote,)), dma((1,))],
        compiler_params=_cparams(("arbitrary",) * len(grid)),
    )(*args, *arrays)
    return list(outs[:n_out]), list(outs[n_out:])


def _all_reduce_small(v):
    rows_n = v.shape[0]

    def body(v_ref, out_ref, buf, send_sems, recv_sems):
        x, y, c = _position()
        me = 4 * x + 2 * y + c
        buf[me] = v_ref[...]
        copies = []
        for k in range(1, N_DEV):
            px = 1 - x if k & 4 else x
            py = 1 - y if k & 2 else y
            pc = 1 - c if k & 1 else c
            copies.append(pltpu.make_async_remote_copy(
                src_ref=v_ref, dst_ref=buf.at[me],
                send_sem=send_sems.at[k - 1], recv_sem=recv_sems.at[k - 1], device_id=(px, py, pc), device_id_type=MESH))
        for cp in copies:
            cp.start()
        for cp in copies:
            cp.wait()
        total = buf[0]
        for d in range(1, N_DEV):
            total = total + buf[d]
        out_ref[...] = total

    return pl.pallas_call(
        body, name="ar_small",
        out_shape=_sds((rows_n, LANES), jnp.float32),
        in_specs=[pl.BlockSpec(memory_space=pltpu.VMEM)],
        out_specs=pl.BlockSpec(memory_space=pltpu.VMEM),
        scratch_shapes=[pltpu.VMEM((N_DEV, rows_n, LANES), jnp.float32),
                        pltpu.SemaphoreType.DMA((N_DEV - 1,)), pltpu.SemaphoreType.DMA((N_DEV - 1,))],
    )(v)


def _adamw_math(w, g, m, v):
    m = ADAM_B1 * m + (1.0 - ADAM_B1) * g
    v = ADAM_B2 * v + (1.0 - ADAM_B2) * (g * g)
    m_hat = m / (1.0 - ADAM_B1 ** ADAM_STEP)
    v_hat = v / (1.0 - ADAM_B2 ** ADAM_STEP)
    delta = -ADAM_LR * (m_hat / (jnp.sqrt(v_hat) + ADAM_EPS) + ADAM_WD * w)
    return delta, m, v


def _rs_sum(own, recv):
    rows_n, width = own.shape
    rb = rows_n // RS_ROW_BLOCKS

    def body(own_ref, r0, r1, r2, g_out):
        g_out[...] = ((own_ref[...] + _f32(r0[0])) + _f32(r1[0])) + _f32(r2[0])

    row = pl.BlockSpec((rb, width), lambda r: (r, 0))

    def slot(k):
        return pl.BlockSpec((1, rb, width), lambda r: (k, r, 0))

    return pl.pallas_call(
        body, name="rs_sum", grid=(RS_ROW_BLOCKS,),
        in_specs=[row, slot(0), slot(1), slot(2)], out_specs=row,
        out_shape=_sds((rows_n, width), jnp.float32),
        compiler_params=_cparams(("parallel",)),
    )(own, recv, recv, recv)


def _adamw(g, w, m, v, name):
    depth, a, b = w.shape

    def body(g_ref, w_ref, m_ref, v_ref, d_out, m_out, v_out):
        delta, m2, v2 = _adamw_math(w_ref[...], g_ref[...], m_ref[...], v_ref[...])
        d_out[...] = delta
        m_out[...] = m2
        v_out[...] = v2

    layer = pl.BlockSpec((1, a, b), lambda l: (l, 0, 0))
    return pl.pallas_call(
        body, name="adamw_" + name, grid=(depth,),
        in_specs=[layer] * 4, out_specs=[layer] * 3,
        out_shape=[_sds(w.shape, jnp.float32)] * 3,
        compiler_params=_cparams(("parallel",)),
    )(g, w, m, v)


def _adamw_small(g, w, m, v):
    def body(g_ref, w_ref, m_ref, v_ref, d_out, m_out, v_out):
        delta, m2, v2 = _adamw_math(w_ref[...], g_ref[...], m_ref[...], v_ref[...])
        d_out[...] = delta
        m_out[...] = m2
        v_out[...] = v2

    vm = pl.BlockSpec(memory_space=pltpu.VMEM)
    return pl.pallas_call(
        body, name="adamw_small",
        in_specs=[vm] * 4, out_specs=[vm] * 3,
        out_shape=[_sds(g.shape, jnp.float32)] * 3,
    )(g, w, m, v)


def _f_chunk(f):
    for cand in (1408, 1024, 512, 256, 128):
        if f % cand == 0:
            return cand
    return f


def _ffn_fwd(x, gain, wg, wu, wd):
    t, d = x.shape
    f = wg.shape[0]
    tb = min(_TB, t)
    fc = _f_chunk(f)

    def body(x_ref, g_ref, wg_ref, wu_ref, wd_ref, xo_ref, a_ref, u_ref):
        xv = x_ref[...]
        hb = _mx(xv * _rsq(_sumsq(xv), d) * g_ref[...])
        y = jnp.zeros((tb, d), jnp.float32)
        for c0 in range(0, f, fc):
            a = _dot_nt(hb, wg_ref[c0:c0 + fc, :])
            u = _dot_nt(hb, wu_ref[c0:c0 + fc, :])
            a_ref[:, c0:c0 + fc] = a.astype(a_ref.dtype)
            u_ref[:, c0:c0 + fc] = u.astype(u_ref.dtype)
            s = a * jax.nn.sigmoid(a) * u
            y = y + _dot(_mx(s), wd_ref[c0:c0 + fc, :])
        xo_ref[...] = xv + 0.5 * y

    return pl.pallas_call(
        body, name="ffn_fwd", grid=(t // tb,),
        in_specs=[_rows(tb, d), _const((1, d)), _const((f, d)), _const((f, d)), _const((f, d))],
        out_specs=[_rows(tb, d), _rows(tb, f), _rows(tb, f)],
        out_shape=[_sds((t, d), jnp.float32), _sds((t, f), _MXU), _sds((t, f), _MXU)],
        compiler_params=_cparams(("parallel",)),
    )(x, gain, wg, wu, wd)


def _ffn_dgrad(x, gain, dxo, a, u, wg, wu, wd, side=None):
    t, d = x.shape
    f = wg.shape[0]
    tb = min(_TB, t)
    fc = _f_chunk(f)

    def body(x_ref, g_ref, dxo_ref, a_ref, u_ref, wg_ref, wu_ref, wd_ref,
             dxi_ref, da_ref, du_ref, s_ref, h_ref, dy_ref, dg_ref):
        xv = x_ref[...]
        gv = g_ref[...]
        r = _rsq(_sumsq(xv), d)
        xhat = xv * r
        h_ref[...] = _mx(xhat * gv)
        dxo = dxo_ref[...]
        dyb = _mx(0.5 * dxo)
        dy_ref[...] = dyb
        dh = jnp.zeros((tb, d), jnp.float32)
        for c0 in range(0, f, fc):
            ds = _dot_nt(dyb, wd_ref[c0:c0 + fc, :])
            av = _f32(a_ref[:, c0:c0 + fc])
            uv = _f32(u_ref[:, c0:c0 + fc])
            sig = jax.nn.sigmoid(av)
            silu = av * sig
            da = _mx(ds * uv * (sig * (1.0 + av * (1.0 - sig))))
            du = _mx(ds * silu)
            s_ref[:, c0:c0 + fc] = _mx(silu * uv)
            da_ref[:, c0:c0 + fc] = da
            du_ref[:, c0:c0 + fc] = du
            dh = dh + _dot(da, wg_ref[c0:c0 + fc, :]) + _dot(du, wu_ref[c0:c0 + fc, :])

        @pl.when(pl.program_id(0) == 0)
        def _():
            dg_ref[...] = jnp.zeros_like(dg_ref)

        dg_ref[...] += _colsum(dh * xhat)
        dn = dh * gv
        dxi_ref[...] = dxo + r * (dn - xhat * (_rowsum(dn * xhat) * (1.0 / d)))

    return _call(
        body, (x, gain, dxo, a, u, wg, wu, wd), side, name="ffn_dgrad", grid=(t // tb,),
        in_specs=[_rows(tb, d), _const((1, d)), _rows(tb, d), _rows(tb, f), _rows(tb, f),
                  _const((f, d)), _const((f, d)), _const((f, d))],
        out_specs=[_rows(tb, d), _rows(tb, f), _rows(tb, f), _rows(tb, f), _rows(tb, d), _rows(tb, d),
                   _acc((1, d))],
        out_shape=[_sds((t, d), jnp.float32), _sds((t, f), _MXU), _sds((t, f), _MXU), _sds((t, f), _MXU),
                   _sds((t, d), _MXU), _sds((t, d), _MXU), _sds((1, d), jnp.float32)],
        semantics=("arbitrary",))


def _tn_matmul(a, b, name):
    t, m = a.shape
    n = b.shape[1]
    tk = min(_TK, t)
    ns = _STREAMS if tk % (16 * _STREAMS) == 0 else 1
    part = tk // ns
    tn = n
    while m * tn * 4 > 12 * 1024 * 1024 and tn % 256 == 0:
        tn //= 2

    def body(*refs):
        a_refs, b_refs, o_ref = refs[:ns], refs[ns:2 * ns], refs[2 * ns]

        @pl.when(pl.program_id(1) == 0)
        def _():
            o_ref[...] = jnp.zeros_like(o_ref)

        av = jnp.concatenate([r[...] for r in a_refs], axis=0)
        bv = jnp.concatenate([r[...] for r in b_refs], axis=0)
        o_ref[...] += _dot_tn(av, bv)

    def slab(width, s, col):
        return pl.BlockSpec((part, width), (lambda j, k: (k * ns + s, j)) if col else (lambda j, k: (k * ns + s, 0)))

    return pl.pallas_call(
        body, name=name, grid=(n // tn, t // tk),
        in_specs=[slab(m, s, False) for s in range(ns)] + [slab(tn, s, True) for s in range(ns)],
        out_specs=pl.BlockSpec((m, tn), lambda j, k: (0, j)),
        out_shape=_sds((m, n), jnp.float32),
        compiler_params=_cparams(("parallel", "arbitrary")),
    )(*([a] * ns + [b] * ns))


PREP_WEIGHTS = ("mix_g", "w_in", "g_qa", "wqb", "g_kva", "w_kvb", "gq_n", "gq_r", "gk_n", "gk_r", "g_sq", "g_sk")
C_CQ, C_CKV, C_KPE, C_QS = 0, MLA_Q_RANK, MLA_Q_RANK + MLA_KV_RANK, MLA_Q_RANK + MLA_KV_RANK + LANES
C_KS = C_QS + SWA_HEADS * SWA_D
C_VS = C_KS + LANES
W_IN_PACKED = C_VS + LANES


def _prep_specs(p):
    return [_const(p[n].shape) for n in PREP_WEIGHTS]


def _pair_norm_rope(t, gain, cos, sin_s):
    return _rope(t * _rsq(_half_sums(t * t), SWA_D) * gain, cos, sin_s)


def _prep_fwd(x, cos, sin_s, p):
    t, d = x.shape
    tb = min(_TB, t)

    def body(x_ref, cos_ref, sin_ref, mix_g, w_in, g_qa, wqb, g_kva, w_kvb, gq_n, gq_r, gk_n, gk_r, g_sq, g_sk,
             qa_ref, ka_ref, va_ref, qb_ref, kb_ref, vb_ref):
        xv = x_ref[...]
        cos_v, sin_v = cos_ref[...], sin_ref[...]
        hb = _mx(xv * _rsq(_sumsq(xv), d) * mix_g[...])
        proj = _dot_nt(hb, w_in[...])
        cq = proj[:, C_CQ:C_CKV]
        cqn = _mx(cq * _rsq(_sumsq(cq), MLA_Q_RANK) * g_qa[...])
        for h in range(MLA_HEADS):
            qh = _dot_nt(cqn, wqb[h])
            qn, qr = qh[:, :MLA_NOPE], qh[:, MLA_NOPE:]
            rh = _rsq(_sumsq(qn) + _sumsq(qr), MLA_QK)
            qa_ref[h, :, 0:MLA_NOPE] = (qn * rh * gq_n[...]).astype(qa_ref.dtype)
            qa_ref[h, :, MLA_NOPE:MLA_QK_PAD] = _rope(qr * rh * gq_r[...], cos_v, sin_v).astype(qa_ref.dtype)
        ckv = proj[:, C_CKV:C_KPE]
        ckvn = _mx(ckv * _rsq(_sumsq(ckv), MLA_KV_RANK) * g_kva[...])
        kpe = proj[:, C_KPE:C_QS]
        ss_pe = _sumsq(kpe)
        kv = _dot_nt(ckvn, w_kvb[...])
        for h in range(MLA_HEADS):
            c0 = h * (MLA_NOPE + MLA_V)
            kn = kv[:, c0:c0 + MLA_NOPE]
            rh = _rsq(_sumsq(kn) + ss_pe, MLA_QK)
            ka_ref[h, :, 0:MLA_NOPE] = (kn * rh * gk_n[...]).astype(ka_ref.dtype)
            ka_ref[h, :, MLA_NOPE:MLA_QK_PAD] = _rope(kpe * rh * gk_r[...], cos_v, sin_v).astype(ka_ref.dtype)
            va_ref[h] = kv[:, c0 + MLA_NOPE:c0 + MLA_NOPE + MLA_V].astype(va_ref.dtype)
        for j in range(SWA_PAIRS):
            c0 = C_QS + j * LANES
            qb_ref[j] = _pair_norm_rope(proj[:, c0:c0 + LANES], g_sq[...], cos_v, sin_v).astype(qb_ref.dtype)
        k0, k1 = _dup_halves(_pair_norm_rope(proj[:, C_KS:C_VS], g_sk[...], cos_v, sin_v))
        kb_ref[0] = k0.astype(kb_ref.dtype)
        kb_ref[1] = k1.astype(kb_ref.dtype)
        v0, v1 = _dup_halves(proj[:, C_VS:W_IN_PACKED])
        vb_ref[0] = v0.astype(vb_ref.dtype)
        vb_ref[1] = v1.astype(vb_ref.dtype)

    return pl.pallas_call(
        body, name="prep_fwd", grid=(t // tb,),
        in_specs=[_rows(tb, d), _rows(tb, LANES), _rows(tb, LANES)] + _prep_specs(p),
        out_specs=[_heads_rows(MLA_HEADS, tb, MLA_QK_PAD), _heads_rows(MLA_HEADS, tb, MLA_QK_PAD),
                   _heads_rows(MLA_HEADS, tb, MLA_V), _heads_rows(SWA_PAIRS, tb, LANES),
                   _heads_rows(SWA_KV, tb, LANES), _heads_rows(SWA_KV, tb, LANES)],
        out_shape=[_sds((MLA_HEADS, t, MLA_QK_PAD), _MXU), _sds((MLA_HEADS, t, MLA_QK_PAD), _MXU),
                   _sds((MLA_HEADS, t, MLA_V), _MXU), _sds((SWA_PAIRS, t, LANES), _MXU),
                   _sds((SWA_KV, t, LANES), _MXU), _sds((SWA_KV, t, LANES), _MXU)],
        compiler_params=_cparams(("parallel",)),
    )(x, cos, sin_s, *[p[n] for n in PREP_WEIGHTS])


def _prep_bwd(x, dxin, cos, sin_s, p, dqa, dka, dva, dqb, dkb, dvb):
    t, d = x.shape
    tb = min(_TB, t)
    n_w = len(PREP_WEIGHTS)

    def body(*refs):
        x_ref, dxin_ref, cos_ref, sin_ref = refs[:4]
        mix_g, w_in, g_qa, wqb, g_kva, w_kvb, gq_n, gq_r, gk_n, gk_r, g_sq, g_sk = refs[4:4 + n_w]
        dqa_ref, dka_ref, dva_ref, dqb_ref, dkb_ref, dvb_ref = refs[4 + n_w:10 + n_w]
        dx_ref = refs[10 + n_w]
        grads = dict(zip(PREP_WEIGHTS, refs[11 + n_w:11 + 2 * n_w]))
        dproj_ref, dkv_ref, dqh_ref = refs[11 + 2 * n_w:]

        @pl.when(pl.program_id(0) == 0)
        def _():
            for ref in grads.values():
                ref[...] = jnp.zeros_like(ref)

        xv = x_ref[...]
        cos_v, sin_v = cos_ref[...], sin_ref[...]
        r0 = _rsq(_sumsq(xv), d)
        xhat = xv * r0
        hb = _mx(xhat * mix_g[...])
        proj = _dot_nt(hb, w_in[...])

        cq = proj[:, C_CQ:C_CKV]
        rq = _rsq(_sumsq(cq), MLA_Q_RANK)
        cqh = cq * rq
        cqn = _mx(cqh * g_qa[...])
        dcqn = jnp.zeros((tb, MLA_Q_RANK), jnp.float32)
        for h in range(MLA_HEADS):
            qh = _dot_nt(cqn, wqb[h])
            qn, qr = qh[:, :MLA_NOPE], qh[:, MLA_NOPE:]
            rh = _rsq(_sumsq(qn) + _sumsq(qr), MLA_QK)
            xh_n, xh_r = qn * rh, qr * rh
            dy_n = dqa_ref[h, :, 0:MLA_NOPE]
            dy_r = _rope_bwd(dqa_ref[h, :, MLA_NOPE:MLA_QK_PAD], cos_v, sin_v)
            grads["gq_n"][...] += _colsum(dy_n * xh_n)
            grads["gq_r"][...] += _colsum(dy_r * xh_r)
            dqn, dqr = _norm_bwd([dy_n * gq_n[...], dy_r * gq_r[...]], [xh_n, xh_r], rh, MLA_QK)
            dqh_ref[:, 0:MLA_NOPE] = _mx(dqn)
            dqh_ref[:, MLA_NOPE:MLA_QK_PAD] = _mx(dqr)
            dqh = dqh_ref[...]
            grads["wqb"][h] += _dot_tn(dqh, cqn)
            dcqn = dcqn + _dot(dqh, wqb[h])
        grads["g_qa"][...] += _colsum(dcqn * cqh)
        (dcq,) = _norm_bwd([dcqn * g_qa[...]], [cqh], rq, MLA_Q_RANK)
        dproj_ref[:, C_CQ:C_CKV] = _mx(dcq)

        ckv = proj[:, C_CKV:C_KPE]
        rkv = _rsq(_sumsq(ckv), MLA_KV_RANK)
        ckvh = ckv * rkv
        ckvn = _mx(ckvh * g_kva[...])
        kpe = proj[:, C_KPE:C_QS]
        ss_pe = _sumsq(kpe)
        kv = _dot_nt(ckvn, w_kvb[...])
        dkpe = jnp.zeros((tb, LANES), jnp.float32)
        for h in range(MLA_HEADS):
            c0 = h * (MLA_NOPE + MLA_V)
            c1 = c0 + MLA_NOPE
            kn = kv[:, c0:c1]
            rh = _rsq(_sumsq(kn) + ss_pe, MLA_QK)
            xh_n, xh_r = kn * rh, kpe * rh
            dy_n = dka_ref[h, :, 0:MLA_NOPE]
            dy_r = _rope_bwd(dka_ref[h, :, MLA_NOPE:MLA_QK_PAD], cos_v, sin_v)
            grads["gk_n"][...] += _colsum(dy_n * xh_n)
            grads["gk_r"][...] += _colsum(dy_r * xh_r)
            dkn, dkr = _norm_bwd([dy_n * gk_n[...], dy_r * gk_r[...]], [xh_n, xh_r], rh, MLA_QK)
            dkpe = dkpe + dkr
            dkv_ref[:, c0:c1] = _mx(dkn)
            dkv_ref[:, c1:c1 + MLA_V] = _mx(dva_ref[h])
        dkv = dkv_ref[...]
        grads["w_kvb"][...] += _dot_tn(dkv, ckvn)
        dckvn = _dot(dkv, w_kvb[...])
        grads["g_kva"][...] += _colsum(dckvn * ckvh)
        (dckv,) = _norm_bwd([dckvn * g_kva[...]], [ckvh], rkv, MLA_KV_RANK)
        dproj_ref[:, C_CKV:C_KPE] = _mx(dckv)
        dproj_ref[:, C_KPE:C_QS] = _mx(dkpe)

        def pair_bwd(tv, dy, g_ref, gname):
            r = _rsq(_half_sums(tv * tv), SWA_D)
            xh = tv * r
            dpre = _rope_bwd(dy, cos_v, sin_v)
            grads[gname][...] += _colsum(dpre * xh)
            dn = dpre * g_ref[...]
            return r * (dn - xh * (_half_sums(dn * xh) * (1.0 / SWA_D)))

        for j in range(SWA_PAIRS):
            c0 = C_QS + j * LANES
            dproj_ref[:, c0:c0 + LANES] = _mx(pair_bwd(proj[:, c0:c0 + LANES], dqb_ref[j], g_sq, "g_sq"))
        dproj_ref[:, C_KS:C_VS] = _mx(pair_bwd(proj[:, C_KS:C_VS], _undup_halves(dkb_ref[0], dkb_ref[1]), g_sk, "g_sk"))
        dproj_ref[:, C_VS:W_IN_PACKED] = _mx(_undup_halves(dvb_ref[0], dvb_ref[1]))

        dproj = dproj_ref[...]
        grads["w_in"][...] += _dot_tn(dproj, hb)
        dh = _dot(dproj, w_in[...])
        grads["mix_g"][...] += _colsum(dh * xhat)
        (dxv,) = _norm_bwd([dh * mix_g[...]], [xhat], r0, d)
        dx_ref[...] = dxin_ref[...] + dxv

    grad_shapes = [p[n].shape for n in PREP_WEIGHTS]
    return pl.pallas_call(
        body, name="prep_bwd", grid=(t // tb,),
        in_specs=[_rows(tb, d), _rows(tb, d), _rows(tb, LANES), _rows(tb, LANES)] + _prep_specs(p) + [
            _heads_rows(MLA_HEADS, tb, MLA_QK_PAD), _heads_rows(MLA_HEADS, tb, MLA_QK_PAD),
            _heads_rows(MLA_HEADS, tb, MLA_V), _heads_rows(SWA_PAIRS, tb, LANES),
            _heads_rows(SWA_KV, tb, LANES), _heads_rows(SWA_KV, tb, LANES)],
        out_specs=[_rows(tb, d)] + [_acc(s) for s in grad_shapes],
        out_shape=[_sds((t, d), jnp.float32)] + [_sds(s, jnp.float32) for s in grad_shapes],
        scratch_shapes=[pltpu.VMEM((tb, W_IN_PACKED), _MXU), pltpu.VMEM((tb, MLA_HEADS * (MLA_NOPE + MLA_V)), _MXU),
                        pltpu.VMEM((tb, MLA_QK_PAD), _MXU)],
        compiler_params=_cparams(("arbitrary",)),
    )(x, dxin, cos, sin_s, *[p[n] for n in PREP_WEIGHTS], dqa, dka, dva, dqb, dkb, dvb)


def _strips(n):
    step = min(_STRIP, n)
    return [slice(r, r + step) for r in range(0, n, step)]


def _mla_fwd(q, k, v, side=None):
    hn, t, dq = q.shape
    dv = v.shape[2]
    bq = min(_BQ, t)
    scale = MLA_QK ** -0.5

    def body(q_ref, k_ref, v_ref, o_ref, l_ref):
        i = pl.program_id(1)
        qv = q_ref[0]

        def step(j, carry, masked):
            m, l, acc = carry
            start = pl.multiple_of(j * bq, bq)
            s = _dot_nt(qv, k_ref[0, pl.ds(start, bq), :]) * scale
            if masked:
                row = lax.broadcasted_iota(jnp.int32, (bq, bq), 0)
                col = lax.broadcasted_iota(jnp.int32, (bq, bq), 1)
                s = jnp.where(col <= row, s, NEG)
            m_new = jnp.maximum(m, _rowmax(s))
            alpha = jnp.exp(m - m_new)
            pv = jnp.exp(s - m_new)
            l = alpha * l + _rowsum(pv)
            acc = alpha * acc + _dot(_mx(pv), v_ref[0, pl.ds(start, bq), :])
            return m_new, l, acc

        init = (jnp.full((bq, 1), NEG, jnp.float32), jnp.zeros((bq, 1), jnp.float32), jnp.zeros((bq, dv), jnp.float32))
        carry = lax.fori_loop(0, i, lambda j, c: step(j, c, False), init)
        m, l, acc = step(i, carry, True)
        o_ref[0] = acc / l
        l_ref[0] = jnp.broadcast_to(m + jnp.log(l), (bq, LANES))

    return _call(
        body, (q, k, v), side, name="mla_fwd", grid=(hn, t // bq),
        in_specs=[pl.BlockSpec((1, bq, dq), lambda h, i: (h, i, 0)),
                  pl.BlockSpec((1, t, dq), lambda h, i: (h, 0, 0)),
                  pl.BlockSpec((1, t, dv), lambda h, i: (h, 0, 0))],
        out_specs=[pl.BlockSpec((1, bq, dv), lambda h, i: (h, i, 0)),
                   pl.BlockSpec((1, bq, LANES), lambda h, i: (h, i, 0))],
        out_shape=[_sds((hn, t, dv), jnp.float32), _sds((hn, t, LANES), jnp.float32)],
        semantics=("parallel", "arbitrary"))


def _mla_bwd(q, k, v, do, lse_rows, dsum_rows, side=None):
    hn, t, dq_w = q.shape
    dv_w = v.shape[2]
    bq = min(_BQ, t)
    nb = t // bq
    scale = MLA_QK ** -0.5

    def body(q_ref, do_ref, l_ref, d_ref, k_ref, v_ref, dq_ref, dk_ref, dv_ref, st_scr, dpt_scr, p_scr, ds_scr):
        j = pl.program_id(1)

        @pl.when(j == 0)
        def _():
            dq_ref[...] = jnp.zeros_like(dq_ref)

        kv = k_ref[0]
        vv = v_ref[0]
        dk_ref[0] = jnp.zeros((bq, dq_w), jnp.float32)
        dv_ref[0] = jnp.zeros((bq, dv_w), jnp.float32)

        def tile(i, masked):
            start = pl.multiple_of(i * bq, bq)
            qv = q_ref[0, pl.ds(start, bq), :]
            dov = do_ref[0, pl.ds(start, bq), :]
            st_scr[...] = _dot_nt(kv, qv)
            dpt_scr[...] = _dot_nt(vv, dov)
            lse = l_ref[0, i]
            dsum = d_ref[0, i]
            for rows in _strips(bq):
                pt = jnp.exp(st_scr[rows, :] * scale - lse)
                if masked:
                    n_rows = rows.stop - rows.start
                    row = lax.broadcasted_iota(jnp.int32, (n_rows, bq), 0) + rows.start
                    col = lax.broadcasted_iota(jnp.int32, (n_rows, bq), 1)
                    pt = jnp.where(row <= col, pt, 0.0)
                p_scr[rows, :] = _mx(pt)
                ds_scr[rows, :] = _mx(pt * (dpt_scr[rows, :] - dsum) * scale)
            ds_t = ds_scr[...]
            dv_ref[0] += _dot(p_scr[...], dov)
            dk_ref[0] += _dot(ds_t, qv)
            dq_ref[0, pl.ds(start, bq), :] += _dot_tn(ds_t, kv)

        def loop_body(i, carry):
            tile(i, False)
            return carry

        tile(j, True)
        lax.fori_loop(j + 1, nb, loop_body, 0)

    return _call(
        body, (q, do, lse_rows, dsum_rows, k, v), side, name="mla_bwd", grid=(hn, nb),
        in_specs=[pl.BlockSpec((1, t, dq_w), lambda h, j: (h, 0, 0)),
                  pl.BlockSpec((1, t, dv_w), lambda h, j: (h, 0, 0)),
                  pl.BlockSpec((1, nb, 1, bq), lambda h, j: (h, 0, 0, 0)),
                  pl.BlockSpec((1, nb, 1, bq), lambda h, j: (h, 0, 0, 0)),
                  pl.BlockSpec((1, bq, dq_w), lambda h, j: (h, j, 0)),
                  pl.BlockSpec((1, bq, dv_w), lambda h, j: (h, j, 0))],
        out_specs=[pl.BlockSpec((1, t, dq_w), lambda h, j: (h, 0, 0)),
                   pl.BlockSpec((1, bq, dq_w), lambda h, j: (h, j, 0)),
                   pl.BlockSpec((1, bq, dv_w), lambda h, j: (h, j, 0))],
        out_shape=[_sds((hn, t, dq_w), jnp.float32), _sds((hn, t, dq_w), jnp.float32), _sds((hn, t, dv_w), jnp.float32)],
        scratch_shapes=[pltpu.VMEM((bq, bq), jnp.float32), pltpu.VMEM((bq, bq), jnp.float32),
                        pltpu.VMEM((bq, bq), _MXU), pltpu.VMEM((bq, bq), _MXU)],
        semantics=("parallel", "arbitrary"))


STACK = SWA_GROUP * SWA_BLOCK


def _swa_stack(ref, c, rows):
    low = _low_half()
    parts = []
    for g in range(SWA_GROUP):
        tv = ref[SWA_GROUP // 2 * c + g // 2, rows, :]
        keep = low if g % 2 == 0 else jnp.logical_not(low)
        parts.append(_mx(jnp.where(keep, tv, jnp.zeros_like(tv))))
    return jnp.concatenate(parts, axis=0)


def _swa_cols(ref, c, rows):
    return jnp.concatenate([ref[SWA_GROUP * c + g, rows, 0:1] for g in range(SWA_GROUP)], axis=0)


def _swa_sink_col(s_ref, c):
    return jnp.concatenate([jnp.broadcast_to(s_ref[SWA_GROUP * c + g][:, 0:1], (SWA_BLOCK, 1))
                            for g in range(SWA_GROUP)], axis=0)


def _swa_band_masks():
    row = lax.broadcasted_iota(jnp.int32, (STACK, SWA_BLOCK), 0) & (SWA_BLOCK - 1)
    col = lax.broadcasted_iota(jnp.int32, (STACK, SWA_BLOCK), 1)
    return col <= row, col > row


def _swa_unstack_pairs(ref, c, rows, stacked):
    for pr in range(SWA_GROUP // 2):
        r0 = 2 * pr * SWA_BLOCK
        ref[SWA_GROUP // 2 * c + pr, rows, :] = _pick_halves(stacked[r0:r0 + SWA_BLOCK], stacked[r0 + SWA_BLOCK:r0 + 2 * SWA_BLOCK])


def _swa_blocks(t):
    nblk = t // SWA_BLOCK
    bps = min(_SWA_STEP, nblk)
    return nblk, bps, bps * SWA_BLOCK


def _swa_fwd(q, k, v, sinks):
    _, t, _ = q.shape
    nblk, bps, sb = _swa_blocks(t)
    scale = SWA_D ** -0.5

    def body(q_ref, k_ref, kp_ref, v_ref, vp_ref, s_ref, o_ref, l_ref):
        n = pl.program_id(0)
        m_cur, m_prev = _swa_band_masks()
        for c in range(SWA_KV):
            sink = _swa_sink_col(s_ref, c)
            for b in range(bps):
                rows = slice(b * SWA_BLOCK, (b + 1) * SWA_BLOCK)
                kc, vc = k_ref[c, rows, :], v_ref[c, rows, :]
                if b == 0:
                    kp, vp, mp = kp_ref[c], vp_ref[c], jnp.logical_and(m_prev, n > 0)
                else:
                    before = slice((b - 1) * SWA_BLOCK, b * SWA_BLOCK)
                    kp, vp, mp = k_ref[c, before, :], v_ref[c, before, :], m_prev
                qs = _swa_stack(q_ref, c, rows)
                s_c = jnp.where(m_cur, _dot_nt(qs, kc) * scale, NEG)
                s_p = jnp.where(mp, _dot_nt(qs, kp) * scale, NEG)
                m = jnp.maximum(jnp.maximum(_rowmax(s_c), _rowmax(s_p)), sink)
                e_c = jnp.exp(s_c - m)
                e_p = jnp.exp(s_p - m)
                denom = _rowsum(e_c) + _rowsum(e_p) + jnp.exp(sink - m)
                inv = 1.0 / denom
                o = _dot(_mx(e_c * inv), vc) + _dot(_mx(e_p * inv), vp)
                lse = m + jnp.log(denom)
                for g in range(SWA_GROUP):
                    l_ref[SWA_GROUP * c + g, rows, :] = jnp.broadcast_to(
                        lse[g * SWA_BLOCK:(g + 1) * SWA_BLOCK], (SWA_BLOCK, LANES))
                _swa_unstack_pairs(o_ref, c, rows, o)

    main = lambda n: (0, n, 0)
    prev = lambda n: (0, jnp.maximum(n * bps - 1, 0), 0)
    return pl.pallas_call(
        body, name="swa_fwd", grid=(nblk // bps,),
        in_specs=[pl.BlockSpec((SWA_PAIRS, sb, LANES), main),
                  pl.BlockSpec((SWA_KV, sb, LANES), main), pl.BlockSpec((SWA_KV, SWA_BLOCK, LANES), prev),
                  pl.BlockSpec((SWA_KV, sb, LANES), main), pl.BlockSpec((SWA_KV, SWA_BLOCK, LANES), prev),
                  _const((SWA_HEADS, 1, LANES))],
        out_specs=[pl.BlockSpec((SWA_PAIRS, sb, LANES), main), pl.BlockSpec((SWA_HEADS, sb, LANES), main)],
        out_shape=[_sds((SWA_PAIRS, t, LANES), jnp.float32), _sds((SWA_HEADS, t, LANES), jnp.float32)],
        compiler_params=_cparams(("parallel",)),
    )(q, k, k, v, v, sinks)


def _swa_bwd(q, k, v, sinks, do, lse, dsum):
    _, t, _ = q.shape
    nblk, bps, sb = _swa_blocks(t)
    steps = nblk // bps
    scale = SWA_D ** -0.5

    def body(q_ref, k_ref, kp_ref, v_ref, vp_ref, s_ref, do_ref, l_ref, d_ref, qn_ref, don_ref, ln_ref, dn_ref,
             dq_ref, dk_ref, dv_ref, ds_ref):
        n = pl.program_id(0)

        @pl.when(n == 0)
        def _():
            ds_ref[...] = jnp.zeros_like(ds_ref)

        m_cur, m_prev = _swa_band_masks()
        everything = slice(0, SWA_BLOCK)

        def probs(qs, keys, mask, lcol):
            return jnp.where(mask, jnp.exp(_dot_nt(qs, keys) * scale - lcol), 0.0)

        def dscores(pm, dos, vals, dcol):
            return _mx(pm * (_dot_nt(dos, vals) - dcol) * scale)

        for c in range(SWA_KV):
            sink = _swa_sink_col(s_ref, c)
            dk_acc = [jnp.zeros((SWA_BLOCK, LANES), jnp.float32) for _ in range(bps)]
            dv_acc = [jnp.zeros((SWA_BLOCK, LANES), jnp.float32) for _ in range(bps)]
            for b in range(bps):
                rows = slice(b * SWA_BLOCK, (b + 1) * SWA_BLOCK)
                kc, vc = k_ref[c, rows, :], v_ref[c, rows, :]
                if b == 0:
                    kp, vp, mp = kp_ref[c], vp_ref[c], jnp.logical_and(m_prev, n > 0)
                else:
                    before = slice((b - 1) * SWA_BLOCK, b * SWA_BLOCK)
                    kp, vp, mp = k_ref[c, before, :], v_ref[c, before, :], m_prev
                qs = _swa_stack(q_ref, c, rows)
                dos = _swa_stack(do_ref, c, rows)
                lcol = _swa_cols(l_ref, c, rows)
                dcol = _swa_cols(d_ref, c, rows)
                p_c = probs(qs, kc, m_cur, lcol)
                p_p = probs(qs, kp, mp, lcol)
                ds_c = dscores(p_c, dos, vc, dcol)
                ds_p = dscores(p_p, dos, vp, dcol)
                _swa_unstack_pairs(dq_ref, c, rows, _dot(ds_c, kc) + _dot(ds_p, kp))
                dk_acc[b] = dk_acc[b] + _dot_tn(ds_c, qs)
                dv_acc[b] = dv_acc[b] + _dot_tn(_mx(p_c), dos)
                if b > 0:
                    dk_acc[b - 1] = dk_acc[b - 1] + _dot_tn(ds_p, qs)
                    dv_acc[b - 1] = dv_acc[b - 1] + _dot_tn(_mx(p_p), dos)
                p_sink = jnp.exp(sink - lcol) * dcol
                for g in range(SWA_GROUP):
                    ds_ref[SWA_GROUP * c + g] += -jnp.sum(p_sink[g * SWA_BLOCK:(g + 1) * SWA_BLOCK])
            tail = slice((bps - 1) * SWA_BLOCK, bps * SWA_BLOCK)
            kc, vc = k_ref[c, tail, :], v_ref[c, tail, :]
            qs = _swa_stack(qn_ref, c, everything)
            dos = _swa_stack(don_ref, c, everything)
            lcol = _swa_cols(ln_ref, c, everything)
            dcol = _swa_cols(dn_ref, c, everything)
            p_p = probs(qs, kc, jnp.logical_and(m_prev, n < steps - 1), lcol)
            ds_p = dscores(p_p, dos, vc, dcol)
            dk_acc[bps - 1] = dk_acc[bps - 1] + _dot_tn(ds_p, qs)
            dv_acc[bps - 1] = dv_acc[bps - 1] + _dot_tn(_mx(p_p), dos)
            for b in range(bps):
                rows = slice(b * SWA_BLOCK, (b + 1) * SWA_BLOCK)
                dk_ref[c, rows, :] = dk_acc[b]
                dv_ref[c, rows, :] = dv_acc[b]

    main = lambda n: (0, n, 0)
    prev = lambda n: (0, jnp.maximum(n * bps - 1, 0), 0)
    nxt = lambda n: (0, jnp.minimum((n + 1) * bps, nblk - 1), 0)
    pairs = pl.BlockSpec((SWA_PAIRS, sb, LANES), main)
    kvs = pl.BlockSpec((SWA_KV, sb, LANES), main)
    kv_prev = pl.BlockSpec((SWA_KV, SWA_BLOCK, LANES), prev)
    stats = pl.BlockSpec((SWA_HEADS, sb, LANES), main)
    pairs_next = pl.BlockSpec((SWA_PAIRS, SWA_BLOCK, LANES), nxt)
    stats_next = pl.BlockSpec((SWA_HEADS, SWA_BLOCK, LANES), nxt)
    return pl.pallas_call(
        body, name="swa_bwd", grid=(steps,),
        in_specs=[pairs, kvs, kv_prev, kvs, kv_prev, _const((SWA_HEADS, 1, LANES)), pairs, stats, stats,
                  pairs_next, pairs_next, stats_next, stats_next],
        out_specs=[pairs, kvs, kvs, _acc((SWA_HEADS, 1, LANES))],
        out_shape=[_sds((SWA_PAIRS, t, LANES), jnp.float32), _sds((SWA_KV, t, LANES), jnp.float32),
                   _sds((SWA_KV, t, LANES), jnp.float32), _sds((SWA_HEADS, 1, LANES), jnp.float32)],
        compiler_params=_cparams(("arbitrary",)),
    )(q, k, k, v, v, sinks, do, lse, dsum, q, do, lse, dsum)


MIX_SLABS = 4
MIX_WIDTH = MIX_SLABS * LANES


def _mix_out_fwd(x, oa, ob, ga, gb, wo_a, wo_b):
    t, d = x.shape
    tb = min(_TB, t)

    def body(x_ref, oa_ref, ob_ref, ga_ref, gb_ref, woa_ref, wob_ref, xo_ref):
        y = x_ref[...]
        for o_ref, g_ref, w_ref in ((oa_ref, ga_ref, woa_ref), (ob_ref, gb_ref, wob_ref)):
            r = _rsq(sum(_sumsq(o_ref[h]) for h in range(MIX_SLABS)), MIX_WIDTH)
            for h in range(MIX_SLABS):
                y = y + _dot(_mx(o_ref[h] * r * g_ref[h]), w_ref[h])
        xo_ref[...] = y

    slab = _heads_rows(MIX_SLABS, tb, LANES)
    return pl.pallas_call(
        body, name="mix_out_fwd", grid=(t // tb,),
        in_specs=[_rows(tb, d), slab, slab, _const(ga.shape), _const(gb.shape), _const(wo_a.shape), _const(wo_b.shape)],
        out_specs=_rows(tb, d),
        out_shape=_sds((t, d), jnp.float32),
        compiler_params=_cparams(("parallel",)),
    )(x, oa, ob, ga, gb, wo_a, wo_b)


def _mix_out_bwd(dx, oa, ob, ga, gb, wo_a, wo_b):
    t, d = dx.shape
    tb = min(_TB, t)

    def group(o_ref, g_ref, w_ref, dyb, do_ref, n_ref, col0, dg_ref):
        r = _rsq(sum(_sumsq(o_ref[h]) for h in range(MIX_SLABS)), MIX_WIDTH)
        xh, dn = [], []
        for h in range(MIX_SLABS):
            xh.append(o_ref[h] * r)
            n_ref[:, col0 + h * LANES:col0 + (h + 1) * LANES] = _mx(xh[h] * g_ref[h])
            dm = _dot_nt(dyb, w_ref[h])
            dg_ref[h] += _colsum(dm * xh[h])
            dn.append(dm * g_ref[h])
        c = sum(_rowsum(dn[h] * xh[h]) for h in range(MIX_SLABS)) * (1.0 / MIX_WIDTH)
        prods = []
        for h in range(MIX_SLABS):
            do = r * (dn[h] - xh[h] * c)
            do_ref[h] = do.astype(do_ref.dtype)
            prods.append(do * o_ref[h])
        return prods

    def body(dx_ref, oa_ref, ob_ref, ga_ref, gb_ref, woa_ref, wob_ref,
             doa_ref, dsa_ref, dob_ref, dsb_ref, n_ref, dy_ref, dga_ref, dgb_ref):
        @pl.when(pl.program_id(0) == 0)
        def _():
            dga_ref[...] = jnp.zeros_like(dga_ref)
            dgb_ref[...] = jnp.zeros_like(dgb_ref)

        dyb = _mx(dx_ref[...])
        dy_ref[...] = dyb
        for h, pr in enumerate(group(oa_ref, ga_ref, woa_ref, dyb, doa_ref, n_ref, 0, dga_ref)):
            dsa_ref[h] = jnp.broadcast_to(_rowsum(pr), (tb, LANES))
        low = _low_half()
        for j, pr in enumerate(group(ob_ref, gb_ref, wob_ref, dyb, dob_ref, n_ref, MIX_WIDTH, dgb_ref)):
            dsb_ref[2 * j] = jnp.broadcast_to(_rowsum(jnp.where(low, pr, 0.0)), (tb, LANES))
            dsb_ref[2 * j + 1] = jnp.broadcast_to(_rowsum(jnp.where(low, 0.0, pr)), (tb, LANES))

    slab = _heads_rows(MIX_SLABS, tb, LANES)
    return pl.pallas_call(
        body, name="mix_out_bwd", grid=(t // tb,),
        in_specs=[_rows(tb, d), slab, slab, _const(ga.shape), _const(gb.shape), _const(wo_a.shape), _const(wo_b.shape)],
        out_specs=[slab, slab, slab, _heads_rows(SWA_HEADS, tb, LANES), _rows(tb, 2 * MIX_WIDTH), _rows(tb, d),
                   _acc(ga.shape), _acc(gb.shape)],
        out_shape=[_sds((MIX_SLABS, t, LANES), _MXU), _sds((MIX_SLABS, t, LANES), jnp.float32),
                   _sds((MIX_SLABS, t, LANES), jnp.float32), _sds((SWA_HEADS, t, LANES), jnp.float32),
                   _sds((t, 2 * MIX_WIDTH), _MXU), _sds((t, d), _MXU),
                   _sds(ga.shape, jnp.float32), _sds(gb.shape, jnp.float32)],
        compiler_params=_cparams(("arbitrary",)),
    )(dx, oa, ob, ga, gb, wo_a, wo_b)


def _loss_head(y, target):
    t, d = y.shape
    tb = min(_TB, t)

    def body(y_ref, t_ref, dy_ref, acc_ref):
        @pl.when(pl.program_id(0) == 0)
        def _():
            acc_ref[...] = jnp.zeros_like(acc_ref)

        err = y_ref[...] - t_ref[...]
        dy_ref[...] = err * (1.0 / d)
        acc_ref[...] += jnp.sum(err * err)

    return pl.pallas_call(
        body, name="loss_head", grid=(t // tb,),
        in_specs=[_rows(tb, d), _rows(tb, d)],
        out_specs=[_rows(tb, d), _acc((8, LANES))],
        out_shape=[_sds((t, d), jnp.float32), _sds((8, LANES), jnp.float32)],
        compiler_params=_cparams(("arbitrary",)),
    )(y, target)


def _is_transposed(name):
    return name not in ROW_SHARDED


def _pack_layer(shards, l, width):
    rows = [(shards[n][l].T if _is_transposed(n) else shards[n][l]).reshape(-1, width) for n in BIG]
    return jnp.concatenate(rows, axis=0)


def _full_shape(like, name):
    _, a, b = like[name].shape
    return (N_DEV * b, a) if _is_transposed(name) else (N_DEV * a, b)


def _unpack_full(gathered, like):
    out, off = {}, 0
    for n in BIG:
        rows_n = like[n][0].size // gathered.shape[-1]
        out[n] = gathered[:, off:off + rows_n].reshape(_full_shape(like, n))
        off += rows_n
    return out


def _pack_by_destination(grads, width):
    return jnp.concatenate([grads[n].reshape(N_DEV, -1, width) for n in BIG], axis=1)


def _unpack_shards(packed, like):
    out, off = {}, 0
    for n in BIG:
        _, a, b = like[n].shape
        rows_n = a * b // packed.shape[-1]
        seg = packed[off:off + rows_n]
        out[n] = seg.reshape(b, a).T if _is_transposed(n) else seg.reshape(a, b)
        off += rows_n
    return out


def _small_rows(n_elems):
    return -(-n_elems // LANES)


def _pack_small(arrays):
    parts = []
    for n in SMALL:
        v = arrays[n]
        depth, width = v.shape
        padded = _small_rows(width) * LANES
        parts.append(jnp.pad(v, ((0, 0), (0, padded - width))).reshape(-1, LANES))
    packed = jnp.concatenate(parts, axis=0)
    return jnp.pad(packed, ((0, (-packed.shape[0]) % 8), (0, 0)))


def _unpack_small(packed, like):
    out, off = {}, 0
    for n in SMALL:
        depth, width = like[n].shape
        rows_n = _small_rows(width)
        seg = packed[off:off + depth * rows_n].reshape(depth, rows_n * LANES)
        out[n] = seg[:, :width]
        off += depth * rows_n
    return out


def _rope_tables(t):
    pos = jnp.arange(t, dtype=jnp.float32)
    inv = 1.0 / (ROPE_THETA ** (jnp.arange(0, MLA_ROPE, 2, dtype=jnp.float32) / MLA_ROPE))
    ang = pos[:, None] * inv[None, :]
    cos, sin = jnp.cos(ang), jnp.sin(ang)
    return jnp.concatenate([cos, cos, cos, cos], axis=1), jnp.concatenate([-sin, sin, -sin, sin], axis=1)


def _pad_lanes(a, width):
    return jnp.pad(a, [(0, 0)] * (a.ndim - 1) + [(0, width - a.shape[-1])])


def _layer_params(full, small, l):
    w_in = full["w_in"]
    d = w_in.shape[1]
    mla_rows = W_IN_COLS[0]
    w_in_p = jnp.concatenate([w_in[:mla_rows], jnp.zeros((LANES - MLA_ROPE, d), w_in.dtype), w_in[mla_rows:]], axis=0)
    wqb = full["mla_w_q_b"].reshape(MLA_HEADS, MLA_QK, MLA_Q_RANK)
    wqb = jnp.pad(wqb, ((0, 0), (0, MLA_QK_PAD - MLA_QK), (0, 0)))
    row = lambda name: small[name][l][None, :]
    twice = lambda g: jnp.concatenate([g, g], axis=1)
    prep = {
        "mix_g": row("mix_norm"), "w_in": w_in_p,
        "g_qa": row("mla_q_a_norm"), "wqb": wqb,
        "g_kva": row("mla_kv_a_norm"), "w_kvb": full["mla_w_kv_b"],
        "gq_n": row("mla_q_norm")[:, :MLA_NOPE], "gq_r": _pad_lanes(row("mla_q_norm")[:, MLA_NOPE:], LANES),
        "gk_n": row("mla_k_norm")[:, :MLA_NOPE], "gk_r": _pad_lanes(row("mla_k_norm")[:, MLA_NOPE:], LANES),
        "g_sq": twice(row("swa_q_norm")), "g_sk": twice(row("swa_k_norm")),
    }
    return {
        "prep": prep,
        "ffn1": (row("ffn1_norm"), full["ffn1_w_gate"], full["ffn1_w_up"], full["ffn1_w_down"]),
        "ffn2": (row("ffn2_norm"), full["ffn2_w_gate"], full["ffn2_w_up"], full["ffn2_w_down"]),
        "sinks": jnp.broadcast_to(small["swa_sinks"][l][:, None, None], (SWA_HEADS, 1, LANES)),
        "ga": small["mla_out_norm"][l].reshape(MIX_SLABS, 1, LANES),
        "gb": small["swa_out_norm"][l].reshape(MIX_SLABS, 1, LANES),
        "wo_a": full["w_o"][:MIX_WIDTH].reshape(MIX_SLABS, LANES, d),
        "wo_b": full["w_o"][MIX_WIDTH:].reshape(MIX_SLABS, LANES, d),
    }


def _ffn_backward(x_in, dxo, a, u, params, tag, side=None):
    gain, wg, wu, wd = params
    (dxi, da, du, s, h, dy, dg), side_out = _ffn_dgrad(x_in, gain, dxo, a, u, wg, wu, wd, side)
    dwg = _tn_matmul(da, h, "wgrad_" + tag + "_gate")
    dwu = _tn_matmul(du, h, "wgrad_" + tag + "_up")
    dwd = _tn_matmul(s, dy, "wgrad_" + tag + "_down")
    return dxi, dg[0], dwg, dwu, dwd, side_out


def kernel(x, ffn1_norm, ffn1_w_gate, ffn1_w_up, ffn1_w_down, mix_norm, w_in, mla_q_a_norm, mla_w_q_b, mla_kv_a_norm, mla_w_kv_b, mla_q_norm, mla_k_norm, swa_q_norm, swa_k_norm, swa_sinks, mla_out_norm, swa_out_norm, w_o, ffn2_norm, ffn2_w_gate, ffn2_w_up, ffn2_w_down, loss_target, m_ffn1_norm, m_ffn1_w_gate, m_ffn1_w_up, m_ffn1_w_down, m_mix_norm, m_w_in, m_mla_q_a_norm, m_mla_w_q_b, m_mla_kv_a_norm, m_mla_w_kv_b, m_mla_q_norm, m_mla_k_norm, m_swa_q_norm, m_swa_k_norm, m_swa_sinks, m_mla_out_norm, m_swa_out_norm, m_w_o, m_ffn2_norm, m_ffn2_w_gate, m_ffn2_w_up, m_ffn2_w_down, v_ffn1_norm, v_ffn1_w_gate, v_ffn1_w_up, v_ffn1_w_down, v_mix_norm, v_w_in, v_mla_q_a_norm, v_mla_w_q_b, v_mla_kv_a_norm, v_mla_w_kv_b, v_mla_q_norm, v_mla_k_norm, v_swa_q_norm, v_swa_k_norm, v_swa_sinks, v_mla_out_norm, v_swa_out_norm, v_w_o, v_ffn2_norm, v_ffn2_w_gate, v_ffn2_w_up, v_ffn2_w_down):
    local = dict(locals())
    w = {n: local[n] for n in WEIGHTS}
    m = {n: local["m_" + n] for n in WEIGHTS}
    v = {n: local["v_" + n] for n in WEIGHTS}
    depth = ffn1_norm.shape[0]
    t, d = x.shape[-2], x.shape[-1]
    x2d = x.reshape(t, d)
    target = loss_target.reshape(t, d)
    bq = min(_BQ, t)

    big = {n: w[n] for n in BIG}
    packed = [_mx(_pack_layer(big, l, d)) for l in range(depth)]
    gathered = _all_gather(packed[0])
    cos, sin_s = _rope_tables(t)
    x_i, y_i, c_i = _position()
    dest_idx = jnp.stack([4 * px + 2 * py + c_i for px, py in _relations(x_i, y_i)]).astype(jnp.int32)

    params, saved = [], []
    xc = x2d
    for l in range(depth):
        pr = _layer_params(_unpack_full(gathered, big), w, l)
        params.append(pr)
        x0 = xc
        x1, a1, u1 = _ffn_fwd(x0, *pr["ffn1"])
        qa, ka, va, qb, kb, vb = _prep_fwd(x1, cos, sin_s, pr["prep"])
        (oa, lse_a), side_out = _mla_fwd(qa, ka, va, _side_gather(packed[l + 1]) if l + 1 < depth else None)
        if side_out:
            gathered = side_out[0]
        ob, lse_b = _swa_fwd(qb, kb, vb, pr["sinks"])
        x2 = _mix_out_fwd(x1, oa, ob, pr["ga"], pr["gb"], pr["wo_a"], pr["wo_b"])
        x3, a2, u2 = _ffn_fwd(x2, *pr["ffn2"])
        saved.append((x0, a1, u1, x1, qa, ka, va, qb, kb, vb, oa, lse_a, ob, lse_b, x2, a2, u2))
        xc = x3

    dx, sq_err = _loss_head(xc, target)
    loss = lax.psum(0.5 / d * sq_err[0, 0], MESH_AXES)

    grad_shards = [None] * depth
    small_grads = {n: [None] * depth for n in SMALL}
    pending = None
    for l in reversed(range(depth)):
        pr = params[l]
        x0, a1, u1, x1, qa, ka, va, qb, kb, vb, oa, lse_a, ob, lse_b, x2, a2, u2 = saved[l]
        g = {}
        dx, small_grads["ffn2_norm"][l], g["ffn2_w_gate"], g["ffn2_w_up"], g["ffn2_w_down"], side_out = _ffn_backward(
            x2, dx, a2, u2, pr["ffn2"], "ffn2", _side_sibling(pending[1]) if pending else None)
        if pending:
            own, wire = _rs_chip_sums(pending[1], side_out[0], dest_idx)

        doa, dsum_a, dob, dsum_b, mixed, dyb, dga, dgb = _mix_out_bwd(
            dx, oa, ob, pr["ga"], pr["gb"], pr["wo_a"], pr["wo_b"])
        small_grads["mla_out_norm"][l] = dga.reshape(-1)
        small_grads["swa_out_norm"][l] = dgb.reshape(-1)
        g["w_o"] = _tn_matmul(mixed, dyb, "wgrad_wo")

        rows_of = lambda s: s[:, :, 0].reshape(MLA_HEADS, t // bq, 1, bq)
        (dqa, dka, dva), side_out = _mla_bwd(qa, ka, va, doa, rows_of(lse_a), rows_of(dsum_a),
                                             _side_chips(wire) if pending else None)
        if pending:
            grad_shards[pending[0]] = _unpack_shards(_rs_sum(own, side_out[0]), big)
        dqb, dkb, dvb, dsinks = _swa_bwd(qb, kb, vb, pr["sinks"], dob, lse_b, dsum_b)
        small_grads["swa_sinks"][l] = dsinks[:, 0, 0]

        outs = _prep_bwd(x1, dx, cos, sin_s, pr["prep"], dqa, dka, dva, dqb, dkb, dvb)
        dx = outs[0]
        pg = dict(zip(PREP_WEIGHTS, outs[1:]))
        g["w_in"] = jnp.concatenate([pg["w_in"][:W_IN_COLS[0]], pg["w_in"][C_QS:]], axis=0)
        g["mla_w_q_b"] = pg["wqb"][:, :MLA_QK].reshape(MLA_HEADS * MLA_QK, MLA_Q_RANK)
        g["mla_w_kv_b"] = pg["w_kvb"]
        fold = lambda gg: gg[0, :HALF] + gg[0, HALF:]
        small_grads["mix_norm"][l] = pg["mix_g"][0]
        small_grads["mla_q_a_norm"][l] = pg["g_qa"][0]
        small_grads["mla_kv_a_norm"][l] = pg["g_kva"][0]
        small_grads["mla_q_norm"][l] = jnp.concatenate([pg["gq_n"][0], pg["gq_r"][0, :MLA_ROPE]])
        small_grads["mla_k_norm"][l] = jnp.concatenate([pg["gk_n"][0], pg["gk_r"][0, :MLA_ROPE]])
        small_grads["swa_q_norm"][l] = fold(pg["g_sq"])
        small_grads["swa_k_norm"][l] = fold(pg["g_sk"])

        dx, small_grads["ffn1_norm"][l], g["ffn1_w_gate"], g["ffn1_w_up"], g["ffn1_w_down"], _ = _ffn_backward(
            x0, dx, a1, u1, pr["ffn1"], "ffn1")
        pending = (l, _pack_by_destination(g, d))

    own, wire = _rs_chip_sums(pending[1], _rs_sibling_exchange(pending[1]), dest_idx)
    grad_shards[pending[0]] = _unpack_shards(_rs_sum(own, _rs_chip_exchange(wire)), big)

    grad_big, delta_big, new_m_big, new_v_big = {}, {}, {}, {}
    for n in BIG:
        grad_big[n] = jnp.stack([grad_shards[l][n] for l in range(depth)])
        delta_big[n], new_m_big[n], new_v_big[n] = _adamw(grad_big[n], w[n], m[n], v[n], n)

    small_partial = _pack_small({n: jnp.stack(small_grads[n]) for n in SMALL})
    g_s = _all_reduce_small(small_partial)
    d_s, m_s, v_s = _adamw_small(g_s, _pack_small(w), _pack_small(m), _pack_small(v))
    grad_small, delta_small, new_m_small, new_v_small = (_unpack_small(a, w) for a in (g_s, d_s, m_s, v_s))

    def ordered(big, small):
        return [big[n] if n in big else small[n] for n in WEIGHTS]

    return (loss, dx.reshape(x.shape), *ordered(grad_big, grad_small), *ordered(delta_big, delta_small),
            *ordered(new_m_big, new_m_small), *ordered(new_v_big, new_v_small))
```

```python
import jax
import jax.numpy as jnp
from jax import lax
from jax.experimental import pallas as pl
from jax.experimental.pallas import tpu as pltpu

N_DEV = 8
EPS = 1e-6
ROPE_THETA = 10000.0
MLA_HEADS = 4
MLA_Q_RANK = 256
MLA_KV_RANK = 128
MLA_NOPE = 128
MLA_ROPE = 64
MLA_V = 128
MLA_QK = MLA_NOPE + MLA_ROPE
MLA_QK_PAD = 256
SWA_HEADS = 8
SWA_KV = 2
SWA_GROUP = SWA_HEADS // SWA_KV
SWA_D = 64
SWA_BLOCK = 128
ADAM_LR = 0.001
ADAM_B1 = 0.9
ADAM_B2 = 0.999
ADAM_EPS = 1e-08
ADAM_WD = 0.01
ADAM_STEP = 10

LANES = 128
HALF = LANES // 2
SWA_PAIRS = SWA_HEADS // 2
W_IN_COLS = (MLA_Q_RANK + MLA_KV_RANK + MLA_ROPE, SWA_HEADS * SWA_D + 2 * SWA_KV * SWA_D)
VMEM_LIMIT = 56 * 1024 * 1024

_MXU = jnp.bfloat16
_TB = 256
_TB_MIX = 512
_BQ = 512
_STRIP = 32
_TK = 512
_SWA_STEP = 4
RS_ROW_BLOCKS = 2

BIG = ("ffn1_w_gate", "ffn1_w_up", "ffn1_w_down", "w_in", "mla_w_q_b", "mla_w_kv_b", "w_o",
       "ffn2_w_gate", "ffn2_w_up", "ffn2_w_down")
ROW_SHARDED = ("ffn1_w_down", "w_o", "ffn2_w_down")
FFN_BIG = ("ffn1_w_gate", "ffn1_w_up", "ffn1_w_down", "ffn2_w_gate", "ffn2_w_up", "ffn2_w_down")
OTHER_BIG = ("w_o", "w_in", "mla_w_q_b", "mla_w_kv_b")
SMALL = ("ffn1_norm", "mix_norm", "mla_q_a_norm", "mla_kv_a_norm", "mla_q_norm", "mla_k_norm",
         "swa_q_norm", "swa_k_norm", "swa_sinks", "mla_out_norm", "swa_out_norm", "ffn2_norm")
WEIGHTS = ("ffn1_norm", "ffn1_w_gate", "ffn1_w_up", "ffn1_w_down", "mix_norm", "w_in", "mla_q_a_norm",
           "mla_w_q_b", "mla_kv_a_norm", "mla_w_kv_b", "mla_q_norm", "mla_k_norm", "swa_q_norm",
           "swa_k_norm", "swa_sinks", "mla_out_norm", "swa_out_norm", "w_o", "ffn2_norm",
           "ffn2_w_gate", "ffn2_w_up", "ffn2_w_down")
MESH_AXES = ("x", "y", "c")
MESH = pl.DeviceIdType.MESH
NEG = -1e30


def _f32(t):
    return t.astype(jnp.float32)


def _mx(t):
    return t.astype(_MXU)


def _dot(a, b):
    return jnp.dot(a, b, preferred_element_type=jnp.float32)


def _dot_nt(a, b):
    return lax.dot_general(a, b, (((1,), (1,)), ((), ())), preferred_element_type=jnp.float32)


def _dot_tn(a, b):
    return lax.dot_general(a, b, (((0,), (0,)), ((), ())), preferred_element_type=jnp.float32)


def _rsq(ss, n):
    return lax.rsqrt(ss * (1.0 / n) + EPS)


def _sumsq(t):
    return jnp.sum(t * t, axis=-1, keepdims=True)


def _rowsum(t):
    return jnp.sum(t, axis=-1, keepdims=True)


def _rowmax(t):
    return jnp.max(t, axis=-1, keepdims=True)


def _colsum(t):
    return jnp.sum(t, axis=0, keepdims=True)


def _lane():
    return lax.broadcasted_iota(jnp.int32, (1, LANES), 1)


def _low_half():
    return _lane() < HALF


def _swap32(t):
    return jnp.where((_lane() & 32) == 0, pltpu.roll(t, 96, 1), pltpu.roll(t, 32, 1))


def _rope(t, cos, sin_signed):
    return t * cos + _swap32(t) * sin_signed


def _rope_bwd(d, cos, sin_signed):
    return d * cos + _swap32(d * sin_signed)


def _half_sums(t):
    low = _low_half()
    return jnp.where(low, _rowsum(jnp.where(low, t, 0.0)), _rowsum(jnp.where(low, 0.0, t)))


def _dup_halves(pair):
    low = _low_half()
    swapped = pltpu.roll(pair, HALF, 1)
    return jnp.where(low, pair, swapped), jnp.where(low, swapped, pair)


def _undup_halves(d0, d1):
    return jnp.where(_low_half(), d0 + pltpu.roll(d0, HALF, 1), d1 + pltpu.roll(d1, HALF, 1))


def _pick_halves(a, b):
    return jnp.where(_low_half(), a, b)


def _norm_bwd(dn_list, xh_list, r, n):
    c = sum(_rowsum(dn * xh) for dn, xh in zip(dn_list, xh_list)) * (1.0 / n)
    return [r * (dn - xh * c) for dn, xh in zip(dn_list, xh_list)]


def _cparams(semantics):
    return pltpu.CompilerParams(dimension_semantics=semantics, vmem_limit_bytes=VMEM_LIMIT)


def _const(shape):
    nd = len(shape)
    return pl.BlockSpec(shape, lambda *_: (0,) * nd, pipeline_mode=pl.Buffered(1))


def _acc(shape):
    nd = len(shape)
    return pl.BlockSpec(shape, lambda *_: (0,) * nd)


def _rows(tb, width):
    return pl.BlockSpec((tb, width), lambda i: (i, 0))


def _heads_rows(h, tb, width):
    return pl.BlockSpec((h, tb, width), lambda i: (0, i, 0))


def _sds(shape, dtype):
    return jax.ShapeDtypeStruct(shape, dtype)


def _position():
    return lax.axis_index("x"), lax.axis_index("y"), lax.axis_index("c")


def _all_gather(xp):
    def body(x_ref, out_ref, send_sems, recv_sems, local_sem):
        x, y, c = _position()
        me, sibling = (x, y, c), (x, y, 1 - c)
        chips = [(1 - x, y), (x, 1 - y), (1 - x, 1 - y)]

        def rows(px, py, pc):
            return out_ref.at[4 * px + 2 * py + pc]

        def copy(k, block, to, src=None):
            return pltpu.make_async_remote_copy(
                src_ref=rows(*block) if src is None else src, dst_ref=rows(*block),
                send_sem=send_sems.at[k], recv_sem=recv_sems.at[k], device_id=to, device_id_type=MESH)

        mine = pltpu.make_async_copy(x_ref, rows(*me), local_sem)
        mine.start()
        first = [copy(0, me, sibling, src=x_ref)]
        first += [copy(1 + j, me, (*chip, c), src=x_ref) for j, chip in enumerate(chips)]
        for cp in first:
            cp.start()
        passed = [copy(4 + j, (*chip, c), sibling) for j, chip in enumerate(chips)]
        for j, chip in enumerate(chips):
            copy(1 + j, (*chip, c), me).wait_recv()
            passed[j].start()
        copy(0, sibling, me).wait_recv()
        for j, chip in enumerate(chips):
            copy(4 + j, (*chip, 1 - c), me).wait_recv()
        for cp in first + passed:
            cp.wait_send()
        mine.wait()

    return pl.pallas_call(
        body, name="ag_weights",
        out_shape=_sds((N_DEV,) + xp.shape, xp.dtype),
        in_specs=[pl.BlockSpec(memory_space=pl.ANY)],
        out_specs=pl.BlockSpec(memory_space=pl.ANY),
        scratch_shapes=[pltpu.SemaphoreType.DMA((7,)), pltpu.SemaphoreType.DMA((7,)), pltpu.SemaphoreType.DMA],
    )(xp)


def _relations(x, y):
    return [(x, y), (1 - x, y), (x, 1 - y), (1 - x, 1 - y)]


def _remote(src, dst, send_sem, recv_sem, device):
    return pltpu.make_async_remote_copy(src_ref=src, dst_ref=dst, send_sem=send_sem, recv_sem=recv_sem,
                                        device_id=device, device_id_type=MESH)


def _sibling_copies(g_refs, out_refs, send, recv):
    x, y, c = _position()
    n = len(g_refs)
    return [_remote(g.at[4 * px + 2 * py + (1 - c)], o.at[k], send.at[k * n + i], recv.at[k * n + i], (x, y, 1 - c))
            for k, (px, py) in enumerate(_relations(x, y)) for i, (g, o) in enumerate(zip(g_refs, out_refs))]


def _chip_copies(w_refs, out_refs, send, recv):
    x, y, c = _position()
    n = len(w_refs)
    return [_remote(w.at[k + 1], o.at[k], send.at[k * n + i], recv.at[k * n + i], (px, py, c))
            for k, (px, py) in enumerate(_relations(x, y)[1:]) for i, (w, o) in enumerate(zip(w_refs, out_refs))]


def _exchange(arrays, lead, relations, copies_fn, name):
    n = len(arrays)

    def body(*refs):
        copies = copies_fn(refs[:n], refs[n:2 * n], refs[2 * n], refs[2 * n + 1])
        for cp in copies:
            cp.start()
        for cp in copies:
            cp.wait()

    hbm = pl.BlockSpec(memory_space=pl.ANY)
    dma = pltpu.SemaphoreType.DMA
    return pl.pallas_call(
        body, name=name, out_shape=[_sds((lead,) + a.shape[1:], a.dtype) for a in arrays],
        in_specs=[hbm] * n, out_specs=[hbm] * n,
        scratch_shapes=[dma((relations * n,)), dma((relations * n,))],
    )(*arrays)


def _rs_sibling_exchange(gs):
    return _exchange(gs, 4, 4, _sibling_copies, "rs_sibling_exchange")


def _rs_chip_exchange(wires):
    return _exchange(wires, 3, 3, _chip_copies, "rs_chip_exchange")


def _side_exchange(arrays, lead, relations, copies_fn):
    shapes = [_sds((lead,) + a.shape[1:], a.dtype) for a in arrays]
    return list(arrays), shapes, relations * len(arrays), lambda ins, outs, send, recv, local: copies_fn(ins, outs, send, recv)


def _side_sibling(gs):
    return _side_exchange(gs, 4, 4, _sibling_copies)


def _side_chips(wires):
    return _side_exchange(wires, 3, 3, _chip_copies)


def _rs_chip_sums(gs, sibs, dest_idx):
    n = len(gs)

    def body(idx_ref, *refs):
        g_refs, s_refs, own_refs, wire_refs = refs[:n], refs[n:2 * n], refs[2 * n:3 * n], refs[3 * n:]
        totals = [g[0] + s[0] for g, s in zip(g_refs, s_refs)]
        for total, wire in zip(totals, wire_refs):
            wire[0] = total.astype(wire.dtype)

        @pl.when(pl.program_id(1) == 0)
        def _():
            for total, own in zip(totals, own_refs):
                own[...] = total

    def blocks(a, index_map, squeeze):
        rb = a.shape[1] // RS_ROW_BLOCKS
        return pl.BlockSpec((rb, a.shape[2]) if squeeze else (1, rb, a.shape[2]), index_map)

    return pl.pallas_call(
        body, name="rs_chip_sums",
        grid_spec=pltpu.PrefetchScalarGridSpec(
            num_scalar_prefetch=1, grid=(RS_ROW_BLOCKS, 4),
            in_specs=[blocks(g, lambda r, k, idx: (idx[k], r, 0), False) for g in gs]
            + [blocks(g, lambda r, k, idx: (k, r, 0), False) for g in gs],
            out_specs=[blocks(g, lambda r, k, idx: (r, 0), True) for g in gs]
            + [blocks(g, lambda r, k, idx: (k, r, 0), False) for g in gs]),
        out_shape=[_sds(g.shape[1:], jnp.float32) for g in gs] + [_sds((4,) + g.shape[1:], _MXU) for g in gs],
        compiler_params=_cparams(("parallel", "arbitrary")),
    )(dest_idx, *gs, *sibs)


def _side_gather(xp):
    def make(ins, outs, send, recv, local):
        (x_ref,), (out_ref,) = ins, outs
        x, y, c = _position()
        me = 4 * x + 2 * y + c
        copies = [pltpu.make_async_copy(x_ref, out_ref.at[me], local.at[0])]
        for k in range(1, N_DEV):
            peer = (1 - x if k & 4 else x, 1 - y if k & 2 else y, 1 - c if k & 1 else c)
            copies.append(_remote(x_ref, out_ref.at[me], send.at[k - 1], recv.at[k - 1], peer))
        return copies

    return [xp], [_sds((N_DEV,) + xp.shape, xp.dtype)], N_DEV - 1, make


def _call(body, args, side, *, name, grid, in_specs, out_specs, out_shape, scratch_shapes=(), semantics):
    in_specs, out_specs, out_shape = list(in_specs), list(out_specs), list(out_shape)
    n_in, n_out, n_scr = len(in_specs), len(out_specs), len(scratch_shapes)
    if side is None:
        outs = pl.pallas_call(body, name=name, grid=grid, in_specs=in_specs, out_specs=out_specs, out_shape=out_shape,
                              scratch_shapes=list(scratch_shapes), compiler_params=_cparams(semantics))(*args)
        return list(outs), []
    arrays, shapes, n_remote, make = side
    n_side_in, n_side_out = len(arrays), len(shapes)
    hbm = pl.BlockSpec(memory_space=pl.ANY)

    def with_copies(*refs):
        main_in, refs = refs[:n_in], refs[n_in:]
        side_in, refs = refs[:n_side_in], refs[n_side_in:]
        main_out, refs = refs[:n_out], refs[n_out:]
        side_out, refs = refs[:n_side_out], refs[n_side_out:]
        main_scr, (send, recv, local) = refs[:n_scr], refs[n_scr:]
        copies = make(side_in, side_out, send, recv, local)
        ids = [pl.program_id(a) for a in range(len(grid))]
        first, last = ids[0] == 0, ids[0] == grid[0] - 1
        for i, size in zip(ids[1:], grid[1:]):
            first, last = jnp.logical_and(first, i == 0), jnp.logical_and(last, i == size - 1)

        @pl.when(first)
        def _():
            for cp in copies:
                cp.start()

        body(*main_in, *main_out, *main_scr)

        @pl.when(last)
        def _():
            for cp in copies:
                cp.wait()

    dma = pltpu.SemaphoreType.DMA
    outs = pl.pallas_call(
        with_copies, name=name, grid=grid, in_specs=in_specs + [hbm] * n_side_in,
        out_specs=out_specs + [hbm] * n_side_out, out_shape=out_shape + list(shapes),
        scratch_shapes=list(scratch_shapes) + [dma((n_remote,)), dma((n_remote,)), dma((1,))],
        compiler_params=_cparams(("arbitrary",) * len(grid)),
    )(*args, *arrays)
    return list(outs[:n_out]), list(outs[n_out:])


def _all_reduce_small(v):
    rows_n = v.shape[0]

    def body(v_ref, out_ref, buf, send_sems, recv_sems):
        x, y, c = _position()
        me = 4 * x + 2 * y + c
        buf[me] = v_ref[...]
        copies = []
        for k in range(1, N_DEV):
            px = 1 - x if k & 4 else x
            py = 1 - y if k & 2 else y
            pc = 1 - c if k & 1 else c
            copies.append(pltpu.make_async_remote_copy(
                src_ref=v_ref, dst_ref=buf.at[me],
                send_sem=send_sems.at[k - 1], recv_sem=recv_sems.at[k - 1], device_id=(px, py, pc), device_id_type=MESH))
        for cp in copies:
            cp.start()
        for cp in copies:
            cp.wait()
        total = buf[0]
        for d in range(1, N_DEV):
            total = total + buf[d]
        out_ref[...] = total

    return pl.pallas_call(
        body, name="ar_small",
        out_shape=_sds((rows_n, LANES), jnp.float32),
        in_specs=[pl.BlockSpec(memory_space=pltpu.VMEM)],
        out_specs=pl.BlockSpec(memory_space=pltpu.VMEM),
        scratch_shapes=[pltpu.VMEM((N_DEV, rows_n, LANES), jnp.float32),
                        pltpu.SemaphoreType.DMA((N_DEV - 1,)), pltpu.SemaphoreType.DMA((N_DEV - 1,))],
    )(v)


def _adamw_math(w, g, m, v):
    m = ADAM_B1 * m + (1.0 - ADAM_B1) * g
    v = ADAM_B2 * v + (1.0 - ADAM_B2) * (g * g)
    m_hat = m / (1.0 - ADAM_B1 ** ADAM_STEP)
    v_hat = v / (1.0 - ADAM_B2 ** ADAM_STEP)
    delta = -ADAM_LR * (m_hat / (jnp.sqrt(v_hat) + ADAM_EPS) + ADAM_WD * w)
    return delta, m, v


def _rs_sum(owns, recvs):
    n = len(owns)

    def body(*refs):
        own_refs, recv_refs, out_refs = refs[:n], refs[n:4 * n], refs[4 * n:]
        for i in range(n):
            r0, r1, r2 = recv_refs[3 * i:3 * i + 3]
            out_refs[i][...] = ((own_refs[i][...] + _f32(r0[0])) + _f32(r1[0])) + _f32(r2[0])

    def row(a):
        return pl.BlockSpec((a.shape[0] // RS_ROW_BLOCKS, a.shape[1]), lambda r: (r, 0))

    def slot(a, k):
        return pl.BlockSpec((1, a.shape[0] // RS_ROW_BLOCKS, a.shape[1]), lambda r: (k, r, 0))

    return pl.pallas_call(
        body, name="rs_sum", grid=(RS_ROW_BLOCKS,),
        in_specs=[row(a) for a in owns] + [slot(a, k) for a in owns for k in range(3)],
        out_specs=[row(a) for a in owns],
        out_shape=[_sds(a.shape, jnp.float32) for a in owns],
        compiler_params=_cparams(("parallel",)),
    )(*owns, *[r for r in recvs for _ in range(3)])


def _adamw(g, w, m, v, name):
    depth, a, b = w.shape

    def body(g_ref, w_ref, m_ref, v_ref, d_out, m_out, v_out):
        delta, m2, v2 = _adamw_math(w_ref[...], g_ref[...], m_ref[...], v_ref[...])
        d_out[...] = delta
        m_out[...] = m2
        v_out[...] = v2

    layer = pl.BlockSpec((1, a, b), lambda l: (l, 0, 0))
    return pl.pallas_call(
        body, name="adamw_" + name, grid=(depth,),
        in_specs=[layer] * 4, out_specs=[layer] * 3,
        out_shape=[_sds(w.shape, jnp.float32)] * 3,
        compiler_params=_cparams(("parallel",)),
    )(g, w, m, v)


def _adamw_small(g, w, m, v):
    def body(g_ref, w_ref, m_ref, v_ref, d_out, m_out, v_out):
        delta, m2, v2 = _adamw_math(w_ref[...], g_ref[...], m_ref[...], v_ref[...])
        d_out[...] = delta
        m_out[...] = m2
        v_out[...] = v2

    vm = pl.BlockSpec(memory_space=pltpu.VMEM)
    return pl.pallas_call(
        body, name="adamw_small",
        in_specs=[vm] * 4, out_specs=[vm] * 3,
        out_shape=[_sds(g.shape, jnp.float32)] * 3,
    )(g, w, m, v)


def _f_chunk(f):
    for cand in (1408, 1024, 512, 256, 128):
        if f % cand == 0:
            return cand
    return f


def _ffn_fwd(x, gain, wg, wu, wd):
    t, d = x.shape
    f = wg.shape[0]
    tb = min(_TB, t)
    fc = _f_chunk(f)

    def body(x_ref, g_ref, wg_ref, wu_ref, wd_ref, xo_ref, a_ref, u_ref):
        xv = x_ref[...]
        hb = _mx(xv * _rsq(_sumsq(xv), d) * g_ref[...])
        y = jnp.zeros((tb, d), jnp.float32)
        for c0 in range(0, f, fc):
            a = _dot_nt(hb, wg_ref[c0:c0 + fc, :])
            u = _dot_nt(hb, wu_ref[c0:c0 + fc, :])
            a_ref[:, c0:c0 + fc] = a.astype(a_ref.dtype)
            u_ref[:, c0:c0 + fc] = u.astype(u_ref.dtype)
            s = a * jax.nn.sigmoid(a) * u
            y = y + _dot(_mx(s), wd_ref[c0:c0 + fc, :])
        xo_ref[...] = xv + 0.5 * y

    return pl.pallas_call(
        body, name="ffn_fwd", grid=(t // tb,),
        in_specs=[_rows(tb, d), _const((1, d)), _const((f, d)), _const((f, d)), _const((f, d))],
        out_specs=[_rows(tb, d), _rows(tb, f), _rows(tb, f)],
        out_shape=[_sds((t, d), jnp.float32), _sds((t, f), _MXU), _sds((t, f), _MXU)],
        compiler_params=_cparams(("parallel",)),
    )(x, gain, wg, wu, wd)


def _ffn_dgrad(x, gain, dxo, a, u, wg, wu, wd, side=None):
    t, d = x.shape
    f = wg.shape[0]
    tb = min(_TB, t)
    fc = _f_chunk(f)

    def body(x_ref, g_ref, dxo_ref, a_ref, u_ref, wg_ref, wu_ref, wd_ref,
             dxi_ref, da_ref, du_ref, s_ref, h_ref, dy_ref, dg_ref):
        xv = x_ref[...]
        gv = g_ref[...]
        r = _rsq(_sumsq(xv), d)
        xhat = xv * r
        h_ref[...] = _mx(xhat * gv)
        dxo = dxo_ref[...]
        dyb = _mx(0.5 * dxo)
        dy_ref[...] = dyb
        dh = jnp.zeros((tb, d), jnp.float32)
        for c0 in range(0, f, fc):
            ds = _dot_nt(dyb, wd_ref[c0:c0 + fc, :])
            av = _f32(a_ref[:, c0:c0 + fc])
            uv = _f32(u_ref[:, c0:c0 + fc])
            sig = jax.nn.sigmoid(av)
            silu = av * sig
            da = _mx(ds * uv * (sig * (1.0 + av * (1.0 - sig))))
            du = _mx(ds * silu)
            s_ref[:, c0:c0 + fc] = _mx(silu * uv)
            da_ref[:, c0:c0 + fc] = da
            du_ref[:, c0:c0 + fc] = du
            dh = dh + _dot(da, wg_ref[c0:c0 + fc, :]) + _dot(du, wu_ref[c0:c0 + fc, :])

        @pl.when(pl.program_id(0) == 0)
        def _():
            dg_ref[...] = jnp.zeros_like(dg_ref)

        dg_ref[...] += _colsum(dh * xhat)
        dn = dh * gv
        dxi_ref[...] = dxo + r * (dn - xhat * (_rowsum(dn * xhat) * (1.0 / d)))

    return _call(
        body, (x, gain, dxo, a, u, wg, wu, wd), side, name="ffn_dgrad", grid=(t // tb,),
        in_specs=[_rows(tb, d), _const((1, d)), _rows(tb, d), _rows(tb, f), _rows(tb, f),
                  _const((f, d)), _const((f, d)), _const((f, d))],
        out_specs=[_rows(tb, d), _rows(tb, f), _rows(tb, f), _rows(tb, f), _rows(tb, d), _rows(tb, d),
                   _acc((1, d))],
        out_shape=[_sds((t, d), jnp.float32), _sds((t, f), _MXU), _sds((t, f), _MXU), _sds((t, f), _MXU),
                   _sds((t, d), _MXU), _sds((t, d), _MXU), _sds((1, d), jnp.float32)],
        semantics=("arbitrary",))


def _tn_matmul(a, b, name):
    t, m = a.shape
    n = b.shape[1]
    tk = min(_TK, t)
    tn = n
    while m * tn * 4 > 12 * 1024 * 1024 and tn % 256 == 0:
        tn //= 2

    def body(a_ref, b_ref, o_ref):
        @pl.when(pl.program_id(1) == 0)
        def _():
            o_ref[...] = jnp.zeros_like(o_ref)

        o_ref[...] += _dot_tn(a_ref[...], b_ref[...])

    return pl.pallas_call(
        body, name=name, grid=(n // tn, t // tk),
        in_specs=[pl.BlockSpec((tk, m), lambda j, k: (k, 0)), pl.BlockSpec((tk, tn), lambda j, k: (k, j))],
        out_specs=pl.BlockSpec((m, tn), lambda j, k: (0, j)),
        out_shape=_sds((m, n), jnp.float32),
        compiler_params=_cparams(("parallel", "arbitrary")),
    )(a, b)


PREP_WEIGHTS = ("mix_g", "w_in", "g_qa", "wqb", "g_kva", "w_kvb", "gq_n", "gq_r", "gk_n", "gk_r", "g_sq", "g_sk")
C_CQ, C_CKV, C_KPE, C_QS = 0, MLA_Q_RANK, MLA_Q_RANK + MLA_KV_RANK, MLA_Q_RANK + MLA_KV_RANK + LANES
C_KS = C_QS + SWA_HEADS * SWA_D
C_VS = C_KS + LANES
W_IN_PACKED = C_VS + LANES


def _prep_specs(p):
    return [_const(p[n].shape) for n in PREP_WEIGHTS]


def _pair_norm_rope(t, gain, cos, sin_s):
    return _rope(t * _rsq(_half_sums(t * t), SWA_D) * gain, cos, sin_s)


def _prep_fwd(x, cos, sin_s, p):
    t, d = x.shape
    tb = min(_TB_MIX, t)

    def body(x_ref, cos_ref, sin_ref, mix_g, w_in, g_qa, wqb, g_kva, w_kvb, gq_n, gq_r, gk_n, gk_r, g_sq, g_sk,
             qa_ref, ka_ref, va_ref, qb_ref, kb_ref, vb_ref):
        xv = x_ref[...]
        cos_v, sin_v = cos_ref[...], sin_ref[...]
        hb = _mx(xv * _rsq(_sumsq(xv), d) * mix_g[...])
        proj = _dot_nt(hb, w_in[...])
        cq = proj[:, C_CQ:C_CKV]
        cqn = _mx(cq * _rsq(_sumsq(cq), MLA_Q_RANK) * g_qa[...])
        for h in range(MLA_HEADS):
            qh = _dot_nt(cqn, wqb[h])
            qn, qr = qh[:, :MLA_NOPE], qh[:, MLA_NOPE:]
            rh = _rsq(_sumsq(qn) + _sumsq(qr), MLA_QK)
            qa_ref[h, :, 0:MLA_NOPE] = (qn * rh * gq_n[...]).astype(qa_ref.dtype)
            qa_ref[h, :, MLA_NOPE:MLA_QK_PAD] = _rope(qr * rh * gq_r[...], cos_v, sin_v).astype(qa_ref.dtype)
        ckv = proj[:, C_CKV:C_KPE]
        ckvn = _mx(ckv * _rsq(_sumsq(ckv), MLA_KV_RANK) * g_kva[...])
        kpe = proj[:, C_KPE:C_QS]
        ss_pe = _sumsq(kpe)
        kv = _dot_nt(ckvn, w_kvb[...])
        for h in range(MLA_HEADS):
            c0 = h * (MLA_NOPE + MLA_V)
            kn = kv[:, c0:c0 + MLA_NOPE]
            rh = _rsq(_sumsq(kn) + ss_pe, MLA_QK)
            ka_ref[h, :, 0:MLA_NOPE] = (kn * rh * gk_n[...]).astype(ka_ref.dtype)
            ka_ref[h, :, MLA_NOPE:MLA_QK_PAD] = _rope(kpe * rh * gk_r[...], cos_v, sin_v).astype(ka_ref.dtype)
            va_ref[h] = kv[:, c0 + MLA_NOPE:c0 + MLA_NOPE + MLA_V].astype(va_ref.dtype)
        for j in range(SWA_PAIRS):
            c0 = C_QS + j * LANES
            qb_ref[j] = _pair_norm_rope(proj[:, c0:c0 + LANES], g_sq[...], cos_v, sin_v).astype(qb_ref.dtype)
        k0, k1 = _dup_halves(_pair_norm_rope(proj[:, C_KS:C_VS], g_sk[...], cos_v, sin_v))
        kb_ref[0] = k0.astype(kb_ref.dtype)
        kb_ref[1] = k1.astype(kb_ref.dtype)
        v0, v1 = _dup_halves(proj[:, C_VS:W_IN_PACKED])
        vb_ref[0] = v0.astype(vb_ref.dtype)
        vb_ref[1] = v1.astype(vb_ref.dtype)

    return pl.pallas_call(
        body, name="prep_fwd", grid=(t // tb,),
        in_specs=[_rows(tb, d), _rows(tb, LANES), _rows(tb, LANES)] + _prep_specs(p),
        out_specs=[_heads_rows(MLA_HEADS, tb, MLA_QK_PAD), _heads_rows(MLA_HEADS, tb, MLA_QK_PAD),
                   _heads_rows(MLA_HEADS, tb, MLA_V), _heads_rows(SWA_PAIRS, tb, LANES),
                   _heads_rows(SWA_KV, tb, LANES), _heads_rows(SWA_KV, tb, LANES)],
        out_shape=[_sds((MLA_HEADS, t, MLA_QK_PAD), _MXU), _sds((MLA_HEADS, t, MLA_QK_PAD), _MXU),
                   _sds((MLA_HEADS, t, MLA_V), _MXU), _sds((SWA_PAIRS, t, LANES), _MXU),
                   _sds((SWA_KV, t, LANES), _MXU), _sds((SWA_KV, t, LANES), _MXU)],
        compiler_params=_cparams(("parallel",)),
    )(x, cos, sin_s, *[p[n] for n in PREP_WEIGHTS])


def _prep_bwd(x, dxin, cos, sin_s, p, dqa, dka, dva, dqb, dkb, dvb):
    t, d = x.shape
    tb = min(_TB_MIX, t)
    n_w = len(PREP_WEIGHTS)

    def body(*refs):
        x_ref, dxin_ref, cos_ref, sin_ref = refs[:4]
        mix_g, w_in, g_qa, wqb, g_kva, w_kvb, gq_n, gq_r, gk_n, gk_r, g_sq, g_sk = refs[4:4 + n_w]
        dqa_ref, dka_ref, dva_ref, dqb_ref, dkb_ref, dvb_ref = refs[4 + n_w:10 + n_w]
        dx_ref = refs[10 + n_w]
        grads = dict(zip(PREP_WEIGHTS, refs[11 + n_w:11 + 2 * n_w]))
        dproj_ref, dkv_ref, dqh_ref = refs[11 + 2 * n_w:]

        @pl.when(pl.program_id(0) == 0)
        def _():
            for ref in grads.values():
                ref[...] = jnp.zeros_like(ref)

        xv = x_ref[...]
        cos_v, sin_v = cos_ref[...], sin_ref[...]
        r0 = _rsq(_sumsq(xv), d)
        xhat = xv * r0
        hb = _mx(xhat * mix_g[...])
        proj = _dot_nt(hb, w_in[...])

        cq = proj[:, C_CQ:C_CKV]
        rq = _rsq(_sumsq(cq), MLA_Q_RANK)
        cqh = cq * rq
        cqn = _mx(cqh * g_qa[...])
        dcqn = jnp.zeros((tb, MLA_Q_RANK), jnp.float32)
        for h in range(MLA_HEADS):
            qh = _dot_nt(cqn, wqb[h])
            qn, qr = qh[:, :MLA_NOPE], qh[:, MLA_NOPE:]
            rh = _rsq(_sumsq(qn) + _sumsq(qr), MLA_QK)
            xh_n, xh_r = qn * rh, qr * rh
            dy_n = dqa_ref[h, :, 0:MLA_NOPE]
            dy_r = _rope_bwd(dqa_ref[h, :, MLA_NOPE:MLA_QK_PAD], cos_v, sin_v)
            grads["gq_n"][...] += _colsum(dy_n * xh_n)
            grads["gq_r"][...] += _colsum(dy_r * xh_r)
            dqn, dqr = _norm_bwd([dy_n * gq_n[...], dy_r * gq_r[...]], [xh_n, xh_r], rh, MLA_QK)
            dqh_ref[:, 0:MLA_NOPE] = _mx(dqn)
            dqh_ref[:, MLA_NOPE:MLA_QK_PAD] = _mx(dqr)
            dqh = dqh_ref[...]
            grads["wqb"][h] += _dot_tn(dqh, cqn)
            dcqn = dcqn + _dot(dqh, wqb[h])
        grads["g_qa"][...] += _colsum(dcqn * cqh)
        (dcq,) = _norm_bwd([dcqn * g_qa[...]], [cqh], rq, MLA_Q_RANK)
        dproj_ref[:, C_CQ:C_CKV] = _mx(dcq)

        ckv = proj[:, C_CKV:C_KPE]
        rkv = _rsq(_sumsq(ckv), MLA_KV_RANK)
        ckvh = ckv * rkv
        ckvn = _mx(ckvh * g_kva[...])
        kpe = proj[:, C_KPE:C_QS]
        ss_pe = _sumsq(kpe)
        kv = _dot_nt(ckvn, w_kvb[...])
        dkpe = jnp.zeros((tb, LANES), jnp.float32)
        for h in range(MLA_HEADS):
            c0 = h * (MLA_NOPE + MLA_V)
            c1 = c0 + MLA_NOPE
            kn = kv[:, c0:c1]
            rh = _rsq(_sumsq(kn) + ss_pe, MLA_QK)
            xh_n, xh_r = kn * rh, kpe * rh
            dy_n = dka_ref[h, :, 0:MLA_NOPE]
            dy_r = _rope_bwd(dka_ref[h, :, MLA_NOPE:MLA_QK_PAD], cos_v, sin_v)
            grads["gk_n"][...] += _colsum(dy_n * xh_n)
            grads["gk_r"][...] += _colsum(dy_r * xh_r)
            dkn, dkr = _norm_bwd([dy_n * gk_n[...], dy_r * gk_r[...]], [xh_n, xh_r], rh, MLA_QK)
            dkpe = dkpe + dkr
            dkv_ref[:, c0:c1] = _mx(dkn)
            dkv_ref[:, c1:c1 + MLA_V] = _mx(dva_ref[h])
        dkv = dkv_ref[...]
        grads["w_kvb"][...] += _dot_tn(dkv, ckvn)
        dckvn = _dot(dkv, w_kvb[...])
        grads["g_kva"][...] += _colsum(dckvn * ckvh)
        (dckv,) = _norm_bwd([dckvn * g_kva[...]], [ckvh], rkv, MLA_KV_RANK)
        dproj_ref[:, C_CKV:C_KPE] = _mx(dckv)
        dproj_ref[:, C_KPE:C_QS] = _mx(dkpe)

        def pair_bwd(tv, dy, g_ref, gname):
            r = _rsq(_half_sums(tv * tv), SWA_D)
            xh = tv * r
            dpre = _rope_bwd(dy, cos_v, sin_v)
            grads[gname][...] += _colsum(dpre * xh)
            dn = dpre * g_ref[...]
            return r * (dn - xh * (_half_sums(dn * xh) * (1.0 / SWA_D)))

        for j in range(SWA_PAIRS):
            c0 = C_QS + j * LANES
            dproj_ref[:, c0:c0 + LANES] = _mx(pair_bwd(proj[:, c0:c0 + LANES], dqb_ref[j], g_sq, "g_sq"))
        dproj_ref[:, C_KS:C_VS] = _mx(pair_bwd(proj[:, C_KS:C_VS], _undup_halves(dkb_ref[0], dkb_ref[1]), g_sk, "g_sk"))
        dproj_ref[:, C_VS:W_IN_PACKED] = _mx(_undup_halves(dvb_ref[0], dvb_ref[1]))

        dproj = dproj_ref[...]
        grads["w_in"][...] += _dot_tn(dproj, hb)
        dh = _dot(dproj, w_in[...])
        grads["mix_g"][...] += _colsum(dh * xhat)
        (dxv,) = _norm_bwd([dh * mix_g[...]], [xhat], r0, d)
        dx_ref[...] = dxin_ref[...] + dxv

    grad_shapes = [p[n].shape for n in PREP_WEIGHTS]
    return pl.pallas_call(
        body, name="prep_bwd", grid=(t // tb,),
        in_specs=[_rows(tb, d), _rows(tb, d), _rows(tb, LANES), _rows(tb, LANES)] + _prep_specs(p) + [
            _heads_rows(MLA_HEADS, tb, MLA_QK_PAD), _heads_rows(MLA_HEADS, tb, MLA_QK_PAD),
            _heads_rows(MLA_HEADS, tb, MLA_V), _heads_rows(SWA_PAIRS, tb, LANES),
            _heads_rows(SWA_KV, tb, LANES), _heads_rows(SWA_KV, tb, LANES)],
        out_specs=[_rows(tb, d)] + [_acc(s) for s in grad_shapes],
        out_shape=[_sds((t, d), jnp.float32)] + [_sds(s, jnp.float32) for s in grad_shapes],
        scratch_shapes=[pltpu.VMEM((tb, W_IN_PACKED), _MXU), pltpu.VMEM((tb, MLA_HEADS * (MLA_NOPE + MLA_V)), _MXU),
                        pltpu.VMEM((tb, MLA_QK_PAD), _MXU)],
        compiler_params=_cparams(("arbitrary",)),
    )(x, dxin, cos, sin_s, *[p[n] for n in PREP_WEIGHTS], dqa, dka, dva, dqb, dkb, dvb)


def _strips(n):
    step = min(_STRIP, n)
    return [slice(r, r + step) for r in range(0, n, step)]


def _mla_fwd(q, k, v, side=None):
    hn, t, dq = q.shape
    dv = v.shape[2]
    bq = min(_BQ, t)
    scale = MLA_QK ** -0.5

    def body(q_ref, k_ref, v_ref, o_ref, l_ref):
        i = pl.program_id(1)
        qv = q_ref[0]

        def step(j, carry, masked):
            m, l, acc = carry
            start = pl.multiple_of(j * bq, bq)
            s = _dot_nt(qv, k_ref[0, pl.ds(start, bq), :]) * scale
            if masked:
                row = lax.broadcasted_iota(jnp.int32, (bq, bq), 0)
                col = lax.broadcasted_iota(jnp.int32, (bq, bq), 1)
                s = jnp.where(col <= row, s, NEG)
            m_new = jnp.maximum(m, _rowmax(s))
            alpha = jnp.exp(m - m_new)
            pv = jnp.exp(s - m_new)
            l = alpha * l + _rowsum(pv)
            acc = alpha * acc + _dot(_mx(pv), v_ref[0, pl.ds(start, bq), :])
            return m_new, l, acc

        init = (jnp.full((bq, 1), NEG, jnp.float32), jnp.zeros((bq, 1), jnp.float32), jnp.zeros((bq, dv), jnp.float32))
        carry = lax.fori_loop(0, i, lambda j, c: step(j, c, False), init)
        m, l, acc = step(i, carry, True)
        o_ref[0] = acc / l
        l_ref[0] = jnp.broadcast_to(m + jnp.log(l), (bq, LANES))

    return _call(
        body, (q, k, v), side, name="mla_fwd", grid=(hn, t // bq),
        in_specs=[pl.BlockSpec((1, bq, dq), lambda h, i: (h, i, 0)),
                  pl.BlockSpec((1, t, dq), lambda h, i: (h, 0, 0)),
                  pl.BlockSpec((1, t, dv), lambda h, i: (h, 0, 0))],
        out_specs=[pl.BlockSpec((1, bq, dv), lambda h, i: (h, i, 0)),
                   pl.BlockSpec((1, bq, LANES), lambda h, i: (h, i, 0))],
        out_shape=[_sds((hn, t, dv), jnp.float32), _sds((hn, t, LANES), jnp.float32)],
        semantics=("parallel", "arbitrary"))


def _mla_bwd(q, k, v, do, lse_rows, dsum_rows, side=None):
    hn, t, dq_w = q.shape
    dv_w = v.shape[2]
    bq = min(_BQ, t)
    nb = t // bq
    scale = MLA_QK ** -0.5

    def body(q_ref, do_ref, l_ref, d_ref, k_ref, v_ref, dq_ref, dk_ref, dv_ref, st_scr, dpt_scr, p_scr, ds_scr):
        j = pl.program_id(1)

        @pl.when(j == 0)
        def _():
            dq_ref[...] = jnp.zeros_like(dq_ref)

        kv = k_ref[0]
        vv = v_ref[0]
        dk_ref[0] = jnp.zeros((bq, dq_w), jnp.float32)
        dv_ref[0] = jnp.zeros((bq, dv_w), jnp.float32)

        def tile(i, masked):
            start = pl.multiple_of(i * bq, bq)
            qv = q_ref[0, pl.ds(start, bq), :]
            dov = do_ref[0, pl.ds(start, bq), :]
            st_scr[...] = _dot_nt(kv, qv)
            dpt_scr[...] = _dot_nt(vv, dov)
            lse = l_ref[0, i]
            dsum = d_ref[0, i]
            for rows in _strips(bq):
                pt = jnp.exp(st_scr[rows, :] * scale - lse)
                if masked:
                    n_rows = rows.stop - rows.start
                    row = lax.broadcasted_iota(jnp.int32, (n_rows, bq), 0) + rows.start
                    col = lax.broadcasted_iota(jnp.int32, (n_rows, bq), 1)
                    pt = jnp.where(row <= col, pt, 0.0)
                p_scr[rows, :] = _mx(pt)
                ds_scr[rows, :] = _mx(pt * (dpt_scr[rows, :] - dsum) * scale)
            ds_t = ds_scr[...]
            dv_ref[0] += _dot(p_scr[...], dov)
            dk_ref[0] += _dot(ds_t, qv)
            dq_ref[0, pl.ds(start, bq), :] += _dot_tn(ds_t, kv)

        def loop_body(i, carry):
            tile(i, False)
            return carry

        tile(j, True)
        lax.fori_loop(j + 1, nb, loop_body, 0)

    return _call(
        body, (q, do, lse_rows, dsum_rows, k, v), side, name="mla_bwd", grid=(hn, nb),
        in_specs=[pl.BlockSpec((1, t, dq_w), lambda h, j: (h, 0, 0)),
                  pl.BlockSpec((1, t, dv_w), lambda h, j: (h, 0, 0)),
                  pl.BlockSpec((1, nb, 1, bq), lambda h, j: (h, 0, 0, 0)),
                  pl.BlockSpec((1, nb, 1, bq), lambda h, j: (h, 0, 0, 0)),
                  pl.BlockSpec((1, bq, dq_w), lambda h, j: (h, j, 0)),
                  pl.BlockSpec((1, bq, dv_w), lambda h, j: (h, j, 0))],
        out_specs=[pl.BlockSpec((1, t, dq_w), lambda h, j: (h, 0, 0)),
                   pl.BlockSpec((1, bq, dq_w), lambda h, j: (h, j, 0)),
                   pl.BlockSpec((1, bq, dv_w), lambda h, j: (h, j, 0))],
        out_shape=[_sds((hn, t, dq_w), jnp.float32), _sds((hn, t, dq_w), jnp.float32), _sds((hn, t, dv_w), jnp.float32)],
        scratch_shapes=[pltpu.VMEM((bq, bq), jnp.float32), pltpu.VMEM((bq, bq), jnp.float32),
                        pltpu.VMEM((bq, bq), _MXU), pltpu.VMEM((bq, bq), _MXU)],
        semantics=("parallel", "arbitrary"))


STACK = SWA_GROUP * SWA_BLOCK


def _swa_stack(ref, c, rows):
    low = _low_half()
    parts = []
    for g in range(SWA_GROUP):
        tv = ref[SWA_GROUP // 2 * c + g // 2, rows, :]
        keep = low if g % 2 == 0 else jnp.logical_not(low)
        parts.append(_mx(jnp.where(keep, tv, jnp.zeros_like(tv))))
    return jnp.concatenate(parts, axis=0)


def _swa_cols(ref, c, rows):
    return jnp.concatenate([ref[SWA_GROUP * c + g, rows, 0:1] for g in range(SWA_GROUP)], axis=0)


def _swa_sink_col(s_ref, c):
    return jnp.concatenate([jnp.broadcast_to(s_ref[SWA_GROUP * c + g][:, 0:1], (SWA_BLOCK, 1))
                            for g in range(SWA_GROUP)], axis=0)


def _swa_band_masks():
    row = lax.broadcasted_iota(jnp.int32, (STACK, SWA_BLOCK), 0) & (SWA_BLOCK - 1)
    col = lax.broadcasted_iota(jnp.int32, (STACK, SWA_BLOCK), 1)
    return col <= row, col > row


def _swa_unstack_pairs(ref, c, rows, stacked):
    for pr in range(SWA_GROUP // 2):
        r0 = 2 * pr * SWA_BLOCK
        ref[SWA_GROUP // 2 * c + pr, rows, :] = _pick_halves(stacked[r0:r0 + SWA_BLOCK], stacked[r0 + SWA_BLOCK:r0 + 2 * SWA_BLOCK])


def _swa_blocks(t):
    nblk = t // SWA_BLOCK
    bps = min(_SWA_STEP, nblk)
    return nblk, bps, bps * SWA_BLOCK


def _swa_fwd(q, k, v, sinks):
    _, t, _ = q.shape
    nblk, bps, sb = _swa_blocks(t)
    scale = SWA_D ** -0.5

    def body(q_ref, k_ref, kp_ref, v_ref, vp_ref, s_ref, o_ref, l_ref):
        n = pl.program_id(0)
        m_cur, m_prev = _swa_band_masks()
        for c in range(SWA_KV):
            sink = _swa_sink_col(s_ref, c)
            for b in range(bps):
                rows = slice(b * SWA_BLOCK, (b + 1) * SWA_BLOCK)
                kc, vc = k_ref[c, rows, :], v_ref[c, rows, :]
                if b == 0:
                    kp, vp, mp = kp_ref[c], vp_ref[c], jnp.logical_and(m_prev, n > 0)
                else:
                    before = slice((b - 1) * SWA_BLOCK, b * SWA_BLOCK)
                    kp, vp, mp = k_ref[c, before, :], v_ref[c, before, :], m_prev
                qs = _swa_stack(q_ref, c, rows)
                s_c = jnp.where(m_cur, _dot_nt(qs, kc) * scale, NEG)
                s_p = jnp.where(mp, _dot_nt(qs, kp) * scale, NEG)
                m = jnp.maximum(jnp.maximum(_rowmax(s_c), _rowmax(s_p)), sink)
                e_c = jnp.exp(s_c - m)
                e_p = jnp.exp(s_p - m)
                denom = _rowsum(e_c) + _rowsum(e_p) + jnp.exp(sink - m)
                inv = 1.0 / denom
                o = _dot(_mx(e_c * inv), vc) + _dot(_mx(e_p * inv), vp)
                lse = m + jnp.log(denom)
                for g in range(SWA_GROUP):
                    l_ref[SWA_GROUP * c + g, rows, :] = jnp.broadcast_to(
                        lse[g * SWA_BLOCK:(g + 1) * SWA_BLOCK], (SWA_BLOCK, LANES))
                _swa_unstack_pairs(o_ref, c, rows, o)

    main = lambda n: (0, n, 0)
    prev = lambda n: (0, jnp.maximum(n * bps - 1, 0), 0)
    return pl.pallas_call(
        body, name="swa_fwd", grid=(nblk // bps,),
        in_specs=[pl.BlockSpec((SWA_PAIRS, sb, LANES), main),
                  pl.BlockSpec((SWA_KV, sb, LANES), main), pl.BlockSpec((SWA_KV, SWA_BLOCK, LANES), prev),
                  pl.BlockSpec((SWA_KV, sb, LANES), main), pl.BlockSpec((SWA_KV, SWA_BLOCK, LANES), prev),
                  _const((SWA_HEADS, 1, LANES))],
        out_specs=[pl.BlockSpec((SWA_PAIRS, sb, LANES), main), pl.BlockSpec((SWA_HEADS, sb, LANES), main)],
        out_shape=[_sds((SWA_PAIRS, t, LANES), jnp.float32), _sds((SWA_HEADS, t, LANES), jnp.float32)],
        compiler_params=_cparams(("parallel",)),
    )(q, k, k, v, v, sinks)


def _swa_bwd(q, k, v, sinks, do, lse, dsum):
    _, t, _ = q.shape
    nblk, bps, sb = _swa_blocks(t)
    steps = nblk // bps
    scale = SWA_D ** -0.5

    def body(q_ref, k_ref, kp_ref, v_ref, vp_ref, s_ref, do_ref, l_ref, d_ref, qn_ref, don_ref, ln_ref, dn_ref,
             dq_ref, dk_ref, dv_ref, ds_ref):
        n = pl.program_id(0)

        @pl.when(n == 0)
        def _():
            ds_ref[...] = jnp.zeros_like(ds_ref)

        m_cur, m_prev = _swa_band_masks()
        everything = slice(0, SWA_BLOCK)

        def probs(qs, keys, mask, lcol):
            return jnp.where(mask, jnp.exp(_dot_nt(qs, keys) * scale - lcol), 0.0)

        def dscores(pm, dos, vals, dcol):
            return _mx(pm * (_dot_nt(dos, vals) - dcol) * scale)

        for c in range(SWA_KV):
            sink = _swa_sink_col(s_ref, c)
            dk_acc = [jnp.zeros((SWA_BLOCK, LANES), jnp.float32) for _ in range(bps)]
            dv_acc = [jnp.zeros((SWA_BLOCK, LANES), jnp.float32) for _ in range(bps)]
            for b in range(bps):
                rows = slice(b * SWA_BLOCK, (b + 1) * SWA_BLOCK)
                kc, vc = k_ref[c, rows, :], v_ref[c, rows, :]
                if b == 0:
                    kp, vp, mp = kp_ref[c], vp_ref[c], jnp.logical_and(m_prev, n > 0)
                else:
                    before = slice((b - 1) * SWA_BLOCK, b * SWA_BLOCK)
                    kp, vp, mp = k_ref[c, before, :], v_ref[c, before, :], m_prev
                qs = _swa_stack(q_ref, c, rows)
                dos = _swa_stack(do_ref, c, rows)
                lcol = _swa_cols(l_ref, c, rows)
                dcol = _swa_cols(d_ref, c, rows)
                p_c = probs(qs, kc, m_cur, lcol)
                p_p = probs(qs, kp, mp, lcol)
                ds_c = dscores(p_c, dos, vc, dcol)
                ds_p = dscores(p_p, dos, vp, dcol)
                _swa_unstack_pairs(dq_ref, c, rows, _dot(ds_c, kc) + _dot(ds_p, kp))
                dk_acc[b] = dk_acc[b] + _dot_tn(ds_c, qs)
                dv_acc[b] = dv_acc[b] + _dot_tn(_mx(p_c), dos)
                if b > 0:
                    dk_acc[b - 1] = dk_acc[b - 1] + _dot_tn(ds_p, qs)
                    dv_acc[b - 1] = dv_acc[b - 1] + _dot_tn(_mx(p_p), dos)
                p_sink = jnp.exp(sink - lcol) * dcol
                for g in range(SWA_GROUP):
                    ds_ref[SWA_GROUP * c + g] += -jnp.sum(p_sink[g * SWA_BLOCK:(g + 1) * SWA_BLOCK])
            tail = slice((bps - 1) * SWA_BLOCK, bps * SWA_BLOCK)
            kc, vc = k_ref[c, tail, :], v_ref[c, tail, :]
            qs = _swa_stack(qn_ref, c, everything)
            dos = _swa_stack(don_ref, c, everything)
            lcol = _swa_cols(ln_ref, c, everything)
            dcol = _swa_cols(dn_ref, c, everything)
            p_p = probs(qs, kc, jnp.logical_and(m_prev, n < steps - 1), lcol)
            ds_p = dscores(p_p, dos, vc, dcol)
            dk_acc[bps - 1] = dk_acc[bps - 1] + _dot_tn(ds_p, qs)
            dv_acc[bps - 1] = dv_acc[bps - 1] + _dot_tn(_mx(p_p), dos)
            for b in range(bps):
                rows = slice(b * SWA_BLOCK, (b + 1) * SWA_BLOCK)
                dk_ref[c, rows, :] = dk_acc[b]
                dv_ref[c, rows, :] = dv_acc[b]

    main = lambda n: (0, n, 0)
    prev = lambda n: (0, jnp.maximum(n * bps - 1, 0), 0)
    nxt = lambda n: (0, jnp.minimum((n + 1) * bps, nblk - 1), 0)
    pairs = pl.BlockSpec((SWA_PAIRS, sb, LANES), main)
    kvs = pl.BlockSpec((SWA_KV, sb, LANES), main)
    kv_prev = pl.BlockSpec((SWA_KV, SWA_BLOCK, LANES), prev)
    stats = pl.BlockSpec((SWA_HEADS, sb, LANES), main)
    pairs_next = pl.BlockSpec((SWA_PAIRS, SWA_BLOCK, LANES), nxt)
    stats_next = pl.BlockSpec((SWA_HEADS, SWA_BLOCK, LANES), nxt)
    return pl.pallas_call(
        body, name="swa_bwd", grid=(steps,),
        in_specs=[pairs, kvs, kv_prev, kvs, kv_prev, _const((SWA_HEADS, 1, LANES)), pairs, stats, stats,
                  pairs_next, pairs_next, stats_next, stats_next],
        out_specs=[pairs, kvs, kvs, _acc((SWA_HEADS, 1, LANES))],
        out_shape=[_sds((SWA_PAIRS, t, LANES), jnp.float32), _sds((SWA_KV, t, LANES), jnp.float32),
                   _sds((SWA_KV, t, LANES), jnp.float32), _sds((SWA_HEADS, 1, LANES), jnp.float32)],
        compiler_params=_cparams(("arbitrary",)),
    )(q, k, k, v, v, sinks, do, lse, dsum, q, do, lse, dsum)


MIX_SLABS = 4
MIX_WIDTH = MIX_SLABS * LANES


def _mix_out_fwd(x, oa, ob, ga, gb, wo_a, wo_b):
    t, d = x.shape
    tb = min(_TB, t)

    def body(x_ref, oa_ref, ob_ref, ga_ref, gb_ref, woa_ref, wob_ref, xo_ref):
        y = x_ref[...]
        for o_ref, g_ref, w_ref in ((oa_ref, ga_ref, woa_ref), (ob_ref, gb_ref, wob_ref)):
            r = _rsq(sum(_sumsq(o_ref[h]) for h in range(MIX_SLABS)), MIX_WIDTH)
            for h in range(MIX_SLABS):
                y = y + _dot(_mx(o_ref[h] * r * g_ref[h]), w_ref[h])
        xo_ref[...] = y

    slab = _heads_rows(MIX_SLABS, tb, LANES)
    return pl.pallas_call(
        body, name="mix_out_fwd", grid=(t // tb,),
        in_specs=[_rows(tb, d), slab, slab, _const(ga.shape), _const(gb.shape), _const(wo_a.shape), _const(wo_b.shape)],
        out_specs=_rows(tb, d),
        out_shape=_sds((t, d), jnp.float32),
        compiler_params=_cparams(("parallel",)),
    )(x, oa, ob, ga, gb, wo_a, wo_b)


def _mix_out_bwd(dx, oa, ob, ga, gb, wo_a, wo_b):
    t, d = dx.shape
    tb = min(_TB, t)

    def group(o_ref, g_ref, w_ref, dyb, do_ref, n_ref, col0, dg_ref):
        r = _rsq(sum(_sumsq(o_ref[h]) for h in range(MIX_SLABS)), MIX_WIDTH)
        xh, dn = [], []
        for h in range(MIX_SLABS):
            xh.append(o_ref[h] * r)
            n_ref[:, col0 + h * LANES:col0 + (h + 1) * LANES] = _mx(xh[h] * g_ref[h])
            dm = _dot_nt(dyb, w_ref[h])
            dg_ref[h] += _colsum(dm * xh[h])
            dn.append(dm * g_ref[h])
        c = sum(_rowsum(dn[h] * xh[h]) for h in range(MIX_SLABS)) * (1.0 / MIX_WIDTH)
        prods = []
        for h in range(MIX_SLABS):
            do = r * (dn[h] - xh[h] * c)
            do_ref[h] = do.astype(do_ref.dtype)
            prods.append(do * o_ref[h])
        return prods

    def body(dx_ref, oa_ref, ob_ref, ga_ref, gb_ref, woa_ref, wob_ref,
             doa_ref, dsa_ref, dob_ref, dsb_ref, n_ref, dy_ref, dga_ref, dgb_ref):
        @pl.when(pl.program_id(0) == 0)
        def _():
            dga_ref[...] = jnp.zeros_like(dga_ref)
            dgb_ref[...] = jnp.zeros_like(dgb_ref)

        dyb = _mx(dx_ref[...])
        dy_ref[...] = dyb
        for h, pr in enumerate(group(oa_ref, ga_ref, woa_ref, dyb, doa_ref, n_ref, 0, dga_ref)):
            dsa_ref[h] = jnp.broadcast_to(_rowsum(pr), (tb, LANES))
        low = _low_half()
        for j, pr in enumerate(group(ob_ref, gb_ref, wob_ref, dyb, dob_ref, n_ref, MIX_WIDTH, dgb_ref)):
            dsb_ref[2 * j] = jnp.broadcast_to(_rowsum(jnp.where(low, pr, 0.0)), (tb, LANES))
            dsb_ref[2 * j + 1] = jnp.broadcast_to(_rowsum(jnp.where(low, 0.0, pr)), (tb, LANES))

    slab = _heads_rows(MIX_SLABS, tb, LANES)
    return pl.pallas_call(
        body, name="mix_out_bwd", grid=(t // tb,),
        in_specs=[_rows(tb, d), slab, slab, _const(ga.shape), _const(gb.shape), _const(wo_a.shape), _const(wo_b.shape)],
        out_specs=[slab, slab, slab, _heads_rows(SWA_HEADS, tb, LANES), _rows(tb, 2 * MIX_WIDTH), _rows(tb, d),
                   _acc(ga.shape), _acc(gb.shape)],
        out_shape=[_sds((MIX_SLABS, t, LANES), _MXU), _sds((MIX_SLABS, t, LANES), jnp.float32),
                   _sds((MIX_SLABS, t, LANES), jnp.float32), _sds((SWA_HEADS, t, LANES), jnp.float32),
                   _sds((t, 2 * MIX_WIDTH), _MXU), _sds((t, d), _MXU),
                   _sds(ga.shape, jnp.float32), _sds(gb.shape, jnp.float32)],
        compiler_params=_cparams(("arbitrary",)),
    )(dx, oa, ob, ga, gb, wo_a, wo_b)


def _loss_head(y, target):
    t, d = y.shape
    tb = min(_TB, t)

    def body(y_ref, t_ref, dy_ref, acc_ref):
        @pl.when(pl.program_id(0) == 0)
        def _():
            acc_ref[...] = jnp.zeros_like(acc_ref)

        err = y_ref[...] - t_ref[...]
        dy_ref[...] = err * (1.0 / d)
        acc_ref[...] += jnp.sum(err * err)

    return pl.pallas_call(
        body, name="loss_head", grid=(t // tb,),
        in_specs=[_rows(tb, d), _rows(tb, d)],
        out_specs=[_rows(tb, d), _acc((8, LANES))],
        out_shape=[_sds((t, d), jnp.float32), _sds((8, LANES), jnp.float32)],
        compiler_params=_cparams(("arbitrary",)),
    )(y, target)


def _is_transposed(name):
    return name not in ROW_SHARDED


def _pack_layer(shards, l, width):
    rows = [(shards[n][l].T if _is_transposed(n) else shards[n][l]).reshape(-1, width) for n in BIG]
    return jnp.concatenate(rows, axis=0)


def _full_shape(like, name):
    _, a, b = like[name].shape
    return (N_DEV * b, a) if _is_transposed(name) else (N_DEV * a, b)


def _unpack_full(gathered, like):
    out, off = {}, 0
    for n in BIG:
        rows_n = like[n][0].size // gathered.shape[-1]
        out[n] = gathered[:, off:off + rows_n].reshape(_full_shape(like, n))
        off += rows_n
    return out


def _stored(a, name):
    return jnp.swapaxes(a, 1, 2) if _is_transposed(name) else a


def _grads_by_destination(grads, width):
    by_dest = lambda n: grads[n].reshape(N_DEV, -1, width)
    return [by_dest(n) for n in FFN_BIG] + [jnp.concatenate([by_dest(n) for n in OTHER_BIG], axis=1)]


def _shards_from_rows(rows, like):
    out = dict(zip(FFN_BIG, rows[:len(FFN_BIG)]))
    rest, off = rows[len(FFN_BIG)], 0
    for n in OTHER_BIG:
        _, a, b = like[n].shape
        rows_n = a * b // rest.shape[-1]
        out[n] = rest[off:off + rows_n].reshape((b, a) if _is_transposed(n) else (a, b))
        off += rows_n
    return out


def _small_rows(n_elems):
    return -(-n_elems // LANES)


def _pack_small(arrays):
    parts = []
    for n in SMALL:
        v = arrays[n]
        depth, width = v.shape
        padded = _small_rows(width) * LANES
        parts.append(jnp.pad(v, ((0, 0), (0, padded - width))).reshape(-1, LANES))
    packed = jnp.concatenate(parts, axis=0)
    return jnp.pad(packed, ((0, (-packed.shape[0]) % 8), (0, 0)))


def _unpack_small(packed, like):
    out, off = {}, 0
    for n in SMALL:
        depth, width = like[n].shape
        rows_n = _small_rows(width)
        seg = packed[off:off + depth * rows_n].reshape(depth, rows_n * LANES)
        out[n] = seg[:, :width]
        off += depth * rows_n
    return out


def _rope_tables(t):
    pos = jnp.arange(t, dtype=jnp.float32)
    inv = 1.0 / (ROPE_THETA ** (jnp.arange(0, MLA_ROPE, 2, dtype=jnp.float32) / MLA_ROPE))
    ang = pos[:, None] * inv[None, :]
    cos, sin = jnp.cos(ang), jnp.sin(ang)
    return jnp.concatenate([cos, cos, cos, cos], axis=1), jnp.concatenate([-sin, sin, -sin, sin], axis=1)


def _pad_lanes(a, width):
    return jnp.pad(a, [(0, 0)] * (a.ndim - 1) + [(0, width - a.shape[-1])])


def _layer_params(full, small, l):
    w_in = full["w_in"]
    d = w_in.shape[1]
    mla_rows = W_IN_COLS[0]
    w_in_p = jnp.concatenate([w_in[:mla_rows], jnp.zeros((LANES - MLA_ROPE, d), w_in.dtype), w_in[mla_rows:]], axis=0)
    wqb = full["mla_w_q_b"].reshape(MLA_HEADS, MLA_QK, MLA_Q_RANK)
    wqb = jnp.pad(wqb, ((0, 0), (0, MLA_QK_PAD - MLA_QK), (0, 0)))
    row = lambda name: small[name][l][None, :]
    twice = lambda g: jnp.concatenate([g, g], axis=1)
    prep = {
        "mix_g": row("mix_norm"), "w_in": w_in_p,
        "g_qa": row("mla_q_a_norm"), "wqb": wqb,
        "g_kva": row("mla_kv_a_norm"), "w_kvb": full["mla_w_kv_b"],
        "gq_n": row("mla_q_norm")[:, :MLA_NOPE], "gq_r": _pad_lanes(row("mla_q_norm")[:, MLA_NOPE:], LANES),
        "gk_n": row("mla_k_norm")[:, :MLA_NOPE], "gk_r": _pad_lanes(row("mla_k_norm")[:, MLA_NOPE:], LANES),
        "g_sq": twice(row("swa_q_norm")), "g_sk": twice(row("swa_k_norm")),
    }
    return {
        "prep": prep,
        "ffn1": (row("ffn1_norm"), full["ffn1_w_gate"], full["ffn1_w_up"], full["ffn1_w_down"]),
        "ffn2": (row("ffn2_norm"), full["ffn2_w_gate"], full["ffn2_w_up"], full["ffn2_w_down"]),
        "sinks": jnp.broadcast_to(small["swa_sinks"][l][:, None, None], (SWA_HEADS, 1, LANES)),
        "ga": small["mla_out_norm"][l].reshape(MIX_SLABS, 1, LANES),
        "gb": small["swa_out_norm"][l].reshape(MIX_SLABS, 1, LANES),
        "wo_a": full["w_o"][:MIX_WIDTH].reshape(MIX_SLABS, LANES, d),
        "wo_b": full["w_o"][MIX_WIDTH:].reshape(MIX_SLABS, LANES, d),
    }


def _ffn_backward(x_in, dxo, a, u, params, tag, side=None):
    gain, wg, wu, wd = params
    (dxi, da, du, s, h, dy, dg), side_out = _ffn_dgrad(x_in, gain, dxo, a, u, wg, wu, wd, side)
    dwg = _tn_matmul(da, h, "wgrad_" + tag + "_gate")
    dwu = _tn_matmul(du, h, "wgrad_" + tag + "_up")
    dwd = _tn_matmul(s, dy, "wgrad_" + tag + "_down")
    return dxi, dg[0], dwg, dwu, dwd, side_out


def kernel(x, ffn1_norm, ffn1_w_gate, ffn1_w_up, ffn1_w_down, mix_norm, w_in, mla_q_a_norm, mla_w_q_b, mla_kv_a_norm, mla_w_kv_b, mla_q_norm, mla_k_norm, swa_q_norm, swa_k_norm, swa_sinks, mla_out_norm, swa_out_norm, w_o, ffn2_norm, ffn2_w_gate, ffn2_w_up, ffn2_w_down, loss_target, m_ffn1_norm, m_ffn1_w_gate, m_ffn1_w_up, m_ffn1_w_down, m_mix_norm, m_w_in, m_mla_q_a_norm, m_mla_w_q_b, m_mla_kv_a_norm, m_mla_w_kv_b, m_mla_q_norm, m_mla_k_norm, m_swa_q_norm, m_swa_k_norm, m_swa_sinks, m_mla_out_norm, m_swa_out_norm, m_w_o, m_ffn2_norm, m_ffn2_w_gate, m_ffn2_w_up, m_ffn2_w_down, v_ffn1_norm, v_ffn1_w_gate, v_ffn1_w_up, v_ffn1_w_down, v_mix_norm, v_w_in, v_mla_q_a_norm, v_mla_w_q_b, v_mla_kv_a_norm, v_mla_w_kv_b, v_mla_q_norm, v_mla_k_norm, v_swa_q_norm, v_swa_k_norm, v_swa_sinks, v_mla_out_norm, v_swa_out_norm, v_w_o, v_ffn2_norm, v_ffn2_w_gate, v_ffn2_w_up, v_ffn2_w_down):
    local = dict(locals())
    w = {n: local[n] for n in WEIGHTS}
    m = {n: local["m_" + n] for n in WEIGHTS}
    v = {n: local["v_" + n] for n in WEIGHTS}
    depth = ffn1_norm.shape[0]
    t, d = x.shape[-2], x.shape[-1]
    x2d = x.reshape(t, d)
    target = loss_target.reshape(t, d)
    bq = min(_BQ, t)

    big = {n: w[n] for n in BIG}
    packed = [_mx(_pack_layer(big, l, d)) for l in range(depth)]
    gathered = _all_gather(packed[0])
    cos, sin_s = _rope_tables(t)
    x_i, y_i, c_i = _position()
    dest_idx = jnp.stack([4 * px + 2 * py + c_i for px, py in _relations(x_i, y_i)]).astype(jnp.int32)

    params, saved = [], []
    xc = x2d
    for l in range(depth):
        pr = _layer_params(_unpack_full(gathered, big), w, l)
        params.append(pr)
        x0 = xc
        x1, a1, u1 = _ffn_fwd(x0, *pr["ffn1"])
        qa, ka, va, qb, kb, vb = _prep_fwd(x1, cos, sin_s, pr["prep"])
        (oa, lse_a), side_out = _mla_fwd(qa, ka, va, _side_gather(packed[l + 1]) if l + 1 < depth else None)
        if side_out:
            gathered = side_out[0]
        ob, lse_b = _swa_fwd(qb, kb, vb, pr["sinks"])
        x2 = _mix_out_fwd(x1, oa, ob, pr["ga"], pr["gb"], pr["wo_a"], pr["wo_b"])
        x3, a2, u2 = _ffn_fwd(x2, *pr["ffn2"])
        saved.append((x0, a1, u1, x1, qa, ka, va, qb, kb, vb, oa, lse_a, ob, lse_b, x2, a2, u2))
        xc = x3

    dx, sq_err = _loss_head(xc, target)
    loss = lax.psum(0.5 / d * sq_err[0, 0], MESH_AXES)

    grad_shards = [None] * depth
    small_grads = {n: [None] * depth for n in SMALL}
    pending = None
    for l in reversed(range(depth)):
        pr = params[l]
        x0, a1, u1, x1, qa, ka, va, qb, kb, vb, oa, lse_a, ob, lse_b, x2, a2, u2 = saved[l]
        g = {}
        dx, small_grads["ffn2_norm"][l], g["ffn2_w_gate"], g["ffn2_w_up"], g["ffn2_w_down"], side_out = _ffn_backward(
            x2, dx, a2, u2, pr["ffn2"], "ffn2", _side_sibling(pending[1]) if pending else None)
        if pending:
            sums = _rs_chip_sums(pending[1], side_out, dest_idx)
            owns, wires = sums[:len(side_out)], sums[len(side_out):]

        doa, dsum_a, dob, dsum_b, mixed, dyb, dga, dgb = _mix_out_bwd(
            dx, oa, ob, pr["ga"], pr["gb"], pr["wo_a"], pr["wo_b"])
        small_grads["mla_out_norm"][l] = dga.reshape(-1)
        small_grads["swa_out_norm"][l] = dgb.reshape(-1)
        g["w_o"] = _tn_matmul(mixed, dyb, "wgrad_wo")

        rows_of = lambda s: s[:, :, 0].reshape(MLA_HEADS, t // bq, 1, bq)
        (dqa, dka, dva), side_out = _mla_bwd(qa, ka, va, doa, rows_of(lse_a), rows_of(dsum_a),
                                             _side_chips(wires) if pending else None)
        if pending:
            grad_shards[pending[0]] = _shards_from_rows(_rs_sum(owns, side_out), big)
        dqb, dkb, dvb, dsinks = _swa_bwd(qb, kb, vb, pr["sinks"], dob, lse_b, dsum_b)
        small_grads["swa_sinks"][l] = dsinks[:, 0, 0]

        outs = _prep_bwd(x1, dx, cos, sin_s, pr["prep"], dqa, dka, dva, dqb, dkb, dvb)
        dx = outs[0]
        pg = dict(zip(PREP_WEIGHTS, outs[1:]))
        g["w_in"] = jnp.concatenate([pg["w_in"][:W_IN_COLS[0]], pg["w_in"][C_QS:]], axis=0)
        g["mla_w_q_b"] = pg["wqb"][:, :MLA_QK].reshape(MLA_HEADS * MLA_QK, MLA_Q_RANK)
        g["mla_w_kv_b"] = pg["w_kvb"]
        fold = lambda gg: gg[0, :HALF] + gg[0, HALF:]
        small_grads["mix_norm"][l] = pg["mix_g"][0]
        small_grads["mla_q_a_norm"][l] = pg["g_qa"][0]
        small_grads["mla_kv_a_norm"][l] = pg["g_kva"][0]
        small_grads["mla_q_norm"][l] = jnp.concatenate([pg["gq_n"][0], pg["gq_r"][0, :MLA_ROPE]])
        small_grads["mla_k_norm"][l] = jnp.concatenate([pg["gk_n"][0], pg["gk_r"][0, :MLA_ROPE]])
        small_grads["swa_q_norm"][l] = fold(pg["g_sq"])
        small_grads["swa_k_norm"][l] = fold(pg["g_sk"])

        dx, small_grads["ffn1_norm"][l], g["ffn1_w_gate"], g["ffn1_w_up"], g["ffn1_w_down"], _ = _ffn_backward(
            x0, dx, a1, u1, pr["ffn1"], "ffn1")
        pending = (l, _grads_by_destination(g, d))

    sibs = _rs_sibling_exchange(pending[1])
    sums = _rs_chip_sums(pending[1], sibs, dest_idx)
    owns, wires = sums[:len(sibs)], sums[len(sibs):]
    grad_shards[pending[0]] = _shards_from_rows(_rs_sum(owns, _rs_chip_exchange(wires)), big)

    grad_big, delta_big, new_m_big, new_v_big = {}, {}, {}, {}
    for n in BIG:
        g_st = jnp.stack([grad_shards[l][n] for l in range(depth)])
        d_st, m_st, v_st = _adamw(g_st, _stored(w[n], n), _stored(m[n], n), _stored(v[n], n), n)
        grad_big[n], delta_big[n], new_m_big[n], new_v_big[n] = (_stored(a, n) for a in (g_st, d_st, m_st, v_st))

    small_partial = _pack_small({n: jnp.stack(small_grads[n]) for n in SMALL})
    g_s = _all_reduce_small(small_partial)
    d_s, m_s, v_s = _adamw_small(g_s, _pack_small(w), _pack_small(m), _pack_small(v))
    grad_small, delta_small, new_m_small, new_v_small = (_unpack_small(a, w) for a in (g_s, d_s, m_s, v_s))

    def ordered(big, small):
        return [big[n] if n in big else small[n] for n in WEIGHTS]

    return (loss, dx.reshape(x.shape), *ordered(grad_big, grad_small), *ordered(delta_big, delta_small),
            *ordered(new_m_big, new_m_small), *ordered(new_v_big, new_v_small))
```

```python
import jax
import jax.numpy as jnp
from jax import lax
from jax.experimental import pallas as pl
from jax.experimental.pallas import tpu as pltpu

N_DEV = 8
EPS = 1e-6
ROPE_THETA = 10000.0
MLA_HEADS = 4
MLA_Q_RANK = 256
MLA_KV_RANK = 128
MLA_NOPE = 128
MLA_ROPE = 64
MLA_V = 128
MLA_QK = MLA_NOPE + MLA_ROPE
MLA_QK_PAD = 256
SWA_HEADS = 8
SWA_KV = 2
SWA_GROUP = SWA_HEADS // SWA_KV
SWA_D = 64
SWA_BLOCK = 128
ADAM_LR = 0.001
ADAM_B1 = 0.9
ADAM_B2 = 0.999
ADAM_EPS = 1e-08
ADAM_WD = 0.01
ADAM_STEP = 10

LANES = 128
HALF = LANES // 2
SWA_PAIRS = SWA_HEADS // 2
W_IN_COLS = (MLA_Q_RANK + MLA_KV_RANK + MLA_ROPE, SWA_HEADS * SWA_D + 2 * SWA_KV * SWA_D)
VMEM_LIMIT = 56 * 1024 * 1024

_MXU = jnp.bfloat16
_TB = 256
_TB_MIX = 512
_BQ = 512
_STRIP = 32
_TK = 512
_SWA_STEP = 4
RS_ROW_BLOCKS = 2

BIG = ("ffn1_w_gate", "ffn1_w_up", "ffn1_w_down", "w_in", "mla_w_q_b", "mla_w_kv_b", "w_o",
       "ffn2_w_gate", "ffn2_w_up", "ffn2_w_down")
ROW_SHARDED = ("ffn1_w_down", "w_o", "ffn2_w_down")
FFN_BIG = ("ffn1_w_gate", "ffn1_w_up", "ffn1_w_down", "ffn2_w_gate", "ffn2_w_up", "ffn2_w_down")
OTHER_BIG = ("w_o", "w_in", "mla_w_q_b", "mla_w_kv_b")
SMALL = ("ffn1_norm", "mix_norm", "mla_q_a_norm", "mla_kv_a_norm", "mla_q_norm", "mla_k_norm",
         "swa_q_norm", "swa_k_norm", "swa_sinks", "mla_out_norm", "swa_out_norm", "ffn2_norm")
WEIGHTS = ("ffn1_norm", "ffn1_w_gate", "ffn1_w_up", "ffn1_w_down", "mix_norm", "w_in", "mla_q_a_norm",
           "mla_w_q_b", "mla_kv_a_norm", "mla_w_kv_b", "mla_q_norm", "mla_k_norm", "swa_q_norm",
           "swa_k_norm", "swa_sinks", "mla_out_norm", "swa_out_norm", "w_o", "ffn2_norm",
           "ffn2_w_gate", "ffn2_w_up", "ffn2_w_down")
MESH_AXES = ("x", "y", "c")
MESH = pl.DeviceIdType.MESH
NEG = -1e30
LOG2_E = 1.4426950408889634


def _f32(t):
    return t.astype(jnp.float32)


def _mx(t):
    return t.astype(_MXU)


def _dot(a, b):
    return jnp.dot(a, b, preferred_element_type=jnp.float32)


def _dot_nt(a, b):
    return lax.dot_general(a, b, (((1,), (1,)), ((), ())), preferred_element_type=jnp.float32)


def _dot_tn(a, b):
    return lax.dot_general(a, b, (((0,), (0,)), ((), ())), preferred_element_type=jnp.float32)


def _rsq(ss, n):
    return lax.rsqrt(ss * (1.0 / n) + EPS)


def _sumsq(t):
    return jnp.sum(t * t, axis=-1, keepdims=True)


def _rowsum(t):
    return jnp.sum(t, axis=-1, keepdims=True)


def _rowmax(t):
    return jnp.max(t, axis=-1, keepdims=True)


def _colsum(t):
    return jnp.sum(t, axis=0, keepdims=True)


def _lane():
    return lax.broadcasted_iota(jnp.int32, (1, LANES), 1)


def _low_half():
    return _lane() < HALF


def _swap32(t):
    return jnp.where((_lane() & 32) == 0, pltpu.roll(t, 96, 1), pltpu.roll(t, 32, 1))


def _rope(t, cos, sin_signed):
    return t * cos + _swap32(t) * sin_signed


def _rope_bwd(d, cos, sin_signed):
    return d * cos + _swap32(d * sin_signed)


def _half_sums(t):
    low = _low_half()
    return jnp.where(low, _rowsum(jnp.where(low, t, 0.0)), _rowsum(jnp.where(low, 0.0, t)))


def _dup_halves(pair):
    low = _low_half()
    swapped = pltpu.roll(pair, HALF, 1)
    return jnp.where(low, pair, swapped), jnp.where(low, swapped, pair)


def _undup_halves(d0, d1):
    return jnp.where(_low_half(), d0 + pltpu.roll(d0, HALF, 1), d1 + pltpu.roll(d1, HALF, 1))


def _pick_halves(a, b):
    return jnp.where(_low_half(), a, b)


def _norm_bwd(dn_list, xh_list, r, n):
    c = sum(_rowsum(dn * xh) for dn, xh in zip(dn_list, xh_list)) * (1.0 / n)
    return [r * (dn - xh * c) for dn, xh in zip(dn_list, xh_list)]


def _cparams(semantics):
    return pltpu.CompilerParams(dimension_semantics=semantics, vmem_limit_bytes=VMEM_LIMIT)


def _const(shape):
    nd = len(shape)
    return pl.BlockSpec(shape, lambda *_: (0,) * nd, pipeline_mode=pl.Buffered(1))


def _acc(shape):
    nd = len(shape)
    return pl.BlockSpec(shape, lambda *_: (0,) * nd)


def _rows(tb, width):
    return pl.BlockSpec((tb, width), lambda i: (i, 0))


def _heads_rows(h, tb, width):
    return pl.BlockSpec((h, tb, width), lambda i: (0, i, 0))


def _sds(shape, dtype):
    return jax.ShapeDtypeStruct(shape, dtype)


def _position():
    return lax.axis_index("x"), lax.axis_index("y"), lax.axis_index("c")


def _all_gather(xp):
    def body(x_ref, out_ref, send_sems, recv_sems, local_sem):
        x, y, c = _position()
        me, sibling = (x, y, c), (x, y, 1 - c)
        chips = [(1 - x, y), (x, 1 - y), (1 - x, 1 - y)]

        def rows(px, py, pc):
            return out_ref.at[4 * px + 2 * py + pc]

        def copy(k, block, to, src=None):
            return pltpu.make_async_remote_copy(
                src_ref=rows(*block) if src is None else src, dst_ref=rows(*block),
                send_sem=send_sems.at[k], recv_sem=recv_sems.at[k], device_id=to, device_id_type=MESH)

        mine = pltpu.make_async_copy(x_ref, rows(*me), local_sem)
        mine.start()
        first = [copy(0, me, sibling, src=x_ref)]
        first += [copy(1 + j, me, (*chip, c), src=x_ref) for j, chip in enumerate(chips)]
        for cp in first:
            cp.start()
        passed = [copy(4 + j, (*chip, c), sibling) for j, chip in enumerate(chips)]
        for j, chip in enumerate(chips):
            copy(1 + j, (*chip, c), me).wait_recv()
            passed[j].start()
        copy(0, sibling, me).wait_recv()
        for j, chip in enumerate(chips):
            copy(4 + j, (*chip, 1 - c), me).wait_recv()
        for cp in first + passed:
            cp.wait_send()
        mine.wait()

    return pl.pallas_call(
        body, name="ag_weights",
        out_shape=_sds((N_DEV,) + xp.shape, xp.dtype),
        in_specs=[pl.BlockSpec(memory_space=pl.ANY)],
        out_specs=pl.BlockSpec(memory_space=pl.ANY),
        scratch_shapes=[pltpu.SemaphoreType.DMA((7,)), pltpu.SemaphoreType.DMA((7,)), pltpu.SemaphoreType.DMA],
    )(xp)


def _relations(x, y):
    return [(x, y), (1 - x, y), (x, 1 - y), (1 - x, 1 - y)]


def _remote(src, dst, send_sem, recv_sem, device):
    return pltpu.make_async_remote_copy(src_ref=src, dst_ref=dst, send_sem=send_sem, recv_sem=recv_sem,
                                        device_id=device, device_id_type=MESH)


def _sibling_copies(g_refs, out_refs, send, recv):
    x, y, c = _position()
    n = len(g_refs)
    return [_remote(g.at[4 * px + 2 * py + (1 - c)], o.at[k], send.at[k * n + i], recv.at[k * n + i], (x, y, 1 - c))
            for k, (px, py) in enumerate(_relations(x, y)) for i, (g, o) in enumerate(zip(g_refs, out_refs))]


def _chip_copies(w_refs, out_refs, send, recv):
    x, y, c = _position()
    n = len(w_refs)
    return [_remote(w.at[k + 1], o.at[k], send.at[k * n + i], recv.at[k * n + i], (px, py, c))
            for k, (px, py) in enumerate(_relations(x, y)[1:]) for i, (w, o) in enumerate(zip(w_refs, out_refs))]


def _exchange(arrays, lead, relations, copies_fn, name):
    n = len(arrays)

    def body(*refs):
        copies = copies_fn(refs[:n], refs[n:2 * n], refs[2 * n], refs[2 * n + 1])
        for cp in copies:
            cp.start()
        for cp in copies:
            cp.wait()

    hbm = pl.BlockSpec(memory_space=pl.ANY)
    dma = pltpu.SemaphoreType.DMA
    return pl.pallas_call(
        body, name=name, out_shape=[_sds((lead,) + a.shape[1:], a.dtype) for a in arrays],
        in_specs=[hbm] * n, out_specs=[hbm] * n,
        scratch_shapes=[dma((relations * n,)), dma((relations * n,))],
    )(*arrays)


def _rs_sibling_exchange(gs):
    return _exchange(gs, 4, 4, _sibling_copies, "rs_sibling_exchange")


def _rs_chip_exchange(wires):
    return _exchange(wires, 3, 3, _chip_copies, "rs_chip_exchange")


def _side_exchange(arrays, lead, relations, copies_fn):
    shapes = [_sds((lead,) + a.shape[1:], a.dtype) for a in arrays]
    return list(arrays), shapes, relations * len(arrays), lambda ins, outs, send, recv, local: copies_fn(ins, outs, send, recv)


def _side_sibling(gs):
    return _side_exchange(gs, 4, 4, _sibling_copies)


def _side_chips(wires):
    return _side_exchange(wires, 3, 3, _chip_copies)


def _rs_chip_sums(gs, sibs, dest_idx):
    n = len(gs)

    def body(idx_ref, *refs):
        g_refs, s_refs, own_refs, wire_refs = refs[:n], refs[n:2 * n], refs[2 * n:3 * n], refs[3 * n:]
        totals = [g[0] + s[0] for g, s in zip(g_refs, s_refs)]
        for total, wire in zip(totals, wire_refs):
            wire[0] = total.astype(wire.dtype)

        @pl.when(pl.program_id(1) == 0)
        def _():
            for total, own in zip(totals, own_refs):
                own[...] = total

    def blocks(a, index_map, squeeze):
        rb = a.shape[1] // RS_ROW_BLOCKS
        return pl.BlockSpec((rb, a.shape[2]) if squeeze else (1, rb, a.shape[2]), index_map)

    return pl.pallas_call(
        body, name="rs_chip_sums",
        grid_spec=pltpu.PrefetchScalarGridSpec(
            num_scalar_prefetch=1, grid=(RS_ROW_BLOCKS, 4),
            in_specs=[blocks(g, lambda r, k, idx: (idx[k], r, 0), False) for g in gs]
            + [blocks(g, lambda r, k, idx: (k, r, 0), False) for g in gs],
            out_specs=[blocks(g, lambda r, k, idx: (r, 0), True) for g in gs]
            + [blocks(g, lambda r, k, idx: (k, r, 0), False) for g in gs]),
        out_shape=[_sds(g.shape[1:], jnp.float32) for g in gs] + [_sds((4,) + g.shape[1:], _MXU) for g in gs],
        compiler_params=_cparams(("parallel", "arbitrary")),
    )(dest_idx, *gs, *sibs)


def _side_gather(xp):
    def make(ins, outs, send, recv, local):
        (x_ref,), (out_ref,) = ins, outs
        x, y, c = _position()
        me = 4 * x + 2 * y + c
        copies = [pltpu.make_async_copy(x_ref, out_ref.at[me], local.at[0])]
        for k in range(1, N_DEV):
            peer = (1 - x if k & 4 else x, 1 - y if k & 2 else y, 1 - c if k & 1 else c)
            copies.append(_remote(x_ref, out_ref.at[me], send.at[k - 1], recv.at[k - 1], peer))
        return copies

    return [xp], [_sds((N_DEV,) + xp.shape, xp.dtype)], N_DEV - 1, make


def _call(body, args, side, *, name, grid, in_specs, out_specs, out_shape, scratch_shapes=(), semantics):
    in_specs, out_specs, out_shape = list(in_specs), list(out_specs), list(out_shape)
    n_in, n_out, n_scr = len(in_specs), len(out_specs), len(scratch_shapes)
    if side is None:
        outs = pl.pallas_call(body, name=name, grid=grid, in_specs=in_specs, out_specs=out_specs, out_shape=out_shape,
                              scratch_shapes=list(scratch_shapes), compiler_params=_cparams(semantics))(*args)
        return list(outs), []
    arrays, shapes, n_remote, make = side
    n_side_in, n_side_out = len(arrays), len(shapes)
    hbm = pl.BlockSpec(memory_space=pl.ANY)

    def with_copies(*refs):
        main_in, refs = refs[:n_in], refs[n_in:]
        side_in, refs = refs[:n_side_in], refs[n_side_in:]
        main_out, refs = refs[:n_out], refs[n_out:]
        side_out, refs = refs[:n_side_out], refs[n_side_out:]
        main_scr, (send, recv, local) = refs[:n_scr], refs[n_scr:]
        copies = make(side_in, side_out, send, recv, local)
        ids = [pl.program_id(a) for a in range(len(grid))]
        first, last = ids[0] == 0, ids[0] == grid[0] - 1
        for i, size in zip(ids[1:], grid[1:]):
            first, last = jnp.logical_and(first, i == 0), jnp.logical_and(last, i == size - 1)

        @pl.when(first)
        def _():
            for cp in copies:
                cp.start()

        body(*main_in, *main_out, *main_scr)

        @pl.when(last)
        def _():
            for cp in copies:
                cp.wait()

    dma = pltpu.SemaphoreType.DMA
    outs = pl.pallas_call(
        with_copies, name=name, grid=grid, in_specs=in_specs + [hbm] * n_side_in,
        out_specs=out_specs + [hbm] * n_side_out, out_shape=out_shape + list(shapes),
        scratch_shapes=list(scratch_shapes) + [dma((n_remote,)), dma((n_remote,)), dma((1,))],
        compiler_params=_cparams(("arbitrary",) * len(grid)),
    )(*args, *arrays)
    return list(outs[:n_out]), list(outs[n_out:])


def _all_reduce_small(v):
    rows_n = v.shape[0]

    def body(v_ref, out_ref, buf, send_sems, recv_sems):
        x, y, c = _position()
        me = 4 * x + 2 * y + c
        buf[me] = v_ref[...]
        copies = []
        for k in range(1, N_DEV):
            px = 1 - x if k & 4 else x
            py = 1 - y if k & 2 else y
            pc = 1 - c if k & 1 else c
            copies.append(pltpu.make_async_remote_copy(
                src_ref=v_ref, dst_ref=buf.at[me],
                send_sem=send_sems.at[k - 1], recv_sem=recv_sems.at[k - 1], device_id=(px, py, pc), device_id_type=MESH))
        for cp in copies:
            cp.start()
        for cp in copies:
            cp.wait()
        total = buf[0]
        for d in range(1, N_DEV):
            total = total + buf[d]
        out_ref[...] = total

    return pl.pallas_call(
        body, name="ar_small",
        out_shape=_sds((rows_n, LANES), jnp.float32),
        in_specs=[pl.BlockSpec(memory_space=pltpu.VMEM)],
        out_specs=pl.BlockSpec(memory_space=pltpu.VMEM),
        scratch_shapes=[pltpu.VMEM((N_DEV, rows_n, LANES), jnp.float32),
                        pltpu.SemaphoreType.DMA((N_DEV - 1,)), pltpu.SemaphoreType.DMA((N_DEV - 1,))],
    )(v)


def _adamw_math(w, g, m, v):
    m = ADAM_B1 * m + (1.0 - ADAM_B1) * g
    v = ADAM_B2 * v + (1.0 - ADAM_B2) * (g * g)
    m_hat = m / (1.0 - ADAM_B1 ** ADAM_STEP)
    v_hat = v / (1.0 - ADAM_B2 ** ADAM_STEP)
    delta = -ADAM_LR * (m_hat / (jnp.sqrt(v_hat) + ADAM_EPS) + ADAM_WD * w)
    return delta, m, v


def _rs_sum(owns, recvs):
    n = len(owns)

    def body(*refs):
        own_refs, recv_refs, out_refs = refs[:n], refs[n:4 * n], refs[4 * n:]
        for i in range(n):
            r0, r1, r2 = recv_refs[3 * i:3 * i + 3]
            out_refs[i][...] = ((own_refs[i][...] + _f32(r0[0])) + _f32(r1[0])) + _f32(r2[0])

    def row(a):
        return pl.BlockSpec((a.shape[0] // RS_ROW_BLOCKS, a.shape[1]), lambda r: (r, 0))

    def slot(a, k):
        return pl.BlockSpec((1, a.shape[0] // RS_ROW_BLOCKS, a.shape[1]), lambda r: (k, r, 0))

    return pl.pallas_call(
        body, name="rs_sum", grid=(RS_ROW_BLOCKS,),
        in_specs=[row(a) for a in owns] + [slot(a, k) for a in owns for k in range(3)],
        out_specs=[row(a) for a in owns],
        out_shape=[_sds(a.shape, jnp.float32) for a in owns],
        compiler_params=_cparams(("parallel",)),
    )(*owns, *[r for r in recvs for _ in range(3)])


def _adamw(g, w, m, v, name):
    depth, a, b = w.shape

    def body(g_ref, w_ref, m_ref, v_ref, d_out, m_out, v_out):
        delta, m2, v2 = _adamw_math(w_ref[...], g_ref[...], m_ref[...], v_ref[...])
        d_out[...] = delta
        m_out[...] = m2
        v_out[...] = v2

    layer = pl.BlockSpec((1, a, b), lambda l: (l, 0, 0))
    return pl.pallas_call(
        body, name="adamw_" + name, grid=(depth,),
        in_specs=[layer] * 4, out_specs=[layer] * 3,
        out_shape=[_sds(w.shape, jnp.float32)] * 3,
        compiler_params=_cparams(("parallel",)),
    )(g, w, m, v)


def _adamw_small(g, w, m, v):
    def body(g_ref, w_ref, m_ref, v_ref, d_out, m_out, v_out):
        delta, m2, v2 = _adamw_math(w_ref[...], g_ref[...], m_ref[...], v_ref[...])
        d_out[...] = delta
        m_out[...] = m2
        v_out[...] = v2

    vm = pl.BlockSpec(memory_space=pltpu.VMEM)
    return pl.pallas_call(
        body, name="adamw_small",
        in_specs=[vm] * 4, out_specs=[vm] * 3,
        out_shape=[_sds(g.shape, jnp.float32)] * 3,
    )(g, w, m, v)


def _f_chunk(f):
    for cand in (1408, 1024, 512, 256, 128):
        if f % cand == 0:
            return cand
    return f


def _ffn_fwd(x, gain, wg, wu, wd):
    t, d = x.shape
    f = wg.shape[0]
    tb = min(_TB, t)
    fc = _f_chunk(f)

    def body(x_ref, g_ref, wg_ref, wu_ref, wd_ref, xo_ref, h_ref, s_ref, fa_ref, fu_ref):
        xv = x_ref[...]
        hb = _mx(xv * _rsq(_sumsq(xv), d) * g_ref[...])
        h_ref[...] = hb
        y = jnp.zeros((tb, d), jnp.float32)
        for c0 in range(0, f, fc):
            a = _dot_nt(hb, wg_ref[c0:c0 + fc, :])
            u = _dot_nt(hb, wu_ref[c0:c0 + fc, :])
            sig = jax.nn.sigmoid(a)
            silu = a * sig
            s = _mx(silu * u)
            s_ref[:, c0:c0 + fc] = s
            fa_ref[:, c0:c0 + fc] = _mx(u * (sig * (1.0 + a * (1.0 - sig))))
            fu_ref[:, c0:c0 + fc] = _mx(silu)
            y = y + _dot(s, wd_ref[c0:c0 + fc, :])
        xo_ref[...] = xv + 0.5 * y

    return pl.pallas_call(
        body, name="ffn_fwd", grid=(t // tb,),
        in_specs=[_rows(tb, d), _const((1, d)), _const((f, d)), _const((f, d)), _const((f, d))],
        out_specs=[_rows(tb, d), _rows(tb, d), _rows(tb, f), _rows(tb, f), _rows(tb, f)],
        out_shape=[_sds((t, d), jnp.float32), _sds((t, d), _MXU), _sds((t, f), _MXU), _sds((t, f), _MXU),
                   _sds((t, f), _MXU)],
        compiler_params=_cparams(("parallel",)),
    )(x, gain, wg, wu, wd)


def _ffn_dgrad(x, gain, dxo, fa, fu, wg, wu, wd, side=None):
    t, d = x.shape
    f = wg.shape[0]
    tb = min(_TB, t)
    fc = _f_chunk(f)

    def body(x_ref, g_ref, dxo_ref, fa_ref, fu_ref, wg_ref, wu_ref, wd_ref, dxi_ref, da_ref, du_ref, dy_ref, dg_ref):
        xv = x_ref[...]
        gv = g_ref[...]
        r = _rsq(_sumsq(xv), d)
        xhat = xv * r
        dxo = dxo_ref[...]
        dyb = _mx(0.5 * dxo)
        dy_ref[...] = dyb
        dh = jnp.zeros((tb, d), jnp.float32)
        for c0 in range(0, f, fc):
            ds = _dot_nt(dyb, wd_ref[c0:c0 + fc, :])
            da = _mx(ds * _f32(fa_ref[:, c0:c0 + fc]))
            du = _mx(ds * _f32(fu_ref[:, c0:c0 + fc]))
            da_ref[:, c0:c0 + fc] = da
            du_ref[:, c0:c0 + fc] = du
            dh = dh + _dot(da, wg_ref[c0:c0 + fc, :]) + _dot(du, wu_ref[c0:c0 + fc, :])

        @pl.when(pl.program_id(0) == 0)
        def _():
            dg_ref[...] = jnp.zeros_like(dg_ref)

        dg_ref[...] += _colsum(dh * xhat)
        dn = dh * gv
        dxi_ref[...] = dxo + r * (dn - xhat * (_rowsum(dn * xhat) * (1.0 / d)))

    return _call(
        body, (x, gain, dxo, fa, fu, wg, wu, wd), side, name="ffn_dgrad", grid=(t // tb,),
        in_specs=[_rows(tb, d), _const((1, d)), _rows(tb, d), _rows(tb, f), _rows(tb, f),
                  _const((f, d)), _const((f, d)), _const((f, d))],
        out_specs=[_rows(tb, d), _rows(tb, f), _rows(tb, f), _rows(tb, d), _acc((1, d))],
        out_shape=[_sds((t, d), jnp.float32), _sds((t, f), _MXU), _sds((t, f), _MXU), _sds((t, d), _MXU),
                   _sds((1, d), jnp.float32)],
        semantics=("arbitrary",))


def _tn_matmul(a, b, name):
    t, m = a.shape
    n = b.shape[1]
    tk = min(_TK, t)
    tn = n
    while m * tn * 4 > 12 * 1024 * 1024 and tn % 256 == 0:
        tn //= 2

    def body(a_ref, b_ref, o_ref):
        @pl.when(pl.program_id(1) == 0)
        def _():
            o_ref[...] = jnp.zeros_like(o_ref)

        o_ref[...] += _dot_tn(a_ref[...], b_ref[...])

    return pl.pallas_call(
        body, name=name, grid=(n // tn, t // tk),
        in_specs=[pl.BlockSpec((tk, m), lambda j, k: (k, 0)), pl.BlockSpec((tk, tn), lambda j, k: (k, j))],
        out_specs=pl.BlockSpec((m, tn), lambda j, k: (0, j)),
        out_shape=_sds((m, n), jnp.float32),
        compiler_params=_cparams(("parallel", "arbitrary")),
    )(a, b)


PREP_WEIGHTS = ("mix_g", "w_in", "g_qa", "wqb", "g_kva", "w_kvb", "gq_n", "gq_r", "gk_n", "gk_r", "g_sq", "g_sk")
C_CQ, C_CKV, C_KPE, C_QS = 0, MLA_Q_RANK, MLA_Q_RANK + MLA_KV_RANK, MLA_Q_RANK + MLA_KV_RANK + LANES
C_KS = C_QS + SWA_HEADS * SWA_D
C_VS = C_KS + LANES
W_IN_PACKED = C_VS + LANES


def _prep_specs(p):
    return [_const(p[n].shape) for n in PREP_WEIGHTS]


def _pair_norm_rope(t, gain, cos, sin_s):
    return _rope(t * _rsq(_half_sums(t * t), SWA_D) * gain, cos, sin_s)


def _prep_fwd(x, cos, sin_s, p):
    t, d = x.shape
    tb = min(_TB_MIX, t)

    def body(x_ref, cos_ref, sin_ref, mix_g, w_in, g_qa, wqb, g_kva, w_kvb, gq_n, gq_r, gk_n, gk_r, g_sq, g_sk,
             qa_ref, ka_ref, va_ref, qb_ref, kb_ref, vb_ref):
        xv = x_ref[...]
        cos_v, sin_v = cos_ref[...], sin_ref[...]
        hb = _mx(xv * _rsq(_sumsq(xv), d) * mix_g[...])
        proj = _dot_nt(hb, w_in[...])
        cq = proj[:, C_CQ:C_CKV]
        cqn = _mx(cq * _rsq(_sumsq(cq), MLA_Q_RANK) * g_qa[...])
        for h in range(MLA_HEADS):
            qh = _dot_nt(cqn, wqb[h])
            qn, qr = qh[:, :MLA_NOPE], qh[:, MLA_NOPE:]
            rh = _rsq(_sumsq(qn) + _sumsq(qr), MLA_QK)
            qa_ref[h, :, 0:MLA_NOPE] = (qn * rh * gq_n[...]).astype(qa_ref.dtype)
            qa_ref[h, :, MLA_NOPE:MLA_QK_PAD] = _rope(qr * rh * gq_r[...], cos_v, sin_v).astype(qa_ref.dtype)
        ckv = proj[:, C_CKV:C_KPE]
        ckvn = _mx(ckv * _rsq(_sumsq(ckv), MLA_KV_RANK) * g_kva[...])
        kpe = proj[:, C_KPE:C_QS]
        ss_pe = _sumsq(kpe)
        kv = _dot_nt(ckvn, w_kvb[...])
        for h in range(MLA_HEADS):
            c0 = h * (MLA_NOPE + MLA_V)
            kn = kv[:, c0:c0 + MLA_NOPE]
            rh = _rsq(_sumsq(kn) + ss_pe, MLA_QK)
            ka_ref[h, :, 0:MLA_NOPE] = (kn * rh * gk_n[...]).astype(ka_ref.dtype)
            ka_ref[h, :, MLA_NOPE:MLA_QK_PAD] = _rope(kpe * rh * gk_r[...], cos_v, sin_v).astype(ka_ref.dtype)
            va_ref[h] = kv[:, c0 + MLA_NOPE:c0 + MLA_NOPE + MLA_V].astype(va_ref.dtype)
        for j in range(SWA_PAIRS):
            c0 = C_QS + j * LANES
            qb_ref[j] = _pair_norm_rope(proj[:, c0:c0 + LANES], g_sq[...], cos_v, sin_v).astype(qb_ref.dtype)
        k0, k1 = _dup_halves(_pair_norm_rope(proj[:, C_KS:C_VS], g_sk[...], cos_v, sin_v))
        kb_ref[0] = k0.astype(kb_ref.dtype)
        kb_ref[1] = k1.astype(kb_ref.dtype)
        v0, v1 = _dup_halves(proj[:, C_VS:W_IN_PACKED])
        vb_ref[0] = v0.astype(vb_ref.dtype)
        vb_ref[1] = v1.astype(vb_ref.dtype)

    return pl.pallas_call(
        body, name="prep_fwd", grid=(t // tb,),
        in_specs=[_rows(tb, d), _rows(tb, LANES), _rows(tb, LANES)] + _prep_specs(p),
        out_specs=[_heads_rows(MLA_HEADS, tb, MLA_QK_PAD), _heads_rows(MLA_HEADS, tb, MLA_QK_PAD),
                   _heads_rows(MLA_HEADS, tb, MLA_V), _heads_rows(SWA_PAIRS, tb, LANES),
                   _heads_rows(SWA_KV, tb, LANES), _heads_rows(SWA_KV, tb, LANES)],
        out_shape=[_sds((MLA_HEADS, t, MLA_QK_PAD), _MXU), _sds((MLA_HEADS, t, MLA_QK_PAD), _MXU),
                   _sds((MLA_HEADS, t, MLA_V), _MXU), _sds((SWA_PAIRS, t, LANES), _MXU),
                   _sds((SWA_KV, t, LANES), _MXU), _sds((SWA_KV, t, LANES), _MXU)],
        compiler_params=_cparams(("parallel",)),
    )(x, cos, sin_s, *[p[n] for n in PREP_WEIGHTS])


def _prep_bwd(x, dxin, cos, sin_s, p, dqa, dka, dva, dqb, dkb, dvb):
    t, d = x.shape
    tb = min(_TB_MIX, t)
    n_w = len(PREP_WEIGHTS)

    def body(*refs):
        x_ref, dxin_ref, cos_ref, sin_ref = refs[:4]
        mix_g, w_in, g_qa, wqb, g_kva, w_kvb, gq_n, gq_r, gk_n, gk_r, g_sq, g_sk = refs[4:4 + n_w]
        dqa_ref, dka_ref, dva_ref, dqb_ref, dkb_ref, dvb_ref = refs[4 + n_w:10 + n_w]
        dx_ref = refs[10 + n_w]
        grads = dict(zip(PREP_WEIGHTS, refs[11 + n_w:11 + 2 * n_w]))
        dproj_ref, dkv_ref, dqh_ref = refs[11 + 2 * n_w:]

        @pl.when(pl.program_id(0) == 0)
        def _():
            for ref in grads.values():
                ref[...] = jnp.zeros_like(ref)

        xv = x_ref[...]
        cos_v, sin_v = cos_ref[...], sin_ref[...]
        r0 = _rsq(_sumsq(xv), d)
        xhat = xv * r0
        hb = _mx(xhat * mix_g[...])
        proj = _dot_nt(hb, w_in[...])

        cq = proj[:, C_CQ:C_CKV]
        rq = _rsq(_sumsq(cq), MLA_Q_RANK)
        cqh = cq * rq
        cqn = _mx(cqh * g_qa[...])
        dcqn = jnp.zeros((tb, MLA_Q_RANK), jnp.float32)
        for h in range(MLA_HEADS):
            qh = _dot_nt(cqn, wqb[h])
            qn, qr = qh[:, :MLA_NOPE], qh[:, MLA_NOPE:]
            rh = _rsq(_sumsq(qn) + _sumsq(qr), MLA_QK)
            xh_n, xh_r = qn * rh, qr * rh
            dy_n = dqa_ref[h, :, 0:MLA_NOPE]
            dy_r = _rope_bwd(dqa_ref[h, :, MLA_NOPE:MLA_QK_PAD], cos_v, sin_v)
            grads["gq_n"][...] += _colsum(dy_n * xh_n)
            grads["gq_r"][...] += _colsum(dy_r * xh_r)
            dqn, dqr = _norm_bwd([dy_n * gq_n[...], dy_r * gq_r[...]], [xh_n, xh_r], rh, MLA_QK)
            dqh_ref[:, 0:MLA_NOPE] = _mx(dqn)
            dqh_ref[:, MLA_NOPE:MLA_QK_PAD] = _mx(dqr)
            dqh = dqh_ref[...]
            grads["wqb"][h] += _dot_tn(dqh, cqn)
            dcqn = dcqn + _dot(dqh, wqb[h])
        grads["g_qa"][...] += _colsum(dcqn * cqh)
        (dcq,) = _norm_bwd([dcqn * g_qa[...]], [cqh], rq, MLA_Q_RANK)
        dproj_ref[:, C_CQ:C_CKV] = _mx(dcq)

        ckv = proj[:, C_CKV:C_KPE]
        rkv = _rsq(_sumsq(ckv), MLA_KV_RANK)
        ckvh = ckv * rkv
        ckvn = _mx(ckvh * g_kva[...])
        kpe = proj[:, C_KPE:C_QS]
        ss_pe = _sumsq(kpe)
        kv = _dot_nt(ckvn, w_kvb[...])
        dkpe = jnp.zeros((tb, LANES), jnp.float32)
        for h in range(MLA_HEADS):
            c0 = h * (MLA_NOPE + MLA_V)
            c1 = c0 + MLA_NOPE
            kn = kv[:, c0:c1]
            rh = _rsq(_sumsq(kn) + ss_pe, MLA_QK)
            xh_n, xh_r = kn * rh, kpe * rh
            dy_n = dka_ref[h, :, 0:MLA_NOPE]
            dy_r = _rope_bwd(dka_ref[h, :, MLA_NOPE:MLA_QK_PAD], cos_v, sin_v)
            grads["gk_n"][...] += _colsum(dy_n * xh_n)
            grads["gk_r"][...] += _colsum(dy_r * xh_r)
            dkn, dkr = _norm_bwd([dy_n * gk_n[...], dy_r * gk_r[...]], [xh_n, xh_r], rh, MLA_QK)
            dkpe = dkpe + dkr
            dkv_ref[:, c0:c1] = _mx(dkn)
            dkv_ref[:, c1:c1 + MLA_V] = _mx(dva_ref[h])
        dkv = dkv_ref[...]
        grads["w_kvb"][...] += _dot_tn(dkv, ckvn)
        dckvn = _dot(dkv, w_kvb[...])
        grads["g_kva"][...] += _colsum(dckvn * ckvh)
        (dckv,) = _norm_bwd([dckvn * g_kva[...]], [ckvh], rkv, MLA_KV_RANK)
        dproj_ref[:, C_CKV:C_KPE] = _mx(dckv)
        dproj_ref[:, C_KPE:C_QS] = _mx(dkpe)

        def pair_bwd(tv, dy, g_ref, gname):
            r = _rsq(_half_sums(tv * tv), SWA_D)
            xh = tv * r
            dpre = _rope_bwd(dy, cos_v, sin_v)
            grads[gname][...] += _colsum(dpre * xh)
            dn = dpre * g_ref[...]
            return r * (dn - xh * (_half_sums(dn * xh) * (1.0 / SWA_D)))

        for j in range(SWA_PAIRS):
            c0 = C_QS + j * LANES
            dproj_ref[:, c0:c0 + LANES] = _mx(pair_bwd(proj[:, c0:c0 + LANES], dqb_ref[j], g_sq, "g_sq"))
        dproj_ref[:, C_KS:C_VS] = _mx(pair_bwd(proj[:, C_KS:C_VS], _undup_halves(dkb_ref[0], dkb_ref[1]), g_sk, "g_sk"))
        dproj_ref[:, C_VS:W_IN_PACKED] = _mx(_undup_halves(dvb_ref[0], dvb_ref[1]))

        dproj = dproj_ref[...]
        grads["w_in"][...] += _dot_tn(dproj, hb)
        dh = _dot(dproj, w_in[...])
        grads["mix_g"][...] += _colsum(dh * xhat)
        (dxv,) = _norm_bwd([dh * mix_g[...]], [xhat], r0, d)
        dx_ref[...] = dxin_ref[...] + dxv

    grad_shapes = [p[n].shape for n in PREP_WEIGHTS]
    return pl.pallas_call(
        body, name="prep_bwd", grid=(t // tb,),
        in_specs=[_rows(tb, d), _rows(tb, d), _rows(tb, LANES), _rows(tb, LANES)] + _prep_specs(p) + [
            _heads_rows(MLA_HEADS, tb, MLA_QK_PAD), _heads_rows(MLA_HEADS, tb, MLA_QK_PAD),
            _heads_rows(MLA_HEADS, tb, MLA_V), _heads_rows(SWA_PAIRS, tb, LANES),
            _heads_rows(SWA_KV, tb, LANES), _heads_rows(SWA_KV, tb, LANES)],
        out_specs=[_rows(tb, d)] + [_acc(s) for s in grad_shapes],
        out_shape=[_sds((t, d), jnp.float32)] + [_sds(s, jnp.float32) for s in grad_shapes],
        scratch_shapes=[pltpu.VMEM((tb, W_IN_PACKED), _MXU), pltpu.VMEM((tb, MLA_HEADS * (MLA_NOPE + MLA_V)), _MXU),
                        pltpu.VMEM((tb, MLA_QK_PAD), _MXU)],
        compiler_params=_cparams(("arbitrary",)),
    )(x, dxin, cos, sin_s, *[p[n] for n in PREP_WEIGHTS], dqa, dka, dva, dqb, dkb, dvb)


def _strips(n):
    step = min(_STRIP, n)
    return [slice(r, r + step) for r in range(0, n, step)]


def _mla_fwd(q, k, v, side=None):
    hn, t, dq = q.shape
    dv = v.shape[2]
    bq = min(_BQ, t)
    scale = MLA_QK ** -0.5
    scale2 = scale * LOG2_E

    def body(q_ref, k_ref, v_ref, o_ref, l_ref):
        i = pl.program_id(1)
        qv = q_ref[0]

        def step(first_block, width, carry, masked):
            m, l, acc = carry
            start = pl.multiple_of(first_block * bq, bq)
            s = _dot_nt(qv, k_ref[0, pl.ds(start, width), :])
            if masked:
                row = lax.broadcasted_iota(jnp.int32, (bq, width), 0)
                col = lax.broadcasted_iota(jnp.int32, (bq, width), 1)
                s = jnp.where(col <= row, s, NEG)
            m_new = jnp.maximum(m, _rowmax(s))
            alpha = jnp.exp2((m - m_new) * scale2)
            pv = jnp.exp2((s - m_new) * scale2)
            l = alpha * l + _rowsum(pv)
            acc = alpha * acc + _dot(_mx(pv), v_ref[0, pl.ds(start, width), :])
            return m_new, l, acc

        init = (jnp.full((bq, 1), NEG, jnp.float32), jnp.zeros((bq, 1), jnp.float32), jnp.zeros((bq, dv), jnp.float32))
        carry, done = init, 0
        for group in (4, 2, 1):
            count = (i - done) // group
            carry = lax.fori_loop(0, count, lambda g, c, done=done, group=group: step(done + group * g, group * bq, c, False), carry)
            done = done + group * count
        m, l, acc = step(i, bq, carry, True)
        o_ref[0] = acc / l
        l_ref[0] = jnp.broadcast_to(m * scale + jnp.log(l), (bq, LANES))

    return _call(
        body, (q, k, v), side, name="mla_fwd", grid=(hn, t // bq),
        in_specs=[pl.BlockSpec((1, bq, dq), lambda h, i: (h, i, 0)),
                  pl.BlockSpec((1, t, dq), lambda h, i: (h, 0, 0)),
                  pl.BlockSpec((1, t, dv), lambda h, i: (h, 0, 0))],
        out_specs=[pl.BlockSpec((1, bq, dv), lambda h, i: (h, i, 0)),
                   pl.BlockSpec((1, bq, LANES), lambda h, i: (h, i, 0))],
        out_shape=[_sds((hn, t, dv), jnp.float32), _sds((hn, t, LANES), jnp.float32)],
        semantics=("parallel", "arbitrary"))


def _mla_bwd(q, k, v, do, lse_rows, dsum_rows, side=None):
    hn, t, dq_w = q.shape
    dv_w = v.shape[2]
    bq = min(_BQ, t)
    nb = t // bq
    scale = MLA_QK ** -0.5

    def body(q_ref, do_ref, l_ref, d_ref, k_ref, v_ref, dq_ref, dk_ref, dv_ref, st_scr, dpt_scr, p_scr, ds_scr):
        j = pl.program_id(1)

        @pl.when(j == 0)
        def _():
            dq_ref[...] = jnp.zeros_like(dq_ref)

        kv = k_ref[0]
        vv = v_ref[0]
        dk_ref[0] = jnp.zeros((bq, dq_w), jnp.float32)
        dv_ref[0] = jnp.zeros((bq, dv_w), jnp.float32)

        def tile(i, masked):
            start = pl.multiple_of(i * bq, bq)
            qv = q_ref[0, pl.ds(start, bq), :]
            dov = do_ref[0, pl.ds(start, bq), :]
            st_scr[...] = _dot_nt(kv, qv)
            dpt_scr[...] = _dot_nt(vv, dov)
            lse2 = l_ref[0, i] * LOG2_E
            dsum = d_ref[0, i]
            for rows in _strips(bq):
                pt = jnp.exp2(st_scr[rows, :] * (scale * LOG2_E) - lse2)
                if masked:
                    n_rows = rows.stop - rows.start
                    row = lax.broadcasted_iota(jnp.int32, (n_rows, bq), 0) + rows.start
                    col = lax.broadcasted_iota(jnp.int32, (n_rows, bq), 1)
                    pt = jnp.where(row <= col, pt, 0.0)
                p_scr[rows, :] = _mx(pt)
                ds_scr[rows, :] = _mx(pt * (dpt_scr[rows, :] - dsum) * scale)
            ds_t = ds_scr[...]
            dv_ref[0] += _dot(p_scr[...], dov)
            dk_ref[0] += _dot(ds_t, qv)
            dq_ref[0, pl.ds(start, bq), :] += _dot_tn(ds_t, kv)

        def loop_body(i, carry):
            tile(i, False)
            return carry

        tile(j, True)
        lax.fori_loop(j + 1, nb, loop_body, 0)

    return _call(
        body, (q, do, lse_rows, dsum_rows, k, v), side, name="mla_bwd", grid=(hn, nb),
        in_specs=[pl.BlockSpec((1, t, dq_w), lambda h, j: (h, 0, 0)),
                  pl.BlockSpec((1, t, dv_w), lambda h, j: (h, 0, 0)),
                  pl.BlockSpec((1, nb, 1, bq), lambda h, j: (h, 0, 0, 0)),
                  pl.BlockSpec((1, nb, 1, bq), lambda h, j: (h, 0, 0, 0)),
                  pl.BlockSpec((1, bq, dq_w), lambda h, j: (h, j, 0)),
                  pl.BlockSpec((1, bq, dv_w), lambda h, j: (h, j, 0))],
        out_specs=[pl.BlockSpec((1, t, dq_w), lambda h, j: (h, 0, 0)),
                   pl.BlockSpec((1, bq, dq_w), lambda h, j: (h, j, 0)),
                   pl.BlockSpec((1, bq, dv_w), lambda h, j: (h, j, 0))],
        out_shape=[_sds((hn, t, dq_w), jnp.float32), _sds((hn, t, dq_w), jnp.float32), _sds((hn, t, dv_w), jnp.float32)],
        scratch_shapes=[pltpu.VMEM((bq, bq), jnp.float32), pltpu.VMEM((bq, bq), jnp.float32),
                        pltpu.VMEM((bq, bq), _MXU), pltpu.VMEM((bq, bq), _MXU)],
        semantics=("parallel", "arbitrary"))


STACK = SWA_GROUP * SWA_BLOCK


def _swa_stack(ref, c, rows):
    low = _low_half()
    parts = []
    for g in range(SWA_GROUP):
        tv = ref[SWA_GROUP // 2 * c + g // 2, rows, :]
        keep = low if g % 2 == 0 else jnp.logical_not(low)
        parts.append(_mx(jnp.where(keep, tv, jnp.zeros_like(tv))))
    return jnp.concatenate(parts, axis=0)


def _swa_cols(ref, c, rows):
    return jnp.concatenate([ref[SWA_GROUP * c + g, rows, 0:1] for g in range(SWA_GROUP)], axis=0)


def _swa_sink_col(s_ref, c):
    return jnp.concatenate([jnp.broadcast_to(s_ref[SWA_GROUP * c + g][:, 0:1], (SWA_BLOCK, 1))
                            for g in range(SWA_GROUP)], axis=0)


def _swa_band_masks():
    row = lax.broadcasted_iota(jnp.int32, (STACK, SWA_BLOCK), 0) & (SWA_BLOCK - 1)
    col = lax.broadcasted_iota(jnp.int32, (STACK, SWA_BLOCK), 1)
    return col <= row, col > row


def _swa_unstack_pairs(ref, c, rows, stacked):
    for pr in range(SWA_GROUP // 2):
        r0 = 2 * pr * SWA_BLOCK
        ref[SWA_GROUP // 2 * c + pr, rows, :] = _pick_halves(stacked[r0:r0 + SWA_BLOCK], stacked[r0 + SWA_BLOCK:r0 + 2 * SWA_BLOCK])


def _swa_blocks(t):
    nblk = t // SWA_BLOCK
    bps = min(_SWA_STEP, nblk)
    return nblk, bps, bps * SWA_BLOCK


def _swa_fwd(q, k, v, sinks):
    _, t, _ = q.shape
    nblk, bps, sb = _swa_blocks(t)
    scale = SWA_D ** -0.5

    def body(q_ref, k_ref, kp_ref, v_ref, vp_ref, s_ref, o_ref, l_ref):
        n = pl.program_id(0)
        m_cur, m_prev = _swa_band_masks()
        for c in range(SWA_KV):
            sink = _swa_sink_col(s_ref, c)
            for b in range(bps):
                rows = slice(b * SWA_BLOCK, (b + 1) * SWA_BLOCK)
                kc, vc = k_ref[c, rows, :], v_ref[c, rows, :]
                if b == 0:
                    kp, vp, mp = kp_ref[c], vp_ref[c], jnp.logical_and(m_prev, n > 0)
                else:
                    before = slice((b - 1) * SWA_BLOCK, b * SWA_BLOCK)
                    kp, vp, mp = k_ref[c, before, :], v_ref[c, before, :], m_prev
                qs = _swa_stack(q_ref, c, rows)
                s_c = jnp.where(m_cur, _dot_nt(qs, kc) * scale, NEG)
                s_p = jnp.where(mp, _dot_nt(qs, kp) * scale, NEG)
                m = jnp.maximum(jnp.maximum(_rowmax(s_c), _rowmax(s_p)), sink)
                e_c = jnp.exp(s_c - m)
                e_p = jnp.exp(s_p - m)
                denom = _rowsum(e_c) + _rowsum(e_p) + jnp.exp(sink - m)
                inv = 1.0 / denom
                o = _dot(_mx(e_c * inv), vc) + _dot(_mx(e_p * inv), vp)
                lse = m + jnp.log(denom)
                for g in range(SWA_GROUP):
                    l_ref[SWA_GROUP * c + g, rows, :] = jnp.broadcast_to(
                        lse[g * SWA_BLOCK:(g + 1) * SWA_BLOCK], (SWA_BLOCK, LANES))
                _swa_unstack_pairs(o_ref, c, rows, o)

    main = lambda n: (0, n, 0)
    prev = lambda n: (0, jnp.maximum(n * bps - 1, 0), 0)
    return pl.pallas_call(
        body, name="swa_fwd", grid=(nblk // bps,),
        in_specs=[pl.BlockSpec((SWA_PAIRS, sb, LANES), main),
                  pl.BlockSpec((SWA_KV, sb, LANES), main), pl.BlockSpec((SWA_KV, SWA_BLOCK, LANES), prev),
                  pl.BlockSpec((SWA_KV, sb, LANES), main), pl.BlockSpec((SWA_KV, SWA_BLOCK, LANES), prev),
                  _const((SWA_HEADS, 1, LANES))],
        out_specs=[pl.BlockSpec((SWA_PAIRS, sb, LANES), main), pl.BlockSpec((SWA_HEADS, sb, LANES), main)],
        out_shape=[_sds((SWA_PAIRS, t, LANES), jnp.float32), _sds((SWA_HEADS, t, LANES), jnp.float32)],
        compiler_params=_cparams(("parallel",)),
    )(q, k, k, v, v, sinks)


def _swa_bwd(q, k, v, sinks, do, lse, dsum):
    _, t, _ = q.shape
    nblk, bps, sb = _swa_blocks(t)
    steps = nblk // bps
    scale = SWA_D ** -0.5

    def body(q_ref, k_ref, kp_ref, v_ref, vp_ref, s_ref, do_ref, l_ref, d_ref, qn_ref, don_ref, ln_ref, dn_ref,
             dq_ref, dk_ref, dv_ref, ds_ref):
        n = pl.program_id(0)

        @pl.when(n == 0)
        def _():
            ds_ref[...] = jnp.zeros_like(ds_ref)

        m_cur, m_prev = _swa_band_masks()
        everything = slice(0, SWA_BLOCK)

        def probs(qs, keys, mask, lcol):
            return jnp.where(mask, jnp.exp(_dot_nt(qs, keys) * scale - lcol), 0.0)

        def dscores(pm, dos, vals, dcol):
            return _mx(pm * (_dot_nt(dos, vals) - dcol) * scale)

        for c in range(SWA_KV):
            sink = _swa_sink_col(s_ref, c)
            dk_acc = [jnp.zeros((SWA_BLOCK, LANES), jnp.float32) for _ in range(bps)]
            dv_acc = [jnp.zeros((SWA_BLOCK, LANES), jnp.float32) for _ in range(bps)]
            for b in range(bps):
                rows = slice(b * SWA_BLOCK, (b + 1) * SWA_BLOCK)
                kc, vc = k_ref[c, rows, :], v_ref[c, rows, :]
                if b == 0:
                    kp, vp, mp = kp_ref[c], vp_ref[c], jnp.logical_and(m_prev, n > 0)
                else:
                    before = slice((b - 1) * SWA_BLOCK, b * SWA_BLOCK)
                    kp, vp, mp = k_ref[c, before, :], v_ref[c, before, :], m_prev
                qs = _swa_stack(q_ref, c, rows)
                dos = _swa_stack(do_ref, c, rows)
                lcol = _swa_cols(l_ref, c, rows)
                dcol = _swa_cols(d_ref, c, rows)
                p_c = probs(qs, kc, m_cur, lcol)
                p_p = probs(qs, kp, mp, lcol)
                ds_c = dscores(p_c, dos, vc, dcol)
                ds_p = dscores(p_p, dos, vp, dcol)
                _swa_unstack_pairs(dq_ref, c, rows, _dot(ds_c, kc) + _dot(ds_p, kp))
                dk_acc[b] = dk_acc[b] + _dot_tn(ds_c, qs)
                dv_acc[b] = dv_acc[b] + _dot_tn(_mx(p_c), dos)
                if b > 0:
                    dk_acc[b - 1] = dk_acc[b - 1] + _dot_tn(ds_p, qs)
                    dv_acc[b - 1] = dv_acc[b - 1] + _dot_tn(_mx(p_p), dos)
                p_sink = jnp.exp(sink - lcol) * dcol
                for g in range(SWA_GROUP):
                    ds_ref[SWA_GROUP * c + g] += -jnp.sum(p_sink[g * SWA_BLOCK:(g + 1) * SWA_BLOCK])
            tail = slice((bps - 1) * SWA_BLOCK, bps * SWA_BLOCK)
            kc, vc = k_ref[c, tail, :], v_ref[c, tail, :]
            qs = _swa_stack(qn_ref, c, everything)
            dos = _swa_stack(don_ref, c, everything)
            lcol = _swa_cols(ln_ref, c, everything)
            dcol = _swa_cols(dn_ref, c, everything)
            p_p = probs(qs, kc, jnp.logical_and(m_prev, n < steps - 1), lcol)
            ds_p = dscores(p_p, dos, vc, dcol)
            dk_acc[bps - 1] = dk_acc[bps - 1] + _dot_tn(ds_p, qs)
            dv_acc[bps - 1] = dv_acc[bps - 1] + _dot_tn(_mx(p_p), dos)
            for b in range(bps):
                rows = slice(b * SWA_BLOCK, (b + 1) * SWA_BLOCK)
                dk_ref[c, rows, :] = dk_acc[b]
                dv_ref[c, rows, :] = dv_acc[b]

    main = lambda n: (0, n, 0)
    prev = lambda n: (0, jnp.maximum(n * bps - 1, 0), 0)
    nxt = lambda n: (0, jnp.minimum((n + 1) * bps, nblk - 1), 0)
    pairs = pl.BlockSpec((SWA_PAIRS, sb, LANES), main)
    kvs = pl.BlockSpec((SWA_KV, sb, LANES), main)
    kv_prev = pl.BlockSpec((SWA_KV, SWA_BLOCK, LANES), prev)
    stats = pl.BlockSpec((SWA_HEADS, sb, LANES), main)
    pairs_next = pl.BlockSpec((SWA_PAIRS, SWA_BLOCK, LANES), nxt)
    stats_next = pl.BlockSpec((SWA_HEADS, SWA_BLOCK, LANES), nxt)
    return pl.pallas_call(
        body, name="swa_bwd", grid=(steps,),
        in_specs=[pairs, kvs, kv_prev, kvs, kv_prev, _const((SWA_HEADS, 1, LANES)), pairs, stats, stats,
                  pairs_next, pairs_next, stats_next, stats_next],
        out_specs=[pairs, kvs, kvs, _acc((SWA_HEADS, 1, LANES))],
        out_shape=[_sds((SWA_PAIRS, t, LANES), jnp.float32), _sds((SWA_KV, t, LANES), jnp.float32),
                   _sds((SWA_KV, t, LANES), jnp.float32), _sds((SWA_HEADS, 1, LANES), jnp.float32)],
        compiler_params=_cparams(("arbitrary",)),
    )(q, k, k, v, v, sinks, do, lse, dsum, q, do, lse, dsum)


MIX_SLABS = 4
MIX_WIDTH = MIX_SLABS * LANES


def _mix_out_fwd(x, oa, ob, ga, gb, wo_a, wo_b):
    t, d = x.shape
    tb = min(_TB, t)

    def body(x_ref, oa_ref, ob_ref, ga_ref, gb_ref, woa_ref, wob_ref, xo_ref):
        y = x_ref[...]
        for o_ref, g_ref, w_ref in ((oa_ref, ga_ref, woa_ref), (ob_ref, gb_ref, wob_ref)):
            r = _rsq(sum(_sumsq(o_ref[h]) for h in range(MIX_SLABS)), MIX_WIDTH)
            for h in range(MIX_SLABS):
                y = y + _dot(_mx(o_ref[h] * r * g_ref[h]), w_ref[h])
        xo_ref[...] = y

    slab = _heads_rows(MIX_SLABS, tb, LANES)
    return pl.pallas_call(
        body, name="mix_out_fwd", grid=(t // tb,),
        in_specs=[_rows(tb, d), slab, slab, _const(ga.shape), _const(gb.shape), _const(wo_a.shape), _const(wo_b.shape)],
        out_specs=_rows(tb, d),
        out_shape=_sds((t, d), jnp.float32),
        compiler_params=_cparams(("parallel",)),
    )(x, oa, ob, ga, gb, wo_a, wo_b)


def _mix_out_bwd(dx, oa, ob, ga, gb, wo_a, wo_b):
    t, d = dx.shape
    tb = min(_TB, t)

    def group(o_ref, g_ref, w_ref, dyb, do_ref, n_ref, col0, dg_ref):
        r = _rsq(sum(_sumsq(o_ref[h]) for h in range(MIX_SLABS)), MIX_WIDTH)
        xh, dn = [], []
        for h in range(MIX_SLABS):
            xh.append(o_ref[h] * r)
            n_ref[:, col0 + h * LANES:col0 + (h + 1) * LANES] = _mx(xh[h] * g_ref[h])
            dm = _dot_nt(dyb, w_ref[h])
            dg_ref[h] += _colsum(dm * xh[h])
            dn.append(dm * g_ref[h])
        c = sum(_rowsum(dn[h] * xh[h]) for h in range(MIX_SLABS)) * (1.0 / MIX_WIDTH)
        prods = []
        for h in range(MIX_SLABS):
            do = r * (dn[h] - xh[h] * c)
            do_ref[h] = do.astype(do_ref.dtype)
            prods.append(do * o_ref[h])
        return prods

    def body(dx_ref, oa_ref, ob_ref, ga_ref, gb_ref, woa_ref, wob_ref,
             doa_ref, dsa_ref, dob_ref, dsb_ref, n_ref, dy_ref, dga_ref, dgb_ref):
        @pl.when(pl.program_id(0) == 0)
        def _():
            dga_ref[...] = jnp.zeros_like(dga_ref)
            dgb_ref[...] = jnp.zeros_like(dgb_ref)

        dyb = _mx(dx_ref[...])
        dy_ref[...] = dyb
        for h, pr in enumerate(group(oa_ref, ga_ref, woa_ref, dyb, doa_ref, n_ref, 0, dga_ref)):
            dsa_ref[h] = jnp.broadcast_to(_rowsum(pr), (tb, LANES))
        low = _low_half()
        for j, pr in enumerate(group(ob_ref, gb_ref, wob_ref, dyb, dob_ref, n_ref, MIX_WIDTH, dgb_ref)):
            dsb_ref[2 * j] = jnp.broadcast_to(_rowsum(jnp.where(low, pr, 0.0)), (tb, LANES))
            dsb_ref[2 * j + 1] = jnp.broadcast_to(_rowsum(jnp.where(low, 0.0, pr)), (tb, LANES))

    slab = _heads_rows(MIX_SLABS, tb, LANES)
    return pl.pallas_call(
        body, name="mix_out_bwd", grid=(t // tb,),
        in_specs=[_rows(tb, d), slab, slab, _const(ga.shape), _const(gb.shape), _const(wo_a.shape), _const(wo_b.shape)],
        out_specs=[slab, slab, slab, _heads_rows(SWA_HEADS, tb, LANES), _rows(tb, 2 * MIX_WIDTH), _rows(tb, d),
                   _acc(ga.shape), _acc(gb.shape)],
        out_shape=[_sds((MIX_SLABS, t, LANES), _MXU), _sds((MIX_SLABS, t, LANES), jnp.float32),
                   _sds((MIX_SLABS, t, LANES), jnp.float32), _sds((SWA_HEADS, t, LANES), jnp.float32),
                   _sds((t, 2 * MIX_WIDTH), _MXU), _sds((t, d), _MXU),
                   _sds(ga.shape, jnp.float32), _sds(gb.shape, jnp.float32)],
        compiler_params=_cparams(("arbitrary",)),
    )(dx, oa, ob, ga, gb, wo_a, wo_b)


def _loss_head(y, target):
    t, d = y.shape
    tb = min(_TB, t)

    def body(y_ref, t_ref, dy_ref, acc_ref):
        @pl.when(pl.program_id(0) == 0)
        def _():
            acc_ref[...] = jnp.zeros_like(acc_ref)

        err = y_ref[...] - t_ref[...]
        dy_ref[...] = err * (1.0 / d)
        acc_ref[...] += jnp.sum(err * err)

    return pl.pallas_call(
        body, name="loss_head", grid=(t // tb,),
        in_specs=[_rows(tb, d), _rows(tb, d)],
        out_specs=[_rows(tb, d), _acc((8, LANES))],
        out_shape=[_sds((t, d), jnp.float32), _sds((8, LANES), jnp.float32)],
        compiler_params=_cparams(("arbitrary",)),
    )(y, target)


def _is_transposed(name):
    return name not in ROW_SHARDED


def _pack_layer(shards, l, width):
    rows = [(shards[n][l].T if _is_transposed(n) else shards[n][l]).reshape(-1, width) for n in BIG]
    return jnp.concatenate(rows, axis=0)


def _full_shape(like, name):
    _, a, b = like[name].shape
    return (N_DEV * b, a) if _is_transposed(name) else (N_DEV * a, b)


def _unpack_full(gathered, like):
    out, off = {}, 0
    for n in BIG:
        rows_n = like[n][0].size // gathered.shape[-1]
        out[n] = gathered[:, off:off + rows_n].reshape(_full_shape(like, n))
        off += rows_n
    return out


def _stored(a, name):
    return jnp.swapaxes(a, 1, 2) if _is_transposed(name) else a


def _grads_by_destination(grads, width):
    by_dest = lambda n: grads[n].reshape(N_DEV, -1, width)
    return [by_dest(n) for n in FFN_BIG] + [jnp.concatenate([by_dest(n) for n in OTHER_BIG], axis=1)]


def _shards_from_rows(rows, like):
    out = dict(zip(FFN_BIG, rows[:len(FFN_BIG)]))
    rest, off = rows[len(FFN_BIG)], 0
    for n in OTHER_BIG:
        _, a, b = like[n].shape
        rows_n = a * b // rest.shape[-1]
        out[n] = rest[off:off + rows_n].reshape((b, a) if _is_transposed(n) else (a, b))
        off += rows_n
    return out


def _small_rows(n_elems):
    return -(-n_elems // LANES)


def _pack_small(arrays):
    parts = []
    for n in SMALL:
        v = arrays[n]
        depth, width = v.shape
        padded = _small_rows(width) * LANES
        parts.append(jnp.pad(v, ((0, 0), (0, padded - width))).reshape(-1, LANES))
    packed = jnp.concatenate(parts, axis=0)
    return jnp.pad(packed, ((0, (-packed.shape[0]) % 8), (0, 0)))


def _unpack_small(packed, like):
    out, off = {}, 0
    for n in SMALL:
        depth, width = like[n].shape
        rows_n = _small_rows(width)
        seg = packed[off:off + depth * rows_n].reshape(depth, rows_n * LANES)
        out[n] = seg[:, :width]
        off += depth * rows_n
    return out


def _rope_tables(t):
    pos = jnp.arange(t, dtype=jnp.float32)
    inv = 1.0 / (ROPE_THETA ** (jnp.arange(0, MLA_ROPE, 2, dtype=jnp.float32) / MLA_ROPE))
    ang = pos[:, None] * inv[None, :]
    cos, sin = jnp.cos(ang), jnp.sin(ang)
    return jnp.concatenate([cos, cos, cos, cos], axis=1), jnp.concatenate([-sin, sin, -sin, sin], axis=1)


def _pad_lanes(a, width):
    return jnp.pad(a, [(0, 0)] * (a.ndim - 1) + [(0, width - a.shape[-1])])


def _layer_params(full, small, l):
    w_in = full["w_in"]
    d = w_in.shape[1]
    mla_rows = W_IN_COLS[0]
    w_in_p = jnp.concatenate([w_in[:mla_rows], jnp.zeros((LANES - MLA_ROPE, d), w_in.dtype), w_in[mla_rows:]], axis=0)
    wqb = full["mla_w_q_b"].reshape(MLA_HEADS, MLA_QK, MLA_Q_RANK)
    wqb = jnp.pad(wqb, ((0, 0), (0, MLA_QK_PAD - MLA_QK), (0, 0)))
    row = lambda name: small[name][l][None, :]
    twice = lambda g: jnp.concatenate([g, g], axis=1)
    prep = {
        "mix_g": row("mix_norm"), "w_in": w_in_p,
        "g_qa": row("mla_q_a_norm"), "wqb": wqb,
        "g_kva": row("mla_kv_a_norm"), "w_kvb": full["mla_w_kv_b"],
        "gq_n": row("mla_q_norm")[:, :MLA_NOPE], "gq_r": _pad_lanes(row("mla_q_norm")[:, MLA_NOPE:], LANES),
        "gk_n": row("mla_k_norm")[:, :MLA_NOPE], "gk_r": _pad_lanes(row("mla_k_norm")[:, MLA_NOPE:], LANES),
        "g_sq": twice(row("swa_q_norm")), "g_sk": twice(row("swa_k_norm")),
    }
    return {
        "prep": prep,
        "ffn1": (row("ffn1_norm"), full["ffn1_w_gate"], full["ffn1_w_up"], full["ffn1_w_down"]),
        "ffn2": (row("ffn2_norm"), full["ffn2_w_gate"], full["ffn2_w_up"], full["ffn2_w_down"]),
        "sinks": jnp.broadcast_to(small["swa_sinks"][l][:, None, None], (SWA_HEADS, 1, LANES)),
        "ga": small["mla_out_norm"][l].reshape(MIX_SLABS, 1, LANES),
        "gb": small["swa_out_norm"][l].reshape(MIX_SLABS, 1, LANES),
        "wo_a": full["w_o"][:MIX_WIDTH].reshape(MIX_SLABS, LANES, d),
        "wo_b": full["w_o"][MIX_WIDTH:].reshape(MIX_SLABS, LANES, d),
    }


def _ffn_backward(x_in, dxo, kept, params, tag, side=None):
    gain, wg, wu, wd = params
    h, s, fa, fu = kept
    (dxi, da, du, dy, dg), side_out = _ffn_dgrad(x_in, gain, dxo, fa, fu, wg, wu, wd, side)
    dwg = _tn_matmul(da, h, "wgrad_" + tag + "_gate")
    dwu = _tn_matmul(du, h, "wgrad_" + tag + "_up")
    dwd = _tn_matmul(s, dy, "wgrad_" + tag + "_down")
    return dxi, dg[0], dwg, dwu, dwd, side_out


def kernel(x, ffn1_norm, ffn1_w_gate, ffn1_w_up, ffn1_w_down, mix_norm, w_in, mla_q_a_norm, mla_w_q_b, mla_kv_a_norm, mla_w_kv_b, mla_q_norm, mla_k_norm, swa_q_norm, swa_k_norm, swa_sinks, mla_out_norm, swa_out_norm, w_o, ffn2_norm, ffn2_w_gate, ffn2_w_up, ffn2_w_down, loss_target, m_ffn1_norm, m_ffn1_w_gate, m_ffn1_w_up, m_ffn1_w_down, m_mix_norm, m_w_in, m_mla_q_a_norm, m_mla_w_q_b, m_mla_kv_a_norm, m_mla_w_kv_b, m_mla_q_norm, m_mla_k_norm, m_swa_q_norm, m_swa_k_norm, m_swa_sinks, m_mla_out_norm, m_swa_out_norm, m_w_o, m_ffn2_norm, m_ffn2_w_gate, m_ffn2_w_up, m_ffn2_w_down, v_ffn1_norm, v_ffn1_w_gate, v_ffn1_w_up, v_ffn1_w_down, v_mix_norm, v_w_in, v_mla_q_a_norm, v_mla_w_q_b, v_mla_kv_a_norm, v_mla_w_kv_b, v_mla_q_norm, v_mla_k_norm, v_swa_q_norm, v_swa_k_norm, v_swa_sinks, v_mla_out_norm, v_swa_out_norm, v_w_o, v_ffn2_norm, v_ffn2_w_gate, v_ffn2_w_up, v_ffn2_w_down):
    local = dict(locals())
    w = {n: local[n] for n in WEIGHTS}
    m = {n: local["m_" + n] for n in WEIGHTS}
    v = {n: local["v_" + n] for n in WEIGHTS}
    depth = ffn1_norm.shape[0]
    t, d = x.shape[-2], x.shape[-1]
    x2d = x.reshape(t, d)
    target = loss_target.reshape(t, d)
    bq = min(_BQ, t)

    big = {n: w[n] for n in BIG}
    packed = [_mx(_pack_layer(big, l, d)) for l in range(depth)]
    gathered = _all_gather(packed[0])
    cos, sin_s = _rope_tables(t)
    x_i, y_i, c_i = _position()
    dest_idx = jnp.stack([4 * px + 2 * py + c_i for px, py in _relations(x_i, y_i)]).astype(jnp.int32)

    params, saved = [], []
    xc = x2d
    for l in range(depth):
        pr = _layer_params(_unpack_full(gathered, big), w, l)
        params.append(pr)
        x0 = xc
        x1, *kept1 = _ffn_fwd(x0, *pr["ffn1"])
        qa, ka, va, qb, kb, vb = _prep_fwd(x1, cos, sin_s, pr["prep"])
        (oa, lse_a), side_out = _mla_fwd(qa, ka, va, _side_gather(packed[l + 1]) if l + 1 < depth else None)
        if side_out:
            gathered = side_out[0]
        ob, lse_b = _swa_fwd(qb, kb, vb, pr["sinks"])
        x2 = _mix_out_fwd(x1, oa, ob, pr["ga"], pr["gb"], pr["wo_a"], pr["wo_b"])
        x3, *kept2 = _ffn_fwd(x2, *pr["ffn2"])
        saved.append((x0, kept1, x1, qa, ka, va, qb, kb, vb, oa, lse_a, ob, lse_b, x2, kept2))
        xc = x3

    dx, sq_err = _loss_head(xc, target)
    loss = lax.psum(0.5 / d * sq_err[0, 0], MESH_AXES)

    grad_shards = [None] * depth
    small_grads = {n: [None] * depth for n in SMALL}
    pending = None
    for l in reversed(range(depth)):
        pr = params[l]
        x0, kept1, x1, qa, ka, va, qb, kb, vb, oa, lse_a, ob, lse_b, x2, kept2 = saved[l]
        g = {}
        dx, small_grads["ffn2_norm"][l], g["ffn2_w_gate"], g["ffn2_w_up"], g["ffn2_w_down"], side_out = _ffn_backward(
            x2, dx, kept2, pr["ffn2"], "ffn2", _side_sibling(pending[1]) if pending else None)
        if pending:
            sums = _rs_chip_sums(pending[1], side_out, dest_idx)
            owns, wires = sums[:len(side_out)], sums[len(side_out):]

        doa, dsum_a, dob, dsum_b, mixed, dyb, dga, dgb = _mix_out_bwd(
            dx, oa, ob, pr["ga"], pr["gb"], pr["wo_a"], pr["wo_b"])
        small_grads["mla_out_norm"][l] = dga.reshape(-1)
        small_grads["swa_out_norm"][l] = dgb.reshape(-1)
        g["w_o"] = _tn_matmul(mixed, dyb, "wgrad_wo")

        rows_of = lambda s: s[:, :, 0].reshape(MLA_HEADS, t // bq, 1, bq)
        (dqa, dka, dva), side_out = _mla_bwd(qa, ka, va, doa, rows_of(lse_a), rows_of(dsum_a),
                                             _side_chips(wires) if pending else None)
        if pending:
            grad_shards[pending[0]] = _shards_from_rows(_rs_sum(owns, side_out), big)
        dqb, dkb, dvb, dsinks = _swa_bwd(qb, kb, vb, pr["sinks"], dob, lse_b, dsum_b)
        small_grads["swa_sinks"][l] = dsinks[:, 0, 0]

        outs = _prep_bwd(x1, dx, cos, sin_s, pr["prep"], dqa, dka, dva, dqb, dkb, dvb)
        dx = outs[0]
        pg = dict(zip(PREP_WEIGHTS, outs[1:]))
        g["w_in"] = jnp.concatenate([pg["w_in"][:W_IN_COLS[0]], pg["w_in"][C_QS:]], axis=0)
        g["mla_w_q_b"] = pg["wqb"][:, :MLA_QK].reshape(MLA_HEADS * MLA_QK, MLA_Q_RANK)
        g["mla_w_kv_b"] = pg["w_kvb"]
        fold = lambda gg: gg[0, :HALF] + gg[0, HALF:]
        small_grads["mix_norm"][l] = pg["mix_g"][0]
        small_grads["mla_q_a_norm"][l] = pg["g_qa"][0]
        small_grads["mla_kv_a_norm"][l] = pg["g_kva"][0]
        small_grads["mla_q_norm"][l] = jnp.concatenate([pg["gq_n"][0], pg["gq_r"][0, :MLA_ROPE]])
        small_grads["mla_k_norm"][l] = jnp.concatenate([pg["gk_n"][0], pg["gk_r"][0, :MLA_ROPE]])
        small_grads["swa_q_norm"][l] = fold(pg["g_sq"])
        small_grads["swa_k_norm"][l] = fold(pg["g_sk"])

        dx, small_grads["ffn1_norm"][l], g["ffn1_w_gate"], g["ffn1_w_up"], g["ffn1_w_down"], _ = _ffn_backward(
            x0, dx, kept1, pr["ffn1"], "ffn1")
        pending = (l, _grads_by_destination(g, d))

    sibs = _rs_sibling_exchange(pending[1])
    sums = _rs_chip_sums(pending[1], sibs, dest_idx)
    owns, wires = sums[:len(sibs)], sums[len(sibs):]
    grad_shards[pending[0]] = _shards_from_rows(_rs_sum(owns, _rs_chip_exchange(wires)), big)

    grad_big, delta_big, new_m_big, new_v_big = {}, {}, {}, {}
    for n in BIG:
        g_st = jnp.stack([grad_shards[l][n] for l in range(depth)])
        d_st, m_st, v_st = _adamw(g_st, _stored(w[n], n), _stored(m[n], n), _stored(v[n], n), n)
        grad_big[n], delta_big[n], new_m_big[n], new_v_big[n] = (_stored(a, n) for a in (g_st, d_st, m_st, v_st))

    small_partial = _pack_small({n: jnp.stack(small_grads[n]) for n in SMALL})
    g_s = _all_reduce_small(small_partial)
    d_s, m_s, v_s = _adamw_small(g_s, _pack_small(w), _pack_small(m), _pack_small(v))
    grad_small, delta_small, new_m_small, new_v_small = (_unpack_small(a, w) for a in (g_s, d_s, m_s, v_s))

    def ordered(big, small):
        return [big[n] if n in big else small[n] for n in WEIGHTS]

    return (loss, dx.reshape(x.shape), *ordered(grad_big, grad_small), *ordered(delta_big, delta_small),
            *ordered(new_m_big, new_m_small), *ordered(new_v_big, new_v_small))
```

```python
import jax
import jax.numpy as jnp
from jax import lax
from jax.experimental import pallas as pl
from jax.experimental.pallas import tpu as pltpu

N_DEV = 8
EPS = 1e-6
ROPE_THETA = 10000.0
MLA_HEADS = 4
MLA_Q_RANK = 256
MLA_KV_RANK = 128
MLA_NOPE = 128
MLA_ROPE = 64
MLA_V = 128
MLA_QK = MLA_NOPE + MLA_ROPE
MLA_QK_PAD = 256
SWA_HEADS = 8
SWA_KV = 2
SWA_GROUP = SWA_HEADS // SWA_KV
SWA_D = 64
SWA_BLOCK = 128
ADAM_LR = 0.001
ADAM_B1 = 0.9
ADAM_B2 = 0.999
ADAM_EPS = 1e-08
ADAM_WD = 0.01
ADAM_STEP = 10

LANES = 128
HALF = LANES // 2
SWA_PAIRS = SWA_HEADS // 2
W_IN_COLS = (MLA_Q_RANK + MLA_KV_RANK + MLA_ROPE, SWA_HEADS * SWA_D + 2 * SWA_KV * SWA_D)
VMEM_LIMIT = 56 * 1024 * 1024

_MXU = jnp.bfloat16
_TB = 256
_TB_MIX = 512
_BQ = 512
_STRIP = 32
_TK = 512
_SWA_STEP = 4
RS_ROW_BLOCKS = 2

BIG = ("ffn1_w_gate", "ffn1_w_up", "ffn1_w_down", "w_in", "mla_w_q_b", "mla_w_kv_b", "w_o",
       "ffn2_w_gate", "ffn2_w_up", "ffn2_w_down")
ROW_SHARDED = ("ffn1_w_down", "w_o", "ffn2_w_down")
FFN1_BIG = ("ffn1_w_gate", "ffn1_w_up", "ffn1_w_down")
FFN2_BIG = ("ffn2_w_gate", "ffn2_w_up", "ffn2_w_down")
FFN_BIG = FFN1_BIG + FFN2_BIG
OTHER_BIG = ("w_o", "w_in", "mla_w_q_b", "mla_w_kv_b")
GATHER_ORDER = (FFN1_BIG, OTHER_BIG, FFN2_BIG)
SMALL = ("ffn1_norm", "mix_norm", "mla_q_a_norm", "mla_kv_a_norm", "mla_q_norm", "mla_k_norm",
         "swa_q_norm", "swa_k_norm", "swa_sinks", "mla_out_norm", "swa_out_norm", "ffn2_norm")
WEIGHTS = ("ffn1_norm", "ffn1_w_gate", "ffn1_w_up", "ffn1_w_down", "mix_norm", "w_in", "mla_q_a_norm",
           "mla_w_q_b", "mla_kv_a_norm", "mla_w_kv_b", "mla_q_norm", "mla_k_norm", "swa_q_norm",
           "swa_k_norm", "swa_sinks", "mla_out_norm", "swa_out_norm", "w_o", "ffn2_norm",
           "ffn2_w_gate", "ffn2_w_up", "ffn2_w_down")
MESH_AXES = ("x", "y", "c")
MESH = pl.DeviceIdType.MESH
NEG = -1e30
LOG2_E = 1.4426950408889634


def _f32(t):
    return t.astype(jnp.float32)


def _mx(t):
    return t.astype(_MXU)


def _dot(a, b):
    return jnp.dot(a, b, preferred_element_type=jnp.float32)


def _dot_nt(a, b):
    return lax.dot_general(a, b, (((1,), (1,)), ((), ())), preferred_element_type=jnp.float32)


def _dot_tn(a, b):
    return lax.dot_general(a, b, (((0,), (0,)), ((), ())), preferred_element_type=jnp.float32)


def _rsq(ss, n):
    return lax.rsqrt(ss * (1.0 / n) + EPS)


def _sumsq(t):
    return jnp.sum(t * t, axis=-1, keepdims=True)


def _rowsum(t):
    return jnp.sum(t, axis=-1, keepdims=True)


def _rowmax(t):
    return jnp.max(t, axis=-1, keepdims=True)


def _colsum(t):
    return jnp.sum(t, axis=0, keepdims=True)


def _lane():
    return lax.broadcasted_iota(jnp.int32, (1, LANES), 1)


def _low_half():
    return _lane() < HALF


def _swap32(t):
    return jnp.where((_lane() & 32) == 0, pltpu.roll(t, 96, 1), pltpu.roll(t, 32, 1))


def _rope(t, cos, sin_signed):
    return t * cos + _swap32(t) * sin_signed


def _rope_bwd(d, cos, sin_signed):
    return d * cos + _swap32(d * sin_signed)


def _half_sums(t):
    low = _low_half()
    return jnp.where(low, _rowsum(jnp.where(low, t, 0.0)), _rowsum(jnp.where(low, 0.0, t)))


def _dup_halves(pair):
    low = _low_half()
    swapped = pltpu.roll(pair, HALF, 1)
    return jnp.where(low, pair, swapped), jnp.where(low, swapped, pair)


def _undup_halves(d0, d1):
    return jnp.where(_low_half(), d0 + pltpu.roll(d0, HALF, 1), d1 + pltpu.roll(d1, HALF, 1))


def _pick_halves(a, b):
    return jnp.where(_low_half(), a, b)


def _norm_bwd(dn_list, xh_list, r, n):
    c = sum(_rowsum(dn * xh) for dn, xh in zip(dn_list, xh_list)) * (1.0 / n)
    return [r * (dn - xh * c) for dn, xh in zip(dn_list, xh_list)]


def _cparams(semantics):
    return pltpu.CompilerParams(dimension_semantics=semantics, vmem_limit_bytes=VMEM_LIMIT)


def _const(shape):
    nd = len(shape)
    return pl.BlockSpec(shape, lambda *_: (0,) * nd, pipeline_mode=pl.Buffered(1))


def _acc(shape):
    nd = len(shape)
    return pl.BlockSpec(shape, lambda *_: (0,) * nd)


def _rows(tb, width):
    return pl.BlockSpec((tb, width), lambda i: (i, 0))


def _heads_rows(h, tb, width):
    return pl.BlockSpec((h, tb, width), lambda i: (0, i, 0))


def _sds(shape, dtype):
    return jax.ShapeDtypeStruct(shape, dtype)


def _position():
    return lax.axis_index("x"), lax.axis_index("y"), lax.axis_index("c")


def _all_gather(xp):
    def body(x_ref, out_ref, send_sems, recv_sems, local_sem):
        x, y, c = _position()
        me, sibling = (x, y, c), (x, y, 1 - c)
        chips = [(1 - x, y), (x, 1 - y), (1 - x, 1 - y)]

        def rows(px, py, pc):
            return out_ref.at[4 * px + 2 * py + pc]

        def copy(k, block, to, src=None):
            return pltpu.make_async_remote_copy(
                src_ref=rows(*block) if src is None else src, dst_ref=rows(*block),
                send_sem=send_sems.at[k], recv_sem=recv_sems.at[k], device_id=to, device_id_type=MESH)

        mine = pltpu.make_async_copy(x_ref, rows(*me), local_sem)
        mine.start()
        first = [copy(0, me, sibling, src=x_ref)]
        first += [copy(1 + j, me, (*chip, c), src=x_ref) for j, chip in enumerate(chips)]
        for cp in first:
            cp.start()
        passed = [copy(4 + j, (*chip, c), sibling) for j, chip in enumerate(chips)]
        for j, chip in enumerate(chips):
            copy(1 + j, (*chip, c), me).wait_recv()
            passed[j].start()
        copy(0, sibling, me).wait_recv()
        for j, chip in enumerate(chips):
            copy(4 + j, (*chip, 1 - c), me).wait_recv()
        for cp in first + passed:
            cp.wait_send()
        mine.wait()

    return pl.pallas_call(
        body, name="ag_weights",
        out_shape=_sds((N_DEV,) + xp.shape, xp.dtype),
        in_specs=[pl.BlockSpec(memory_space=pl.ANY)],
        out_specs=pl.BlockSpec(memory_space=pl.ANY),
        scratch_shapes=[pltpu.SemaphoreType.DMA((7,)), pltpu.SemaphoreType.DMA((7,)), pltpu.SemaphoreType.DMA],
    )(xp)


def _relations(x, y):
    return [(x, y), (1 - x, y), (x, 1 - y), (1 - x, 1 - y)]


def _remote(src, dst, send_sem, recv_sem, device):
    return pltpu.make_async_remote_copy(src_ref=src, dst_ref=dst, send_sem=send_sem, recv_sem=recv_sem,
                                        device_id=device, device_id_type=MESH)


def _sibling_copies(g_refs, out_refs, send, recv):
    x, y, c = _position()
    n = len(g_refs)
    return [_remote(g.at[4 * px + 2 * py + (1 - c)], o.at[k], send.at[k * n + i], recv.at[k * n + i], (x, y, 1 - c))
            for k, (px, py) in enumerate(_relations(x, y)) for i, (g, o) in enumerate(zip(g_refs, out_refs))]


def _chip_copies(w_refs, out_refs, send, recv):
    x, y, c = _position()
    n = len(w_refs)
    return [_remote(w.at[k + 1], o.at[k], send.at[k * n + i], recv.at[k * n + i], (px, py, c))
            for k, (px, py) in enumerate(_relations(x, y)[1:]) for i, (w, o) in enumerate(zip(w_refs, out_refs))]


def _exchange(arrays, lead, relations, copies_fn, name):
    n = len(arrays)

    def body(*refs):
        copies = copies_fn(refs[:n], refs[n:2 * n], refs[2 * n], refs[2 * n + 1])
        for cp in copies:
            cp.start()
        for cp in copies:
            cp.wait()

    hbm = pl.BlockSpec(memory_space=pl.ANY)
    dma = pltpu.SemaphoreType.DMA
    return pl.pallas_call(
        body, name=name, out_shape=[_sds((lead,) + a.shape[1:], a.dtype) for a in arrays],
        in_specs=[hbm] * n, out_specs=[hbm] * n,
        scratch_shapes=[dma((relations * n,)), dma((relations * n,))],
    )(*arrays)


def _rs_sibling_exchange(gs):
    return _exchange(gs, 4, 4, _sibling_copies, "rs_sibling_exchange")


def _rs_chip_exchange(wires):
    return _exchange(wires, 3, 3, _chip_copies, "rs_chip_exchange")


def _side_exchange(arrays, lead, relations, copies_fn):
    shapes = [_sds((lead,) + a.shape[1:], a.dtype) for a in arrays]
    return list(arrays), shapes, relations * len(arrays), lambda ins, outs, send, recv, local: copies_fn(ins, outs, send, recv)


def _side_sibling(gs):
    return _side_exchange(gs, 4, 4, _sibling_copies)


def _side_chips(wires):
    return _side_exchange(wires, 3, 3, _chip_copies)


def _rs_chip_sums(gs, sibs, dest_idx):
    n = len(gs)

    def body(idx_ref, *refs):
        g_refs, s_refs, own_refs, wire_refs = refs[:n], refs[n:2 * n], refs[2 * n:3 * n], refs[3 * n:]
        totals = [g[0] + s[0] for g, s in zip(g_refs, s_refs)]
        for total, wire in zip(totals, wire_refs):
            wire[0] = total.astype(wire.dtype)

        @pl.when(pl.program_id(1) == 0)
        def _():
            for total, own in zip(totals, own_refs):
                own[...] = total

    def blocks(a, index_map, squeeze):
        rb = a.shape[1] // RS_ROW_BLOCKS
        return pl.BlockSpec((rb, a.shape[2]) if squeeze else (1, rb, a.shape[2]), index_map)

    return pl.pallas_call(
        body, name="rs_chip_sums",
        grid_spec=pltpu.PrefetchScalarGridSpec(
            num_scalar_prefetch=1, grid=(RS_ROW_BLOCKS, 4),
            in_specs=[blocks(g, lambda r, k, idx: (idx[k], r, 0), False) for g in gs]
            + [blocks(g, lambda r, k, idx: (k, r, 0), False) for g in gs],
            out_specs=[blocks(g, lambda r, k, idx: (r, 0), True) for g in gs]
            + [blocks(g, lambda r, k, idx: (k, r, 0), False) for g in gs]),
        out_shape=[_sds(g.shape[1:], jnp.float32) for g in gs] + [_sds((4,) + g.shape[1:], _MXU) for g in gs],
        compiler_params=_cparams(("parallel", "arbitrary")),
    )(dest_idx, *gs, *sibs)


def _side_gather(xp):
    def make(ins, outs, send, recv, local):
        (x_ref,), (out_ref,) = ins, outs
        x, y, c = _position()
        me = 4 * x + 2 * y + c
        copies = [pltpu.make_async_copy(x_ref, out_ref.at[me], local.at[0])]
        for k in range(1, N_DEV):
            peer = (1 - x if k & 4 else x, 1 - y if k & 2 else y, 1 - c if k & 1 else c)
            copies.append(_remote(x_ref, out_ref.at[me], send.at[k - 1], recv.at[k - 1], peer))
        return copies

    return [xp], [_sds((N_DEV,) + xp.shape, xp.dtype)], N_DEV - 1, make


def _call(body, args, sides, *, name, grid, in_specs, out_specs, out_shape, scratch_shapes=(), semantics):
    in_specs, out_specs, out_shape = list(in_specs), list(out_specs), list(out_shape)
    n_in, n_out, n_scr = len(in_specs), len(out_specs), len(scratch_shapes)
    sides = list(sides or [])
    if not sides:
        outs = pl.pallas_call(body, name=name, grid=grid, in_specs=in_specs, out_specs=out_specs, out_shape=out_shape,
                              scratch_shapes=list(scratch_shapes), compiler_params=_cparams(semantics))(*args)
        return list(outs), []
    arrays = [a for side in sides for a in side[0]]
    shapes = [s for side in sides for s in side[1]]
    n_side_in, n_side_out = len(arrays), len(shapes)
    hbm = pl.BlockSpec(memory_space=pl.ANY)

    def with_copies(*refs):
        main_in, refs = refs[:n_in], refs[n_in:]
        side_in, refs = refs[:n_side_in], refs[n_side_in:]
        main_out, refs = refs[:n_out], refs[n_out:]
        side_out, refs = refs[:n_side_out], refs[n_side_out:]
        main_scr, sems = refs[:n_scr], refs[n_scr:]
        copies = []
        for k, (side_arrays, side_shapes, _, make) in enumerate(sides):
            copies += make(side_in[:len(side_arrays)], side_out[:len(side_shapes)], *sems[3 * k:3 * k + 3])
            side_in, side_out = side_in[len(side_arrays):], side_out[len(side_shapes):]
        ids = [pl.program_id(a) for a in range(len(grid))]
        first, last = ids[0] == 0, ids[0] == grid[0] - 1
        for i, size in zip(ids[1:], grid[1:]):
            first, last = jnp.logical_and(first, i == 0), jnp.logical_and(last, i == size - 1)

        @pl.when(first)
        def _():
            for cp in copies:
                cp.start()

        body(*main_in, *main_out, *main_scr)

        @pl.when(last)
        def _():
            for cp in copies:
                cp.wait()

    dma = pltpu.SemaphoreType.DMA
    sem_shapes = [dma((n,)) for side in sides for n in (side[2], side[2], 1)]
    outs = pl.pallas_call(
        with_copies, name=name, grid=grid, in_specs=in_specs + [hbm] * n_side_in,
        out_specs=out_specs + [hbm] * n_side_out, out_shape=out_shape + shapes,
        scratch_shapes=list(scratch_shapes) + sem_shapes,
        compiler_params=_cparams(("arbitrary",) * len(grid)),
    )(*args, *arrays)
    side_outs, rest = [], list(outs[n_out:])
    for side in sides:
        side_outs.append(rest[:len(side[1])])
        rest = rest[len(side[1]):]
    return list(outs[:n_out]), side_outs


def _all_reduce_small(v):
    rows_n = v.shape[0]

    def body(v_ref, out_ref, buf, send_sems, recv_sems):
        x, y, c = _position()
        me = 4 * x + 2 * y + c
        buf[me] = v_ref[...]
        copies = []
        for k in range(1, N_DEV):
            px = 1 - x if k & 4 else x
            py = 1 - y if k & 2 else y
            pc = 1 - c if k & 1 else c
            copies.append(pltpu.make_async_remote_copy(
                src_ref=v_ref, dst_ref=buf.at[me],
                send_sem=send_sems.at[k - 1], recv_sem=recv_sems.at[k - 1], device_id=(px, py, pc), device_id_type=MESH))
        for cp in copies:
            cp.start()
        for cp in copies:
            cp.wait()
        total = buf[0]
        for d in range(1, N_DEV):
            total = total + buf[d]
        out_ref[...] = total

    return pl.pallas_call(
        body, name="ar_small",
        out_shape=_sds((rows_n, LANES), jnp.float32),
        in_specs=[pl.BlockSpec(memory_space=pltpu.VMEM)],
        out_specs=pl.BlockSpec(memory_space=pltpu.VMEM),
        scratch_shapes=[pltpu.VMEM((N_DEV, rows_n, LANES), jnp.float32),
                        pltpu.SemaphoreType.DMA((N_DEV - 1,)), pltpu.SemaphoreType.DMA((N_DEV - 1,))],
    )(v)


def _adamw_math(w, g, m, v):
    m = ADAM_B1 * m + (1.0 - ADAM_B1) * g
    v = ADAM_B2 * v + (1.0 - ADAM_B2) * (g * g)
    m_hat = m / (1.0 - ADAM_B1 ** ADAM_STEP)
    v_hat = v / (1.0 - ADAM_B2 ** ADAM_STEP)
    delta = -ADAM_LR * (m_hat / (jnp.sqrt(v_hat) + ADAM_EPS) + ADAM_WD * w)
    return delta, m, v


def _rs_sum(owns, recvs):
    n = len(owns)

    def body(*refs):
        own_refs, recv_refs, out_refs = refs[:n], refs[n:4 * n], refs[4 * n:]
        for i in range(n):
            r0, r1, r2 = recv_refs[3 * i:3 * i + 3]
            out_refs[i][...] = ((own_refs[i][...] + _f32(r0[0])) + _f32(r1[0])) + _f32(r2[0])

    def row(a):
        return pl.BlockSpec((a.shape[0] // RS_ROW_BLOCKS, a.shape[1]), lambda r: (r, 0))

    def slot(a, k):
        return pl.BlockSpec((1, a.shape[0] // RS_ROW_BLOCKS, a.shape[1]), lambda r: (k, r, 0))

    return pl.pallas_call(
        body, name="rs_sum", grid=(RS_ROW_BLOCKS,),
        in_specs=[row(a) for a in owns] + [slot(a, k) for a in owns for k in range(3)],
        out_specs=[row(a) for a in owns],
        out_shape=[_sds(a.shape, jnp.float32) for a in owns],
        compiler_params=_cparams(("parallel",)),
    )(*owns, *[r for r in recvs for _ in range(3)])


def _adamw(g, w, m, v, name):
    depth, a, b = w.shape

    def body(g_ref, w_ref, m_ref, v_ref, d_out, m_out, v_out):
        delta, m2, v2 = _adamw_math(w_ref[...], g_ref[...], m_ref[...], v_ref[...])
        d_out[...] = delta
        m_out[...] = m2
        v_out[...] = v2

    layer = pl.BlockSpec((1, a, b), lambda l: (l, 0, 0))
    return pl.pallas_call(
        body, name="adamw_" + name, grid=(depth,),
        in_specs=[layer] * 4, out_specs=[layer] * 3,
        out_shape=[_sds(w.shape, jnp.float32)] * 3,
        compiler_params=_cparams(("parallel",)),
    )(g, w, m, v)


def _adamw_small(g, w, m, v):
    def body(g_ref, w_ref, m_ref, v_ref, d_out, m_out, v_out):
        delta, m2, v2 = _adamw_math(w_ref[...], g_ref[...], m_ref[...], v_ref[...])
        d_out[...] = delta
        m_out[...] = m2
        v_out[...] = v2

    vm = pl.BlockSpec(memory_space=pltpu.VMEM)
    return pl.pallas_call(
        body, name="adamw_small",
        in_specs=[vm] * 4, out_specs=[vm] * 3,
        out_shape=[_sds(g.shape, jnp.float32)] * 3,
    )(g, w, m, v)


def _f_chunk(f):
    for cand in (1408, 1024, 512, 256, 128):
        if f % cand == 0:
            return cand
    return f


def _ffn_fwd(x, gain, wg, wu, wd, sides=()):
    t, d = x.shape
    f = wg.shape[0]
    tb = min(_TB, t)
    fc = _f_chunk(f)

    def body(x_ref, g_ref, wg_ref, wu_ref, wd_ref, xo_ref, h_ref, s_ref, fa_ref, fu_ref):
        xv = x_ref[...]
        hb = _mx(xv * _rsq(_sumsq(xv), d) * g_ref[...])
        h_ref[...] = hb
        y = jnp.zeros((tb, d), jnp.float32)
        for c0 in range(0, f, fc):
            a = _dot_nt(hb, wg_ref[c0:c0 + fc, :])
            u = _dot_nt(hb, wu_ref[c0:c0 + fc, :])
            sig = jax.nn.sigmoid(a)
            silu = a * sig
            s = _mx(silu * u)
            s_ref[:, c0:c0 + fc] = s
            fa_ref[:, c0:c0 + fc] = _mx(u * (sig * (1.0 + a * (1.0 - sig))))
            fu_ref[:, c0:c0 + fc] = _mx(silu)
            y = y + _dot(s, wd_ref[c0:c0 + fc, :])
        xo_ref[...] = xv + 0.5 * y

    return _call(
        body, (x, gain, wg, wu, wd), sides, name="ffn_fwd", grid=(t // tb,),
        in_specs=[_rows(tb, d), _const((1, d)), _const((f, d)), _const((f, d)), _const((f, d))],
        out_specs=[_rows(tb, d), _rows(tb, d), _rows(tb, f), _rows(tb, f), _rows(tb, f)],
        out_shape=[_sds((t, d), jnp.float32), _sds((t, d), _MXU), _sds((t, f), _MXU), _sds((t, f), _MXU),
                   _sds((t, f), _MXU)],
        semantics=("parallel",))


def _ffn_dgrad(x, gain, dxo, fa, fu, wg, wu, wd, sides=()):
    t, d = x.shape
    f = wg.shape[0]
    tb = min(_TB, t)
    fc = _f_chunk(f)

    def body(x_ref, g_ref, dxo_ref, fa_ref, fu_ref, wg_ref, wu_ref, wd_ref, dxi_ref, da_ref, du_ref, dy_ref, dg_ref):
        xv = x_ref[...]
        gv = g_ref[...]
        r = _rsq(_sumsq(xv), d)
        xhat = xv * r
        dxo = dxo_ref[...]
        dyb = _mx(0.5 * dxo)
        dy_ref[...] = dyb
        dh = jnp.zeros((tb, d), jnp.float32)
        for c0 in range(0, f, fc):
            ds = _dot_nt(dyb, wd_ref[c0:c0 + fc, :])
            da = _mx(ds * _f32(fa_ref[:, c0:c0 + fc]))
            du = _mx(ds * _f32(fu_ref[:, c0:c0 + fc]))
            da_ref[:, c0:c0 + fc] = da
            du_ref[:, c0:c0 + fc] = du
            dh = dh + _dot(da, wg_ref[c0:c0 + fc, :]) + _dot(du, wu_ref[c0:c0 + fc, :])

        @pl.when(pl.program_id(0) == 0)
        def _():
            dg_ref[...] = jnp.zeros_like(dg_ref)

        dg_ref[...] += _colsum(dh * xhat)
        dn = dh * gv
        dxi_ref[...] = dxo + r * (dn - xhat * (_rowsum(dn * xhat) * (1.0 / d)))

    return _call(
        body, (x, gain, dxo, fa, fu, wg, wu, wd), sides, name="ffn_dgrad", grid=(t // tb,),
        in_specs=[_rows(tb, d), _const((1, d)), _rows(tb, d), _rows(tb, f), _rows(tb, f),
                  _const((f, d)), _const((f, d)), _const((f, d))],
        out_specs=[_rows(tb, d), _rows(tb, f), _rows(tb, f), _rows(tb, d), _acc((1, d))],
        out_shape=[_sds((t, d), jnp.float32), _sds((t, f), _MXU), _sds((t, f), _MXU), _sds((t, d), _MXU),
                   _sds((1, d), jnp.float32)],
        semantics=("arbitrary",))


def _tn_matmul(a, b, name):
    t, m = a.shape
    n = b.shape[1]
    tk = min(_TK, t)
    tn = n
    while m * tn * 4 > 12 * 1024 * 1024 and tn % 256 == 0:
        tn //= 2

    def body(a_ref, b_ref, o_ref):
        @pl.when(pl.program_id(1) == 0)
        def _():
            o_ref[...] = jnp.zeros_like(o_ref)

        o_ref[...] += _dot_tn(a_ref[...], b_ref[...])

    return pl.pallas_call(
        body, name=name, grid=(n // tn, t // tk),
        in_specs=[pl.BlockSpec((tk, m), lambda j, k: (k, 0)), pl.BlockSpec((tk, tn), lambda j, k: (k, j))],
        out_specs=pl.BlockSpec((m, tn), lambda j, k: (0, j)),
        out_shape=_sds((m, n), jnp.float32),
        compiler_params=_cparams(("parallel", "arbitrary")),
    )(a, b)


PREP_WEIGHTS = ("mix_g", "w_in", "g_qa", "wqb", "g_kva", "w_kvb", "gq_n", "gq_r", "gk_n", "gk_r", "g_sq", "g_sk")
C_CQ, C_CKV, C_KPE, C_QS = 0, MLA_Q_RANK, MLA_Q_RANK + MLA_KV_RANK, MLA_Q_RANK + MLA_KV_RANK + LANES
C_KS = C_QS + SWA_HEADS * SWA_D
C_VS = C_KS + LANES
W_IN_PACKED = C_VS + LANES


def _prep_specs(p):
    return [_const(p[n].shape) for n in PREP_WEIGHTS]


def _pair_norm_rope(t, gain, cos, sin_s):
    return _rope(t * _rsq(_half_sums(t * t), SWA_D) * gain, cos, sin_s)


def _prep_fwd(x, cos, sin_s, p):
    t, d = x.shape
    tb = min(_TB_MIX, t)

    def body(x_ref, cos_ref, sin_ref, mix_g, w_in, g_qa, wqb, g_kva, w_kvb, gq_n, gq_r, gk_n, gk_r, g_sq, g_sk,
             qa_ref, ka_ref, va_ref, qb_ref, kb_ref, vb_ref):
        xv = x_ref[...]
        cos_v, sin_v = cos_ref[...], sin_ref[...]
        hb = _mx(xv * _rsq(_sumsq(xv), d) * mix_g[...])
        proj = _dot_nt(hb, w_in[...])
        cq = proj[:, C_CQ:C_CKV]
        cqn = _mx(cq * _rsq(_sumsq(cq), MLA_Q_RANK) * g_qa[...])
        for h in range(MLA_HEADS):
            qh = _dot_nt(cqn, wqb[h])
            qn, qr = qh[:, :MLA_NOPE], qh[:, MLA_NOPE:]
            rh = _rsq(_sumsq(qn) + _sumsq(qr), MLA_QK)
            qa_ref[h, :, 0:MLA_NOPE] = (qn * rh * gq_n[...]).astype(qa_ref.dtype)
            qa_ref[h, :, MLA_NOPE:MLA_QK_PAD] = _rope(qr * rh * gq_r[...], cos_v, sin_v).astype(qa_ref.dtype)
        ckv = proj[:, C_CKV:C_KPE]
        ckvn = _mx(ckv * _rsq(_sumsq(ckv), MLA_KV_RANK) * g_kva[...])
        kpe = proj[:, C_KPE:C_QS]
        ss_pe = _sumsq(kpe)
        kv = _dot_nt(ckvn, w_kvb[...])
        for h in range(MLA_HEADS):
            c0 = h * (MLA_NOPE + MLA_V)
            kn = kv[:, c0:c0 + MLA_NOPE]
            rh = _rsq(_sumsq(kn) + ss_pe, MLA_QK)
            ka_ref[h, :, 0:MLA_NOPE] = (kn * rh * gk_n[...]).astype(ka_ref.dtype)
            ka_ref[h, :, MLA_NOPE:MLA_QK_PAD] = _rope(kpe * rh * gk_r[...], cos_v, sin_v).astype(ka_ref.dtype)
            va_ref[h] = kv[:, c0 + MLA_NOPE:c0 + MLA_NOPE + MLA_V].astype(va_ref.dtype)
        for j in range(SWA_PAIRS):
            c0 = C_QS + j * LANES
            qb_ref[j] = _pair_norm_rope(proj[:, c0:c0 + LANES], g_sq[...], cos_v, sin_v).astype(qb_ref.dtype)
        k0, k1 = _dup_halves(_pair_norm_rope(proj[:, C_KS:C_VS], g_sk[...], cos_v, sin_v))
        kb_ref[0] = k0.astype(kb_ref.dtype)
        kb_ref[1] = k1.astype(kb_ref.dtype)
        v0, v1 = _dup_halves(proj[:, C_VS:W_IN_PACKED])
        vb_ref[0] = v0.astype(vb_ref.dtype)
        vb_ref[1] = v1.astype(vb_ref.dtype)

    return pl.pallas_call(
        body, name="prep_fwd", grid=(t // tb,),
        in_specs=[_rows(tb, d), _rows(tb, LANES), _rows(tb, LANES)] + _prep_specs(p),
        out_specs=[_heads_rows(MLA_HEADS, tb, MLA_QK_PAD), _heads_rows(MLA_HEADS, tb, MLA_QK_PAD),
                   _heads_rows(MLA_HEADS, tb, MLA_V), _heads_rows(SWA_PAIRS, tb, LANES),
                   _heads_rows(SWA_KV, tb, LANES), _heads_rows(SWA_KV, tb, LANES)],
        out_shape=[_sds((MLA_HEADS, t, MLA_QK_PAD), _MXU), _sds((MLA_HEADS, t, MLA_QK_PAD), _MXU),
                   _sds((MLA_HEADS, t, MLA_V), _MXU), _sds((SWA_PAIRS, t, LANES), _MXU),
                   _sds((SWA_KV, t, LANES), _MXU), _sds((SWA_KV, t, LANES), _MXU)],
        compiler_params=_cparams(("parallel",)),
    )(x, cos, sin_s, *[p[n] for n in PREP_WEIGHTS])


def _prep_bwd(x, dxin, cos, sin_s, p, dqa, dka, dva, dqb, dkb, dvb, sides=()):
    t, d = x.shape
    tb = min(_TB_MIX, t)
    n_w = len(PREP_WEIGHTS)

    def body(*refs):
        x_ref, dxin_ref, cos_ref, sin_ref = refs[:4]
        mix_g, w_in, g_qa, wqb, g_kva, w_kvb, gq_n, gq_r, gk_n, gk_r, g_sq, g_sk = refs[4:4 + n_w]
        dqa_ref, dka_ref, dva_ref, dqb_ref, dkb_ref, dvb_ref = refs[4 + n_w:10 + n_w]
        dx_ref = refs[10 + n_w]
        grads = dict(zip(PREP_WEIGHTS, refs[11 + n_w:11 + 2 * n_w]))
        dproj_ref, dkv_ref, dqh_ref = refs[11 + 2 * n_w:]

        @pl.when(pl.program_id(0) == 0)
        def _():
            for ref in grads.values():
                ref[...] = jnp.zeros_like(ref)

        xv = x_ref[...]
        cos_v, sin_v = cos_ref[...], sin_ref[...]
        r0 = _rsq(_sumsq(xv), d)
        xhat = xv * r0
        hb = _mx(xhat * mix_g[...])
        proj = _dot_nt(hb, w_in[...])

        cq = proj[:, C_CQ:C_CKV]
        rq = _rsq(_sumsq(cq), MLA_Q_RANK)
        cqh = cq * rq
        cqn = _mx(cqh * g_qa[...])
        dcqn = jnp.zeros((tb, MLA_Q_RANK), jnp.float32)
        for h in range(MLA_HEADS):
            qh = _dot_nt(cqn, wqb[h])
            qn, qr = qh[:, :MLA_NOPE], qh[:, MLA_NOPE:]
            rh = _rsq(_sumsq(qn) + _sumsq(qr), MLA_QK)
            xh_n, xh_r = qn * rh, qr * rh
            dy_n = dqa_ref[h, :, 0:MLA_NOPE]
            dy_r = _rope_bwd(dqa_ref[h, :, MLA_NOPE:MLA_QK_PAD], cos_v, sin_v)
            grads["gq_n"][...] += _colsum(dy_n * xh_n)
            grads["gq_r"][...] += _colsum(dy_r * xh_r)
            dqn, dqr = _norm_bwd([dy_n * gq_n[...], dy_r * gq_r[...]], [xh_n, xh_r], rh, MLA_QK)
            dqh_ref[:, 0:MLA_NOPE] = _mx(dqn)
            dqh_ref[:, MLA_NOPE:MLA_QK_PAD] = _mx(dqr)
            dqh = dqh_ref[...]
            grads["wqb"][h] += _dot_tn(dqh, cqn)
            dcqn = dcqn + _dot(dqh, wqb[h])
        grads["g_qa"][...] += _colsum(dcqn * cqh)
        (dcq,) = _norm_bwd([dcqn * g_qa[...]], [cqh], rq, MLA_Q_RANK)
        dproj_ref[:, C_CQ:C_CKV] = _mx(dcq)

        ckv = proj[:, C_CKV:C_KPE]
        rkv = _rsq(_sumsq(ckv), MLA_KV_RANK)
        ckvh = ckv * rkv
        ckvn = _mx(ckvh * g_kva[...])
        kpe = proj[:, C_KPE:C_QS]
        ss_pe = _sumsq(kpe)
        kv = _dot_nt(ckvn, w_kvb[...])
        dkpe = jnp.zeros((tb, LANES), jnp.float32)
        for h in range(MLA_HEADS):
            c0 = h * (MLA_NOPE + MLA_V)
            c1 = c0 + MLA_NOPE
            kn = kv[:, c0:c1]
            rh = _rsq(_sumsq(kn) + ss_pe, MLA_QK)
            xh_n, xh_r = kn * rh, kpe * rh
            dy_n = dka_ref[h, :, 0:MLA_NOPE]
            dy_r = _rope_bwd(dka_ref[h, :, MLA_NOPE:MLA_QK_PAD], cos_v, sin_v)
            grads["gk_n"][...] += _colsum(dy_n * xh_n)
            grads["gk_r"][...] += _colsum(dy_r * xh_r)
            dkn, dkr = _norm_bwd([dy_n * gk_n[...], dy_r * gk_r[...]], [xh_n, xh_r], rh, MLA_QK)
            dkpe = dkpe + dkr
            dkv_ref[:, c0:c1] = _mx(dkn)
            dkv_ref[:, c1:c1 + MLA_V] = _mx(dva_ref[h])
        dkv = dkv_ref[...]
        grads["w_kvb"][...] += _dot_tn(dkv, ckvn)
        dckvn = _dot(dkv, w_kvb[...])
        grads["g_kva"][...] += _colsum(dckvn * ckvh)
        (dckv,) = _norm_bwd([dckvn * g_kva[...]], [ckvh], rkv, MLA_KV_RANK)
        dproj_ref[:, C_CKV:C_KPE] = _mx(dckv)
        dproj_ref[:, C_KPE:C_QS] = _mx(dkpe)

        def pair_bwd(tv, dy, g_ref, gname):
            r = _rsq(_half_sums(tv * tv), SWA_D)
            xh = tv * r
            dpre = _rope_bwd(dy, cos_v, sin_v)
            grads[gname][...] += _colsum(dpre * xh)
            dn = dpre * g_ref[...]
            return r * (dn - xh * (_half_sums(dn * xh) * (1.0 / SWA_D)))

        for j in range(SWA_PAIRS):
            c0 = C_QS + j * LANES
            dproj_ref[:, c0:c0 + LANES] = _mx(pair_bwd(proj[:, c0:c0 + LANES], dqb_ref[j], g_sq, "g_sq"))
        dproj_ref[:, C_KS:C_VS] = _mx(pair_bwd(proj[:, C_KS:C_VS], _undup_halves(dkb_ref[0], dkb_ref[1]), g_sk, "g_sk"))
        dproj_ref[:, C_VS:W_IN_PACKED] = _mx(_undup_halves(dvb_ref[0], dvb_ref[1]))

        dproj = dproj_ref[...]
        grads["w_in"][...] += _dot_tn(dproj, hb)
        dh = _dot(dproj, w_in[...])
        grads["mix_g"][...] += _colsum(dh * xhat)
        (dxv,) = _norm_bwd([dh * mix_g[...]], [xhat], r0, d)
        dx_ref[...] = dxin_ref[...] + dxv

    grad_shapes = [p[n].shape for n in PREP_WEIGHTS]
    args = (x, dxin, cos, sin_s, *[p[n] for n in PREP_WEIGHTS], dqa, dka, dva, dqb, dkb, dvb)
    return _call(
        body, args, sides, name="prep_bwd", grid=(t // tb,),
        in_specs=[_rows(tb, d), _rows(tb, d), _rows(tb, LANES), _rows(tb, LANES)] + _prep_specs(p) + [
            _heads_rows(MLA_HEADS, tb, MLA_QK_PAD), _heads_rows(MLA_HEADS, tb, MLA_QK_PAD),
            _heads_rows(MLA_HEADS, tb, MLA_V), _heads_rows(SWA_PAIRS, tb, LANES),
            _heads_rows(SWA_KV, tb, LANES), _heads_rows(SWA_KV, tb, LANES)],
        out_specs=[_rows(tb, d)] + [_acc(s) for s in grad_shapes],
        out_shape=[_sds((t, d), jnp.float32)] + [_sds(s, jnp.float32) for s in grad_shapes],
        scratch_shapes=[pltpu.VMEM((tb, W_IN_PACKED), _MXU), pltpu.VMEM((tb, MLA_HEADS * (MLA_NOPE + MLA_V)), _MXU),
                        pltpu.VMEM((tb, MLA_QK_PAD), _MXU)],
        semantics=("arbitrary",))


def _strips(n):
    step = min(_STRIP, n)
    return [slice(r, r + step) for r in range(0, n, step)]


def _mla_fwd(q, k, v, sides=()):
    hn, t, dq = q.shape
    dv = v.shape[2]
    bq = min(_BQ, t)
    scale = MLA_QK ** -0.5
    scale2 = scale * LOG2_E

    def body(q_ref, k_ref, v_ref, o_ref, l_ref):
        i = pl.program_id(1)
        qv = q_ref[0]

        def step(first_block, width, carry, masked):
            m, l, acc = carry
            start = pl.multiple_of(first_block * bq, bq)
            s = _dot_nt(qv, k_ref[0, pl.ds(start, width), :])
            if masked:
                row = lax.broadcasted_iota(jnp.int32, (bq, width), 0)
                col = lax.broadcasted_iota(jnp.int32, (bq, width), 1)
                s = jnp.where(col <= row, s, NEG)
            m_new = jnp.maximum(m, _rowmax(s))
            alpha = jnp.exp2((m - m_new) * scale2)
            pv = jnp.exp2((s - m_new) * scale2)
            l = alpha * l + _rowsum(pv)
            acc = alpha * acc + _dot(_mx(pv), v_ref[0, pl.ds(start, width), :])
            return m_new, l, acc

        init = (jnp.full((bq, 1), NEG, jnp.float32), jnp.zeros((bq, 1), jnp.float32), jnp.zeros((bq, dv), jnp.float32))
        carry, done = init, 0
        for group in (4, 2, 1):
            count = (i - done) // group
            carry = lax.fori_loop(0, count, lambda g, c, done=done, group=group: step(done + group * g, group * bq, c, False), carry)
            done = done + group * count
        m, l, acc = step(i, bq, carry, True)
        o_ref[0] = acc / l
        l_ref[0] = jnp.broadcast_to(m * scale + jnp.log(l), (bq, LANES))

    return _call(
        body, (q, k, v), sides, name="mla_fwd", grid=(hn, t // bq),
        in_specs=[pl.BlockSpec((1, bq, dq), lambda h, i: (h, i, 0)),
                  pl.BlockSpec((1, t, dq), lambda h, i: (h, 0, 0)),
                  pl.BlockSpec((1, t, dv), lambda h, i: (h, 0, 0))],
        out_specs=[pl.BlockSpec((1, bq, dv), lambda h, i: (h, i, 0)),
                   pl.BlockSpec((1, bq, LANES), lambda h, i: (h, i, 0))],
        out_shape=[_sds((hn, t, dv), jnp.float32), _sds((hn, t, LANES), jnp.float32)],
        semantics=("parallel", "arbitrary"))


def _mla_bwd(q, k, v, do, lse_rows, dsum_rows, sides=()):
    hn, t, dq_w = q.shape
    dv_w = v.shape[2]
    bq = min(_BQ, t)
    nb = t // bq
    scale = MLA_QK ** -0.5

    def body(q_ref, do_ref, l_ref, d_ref, k_ref, v_ref, dq_ref, dk_ref, dv_ref, st_scr, dpt_scr, p_scr, ds_scr):
        j = pl.program_id(1)

        @pl.when(j == 0)
        def _():
            dq_ref[...] = jnp.zeros_like(dq_ref)

        kv = k_ref[0]
        vv = v_ref[0]
        dk_ref[0] = jnp.zeros((bq, dq_w), jnp.float32)
        dv_ref[0] = jnp.zeros((bq, dv_w), jnp.float32)

        def tile(i, masked):
            start = pl.multiple_of(i * bq, bq)
            qv = q_ref[0, pl.ds(start, bq), :]
            dov = do_ref[0, pl.ds(start, bq), :]
            st_scr[...] = _dot_nt(kv, qv)
            dpt_scr[...] = _dot_nt(vv, dov)
            lse2 = l_ref[0, i] * LOG2_E
            dsum = d_ref[0, i]
            for rows in _strips(bq):
                pt = jnp.exp2(st_scr[rows, :] * (scale * LOG2_E) - lse2)
                if masked:
                    n_rows = rows.stop - rows.start
                    row = lax.broadcasted_iota(jnp.int32, (n_rows, bq), 0) + rows.start
                    col = lax.broadcasted_iota(jnp.int32, (n_rows, bq), 1)
                    pt = jnp.where(row <= col, pt, 0.0)
                p_scr[rows, :] = _mx(pt)
                ds_scr[rows, :] = _mx(pt * (dpt_scr[rows, :] - dsum) * scale)
            ds_t = ds_scr[...]
            dv_ref[0] += _dot(p_scr[...], dov)
            dk_ref[0] += _dot(ds_t, qv)
            dq_ref[0, pl.ds(start, bq), :] += _dot_tn(ds_t, kv)

        def loop_body(i, carry):
            tile(i, False)
            return carry

        tile(j, True)
        lax.fori_loop(j + 1, nb, loop_body, 0)

    return _call(
        body, (q, do, lse_rows, dsum_rows, k, v), sides, name="mla_bwd", grid=(hn, nb),
        in_specs=[pl.BlockSpec((1, t, dq_w), lambda h, j: (h, 0, 0)),
                  pl.BlockSpec((1, t, dv_w), lambda h, j: (h, 0, 0)),
                  pl.BlockSpec((1, nb, 1, bq), lambda h, j: (h, 0, 0, 0)),
                  pl.BlockSpec((1, nb, 1, bq), lambda h, j: (h, 0, 0, 0)),
                  pl.BlockSpec((1, bq, dq_w), lambda h, j: (h, j, 0)),
                  pl.BlockSpec((1, bq, dv_w), lambda h, j: (h, j, 0))],
        out_specs=[pl.BlockSpec((1, t, dq_w), lambda h, j: (h, 0, 0)),
                   pl.BlockSpec((1, bq, dq_w), lambda h, j: (h, j, 0)),
                   pl.BlockSpec((1, bq, dv_w), lambda h, j: (h, j, 0))],
        out_shape=[_sds((hn, t, dq_w), jnp.float32), _sds((hn, t, dq_w), jnp.float32), _sds((hn, t, dv_w), jnp.float32)],
        scratch_shapes=[pltpu.VMEM((bq, bq), jnp.float32), pltpu.VMEM((bq, bq), jnp.float32),
                        pltpu.VMEM((bq, bq), _MXU), pltpu.VMEM((bq, bq), _MXU)],
        semantics=("parallel", "arbitrary"))


STACK = SWA_GROUP * SWA_BLOCK


def _swa_stack(ref, c, rows):
    low = _low_half()
    parts = []
    for g in range(SWA_GROUP):
        tv = ref[SWA_GROUP // 2 * c + g // 2, rows, :]
        keep = low if g % 2 == 0 else jnp.logical_not(low)
        parts.append(_mx(jnp.where(keep, tv, jnp.zeros_like(tv))))
    return jnp.concatenate(parts, axis=0)


def _swa_cols(ref, c, rows):
    return jnp.concatenate([ref[SWA_GROUP * c + g, rows, 0:1] for g in range(SWA_GROUP)], axis=0)


def _swa_sink_col(s_ref, c):
    return jnp.concatenate([jnp.broadcast_to(s_ref[SWA_GROUP * c + g][:, 0:1], (SWA_BLOCK, 1))
                            for g in range(SWA_GROUP)], axis=0)


def _swa_band_masks():
    row = lax.broadcasted_iota(jnp.int32, (STACK, SWA_BLOCK), 0) & (SWA_BLOCK - 1)
    col = lax.broadcasted_iota(jnp.int32, (STACK, SWA_BLOCK), 1)
    return col <= row, col > row


def _swa_unstack_pairs(ref, c, rows, stacked):
    for pr in range(SWA_GROUP // 2):
        r0 = 2 * pr * SWA_BLOCK
        ref[SWA_GROUP // 2 * c + pr, rows, :] = _pick_halves(stacked[r0:r0 + SWA_BLOCK], stacked[r0 + SWA_BLOCK:r0 + 2 * SWA_BLOCK])


def _swa_blocks(t):
    nblk = t // SWA_BLOCK
    bps = min(_SWA_STEP, nblk)
    return nblk, bps, bps * SWA_BLOCK


def _swa_fwd(q, k, v, sinks):
    _, t, _ = q.shape
    nblk, bps, sb = _swa_blocks(t)
    scale = SWA_D ** -0.5

    def body(q_ref, k_ref, kp_ref, v_ref, vp_ref, s_ref, o_ref, l_ref):
        n = pl.program_id(0)
        m_cur, m_prev = _swa_band_masks()
        for c in range(SWA_KV):
            sink = _swa_sink_col(s_ref, c)
            for b in range(bps):
                rows = slice(b * SWA_BLOCK, (b + 1) * SWA_BLOCK)
                kc, vc = k_ref[c, rows, :], v_ref[c, rows, :]
                if b == 0:
                    kp, vp, mp = kp_ref[c], vp_ref[c], jnp.logical_and(m_prev, n > 0)
                else:
                    before = slice((b - 1) * SWA_BLOCK, b * SWA_BLOCK)
                    kp, vp, mp = k_ref[c, before, :], v_ref[c, before, :], m_prev
                qs = _swa_stack(q_ref, c, rows)
                s_c = jnp.where(m_cur, _dot_nt(qs, kc) * scale, NEG)
                s_p = jnp.where(mp, _dot_nt(qs, kp) * scale, NEG)
                m = jnp.maximum(jnp.maximum(_rowmax(s_c), _rowmax(s_p)), sink)
                e_c = jnp.exp(s_c - m)
                e_p = jnp.exp(s_p - m)
                denom = _rowsum(e_c) + _rowsum(e_p) + jnp.exp(sink - m)
                inv = 1.0 / denom
                o = _dot(_mx(e_c * inv), vc) + _dot(_mx(e_p * inv), vp)
                lse = m + jnp.log(denom)
                for g in range(SWA_GROUP):
                    l_ref[SWA_GROUP * c + g, rows, :] = jnp.broadcast_to(
                        lse[g * SWA_BLOCK:(g + 1) * SWA_BLOCK], (SWA_BLOCK, LANES))
                _swa_unstack_pairs(o_ref, c, rows, o)

    main = lambda n: (0, n, 0)
    prev = lambda n: (0, jnp.maximum(n * bps - 1, 0), 0)
    return pl.pallas_call(
        body, name="swa_fwd", grid=(nblk // bps,),
        in_specs=[pl.BlockSpec((SWA_PAIRS, sb, LANES), main),
                  pl.BlockSpec((SWA_KV, sb, LANES), main), pl.BlockSpec((SWA_KV, SWA_BLOCK, LANES), prev),
                  pl.BlockSpec((SWA_KV, sb, LANES), main), pl.BlockSpec((SWA_KV, SWA_BLOCK, LANES), prev),
                  _const((SWA_HEADS, 1, LANES))],
        out_specs=[pl.BlockSpec((SWA_PAIRS, sb, LANES), main), pl.BlockSpec((SWA_HEADS, sb, LANES), main)],
        out_shape=[_sds((SWA_PAIRS, t, LANES), jnp.float32), _sds((SWA_HEADS, t, LANES), jnp.float32)],
        compiler_params=_cparams(("parallel",)),
    )(q, k, k, v, v, sinks)


def _swa_bwd(q, k, v, sinks, do, lse, dsum):
    _, t, _ = q.shape
    nblk, bps, sb = _swa_blocks(t)
    steps = nblk // bps
    scale = SWA_D ** -0.5

    def body(q_ref, k_ref, kp_ref, v_ref, vp_ref, s_ref, do_ref, l_ref, d_ref, qn_ref, don_ref, ln_ref, dn_ref,
             dq_ref, dk_ref, dv_ref, ds_ref):
        n = pl.program_id(0)

        @pl.when(n == 0)
        def _():
            ds_ref[...] = jnp.zeros_like(ds_ref)

        m_cur, m_prev = _swa_band_masks()
        everything = slice(0, SWA_BLOCK)

        def probs(qs, keys, mask, lcol):
            return jnp.where(mask, jnp.exp(_dot_nt(qs, keys) * scale - lcol), 0.0)

        def dscores(pm, dos, vals, dcol):
            return _mx(pm * (_dot_nt(dos, vals) - dcol) * scale)

        for c in range(SWA_KV):
            sink = _swa_sink_col(s_ref, c)
            dk_acc = [jnp.zeros((SWA_BLOCK, LANES), jnp.float32) for _ in range(bps)]
            dv_acc = [jnp.zeros((SWA_BLOCK, LANES), jnp.float32) for _ in range(bps)]
            for b in range(bps):
                rows = slice(b * SWA_BLOCK, (b + 1) * SWA_BLOCK)
                kc, vc = k_ref[c, rows, :], v_ref[c, rows, :]
                if b == 0:
                    kp, vp, mp = kp_ref[c], vp_ref[c], jnp.logical_and(m_prev, n > 0)
                else:
                    before = slice((b - 1) * SWA_BLOCK, b * SWA_BLOCK)
                    kp, vp, mp = k_ref[c, before, :], v_ref[c, before, :], m_prev
                qs = _swa_stack(q_ref, c, rows)
                dos = _swa_stack(do_ref, c, rows)
                lcol = _swa_cols(l_ref, c, rows)
                dcol = _swa_cols(d_ref, c, rows)
                p_c = probs(qs, kc, m_cur, lcol)
                p_p = probs(qs, kp, mp, lcol)
                ds_c = dscores(p_c, dos, vc, dcol)
                ds_p = dscores(p_p, dos, vp, dcol)
                _swa_unstack_pairs(dq_ref, c, rows, _dot(ds_c, kc) + _dot(ds_p, kp))
                dk_acc[b] = dk_acc[b] + _dot_tn(ds_c, qs)
                dv_acc[b] = dv_acc[b] + _dot_tn(_mx(p_c), dos)
                if b > 0:
                    dk_acc[b - 1] = dk_acc[b - 1] + _dot_tn(ds_p, qs)
                    dv_acc[b - 1] = dv_acc[b - 1] + _dot_tn(_mx(p_p), dos)
                p_sink = jnp.exp(sink - lcol) * dcol
                for g in range(SWA_GROUP):
                    ds_ref[SWA_GROUP * c + g] += -jnp.sum(p_sink[g * SWA_BLOCK:(g + 1) * SWA_BLOCK])
            tail = slice((bps - 1) * SWA_BLOCK, bps * SWA_BLOCK)
            kc, vc = k_ref[c, tail, :], v_ref[c, tail, :]
            qs = _swa_stack(qn_ref, c, everything)
            dos = _swa_stack(don_ref, c, everything)
            lcol = _swa_cols(ln_ref, c, everything)
            dcol = _swa_cols(dn_ref, c, everything)
            p_p = probs(qs, kc, jnp.logical_and(m_prev, n < steps - 1), lcol)
            ds_p = dscores(p_p, dos, vc, dcol)
            dk_acc[bps - 1] = dk_acc[bps - 1] + _dot_tn(ds_p, qs)
            dv_acc[bps - 1] = dv_acc[bps - 1] + _dot_tn(_mx(p_p), dos)
            for b in range(bps):
                rows = slice(b * SWA_BLOCK, (b + 1) * SWA_BLOCK)
                dk_ref[c, rows, :] = dk_acc[b]
                dv_ref[c, rows, :] = dv_acc[b]

    main = lambda n: (0, n, 0)
    prev = lambda n: (0, jnp.maximum(n * bps - 1, 0), 0)
    nxt = lambda n: (0, jnp.minimum((n + 1) * bps, nblk - 1), 0)
    pairs = pl.BlockSpec((SWA_PAIRS, sb, LANES), main)
    kvs = pl.BlockSpec((SWA_KV, sb, LANES), main)
    kv_prev = pl.BlockSpec((SWA_KV, SWA_BLOCK, LANES), prev)
    stats = pl.BlockSpec((SWA_HEADS, sb, LANES), main)
    pairs_next = pl.BlockSpec((SWA_PAIRS, SWA_BLOCK, LANES), nxt)
    stats_next = pl.BlockSpec((SWA_HEADS, SWA_BLOCK, LANES), nxt)
    return pl.pallas_call(
        body, name="swa_bwd", grid=(steps,),
        in_specs=[pairs, kvs, kv_prev, kvs, kv_prev, _const((SWA_HEADS, 1, LANES)), pairs, stats, stats,
                  pairs_next, pairs_next, stats_next, stats_next],
        out_specs=[pairs, kvs, kvs, _acc((SWA_HEADS, 1, LANES))],
        out_shape=[_sds((SWA_PAIRS, t, LANES), jnp.float32), _sds((SWA_KV, t, LANES), jnp.float32),
                   _sds((SWA_KV, t, LANES), jnp.float32), _sds((SWA_HEADS, 1, LANES), jnp.float32)],
        compiler_params=_cparams(("arbitrary",)),
    )(q, k, k, v, v, sinks, do, lse, dsum, q, do, lse, dsum)


MIX_SLABS = 4
MIX_WIDTH = MIX_SLABS * LANES


def _mix_out_fwd(x, oa, ob, ga, gb, wo_a, wo_b):
    t, d = x.shape
    tb = min(_TB, t)

    def body(x_ref, oa_ref, ob_ref, ga_ref, gb_ref, woa_ref, wob_ref, xo_ref):
        y = x_ref[...]
        for o_ref, g_ref, w_ref in ((oa_ref, ga_ref, woa_ref), (ob_ref, gb_ref, wob_ref)):
            r = _rsq(sum(_sumsq(o_ref[h]) for h in range(MIX_SLABS)), MIX_WIDTH)
            for h in range(MIX_SLABS):
                y = y + _dot(_mx(o_ref[h] * r * g_ref[h]), w_ref[h])
        xo_ref[...] = y

    slab = _heads_rows(MIX_SLABS, tb, LANES)
    return pl.pallas_call(
        body, name="mix_out_fwd", grid=(t // tb,),
        in_specs=[_rows(tb, d), slab, slab, _const(ga.shape), _const(gb.shape), _const(wo_a.shape), _const(wo_b.shape)],
        out_specs=_rows(tb, d),
        out_shape=_sds((t, d), jnp.float32),
        compiler_params=_cparams(("parallel",)),
    )(x, oa, ob, ga, gb, wo_a, wo_b)


def _mix_out_bwd(dx, oa, ob, ga, gb, wo_a, wo_b):
    t, d = dx.shape
    tb = min(_TB, t)

    def group(o_ref, g_ref, w_ref, dyb, do_ref, n_ref, col0, dg_ref):
        r = _rsq(sum(_sumsq(o_ref[h]) for h in range(MIX_SLABS)), MIX_WIDTH)
        xh, dn = [], []
        for h in range(MIX_SLABS):
            xh.append(o_ref[h] * r)
            n_ref[:, col0 + h * LANES:col0 + (h + 1) * LANES] = _mx(xh[h] * g_ref[h])
            dm = _dot_nt(dyb, w_ref[h])
            dg_ref[h] += _colsum(dm * xh[h])
            dn.append(dm * g_ref[h])
        c = sum(_rowsum(dn[h] * xh[h]) for h in range(MIX_SLABS)) * (1.0 / MIX_WIDTH)
        prods = []
        for h in range(MIX_SLABS):
            do = r * (dn[h] - xh[h] * c)
            do_ref[h] = do.astype(do_ref.dtype)
            prods.append(do * o_ref[h])
        return prods

    def body(dx_ref, oa_ref, ob_ref, ga_ref, gb_ref, woa_ref, wob_ref,
             doa_ref, dsa_ref, dob_ref, dsb_ref, n_ref, dy_ref, dga_ref, dgb_ref):
        @pl.when(pl.program_id(0) == 0)
        def _():
            dga_ref[...] = jnp.zeros_like(dga_ref)
            dgb_ref[...] = jnp.zeros_like(dgb_ref)

        dyb = _mx(dx_ref[...])
        dy_ref[...] = dyb
        for h, pr in enumerate(group(oa_ref, ga_ref, woa_ref, dyb, doa_ref, n_ref, 0, dga_ref)):
            dsa_ref[h] = jnp.broadcast_to(_rowsum(pr), (tb, LANES))
        low = _low_half()
        for j, pr in enumerate(group(ob_ref, gb_ref, wob_ref, dyb, dob_ref, n_ref, MIX_WIDTH, dgb_ref)):
            dsb_ref[2 * j] = jnp.broadcast_to(_rowsum(jnp.where(low, pr, 0.0)), (tb, LANES))
            dsb_ref[2 * j + 1] = jnp.broadcast_to(_rowsum(jnp.where(low, 0.0, pr)), (tb, LANES))

    slab = _heads_rows(MIX_SLABS, tb, LANES)
    return pl.pallas_call(
        body, name="mix_out_bwd", grid=(t // tb,),
        in_specs=[_rows(tb, d), slab, slab, _const(ga.shape), _const(gb.shape), _const(wo_a.shape), _const(wo_b.shape)],
        out_specs=[slab, slab, slab, _heads_rows(SWA_HEADS, tb, LANES), _rows(tb, 2 * MIX_WIDTH), _rows(tb, d),
                   _acc(ga.shape), _acc(gb.shape)],
        out_shape=[_sds((MIX_SLABS, t, LANES), _MXU), _sds((MIX_SLABS, t, LANES), jnp.float32),
                   _sds((MIX_SLABS, t, LANES), jnp.float32), _sds((SWA_HEADS, t, LANES), jnp.float32),
                   _sds((t, 2 * MIX_WIDTH), _MXU), _sds((t, d), _MXU),
                   _sds(ga.shape, jnp.float32), _sds(gb.shape, jnp.float32)],
        compiler_params=_cparams(("arbitrary",)),
    )(dx, oa, ob, ga, gb, wo_a, wo_b)


def _loss_head(y, target):
    t, d = y.shape
    tb = min(_TB, t)

    def body(y_ref, t_ref, dy_ref, acc_ref):
        @pl.when(pl.program_id(0) == 0)
        def _():
            acc_ref[...] = jnp.zeros_like(acc_ref)

        err = y_ref[...] - t_ref[...]
        dy_ref[...] = err * (1.0 / d)
        acc_ref[...] += jnp.sum(err * err)

    return pl.pallas_call(
        body, name="loss_head", grid=(t // tb,),
        in_specs=[_rows(tb, d), _rows(tb, d)],
        out_specs=[_rows(tb, d), _acc((8, LANES))],
        out_shape=[_sds((t, d), jnp.float32), _sds((8, LANES), jnp.float32)],
        compiler_params=_cparams(("arbitrary",)),
    )(y, target)


def _is_transposed(name):
    return name not in ROW_SHARDED


def _pack_layer(shards, l, width, names):
    rows = [(shards[n][l].T if _is_transposed(n) else shards[n][l]).reshape(-1, width) for n in names]
    return jnp.concatenate(rows, axis=0)


def _full_shape(like, name):
    _, a, b = like[name].shape
    return (N_DEV * b, a) if _is_transposed(name) else (N_DEV * a, b)


def _unpack_full(gathered, like, names):
    out, off = {}, 0
    for n in names:
        rows_n = like[n][0].size // gathered.shape[-1]
        out[n] = gathered[:, off:off + rows_n].reshape(_full_shape(like, n))
        off += rows_n
    return out


def _stored(a, name):
    return jnp.swapaxes(a, 1, 2) if _is_transposed(name) else a


def _grads_by_destination(grads, width, names):
    by_dest = lambda n: grads[n].reshape(N_DEV, -1, width)
    if names is OTHER_BIG:
        return [jnp.concatenate([by_dest(n) for n in names], axis=1)]
    return [by_dest(n) for n in names]


def _shards_from_rows(rows, like):
    out = dict(zip(FFN_BIG, rows[:len(FFN_BIG)]))
    rest, off = rows[len(FFN_BIG)], 0
    for n in OTHER_BIG:
        _, a, b = like[n].shape
        rows_n = a * b // rest.shape[-1]
        out[n] = rest[off:off + rows_n].reshape((b, a) if _is_transposed(n) else (a, b))
        off += rows_n
    return out


def _small_rows(n_elems):
    return -(-n_elems // LANES)


def _pack_small(arrays):
    parts = []
    for n in SMALL:
        v = arrays[n]
        depth, width = v.shape
        padded = _small_rows(width) * LANES
        parts.append(jnp.pad(v, ((0, 0), (0, padded - width))).reshape(-1, LANES))
    packed = jnp.concatenate(parts, axis=0)
    return jnp.pad(packed, ((0, (-packed.shape[0]) % 8), (0, 0)))


def _unpack_small(packed, like):
    out, off = {}, 0
    for n in SMALL:
        depth, width = like[n].shape
        rows_n = _small_rows(width)
        seg = packed[off:off + depth * rows_n].reshape(depth, rows_n * LANES)
        out[n] = seg[:, :width]
        off += depth * rows_n
    return out


def _rope_tables(t):
    pos = jnp.arange(t, dtype=jnp.float32)
    inv = 1.0 / (ROPE_THETA ** (jnp.arange(0, MLA_ROPE, 2, dtype=jnp.float32) / MLA_ROPE))
    ang = pos[:, None] * inv[None, :]
    cos, sin = jnp.cos(ang), jnp.sin(ang)
    return jnp.concatenate([cos, cos, cos, cos], axis=1), jnp.concatenate([-sin, sin, -sin, sin], axis=1)


def _pad_lanes(a, width):
    return jnp.pad(a, [(0, 0)] * (a.ndim - 1) + [(0, width - a.shape[-1])])


def _ffn_params(full, small, l, tag):
    return small[tag + "_norm"][l][None, :], full[tag + "_w_gate"], full[tag + "_w_up"], full[tag + "_w_down"]


def _mixer_params(full, small, l):
    w_in = full["w_in"]
    d = w_in.shape[1]
    mla_rows = W_IN_COLS[0]
    w_in_p = jnp.concatenate([w_in[:mla_rows], jnp.zeros((LANES - MLA_ROPE, d), w_in.dtype), w_in[mla_rows:]], axis=0)
    wqb = full["mla_w_q_b"].reshape(MLA_HEADS, MLA_QK, MLA_Q_RANK)
    wqb = jnp.pad(wqb, ((0, 0), (0, MLA_QK_PAD - MLA_QK), (0, 0)))
    row = lambda name: small[name][l][None, :]
    twice = lambda g: jnp.concatenate([g, g], axis=1)
    prep = {
        "mix_g": row("mix_norm"), "w_in": w_in_p,
        "g_qa": row("mla_q_a_norm"), "wqb": wqb,
        "g_kva": row("mla_kv_a_norm"), "w_kvb": full["mla_w_kv_b"],
        "gq_n": row("mla_q_norm")[:, :MLA_NOPE], "gq_r": _pad_lanes(row("mla_q_norm")[:, MLA_NOPE:], LANES),
        "gk_n": row("mla_k_norm")[:, :MLA_NOPE], "gk_r": _pad_lanes(row("mla_k_norm")[:, MLA_NOPE:], LANES),
        "g_sq": twice(row("swa_q_norm")), "g_sk": twice(row("swa_k_norm")),
    }
    return {
        "prep": prep,
        "sinks": jnp.broadcast_to(small["swa_sinks"][l][:, None, None], (SWA_HEADS, 1, LANES)),
        "ga": small["mla_out_norm"][l].reshape(MIX_SLABS, 1, LANES),
        "gb": small["swa_out_norm"][l].reshape(MIX_SLABS, 1, LANES),
        "wo_a": full["w_o"][:MIX_WIDTH].reshape(MIX_SLABS, LANES, d),
        "wo_b": full["w_o"][MIX_WIDTH:].reshape(MIX_SLABS, LANES, d),
    }


def _ffn_backward(x_in, dxo, kept, params, tag, sides=()):
    gain, wg, wu, wd = params
    h, s, fa, fu = kept
    (dxi, da, du, dy, dg), side_out = _ffn_dgrad(x_in, gain, dxo, fa, fu, wg, wu, wd, sides)
    dwg = _tn_matmul(da, h, "wgrad_" + tag + "_gate")
    dwu = _tn_matmul(du, h, "wgrad_" + tag + "_up")
    dwd = _tn_matmul(s, dy, "wgrad_" + tag + "_down")
    return dxi, dg[0], dwg, dwu, dwd, side_out


def kernel(x, ffn1_norm, ffn1_w_gate, ffn1_w_up, ffn1_w_down, mix_norm, w_in, mla_q_a_norm, mla_w_q_b, mla_kv_a_norm, mla_w_kv_b, mla_q_norm, mla_k_norm, swa_q_norm, swa_k_norm, swa_sinks, mla_out_norm, swa_out_norm, w_o, ffn2_norm, ffn2_w_gate, ffn2_w_up, ffn2_w_down, loss_target, m_ffn1_norm, m_ffn1_w_gate, m_ffn1_w_up, m_ffn1_w_down, m_mix_norm, m_w_in, m_mla_q_a_norm, m_mla_w_q_b, m_mla_kv_a_norm, m_mla_w_kv_b, m_mla_q_norm, m_mla_k_norm, m_swa_q_norm, m_swa_k_norm, m_swa_sinks, m_mla_out_norm, m_swa_out_norm, m_w_o, m_ffn2_norm, m_ffn2_w_gate, m_ffn2_w_up, m_ffn2_w_down, v_ffn1_norm, v_ffn1_w_gate, v_ffn1_w_up, v_ffn1_w_down, v_mix_norm, v_w_in, v_mla_q_a_norm, v_mla_w_q_b, v_mla_kv_a_norm, v_mla_w_kv_b, v_mla_q_norm, v_mla_k_norm, v_swa_q_norm, v_swa_k_norm, v_swa_sinks, v_mla_out_norm, v_swa_out_norm, v_w_o, v_ffn2_norm, v_ffn2_w_gate, v_ffn2_w_up, v_ffn2_w_down):
    local = dict(locals())
    w = {n: local[n] for n in WEIGHTS}
    m = {n: local["m_" + n] for n in WEIGHTS}
    v = {n: local["v_" + n] for n in WEIGHTS}
    depth = ffn1_norm.shape[0]
    t, d = x.shape[-2], x.shape[-1]
    x2d = x.reshape(t, d)
    target = loss_target.reshape(t, d)
    bq = min(_BQ, t)

    big = {n: w[n] for n in BIG}
    packed = [[_mx(_pack_layer(big, l, d, names)) for names in GATHER_ORDER] for l in range(depth)]
    cos, sin_s = _rope_tables(t)
    x_i, y_i, c_i = _position()
    dest_idx = jnp.stack([4 * px + 2 * py + c_i for px, py in _relations(x_i, y_i)]).astype(jnp.int32)

    params, saved = [], []
    xc = x2d
    ffn1_full = _unpack_full(_all_gather(packed[0][0]), big, FFN1_BIG)
    for l in range(depth):
        pr = {"ffn1": _ffn_params(ffn1_full, w, l, "ffn1")}
        x0 = xc
        (x1, *kept1), ((mixer_gathered,),) = _ffn_fwd(x0, *pr["ffn1"], sides=[_side_gather(packed[l][1])])
        pr.update(_mixer_params(_unpack_full(mixer_gathered, big, OTHER_BIG), w, l))
        qa, ka, va, qb, kb, vb = _prep_fwd(x1, cos, sin_s, pr["prep"])
        (oa, lse_a), ((ffn2_gathered,),) = _mla_fwd(qa, ka, va, sides=[_side_gather(packed[l][2])])
        pr["ffn2"] = _ffn_params(_unpack_full(ffn2_gathered, big, FFN2_BIG), w, l, "ffn2")
        ob, lse_b = _swa_fwd(qb, kb, vb, pr["sinks"])
        x2 = _mix_out_fwd(x1, oa, ob, pr["ga"], pr["gb"], pr["wo_a"], pr["wo_b"])
        (x3, *kept2), next_gathered = _ffn_fwd(
            x2, *pr["ffn2"], sides=[_side_gather(packed[l + 1][0])] if l + 1 < depth else [])
        if next_gathered:
            ffn1_full = _unpack_full(next_gathered[0][0], big, FFN1_BIG)
        params.append(pr)
        saved.append((x0, kept1, x1, qa, ka, va, qb, kb, vb, oa, lse_a, ob, lse_b, x2, kept2))
        xc = x3

    dx, sq_err = _loss_head(xc, target)
    loss = lax.psum(0.5 / d * sq_err[0, 0], MESH_AXES)

    def chip_sums(arrays, sibling_parts):
        sums = _rs_chip_sums(arrays, sibling_parts, dest_idx)
        return sums[:len(arrays)], sums[len(arrays):]

    grad_shards = [None] * depth
    small_grads = {n: [None] * depth for n in SMALL}
    upper = None
    for l in reversed(range(depth)):
        pr = params[l]
        lowest = l == 0
        x0, kept1, x1, qa, ka, va, qb, kb, vb, oa, lse_a, ob, lse_b, x2, kept2 = saved[l]
        g = {}
        dx, small_grads["ffn2_norm"][l], g["ffn2_w_gate"], g["ffn2_w_up"], g["ffn2_w_down"], side_out = _ffn_backward(
            x2, dx, kept2, pr["ffn2"], "ffn2", [_side_sibling(upper)] if upper else [])
        if upper:
            upper_owns, upper_wires = chip_sums(upper, side_out[0])
        early = _grads_by_destination(g, d, FFN2_BIG) if lowest else None

        doa, dsum_a, dob, dsum_b, mixed, dyb, dga, dgb = _mix_out_bwd(
            dx, oa, ob, pr["ga"], pr["gb"], pr["wo_a"], pr["wo_b"])
        small_grads["mla_out_norm"][l] = dga.reshape(-1)
        small_grads["swa_out_norm"][l] = dgb.reshape(-1)
        g["w_o"] = _tn_matmul(mixed, dyb, "wgrad_wo")

        rows_of = lambda s: s[:, :, 0].reshape(MLA_HEADS, t // bq, 1, bq)
        sides = ([_side_chips(upper_wires)] if upper else []) + ([_side_sibling(early)] if lowest else [])
        (dqa, dka, dva), side_out = _mla_bwd(qa, ka, va, doa, rows_of(lse_a), rows_of(dsum_a), sides)
        if upper:
            grad_shards[l + 1] = _shards_from_rows(_rs_sum(upper_owns, side_out[0]), big)
        if lowest:
            early_owns, early_wires = chip_sums(early, side_out[-1])
        dqb, dkb, dvb, dsinks = _swa_bwd(qb, kb, vb, pr["sinks"], dob, lse_b, dsum_b)
        small_grads["swa_sinks"][l] = dsinks[:, 0, 0]

        outs, side_out = _prep_bwd(x1, dx, cos, sin_s, pr["prep"], dqa, dka, dva, dqb, dkb, dvb,
                                   [_side_chips(early_wires)] if lowest else [])
        if lowest:
            early_rows = _rs_sum(early_owns, side_out[0])
        dx = outs[0]
        pg = dict(zip(PREP_WEIGHTS, outs[1:]))
        g["w_in"] = jnp.concatenate([pg["w_in"][:W_IN_COLS[0]], pg["w_in"][C_QS:]], axis=0)
        g["mla_w_q_b"] = pg["wqb"][:, :MLA_QK].reshape(MLA_HEADS * MLA_QK, MLA_Q_RANK)
        g["mla_w_kv_b"] = pg["w_kvb"]
        fold = lambda gg: gg[0, :HALF] + gg[0, HALF:]
        small_grads["mix_norm"][l] = pg["mix_g"][0]
        small_grads["mla_q_a_norm"][l] = pg["g_qa"][0]
        small_grads["mla_kv_a_norm"][l] = pg["g_kva"][0]
        small_grads["mla_q_norm"][l] = jnp.concatenate([pg["gq_n"][0], pg["gq_r"][0, :MLA_ROPE]])
        small_grads["mla_k_norm"][l] = jnp.concatenate([pg["gk_n"][0], pg["gk_r"][0, :MLA_ROPE]])
        small_grads["swa_q_norm"][l] = fold(pg["g_sq"])
        small_grads["swa_k_norm"][l] = fold(pg["g_sk"])

        dx, small_grads["ffn1_norm"][l], g["ffn1_w_gate"], g["ffn1_w_up"], g["ffn1_w_down"], _ = _ffn_backward(
            x0, dx, kept1, pr["ffn1"], "ffn1")
        late = _grads_by_destination(g, d, FFN1_BIG) + _grads_by_destination(g, d, OTHER_BIG)
        if lowest:
            late_owns, late_wires = chip_sums(late, _rs_sibling_exchange(late))
            late_rows = _rs_sum(late_owns, _rs_chip_exchange(late_wires))
            grad_shards[l] = _shards_from_rows(late_rows[:-1] + early_rows + late_rows[-1:], big)
        else:
            upper = late[:-1] + _grads_by_destination(g, d, FFN2_BIG) + late[-1:]

    grad_big, delta_big, new_m_big, new_v_big = {}, {}, {}, {}
    for n in BIG:
        g_st = jnp.stack([grad_shards[l][n] for l in range(depth)])
        d_st, m_st, v_st = _adamw(g_st, _stored(w[n], n), _stored(m[n], n), _stored(v[n], n), n)
        grad_big[n], delta_big[n], new_m_big[n], new_v_big[n] = (_stored(a, n) for a in (g_st, d_st, m_st, v_st))

    small_partial = _pack_small({n: jnp.stack(small_grads[n]) for n in SMALL})
    g_s = _all_reduce_small(small_partial)
    d_s, m_s, v_s = _adamw_small(g_s, _pack_small(w), _pack_small(m), _pack_small(v))
    grad_small, delta_small, new_m_small, new_v_small = (_unpack_small(a, w) for a in (g_s, d_s, m_s, v_s))

    def ordered(big, small):
        return [big[n] if n in big else small[n] for n in WEIGHTS]

    return (loss, dx.reshape(x.shape), *ordered(grad_big, grad_small), *ordered(delta_big, delta_small),
            *ordered(new_m_big, new_m_small), *ordered(new_v_big, new_v_small))
```

```python
import jax
import jax.numpy as jnp
from jax import lax
from jax.experimental import pallas as pl
from jax.experimental.pallas import tpu as pltpu

N_DEV = 8
EPS = 1e-6
ROPE_THETA = 10000.0
MLA_HEADS = 4
MLA_Q_RANK = 256
MLA_KV_RANK = 128
MLA_NOPE = 128
MLA_ROPE = 64
MLA_V = 128
MLA_QK = MLA_NOPE + MLA_ROPE
MLA_QK_PAD = 256
SWA_HEADS = 8
SWA_KV = 2
SWA_GROUP = SWA_HEADS // SWA_KV
SWA_D = 64
SWA_BLOCK = 128
ADAM_LR = 0.001
ADAM_B1 = 0.9
ADAM_B2 = 0.999
ADAM_EPS = 1e-08
ADAM_WD = 0.01
ADAM_STEP = 10

LANES = 128
HALF = LANES // 2
SWA_PAIRS = SWA_HEADS // 2
W_IN_COLS = (MLA_Q_RANK + MLA_KV_RANK + MLA_ROPE, SWA_HEADS * SWA_D + 2 * SWA_KV * SWA_D)
VMEM_LIMIT = 56 * 1024 * 1024

_MXU = jnp.bfloat16
_TB = 256
_TB_MIX = 512
_BQ = 512
_STRIP = 32
_BWD_GROUPS = (4, 2, 1)
_TK = 512
_SWA_STEP = 4
RS_ROW_BLOCKS = 2

BIG = ("ffn1_w_gate", "ffn1_w_up", "ffn1_w_down", "w_in", "mla_w_q_b", "mla_w_kv_b", "w_o",
       "ffn2_w_gate", "ffn2_w_up", "ffn2_w_down")
ROW_SHARDED = ("ffn1_w_down", "w_o", "ffn2_w_down")
FFN1_BIG = ("ffn1_w_gate", "ffn1_w_up", "ffn1_w_down")
FFN2_BIG = ("ffn2_w_gate", "ffn2_w_up", "ffn2_w_down")
FFN_BIG = FFN1_BIG + FFN2_BIG
OTHER_BIG = ("w_o", "w_in", "mla_w_q_b", "mla_w_kv_b")
GATHER_ORDER = (FFN1_BIG, OTHER_BIG, FFN2_BIG)
SMALL = ("ffn1_norm", "mix_norm", "mla_q_a_norm", "mla_kv_a_norm", "mla_q_norm", "mla_k_norm",
         "swa_q_norm", "swa_k_norm", "swa_sinks", "mla_out_norm", "swa_out_norm", "ffn2_norm")
WEIGHTS = ("ffn1_norm", "ffn1_w_gate", "ffn1_w_up", "ffn1_w_down", "mix_norm", "w_in", "mla_q_a_norm",
           "mla_w_q_b", "mla_kv_a_norm", "mla_w_kv_b", "mla_q_norm", "mla_k_norm", "swa_q_norm",
           "swa_k_norm", "swa_sinks", "mla_out_norm", "swa_out_norm", "w_o", "ffn2_norm",
           "ffn2_w_gate", "ffn2_w_up", "ffn2_w_down")
MESH_AXES = ("x", "y", "c")
MESH = pl.DeviceIdType.MESH
NEG = -1e30
LOG2_E = 1.4426950408889634


def _f32(t):
    return t.astype(jnp.float32)


def _mx(t):
    return t.astype(_MXU)


def _dot(a, b):
    return jnp.dot(a, b, preferred_element_type=jnp.float32)


def _dot_nt(a, b):
    return lax.dot_general(a, b, (((1,), (1,)), ((), ())), preferred_element_type=jnp.float32)


def _dot_tn(a, b):
    return lax.dot_general(a, b, (((0,), (0,)), ((), ())), preferred_element_type=jnp.float32)


def _rsq(ss, n):
    return lax.rsqrt(ss * (1.0 / n) + EPS)


def _sumsq(t):
    return jnp.sum(t * t, axis=-1, keepdims=True)


def _rowsum(t):
    return jnp.sum(t, axis=-1, keepdims=True)


def _rowmax(t):
    return jnp.max(t, axis=-1, keepdims=True)


def _colsum(t):
    return jnp.sum(t, axis=0, keepdims=True)


def _lane():
    return lax.broadcasted_iota(jnp.int32, (1, LANES), 1)


def _low_half():
    return _lane() < HALF


def _swap32(t):
    return jnp.where((_lane() & 32) == 0, pltpu.roll(t, 96, 1), pltpu.roll(t, 32, 1))


def _rope(t, cos, sin_signed):
    return t * cos + _swap32(t) * sin_signed


def _rope_bwd(d, cos, sin_signed):
    return d * cos + _swap32(d * sin_signed)


def _half_sums(t):
    low = _low_half()
    return jnp.where(low, _rowsum(jnp.where(low, t, 0.0)), _rowsum(jnp.where(low, 0.0, t)))


def _dup_halves(pair):
    low = _low_half()
    swapped = pltpu.roll(pair, HALF, 1)
    return jnp.where(low, pair, swapped), jnp.where(low, swapped, pair)


def _undup_halves(d0, d1):
    return jnp.where(_low_half(), d0 + pltpu.roll(d0, HALF, 1), d1 + pltpu.roll(d1, HALF, 1))


def _pick_halves(a, b):
    return jnp.where(_low_half(), a, b)


def _norm_bwd(dn_list, xh_list, r, n):
    c = sum(_rowsum(dn * xh) for dn, xh in zip(dn_list, xh_list)) * (1.0 / n)
    return [r * (dn - xh * c) for dn, xh in zip(dn_list, xh_list)]


def _cparams(semantics):
    return pltpu.CompilerParams(dimension_semantics=semantics, vmem_limit_bytes=VMEM_LIMIT)


def _const(shape):
    nd = len(shape)
    return pl.BlockSpec(shape, lambda *_: (0,) * nd, pipeline_mode=pl.Buffered(1))


def _acc(shape):
    nd = len(shape)
    return pl.BlockSpec(shape, lambda *_: (0,) * nd)


def _rows(tb, width):
    return pl.BlockSpec((tb, width), lambda i: (i, 0))


def _heads_rows(h, tb, width):
    return pl.BlockSpec((h, tb, width), lambda i: (0, i, 0))


def _sds(shape, dtype):
    return jax.ShapeDtypeStruct(shape, dtype)


def _position():
    return lax.axis_index("x"), lax.axis_index("y"), lax.axis_index("c")


def _all_gather(xp):
    def body(x_ref, out_ref, send_sems, recv_sems, local_sem):
        x, y, c = _position()
        me, sibling = (x, y, c), (x, y, 1 - c)
        chips = [(1 - x, y), (x, 1 - y), (1 - x, 1 - y)]

        def rows(px, py, pc):
            return out_ref.at[4 * px + 2 * py + pc]

        def copy(k, block, to, src=None):
            return pltpu.make_async_remote_copy(
                src_ref=rows(*block) if src is None else src, dst_ref=rows(*block),
                send_sem=send_sems.at[k], recv_sem=recv_sems.at[k], device_id=to, device_id_type=MESH)

        mine = pltpu.make_async_copy(x_ref, rows(*me), local_sem)
        mine.start()
        first = [copy(0, me, sibling, src=x_ref)]
        first += [copy(1 + j, me, (*chip, c), src=x_ref) for j, chip in enumerate(chips)]
        for cp in first:
            cp.start()
        passed = [copy(4 + j, (*chip, c), sibling) for j, chip in enumerate(chips)]
        for j, chip in enumerate(chips):
            copy(1 + j, (*chip, c), me).wait_recv()
            passed[j].start()
        copy(0, sibling, me).wait_recv()
        for j, chip in enumerate(chips):
            copy(4 + j, (*chip, 1 - c), me).wait_recv()
        for cp in first + passed:
            cp.wait_send()
        mine.wait()

    return pl.pallas_call(
        body, name="ag_weights",
        out_shape=_sds((N_DEV,) + xp.shape, xp.dtype),
        in_specs=[pl.BlockSpec(memory_space=pl.ANY)],
        out_specs=pl.BlockSpec(memory_space=pl.ANY),
        scratch_shapes=[pltpu.SemaphoreType.DMA((7,)), pltpu.SemaphoreType.DMA((7,)), pltpu.SemaphoreType.DMA],
    )(xp)


def _relations(x, y):
    return [(x, y), (1 - x, y), (x, 1 - y), (1 - x, 1 - y)]


def _remote(src, dst, send_sem, recv_sem, device):
    return pltpu.make_async_remote_copy(src_ref=src, dst_ref=dst, send_sem=send_sem, recv_sem=recv_sem,
                                        device_id=device, device_id_type=MESH)


def _sibling_copies(g_refs, out_refs, send, recv):
    x, y, c = _position()
    n = len(g_refs)
    return [_remote(g.at[4 * px + 2 * py + (1 - c)], o.at[k], send.at[k * n + i], recv.at[k * n + i], (x, y, 1 - c))
            for k, (px, py) in enumerate(_relations(x, y)) for i, (g, o) in enumerate(zip(g_refs, out_refs))]


def _chip_copies(w_refs, out_refs, send, recv):
    x, y, c = _position()
    n = len(w_refs)
    return [_remote(w.at[k + 1], o.at[k], send.at[k * n + i], recv.at[k * n + i], (px, py, c))
            for k, (px, py) in enumerate(_relations(x, y)[1:]) for i, (w, o) in enumerate(zip(w_refs, out_refs))]


def _exchange(arrays, lead, relations, copies_fn, name):
    n = len(arrays)

    def body(*refs):
        copies = copies_fn(refs[:n], refs[n:2 * n], refs[2 * n], refs[2 * n + 1])
        for cp in copies:
            cp.start()
        for cp in copies:
            cp.wait()

    hbm = pl.BlockSpec(memory_space=pl.ANY)
    dma = pltpu.SemaphoreType.DMA
    return pl.pallas_call(
        body, name=name, out_shape=[_sds((lead,) + a.shape[1:], a.dtype) for a in arrays],
        in_specs=[hbm] * n, out_specs=[hbm] * n,
        scratch_shapes=[dma((relations * n,)), dma((relations * n,))],
    )(*arrays)


def _rs_sibling_exchange(gs):
    return _exchange(gs, 4, 4, _sibling_copies, "rs_sibling_exchange")


def _rs_chip_exchange(wires):
    return _exchange(wires, 3, 3, _chip_copies, "rs_chip_exchange")


def _side_exchange(arrays, lead, relations, copies_fn):
    shapes = [_sds((lead,) + a.shape[1:], a.dtype) for a in arrays]
    return list(arrays), shapes, relations * len(arrays), lambda ins, outs, send, recv, local: copies_fn(ins, outs, send, recv)


def _side_sibling(gs):
    return _side_exchange(gs, 4, 4, _sibling_copies)


def _side_chips(wires):
    return _side_exchange(wires, 3, 3, _chip_copies)


def _rs_chip_sums(gs, sibs, dest_idx):
    n = len(gs)

    def body(idx_ref, *refs):
        g_refs, s_refs, own_refs, wire_refs = refs[:n], refs[n:2 * n], refs[2 * n:3 * n], refs[3 * n:]
        totals = [g[0] + s[0] for g, s in zip(g_refs, s_refs)]
        for total, wire in zip(totals, wire_refs):
            wire[0] = total.astype(wire.dtype)

        @pl.when(pl.program_id(1) == 0)
        def _():
            for total, own in zip(totals, own_refs):
                own[...] = total

    def blocks(a, index_map, squeeze):
        rb = a.shape[1] // RS_ROW_BLOCKS
        return pl.BlockSpec((rb, a.shape[2]) if squeeze else (1, rb, a.shape[2]), index_map)

    return pl.pallas_call(
        body, name="rs_chip_sums",
        grid_spec=pltpu.PrefetchScalarGridSpec(
            num_scalar_prefetch=1, grid=(RS_ROW_BLOCKS, 4),
            in_specs=[blocks(g, lambda r, k, idx: (idx[k], r, 0), False) for g in gs]
            + [blocks(g, lambda r, k, idx: (k, r, 0), False) for g in gs],
            out_specs=[blocks(g, lambda r, k, idx: (r, 0), True) for g in gs]
            + [blocks(g, lambda r, k, idx: (k, r, 0), False) for g in gs]),
        out_shape=[_sds(g.shape[1:], jnp.float32) for g in gs] + [_sds((4,) + g.shape[1:], _MXU) for g in gs],
        compiler_params=_cparams(("parallel", "arbitrary")),
    )(dest_idx, *gs, *sibs)


def _side_gather(xp):
    def make(ins, outs, send, recv, local):
        (x_ref,), (out_ref,) = ins, outs
        x, y, c = _position()
        me = 4 * x + 2 * y + c
        copies = [pltpu.make_async_copy(x_ref, out_ref.at[me], local.at[0])]
        for k in range(1, N_DEV):
            peer = (1 - x if k & 4 else x, 1 - y if k & 2 else y, 1 - c if k & 1 else c)
            copies.append(_remote(x_ref, out_ref.at[me], send.at[k - 1], recv.at[k - 1], peer))
        return copies

    return [xp], [_sds((N_DEV,) + xp.shape, xp.dtype)], N_DEV - 1, make


def _call(body, args, sides, *, name, grid, in_specs, out_specs, out_shape, scratch_shapes=(), semantics):
    in_specs, out_specs, out_shape = list(in_specs), list(out_specs), list(out_shape)
    n_in, n_out, n_scr = len(in_specs), len(out_specs), len(scratch_shapes)
    sides = list(sides or [])
    if not sides:
        outs = pl.pallas_call(body, name=name, grid=grid, in_specs=in_specs, out_specs=out_specs, out_shape=out_shape,
                              scratch_shapes=list(scratch_shapes), compiler_params=_cparams(semantics))(*args)
        return list(outs), []
    arrays = [a for side in sides for a in side[0]]
    shapes = [s for side in sides for s in side[1]]
    n_side_in, n_side_out = len(arrays), len(shapes)
    hbm = pl.BlockSpec(memory_space=pl.ANY)

    def with_copies(*refs):
        main_in, refs = refs[:n_in], refs[n_in:]
        side_in, refs = refs[:n_side_in], refs[n_side_in:]
        main_out, refs = refs[:n_out], refs[n_out:]
        side_out, refs = refs[:n_side_out], refs[n_side_out:]
        main_scr, sems = refs[:n_scr], refs[n_scr:]
        copies = []
        for k, (side_arrays, side_shapes, _, make) in enumerate(sides):
            copies += make(side_in[:len(side_arrays)], side_out[:len(side_shapes)], *sems[3 * k:3 * k + 3])
            side_in, side_out = side_in[len(side_arrays):], side_out[len(side_shapes):]
        ids = [pl.program_id(a) for a in range(len(grid))]
        first, last = ids[0] == 0, ids[0] == grid[0] - 1
        for i, size in zip(ids[1:], grid[1:]):
            first, last = jnp.logical_and(first, i == 0), jnp.logical_and(last, i == size - 1)

        @pl.when(first)
        def _():
            for cp in copies:
                cp.start()

        body(*main_in, *main_out, *main_scr)

        @pl.when(last)
        def _():
            for cp in copies:
                cp.wait()

    dma = pltpu.SemaphoreType.DMA
    sem_shapes = [dma((n,)) for side in sides for n in (side[2], side[2], 1)]
    outs = pl.pallas_call(
        with_copies, name=name, grid=grid, in_specs=in_specs + [hbm] * n_side_in,
        out_specs=out_specs + [hbm] * n_side_out, out_shape=out_shape + shapes,
        scratch_shapes=list(scratch_shapes) + sem_shapes,
        compiler_params=_cparams(("arbitrary",) * len(grid)),
    )(*args, *arrays)
    side_outs, rest = [], list(outs[n_out:])
    for side in sides:
        side_outs.append(rest[:len(side[1])])
        rest = rest[len(side[1]):]
    return list(outs[:n_out]), side_outs


def _all_reduce_small(v):
    rows_n = v.shape[0]

    def body(v_ref, out_ref, buf, send_sems, recv_sems):
        x, y, c = _position()
        me = 4 * x + 2 * y + c
        buf[me] = v_ref[...]
        copies = []
        for k in range(1, N_DEV):
            px = 1 - x if k & 4 else x
            py = 1 - y if k & 2 else y
            pc = 1 - c if k & 1 else c
            copies.append(pltpu.make_async_remote_copy(
                src_ref=v_ref, dst_ref=buf.at[me],
                send_sem=send_sems.at[k - 1], recv_sem=recv_sems.at[k - 1], device_id=(px, py, pc), device_id_type=MESH))
        for cp in copies:
            cp.start()
        for cp in copies:
            cp.wait()
        total = buf[0]
        for d in range(1, N_DEV):
            total = total + buf[d]
        out_ref[...] = total

    return pl.pallas_call(
        body, name="ar_small",
        out_shape=_sds((rows_n, LANES), jnp.float32),
        in_specs=[pl.BlockSpec(memory_space=pltpu.VMEM)],
        out_specs=pl.BlockSpec(memory_space=pltpu.VMEM),
        scratch_shapes=[pltpu.VMEM((N_DEV, rows_n, LANES), jnp.float32),
                        pltpu.SemaphoreType.DMA((N_DEV - 1,)), pltpu.SemaphoreType.DMA((N_DEV - 1,))],
    )(v)


def _adamw_math(w, g, m, v):
    m = ADAM_B1 * m + (1.0 - ADAM_B1) * g
    v = ADAM_B2 * v + (1.0 - ADAM_B2) * (g * g)
    m_hat = m / (1.0 - ADAM_B1 ** ADAM_STEP)
    v_hat = v / (1.0 - ADAM_B2 ** ADAM_STEP)
    delta = -ADAM_LR * (m_hat / (jnp.sqrt(v_hat) + ADAM_EPS) + ADAM_WD * w)
    return delta, m, v


def _rs_sum(owns, recvs):
    n = len(owns)

    def body(*refs):
        own_refs, recv_refs, out_refs = refs[:n], refs[n:4 * n], refs[4 * n:]
        for i in range(n):
            r0, r1, r2 = recv_refs[3 * i:3 * i + 3]
            out_refs[i][...] = ((own_refs[i][...] + _f32(r0[0])) + _f32(r1[0])) + _f32(r2[0])

    def row(a):
        return pl.BlockSpec((a.shape[0] // RS_ROW_BLOCKS, a.shape[1]), lambda r: (r, 0))

    def slot(a, k):
        return pl.BlockSpec((1, a.shape[0] // RS_ROW_BLOCKS, a.shape[1]), lambda r: (k, r, 0))

    return pl.pallas_call(
        body, name="rs_sum", grid=(RS_ROW_BLOCKS,),
        in_specs=[row(a) for a in owns] + [slot(a, k) for a in owns for k in range(3)],
        out_specs=[row(a) for a in owns],
        out_shape=[_sds(a.shape, jnp.float32) for a in owns],
        compiler_params=_cparams(("parallel",)),
    )(*owns, *[r for r in recvs for _ in range(3)])


def _adamw(g, w, m, v, name):
    depth, a, b = w.shape

    def body(g_ref, w_ref, m_ref, v_ref, d_out, m_out, v_out):
        delta, m2, v2 = _adamw_math(w_ref[...], g_ref[...], m_ref[...], v_ref[...])
        d_out[...] = delta
        m_out[...] = m2
        v_out[...] = v2

    layer = pl.BlockSpec((1, a, b), lambda l: (l, 0, 0))
    return pl.pallas_call(
        body, name="adamw_" + name, grid=(depth,),
        in_specs=[layer] * 4, out_specs=[layer] * 3,
        out_shape=[_sds(w.shape, jnp.float32)] * 3,
        compiler_params=_cparams(("parallel",)),
    )(g, w, m, v)


def _adamw_small(g, w, m, v):
    def body(g_ref, w_ref, m_ref, v_ref, d_out, m_out, v_out):
        delta, m2, v2 = _adamw_math(w_ref[...], g_ref[...], m_ref[...], v_ref[...])
        d_out[...] = delta
        m_out[...] = m2
        v_out[...] = v2

    vm = pl.BlockSpec(memory_space=pltpu.VMEM)
    return pl.pallas_call(
        body, name="adamw_small",
        in_specs=[vm] * 4, out_specs=[vm] * 3,
        out_shape=[_sds(g.shape, jnp.float32)] * 3,
    )(g, w, m, v)


def _f_chunk(f):
    for cand in (1408, 1024, 512, 256, 128):
        if f % cand == 0:
            return cand
    return f


def _ffn_fwd(x, gain, wg, wu, wd, sides=()):
    t, d = x.shape
    f = wg.shape[0]
    tb = min(_TB, t)
    fc = _f_chunk(f)

    def body(x_ref, g_ref, wg_ref, wu_ref, wd_ref, xo_ref, h_ref, s_ref, fa_ref, fu_ref):
        xv = x_ref[...]
        hb = _mx(xv * _rsq(_sumsq(xv), d) * g_ref[...])
        h_ref[...] = hb
        y = jnp.zeros((tb, d), jnp.float32)
        for c0 in range(0, f, fc):
            a = _dot_nt(hb, wg_ref[c0:c0 + fc, :])
            u = _dot_nt(hb, wu_ref[c0:c0 + fc, :])
            sig = jax.nn.sigmoid(a)
            silu = a * sig
            s = _mx(silu * u)
            s_ref[:, c0:c0 + fc] = s
            fa_ref[:, c0:c0 + fc] = _mx(u * (sig * (1.0 + a * (1.0 - sig))))
            fu_ref[:, c0:c0 + fc] = _mx(silu)
            y = y + _dot(s, wd_ref[c0:c0 + fc, :])
        xo_ref[...] = xv + 0.5 * y

    return _call(
        body, (x, gain, wg, wu, wd), sides, name="ffn_fwd", grid=(t // tb,),
        in_specs=[_rows(tb, d), _const((1, d)), _const((f, d)), _const((f, d)), _const((f, d))],
        out_specs=[_rows(tb, d), _rows(tb, d), _rows(tb, f), _rows(tb, f), _rows(tb, f)],
        out_shape=[_sds((t, d), jnp.float32), _sds((t, d), _MXU), _sds((t, f), _MXU), _sds((t, f), _MXU),
                   _sds((t, f), _MXU)],
        semantics=("parallel",))


def _ffn_dgrad(x, gain, dxo, fa, fu, wg, wu, wd, sides=()):
    t, d = x.shape
    f = wg.shape[0]
    tb = min(_TB, t)
    fc = _f_chunk(f)

    def body(x_ref, g_ref, dxo_ref, fa_ref, fu_ref, wg_ref, wu_ref, wd_ref, dxi_ref, da_ref, du_ref, dy_ref, dg_ref):
        xv = x_ref[...]
        gv = g_ref[...]
        r = _rsq(_sumsq(xv), d)
        xhat = xv * r
        dxo = dxo_ref[...]
        dyb = _mx(0.5 * dxo)
        dy_ref[...] = dyb
        dh = jnp.zeros((tb, d), jnp.float32)
        for c0 in range(0, f, fc):
            ds = _dot_nt(dyb, wd_ref[c0:c0 + fc, :])
            da = _mx(ds * _f32(fa_ref[:, c0:c0 + fc]))
            du = _mx(ds * _f32(fu_ref[:, c0:c0 + fc]))
            da_ref[:, c0:c0 + fc] = da
            du_ref[:, c0:c0 + fc] = du
            dh = dh + _dot(da, wg_ref[c0:c0 + fc, :]) + _dot(du, wu_ref[c0:c0 + fc, :])

        @pl.when(pl.program_id(0) == 0)
        def _():
            dg_ref[...] = jnp.zeros_like(dg_ref)

        dg_ref[...] += _colsum(dh * xhat)
        dn = dh * gv
        dxi_ref[...] = dxo + r * (dn - xhat * (_rowsum(dn * xhat) * (1.0 / d)))

    return _call(
        body, (x, gain, dxo, fa, fu, wg, wu, wd), sides, name="ffn_dgrad", grid=(t // tb,),
        in_specs=[_rows(tb, d), _const((1, d)), _rows(tb, d), _rows(tb, f), _rows(tb, f),
                  _const((f, d)), _const((f, d)), _const((f, d))],
        out_specs=[_rows(tb, d), _rows(tb, f), _rows(tb, f), _rows(tb, d), _acc((1, d))],
        out_shape=[_sds((t, d), jnp.float32), _sds((t, f), _MXU), _sds((t, f), _MXU), _sds((t, d), _MXU),
                   _sds((1, d), jnp.float32)],
        semantics=("arbitrary",))


def _tn_matmul(a, b, name):
    t, m = a.shape
    n = b.shape[1]
    tk = min(_TK, t)
    tn = n
    while m * tn * 4 > 12 * 1024 * 1024 and tn % 256 == 0:
        tn //= 2

    def body(a_ref, b_ref, o_ref):
        @pl.when(pl.program_id(1) == 0)
        def _():
            o_ref[...] = jnp.zeros_like(o_ref)

        o_ref[...] += _dot_tn(a_ref[...], b_ref[...])

    return pl.pallas_call(
        body, name=name, grid=(n // tn, t // tk),
        in_specs=[pl.BlockSpec((tk, m), lambda j, k: (k, 0)), pl.BlockSpec((tk, tn), lambda j, k: (k, j))],
        out_specs=pl.BlockSpec((m, tn), lambda j, k: (0, j)),
        out_shape=_sds((m, n), jnp.float32),
        compiler_params=_cparams(("parallel", "arbitrary")),
    )(a, b)


PREP_WEIGHTS = ("mix_g", "w_in", "g_qa", "wqb", "g_kva", "w_kvb", "gq_n", "gq_r", "gk_n", "gk_r", "g_sq", "g_sk")
C_CQ, C_CKV, C_KPE, C_QS = 0, MLA_Q_RANK, MLA_Q_RANK + MLA_KV_RANK, MLA_Q_RANK + MLA_KV_RANK + LANES
C_KS = C_QS + SWA_HEADS * SWA_D
C_VS = C_KS + LANES
W_IN_PACKED = C_VS + LANES


def _prep_specs(p):
    return [_const(p[n].shape) for n in PREP_WEIGHTS]


def _pair_norm_rope(t, gain, cos, sin_s):
    return _rope(t * _rsq(_half_sums(t * t), SWA_D) * gain, cos, sin_s)


def _prep_fwd(x, cos, sin_s, p):
    t, d = x.shape
    tb = min(_TB_MIX, t)

    def body(x_ref, cos_ref, sin_ref, mix_g, w_in, g_qa, wqb, g_kva, w_kvb, gq_n, gq_r, gk_n, gk_r, g_sq, g_sk,
             qa_ref, ka_ref, va_ref, qb_ref, kb_ref, vb_ref):
        xv = x_ref[...]
        cos_v, sin_v = cos_ref[...], sin_ref[...]
        hb = _mx(xv * _rsq(_sumsq(xv), d) * mix_g[...])
        proj = _dot_nt(hb, w_in[...])
        cq = proj[:, C_CQ:C_CKV]
        cqn = _mx(cq * _rsq(_sumsq(cq), MLA_Q_RANK) * g_qa[...])
        for h in range(MLA_HEADS):
            qh = _dot_nt(cqn, wqb[h])
            qn, qr = qh[:, :MLA_NOPE], qh[:, MLA_NOPE:]
            rh = _rsq(_sumsq(qn) + _sumsq(qr), MLA_QK)
            qa_ref[h, :, 0:MLA_NOPE] = (qn * rh * gq_n[...]).astype(qa_ref.dtype)
            qa_ref[h, :, MLA_NOPE:MLA_QK_PAD] = _rope(qr * rh * gq_r[...], cos_v, sin_v).astype(qa_ref.dtype)
        ckv = proj[:, C_CKV:C_KPE]
        ckvn = _mx(ckv * _rsq(_sumsq(ckv), MLA_KV_RANK) * g_kva[...])
        kpe = proj[:, C_KPE:C_QS]
        ss_pe = _sumsq(kpe)
        kv = _dot_nt(ckvn, w_kvb[...])
        for h in range(MLA_HEADS):
            c0 = h * (MLA_NOPE + MLA_V)
            kn = kv[:, c0:c0 + MLA_NOPE]
            rh = _rsq(_sumsq(kn) + ss_pe, MLA_QK)
            ka_ref[h, :, 0:MLA_NOPE] = (kn * rh * gk_n[...]).astype(ka_ref.dtype)
            ka_ref[h, :, MLA_NOPE:MLA_QK_PAD] = _rope(kpe * rh * gk_r[...], cos_v, sin_v).astype(ka_ref.dtype)
            va_ref[h] = kv[:, c0 + MLA_NOPE:c0 + MLA_NOPE + MLA_V].astype(va_ref.dtype)
        for j in range(SWA_PAIRS):
            c0 = C_QS + j * LANES
            qb_ref[j] = _pair_norm_rope(proj[:, c0:c0 + LANES], g_sq[...], cos_v, sin_v).astype(qb_ref.dtype)
        k0, k1 = _dup_halves(_pair_norm_rope(proj[:, C_KS:C_VS], g_sk[...], cos_v, sin_v))
        kb_ref[0] = k0.astype(kb_ref.dtype)
        kb_ref[1] = k1.astype(kb_ref.dtype)
        v0, v1 = _dup_halves(proj[:, C_VS:W_IN_PACKED])
        vb_ref[0] = v0.astype(vb_ref.dtype)
        vb_ref[1] = v1.astype(vb_ref.dtype)

    return pl.pallas_call(
        body, name="prep_fwd", grid=(t // tb,),
        in_specs=[_rows(tb, d), _rows(tb, LANES), _rows(tb, LANES)] + _prep_specs(p),
        out_specs=[_heads_rows(MLA_HEADS, tb, MLA_QK_PAD), _heads_rows(MLA_HEADS, tb, MLA_QK_PAD),
                   _heads_rows(MLA_HEADS, tb, MLA_V), _heads_rows(SWA_PAIRS, tb, LANES),
                   _heads_rows(SWA_KV, tb, LANES), _heads_rows(SWA_KV, tb, LANES)],
        out_shape=[_sds((MLA_HEADS, t, MLA_QK_PAD), _MXU), _sds((MLA_HEADS, t, MLA_QK_PAD), _MXU),
                   _sds((MLA_HEADS, t, MLA_V), _MXU), _sds((SWA_PAIRS, t, LANES), _MXU),
                   _sds((SWA_KV, t, LANES), _MXU), _sds((SWA_KV, t, LANES), _MXU)],
        compiler_params=_cparams(("parallel",)),
    )(x, cos, sin_s, *[p[n] for n in PREP_WEIGHTS])


def _prep_bwd(x, dxin, cos, sin_s, p, dqa, dka, dva, dqb, dkb, dvb, sides=()):
    t, d = x.shape
    tb = min(_TB_MIX, t)
    n_w = len(PREP_WEIGHTS)

    def body(*refs):
        x_ref, dxin_ref, cos_ref, sin_ref = refs[:4]
        mix_g, w_in, g_qa, wqb, g_kva, w_kvb, gq_n, gq_r, gk_n, gk_r, g_sq, g_sk = refs[4:4 + n_w]
        dqa_ref, dka_ref, dva_ref, dqb_ref, dkb_ref, dvb_ref = refs[4 + n_w:10 + n_w]
        dx_ref = refs[10 + n_w]
        grads = dict(zip(PREP_WEIGHTS, refs[11 + n_w:11 + 2 * n_w]))
        dproj_ref, dkv_ref, dqh_ref = refs[11 + 2 * n_w:]

        @pl.when(pl.program_id(0) == 0)
        def _():
            for ref in grads.values():
                ref[...] = jnp.zeros_like(ref)

        xv = x_ref[...]
        cos_v, sin_v = cos_ref[...], sin_ref[...]
        r0 = _rsq(_sumsq(xv), d)
        xhat = xv * r0
        hb = _mx(xhat * mix_g[...])
        proj = _dot_nt(hb, w_in[...])

        cq = proj[:, C_CQ:C_CKV]
        rq = _rsq(_sumsq(cq), MLA_Q_RANK)
        cqh = cq * rq
        cqn = _mx(cqh * g_qa[...])
        dcqn = jnp.zeros((tb, MLA_Q_RANK), jnp.float32)
        for h in range(MLA_HEADS):
            qh = _dot_nt(cqn, wqb[h])
            qn, qr = qh[:, :MLA_NOPE], qh[:, MLA_NOPE:]
            rh = _rsq(_sumsq(qn) + _sumsq(qr), MLA_QK)
            xh_n, xh_r = qn * rh, qr * rh
            dy_n = dqa_ref[h, :, 0:MLA_NOPE]
            dy_r = _rope_bwd(dqa_ref[h, :, MLA_NOPE:MLA_QK_PAD], cos_v, sin_v)
            grads["gq_n"][...] += _colsum(dy_n * xh_n)
            grads["gq_r"][...] += _colsum(dy_r * xh_r)
            dqn, dqr = _norm_bwd([dy_n * gq_n[...], dy_r * gq_r[...]], [xh_n, xh_r], rh, MLA_QK)
            dqh_ref[:, 0:MLA_NOPE] = _mx(dqn)
            dqh_ref[:, MLA_NOPE:MLA_QK_PAD] = _mx(dqr)
            dqh = dqh_ref[...]
            grads["wqb"][h] += _dot_tn(dqh, cqn)
            dcqn = dcqn + _dot(dqh, wqb[h])
        grads["g_qa"][...] += _colsum(dcqn * cqh)
        (dcq,) = _norm_bwd([dcqn * g_qa[...]], [cqh], rq, MLA_Q_RANK)
        dproj_ref[:, C_CQ:C_CKV] = _mx(dcq)

        ckv = proj[:, C_CKV:C_KPE]
        rkv = _rsq(_sumsq(ckv), MLA_KV_RANK)
        ckvh = ckv * rkv
        ckvn = _mx(ckvh * g_kva[...])
        kpe = proj[:, C_KPE:C_QS]
        ss_pe = _sumsq(kpe)
        kv = _dot_nt(ckvn, w_kvb[...])
        dkpe = jnp.zeros((tb, LANES), jnp.float32)
        for h in range(MLA_HEADS):
            c0 = h * (MLA_NOPE + MLA_V)
            c1 = c0 + MLA_NOPE
            kn = kv[:, c0:c1]
            rh = _rsq(_sumsq(kn) + ss_pe, MLA_QK)
            xh_n, xh_r = kn * rh, kpe * rh
            dy_n = dka_ref[h, :, 0:MLA_NOPE]
            dy_r = _rope_bwd(dka_ref[h, :, MLA_NOPE:MLA_QK_PAD], cos_v, sin_v)
            grads["gk_n"][...] += _colsum(dy_n * xh_n)
            grads["gk_r"][...] += _colsum(dy_r * xh_r)
            dkn, dkr = _norm_bwd([dy_n * gk_n[...], dy_r * gk_r[...]], [xh_n, xh_r], rh, MLA_QK)
            dkpe = dkpe + dkr
            dkv_ref[:, c0:c1] = _mx(dkn)
            dkv_ref[:, c1:c1 + MLA_V] = _mx(dva_ref[h])
        dkv = dkv_ref[...]
        grads["w_kvb"][...] += _dot_tn(dkv, ckvn)
        dckvn = _dot(dkv, w_kvb[...])
        grads["g_kva"][...] += _colsum(dckvn * ckvh)
        (dckv,) = _norm_bwd([dckvn * g_kva[...]], [ckvh], rkv, MLA_KV_RANK)
        dproj_ref[:, C_CKV:C_KPE] = _mx(dckv)
        dproj_ref[:, C_KPE:C_QS] = _mx(dkpe)

        def pair_bwd(tv, dy, g_ref, gname):
            r = _rsq(_half_sums(tv * tv), SWA_D)
            xh = tv * r
            dpre = _rope_bwd(dy, cos_v, sin_v)
            grads[gname][...] += _colsum(dpre * xh)
            dn = dpre * g_ref[...]
            return r * (dn - xh * (_half_sums(dn * xh) * (1.0 / SWA_D)))

        for j in range(SWA_PAIRS):
            c0 = C_QS + j * LANES
            dproj_ref[:, c0:c0 + LANES] = _mx(pair_bwd(proj[:, c0:c0 + LANES], dqb_ref[j], g_sq, "g_sq"))
        dproj_ref[:, C_KS:C_VS] = _mx(pair_bwd(proj[:, C_KS:C_VS], _undup_halves(dkb_ref[0], dkb_ref[1]), g_sk, "g_sk"))
        dproj_ref[:, C_VS:W_IN_PACKED] = _mx(_undup_halves(dvb_ref[0], dvb_ref[1]))

        dproj = dproj_ref[...]
        grads["w_in"][...] += _dot_tn(dproj, hb)
        dh = _dot(dproj, w_in[...])
        grads["mix_g"][...] += _colsum(dh * xhat)
        (dxv,) = _norm_bwd([dh * mix_g[...]], [xhat], r0, d)
        dx_ref[...] = dxin_ref[...] + dxv

    grad_shapes = [p[n].shape for n in PREP_WEIGHTS]
    args = (x, dxin, cos, sin_s, *[p[n] for n in PREP_WEIGHTS], dqa, dka, dva, dqb, dkb, dvb)
    return _call(
        body, args, sides, name="prep_bwd", grid=(t // tb,),
        in_specs=[_rows(tb, d), _rows(tb, d), _rows(tb, LANES), _rows(tb, LANES)] + _prep_specs(p) + [
            _heads_rows(MLA_HEADS, tb, MLA_QK_PAD), _heads_rows(MLA_HEADS, tb, MLA_QK_PAD),
            _heads_rows(MLA_HEADS, tb, MLA_V), _heads_rows(SWA_PAIRS, tb, LANES),
            _heads_rows(SWA_KV, tb, LANES), _heads_rows(SWA_KV, tb, LANES)],
        out_specs=[_rows(tb, d)] + [_acc(s) for s in grad_shapes],
        out_shape=[_sds((t, d), jnp.float32)] + [_sds(s, jnp.float32) for s in grad_shapes],
        scratch_shapes=[pltpu.VMEM((tb, W_IN_PACKED), _MXU), pltpu.VMEM((tb, MLA_HEADS * (MLA_NOPE + MLA_V)), _MXU),
                        pltpu.VMEM((tb, MLA_QK_PAD), _MXU)],
        semantics=("arbitrary",))


def _strips(n):
    step = min(_STRIP, n)
    return [slice(r, r + step) for r in range(0, n, step)]


def _mla_fwd(q, k, v, sides=()):
    hn, t, dq = q.shape
    dv = v.shape[2]
    bq = min(_BQ, t)
    scale = MLA_QK ** -0.5
    scale2 = scale * LOG2_E

    def body(q_ref, k_ref, v_ref, o_ref, l_ref):
        i = pl.program_id(1)
        qv = q_ref[0]

        def step(first_block, width, carry, masked):
            m, l, acc = carry
            start = pl.multiple_of(first_block * bq, bq)
            s = _dot_nt(qv, k_ref[0, pl.ds(start, width), :])
            if masked:
                row = lax.broadcasted_iota(jnp.int32, (bq, width), 0)
                col = lax.broadcasted_iota(jnp.int32, (bq, width), 1)
                s = jnp.where(col <= row, s, NEG)
            m_new = jnp.maximum(m, _rowmax(s))
            alpha = jnp.exp2((m - m_new) * scale2)
            pv = jnp.exp2((s - m_new) * scale2)
            l = alpha * l + _rowsum(pv)
            acc = alpha * acc + _dot(_mx(pv), v_ref[0, pl.ds(start, width), :])
            return m_new, l, acc

        init = (jnp.full((bq, 1), NEG, jnp.float32), jnp.zeros((bq, 1), jnp.float32), jnp.zeros((bq, dv), jnp.float32))
        carry, done = init, 0
        for group in (4, 2, 1):
            count = (i - done) // group
            carry = lax.fori_loop(0, count, lambda g, c, done=done, group=group: step(done + group * g, group * bq, c, False), carry)
            done = done + group * count
        m, l, acc = step(i, bq, carry, True)
        o_ref[0] = acc / l
        l_ref[0] = jnp.broadcast_to(m * scale + jnp.log(l), (bq, LANES))

    return _call(
        body, (q, k, v), sides, name="mla_fwd", grid=(hn, t // bq),
        in_specs=[pl.BlockSpec((1, bq, dq), lambda h, i: (h, i, 0)),
                  pl.BlockSpec((1, t, dq), lambda h, i: (h, 0, 0)),
                  pl.BlockSpec((1, t, dv), lambda h, i: (h, 0, 0))],
        out_specs=[pl.BlockSpec((1, bq, dv), lambda h, i: (h, i, 0)),
                   pl.BlockSpec((1, bq, LANES), lambda h, i: (h, i, 0))],
        out_shape=[_sds((hn, t, dv), jnp.float32), _sds((hn, t, LANES), jnp.float32)],
        semantics=("parallel", "arbitrary"))


def _mla_bwd(q, k, v, do, lse_rows, dsum_rows, sides=()):
    hn, t, dq_w = q.shape
    dv_w = v.shape[2]
    bq = min(_BQ, t)
    nb = t // bq
    wide = max(_BWD_GROUPS) * bq
    scale = MLA_QK ** -0.5

    def body(q_ref, do_ref, l_ref, d_ref, k_ref, v_ref, dq_ref, dk_ref, dv_ref, st_scr, dpt_scr, p_scr, ds_scr):
        j = pl.program_id(1)

        @pl.when(j == 0)
        def _():
            dq_ref[...] = jnp.zeros_like(dq_ref)

        kv = k_ref[0]
        vv = v_ref[0]
        dk_ref[0] = jnp.zeros((bq, dq_w), jnp.float32)
        dv_ref[0] = jnp.zeros((bq, dv_w), jnp.float32)

        def tile(first_block, n_blk, masked):
            width = n_blk * bq
            start = pl.multiple_of(first_block * bq, bq)
            qv = q_ref[0, pl.ds(start, width), :]
            dov = do_ref[0, pl.ds(start, width), :]
            st_scr[:, :width] = _dot_nt(kv, qv)
            dpt_scr[:, :width] = _dot_nt(vv, dov)
            lse2 = jnp.concatenate([l_ref[0, first_block + b] for b in range(n_blk)], axis=1) * LOG2_E
            dsum = jnp.concatenate([d_ref[0, first_block + b] for b in range(n_blk)], axis=1)
            for rows in _strips(bq):
                pt = jnp.exp2(st_scr[rows, :width] * (scale * LOG2_E) - lse2)
                if masked:
                    n_rows = rows.stop - rows.start
                    row = lax.broadcasted_iota(jnp.int32, (n_rows, width), 0) + rows.start
                    col = lax.broadcasted_iota(jnp.int32, (n_rows, width), 1)
                    pt = jnp.where(row <= col, pt, 0.0)
                p_scr[rows, :width] = _mx(pt)
                ds_scr[rows, :width] = _mx(pt * (dpt_scr[rows, :width] - dsum) * scale)
            ds_t = ds_scr[:, :width]
            dv_ref[0] += _dot(p_scr[:, :width], dov)
            dk_ref[0] += _dot(ds_t, qv)
            dq_ref[0, pl.ds(start, width), :] += _dot_tn(ds_t, kv)

        def group_loop(first_block, n_blk, count):
            def loop_body(g, carry):
                tile(first_block + n_blk * g, n_blk, False)
                return carry

            lax.fori_loop(0, count, loop_body, 0)

        tile(j, 1, True)
        done = j + 1
        for n_blk in _BWD_GROUPS:
            count = (nb - done) // n_blk
            group_loop(done, n_blk, count)
            done = done + n_blk * count

    return _call(
        body, (q, do, lse_rows, dsum_rows, k, v), sides, name="mla_bwd", grid=(hn, nb),
        in_specs=[pl.BlockSpec((1, t, dq_w), lambda h, j: (h, 0, 0)),
                  pl.BlockSpec((1, t, dv_w), lambda h, j: (h, 0, 0)),
                  pl.BlockSpec((1, nb, 1, bq), lambda h, j: (h, 0, 0, 0)),
                  pl.BlockSpec((1, nb, 1, bq), lambda h, j: (h, 0, 0, 0)),
                  pl.BlockSpec((1, bq, dq_w), lambda h, j: (h, j, 0)),
                  pl.BlockSpec((1, bq, dv_w), lambda h, j: (h, j, 0))],
        out_specs=[pl.BlockSpec((1, t, dq_w), lambda h, j: (h, 0, 0)),
                   pl.BlockSpec((1, bq, dq_w), lambda h, j: (h, j, 0)),
                   pl.BlockSpec((1, bq, dv_w), lambda h, j: (h, j, 0))],
        out_shape=[_sds((hn, t, dq_w), jnp.float32), _sds((hn, t, dq_w), jnp.float32), _sds((hn, t, dv_w), jnp.float32)],
        scratch_shapes=[pltpu.VMEM((bq, wide), jnp.float32), pltpu.VMEM((bq, wide), jnp.float32),
                        pltpu.VMEM((bq, wide), _MXU), pltpu.VMEM((bq, wide), _MXU)],
        semantics=("parallel", "arbitrary"))


STACK = SWA_GROUP * SWA_BLOCK


def _swa_stack(ref, c, rows):
    low = _low_half()
    parts = []
    for g in range(SWA_GROUP):
        tv = ref[SWA_GROUP // 2 * c + g // 2, rows, :]
        keep = low if g % 2 == 0 else jnp.logical_not(low)
        parts.append(_mx(jnp.where(keep, tv, jnp.zeros_like(tv))))
    return jnp.concatenate(parts, axis=0)


def _swa_cols(ref, c, rows):
    return jnp.concatenate([ref[SWA_GROUP * c + g, rows, 0:1] for g in range(SWA_GROUP)], axis=0)


def _swa_sink_col(s_ref, c):
    return jnp.concatenate([jnp.broadcast_to(s_ref[SWA_GROUP * c + g][:, 0:1], (SWA_BLOCK, 1))
                            for g in range(SWA_GROUP)], axis=0)


def _swa_band_masks():
    row = lax.broadcasted_iota(jnp.int32, (STACK, SWA_BLOCK), 0) & (SWA_BLOCK - 1)
    col = lax.broadcasted_iota(jnp.int32, (STACK, SWA_BLOCK), 1)
    return col <= row, col > row


def _swa_band(has_previous):
    row = lax.broadcasted_iota(jnp.int32, (STACK, 2 * SWA_BLOCK), 0) & (SWA_BLOCK - 1)
    col = lax.broadcasted_iota(jnp.int32, (STACK, 2 * SWA_BLOCK), 1)
    before = jnp.logical_and(col < SWA_BLOCK, col > row)
    if has_previous is not True:
        before = jnp.logical_and(before, has_previous)
    return jnp.logical_or(before, jnp.logical_and(col >= SWA_BLOCK, col - SWA_BLOCK <= row))


def _swa_keys(ref, prev_ref, c, b):
    if b == 0:
        return jnp.concatenate([prev_ref[c], ref[c, 0:SWA_BLOCK, :]], axis=0)
    return ref[c, (b - 1) * SWA_BLOCK:(b + 1) * SWA_BLOCK, :]


def _swa_unstack_pairs(ref, c, rows, stacked):
    for pr in range(SWA_GROUP // 2):
        r0 = 2 * pr * SWA_BLOCK
        ref[SWA_GROUP // 2 * c + pr, rows, :] = _pick_halves(stacked[r0:r0 + SWA_BLOCK], stacked[r0 + SWA_BLOCK:r0 + 2 * SWA_BLOCK])


def _swa_blocks(t):
    nblk = t // SWA_BLOCK
    bps = min(_SWA_STEP, nblk)
    return nblk, bps, bps * SWA_BLOCK


def _swa_fwd(q, k, v, sinks):
    _, t, _ = q.shape
    nblk, bps, sb = _swa_blocks(t)
    scale = SWA_D ** -0.5

    def body(q_ref, k_ref, kp_ref, v_ref, vp_ref, s_ref, o_ref, l_ref):
        n = pl.program_id(0)
        band_first, band = _swa_band(n > 0), _swa_band(True)
        for c in range(SWA_KV):
            sink = _swa_sink_col(s_ref, c)
            for b in range(bps):
                rows = slice(b * SWA_BLOCK, (b + 1) * SWA_BLOCK)
                qs = _swa_stack(q_ref, c, rows)
                s = jnp.where(band_first if b == 0 else band, _dot_nt(qs, _swa_keys(k_ref, kp_ref, c, b)) * scale, NEG)
                m = jnp.maximum(_rowmax(s), sink)
                e = jnp.exp(s - m)
                denom = _rowsum(e) + jnp.exp(sink - m)
                o = _dot(_mx(e * (1.0 / denom)), _swa_keys(v_ref, vp_ref, c, b))
                lse = m + jnp.log(denom)
                for g in range(SWA_GROUP):
                    l_ref[SWA_GROUP * c + g, rows, :] = jnp.broadcast_to(
                        lse[g * SWA_BLOCK:(g + 1) * SWA_BLOCK], (SWA_BLOCK, LANES))
                _swa_unstack_pairs(o_ref, c, rows, o)

    main = lambda n: (0, n, 0)
    prev = lambda n: (0, jnp.maximum(n * bps - 1, 0), 0)
    return pl.pallas_call(
        body, name="swa_fwd", grid=(nblk // bps,),
        in_specs=[pl.BlockSpec((SWA_PAIRS, sb, LANES), main),
                  pl.BlockSpec((SWA_KV, sb, LANES), main), pl.BlockSpec((SWA_KV, SWA_BLOCK, LANES), prev),
                  pl.BlockSpec((SWA_KV, sb, LANES), main), pl.BlockSpec((SWA_KV, SWA_BLOCK, LANES), prev),
                  _const((SWA_HEADS, 1, LANES))],
        out_specs=[pl.BlockSpec((SWA_PAIRS, sb, LANES), main), pl.BlockSpec((SWA_HEADS, sb, LANES), main)],
        out_shape=[_sds((SWA_PAIRS, t, LANES), jnp.float32), _sds((SWA_HEADS, t, LANES), jnp.float32)],
        compiler_params=_cparams(("parallel",)),
    )(q, k, k, v, v, sinks)


def _swa_bwd(q, k, v, sinks, do, lse, dsum):
    _, t, _ = q.shape
    nblk, bps, sb = _swa_blocks(t)
    steps = nblk // bps
    scale = SWA_D ** -0.5

    def body(q_ref, k_ref, kp_ref, v_ref, vp_ref, s_ref, do_ref, l_ref, d_ref, qn_ref, don_ref, ln_ref, dn_ref,
             dq_ref, dk_ref, dv_ref, ds_ref):
        n = pl.program_id(0)

        @pl.when(n == 0)
        def _():
            ds_ref[...] = jnp.zeros_like(ds_ref)

        _, m_prev = _swa_band_masks()
        band_first, band = _swa_band(n > 0), _swa_band(True)
        everything = slice(0, SWA_BLOCK)

        def probs(qs, keys, mask, lcol):
            return jnp.where(mask, jnp.exp(_dot_nt(qs, keys) * scale - lcol), 0.0)

        def dscores(pm, dos, vals, dcol):
            return _mx(pm * (_dot_nt(dos, vals) - dcol) * scale)

        for c in range(SWA_KV):
            sink = _swa_sink_col(s_ref, c)
            dk_acc = [jnp.zeros((SWA_BLOCK, LANES), jnp.float32) for _ in range(bps)]
            dv_acc = [jnp.zeros((SWA_BLOCK, LANES), jnp.float32) for _ in range(bps)]
            for b in range(bps):
                rows = slice(b * SWA_BLOCK, (b + 1) * SWA_BLOCK)
                keys, vals = _swa_keys(k_ref, kp_ref, c, b), _swa_keys(v_ref, vp_ref, c, b)
                qs = _swa_stack(q_ref, c, rows)
                dos = _swa_stack(do_ref, c, rows)
                lcol = _swa_cols(l_ref, c, rows)
                dcol = _swa_cols(d_ref, c, rows)
                pm = probs(qs, keys, band_first if b == 0 else band, lcol)
                ds = dscores(pm, dos, vals, dcol)
                _swa_unstack_pairs(dq_ref, c, rows, _dot(ds, keys))
                dk_both = _dot_tn(ds, qs)
                dv_both = _dot_tn(_mx(pm), dos)
                dk_acc[b] = dk_acc[b] + dk_both[SWA_BLOCK:]
                dv_acc[b] = dv_acc[b] + dv_both[SWA_BLOCK:]
                if b > 0:
                    dk_acc[b - 1] = dk_acc[b - 1] + dk_both[:SWA_BLOCK]
                    dv_acc[b - 1] = dv_acc[b - 1] + dv_both[:SWA_BLOCK]
                p_sink = jnp.exp(sink - lcol) * dcol
                for g in range(SWA_GROUP):
                    ds_ref[SWA_GROUP * c + g] += -jnp.sum(p_sink[g * SWA_BLOCK:(g + 1) * SWA_BLOCK])
            tail = slice((bps - 1) * SWA_BLOCK, bps * SWA_BLOCK)
            kc, vc = k_ref[c, tail, :], v_ref[c, tail, :]
            qs = _swa_stack(qn_ref, c, everything)
            dos = _swa_stack(don_ref, c, everything)
            lcol = _swa_cols(ln_ref, c, everything)
            dcol = _swa_cols(dn_ref, c, everything)
            p_p = probs(qs, kc, jnp.logical_and(m_prev, n < steps - 1), lcol)
            ds_p = dscores(p_p, dos, vc, dcol)
            dk_acc[bps - 1] = dk_acc[bps - 1] + _dot_tn(ds_p, qs)
            dv_acc[bps - 1] = dv_acc[bps - 1] + _dot_tn(_mx(p_p), dos)
            for b in range(bps):
                rows = slice(b * SWA_BLOCK, (b + 1) * SWA_BLOCK)
                dk_ref[c, rows, :] = dk_acc[b]
                dv_ref[c, rows, :] = dv_acc[b]

    main = lambda n: (0, n, 0)
    prev = lambda n: (0, jnp.maximum(n * bps - 1, 0), 0)
    nxt = lambda n: (0, jnp.minimum((n + 1) * bps, nblk - 1), 0)
    pairs = pl.BlockSpec((SWA_PAIRS, sb, LANES), main)
    kvs = pl.BlockSpec((SWA_KV, sb, LANES), main)
    kv_prev = pl.BlockSpec((SWA_KV, SWA_BLOCK, LANES), prev)
    stats = pl.BlockSpec((SWA_HEADS, sb, LANES), main)
    pairs_next = pl.BlockSpec((SWA_PAIRS, SWA_BLOCK, LANES), nxt)
    stats_next = pl.BlockSpec((SWA_HEADS, SWA_BLOCK, LANES), nxt)
    return pl.pallas_call(
        body, name="swa_bwd", grid=(steps,),
        in_specs=[pairs, kvs, kv_prev, kvs, kv_prev, _const((SWA_HEADS, 1, LANES)), pairs, stats, stats,
                  pairs_next, pairs_next, stats_next, stats_next],
        out_specs=[pairs, kvs, kvs, _acc((SWA_HEADS, 1, LANES))],
        out_shape=[_sds((SWA_PAIRS, t, LANES), jnp.float32), _sds((SWA_KV, t, LANES), jnp.float32),
                   _sds((SWA_KV, t, LANES), jnp.float32), _sds((SWA_HEADS, 1, LANES), jnp.float32)],
        compiler_params=_cparams(("arbitrary",)),
    )(q, k, k, v, v, sinks, do, lse, dsum, q, do, lse, dsum)


MIX_SLABS = 4
MIX_WIDTH = MIX_SLABS * LANES


def _mix_out_fwd(x, oa, ob, ga, gb, wo_a, wo_b):
    t, d = x.shape
    tb = min(_TB, t)

    def body(x_ref, oa_ref, ob_ref, ga_ref, gb_ref, woa_ref, wob_ref, xo_ref):
        y = x_ref[...]
        for o_ref, g_ref, w_ref in ((oa_ref, ga_ref, woa_ref), (ob_ref, gb_ref, wob_ref)):
            r = _rsq(sum(_sumsq(o_ref[h]) for h in range(MIX_SLABS)), MIX_WIDTH)
            for h in range(MIX_SLABS):
                y = y + _dot(_mx(o_ref[h] * r * g_ref[h]), w_ref[h])
        xo_ref[...] = y

    slab = _heads_rows(MIX_SLABS, tb, LANES)
    return pl.pallas_call(
        body, name="mix_out_fwd", grid=(t // tb,),
        in_specs=[_rows(tb, d), slab, slab, _const(ga.shape), _const(gb.shape), _const(wo_a.shape), _const(wo_b.shape)],
        out_specs=_rows(tb, d),
        out_shape=_sds((t, d), jnp.float32),
        compiler_params=_cparams(("parallel",)),
    )(x, oa, ob, ga, gb, wo_a, wo_b)


def _mix_out_bwd(dx, oa, ob, ga, gb, wo_a, wo_b):
    t, d = dx.shape
    tb = min(_TB, t)

    def group(o_ref, g_ref, w_ref, dyb, do_ref, n_ref, col0, dg_ref):
        r = _rsq(sum(_sumsq(o_ref[h]) for h in range(MIX_SLABS)), MIX_WIDTH)
        xh, dn = [], []
        for h in range(MIX_SLABS):
            xh.append(o_ref[h] * r)
            n_ref[:, col0 + h * LANES:col0 + (h + 1) * LANES] = _mx(xh[h] * g_ref[h])
            dm = _dot_nt(dyb, w_ref[h])
            dg_ref[h] += _colsum(dm * xh[h])
            dn.append(dm * g_ref[h])
        c = sum(_rowsum(dn[h] * xh[h]) for h in range(MIX_SLABS)) * (1.0 / MIX_WIDTH)
        prods = []
        for h in range(MIX_SLABS):
            do = r * (dn[h] - xh[h] * c)
            do_ref[h] = do.astype(do_ref.dtype)
            prods.append(do * o_ref[h])
        return prods

    def body(dx_ref, oa_ref, ob_ref, ga_ref, gb_ref, woa_ref, wob_ref,
             doa_ref, dsa_ref, dob_ref, dsb_ref, n_ref, dy_ref, dga_ref, dgb_ref):
        @pl.when(pl.program_id(0) == 0)
        def _():
            dga_ref[...] = jnp.zeros_like(dga_ref)
            dgb_ref[...] = jnp.zeros_like(dgb_ref)

        dyb = _mx(dx_ref[...])
        dy_ref[...] = dyb
        for h, pr in enumerate(group(oa_ref, ga_ref, woa_ref, dyb, doa_ref, n_ref, 0, dga_ref)):
            dsa_ref[h] = jnp.broadcast_to(_rowsum(pr), (tb, LANES))
        low = _low_half()
        for j, pr in enumerate(group(ob_ref, gb_ref, wob_ref, dyb, dob_ref, n_ref, MIX_WIDTH, dgb_ref)):
            dsb_ref[2 * j] = jnp.broadcast_to(_rowsum(jnp.where(low, pr, 0.0)), (tb, LANES))
            dsb_ref[2 * j + 1] = jnp.broadcast_to(_rowsum(jnp.where(low, 0.0, pr)), (tb, LANES))

    slab = _heads_rows(MIX_SLABS, tb, LANES)
    return pl.pallas_call(
        body, name="mix_out_bwd", grid=(t // tb,),
        in_specs=[_rows(tb, d), slab, slab, _const(ga.shape), _const(gb.shape), _const(wo_a.shape), _const(wo_b.shape)],
        out_specs=[slab, slab, slab, _heads_rows(SWA_HEADS, tb, LANES), _rows(tb, 2 * MIX_WIDTH), _rows(tb, d),
                   _acc(ga.shape), _acc(gb.shape)],
        out_shape=[_sds((MIX_SLABS, t, LANES), _MXU), _sds((MIX_SLABS, t, LANES), jnp.float32),
                   _sds((MIX_SLABS, t, LANES), jnp.float32), _sds((SWA_HEADS, t, LANES), jnp.float32),
                   _sds((t, 2 * MIX_WIDTH), _MXU), _sds((t, d), _MXU),
                   _sds(ga.shape, jnp.float32), _sds(gb.shape, jnp.float32)],
        compiler_params=_cparams(("arbitrary",)),
    )(dx, oa, ob, ga, gb, wo_a, wo_b)


def _loss_head(y, target):
    t, d = y.shape
    tb = min(_TB, t)

    def body(y_ref, t_ref, dy_ref, acc_ref):
        @pl.when(pl.program_id(0) == 0)
        def _():
            acc_ref[...] = jnp.zeros_like(acc_ref)

        err = y_ref[...] - t_ref[...]
        dy_ref[...] = err * (1.0 / d)
        acc_ref[...] += jnp.sum(err * err)

    return pl.pallas_call(
        body, name="loss_head", grid=(t // tb,),
        in_specs=[_rows(tb, d), _rows(tb, d)],
        out_specs=[_rows(tb, d), _acc((8, LANES))],
        out_shape=[_sds((t, d), jnp.float32), _sds((8, LANES), jnp.float32)],
        compiler_params=_cparams(("arbitrary",)),
    )(y, target)


def _is_transposed(name):
    return name not in ROW_SHARDED


def _pack_layer(shards, l, width, names):
    rows = [(shards[n][l].T if _is_transposed(n) else shards[n][l]).reshape(-1, width) for n in names]
    return jnp.concatenate(rows, axis=0)


def _full_shape(like, name):
    _, a, b = like[name].shape
    return (N_DEV * b, a) if _is_transposed(name) else (N_DEV * a, b)


def _unpack_full(gathered, like, names):
    out, off = {}, 0
    for n in names:
        rows_n = like[n][0].size // gathered.shape[-1]
        out[n] = gathered[:, off:off + rows_n].reshape(_full_shape(like, n))
        off += rows_n
    return out


def _stored(a, name):
    return jnp.swapaxes(a, 1, 2) if _is_transposed(name) else a


def _grads_by_destination(grads, width, names):
    by_dest = lambda n: grads[n].reshape(N_DEV, -1, width)
    if names is OTHER_BIG:
        return [jnp.concatenate([by_dest(n) for n in names], axis=1)]
    return [by_dest(n) for n in names]


def _shards_from_rows(rows, like):
    out = dict(zip(FFN_BIG, rows[:len(FFN_BIG)]))
    rest, off = rows[len(FFN_BIG)], 0
    for n in OTHER_BIG:
        _, a, b = like[n].shape
        rows_n = a * b // rest.shape[-1]
        out[n] = rest[off:off + rows_n].reshape((b, a) if _is_transposed(n) else (a, b))
        off += rows_n
    return out


def _small_rows(n_elems):
    return -(-n_elems // LANES)


def _pack_small(arrays):
    parts = []
    for n in SMALL:
        v = arrays[n]
        depth, width = v.shape
        padded = _small_rows(width) * LANES
        parts.append(jnp.pad(v, ((0, 0), (0, padded - width))).reshape(-1, LANES))
    packed = jnp.concatenate(parts, axis=0)
    return jnp.pad(packed, ((0, (-packed.shape[0]) % 8), (0, 0)))


def _unpack_small(packed, like):
    out, off = {}, 0
    for n in SMALL:
        depth, width = like[n].shape
        rows_n = _small_rows(width)
        seg = packed[off:off + depth * rows_n].reshape(depth, rows_n * LANES)
        out[n] = seg[:, :width]
        off += depth * rows_n
    return out


def _rope_tables(t):
    pos = jnp.arange(t, dtype=jnp.float32)
    inv = 1.0 / (ROPE_THETA ** (jnp.arange(0, MLA_ROPE, 2, dtype=jnp.float32) / MLA_ROPE))
    ang = pos[:, None] * inv[None, :]
    cos, sin = jnp.cos(ang), jnp.sin(ang)
    return jnp.concatenate([cos, cos, cos, cos], axis=1), jnp.concatenate([-sin, sin, -sin, sin], axis=1)


def _pad_lanes(a, width):
    return jnp.pad(a, [(0, 0)] * (a.ndim - 1) + [(0, width - a.shape[-1])])


def _ffn_params(full, small, l, tag):
    return small[tag + "_norm"][l][None, :], full[tag + "_w_gate"], full[tag + "_w_up"], full[tag + "_w_down"]


def _mixer_params(full, small, l):
    w_in = full["w_in"]
    d = w_in.shape[1]
    mla_rows = W_IN_COLS[0]
    w_in_p = jnp.concatenate([w_in[:mla_rows], jnp.zeros((LANES - MLA_ROPE, d), w_in.dtype), w_in[mla_rows:]], axis=0)
    wqb = full["mla_w_q_b"].reshape(MLA_HEADS, MLA_QK, MLA_Q_RANK)
    wqb = jnp.pad(wqb, ((0, 0), (0, MLA_QK_PAD - MLA_QK), (0, 0)))
    row = lambda name: small[name][l][None, :]
    twice = lambda g: jnp.concatenate([g, g], axis=1)
    prep = {
        "mix_g": row("mix_norm"), "w_in": w_in_p,
        "g_qa": row("mla_q_a_norm"), "wqb": wqb,
        "g_kva": row("mla_kv_a_norm"), "w_kvb": full["mla_w_kv_b"],
        "gq_n": row("mla_q_norm")[:, :MLA_NOPE], "gq_r": _pad_lanes(row("mla_q_norm")[:, MLA_NOPE:], LANES),
        "gk_n": row("mla_k_norm")[:, :MLA_NOPE], "gk_r": _pad_lanes(row("mla_k_norm")[:, MLA_NOPE:], LANES),
        "g_sq": twice(row("swa_q_norm")), "g_sk": twice(row("swa_k_norm")),
    }
    return {
        "prep": prep,
        "sinks": jnp.broadcast_to(small["swa_sinks"][l][:, None, None], (SWA_HEADS, 1, LANES)),
        "ga": small["mla_out_norm"][l].reshape(MIX_SLABS, 1, LANES),
        "gb": small["swa_out_norm"][l].reshape(MIX_SLABS, 1, LANES),
        "wo_a": full["w_o"][:MIX_WIDTH].reshape(MIX_SLABS, LANES, d),
        "wo_b": full["w_o"][MIX_WIDTH:].reshape(MIX_SLABS, LANES, d),
    }


def _ffn_backward(x_in, dxo, kept, params, tag, sides=()):
    gain, wg, wu, wd = params
    h, s, fa, fu = kept
    (dxi, da, du, dy, dg), side_out = _ffn_dgrad(x_in, gain, dxo, fa, fu, wg, wu, wd, sides)
    dwg = _tn_matmul(da, h, "wgrad_" + tag + "_gate")
    dwu = _tn_matmul(du, h, "wgrad_" + tag + "_up")
    dwd = _tn_matmul(s, dy, "wgrad_" + tag + "_down")
    return dxi, dg[0], dwg, dwu, dwd, side_out


def kernel(x, ffn1_norm, ffn1_w_gate, ffn1_w_up, ffn1_w_down, mix_norm, w_in, mla_q_a_norm, mla_w_q_b, mla_kv_a_norm, mla_w_kv_b, mla_q_norm, mla_k_norm, swa_q_norm, swa_k_norm, swa_sinks, mla_out_norm, swa_out_norm, w_o, ffn2_norm, ffn2_w_gate, ffn2_w_up, ffn2_w_down, loss_target, m_ffn1_norm, m_ffn1_w_gate, m_ffn1_w_up, m_ffn1_w_down, m_mix_norm, m_w_in, m_mla_q_a_norm, m_mla_w_q_b, m_mla_kv_a_norm, m_mla_w_kv_b, m_mla_q_norm, m_mla_k_norm, m_swa_q_norm, m_swa_k_norm, m_swa_sinks, m_mla_out_norm, m_swa_out_norm, m_w_o, m_ffn2_norm, m_ffn2_w_gate, m_ffn2_w_up, m_ffn2_w_down, v_ffn1_norm, v_ffn1_w_gate, v_ffn1_w_up, v_ffn1_w_down, v_mix_norm, v_w_in, v_mla_q_a_norm, v_mla_w_q_b, v_mla_kv_a_norm, v_mla_w_kv_b, v_mla_q_norm, v_mla_k_norm, v_swa_q_norm, v_swa_k_norm, v_swa_sinks, v_mla_out_norm, v_swa_out_norm, v_w_o, v_ffn2_norm, v_ffn2_w_gate, v_ffn2_w_up, v_ffn2_w_down):
    local = dict(locals())
    w = {n: local[n] for n in WEIGHTS}
    m = {n: local["m_" + n] for n in WEIGHTS}
    v = {n: local["v_" + n] for n in WEIGHTS}
    depth = ffn1_norm.shape[0]
    t, d = x.shape[-2], x.shape[-1]
    x2d = x.reshape(t, d)
    target = loss_target.reshape(t, d)
    bq = min(_BQ, t)

    big = {n: w[n] for n in BIG}
    packed = [[_mx(_pack_layer(big, l, d, names)) for names in GATHER_ORDER] for l in range(depth)]
    cos, sin_s = _rope_tables(t)
    x_i, y_i, c_i = _position()
    dest_idx = jnp.stack([4 * px + 2 * py + c_i for px, py in _relations(x_i, y_i)]).astype(jnp.int32)

    params, saved = [], []
    xc = x2d
    ffn1_full = _unpack_full(_all_gather(packed[0][0]), big, FFN1_BIG)
    for l in range(depth):
        pr = {"ffn1": _ffn_params(ffn1_full, w, l, "ffn1")}
        x0 = xc
        (x1, *kept1), ((mixer_gathered,),) = _ffn_fwd(x0, *pr["ffn1"], sides=[_side_gather(packed[l][1])])
        pr.update(_mixer_params(_unpack_full(mixer_gathered, big, OTHER_BIG), w, l))
        qa, ka, va, qb, kb, vb = _prep_fwd(x1, cos, sin_s, pr["prep"])
        (oa, lse_a), ((ffn2_gathered,),) = _mla_fwd(qa, ka, va, sides=[_side_gather(packed[l][2])])
        pr["ffn2"] = _ffn_params(_unpack_full(ffn2_gathered, big, FFN2_BIG), w, l, "ffn2")
        ob, lse_b = _swa_fwd(qb, kb, vb, pr["sinks"])
        x2 = _mix_out_fwd(x1, oa, ob, pr["ga"], pr["gb"], pr["wo_a"], pr["wo_b"])
        (x3, *kept2), next_gathered = _ffn_fwd(
            x2, *pr["ffn2"], sides=[_side_gather(packed[l + 1][0])] if l + 1 < depth else [])
        if next_gathered:
            ffn1_full = _unpack_full(next_gathered[0][0], big, FFN1_BIG)
        params.append(pr)
        saved.append((x0, kept1, x1, qa, ka, va, qb, kb, vb, oa, lse_a, ob, lse_b, x2, kept2))
        xc = x3

    dx, sq_err = _loss_head(xc, target)
    loss = lax.psum(0.5 / d * sq_err[0, 0], MESH_AXES)

    def chip_sums(arrays, sibling_parts):
        sums = _rs_chip_sums(arrays, sibling_parts, dest_idx)
        return sums[:len(arrays)], sums[len(arrays):]

    grad_shards = [None] * depth
    small_grads = {n: [None] * depth for n in SMALL}
    upper = None
    for l in reversed(range(depth)):
        pr = params[l]
        lowest = l == 0
        x0, kept1, x1, qa, ka, va, qb, kb, vb, oa, lse_a, ob, lse_b, x2, kept2 = saved[l]
        g = {}
        dx, small_grads["ffn2_norm"][l], g["ffn2_w_gate"], g["ffn2_w_up"], g["ffn2_w_down"], side_out = _ffn_backward(
            x2, dx, kept2, pr["ffn2"], "ffn2", [_side_sibling(upper)] if upper else [])
        if upper:
            upper_owns, upper_wires = chip_sums(upper, side_out[0])
        early = _grads_by_destination(g, d, FFN2_BIG) if lowest else None

        doa, dsum_a, dob, dsum_b, mixed, dyb, dga, dgb = _mix_out_bwd(
            dx, oa, ob, pr["ga"], pr["gb"], pr["wo_a"], pr["wo_b"])
        small_grads["mla_out_norm"][l] = dga.reshape(-1)
        small_grads["swa_out_norm"][l] = dgb.reshape(-1)
        g["w_o"] = _tn_matmul(mixed, dyb, "wgrad_wo")

        rows_of = lambda s: s[:, :, 0].reshape(MLA_HEADS, t // bq, 1, bq)
        sides = ([_side_chips(upper_wires)] if upper else []) + ([_side_sibling(early)] if lowest else [])
        (dqa, dka, dva), side_out = _mla_bwd(qa, ka, va, doa, rows_of(lse_a), rows_of(dsum_a), sides)
        if upper:
            grad_shards[l + 1] = _shards_from_rows(_rs_sum(upper_owns, side_out[0]), big)
        if lowest:
            early_owns, early_wires = chip_sums(early, side_out[-1])
        dqb, dkb, dvb, dsinks = _swa_bwd(qb, kb, vb, pr["sinks"], dob, lse_b, dsum_b)
        small_grads["swa_sinks"][l] = dsinks[:, 0, 0]

        outs, side_out = _prep_bwd(x1, dx, cos, sin_s, pr["prep"], dqa, dka, dva, dqb, dkb, dvb,
                                   [_side_chips(early_wires)] if lowest else [])
        if lowest:
            early_rows = _rs_sum(early_owns, side_out[0])
        dx = outs[0]
        pg = dict(zip(PREP_WEIGHTS, outs[1:]))
        g["w_in"] = jnp.concatenate([pg["w_in"][:W_IN_COLS[0]], pg["w_in"][C_QS:]], axis=0)
        g["mla_w_q_b"] = pg["wqb"][:, :MLA_QK].reshape(MLA_HEADS * MLA_QK, MLA_Q_RANK)
        g["mla_w_kv_b"] = pg["w_kvb"]
        fold = lambda gg: gg[0, :HALF] + gg[0, HALF:]
        small_grads["mix_norm"][l] = pg["mix_g"][0]
        small_grads["mla_q_a_norm"][l] = pg["g_qa"][0]
        small_grads["mla_kv_a_norm"][l] = pg["g_kva"][0]
        small_grads["mla_q_norm"][l] = jnp.concatenate([pg["gq_n"][0], pg["gq_r"][0, :MLA_ROPE]])
        small_grads["mla_k_norm"][l] = jnp.concatenate([pg["gk_n"][0], pg["gk_r"][0, :MLA_ROPE]])
        small_grads["swa_q_norm"][l] = fold(pg["g_sq"])
        small_grads["swa_k_norm"][l] = fold(pg["g_sk"])

        dx, small_grads["ffn1_norm"][l], g["ffn1_w_gate"], g["ffn1_w_up"], g["ffn1_w_down"], _ = _ffn_backward(
            x0, dx, kept1, pr["ffn1"], "ffn1")
        late = _grads_by_destination(g, d, FFN1_BIG) + _grads_by_destination(g, d, OTHER_BIG)
        if lowest:
            late_owns, late_wires = chip_sums(late, _rs_sibling_exchange(late))
            late_rows = _rs_sum(late_owns, _rs_chip_exchange(late_wires))
            grad_shards[l] = _shards_from_rows(late_rows[:-1] + early_rows + late_rows[-1:], big)
        else:
            upper = late[:-1] + _grads_by_destination(g, d, FFN2_BIG) + late[-1:]

    grad_big, delta_big, new_m_big, new_v_big = {}, {}, {}, {}
    for n in BIG:
        g_st = jnp.stack([grad_shards[l][n] for l in range(depth)])
        d_st, m_st, v_st = _adamw(g_st, _stored(w[n], n), _stored(m[n], n), _stored(v[n], n), n)
        grad_big[n], delta_big[n], new_m_big[n], new_v_big[n] = (_stored(a, n) for a in (g_st, d_st, m_st, v_st))

    small_partial = _pack_small({n: jnp.stack(small_grads[n]) for n in SMALL})
    g_s = _all_reduce_small(small_partial)
    d_s, m_s, v_s = _adamw_small(g_s, _pack_small(w), _pack_small(m), _pack_small(v))
    grad_small, delta_small, new_m_small, new_v_small = (_unpack_small(a, w) for a in (g_s, d_s, m_s, v_s))

    def ordered(big, small):
        return [big[n] if n in big else small[n] for n in WEIGHTS]

    return (loss, dx.reshape(x.shape), *ordered(grad_big, grad_small), *ordered(delta_big, delta_small),
            *ordered(new_m_big, new_m_small), *ordered(new_v_big, new_v_small))
```

```python
import jax
import jax.numpy as jnp
from jax import lax
from jax.experimental import pallas as pl
from jax.experimental.pallas import tpu as pltpu

N_DEV = 8
EPS = 1e-6
ROPE_THETA = 10000.0
MLA_HEADS = 4
MLA_Q_RANK = 256
MLA_KV_RANK = 128
MLA_NOPE = 128
MLA_ROPE = 64
MLA_V = 128
MLA_QK = MLA_NOPE + MLA_ROPE
MLA_QK_PAD = 256
SWA_HEADS = 8
SWA_KV = 2
SWA_GROUP = SWA_HEADS // SWA_KV
SWA_D = 64
SWA_BLOCK = 128
ADAM_LR = 0.001
ADAM_B1 = 0.9
ADAM_B2 = 0.999
ADAM_EPS = 1e-08
ADAM_WD = 0.01
ADAM_STEP = 10

LANES = 128
HALF = LANES // 2
SWA_PAIRS = SWA_HEADS // 2
W_IN_COLS = (MLA_Q_RANK + MLA_KV_RANK + MLA_ROPE, SWA_HEADS * SWA_D + 2 * SWA_KV * SWA_D)
VMEM_LIMIT = 56 * 1024 * 1024

_MXU = jnp.bfloat16
_TB = 256
_TB_MIX = 512
_BQ = 512
_STRIP = 32
_BWD_GROUPS = (4, 2, 1)
_TK = 1024
_SWA_STEP = 4
RS_ROW_BLOCKS = 2

BIG = ("ffn1_w_gate", "ffn1_w_up", "ffn1_w_down", "w_in", "mla_w_q_b", "mla_w_kv_b", "w_o",
       "ffn2_w_gate", "ffn2_w_up", "ffn2_w_down")
ROW_SHARDED = ("ffn1_w_down", "w_o", "ffn2_w_down")
FFN1_BIG = ("ffn1_w_gate", "ffn1_w_up", "ffn1_w_down")
FFN2_BIG = ("ffn2_w_gate", "ffn2_w_up", "ffn2_w_down")
FFN_BIG = FFN1_BIG + FFN2_BIG
OTHER_BIG = ("w_o", "w_in", "mla_w_q_b", "mla_w_kv_b")
GATHER_ORDER = (FFN1_BIG, OTHER_BIG, FFN2_BIG)
SMALL = ("ffn1_norm", "mix_norm", "mla_q_a_norm", "mla_kv_a_norm", "mla_q_norm", "mla_k_norm",
         "swa_q_norm", "swa_k_norm", "swa_sinks", "mla_out_norm", "swa_out_norm", "ffn2_norm")
WEIGHTS = ("ffn1_norm", "ffn1_w_gate", "ffn1_w_up", "ffn1_w_down", "mix_norm", "w_in", "mla_q_a_norm",
           "mla_w_q_b", "mla_kv_a_norm", "mla_w_kv_b", "mla_q_norm", "mla_k_norm", "swa_q_norm",
           "swa_k_norm", "swa_sinks", "mla_out_norm", "swa_out_norm", "w_o", "ffn2_norm",
           "ffn2_w_gate", "ffn2_w_up", "ffn2_w_down")
MESH_AXES = ("x", "y", "c")
MESH = pl.DeviceIdType.MESH
NEG = -1e30
LOG2_E = 1.4426950408889634


def _f32(t):
    return t.astype(jnp.float32)


def _mx(t):
    return t.astype(_MXU)


def _dot(a, b):
    return jnp.dot(a, b, preferred_element_type=jnp.float32)


def _dot_nt(a, b):
    return lax.dot_general(a, b, (((1,), (1,)), ((), ())), preferred_element_type=jnp.float32)


def _dot_tn(a, b):
    return lax.dot_general(a, b, (((0,), (0,)), ((), ())), preferred_element_type=jnp.float32)


def _rsq(ss, n):
    return lax.rsqrt(ss * (1.0 / n) + EPS)


def _sumsq(t):
    return jnp.sum(t * t, axis=-1, keepdims=True)


def _rowsum(t):
    return jnp.sum(t, axis=-1, keepdims=True)


def _rowmax(t):
    return jnp.max(t, axis=-1, keepdims=True)


def _colsum(t):
    return jnp.sum(t, axis=0, keepdims=True)


def _lane():
    return lax.broadcasted_iota(jnp.int32, (1, LANES), 1)


def _low_half():
    return _lane() < HALF


def _swap32(t):
    return jnp.where((_lane() & 32) == 0, pltpu.roll(t, 96, 1), pltpu.roll(t, 32, 1))


def _rope(t, cos, sin_signed):
    return t * cos + _swap32(t) * sin_signed


def _rope_bwd(d, cos, sin_signed):
    return d * cos + _swap32(d * sin_signed)


def _half_sums(t):
    low = _low_half()
    return jnp.where(low, _rowsum(jnp.where(low, t, 0.0)), _rowsum(jnp.where(low, 0.0, t)))


def _dup_halves(pair):
    low = _low_half()
    swapped = pltpu.roll(pair, HALF, 1)
    return jnp.where(low, pair, swapped), jnp.where(low, swapped, pair)


def _undup_halves(d0, d1):
    return jnp.where(_low_half(), d0 + pltpu.roll(d0, HALF, 1), d1 + pltpu.roll(d1, HALF, 1))


def _pick_halves(a, b):
    return jnp.where(_low_half(), a, b)


def _norm_bwd(dn_list, xh_list, r, n):
    c = sum(_rowsum(dn * xh) for dn, xh in zip(dn_list, xh_list)) * (1.0 / n)
    return [r * (dn - xh * c) for dn, xh in zip(dn_list, xh_list)]


def _cparams(semantics):
    return pltpu.CompilerParams(dimension_semantics=semantics, vmem_limit_bytes=VMEM_LIMIT)


def _const(shape):
    nd = len(shape)
    return pl.BlockSpec(shape, lambda *_: (0,) * nd, pipeline_mode=pl.Buffered(1))


def _acc(shape):
    nd = len(shape)
    return pl.BlockSpec(shape, lambda *_: (0,) * nd)


def _rows(tb, width):
    return pl.BlockSpec((tb, width), lambda i: (i, 0))


def _heads_rows(h, tb, width):
    return pl.BlockSpec((h, tb, width), lambda i: (0, i, 0))


def _sds(shape, dtype):
    return jax.ShapeDtypeStruct(shape, dtype)


def _position():
    return lax.axis_index("x"), lax.axis_index("y"), lax.axis_index("c")


def _all_gather(xp):
    def body(x_ref, out_ref, send_sems, recv_sems, local_sem):
        x, y, c = _position()
        me, sibling = (x, y, c), (x, y, 1 - c)
        chips = [(1 - x, y), (x, 1 - y), (1 - x, 1 - y)]

        def rows(px, py, pc):
            return out_ref.at[4 * px + 2 * py + pc]

        def copy(k, block, to, src=None):
            return pltpu.make_async_remote_copy(
                src_ref=rows(*block) if src is None else src, dst_ref=rows(*block),
                send_sem=send_sems.at[k], recv_sem=recv_sems.at[k], device_id=to, device_id_type=MESH)

        mine = pltpu.make_async_copy(x_ref, rows(*me), local_sem)
        mine.start()
        first = [copy(0, me, sibling, src=x_ref)]
        first += [copy(1 + j, me, (*chip, c), src=x_ref) for j, chip in enumerate(chips)]
        for cp in first:
            cp.start()
        passed = [copy(4 + j, (*chip, c), sibling) for j, chip in enumerate(chips)]
        for j, chip in enumerate(chips):
            copy(1 + j, (*chip, c), me).wait_recv()
            passed[j].start()
        copy(0, sibling, me).wait_recv()
        for j, chip in enumerate(chips):
            copy(4 + j, (*chip, 1 - c), me).wait_recv()
        for cp in first + passed:
            cp.wait_send()
        mine.wait()

    return pl.pallas_call(
        body, name="ag_weights",
        out_shape=_sds((N_DEV,) + xp.shape, xp.dtype),
        in_specs=[pl.BlockSpec(memory_space=pl.ANY)],
        out_specs=pl.BlockSpec(memory_space=pl.ANY),
        scratch_shapes=[pltpu.SemaphoreType.DMA((7,)), pltpu.SemaphoreType.DMA((7,)), pltpu.SemaphoreType.DMA],
    )(xp)


def _relations(x, y):
    return [(x, y), (1 - x, y), (x, 1 - y), (1 - x, 1 - y)]


def _remote(src, dst, send_sem, recv_sem, device):
    return pltpu.make_async_remote_copy(src_ref=src, dst_ref=dst, send_sem=send_sem, recv_sem=recv_sem,
                                        device_id=device, device_id_type=MESH)


def _sibling_copies(g_refs, out_refs, send, recv):
    x, y, c = _position()
    n = len(g_refs)
    return [_remote(g.at[4 * px + 2 * py + (1 - c)], o.at[k], send.at[k * n + i], recv.at[k * n + i], (x, y, 1 - c))
            for k, (px, py) in enumerate(_relations(x, y)) for i, (g, o) in enumerate(zip(g_refs, out_refs))]


def _chip_copies(w_refs, out_refs, send, recv):
    x, y, c = _position()
    n = len(w_refs)
    return [_remote(w.at[k + 1], o.at[k], send.at[k * n + i], recv.at[k * n + i], (px, py, c))
            for k, (px, py) in enumerate(_relations(x, y)[1:]) for i, (w, o) in enumerate(zip(w_refs, out_refs))]


def _exchange(arrays, lead, relations, copies_fn, name):
    n = len(arrays)

    def body(*refs):
        copies = copies_fn(refs[:n], refs[n:2 * n], refs[2 * n], refs[2 * n + 1])
        for cp in copies:
            cp.start()
        for cp in copies:
            cp.wait()

    hbm = pl.BlockSpec(memory_space=pl.ANY)
    dma = pltpu.SemaphoreType.DMA
    return pl.pallas_call(
        body, name=name, out_shape=[_sds((lead,) + a.shape[1:], a.dtype) for a in arrays],
        in_specs=[hbm] * n, out_specs=[hbm] * n,
        scratch_shapes=[dma((relations * n,)), dma((relations * n,))],
    )(*arrays)


def _rs_sibling_exchange(gs):
    return _exchange(gs, 4, 4, _sibling_copies, "rs_sibling_exchange")


def _rs_chip_exchange(wires):
    return _exchange(wires, 3, 3, _chip_copies, "rs_chip_exchange")


def _side_exchange(arrays, lead, relations, copies_fn):
    shapes = [_sds((lead,) + a.shape[1:], a.dtype) for a in arrays]
    return list(arrays), shapes, relations * len(arrays), lambda ins, outs, send, recv, local: copies_fn(ins, outs, send, recv)


def _side_sibling(gs):
    return _side_exchange(gs, 4, 4, _sibling_copies)


def _side_chips(wires):
    return _side_exchange(wires, 3, 3, _chip_copies)


def _rs_chip_sums(gs, sibs, dest_idx):
    n = len(gs)

    def body(idx_ref, *refs):
        g_refs, s_refs, own_refs, wire_refs = refs[:n], refs[n:2 * n], refs[2 * n:3 * n], refs[3 * n:]
        totals = [g[0] + s[0] for g, s in zip(g_refs, s_refs)]
        for total, wire in zip(totals, wire_refs):
            wire[0] = total.astype(wire.dtype)

        @pl.when(pl.program_id(1) == 0)
        def _():
            for total, own in zip(totals, own_refs):
                own[...] = total

    def blocks(a, index_map, squeeze):
        rb = a.shape[1] // RS_ROW_BLOCKS
        return pl.BlockSpec((rb, a.shape[2]) if squeeze else (1, rb, a.shape[2]), index_map)

    return pl.pallas_call(
        body, name="rs_chip_sums",
        grid_spec=pltpu.PrefetchScalarGridSpec(
            num_scalar_prefetch=1, grid=(RS_ROW_BLOCKS, 4),
            in_specs=[blocks(g, lambda r, k, idx: (idx[k], r, 0), False) for g in gs]
            + [blocks(g, lambda r, k, idx: (k, r, 0), False) for g in gs],
            out_specs=[blocks(g, lambda r, k, idx: (r, 0), True) for g in gs]
            + [blocks(g, lambda r, k, idx: (k, r, 0), False) for g in gs]),
        out_shape=[_sds(g.shape[1:], jnp.float32) for g in gs] + [_sds((4,) + g.shape[1:], _MXU) for g in gs],
        compiler_params=_cparams(("parallel", "arbitrary")),
    )(dest_idx, *gs, *sibs)


def _side_gather(xp):
    def make(ins, outs, send, recv, local):
        (x_ref,), (out_ref,) = ins, outs
        x, y, c = _position()
        me = 4 * x + 2 * y + c
        copies = [pltpu.make_async_copy(x_ref, out_ref.at[me], local.at[0])]
        for k in range(1, N_DEV):
            peer = (1 - x if k & 4 else x, 1 - y if k & 2 else y, 1 - c if k & 1 else c)
            copies.append(_remote(x_ref, out_ref.at[me], send.at[k - 1], recv.at[k - 1], peer))
        return copies

    return [xp], [_sds((N_DEV,) + xp.shape, xp.dtype)], N_DEV - 1, make


def _call(body, args, sides, *, name, grid, in_specs, out_specs, out_shape, scratch_shapes=(), semantics):
    in_specs, out_specs, out_shape = list(in_specs), list(out_specs), list(out_shape)
    n_in, n_out, n_scr = len(in_specs), len(out_specs), len(scratch_shapes)
    sides = list(sides or [])
    if not sides:
        outs = pl.pallas_call(body, name=name, grid=grid, in_specs=in_specs, out_specs=out_specs, out_shape=out_shape,
                              scratch_shapes=list(scratch_shapes), compiler_params=_cparams(semantics))(*args)
        return list(outs), []
    arrays = [a for side in sides for a in side[0]]
    shapes = [s for side in sides for s in side[1]]
    n_side_in, n_side_out = len(arrays), len(shapes)
    hbm = pl.BlockSpec(memory_space=pl.ANY)

    def with_copies(*refs):
        main_in, refs = refs[:n_in], refs[n_in:]
        side_in, refs = refs[:n_side_in], refs[n_side_in:]
        main_out, refs = refs[:n_out], refs[n_out:]
        side_out, refs = refs[:n_side_out], refs[n_side_out:]
        main_scr, sems = refs[:n_scr], refs[n_scr:]
        copies = []
        for k, (side_arrays, side_shapes, _, make) in enumerate(sides):
            copies += make(side_in[:len(side_arrays)], side_out[:len(side_shapes)], *sems[3 * k:3 * k + 3])
            side_in, side_out = side_in[len(side_arrays):], side_out[len(side_shapes):]
        ids = [pl.program_id(a) for a in range(len(grid))]
        first, last = ids[0] == 0, ids[0] == grid[0] - 1
        for i, size in zip(ids[1:], grid[1:]):
            first, last = jnp.logical_and(first, i == 0), jnp.logical_and(last, i == size - 1)

        @pl.when(first)
        def _():
            for cp in copies:
                cp.start()

        body(*main_in, *main_out, *main_scr)

        @pl.when(last)
        def _():
            for cp in copies:
                cp.wait()

    dma = pltpu.SemaphoreType.DMA
    sem_shapes = [dma((n,)) for side in sides for n in (side[2], side[2], 1)]
    outs = pl.pallas_call(
        with_copies, name=name, grid=grid, in_specs=in_specs + [hbm] * n_side_in,
        out_specs=out_specs + [hbm] * n_side_out, out_shape=out_shape + shapes,
        scratch_shapes=list(scratch_shapes) + sem_shapes,
        compiler_params=_cparams(("arbitrary",) * len(grid)),
    )(*args, *arrays)
    side_outs, rest = [], list(outs[n_out:])
    for side in sides:
        side_outs.append(rest[:len(side[1])])
        rest = rest[len(side[1]):]
    return list(outs[:n_out]), side_outs


def _all_reduce_small(v):
    rows_n = v.shape[0]

    def body(v_ref, out_ref, buf, send_sems, recv_sems):
        x, y, c = _position()
        me = 4 * x + 2 * y + c
        buf[me] = v_ref[...]
        copies = []
        for k in range(1, N_DEV):
            px = 1 - x if k & 4 else x
            py = 1 - y if k & 2 else y
            pc = 1 - c if k & 1 else c
            copies.append(pltpu.make_async_remote_copy(
                src_ref=v_ref, dst_ref=buf.at[me],
                send_sem=send_sems.at[k - 1], recv_sem=recv_sems.at[k - 1], device_id=(px, py, pc), device_id_type=MESH))
        for cp in copies:
            cp.start()
        for cp in copies:
            cp.wait()
        total = buf[0]
        for d in range(1, N_DEV):
            total = total + buf[d]
        out_ref[...] = total

    return pl.pallas_call(
        body, name="ar_small",
        out_shape=_sds((rows_n, LANES), jnp.float32),
        in_specs=[pl.BlockSpec(memory_space=pltpu.VMEM)],
        out_specs=pl.BlockSpec(memory_space=pltpu.VMEM),
        scratch_shapes=[pltpu.VMEM((N_DEV, rows_n, LANES), jnp.float32),
                        pltpu.SemaphoreType.DMA((N_DEV - 1,)), pltpu.SemaphoreType.DMA((N_DEV - 1,))],
    )(v)


def _adamw_math(w, g, m, v):
    m = ADAM_B1 * m + (1.0 - ADAM_B1) * g
    v = ADAM_B2 * v + (1.0 - ADAM_B2) * (g * g)
    m_hat = m / (1.0 - ADAM_B1 ** ADAM_STEP)
    v_hat = v / (1.0 - ADAM_B2 ** ADAM_STEP)
    delta = -ADAM_LR * (m_hat / (jnp.sqrt(v_hat) + ADAM_EPS) + ADAM_WD * w)
    return delta, m, v


def _rs_sum(owns, recvs):
    n = len(owns)

    def body(*refs):
        own_refs, recv_refs, out_refs = refs[:n], refs[n:4 * n], refs[4 * n:]
        for i in range(n):
            r0, r1, r2 = recv_refs[3 * i:3 * i + 3]
            out_refs[i][...] = ((own_refs[i][...] + _f32(r0[0])) + _f32(r1[0])) + _f32(r2[0])

    def row(a):
        return pl.BlockSpec((a.shape[0] // RS_ROW_BLOCKS, a.shape[1]), lambda r: (r, 0))

    def slot(a, k):
        return pl.BlockSpec((1, a.shape[0] // RS_ROW_BLOCKS, a.shape[1]), lambda r: (k, r, 0))

    return pl.pallas_call(
        body, name="rs_sum", grid=(RS_ROW_BLOCKS,),
        in_specs=[row(a) for a in owns] + [slot(a, k) for a in owns for k in range(3)],
        out_specs=[row(a) for a in owns],
        out_shape=[_sds(a.shape, jnp.float32) for a in owns],
        compiler_params=_cparams(("parallel",)),
    )(*owns, *[r for r in recvs for _ in range(3)])


def _adamw(g, w, m, v, name):
    depth, a, b = w.shape

    def body(g_ref, w_ref, m_ref, v_ref, d_out, m_out, v_out):
        delta, m2, v2 = _adamw_math(w_ref[...], g_ref[...], m_ref[...], v_ref[...])
        d_out[...] = delta
        m_out[...] = m2
        v_out[...] = v2

    layer = pl.BlockSpec((1, a, b), lambda l: (l, 0, 0))
    return pl.pallas_call(
        body, name="adamw_" + name, grid=(depth,),
        in_specs=[layer] * 4, out_specs=[layer] * 3,
        out_shape=[_sds(w.shape, jnp.float32)] * 3,
        compiler_params=_cparams(("parallel",)),
    )(g, w, m, v)


def _adamw_small(g, w, m, v):
    def body(g_ref, w_ref, m_ref, v_ref, d_out, m_out, v_out):
        delta, m2, v2 = _adamw_math(w_ref[...], g_ref[...], m_ref[...], v_ref[...])
        d_out[...] = delta
        m_out[...] = m2
        v_out[...] = v2

    vm = pl.BlockSpec(memory_space=pltpu.VMEM)
    return pl.pallas_call(
        body, name="adamw_small",
        in_specs=[vm] * 4, out_specs=[vm] * 3,
        out_shape=[_sds(g.shape, jnp.float32)] * 3,
    )(g, w, m, v)


def _f_chunk(f):
    for cand in (1408, 1024, 512, 256, 128):
        if f % cand == 0:
            return cand
    return f


def _ffn_fwd(x, gain, wg, wu, wd, sides=()):
    t, d = x.shape
    f = wg.shape[0]
    tb = min(_TB, t)
    fc = _f_chunk(f)

    def body(x_ref, g_ref, wg_ref, wu_ref, wd_ref, xo_ref, h_ref, s_ref, fa_ref, fu_ref):
        xv = x_ref[...]
        hb = _mx(xv * _rsq(_sumsq(xv), d) * g_ref[...])
        h_ref[...] = hb
        y = jnp.zeros((tb, d), jnp.float32)
        for c0 in range(0, f, fc):
            a = _dot_nt(hb, wg_ref[c0:c0 + fc, :])
            u = _dot_nt(hb, wu_ref[c0:c0 + fc, :])
            sig = jax.nn.sigmoid(a)
            silu = a * sig
            s = _mx(silu * u)
            s_ref[:, c0:c0 + fc] = s
            fa_ref[:, c0:c0 + fc] = _mx(u * (sig * (1.0 + a * (1.0 - sig))))
            fu_ref[:, c0:c0 + fc] = _mx(silu)
            y = y + _dot(s, wd_ref[c0:c0 + fc, :])
        xo_ref[...] = xv + 0.5 * y

    return _call(
        body, (x, gain, wg, wu, wd), sides, name="ffn_fwd", grid=(t // tb,),
        in_specs=[_rows(tb, d), _const((1, d)), _const((f, d)), _const((f, d)), _const((f, d))],
        out_specs=[_rows(tb, d), _rows(tb, d), _rows(tb, f), _rows(tb, f), _rows(tb, f)],
        out_shape=[_sds((t, d), jnp.float32), _sds((t, d), _MXU), _sds((t, f), _MXU), _sds((t, f), _MXU),
                   _sds((t, f), _MXU)],
        semantics=("parallel",))


def _ffn_dgrad(x, gain, dxo, fa, fu, wg, wu, wd, sides=()):
    t, d = x.shape
    f = wg.shape[0]
    tb = min(_TB, t)
    fc = _f_chunk(f)

    def body(x_ref, g_ref, dxo_ref, fa_ref, fu_ref, wg_ref, wu_ref, wd_ref, dxi_ref, da_ref, du_ref, dy_ref, dg_ref):
        xv = x_ref[...]
        gv = g_ref[...]
        r = _rsq(_sumsq(xv), d)
        xhat = xv * r
        dxo = dxo_ref[...]
        dyb = _mx(0.5 * dxo)
        dy_ref[...] = dyb
        dh = jnp.zeros((tb, d), jnp.float32)
        for c0 in range(0, f, fc):
            ds = _dot_nt(dyb, wd_ref[c0:c0 + fc, :])
            da = _mx(ds * _f32(fa_ref[:, c0:c0 + fc]))
            du = _mx(ds * _f32(fu_ref[:, c0:c0 + fc]))
            da_ref[:, c0:c0 + fc] = da
            du_ref[:, c0:c0 + fc] = du
            dh = dh + _dot(da, wg_ref[c0:c0 + fc, :]) + _dot(du, wu_ref[c0:c0 + fc, :])

        @pl.when(pl.program_id(0) == 0)
        def _():
            dg_ref[...] = jnp.zeros_like(dg_ref)

        dg_ref[...] += _colsum(dh * xhat)
        dn = dh * gv
        dxi_ref[...] = dxo + r * (dn - xhat * (_rowsum(dn * xhat) * (1.0 / d)))

    return _call(
        body, (x, gain, dxo, fa, fu, wg, wu, wd), sides, name="ffn_dgrad", grid=(t // tb,),
        in_specs=[_rows(tb, d), _const((1, d)), _rows(tb, d), _rows(tb, f), _rows(tb, f),
                  _const((f, d)), _const((f, d)), _const((f, d))],
        out_specs=[_rows(tb, d), _rows(tb, f), _rows(tb, f), _rows(tb, d), _acc((1, d))],
        out_shape=[_sds((t, d), jnp.float32), _sds((t, f), _MXU), _sds((t, f), _MXU), _sds((t, d), _MXU),
                   _sds((1, d), jnp.float32)],
        semantics=("arbitrary",))


def _tn_matmul(a, b, name, sides=()):
    t, m = a.shape
    n = b.shape[1]
    tk = min(_TK, t)
    tn = n
    while m * tn * 4 > 12 * 1024 * 1024 and tn % 256 == 0:
        tn //= 2

    def body(a_ref, b_ref, o_ref):
        @pl.when(pl.program_id(1) == 0)
        def _():
            o_ref[...] = jnp.zeros_like(o_ref)

        o_ref[...] += _dot_tn(a_ref[...], b_ref[...])

    (out,), side_outs = _call(
        body, (a, b), sides, name=name, grid=(n // tn, t // tk),
        in_specs=[pl.BlockSpec((tk, m), lambda j, k: (k, 0)), pl.BlockSpec((tk, tn), lambda j, k: (k, j))],
        out_specs=[pl.BlockSpec((m, tn), lambda j, k: (0, j))],
        out_shape=[_sds((m, n), jnp.float32)],
        semantics=("parallel", "arbitrary"))
    return out, side_outs


PREP_WEIGHTS = ("mix_g", "w_in", "g_qa", "wqb", "g_kva", "w_kvb", "gq_n", "gq_r", "gk_n", "gk_r", "g_sq", "g_sk")
C_CQ, C_CKV, C_KPE, C_QS = 0, MLA_Q_RANK, MLA_Q_RANK + MLA_KV_RANK, MLA_Q_RANK + MLA_KV_RANK + LANES
C_KS = C_QS + SWA_HEADS * SWA_D
C_VS = C_KS + LANES
W_IN_PACKED = C_VS + LANES


def _prep_specs(p):
    return [_const(p[n].shape) for n in PREP_WEIGHTS]


def _pair_norm_rope(t, gain, cos, sin_s):
    return _rope(t * _rsq(_half_sums(t * t), SWA_D) * gain, cos, sin_s)


def _prep_fwd(x, cos, sin_s, p):
    t, d = x.shape
    tb = min(_TB_MIX, t)

    def body(x_ref, cos_ref, sin_ref, mix_g, w_in, g_qa, wqb, g_kva, w_kvb, gq_n, gq_r, gk_n, gk_r, g_sq, g_sk,
             qa_ref, ka_ref, va_ref, qb_ref, kb_ref, vb_ref):
        xv = x_ref[...]
        cos_v, sin_v = cos_ref[...], sin_ref[...]
        hb = _mx(xv * _rsq(_sumsq(xv), d) * mix_g[...])
        proj = _dot_nt(hb, w_in[...])
        cq = proj[:, C_CQ:C_CKV]
        cqn = _mx(cq * _rsq(_sumsq(cq), MLA_Q_RANK) * g_qa[...])
        for h in range(MLA_HEADS):
            qh = _dot_nt(cqn, wqb[h])
            qn, qr = qh[:, :MLA_NOPE], qh[:, MLA_NOPE:]
            rh = _rsq(_sumsq(qn) + _sumsq(qr), MLA_QK)
            qa_ref[h, :, 0:MLA_NOPE] = (qn * rh * gq_n[...]).astype(qa_ref.dtype)
            qa_ref[h, :, MLA_NOPE:MLA_QK_PAD] = _rope(qr * rh * gq_r[...], cos_v, sin_v).astype(qa_ref.dtype)
        ckv = proj[:, C_CKV:C_KPE]
        ckvn = _mx(ckv * _rsq(_sumsq(ckv), MLA_KV_RANK) * g_kva[...])
        kpe = proj[:, C_KPE:C_QS]
        ss_pe = _sumsq(kpe)
        kv = _dot_nt(ckvn, w_kvb[...])
        for h in range(MLA_HEADS):
            c0 = h * (MLA_NOPE + MLA_V)
            kn = kv[:, c0:c0 + MLA_NOPE]
            rh = _rsq(_sumsq(kn) + ss_pe, MLA_QK)
            ka_ref[h, :, 0:MLA_NOPE] = (kn * rh * gk_n[...]).astype(ka_ref.dtype)
            ka_ref[h, :, MLA_NOPE:MLA_QK_PAD] = _rope(kpe * rh * gk_r[...], cos_v, sin_v).astype(ka_ref.dtype)
            va_ref[h] = kv[:, c0 + MLA_NOPE:c0 + MLA_NOPE + MLA_V].astype(va_ref.dtype)
        for j in range(SWA_PAIRS):
            c0 = C_QS + j * LANES
            qb_ref[j] = _pair_norm_rope(proj[:, c0:c0 + LANES], g_sq[...], cos_v, sin_v).astype(qb_ref.dtype)
        k0, k1 = _dup_halves(_pair_norm_rope(proj[:, C_KS:C_VS], g_sk[...], cos_v, sin_v))
        kb_ref[0] = k0.astype(kb_ref.dtype)
        kb_ref[1] = k1.astype(kb_ref.dtype)
        v0, v1 = _dup_halves(proj[:, C_VS:W_IN_PACKED])
        vb_ref[0] = v0.astype(vb_ref.dtype)
        vb_ref[1] = v1.astype(vb_ref.dtype)

    return pl.pallas_call(
        body, name="prep_fwd", grid=(t // tb,),
        in_specs=[_rows(tb, d), _rows(tb, LANES), _rows(tb, LANES)] + _prep_specs(p),
        out_specs=[_heads_rows(MLA_HEADS, tb, MLA_QK_PAD), _heads_rows(MLA_HEADS, tb, MLA_QK_PAD),
                   _heads_rows(MLA_HEADS, tb, MLA_V), _heads_rows(SWA_PAIRS, tb, LANES),
                   _heads_rows(SWA_KV, tb, LANES), _heads_rows(SWA_KV, tb, LANES)],
        out_shape=[_sds((MLA_HEADS, t, MLA_QK_PAD), _MXU), _sds((MLA_HEADS, t, MLA_QK_PAD), _MXU),
                   _sds((MLA_HEADS, t, MLA_V), _MXU), _sds((SWA_PAIRS, t, LANES), _MXU),
                   _sds((SWA_KV, t, LANES), _MXU), _sds((SWA_KV, t, LANES), _MXU)],
        compiler_params=_cparams(("parallel",)),
    )(x, cos, sin_s, *[p[n] for n in PREP_WEIGHTS])


def _prep_bwd(x, dxin, cos, sin_s, p, dqa, dka, dva, dqb, dkb, dvb, sides=()):
    t, d = x.shape
    tb = min(_TB_MIX, t)
    n_w = len(PREP_WEIGHTS)

    def body(*refs):
        x_ref, dxin_ref, cos_ref, sin_ref = refs[:4]
        mix_g, w_in, g_qa, wqb, g_kva, w_kvb, gq_n, gq_r, gk_n, gk_r, g_sq, g_sk = refs[4:4 + n_w]
        dqa_ref, dka_ref, dva_ref, dqb_ref, dkb_ref, dvb_ref = refs[4 + n_w:10 + n_w]
        dx_ref = refs[10 + n_w]
        grads = dict(zip(PREP_WEIGHTS, refs[11 + n_w:11 + 2 * n_w]))
        dproj_ref, dkv_ref, dqh_ref = refs[11 + 2 * n_w:]

        @pl.when(pl.program_id(0) == 0)
        def _():
            for ref in grads.values():
                ref[...] = jnp.zeros_like(ref)

        xv = x_ref[...]
        cos_v, sin_v = cos_ref[...], sin_ref[...]
        r0 = _rsq(_sumsq(xv), d)
        xhat = xv * r0
        hb = _mx(xhat * mix_g[...])
        proj = _dot_nt(hb, w_in[...])

        cq = proj[:, C_CQ:C_CKV]
        rq = _rsq(_sumsq(cq), MLA_Q_RANK)
        cqh = cq * rq
        cqn = _mx(cqh * g_qa[...])
        dcqn = jnp.zeros((tb, MLA_Q_RANK), jnp.float32)
        for h in range(MLA_HEADS):
            qh = _dot_nt(cqn, wqb[h])
            qn, qr = qh[:, :MLA_NOPE], qh[:, MLA_NOPE:]
            rh = _rsq(_sumsq(qn) + _sumsq(qr), MLA_QK)
            xh_n, xh_r = qn * rh, qr * rh
            dy_n = dqa_ref[h, :, 0:MLA_NOPE]
            dy_r = _rope_bwd(dqa_ref[h, :, MLA_NOPE:MLA_QK_PAD], cos_v, sin_v)
            grads["gq_n"][...] += _colsum(dy_n * xh_n)
            grads["gq_r"][...] += _colsum(dy_r * xh_r)
            dqn, dqr = _norm_bwd([dy_n * gq_n[...], dy_r * gq_r[...]], [xh_n, xh_r], rh, MLA_QK)
            dqh_ref[:, 0:MLA_NOPE] = _mx(dqn)
            dqh_ref[:, MLA_NOPE:MLA_QK_PAD] = _mx(dqr)
            dqh = dqh_ref[...]
            grads["wqb"][h] += _dot_tn(dqh, cqn)
            dcqn = dcqn + _dot(dqh, wqb[h])
        grads["g_qa"][...] += _colsum(dcqn * cqh)
        (dcq,) = _norm_bwd([dcqn * g_qa[...]], [cqh], rq, MLA_Q_RANK)
        dproj_ref[:, C_CQ:C_CKV] = _mx(dcq)

        ckv = proj[:, C_CKV:C_KPE]
        rkv = _rsq(_sumsq(ckv), MLA_KV_RANK)
        ckvh = ckv * rkv
        ckvn = _mx(ckvh * g_kva[...])
        kpe = proj[:, C_KPE:C_QS]
        ss_pe = _sumsq(kpe)
        kv = _dot_nt(ckvn, w_kvb[...])
        dkpe = jnp.zeros((tb, LANES), jnp.float32)
        for h in range(MLA_HEADS):
            c0 = h * (MLA_NOPE + MLA_V)
            c1 = c0 + MLA_NOPE
            kn = kv[:, c0:c1]
            rh = _rsq(_sumsq(kn) + ss_pe, MLA_QK)
            xh_n, xh_r = kn * rh, kpe * rh
            dy_n = dka_ref[h, :, 0:MLA_NOPE]
            dy_r = _rope_bwd(dka_ref[h, :, MLA_NOPE:MLA_QK_PAD], cos_v, sin_v)
            grads["gk_n"][...] += _colsum(dy_n * xh_n)
            grads["gk_r"][...] += _colsum(dy_r * xh_r)
            dkn, dkr = _norm_bwd([dy_n * gk_n[...], dy_r * gk_r[...]], [xh_n, xh_r], rh, MLA_QK)
            dkpe = dkpe + dkr
            dkv_ref[:, c0:c1] = _mx(dkn)
            dkv_ref[:, c1:c1 + MLA_V] = _mx(dva_ref[h])
        dkv = dkv_ref[...]
        grads["w_kvb"][...] += _dot_tn(dkv, ckvn)
        dckvn = _dot(dkv, w_kvb[...])
        grads["g_kva"][...] += _colsum(dckvn * ckvh)
        (dckv,) = _norm_bwd([dckvn * g_kva[...]], [ckvh], rkv, MLA_KV_RANK)
        dproj_ref[:, C_CKV:C_KPE] = _mx(dckv)
        dproj_ref[:, C_KPE:C_QS] = _mx(dkpe)

        def pair_bwd(tv, dy, g_ref, gname):
            r = _rsq(_half_sums(tv * tv), SWA_D)
            xh = tv * r
            dpre = _rope_bwd(dy, cos_v, sin_v)
            grads[gname][...] += _colsum(dpre * xh)
            dn = dpre * g_ref[...]
            return r * (dn - xh * (_half_sums(dn * xh) * (1.0 / SWA_D)))

        for j in range(SWA_PAIRS):
            c0 = C_QS + j * LANES
            dproj_ref[:, c0:c0 + LANES] = _mx(pair_bwd(proj[:, c0:c0 + LANES], dqb_ref[j], g_sq, "g_sq"))
        dproj_ref[:, C_KS:C_VS] = _mx(pair_bwd(proj[:, C_KS:C_VS], _undup_halves(dkb_ref[0], dkb_ref[1]), g_sk, "g_sk"))
        dproj_ref[:, C_VS:W_IN_PACKED] = _mx(_undup_halves(dvb_ref[0], dvb_ref[1]))

        dproj = dproj_ref[...]
        grads["w_in"][...] += _dot_tn(dproj, hb)
        dh = _dot(dproj, w_in[...])
        grads["mix_g"][...] += _colsum(dh * xhat)
        (dxv,) = _norm_bwd([dh * mix_g[...]], [xhat], r0, d)
        dx_ref[...] = dxin_ref[...] + dxv

    grad_shapes = [p[n].shape for n in PREP_WEIGHTS]
    args = (x, dxin, cos, sin_s, *[p[n] for n in PREP_WEIGHTS], dqa, dka, dva, dqb, dkb, dvb)
    return _call(
        body, args, sides, name="prep_bwd", grid=(t // tb,),
        in_specs=[_rows(tb, d), _rows(tb, d), _rows(tb, LANES), _rows(tb, LANES)] + _prep_specs(p) + [
            _heads_rows(MLA_HEADS, tb, MLA_QK_PAD), _heads_rows(MLA_HEADS, tb, MLA_QK_PAD),
            _heads_rows(MLA_HEADS, tb, MLA_V), _heads_rows(SWA_PAIRS, tb, LANES),
            _heads_rows(SWA_KV, tb, LANES), _heads_rows(SWA_KV, tb, LANES)],
        out_specs=[_rows(tb, d)] + [_acc(s) for s in grad_shapes],
        out_shape=[_sds((t, d), jnp.float32)] + [_sds(s, jnp.float32) for s in grad_shapes],
        scratch_shapes=[pltpu.VMEM((tb, W_IN_PACKED), _MXU), pltpu.VMEM((tb, MLA_HEADS * (MLA_NOPE + MLA_V)), _MXU),
                        pltpu.VMEM((tb, MLA_QK_PAD), _MXU)],
        semantics=("arbitrary",))


def _strips(n):
    step = min(_STRIP, n)
    return [slice(r, r + step) for r in range(0, n, step)]


def _mla_fwd(q, k, v, sides=()):
    hn, t, dq = q.shape
    dv = v.shape[2]
    bq = min(_BQ, t)
    scale = MLA_QK ** -0.5
    scale2 = scale * LOG2_E

    def body(q_ref, k_ref, v_ref, o_ref, l_ref):
        i = pl.program_id(1)
        qv = q_ref[0]

        def step(first_block, width, carry, masked):
            m, l, acc = carry
            start = pl.multiple_of(first_block * bq, bq)
            s = _dot_nt(qv, k_ref[0, pl.ds(start, width), :])
            if masked:
                row = lax.broadcasted_iota(jnp.int32, (bq, width), 0)
                col = lax.broadcasted_iota(jnp.int32, (bq, width), 1)
                s = jnp.where(col <= row, s, NEG)
            m_new = jnp.maximum(m, _rowmax(s))
            alpha = jnp.exp2((m - m_new) * scale2)
            pv = jnp.exp2((s - m_new) * scale2)
            l = alpha * l + _rowsum(pv)
            acc = alpha * acc + _dot(_mx(pv), v_ref[0, pl.ds(start, width), :])
            return m_new, l, acc

        init = (jnp.full((bq, 1), NEG, jnp.float32), jnp.zeros((bq, 1), jnp.float32), jnp.zeros((bq, dv), jnp.float32))
        carry, done = init, 0
        for group in (4, 2, 1):
            count = (i - done) // group
            carry = lax.fori_loop(0, count, lambda g, c, done=done, group=group: step(done + group * g, group * bq, c, False), carry)
            done = done + group * count
        m, l, acc = step(i, bq, carry, True)
        o_ref[0] = acc / l
        l_ref[0] = jnp.broadcast_to(m * scale + jnp.log(l), (bq, LANES))

    return _call(
        body, (q, k, v), sides, name="mla_fwd", grid=(hn, t // bq),
        in_specs=[pl.BlockSpec((1, bq, dq), lambda h, i: (h, i, 0)),
                  pl.BlockSpec((1, t, dq), lambda h, i: (h, 0, 0)),
                  pl.BlockSpec((1, t, dv), lambda h, i: (h, 0, 0))],
        out_specs=[pl.BlockSpec((1, bq, dv), lambda h, i: (h, i, 0)),
                   pl.BlockSpec((1, bq, LANES), lambda h, i: (h, i, 0))],
        out_shape=[_sds((hn, t, dv), jnp.float32), _sds((hn, t, LANES), jnp.float32)],
        semantics=("parallel", "arbitrary"))


def _mla_bwd(q, k, v, do, lse_rows, dsum_rows, sides=()):
    hn, t, dq_w = q.shape
    dv_w = v.shape[2]
    bq = min(_BQ, t)
    nb = t // bq
    wide = max(_BWD_GROUPS) * bq
    scale = MLA_QK ** -0.5

    def body(q_ref, do_ref, l_ref, d_ref, k_ref, v_ref, dq_ref, dk_ref, dv_ref, st_scr, dpt_scr, p_scr, ds_scr):
        j = pl.program_id(1)

        @pl.when(j == 0)
        def _():
            dq_ref[...] = jnp.zeros_like(dq_ref)

        kv = k_ref[0]
        vv = v_ref[0]
        dk_ref[0] = jnp.zeros((bq, dq_w), jnp.float32)
        dv_ref[0] = jnp.zeros((bq, dv_w), jnp.float32)

        def tile(first_block, n_blk, masked):
            width = n_blk * bq
            start = pl.multiple_of(first_block * bq, bq)
            qv = q_ref[0, pl.ds(start, width), :]
            dov = do_ref[0, pl.ds(start, width), :]
            st_scr[:, :width] = _dot_nt(kv, qv)
            dpt_scr[:, :width] = _dot_nt(vv, dov)
            lse2 = jnp.concatenate([l_ref[0, first_block + b] for b in range(n_blk)], axis=1) * LOG2_E
            dsum = jnp.concatenate([d_ref[0, first_block + b] for b in range(n_blk)], axis=1)
            for rows in _strips(bq):
                pt = jnp.exp2(st_scr[rows, :width] * (scale * LOG2_E) - lse2)
                if masked:
                    n_rows = rows.stop - rows.start
                    row = lax.broadcasted_iota(jnp.int32, (n_rows, width), 0) + rows.start
                    col = lax.broadcasted_iota(jnp.int32, (n_rows, width), 1)
                    pt = jnp.where(row <= col, pt, 0.0)
                p_scr[rows, :width] = _mx(pt)
                ds_scr[rows, :width] = _mx(pt * (dpt_scr[rows, :width] - dsum) * scale)
            ds_t = ds_scr[:, :width]
            dv_ref[0] += _dot(p_scr[:, :width], dov)
            dk_ref[0] += _dot(ds_t, qv)
            dq_ref[0, pl.ds(start, width), :] += _dot_tn(ds_t, kv)

        def group_loop(first_block, n_blk, count):
            def loop_body(g, carry):
                tile(first_block + n_blk * g, n_blk, False)
                return carry

            lax.fori_loop(0, count, loop_body, 0)

        tile(j, 1, True)
        done = j + 1
        for n_blk in _BWD_GROUPS:
            count = (nb - done) // n_blk
            group_loop(done, n_blk, count)
            done = done + n_blk * count

    return _call(
        body, (q, do, lse_rows, dsum_rows, k, v), sides, name="mla_bwd", grid=(hn, nb),
        in_specs=[pl.BlockSpec((1, t, dq_w), lambda h, j: (h, 0, 0)),
                  pl.BlockSpec((1, t, dv_w), lambda h, j: (h, 0, 0)),
                  pl.BlockSpec((1, nb, 1, bq), lambda h, j: (h, 0, 0, 0)),
                  pl.BlockSpec((1, nb, 1, bq), lambda h, j: (h, 0, 0, 0)),
                  pl.BlockSpec((1, bq, dq_w), lambda h, j: (h, j, 0)),
                  pl.BlockSpec((1, bq, dv_w), lambda h, j: (h, j, 0))],
        out_specs=[pl.BlockSpec((1, t, dq_w), lambda h, j: (h, 0, 0)),
                   pl.BlockSpec((1, bq, dq_w), lambda h, j: (h, j, 0)),
                   pl.BlockSpec((1, bq, dv_w), lambda h, j: (h, j, 0))],
        out_shape=[_sds((hn, t, dq_w), jnp.float32), _sds((hn, t, dq_w), jnp.float32), _sds((hn, t, dv_w), jnp.float32)],
        scratch_shapes=[pltpu.VMEM((bq, wide), jnp.float32), pltpu.VMEM((bq, wide), jnp.float32),
                        pltpu.VMEM((bq, wide), _MXU), pltpu.VMEM((bq, wide), _MXU)],
        semantics=("parallel", "arbitrary"))


STACK = SWA_GROUP * SWA_BLOCK


def _swa_stack(ref, c, rows):
    low = _low_half()
    parts = []
    for g in range(SWA_GROUP):
        tv = ref[SWA_GROUP // 2 * c + g // 2, rows, :]
        keep = low if g % 2 == 0 else jnp.logical_not(low)
        parts.append(_mx(jnp.where(keep, tv, jnp.zeros_like(tv))))
    return jnp.concatenate(parts, axis=0)


def _swa_cols(ref, c, rows):
    return jnp.concatenate([ref[SWA_GROUP * c + g, rows, 0:1] for g in range(SWA_GROUP)], axis=0)


def _swa_sink_col(s_ref, c):
    return jnp.concatenate([jnp.broadcast_to(s_ref[SWA_GROUP * c + g][:, 0:1], (SWA_BLOCK, 1))
                            for g in range(SWA_GROUP)], axis=0)


def _swa_band_masks():
    row = lax.broadcasted_iota(jnp.int32, (STACK, SWA_BLOCK), 0) & (SWA_BLOCK - 1)
    col = lax.broadcasted_iota(jnp.int32, (STACK, SWA_BLOCK), 1)
    return col <= row, col > row


def _swa_band(has_previous):
    row = lax.broadcasted_iota(jnp.int32, (STACK, 2 * SWA_BLOCK), 0) & (SWA_BLOCK - 1)
    col = lax.broadcasted_iota(jnp.int32, (STACK, 2 * SWA_BLOCK), 1)
    before = jnp.logical_and(col < SWA_BLOCK, col > row)
    if has_previous is not True:
        before = jnp.logical_and(before, has_previous)
    return jnp.logical_or(before, jnp.logical_and(col >= SWA_BLOCK, col - SWA_BLOCK <= row))


def _swa_keys(ref, prev_ref, c, b):
    if b == 0:
        return jnp.concatenate([prev_ref[c], ref[c, 0:SWA_BLOCK, :]], axis=0)
    return ref[c, (b - 1) * SWA_BLOCK:(b + 1) * SWA_BLOCK, :]


def _swa_unstack_pairs(ref, c, rows, stacked):
    for pr in range(SWA_GROUP // 2):
        r0 = 2 * pr * SWA_BLOCK
        ref[SWA_GROUP // 2 * c + pr, rows, :] = _pick_halves(stacked[r0:r0 + SWA_BLOCK], stacked[r0 + SWA_BLOCK:r0 + 2 * SWA_BLOCK])


def _swa_blocks(t):
    nblk = t // SWA_BLOCK
    bps = min(_SWA_STEP, nblk)
    return nblk, bps, bps * SWA_BLOCK


def _swa_fwd(q, k, v, sinks):
    _, t, _ = q.shape
    nblk, bps, sb = _swa_blocks(t)
    scale = SWA_D ** -0.5

    def body(q_ref, k_ref, kp_ref, v_ref, vp_ref, s_ref, o_ref, l_ref):
        n = pl.program_id(0)
        band_first, band = _swa_band(n > 0), _swa_band(True)
        for c in range(SWA_KV):
            sink = _swa_sink_col(s_ref, c)
            for b in range(bps):
                rows = slice(b * SWA_BLOCK, (b + 1) * SWA_BLOCK)
                qs = _swa_stack(q_ref, c, rows)
                s = jnp.where(band_first if b == 0 else band, _dot_nt(qs, _swa_keys(k_ref, kp_ref, c, b)) * scale, NEG)
                m = jnp.maximum(_rowmax(s), sink)
                e = jnp.exp(s - m)
                denom = _rowsum(e) + jnp.exp(sink - m)
                o = _dot(_mx(e * (1.0 / denom)), _swa_keys(v_ref, vp_ref, c, b))
                lse = m + jnp.log(denom)
                for g in range(SWA_GROUP):
                    l_ref[SWA_GROUP * c + g, rows, :] = jnp.broadcast_to(
                        lse[g * SWA_BLOCK:(g + 1) * SWA_BLOCK], (SWA_BLOCK, LANES))
                _swa_unstack_pairs(o_ref, c, rows, o)

    main = lambda n: (0, n, 0)
    prev = lambda n: (0, jnp.maximum(n * bps - 1, 0), 0)
    return pl.pallas_call(
        body, name="swa_fwd", grid=(nblk // bps,),
        in_specs=[pl.BlockSpec((SWA_PAIRS, sb, LANES), main),
                  pl.BlockSpec((SWA_KV, sb, LANES), main), pl.BlockSpec((SWA_KV, SWA_BLOCK, LANES), prev),
                  pl.BlockSpec((SWA_KV, sb, LANES), main), pl.BlockSpec((SWA_KV, SWA_BLOCK, LANES), prev),
                  _const((SWA_HEADS, 1, LANES))],
        out_specs=[pl.BlockSpec((SWA_PAIRS, sb, LANES), main), pl.BlockSpec((SWA_HEADS, sb, LANES), main)],
        out_shape=[_sds((SWA_PAIRS, t, LANES), jnp.float32), _sds((SWA_HEADS, t, LANES), jnp.float32)],
        compiler_params=_cparams(("parallel",)),
    )(q, k, k, v, v, sinks)


def _swa_bwd(q, k, v, sinks, do, lse, dsum):
    _, t, _ = q.shape
    nblk, bps, sb = _swa_blocks(t)
    steps = nblk // bps
    scale = SWA_D ** -0.5

    def body(q_ref, k_ref, kp_ref, v_ref, vp_ref, s_ref, do_ref, l_ref, d_ref, qn_ref, don_ref, ln_ref, dn_ref,
             dq_ref, dk_ref, dv_ref, ds_ref):
        n = pl.program_id(0)

        @pl.when(n == 0)
        def _():
            ds_ref[...] = jnp.zeros_like(ds_ref)

        _, m_prev = _swa_band_masks()
        band_first, band = _swa_band(n > 0), _swa_band(True)
        everything = slice(0, SWA_BLOCK)

        def probs(qs, keys, mask, lcol):
            return jnp.where(mask, jnp.exp(_dot_nt(qs, keys) * scale - lcol), 0.0)

        def dscores(pm, dos, vals, dcol):
            return _mx(pm * (_dot_nt(dos, vals) - dcol) * scale)

        for c in range(SWA_KV):
            sink = _swa_sink_col(s_ref, c)
            dk_acc = [jnp.zeros((SWA_BLOCK, LANES), jnp.float32) for _ in range(bps)]
            dv_acc = [jnp.zeros((SWA_BLOCK, LANES), jnp.float32) for _ in range(bps)]
            for b in range(bps):
                rows = slice(b * SWA_BLOCK, (b + 1) * SWA_BLOCK)
                keys, vals = _swa_keys(k_ref, kp_ref, c, b), _swa_keys(v_ref, vp_ref, c, b)
                qs = _swa_stack(q_ref, c, rows)
                dos = _swa_stack(do_ref, c, rows)
                lcol = _swa_cols(l_ref, c, rows)
                dcol = _swa_cols(d_ref, c, rows)
                pm = probs(qs, keys, band_first if b == 0 else band, lcol)
                ds = dscores(pm, dos, vals, dcol)
                _swa_unstack_pairs(dq_ref, c, rows, _dot(ds, keys))
                dk_both = _dot_tn(ds, qs)
                dv_both = _dot_tn(_mx(pm), dos)
                dk_acc[b] = dk_acc[b] + dk_both[SWA_BLOCK:]
                dv_acc[b] = dv_acc[b] + dv_both[SWA_BLOCK:]
                if b > 0:
                    dk_acc[b - 1] = dk_acc[b - 1] + dk_both[:SWA_BLOCK]
                    dv_acc[b - 1] = dv_acc[b - 1] + dv_both[:SWA_BLOCK]
                p_sink = jnp.exp(sink - lcol) * dcol
                for g in range(SWA_GROUP):
                    ds_ref[SWA_GROUP * c + g] += -jnp.sum(p_sink[g * SWA_BLOCK:(g + 1) * SWA_BLOCK])
            tail = slice((bps - 1) * SWA_BLOCK, bps * SWA_BLOCK)
            kc, vc = k_ref[c, tail, :], v_ref[c, tail, :]
            qs = _swa_stack(qn_ref, c, everything)
            dos = _swa_stack(don_ref, c, everything)
            lcol = _swa_cols(ln_ref, c, everything)
            dcol = _swa_cols(dn_ref, c, everything)
            p_p = probs(qs, kc, jnp.logical_and(m_prev, n < steps - 1), lcol)
            ds_p = dscores(p_p, dos, vc, dcol)
            dk_acc[bps - 1] = dk_acc[bps - 1] + _dot_tn(ds_p, qs)
            dv_acc[bps - 1] = dv_acc[bps - 1] + _dot_tn(_mx(p_p), dos)
            for b in range(bps):
                rows = slice(b * SWA_BLOCK, (b + 1) * SWA_BLOCK)
                dk_ref[c, rows, :] = dk_acc[b]
                dv_ref[c, rows, :] = dv_acc[b]

    main = lambda n: (0, n, 0)
    prev = lambda n: (0, jnp.maximum(n * bps - 1, 0), 0)
    nxt = lambda n: (0, jnp.minimum((n + 1) * bps, nblk - 1), 0)
    pairs = pl.BlockSpec((SWA_PAIRS, sb, LANES), main)
    kvs = pl.BlockSpec((SWA_KV, sb, LANES), main)
    kv_prev = pl.BlockSpec((SWA_KV, SWA_BLOCK, LANES), prev)
    stats = pl.BlockSpec((SWA_HEADS, sb, LANES), main)
    pairs_next = pl.BlockSpec((SWA_PAIRS, SWA_BLOCK, LANES), nxt)
    stats_next = pl.BlockSpec((SWA_HEADS, SWA_BLOCK, LANES), nxt)
    return pl.pallas_call(
        body, name="swa_bwd", grid=(steps,),
        in_specs=[pairs, kvs, kv_prev, kvs, kv_prev, _const((SWA_HEADS, 1, LANES)), pairs, stats, stats,
                  pairs_next, pairs_next, stats_next, stats_next],
        out_specs=[pairs, kvs, kvs, _acc((SWA_HEADS, 1, LANES))],
        out_shape=[_sds((SWA_PAIRS, t, LANES), jnp.float32), _sds((SWA_KV, t, LANES), jnp.float32),
                   _sds((SWA_KV, t, LANES), jnp.float32), _sds((SWA_HEADS, 1, LANES), jnp.float32)],
        compiler_params=_cparams(("arbitrary",)),
    )(q, k, k, v, v, sinks, do, lse, dsum, q, do, lse, dsum)


MIX_SLABS = 4
MIX_WIDTH = MIX_SLABS * LANES


def _mix_out_fwd(x, oa, ob, ga, gb, wo_a, wo_b):
    t, d = x.shape
    tb = min(_TB, t)

    def body(x_ref, oa_ref, ob_ref, ga_ref, gb_ref, woa_ref, wob_ref, xo_ref):
        y = x_ref[...]
        for o_ref, g_ref, w_ref in ((oa_ref, ga_ref, woa_ref), (ob_ref, gb_ref, wob_ref)):
            r = _rsq(sum(_sumsq(o_ref[h]) for h in range(MIX_SLABS)), MIX_WIDTH)
            for h in range(MIX_SLABS):
                y = y + _dot(_mx(o_ref[h] * r * g_ref[h]), w_ref[h])
        xo_ref[...] = y

    slab = _heads_rows(MIX_SLABS, tb, LANES)
    return pl.pallas_call(
        body, name="mix_out_fwd", grid=(t // tb,),
        in_specs=[_rows(tb, d), slab, slab, _const(ga.shape), _const(gb.shape), _const(wo_a.shape), _const(wo_b.shape)],
        out_specs=_rows(tb, d),
        out_shape=_sds((t, d), jnp.float32),
        compiler_params=_cparams(("parallel",)),
    )(x, oa, ob, ga, gb, wo_a, wo_b)


def _mix_out_bwd(dx, oa, ob, ga, gb, wo_a, wo_b):
    t, d = dx.shape
    tb = min(_TB, t)

    def group(o_ref, g_ref, w_ref, dyb, do_ref, n_ref, col0, dg_ref):
        r = _rsq(sum(_sumsq(o_ref[h]) for h in range(MIX_SLABS)), MIX_WIDTH)
        xh, dn = [], []
        for h in range(MIX_SLABS):
            xh.append(o_ref[h] * r)
            n_ref[:, col0 + h * LANES:col0 + (h + 1) * LANES] = _mx(xh[h] * g_ref[h])
            dm = _dot_nt(dyb, w_ref[h])
            dg_ref[h] += _colsum(dm * xh[h])
            dn.append(dm * g_ref[h])
        c = sum(_rowsum(dn[h] * xh[h]) for h in range(MIX_SLABS)) * (1.0 / MIX_WIDTH)
        prods = []
        for h in range(MIX_SLABS):
            do = r * (dn[h] - xh[h] * c)
            do_ref[h] = do.astype(do_ref.dtype)
            prods.append(do * o_ref[h])
        return prods

    def body(dx_ref, oa_ref, ob_ref, ga_ref, gb_ref, woa_ref, wob_ref,
             doa_ref, dsa_ref, dob_ref, dsb_ref, n_ref, dy_ref, dga_ref, dgb_ref):
        @pl.when(pl.program_id(0) == 0)
        def _():
            dga_ref[...] = jnp.zeros_like(dga_ref)
            dgb_ref[...] = jnp.zeros_like(dgb_ref)

        dyb = _mx(dx_ref[...])
        dy_ref[...] = dyb
        for h, pr in enumerate(group(oa_ref, ga_ref, woa_ref, dyb, doa_ref, n_ref, 0, dga_ref)):
            dsa_ref[h] = jnp.broadcast_to(_rowsum(pr), (tb, LANES))
        low = _low_half()
        for j, pr in enumerate(group(ob_ref, gb_ref, wob_ref, dyb, dob_ref, n_ref, MIX_WIDTH, dgb_ref)):
            dsb_ref[2 * j] = jnp.broadcast_to(_rowsum(jnp.where(low, pr, 0.0)), (tb, LANES))
            dsb_ref[2 * j + 1] = jnp.broadcast_to(_rowsum(jnp.where(low, 0.0, pr)), (tb, LANES))

    slab = _heads_rows(MIX_SLABS, tb, LANES)
    return pl.pallas_call(
        body, name="mix_out_bwd", grid=(t // tb,),
        in_specs=[_rows(tb, d), slab, slab, _const(ga.shape), _const(gb.shape), _const(wo_a.shape), _const(wo_b.shape)],
        out_specs=[slab, slab, slab, _heads_rows(SWA_HEADS, tb, LANES), _rows(tb, 2 * MIX_WIDTH), _rows(tb, d),
                   _acc(ga.shape), _acc(gb.shape)],
        out_shape=[_sds((MIX_SLABS, t, LANES), _MXU), _sds((MIX_SLABS, t, LANES), jnp.float32),
                   _sds((MIX_SLABS, t, LANES), jnp.float32), _sds((SWA_HEADS, t, LANES), jnp.float32),
                   _sds((t, 2 * MIX_WIDTH), _MXU), _sds((t, d), _MXU),
                   _sds(ga.shape, jnp.float32), _sds(gb.shape, jnp.float32)],
        compiler_params=_cparams(("arbitrary",)),
    )(dx, oa, ob, ga, gb, wo_a, wo_b)


def _loss_head(y, target):
    t, d = y.shape
    tb = min(_TB, t)

    def body(y_ref, t_ref, dy_ref, acc_ref):
        @pl.when(pl.program_id(0) == 0)
        def _():
            acc_ref[...] = jnp.zeros_like(acc_ref)

        err = y_ref[...] - t_ref[...]
        dy_ref[...] = err * (1.0 / d)
        acc_ref[...] += jnp.sum(err * err)

    return pl.pallas_call(
        body, name="loss_head", grid=(t // tb,),
        in_specs=[_rows(tb, d), _rows(tb, d)],
        out_specs=[_rows(tb, d), _acc((8, LANES))],
        out_shape=[_sds((t, d), jnp.float32), _sds((8, LANES), jnp.float32)],
        compiler_params=_cparams(("arbitrary",)),
    )(y, target)


def _is_transposed(name):
    return name not in ROW_SHARDED


def _pack_layer(shards, l, width, names):
    rows = [(shards[n][l].T if _is_transposed(n) else shards[n][l]).reshape(-1, width) for n in names]
    return jnp.concatenate(rows, axis=0)


def _full_shape(like, name):
    _, a, b = like[name].shape
    return (N_DEV * b, a) if _is_transposed(name) else (N_DEV * a, b)


def _unpack_full(gathered, like, names):
    out, off = {}, 0
    for n in names:
        rows_n = like[n][0].size // gathered.shape[-1]
        out[n] = gathered[:, off:off + rows_n].reshape(_full_shape(like, n))
        off += rows_n
    return out


def _stored(a, name):
    return jnp.swapaxes(a, 1, 2) if _is_transposed(name) else a


def _grads_by_destination(grads, width, names):
    by_dest = lambda n: grads[n].reshape(N_DEV, -1, width)
    if names is OTHER_BIG:
        return [jnp.concatenate([by_dest(n) for n in names], axis=1)]
    return [by_dest(n) for n in names]


def _shards_from_rows(rows, like):
    out = dict(zip(FFN_BIG, rows[:len(FFN_BIG)]))
    rest, off = rows[len(FFN_BIG)], 0
    for n in OTHER_BIG:
        _, a, b = like[n].shape
        rows_n = a * b // rest.shape[-1]
        out[n] = rest[off:off + rows_n].reshape((b, a) if _is_transposed(n) else (a, b))
        off += rows_n
    return out


def _small_rows(n_elems):
    return -(-n_elems // LANES)


def _pack_small(arrays):
    parts = []
    for n in SMALL:
        v = arrays[n]
        depth, width = v.shape
        padded = _small_rows(width) * LANES
        parts.append(jnp.pad(v, ((0, 0), (0, padded - width))).reshape(-1, LANES))
    packed = jnp.concatenate(parts, axis=0)
    return jnp.pad(packed, ((0, (-packed.shape[0]) % 8), (0, 0)))


def _unpack_small(packed, like):
    out, off = {}, 0
    for n in SMALL:
        depth, width = like[n].shape
        rows_n = _small_rows(width)
        seg = packed[off:off + depth * rows_n].reshape(depth, rows_n * LANES)
        out[n] = seg[:, :width]
        off += depth * rows_n
    return out


def _rope_tables(t):
    pos = jnp.arange(t, dtype=jnp.float32)
    inv = 1.0 / (ROPE_THETA ** (jnp.arange(0, MLA_ROPE, 2, dtype=jnp.float32) / MLA_ROPE))
    ang = pos[:, None] * inv[None, :]
    cos, sin = jnp.cos(ang), jnp.sin(ang)
    return jnp.concatenate([cos, cos, cos, cos], axis=1), jnp.concatenate([-sin, sin, -sin, sin], axis=1)


def _pad_lanes(a, width):
    return jnp.pad(a, [(0, 0)] * (a.ndim - 1) + [(0, width - a.shape[-1])])


def _ffn_params(full, small, l, tag):
    return small[tag + "_norm"][l][None, :], full[tag + "_w_gate"], full[tag + "_w_up"], full[tag + "_w_down"]


def _mixer_params(full, small, l):
    w_in = full["w_in"]
    d = w_in.shape[1]
    mla_rows = W_IN_COLS[0]
    w_in_p = jnp.concatenate([w_in[:mla_rows], jnp.zeros((LANES - MLA_ROPE, d), w_in.dtype), w_in[mla_rows:]], axis=0)
    wqb = full["mla_w_q_b"].reshape(MLA_HEADS, MLA_QK, MLA_Q_RANK)
    wqb = jnp.pad(wqb, ((0, 0), (0, MLA_QK_PAD - MLA_QK), (0, 0)))
    row = lambda name: small[name][l][None, :]
    twice = lambda g: jnp.concatenate([g, g], axis=1)
    prep = {
        "mix_g": row("mix_norm"), "w_in": w_in_p,
        "g_qa": row("mla_q_a_norm"), "wqb": wqb,
        "g_kva": row("mla_kv_a_norm"), "w_kvb": full["mla_w_kv_b"],
        "gq_n": row("mla_q_norm")[:, :MLA_NOPE], "gq_r": _pad_lanes(row("mla_q_norm")[:, MLA_NOPE:], LANES),
        "gk_n": row("mla_k_norm")[:, :MLA_NOPE], "gk_r": _pad_lanes(row("mla_k_norm")[:, MLA_NOPE:], LANES),
        "g_sq": twice(row("swa_q_norm")), "g_sk": twice(row("swa_k_norm")),
    }
    return {
        "prep": prep,
        "sinks": jnp.broadcast_to(small["swa_sinks"][l][:, None, None], (SWA_HEADS, 1, LANES)),
        "ga": small["mla_out_norm"][l].reshape(MIX_SLABS, 1, LANES),
        "gb": small["swa_out_norm"][l].reshape(MIX_SLABS, 1, LANES),
        "wo_a": full["w_o"][:MIX_WIDTH].reshape(MIX_SLABS, LANES, d),
        "wo_b": full["w_o"][MIX_WIDTH:].reshape(MIX_SLABS, LANES, d),
    }


def _ffn_backward(x_in, dxo, kept, params, tag, sides=()):
    gain, wg, wu, wd = params
    h, s, fa, fu = kept
    (dxi, da, du, dy, dg), side_out = _ffn_dgrad(x_in, gain, dxo, fa, fu, wg, wu, wd, sides)
    dwg, _ = _tn_matmul(da, h, "wgrad_" + tag + "_gate")
    dwu, _ = _tn_matmul(du, h, "wgrad_" + tag + "_up")
    dwd, _ = _tn_matmul(s, dy, "wgrad_" + tag + "_down")
    return dxi, dg[0], dwg, dwu, dwd, side_out


def _ffn_backward_exchanging(x_in, dxo, kept, params, tag, ready, chip_sums, width):
    gain, wg, wu, wd = params
    h, s, fa, fu = kept
    (dxi, da, du, dy, dg), (ready_sib,) = _ffn_dgrad(x_in, gain, dxo, fa, fu, wg, wu, wd, [_side_sibling(ready)])
    ready_owns, ready_wires = chip_sums(ready, ready_sib)
    dwd, (ready_recv,) = _tn_matmul(s, dy, "wgrad_" + tag + "_down", [_side_chips(ready_wires)])
    down = [dwd.reshape(N_DEV, -1, width)]
    dwg, (down_sib,) = _tn_matmul(da, h, "wgrad_" + tag + "_gate", [_side_sibling(down)])
    down_owns, down_wires = chip_sums(down, down_sib)
    dwu, (down_recv,) = _tn_matmul(du, h, "wgrad_" + tag + "_up", [_side_chips(down_wires)])
    return dxi, dg[0], dwg, dwu, _rs_sum(ready_owns, ready_recv), _rs_sum(down_owns, down_recv)


def kernel(x, ffn1_norm, ffn1_w_gate, ffn1_w_up, ffn1_w_down, mix_norm, w_in, mla_q_a_norm, mla_w_q_b, mla_kv_a_norm, mla_w_kv_b, mla_q_norm, mla_k_norm, swa_q_norm, swa_k_norm, swa_sinks, mla_out_norm, swa_out_norm, w_o, ffn2_norm, ffn2_w_gate, ffn2_w_up, ffn2_w_down, loss_target, m_ffn1_norm, m_ffn1_w_gate, m_ffn1_w_up, m_ffn1_w_down, m_mix_norm, m_w_in, m_mla_q_a_norm, m_mla_w_q_b, m_mla_kv_a_norm, m_mla_w_kv_b, m_mla_q_norm, m_mla_k_norm, m_swa_q_norm, m_swa_k_norm, m_swa_sinks, m_mla_out_norm, m_swa_out_norm, m_w_o, m_ffn2_norm, m_ffn2_w_gate, m_ffn2_w_up, m_ffn2_w_down, v_ffn1_norm, v_ffn1_w_gate, v_ffn1_w_up, v_ffn1_w_down, v_mix_norm, v_w_in, v_mla_q_a_norm, v_mla_w_q_b, v_mla_kv_a_norm, v_mla_w_kv_b, v_mla_q_norm, v_mla_k_norm, v_swa_q_norm, v_swa_k_norm, v_swa_sinks, v_mla_out_norm, v_swa_out_norm, v_w_o, v_ffn2_norm, v_ffn2_w_gate, v_ffn2_w_up, v_ffn2_w_down):
    local = dict(locals())
    w = {n: local[n] for n in WEIGHTS}
    m = {n: local["m_" + n] for n in WEIGHTS}
    v = {n: local["v_" + n] for n in WEIGHTS}
    depth = ffn1_norm.shape[0]
    t, d = x.shape[-2], x.shape[-1]
    x2d = x.reshape(t, d)
    target = loss_target.reshape(t, d)
    bq = min(_BQ, t)

    big = {n: w[n] for n in BIG}
    packed = [[_mx(_pack_layer(big, l, d, names)) for names in GATHER_ORDER] for l in range(depth)]
    cos, sin_s = _rope_tables(t)
    x_i, y_i, c_i = _position()
    dest_idx = jnp.stack([4 * px + 2 * py + c_i for px, py in _relations(x_i, y_i)]).astype(jnp.int32)

    params, saved = [], []
    xc = x2d
    ffn1_full = _unpack_full(_all_gather(packed[0][0]), big, FFN1_BIG)
    for l in range(depth):
        pr = {"ffn1": _ffn_params(ffn1_full, w, l, "ffn1")}
        x0 = xc
        (x1, *kept1), ((mixer_gathered,),) = _ffn_fwd(x0, *pr["ffn1"], sides=[_side_gather(packed[l][1])])
        pr.update(_mixer_params(_unpack_full(mixer_gathered, big, OTHER_BIG), w, l))
        qa, ka, va, qb, kb, vb = _prep_fwd(x1, cos, sin_s, pr["prep"])
        (oa, lse_a), ((ffn2_gathered,),) = _mla_fwd(qa, ka, va, sides=[_side_gather(packed[l][2])])
        pr["ffn2"] = _ffn_params(_unpack_full(ffn2_gathered, big, FFN2_BIG), w, l, "ffn2")
        ob, lse_b = _swa_fwd(qb, kb, vb, pr["sinks"])
        x2 = _mix_out_fwd(x1, oa, ob, pr["ga"], pr["gb"], pr["wo_a"], pr["wo_b"])
        (x3, *kept2), next_gathered = _ffn_fwd(
            x2, *pr["ffn2"], sides=[_side_gather(packed[l + 1][0])] if l + 1 < depth else [])
        if next_gathered:
            ffn1_full = _unpack_full(next_gathered[0][0], big, FFN1_BIG)
        params.append(pr)
        saved.append((x0, kept1, x1, qa, ka, va, qb, kb, vb, oa, lse_a, ob, lse_b, x2, kept2))
        xc = x3

    dx, sq_err = _loss_head(xc, target)
    loss = lax.psum(0.5 / d * sq_err[0, 0], MESH_AXES)

    def chip_sums(arrays, sibling_parts):
        sums = _rs_chip_sums(arrays, sibling_parts, dest_idx)
        return sums[:len(arrays)], sums[len(arrays):]

    grad_shards = [None] * depth
    small_grads = {n: [None] * depth for n in SMALL}
    upper = None
    for l in reversed(range(depth)):
        pr = params[l]
        lowest = l == 0
        x0, kept1, x1, qa, ka, va, qb, kb, vb, oa, lse_a, ob, lse_b, x2, kept2 = saved[l]
        g = {}
        dx, small_grads["ffn2_norm"][l], g["ffn2_w_gate"], g["ffn2_w_up"], g["ffn2_w_down"], side_out = _ffn_backward(
            x2, dx, kept2, pr["ffn2"], "ffn2", [_side_sibling(upper)] if upper else [])
        if upper:
            upper_owns, upper_wires = chip_sums(upper, side_out[0])
        early = _grads_by_destination(g, d, FFN2_BIG) if lowest else None

        doa, dsum_a, dob, dsum_b, mixed, dyb, dga, dgb = _mix_out_bwd(
            dx, oa, ob, pr["ga"], pr["gb"], pr["wo_a"], pr["wo_b"])
        small_grads["mla_out_norm"][l] = dga.reshape(-1)
        small_grads["swa_out_norm"][l] = dgb.reshape(-1)
        g["w_o"], _ = _tn_matmul(mixed, dyb, "wgrad_wo")

        rows_of = lambda s: s[:, :, 0].reshape(MLA_HEADS, t // bq, 1, bq)
        sides = ([_side_chips(upper_wires)] if upper else []) + ([_side_sibling(early)] if lowest else [])
        (dqa, dka, dva), side_out = _mla_bwd(qa, ka, va, doa, rows_of(lse_a), rows_of(dsum_a), sides)
        if upper:
            grad_shards[l + 1] = _shards_from_rows(_rs_sum(upper_owns, side_out[0]), big)
        if lowest:
            early_owns, early_wires = chip_sums(early, side_out[-1])
        dqb, dkb, dvb, dsinks = _swa_bwd(qb, kb, vb, pr["sinks"], dob, lse_b, dsum_b)
        small_grads["swa_sinks"][l] = dsinks[:, 0, 0]

        outs, side_out = _prep_bwd(x1, dx, cos, sin_s, pr["prep"], dqa, dka, dva, dqb, dkb, dvb,
                                   [_side_chips(early_wires)] if lowest else [])
        if lowest:
            early_rows = _rs_sum(early_owns, side_out[0])
        dx = outs[0]
        pg = dict(zip(PREP_WEIGHTS, outs[1:]))
        g["w_in"] = jnp.concatenate([pg["w_in"][:W_IN_COLS[0]], pg["w_in"][C_QS:]], axis=0)
        g["mla_w_q_b"] = pg["wqb"][:, :MLA_QK].reshape(MLA_HEADS * MLA_QK, MLA_Q_RANK)
        g["mla_w_kv_b"] = pg["w_kvb"]
        fold = lambda gg: gg[0, :HALF] + gg[0, HALF:]
        small_grads["mix_norm"][l] = pg["mix_g"][0]
        small_grads["mla_q_a_norm"][l] = pg["g_qa"][0]
        small_grads["mla_kv_a_norm"][l] = pg["g_kva"][0]
        small_grads["mla_q_norm"][l] = jnp.concatenate([pg["gq_n"][0], pg["gq_r"][0, :MLA_ROPE]])
        small_grads["mla_k_norm"][l] = jnp.concatenate([pg["gk_n"][0], pg["gk_r"][0, :MLA_ROPE]])
        small_grads["swa_q_norm"][l] = fold(pg["g_sq"])
        small_grads["swa_k_norm"][l] = fold(pg["g_sk"])

        if lowest:
            other = _grads_by_destination(g, d, OTHER_BIG)
            dx, small_grads["ffn1_norm"][l], g["ffn1_w_gate"], g["ffn1_w_up"], other_rows, down_rows = (
                _ffn_backward_exchanging(x0, dx, kept1, pr["ffn1"], "ffn1", other, chip_sums, d))
            late = _grads_by_destination(g, d, FFN1_BIG[:2])
            late_owns, late_wires = chip_sums(late, _rs_sibling_exchange(late))
            late_rows = _rs_sum(late_owns, _rs_chip_exchange(late_wires))
            grad_shards[l] = _shards_from_rows(late_rows + down_rows + early_rows + other_rows, big)
        else:
            dx, small_grads["ffn1_norm"][l], g["ffn1_w_gate"], g["ffn1_w_up"], g["ffn1_w_down"], _ = _ffn_backward(
                x0, dx, kept1, pr["ffn1"], "ffn1")
            upper = _grads_by_destination(g, d, FFN_BIG) + _grads_by_destination(g, d, OTHER_BIG)

    grad_big, delta_big, new_m_big, new_v_big = {}, {}, {}, {}
    for n in BIG:
        g_st = jnp.stack([grad_shards[l][n] for l in range(depth)])
        d_st, m_st, v_st = _adamw(g_st, _stored(w[n], n), _stored(m[n], n), _stored(v[n], n), n)
        grad_big[n], delta_big[n], new_m_big[n], new_v_big[n] = (_stored(a, n) for a in (g_st, d_st, m_st, v_st))

    small_partial = _pack_small({n: jnp.stack(small_grads[n]) for n in SMALL})
    g_s = _all_reduce_small(small_partial)
    d_s, m_s, v_s = _adamw_small(g_s, _pack_small(w), _pack_small(m), _pack_small(v))
    grad_small, delta_small, new_m_small, new_v_small = (_unpack_small(a, w) for a in (g_s, d_s, m_s, v_s))

    def ordered(big, small):
        return [big[n] if n in big else small[n] for n in WEIGHTS]

    return (loss, dx.reshape(x.shape), *ordered(grad_big, grad_small), *ordered(delta_big, delta_small),
            *ordered(new_m_big, new_m_small), *ordered(new_v_big, new_v_small))
```

```python
import jax
import jax.numpy as jnp
from jax import lax
from jax.experimental import pallas as pl
from jax.experimental.pallas import tpu as pltpu

N_DEV = 8
EPS = 1e-6
ROPE_THETA = 10000.0
MLA_HEADS = 4
MLA_Q_RANK = 256
MLA_KV_RANK = 128
MLA_NOPE = 128
MLA_ROPE = 64
MLA_V = 128
MLA_QK = MLA_NOPE + MLA_ROPE
MLA_QK_PAD = 256
SWA_HEADS = 8
SWA_KV = 2
SWA_GROUP = SWA_HEADS // SWA_KV
SWA_D = 64
SWA_BLOCK = 128
ADAM_LR = 0.001
ADAM_B1 = 0.9
ADAM_B2 = 0.999
ADAM_EPS = 1e-08
ADAM_WD = 0.01
ADAM_STEP = 10

LANES = 128
HALF = LANES // 2
SWA_PAIRS = SWA_HEADS // 2
W_IN_COLS = (MLA_Q_RANK + MLA_KV_RANK + MLA_ROPE, SWA_HEADS * SWA_D + 2 * SWA_KV * SWA_D)
VMEM_LIMIT = 56 * 1024 * 1024

_MXU = jnp.bfloat16
_TB = 256
_TB_MIX = 512
_BQ = 512
_STRIP = 32
_BWD_GROUPS = (4, 2, 1)
_TK = 1024
_SWA_STEP = 4
RS_ROW_BLOCKS = 2

BIG = ("ffn1_w_gate", "ffn1_w_up", "ffn1_w_down", "w_in", "mla_w_q_b", "mla_w_kv_b", "w_o",
       "ffn2_w_gate", "ffn2_w_up", "ffn2_w_down")
ROW_SHARDED = ("ffn1_w_down", "w_o", "ffn2_w_down")
FFN1_BIG = ("ffn1_w_gate", "ffn1_w_up", "ffn1_w_down")
FFN2_BIG = ("ffn2_w_gate", "ffn2_w_up", "ffn2_w_down")
FFN_BIG = FFN1_BIG + FFN2_BIG
OTHER_BIG = ("w_o", "w_in", "mla_w_q_b", "mla_w_kv_b")
GATHER_ORDER = (FFN1_BIG, OTHER_BIG, FFN2_BIG)
SMALL = ("ffn1_norm", "mix_norm", "mla_q_a_norm", "mla_kv_a_norm", "mla_q_norm", "mla_k_norm",
         "swa_q_norm", "swa_k_norm", "swa_sinks", "mla_out_norm", "swa_out_norm", "ffn2_norm")
WEIGHTS = ("ffn1_norm", "ffn1_w_gate", "ffn1_w_up", "ffn1_w_down", "mix_norm", "w_in", "mla_q_a_norm",
           "mla_w_q_b", "mla_kv_a_norm", "mla_w_kv_b", "mla_q_norm", "mla_k_norm", "swa_q_norm",
           "swa_k_norm", "swa_sinks", "mla_out_norm", "swa_out_norm", "w_o", "ffn2_norm",
           "ffn2_w_gate", "ffn2_w_up", "ffn2_w_down")
MESH_AXES = ("x", "y", "c")
MESH = pl.DeviceIdType.MESH
NEG = -1e30
LOG2_E = 1.4426950408889634


def _f32(t):
    return t.astype(jnp.float32)


def _mx(t):
    return t.astype(_MXU)


def _dot(a, b):
    return jnp.dot(a, b, preferred_element_type=jnp.float32)


def _dot_nt(a, b):
    return lax.dot_general(a, b, (((1,), (1,)), ((), ())), preferred_element_type=jnp.float32)


def _dot_tn(a, b):
    return lax.dot_general(a, b, (((0,), (0,)), ((), ())), preferred_element_type=jnp.float32)


def _rsq(ss, n):
    return lax.rsqrt(ss * (1.0 / n) + EPS)


def _sumsq(t):
    return jnp.sum(t * t, axis=-1, keepdims=True)


def _rowsum(t):
    return jnp.sum(t, axis=-1, keepdims=True)


def _rowmax(t):
    return jnp.max(t, axis=-1, keepdims=True)


def _colsum(t):
    return jnp.sum(t, axis=0, keepdims=True)


def _lane():
    return lax.broadcasted_iota(jnp.int32, (1, LANES), 1)


def _low_half():
    return _lane() < HALF


def _swap32(t):
    return jnp.where((_lane() & 32) == 0, pltpu.roll(t, 96, 1), pltpu.roll(t, 32, 1))


def _rope(t, cos, sin_signed):
    return t * cos + _swap32(t) * sin_signed


def _rope_bwd(d, cos, sin_signed):
    return d * cos + _swap32(d * sin_signed)


def _half_sums(t):
    low = _low_half()
    return jnp.where(low, _rowsum(jnp.where(low, t, 0.0)), _rowsum(jnp.where(low, 0.0, t)))


def _dup_halves(pair):
    low = _low_half()
    swapped = pltpu.roll(pair, HALF, 1)
    return jnp.where(low, pair, swapped), jnp.where(low, swapped, pair)


def _undup_halves(d0, d1):
    return jnp.where(_low_half(), d0 + pltpu.roll(d0, HALF, 1), d1 + pltpu.roll(d1, HALF, 1))


def _pick_halves(a, b):
    return jnp.where(_low_half(), a, b)


def _norm_bwd(dn_list, xh_list, r, n):
    c = sum(_rowsum(dn * xh) for dn, xh in zip(dn_list, xh_list)) * (1.0 / n)
    return [r * (dn - xh * c) for dn, xh in zip(dn_list, xh_list)]


def _cparams(semantics):
    return pltpu.CompilerParams(dimension_semantics=semantics, vmem_limit_bytes=VMEM_LIMIT)


def _const(shape):
    nd = len(shape)
    return pl.BlockSpec(shape, lambda *_: (0,) * nd, pipeline_mode=pl.Buffered(1))


def _acc(shape):
    nd = len(shape)
    return pl.BlockSpec(shape, lambda *_: (0,) * nd)


def _rows(tb, width):
    return pl.BlockSpec((tb, width), lambda i: (i, 0))


def _heads_rows(h, tb, width):
    return pl.BlockSpec((h, tb, width), lambda i: (0, i, 0))


def _sds(shape, dtype):
    return jax.ShapeDtypeStruct(shape, dtype)


def _position():
    return lax.axis_index("x"), lax.axis_index("y"), lax.axis_index("c")


def _all_gather(xs):
    n = len(xs)

    def body(*refs):
        x_refs, out_refs, (send_sems, recv_sems, local_sems) = refs[:n], refs[n:2 * n], refs[2 * n:]
        x, y, c = _position()
        me, sibling = (x, y, c), (x, y, 1 - c)
        chips = [(1 - x, y), (x, 1 - y), (1 - x, 1 - y)]

        def rows(i, px, py, pc):
            return out_refs[i].at[4 * px + 2 * py + pc]

        def copy(i, k, block, to, from_input=False):
            return _remote(x_refs[i] if from_input else rows(i, *block), rows(i, *block),
                           send_sems.at[k * n + i], recv_sems.at[k * n + i], to)

        every = range(n)
        mine = [pltpu.make_async_copy(x_refs[i], rows(i, *me), local_sems.at[i]) for i in every]
        first = [copy(i, 0, me, sibling, True) for i in every]
        first += [copy(i, 1 + j, me, (*chip, c), True) for j, chip in enumerate(chips) for i in every]
        for cp in mine + first:
            cp.start()
        passed = [[copy(i, 4 + j, (*chip, c), sibling) for i in every] for j, chip in enumerate(chips)]
        for j, chip in enumerate(chips):
            for i in every:
                copy(i, 1 + j, (*chip, c), me).wait_recv()
                passed[j][i].start()
        for i in every:
            copy(i, 0, sibling, me).wait_recv()
        for j, chip in enumerate(chips):
            for i in every:
                copy(i, 4 + j, (*chip, 1 - c), me).wait_recv()
        for cp in first + [cp for group in passed for cp in group]:
            cp.wait_send()
        for cp in mine:
            cp.wait()

    hbm = pl.BlockSpec(memory_space=pl.ANY)
    dma = pltpu.SemaphoreType.DMA
    return pl.pallas_call(
        body, name="ag_weights",
        out_shape=[_sds((N_DEV,) + a.shape, a.dtype) for a in xs],
        in_specs=[hbm] * n, out_specs=[hbm] * n,
        scratch_shapes=[dma((7 * n,)), dma((7 * n,)), dma((n,))],
    )(*xs)


def _relations(x, y):
    return [(x, y), (1 - x, y), (x, 1 - y), (1 - x, 1 - y)]


def _remote(src, dst, send_sem, recv_sem, device):
    return pltpu.make_async_remote_copy(src_ref=src, dst_ref=dst, send_sem=send_sem, recv_sem=recv_sem,
                                        device_id=device, device_id_type=MESH)


def _sibling_copies(g_refs, out_refs, send, recv):
    x, y, c = _position()
    n = len(g_refs)
    return [_remote(g.at[4 * px + 2 * py + (1 - c)], o.at[k], send.at[k * n + i], recv.at[k * n + i], (x, y, 1 - c))
            for k, (px, py) in enumerate(_relations(x, y)) for i, (g, o) in enumerate(zip(g_refs, out_refs))]


def _chip_copies(w_refs, out_refs, send, recv):
    x, y, c = _position()
    n = len(w_refs)
    return [_remote(w.at[k + 1], o.at[k], send.at[k * n + i], recv.at[k * n + i], (px, py, c))
            for k, (px, py) in enumerate(_relations(x, y)[1:]) for i, (w, o) in enumerate(zip(w_refs, out_refs))]


def _exchange(arrays, lead, relations, copies_fn, name):
    n = len(arrays)

    def body(*refs):
        copies = copies_fn(refs[:n], refs[n:2 * n], refs[2 * n], refs[2 * n + 1])
        for cp in copies:
            cp.start()
        for cp in copies:
            cp.wait()

    hbm = pl.BlockSpec(memory_space=pl.ANY)
    dma = pltpu.SemaphoreType.DMA
    return pl.pallas_call(
        body, name=name, out_shape=[_sds((lead,) + a.shape[1:], a.dtype) for a in arrays],
        in_specs=[hbm] * n, out_specs=[hbm] * n,
        scratch_shapes=[dma((relations * n,)), dma((relations * n,))],
    )(*arrays)


def _rs_sibling_exchange(gs):
    return _exchange(gs, 4, 4, _sibling_copies, "rs_sibling_exchange")


def _rs_chip_exchange(wires):
    return _exchange(wires, 3, 3, _chip_copies, "rs_chip_exchange")


def _side_exchange(arrays, lead, relations, copies_fn):
    shapes = [_sds((lead,) + a.shape[1:], a.dtype) for a in arrays]
    return list(arrays), shapes, relations * len(arrays), lambda ins, outs, send, recv, local: copies_fn(ins, outs, send, recv)


def _side_sibling(gs):
    return _side_exchange(gs, 4, 4, _sibling_copies)


def _side_chips(wires):
    return _side_exchange(wires, 3, 3, _chip_copies)


def _rs_chip_sums(gs, sibs, dest_idx):
    n = len(gs)

    def body(idx_ref, *refs):
        g_refs, s_refs, own_refs, wire_refs = refs[:n], refs[n:2 * n], refs[2 * n:3 * n], refs[3 * n:]
        totals = [g[0] + s[0] for g, s in zip(g_refs, s_refs)]
        for total, wire in zip(totals, wire_refs):
            wire[0] = total.astype(wire.dtype)

        @pl.when(pl.program_id(1) == 0)
        def _():
            for total, own in zip(totals, own_refs):
                own[...] = total

    def blocks(a, index_map, squeeze):
        rb = a.shape[1] // RS_ROW_BLOCKS
        return pl.BlockSpec((rb, a.shape[2]) if squeeze else (1, rb, a.shape[2]), index_map)

    return pl.pallas_call(
        body, name="rs_chip_sums",
        grid_spec=pltpu.PrefetchScalarGridSpec(
            num_scalar_prefetch=1, grid=(RS_ROW_BLOCKS, 4),
            in_specs=[blocks(g, lambda r, k, idx: (idx[k], r, 0), False) for g in gs]
            + [blocks(g, lambda r, k, idx: (k, r, 0), False) for g in gs],
            out_specs=[blocks(g, lambda r, k, idx: (r, 0), True) for g in gs]
            + [blocks(g, lambda r, k, idx: (k, r, 0), False) for g in gs]),
        out_shape=[_sds(g.shape[1:], jnp.float32) for g in gs] + [_sds((4,) + g.shape[1:], _MXU) for g in gs],
        compiler_params=_cparams(("parallel", "arbitrary")),
    )(dest_idx, *gs, *sibs)


def _side_gather(xs):
    n = len(xs)

    def make(ins, outs, send, recv, local):
        x, y, c = _position()
        me = 4 * x + 2 * y + c
        copies = [pltpu.make_async_copy(x_ref, out_ref.at[me], local.at[i])
                  for i, (x_ref, out_ref) in enumerate(zip(ins, outs))]
        for k in range(1, N_DEV):
            peer = (1 - x if k & 4 else x, 1 - y if k & 2 else y, 1 - c if k & 1 else c)
            copies += [_remote(x_ref, out_ref.at[me], send.at[(k - 1) * n + i], recv.at[(k - 1) * n + i], peer)
                       for i, (x_ref, out_ref) in enumerate(zip(ins, outs))]
        return copies

    return list(xs), [_sds((N_DEV,) + a.shape, a.dtype) for a in xs], (N_DEV - 1) * n, make


def _call(body, args, sides, *, name, grid, in_specs, out_specs, out_shape, scratch_shapes=(), semantics):
    in_specs, out_specs, out_shape = list(in_specs), list(out_specs), list(out_shape)
    n_in, n_out, n_scr = len(in_specs), len(out_specs), len(scratch_shapes)
    sides = list(sides or [])
    if not sides:
        outs = pl.pallas_call(body, name=name, grid=grid, in_specs=in_specs, out_specs=out_specs, out_shape=out_shape,
                              scratch_shapes=list(scratch_shapes), compiler_params=_cparams(semantics))(*args)
        return list(outs), []
    arrays = [a for side in sides for a in side[0]]
    shapes = [s for side in sides for s in side[1]]
    n_side_in, n_side_out = len(arrays), len(shapes)
    hbm = pl.BlockSpec(memory_space=pl.ANY)

    def with_copies(*refs):
        main_in, refs = refs[:n_in], refs[n_in:]
        side_in, refs = refs[:n_side_in], refs[n_side_in:]
        main_out, refs = refs[:n_out], refs[n_out:]
        side_out, refs = refs[:n_side_out], refs[n_side_out:]
        main_scr, sems = refs[:n_scr], refs[n_scr:]
        copies = []
        for k, (side_arrays, side_shapes, _, make) in enumerate(sides):
            copies += make(side_in[:len(side_arrays)], side_out[:len(side_shapes)], *sems[3 * k:3 * k + 3])
            side_in, side_out = side_in[len(side_arrays):], side_out[len(side_shapes):]
        ids = [pl.program_id(a) for a in range(len(grid))]
        first, last = ids[0] == 0, ids[0] == grid[0] - 1
        for i, size in zip(ids[1:], grid[1:]):
            first, last = jnp.logical_and(first, i == 0), jnp.logical_and(last, i == size - 1)

        @pl.when(first)
        def _():
            for cp in copies:
                cp.start()

        body(*main_in, *main_out, *main_scr)

        @pl.when(last)
        def _():
            for cp in copies:
                cp.wait()

    dma = pltpu.SemaphoreType.DMA
    sem_shapes = [dma((n,)) for side in sides for n in (side[2], side[2], len(side[0]))]
    outs = pl.pallas_call(
        with_copies, name=name, grid=grid, in_specs=in_specs + [hbm] * n_side_in,
        out_specs=out_specs + [hbm] * n_side_out, out_shape=out_shape + shapes,
        scratch_shapes=list(scratch_shapes) + sem_shapes,
        compiler_params=_cparams(("arbitrary",) * len(grid)),
    )(*args, *arrays)
    side_outs, rest = [], list(outs[n_out:])
    for side in sides:
        side_outs.append(rest[:len(side[1])])
        rest = rest[len(side[1]):]
    return list(outs[:n_out]), side_outs


def _all_reduce_small(v):
    rows_n = v.shape[0]

    def body(v_ref, out_ref, buf, send_sems, recv_sems):
        x, y, c = _position()
        me = 4 * x + 2 * y + c
        buf[me] = v_ref[...]
        copies = []
        for k in range(1, N_DEV):
            px = 1 - x if k & 4 else x
            py = 1 - y if k & 2 else y
            pc = 1 - c if k & 1 else c
            copies.append(pltpu.make_async_remote_copy(
                src_ref=v_ref, dst_ref=buf.at[me],
                send_sem=send_sems.at[k - 1], recv_sem=recv_sems.at[k - 1], device_id=(px, py, pc), device_id_type=MESH))
        for cp in copies:
            cp.start()
        for cp in copies:
            cp.wait()
        total = buf[0]
        for d in range(1, N_DEV):
            total = total + buf[d]
        out_ref[...] = total

    return pl.pallas_call(
        body, name="ar_small",
        out_shape=_sds((rows_n, LANES), jnp.float32),
        in_specs=[pl.BlockSpec(memory_space=pltpu.VMEM)],
        out_specs=pl.BlockSpec(memory_space=pltpu.VMEM),
        scratch_shapes=[pltpu.VMEM((N_DEV, rows_n, LANES), jnp.float32),
                        pltpu.SemaphoreType.DMA((N_DEV - 1,)), pltpu.SemaphoreType.DMA((N_DEV - 1,))],
    )(v)


def _adamw_math(w, g, m, v):
    m = ADAM_B1 * m + (1.0 - ADAM_B1) * g
    v = ADAM_B2 * v + (1.0 - ADAM_B2) * (g * g)
    m_hat = m / (1.0 - ADAM_B1 ** ADAM_STEP)
    v_hat = v / (1.0 - ADAM_B2 ** ADAM_STEP)
    delta = -ADAM_LR * (m_hat / (jnp.sqrt(v_hat) + ADAM_EPS) + ADAM_WD * w)
    return delta, m, v


def _rs_sum(owns, recvs):
    n = len(owns)

    def body(*refs):
        own_refs, recv_refs, out_refs = refs[:n], refs[n:4 * n], refs[4 * n:]
        for i in range(n):
            r0, r1, r2 = recv_refs[3 * i:3 * i + 3]
            out_refs[i][...] = ((own_refs[i][...] + _f32(r0[0])) + _f32(r1[0])) + _f32(r2[0])

    def row(a):
        return pl.BlockSpec((a.shape[0] // RS_ROW_BLOCKS, a.shape[1]), lambda r: (r, 0))

    def slot(a, k):
        return pl.BlockSpec((1, a.shape[0] // RS_ROW_BLOCKS, a.shape[1]), lambda r: (k, r, 0))

    return pl.pallas_call(
        body, name="rs_sum", grid=(RS_ROW_BLOCKS,),
        in_specs=[row(a) for a in owns] + [slot(a, k) for a in owns for k in range(3)],
        out_specs=[row(a) for a in owns],
        out_shape=[_sds(a.shape, jnp.float32) for a in owns],
        compiler_params=_cparams(("parallel",)),
    )(*owns, *[r for r in recvs for _ in range(3)])


def _adamw(g, w, m, v, name):
    depth, a, b = w.shape

    def body(g_ref, w_ref, m_ref, v_ref, d_out, m_out, v_out):
        delta, m2, v2 = _adamw_math(w_ref[...], g_ref[...], m_ref[...], v_ref[...])
        d_out[...] = delta
        m_out[...] = m2
        v_out[...] = v2

    layer = pl.BlockSpec((1, a, b), lambda l: (l, 0, 0))
    return pl.pallas_call(
        body, name="adamw_" + name, grid=(depth,),
        in_specs=[layer] * 4, out_specs=[layer] * 3,
        out_shape=[_sds(w.shape, jnp.float32)] * 3,
        compiler_params=_cparams(("parallel",)),
    )(g, w, m, v)


def _adamw_small(g, w, m, v):
    def body(g_ref, w_ref, m_ref, v_ref, d_out, m_out, v_out):
        delta, m2, v2 = _adamw_math(w_ref[...], g_ref[...], m_ref[...], v_ref[...])
        d_out[...] = delta
        m_out[...] = m2
        v_out[...] = v2

    vm = pl.BlockSpec(memory_space=pltpu.VMEM)
    return pl.pallas_call(
        body, name="adamw_small",
        in_specs=[vm] * 4, out_specs=[vm] * 3,
        out_shape=[_sds(g.shape, jnp.float32)] * 3,
    )(g, w, m, v)


def _f_chunk(f):
    for cand in (1408, 1024, 512, 256, 128):
        if f % cand == 0:
            return cand
    return f


def _ffn_fwd(x, gain, wg, wu, wd, sides=()):
    t, d = x.shape
    f = wg.shape[0]
    tb = min(_TB, t)
    fc = _f_chunk(f)

    def body(x_ref, g_ref, wg_ref, wu_ref, wd_ref, xo_ref, h_ref, s_ref, fa_ref, fu_ref):
        xv = x_ref[...]
        hb = _mx(xv * _rsq(_sumsq(xv), d) * g_ref[...])
        h_ref[...] = hb
        y = jnp.zeros((tb, d), jnp.float32)
        for c0 in range(0, f, fc):
            a = _dot_nt(hb, wg_ref[c0:c0 + fc, :])
            u = _dot_nt(hb, wu_ref[c0:c0 + fc, :])
            sig = jax.nn.sigmoid(a)
            silu = a * sig
            s = _mx(silu * u)
            s_ref[:, c0:c0 + fc] = s
            fa_ref[:, c0:c0 + fc] = _mx(u * (sig * (1.0 + a * (1.0 - sig))))
            fu_ref[:, c0:c0 + fc] = _mx(silu)
            y = y + _dot(s, wd_ref[c0:c0 + fc, :])
        xo_ref[...] = xv + 0.5 * y

    return _call(
        body, (x, gain, wg, wu, wd), sides, name="ffn_fwd", grid=(t // tb,),
        in_specs=[_rows(tb, d), _const((1, d)), _const((f, d)), _const((f, d)), _const((f, d))],
        out_specs=[_rows(tb, d), _rows(tb, d), _rows(tb, f), _rows(tb, f), _rows(tb, f)],
        out_shape=[_sds((t, d), jnp.float32), _sds((t, d), _MXU), _sds((t, f), _MXU), _sds((t, f), _MXU),
                   _sds((t, f), _MXU)],
        semantics=("parallel",))


def _ffn_dgrad(x, gain, dxo, fa, fu, wg, wu, wd, sides=()):
    t, d = x.shape
    f = wg.shape[0]
    tb = min(_TB, t)
    fc = _f_chunk(f)

    def body(x_ref, g_ref, dxo_ref, fa_ref, fu_ref, wg_ref, wu_ref, wd_ref, dxi_ref, da_ref, du_ref, dy_ref, dg_ref):
        xv = x_ref[...]
        gv = g_ref[...]
        r = _rsq(_sumsq(xv), d)
        xhat = xv * r
        dxo = dxo_ref[...]
        dyb = _mx(0.5 * dxo)
        dy_ref[...] = dyb
        dh = jnp.zeros((tb, d), jnp.float32)
        for c0 in range(0, f, fc):
            ds = _dot_nt(dyb, wd_ref[c0:c0 + fc, :])
            da = _mx(ds * _f32(fa_ref[:, c0:c0 + fc]))
            du = _mx(ds * _f32(fu_ref[:, c0:c0 + fc]))
            da_ref[:, c0:c0 + fc] = da
            du_ref[:, c0:c0 + fc] = du
            dh = dh + _dot(da, wg_ref[c0:c0 + fc, :]) + _dot(du, wu_ref[c0:c0 + fc, :])

        @pl.when(pl.program_id(0) == 0)
        def _():
            dg_ref[...] = jnp.zeros_like(dg_ref)

        dg_ref[...] += _colsum(dh * xhat)
        dn = dh * gv
        dxi_ref[...] = dxo + r * (dn - xhat * (_rowsum(dn * xhat) * (1.0 / d)))

    return _call(
        body, (x, gain, dxo, fa, fu, wg, wu, wd), sides, name="ffn_dgrad", grid=(t // tb,),
        in_specs=[_rows(tb, d), _const((1, d)), _rows(tb, d), _rows(tb, f), _rows(tb, f),
                  _const((f, d)), _const((f, d)), _const((f, d))],
        out_specs=[_rows(tb, d), _rows(tb, f), _rows(tb, f), _rows(tb, d), _acc((1, d))],
        out_shape=[_sds((t, d), jnp.float32), _sds((t, f), _MXU), _sds((t, f), _MXU), _sds((t, d), _MXU),
                   _sds((1, d), jnp.float32)],
        semantics=("arbitrary",))


def _tn_matmul(a, b, name, sides=()):
    t, m = a.shape
    n = b.shape[1]
    tk = min(_TK, t)
    tn = n
    while m * tn * 4 > 12 * 1024 * 1024 and tn % 256 == 0:
        tn //= 2

    def body(a_ref, b_ref, o_ref):
        @pl.when(pl.program_id(1) == 0)
        def _():
            o_ref[...] = jnp.zeros_like(o_ref)

        o_ref[...] += _dot_tn(a_ref[...], b_ref[...])

    (out,), side_outs = _call(
        body, (a, b), sides, name=name, grid=(n // tn, t // tk),
        in_specs=[pl.BlockSpec((tk, m), lambda j, k: (k, 0)), pl.BlockSpec((tk, tn), lambda j, k: (k, j))],
        out_specs=[pl.BlockSpec((m, tn), lambda j, k: (0, j))],
        out_shape=[_sds((m, n), jnp.float32)],
        semantics=("parallel", "arbitrary"))
    return out, side_outs


PREP_WEIGHTS = ("mix_g", "w_in", "g_qa", "wqb", "g_kva", "w_kvb", "gq_n", "gq_r", "gk_n", "gk_r", "g_sq", "g_sk")
C_CQ, C_CKV, C_KPE, C_QS = 0, MLA_Q_RANK, MLA_Q_RANK + MLA_KV_RANK, MLA_Q_RANK + MLA_KV_RANK + LANES
C_KS = C_QS + SWA_HEADS * SWA_D
C_VS = C_KS + LANES
W_IN_PACKED = C_VS + LANES


def _prep_specs(p):
    return [_const(p[n].shape) for n in PREP_WEIGHTS]


def _pair_norm_rope(t, gain, cos, sin_s):
    return _rope(t * _rsq(_half_sums(t * t), SWA_D) * gain, cos, sin_s)


def _prep_fwd(x, cos, sin_s, p):
    t, d = x.shape
    tb = min(_TB_MIX, t)

    def body(x_ref, cos_ref, sin_ref, mix_g, w_in, g_qa, wqb, g_kva, w_kvb, gq_n, gq_r, gk_n, gk_r, g_sq, g_sk,
             qa_ref, ka_ref, va_ref, qb_ref, kb_ref, vb_ref):
        xv = x_ref[...]
        cos_v, sin_v = cos_ref[...], sin_ref[...]
        hb = _mx(xv * _rsq(_sumsq(xv), d) * mix_g[...])
        proj = _dot_nt(hb, w_in[...])
        cq = proj[:, C_CQ:C_CKV]
        cqn = _mx(cq * _rsq(_sumsq(cq), MLA_Q_RANK) * g_qa[...])
        for h in range(MLA_HEADS):
            qh = _dot_nt(cqn, wqb[h])
            qn, qr = qh[:, :MLA_NOPE], qh[:, MLA_NOPE:]
            rh = _rsq(_sumsq(qn) + _sumsq(qr), MLA_QK)
            qa_ref[h, :, 0:MLA_NOPE] = (qn * rh * gq_n[...]).astype(qa_ref.dtype)
            qa_ref[h, :, MLA_NOPE:MLA_QK_PAD] = _rope(qr * rh * gq_r[...], cos_v, sin_v).astype(qa_ref.dtype)
        ckv = proj[:, C_CKV:C_KPE]
        ckvn = _mx(ckv * _rsq(_sumsq(ckv), MLA_KV_RANK) * g_kva[...])
        kpe = proj[:, C_KPE:C_QS]
        ss_pe = _sumsq(kpe)
        kv = _dot_nt(ckvn, w_kvb[...])
        for h in range(MLA_HEADS):
            c0 = h * (MLA_NOPE + MLA_V)
            kn = kv[:, c0:c0 + MLA_NOPE]
            rh = _rsq(_sumsq(kn) + ss_pe, MLA_QK)
            ka_ref[h, :, 0:MLA_NOPE] = (kn * rh * gk_n[...]).astype(ka_ref.dtype)
            ka_ref[h, :, MLA_NOPE:MLA_QK_PAD] = _rope(kpe * rh * gk_r[...], cos_v, sin_v).astype(ka_ref.dtype)
            va_ref[h] = kv[:, c0 + MLA_NOPE:c0 + MLA_NOPE + MLA_V].astype(va_ref.dtype)
        for j in range(SWA_PAIRS):
            c0 = C_QS + j * LANES
            qb_ref[j] = _pair_norm_rope(proj[:, c0:c0 + LANES], g_sq[...], cos_v, sin_v).astype(qb_ref.dtype)
        k0, k1 = _dup_halves(_pair_norm_rope(proj[:, C_KS:C_VS], g_sk[...], cos_v, sin_v))
        kb_ref[0] = k0.astype(kb_ref.dtype)
        kb_ref[1] = k1.astype(kb_ref.dtype)
        v0, v1 = _dup_halves(proj[:, C_VS:W_IN_PACKED])
        vb_ref[0] = v0.astype(vb_ref.dtype)
        vb_ref[1] = v1.astype(vb_ref.dtype)

    return pl.pallas_call(
        body, name="prep_fwd", grid=(t // tb,),
        in_specs=[_rows(tb, d), _rows(tb, LANES), _rows(tb, LANES)] + _prep_specs(p),
        out_specs=[_heads_rows(MLA_HEADS, tb, MLA_QK_PAD), _heads_rows(MLA_HEADS, tb, MLA_QK_PAD),
                   _heads_rows(MLA_HEADS, tb, MLA_V), _heads_rows(SWA_PAIRS, tb, LANES),
                   _heads_rows(SWA_KV, tb, LANES), _heads_rows(SWA_KV, tb, LANES)],
        out_shape=[_sds((MLA_HEADS, t, MLA_QK_PAD), _MXU), _sds((MLA_HEADS, t, MLA_QK_PAD), _MXU),
                   _sds((MLA_HEADS, t, MLA_V), _MXU), _sds((SWA_PAIRS, t, LANES), _MXU),
                   _sds((SWA_KV, t, LANES), _MXU), _sds((SWA_KV, t, LANES), _MXU)],
        compiler_params=_cparams(("parallel",)),
    )(x, cos, sin_s, *[p[n] for n in PREP_WEIGHTS])


def _prep_bwd(x, dxin, cos, sin_s, p, dqa, dka, dva, dqb, dkb, dvb, sides=()):
    t, d = x.shape
    tb = min(_TB_MIX, t)
    n_w = len(PREP_WEIGHTS)

    def body(*refs):
        x_ref, dxin_ref, cos_ref, sin_ref = refs[:4]
        mix_g, w_in, g_qa, wqb, g_kva, w_kvb, gq_n, gq_r, gk_n, gk_r, g_sq, g_sk = refs[4:4 + n_w]
        dqa_ref, dka_ref, dva_ref, dqb_ref, dkb_ref, dvb_ref = refs[4 + n_w:10 + n_w]
        dx_ref = refs[10 + n_w]
        grads = dict(zip(PREP_WEIGHTS, refs[11 + n_w:11 + 2 * n_w]))
        dproj_ref, dkv_ref, dqh_ref = refs[11 + 2 * n_w:]

        @pl.when(pl.program_id(0) == 0)
        def _():
            for ref in grads.values():
                ref[...] = jnp.zeros_like(ref)

        xv = x_ref[...]
        cos_v, sin_v = cos_ref[...], sin_ref[...]
        r0 = _rsq(_sumsq(xv), d)
        xhat = xv * r0
        hb = _mx(xhat * mix_g[...])
        proj = _dot_nt(hb, w_in[...])

        cq = proj[:, C_CQ:C_CKV]
        rq = _rsq(_sumsq(cq), MLA_Q_RANK)
        cqh = cq * rq
        cqn = _mx(cqh * g_qa[...])
        dcqn = jnp.zeros((tb, MLA_Q_RANK), jnp.float32)
        for h in range(MLA_HEADS):
            qh = _dot_nt(cqn, wqb[h])
            qn, qr = qh[:, :MLA_NOPE], qh[:, MLA_NOPE:]
            rh = _rsq(_sumsq(qn) + _sumsq(qr), MLA_QK)
            xh_n, xh_r = qn * rh, qr * rh
            dy_n = dqa_ref[h, :, 0:MLA_NOPE]
            dy_r = _rope_bwd(dqa_ref[h, :, MLA_NOPE:MLA_QK_PAD], cos_v, sin_v)
            grads["gq_n"][...] += _colsum(dy_n * xh_n)
            grads["gq_r"][...] += _colsum(dy_r * xh_r)
            dqn, dqr = _norm_bwd([dy_n * gq_n[...], dy_r * gq_r[...]], [xh_n, xh_r], rh, MLA_QK)
            dqh_ref[:, 0:MLA_NOPE] = _mx(dqn)
            dqh_ref[:, MLA_NOPE:MLA_QK_PAD] = _mx(dqr)
            dqh = dqh_ref[...]
            grads["wqb"][h] += _dot_tn(dqh, cqn)
            dcqn = dcqn + _dot(dqh, wqb[h])
        grads["g_qa"][...] += _colsum(dcqn * cqh)
        (dcq,) = _norm_bwd([dcqn * g_qa[...]], [cqh], rq, MLA_Q_RANK)
        dproj_ref[:, C_CQ:C_CKV] = _mx(dcq)

        ckv = proj[:, C_CKV:C_KPE]
        rkv = _rsq(_sumsq(ckv), MLA_KV_RANK)
        ckvh = ckv * rkv
        ckvn = _mx(ckvh * g_kva[...])
        kpe = proj[:, C_KPE:C_QS]
        ss_pe = _sumsq(kpe)
        kv = _dot_nt(ckvn, w_kvb[...])
        dkpe = jnp.zeros((tb, LANES), jnp.float32)
        for h in range(MLA_HEADS):
            c0 = h * (MLA_NOPE + MLA_V)
            c1 = c0 + MLA_NOPE
            kn = kv[:, c0:c1]
            rh = _rsq(_sumsq(kn) + ss_pe, MLA_QK)
            xh_n, xh_r = kn * rh, kpe * rh
            dy_n = dka_ref[h, :, 0:MLA_NOPE]
            dy_r = _rope_bwd(dka_ref[h, :, MLA_NOPE:MLA_QK_PAD], cos_v, sin_v)
            grads["gk_n"][...] += _colsum(dy_n * xh_n)
            grads["gk_r"][...] += _colsum(dy_r * xh_r)
            dkn, dkr = _norm_bwd([dy_n * gk_n[...], dy_r * gk_r[...]], [xh_n, xh_r], rh, MLA_QK)
            dkpe = dkpe + dkr
            dkv_ref[:, c0:c1] = _mx(dkn)
            dkv_ref[:, c1:c1 + MLA_V] = _mx(dva_ref[h])
        dkv = dkv_ref[...]
        grads["w_kvb"][...] += _dot_tn(dkv, ckvn)
        dckvn = _dot(dkv, w_kvb[...])
        grads["g_kva"][...] += _colsum(dckvn * ckvh)
        (dckv,) = _norm_bwd([dckvn * g_kva[...]], [ckvh], rkv, MLA_KV_RANK)
        dproj_ref[:, C_CKV:C_KPE] = _mx(dckv)
        dproj_ref[:, C_KPE:C_QS] = _mx(dkpe)

        def pair_bwd(tv, dy, g_ref, gname):
            r = _rsq(_half_sums(tv * tv), SWA_D)
            xh = tv * r
            dpre = _rope_bwd(dy, cos_v, sin_v)
            grads[gname][...] += _colsum(dpre * xh)
            dn = dpre * g_ref[...]
            return r * (dn - xh * (_half_sums(dn * xh) * (1.0 / SWA_D)))

        for j in range(SWA_PAIRS):
            c0 = C_QS + j * LANES
            dproj_ref[:, c0:c0 + LANES] = _mx(pair_bwd(proj[:, c0:c0 + LANES], dqb_ref[j], g_sq, "g_sq"))
        dproj_ref[:, C_KS:C_VS] = _mx(pair_bwd(proj[:, C_KS:C_VS], _undup_halves(dkb_ref[0], dkb_ref[1]), g_sk, "g_sk"))
        dproj_ref[:, C_VS:W_IN_PACKED] = _mx(_undup_halves(dvb_ref[0], dvb_ref[1]))

        dproj = dproj_ref[...]
        grads["w_in"][...] += _dot_tn(dproj, hb)
        dh = _dot(dproj, w_in[...])
        grads["mix_g"][...] += _colsum(dh * xhat)
        (dxv,) = _norm_bwd([dh * mix_g[...]], [xhat], r0, d)
        dx_ref[...] = dxin_ref[...] + dxv

    grad_shapes = [p[n].shape for n in PREP_WEIGHTS]
    args = (x, dxin, cos, sin_s, *[p[n] for n in PREP_WEIGHTS], dqa, dka, dva, dqb, dkb, dvb)
    return _call(
        body, args, sides, name="prep_bwd", grid=(t // tb,),
        in_specs=[_rows(tb, d), _rows(tb, d), _rows(tb, LANES), _rows(tb, LANES)] + _prep_specs(p) + [
            _heads_rows(MLA_HEADS, tb, MLA_QK_PAD), _heads_rows(MLA_HEADS, tb, MLA_QK_PAD),
            _heads_rows(MLA_HEADS, tb, MLA_V), _heads_rows(SWA_PAIRS, tb, LANES),
            _heads_rows(SWA_KV, tb, LANES), _heads_rows(SWA_KV, tb, LANES)],
        out_specs=[_rows(tb, d)] + [_acc(s) for s in grad_shapes],
        out_shape=[_sds((t, d), jnp.float32)] + [_sds(s, jnp.float32) for s in grad_shapes],
        scratch_shapes=[pltpu.VMEM((tb, W_IN_PACKED), _MXU), pltpu.VMEM((tb, MLA_HEADS * (MLA_NOPE + MLA_V)), _MXU),
                        pltpu.VMEM((tb, MLA_QK_PAD), _MXU)],
        semantics=("arbitrary",))


def _strips(n):
    step = min(_STRIP, n)
    return [slice(r, r + step) for r in range(0, n, step)]


def _mla_fwd(q, k, v, sides=()):
    hn, t, dq = q.shape
    dv = v.shape[2]
    bq = min(_BQ, t)
    scale = MLA_QK ** -0.5
    scale2 = scale * LOG2_E

    def body(q_ref, k_ref, v_ref, o_ref, l_ref):
        i = pl.program_id(1)
        qv = q_ref[0]

        def step(first_block, width, carry, masked):
            m, l, acc = carry
            start = pl.multiple_of(first_block * bq, bq)
            s = _dot_nt(qv, k_ref[0, pl.ds(start, width), :])
            if masked:
                row = lax.broadcasted_iota(jnp.int32, (bq, width), 0)
                col = lax.broadcasted_iota(jnp.int32, (bq, width), 1)
                s = jnp.where(col <= row, s, NEG)
            m_new = jnp.maximum(m, _rowmax(s))
            alpha = jnp.exp2((m - m_new) * scale2)
            pv = jnp.exp2((s - m_new) * scale2)
            l = alpha * l + _rowsum(pv)
            acc = alpha * acc + _dot(_mx(pv), v_ref[0, pl.ds(start, width), :])
            return m_new, l, acc

        init = (jnp.full((bq, 1), NEG, jnp.float32), jnp.zeros((bq, 1), jnp.float32), jnp.zeros((bq, dv), jnp.float32))
        carry, done = init, 0
        for group in (4, 2, 1):
            count = (i - done) // group
            carry = lax.fori_loop(0, count, lambda g, c, done=done, group=group: step(done + group * g, group * bq, c, False), carry)
            done = done + group * count
        m, l, acc = step(i, bq, carry, True)
        o_ref[0] = acc / l
        l_ref[0] = jnp.broadcast_to(m * scale + jnp.log(l), (bq, LANES))

    return _call(
        body, (q, k, v), sides, name="mla_fwd", grid=(hn, t // bq),
        in_specs=[pl.BlockSpec((1, bq, dq), lambda h, i: (h, i, 0)),
                  pl.BlockSpec((1, t, dq), lambda h, i: (h, 0, 0)),
                  pl.BlockSpec((1, t, dv), lambda h, i: (h, 0, 0))],
        out_specs=[pl.BlockSpec((1, bq, dv), lambda h, i: (h, i, 0)),
                   pl.BlockSpec((1, bq, LANES), lambda h, i: (h, i, 0))],
        out_shape=[_sds((hn, t, dv), jnp.float32), _sds((hn, t, LANES), jnp.float32)],
        semantics=("parallel", "arbitrary"))


def _mla_bwd(q, k, v, do, lse_rows, dsum_rows, sides=()):
    hn, t, dq_w = q.shape
    dv_w = v.shape[2]
    bq = min(_BQ, t)
    nb = t // bq
    wide = max(_BWD_GROUPS) * bq
    scale = MLA_QK ** -0.5

    def body(q_ref, do_ref, l_ref, d_ref, k_ref, v_ref, dq_ref, dk_ref, dv_ref, st_scr, dpt_scr, p_scr, ds_scr):
        j = pl.program_id(1)

        @pl.when(j == 0)
        def _():
            dq_ref[...] = jnp.zeros_like(dq_ref)

        kv = k_ref[0]
        vv = v_ref[0]
        dk_ref[0] = jnp.zeros((bq, dq_w), jnp.float32)
        dv_ref[0] = jnp.zeros((bq, dv_w), jnp.float32)

        def tile(first_block, n_blk, masked):
            width = n_blk * bq
            start = pl.multiple_of(first_block * bq, bq)
            qv = q_ref[0, pl.ds(start, width), :]
            dov = do_ref[0, pl.ds(start, width), :]
            st_scr[:, :width] = _dot_nt(kv, qv)
            dpt_scr[:, :width] = _dot_nt(vv, dov)
            lse2 = jnp.concatenate([l_ref[0, first_block + b] for b in range(n_blk)], axis=1) * LOG2_E
            dsum = jnp.concatenate([d_ref[0, first_block + b] for b in range(n_blk)], axis=1)
            for rows in _strips(bq):
                pt = jnp.exp2(st_scr[rows, :width] * (scale * LOG2_E) - lse2)
                if masked:
                    n_rows = rows.stop - rows.start
                    row = lax.broadcasted_iota(jnp.int32, (n_rows, width), 0) + rows.start
                    col = lax.broadcasted_iota(jnp.int32, (n_rows, width), 1)
                    pt = jnp.where(row <= col, pt, 0.0)
                p_scr[rows, :width] = _mx(pt)
                ds_scr[rows, :width] = _mx(pt * (dpt_scr[rows, :width] - dsum) * scale)
            ds_t = ds_scr[:, :width]
            dv_ref[0] += _dot(p_scr[:, :width], dov)
            dk_ref[0] += _dot(ds_t, qv)
            dq_ref[0, pl.ds(start, width), :] += _dot_tn(ds_t, kv)

        def group_loop(first_block, n_blk, count):
            def loop_body(g, carry):
                tile(first_block + n_blk * g, n_blk, False)
                return carry

            lax.fori_loop(0, count, loop_body, 0)

        tile(j, 1, True)
        done = j + 1
        for n_blk in _BWD_GROUPS:
            count = (nb - done) // n_blk
            group_loop(done, n_blk, count)
            done = done + n_blk * count

    return _call(
        body, (q, do, lse_rows, dsum_rows, k, v), sides, name="mla_bwd", grid=(hn, nb),
        in_specs=[pl.BlockSpec((1, t, dq_w), lambda h, j: (h, 0, 0)),
                  pl.BlockSpec((1, t, dv_w), lambda h, j: (h, 0, 0)),
                  pl.BlockSpec((1, nb, 1, bq), lambda h, j: (h, 0, 0, 0)),
                  pl.BlockSpec((1, nb, 1, bq), lambda h, j: (h, 0, 0, 0)),
                  pl.BlockSpec((1, bq, dq_w), lambda h, j: (h, j, 0)),
                  pl.BlockSpec((1, bq, dv_w), lambda h, j: (h, j, 0))],
        out_specs=[pl.BlockSpec((1, t, dq_w), lambda h, j: (h, 0, 0)),
                   pl.BlockSpec((1, bq, dq_w), lambda h, j: (h, j, 0)),
                   pl.BlockSpec((1, bq, dv_w), lambda h, j: (h, j, 0))],
        out_shape=[_sds((hn, t, dq_w), jnp.float32), _sds((hn, t, dq_w), jnp.float32), _sds((hn, t, dv_w), jnp.float32)],
        scratch_shapes=[pltpu.VMEM((bq, wide), jnp.float32), pltpu.VMEM((bq, wide), jnp.float32),
                        pltpu.VMEM((bq, wide), _MXU), pltpu.VMEM((bq, wide), _MXU)],
        semantics=("parallel", "arbitrary"))


STACK = SWA_GROUP * SWA_BLOCK


def _swa_stack(ref, c, rows):
    low = _low_half()
    parts = []
    for g in range(SWA_GROUP):
        tv = ref[SWA_GROUP // 2 * c + g // 2, rows, :]
        keep = low if g % 2 == 0 else jnp.logical_not(low)
        parts.append(_mx(jnp.where(keep, tv, jnp.zeros_like(tv))))
    return jnp.concatenate(parts, axis=0)


def _swa_cols(ref, c, rows):
    return jnp.concatenate([ref[SWA_GROUP * c + g, rows, 0:1] for g in range(SWA_GROUP)], axis=0)


def _swa_sink_col(s_ref, c):
    return jnp.concatenate([jnp.broadcast_to(s_ref[SWA_GROUP * c + g][:, 0:1], (SWA_BLOCK, 1))
                            for g in range(SWA_GROUP)], axis=0)


def _swa_band_masks():
    row = lax.broadcasted_iota(jnp.int32, (STACK, SWA_BLOCK), 0) & (SWA_BLOCK - 1)
    col = lax.broadcasted_iota(jnp.int32, (STACK, SWA_BLOCK), 1)
    return col <= row, col > row


def _swa_band(has_previous):
    row = lax.broadcasted_iota(jnp.int32, (STACK, 2 * SWA_BLOCK), 0) & (SWA_BLOCK - 1)
    col = lax.broadcasted_iota(jnp.int32, (STACK, 2 * SWA_BLOCK), 1)
    before = jnp.logical_and(col < SWA_BLOCK, col > row)
    if has_previous is not True:
        before = jnp.logical_and(before, has_previous)
    return jnp.logical_or(before, jnp.logical_and(col >= SWA_BLOCK, col - SWA_BLOCK <= row))


def _swa_keys(ref, prev_ref, c, b):
    if b == 0:
        return jnp.concatenate([prev_ref[c], ref[c, 0:SWA_BLOCK, :]], axis=0)
    return ref[c, (b - 1) * SWA_BLOCK:(b + 1) * SWA_BLOCK, :]


def _swa_unstack_pairs(ref, c, rows, stacked):
    for pr in range(SWA_GROUP // 2):
        r0 = 2 * pr * SWA_BLOCK
        ref[SWA_GROUP // 2 * c + pr, rows, :] = _pick_halves(stacked[r0:r0 + SWA_BLOCK], stacked[r0 + SWA_BLOCK:r0 + 2 * SWA_BLOCK])


def _swa_blocks(t):
    nblk = t // SWA_BLOCK
    bps = min(_SWA_STEP, nblk)
    return nblk, bps, bps * SWA_BLOCK


def _swa_fwd(q, k, v, sinks):
    _, t, _ = q.shape
    nblk, bps, sb = _swa_blocks(t)
    scale = SWA_D ** -0.5

    def body(q_ref, k_ref, kp_ref, v_ref, vp_ref, s_ref, o_ref, l_ref):
        n = pl.program_id(0)
        band_first, band = _swa_band(n > 0), _swa_band(True)
        for c in range(SWA_KV):
            sink = _swa_sink_col(s_ref, c)
            for b in range(bps):
                rows = slice(b * SWA_BLOCK, (b + 1) * SWA_BLOCK)
                qs = _swa_stack(q_ref, c, rows)
                s = jnp.where(band_first if b == 0 else band, _dot_nt(qs, _swa_keys(k_ref, kp_ref, c, b)) * scale, NEG)
                m = jnp.maximum(_rowmax(s), sink)
                e = jnp.exp(s - m)
                denom = _rowsum(e) + jnp.exp(sink - m)
                o = _dot(_mx(e * (1.0 / denom)), _swa_keys(v_ref, vp_ref, c, b))
                lse = m + jnp.log(denom)
                for g in range(SWA_GROUP):
                    l_ref[SWA_GROUP * c + g, rows, :] = jnp.broadcast_to(
                        lse[g * SWA_BLOCK:(g + 1) * SWA_BLOCK], (SWA_BLOCK, LANES))
                _swa_unstack_pairs(o_ref, c, rows, o)

    main = lambda n: (0, n, 0)
    prev = lambda n: (0, jnp.maximum(n * bps - 1, 0), 0)
    return pl.pallas_call(
        body, name="swa_fwd", grid=(nblk // bps,),
        in_specs=[pl.BlockSpec((SWA_PAIRS, sb, LANES), main),
                  pl.BlockSpec((SWA_KV, sb, LANES), main), pl.BlockSpec((SWA_KV, SWA_BLOCK, LANES), prev),
                  pl.BlockSpec((SWA_KV, sb, LANES), main), pl.BlockSpec((SWA_KV, SWA_BLOCK, LANES), prev),
                  _const((SWA_HEADS, 1, LANES))],
        out_specs=[pl.BlockSpec((SWA_PAIRS, sb, LANES), main), pl.BlockSpec((SWA_HEADS, sb, LANES), main)],
        out_shape=[_sds((SWA_PAIRS, t, LANES), jnp.float32), _sds((SWA_HEADS, t, LANES), jnp.float32)],
        compiler_params=_cparams(("parallel",)),
    )(q, k, k, v, v, sinks)


def _swa_bwd(q, k, v, sinks, do, lse, dsum):
    _, t, _ = q.shape
    nblk, bps, sb = _swa_blocks(t)
    steps = nblk // bps
    scale = SWA_D ** -0.5

    def body(q_ref, k_ref, kp_ref, v_ref, vp_ref, s_ref, do_ref, l_ref, d_ref, qn_ref, don_ref, ln_ref, dn_ref,
             dq_ref, dk_ref, dv_ref, ds_ref):
        n = pl.program_id(0)

        @pl.when(n == 0)
        def _():
            ds_ref[...] = jnp.zeros_like(ds_ref)

        _, m_prev = _swa_band_masks()
        band_first, band = _swa_band(n > 0), _swa_band(True)
        everything = slice(0, SWA_BLOCK)

        def probs(qs, keys, mask, lcol):
            return jnp.where(mask, jnp.exp(_dot_nt(qs, keys) * scale - lcol), 0.0)

        def dscores(pm, dos, vals, dcol):
            return _mx(pm * (_dot_nt(dos, vals) - dcol) * scale)

        for c in range(SWA_KV):
            sink = _swa_sink_col(s_ref, c)
            dk_acc = [jnp.zeros((SWA_BLOCK, LANES), jnp.float32) for _ in range(bps)]
            dv_acc = [jnp.zeros((SWA_BLOCK, LANES), jnp.float32) for _ in range(bps)]
            for b in range(bps):
                rows = slice(b * SWA_BLOCK, (b + 1) * SWA_BLOCK)
                keys, vals = _swa_keys(k_ref, kp_ref, c, b), _swa_keys(v_ref, vp_ref, c, b)
                qs = _swa_stack(q_ref, c, rows)
                dos = _swa_stack(do_ref, c, rows)
                lcol = _swa_cols(l_ref, c, rows)
                dcol = _swa_cols(d_ref, c, rows)
                pm = probs(qs, keys, band_first if b == 0 else band, lcol)
                ds = dscores(pm, dos, vals, dcol)
                _swa_unstack_pairs(dq_ref, c, rows, _dot(ds, keys))
                dk_both = _dot_tn(ds, qs)
                dv_both = _dot_tn(_mx(pm), dos)
                dk_acc[b] = dk_acc[b] + dk_both[SWA_BLOCK:]
                dv_acc[b] = dv_acc[b] + dv_both[SWA_BLOCK:]
                if b > 0:
                    dk_acc[b - 1] = dk_acc[b - 1] + dk_both[:SWA_BLOCK]
                    dv_acc[b - 1] = dv_acc[b - 1] + dv_both[:SWA_BLOCK]
                p_sink = jnp.exp(sink - lcol) * dcol
                for g in range(SWA_GROUP):
                    ds_ref[SWA_GROUP * c + g] += -jnp.sum(p_sink[g * SWA_BLOCK:(g + 1) * SWA_BLOCK])
            tail = slice((bps - 1) * SWA_BLOCK, bps * SWA_BLOCK)
            kc, vc = k_ref[c, tail, :], v_ref[c, tail, :]
            qs = _swa_stack(qn_ref, c, everything)
            dos = _swa_stack(don_ref, c, everything)
            lcol = _swa_cols(ln_ref, c, everything)
            dcol = _swa_cols(dn_ref, c, everything)
            p_p = probs(qs, kc, jnp.logical_and(m_prev, n < steps - 1), lcol)
            ds_p = dscores(p_p, dos, vc, dcol)
            dk_acc[bps - 1] = dk_acc[bps - 1] + _dot_tn(ds_p, qs)
            dv_acc[bps - 1] = dv_acc[bps - 1] + _dot_tn(_mx(p_p), dos)
            for b in range(bps):
                rows = slice(b * SWA_BLOCK, (b + 1) * SWA_BLOCK)
                dk_ref[c, rows, :] = dk_acc[b]
                dv_ref[c, rows, :] = dv_acc[b]

    main = lambda n: (0, n, 0)
    prev = lambda n: (0, jnp.maximum(n * bps - 1, 0), 0)
    nxt = lambda n: (0, jnp.minimum((n + 1) * bps, nblk - 1), 0)
    pairs = pl.BlockSpec((SWA_PAIRS, sb, LANES), main)
    kvs = pl.BlockSpec((SWA_KV, sb, LANES), main)
    kv_prev = pl.BlockSpec((SWA_KV, SWA_BLOCK, LANES), prev)
    stats = pl.BlockSpec((SWA_HEADS, sb, LANES), main)
    pairs_next = pl.BlockSpec((SWA_PAIRS, SWA_BLOCK, LANES), nxt)
    stats_next = pl.BlockSpec((SWA_HEADS, SWA_BLOCK, LANES), nxt)
    return pl.pallas_call(
        body, name="swa_bwd", grid=(steps,),
        in_specs=[pairs, kvs, kv_prev, kvs, kv_prev, _const((SWA_HEADS, 1, LANES)), pairs, stats, stats,
                  pairs_next, pairs_next, stats_next, stats_next],
        out_specs=[pairs, kvs, kvs, _acc((SWA_HEADS, 1, LANES))],
        out_shape=[_sds((SWA_PAIRS, t, LANES), jnp.float32), _sds((SWA_KV, t, LANES), jnp.float32),
                   _sds((SWA_KV, t, LANES), jnp.float32), _sds((SWA_HEADS, 1, LANES), jnp.float32)],
        compiler_params=_cparams(("arbitrary",)),
    )(q, k, k, v, v, sinks, do, lse, dsum, q, do, lse, dsum)


MIX_SLABS = 4
MIX_WIDTH = MIX_SLABS * LANES


def _mix_out_fwd(x, oa, ob, ga, gb, wo_a, wo_b):
    t, d = x.shape
    tb = min(_TB, t)

    def body(x_ref, oa_ref, ob_ref, ga_ref, gb_ref, woa_ref, wob_ref, xo_ref):
        y = x_ref[...]
        for o_ref, g_ref, w_ref in ((oa_ref, ga_ref, woa_ref), (ob_ref, gb_ref, wob_ref)):
            r = _rsq(sum(_sumsq(o_ref[h]) for h in range(MIX_SLABS)), MIX_WIDTH)
            for h in range(MIX_SLABS):
                y = y + _dot(_mx(o_ref[h] * r * g_ref[h]), w_ref[h])
        xo_ref[...] = y

    slab = _heads_rows(MIX_SLABS, tb, LANES)
    return pl.pallas_call(
        body, name="mix_out_fwd", grid=(t // tb,),
        in_specs=[_rows(tb, d), slab, slab, _const(ga.shape), _const(gb.shape), _const(wo_a.shape), _const(wo_b.shape)],
        out_specs=_rows(tb, d),
        out_shape=_sds((t, d), jnp.float32),
        compiler_params=_cparams(("parallel",)),
    )(x, oa, ob, ga, gb, wo_a, wo_b)


def _mix_out_bwd(dx, oa, ob, ga, gb, wo_a, wo_b):
    t, d = dx.shape
    tb = min(_TB, t)

    def group(o_ref, g_ref, w_ref, dyb, do_ref, n_ref, col0, dg_ref):
        r = _rsq(sum(_sumsq(o_ref[h]) for h in range(MIX_SLABS)), MIX_WIDTH)
        xh, dn = [], []
        for h in range(MIX_SLABS):
            xh.append(o_ref[h] * r)
            n_ref[:, col0 + h * LANES:col0 + (h + 1) * LANES] = _mx(xh[h] * g_ref[h])
            dm = _dot_nt(dyb, w_ref[h])
            dg_ref[h] += _colsum(dm * xh[h])
            dn.append(dm * g_ref[h])
        c = sum(_rowsum(dn[h] * xh[h]) for h in range(MIX_SLABS)) * (1.0 / MIX_WIDTH)
        prods = []
        for h in range(MIX_SLABS):
            do = r * (dn[h] - xh[h] * c)
            do_ref[h] = do.astype(do_ref.dtype)
            prods.append(do * o_ref[h])
        return prods

    def body(dx_ref, oa_ref, ob_ref, ga_ref, gb_ref, woa_ref, wob_ref,
             doa_ref, dsa_ref, dob_ref, dsb_ref, n_ref, dy_ref, dga_ref, dgb_ref):
        @pl.when(pl.program_id(0) == 0)
        def _():
            dga_ref[...] = jnp.zeros_like(dga_ref)
            dgb_ref[...] = jnp.zeros_like(dgb_ref)

        dyb = _mx(dx_ref[...])
        dy_ref[...] = dyb
        for h, pr in enumerate(group(oa_ref, ga_ref, woa_ref, dyb, doa_ref, n_ref, 0, dga_ref)):
            dsa_ref[h] = jnp.broadcast_to(_rowsum(pr), (tb, LANES))
        low = _low_half()
        for j, pr in enumerate(group(ob_ref, gb_ref, wob_ref, dyb, dob_ref, n_ref, MIX_WIDTH, dgb_ref)):
            dsb_ref[2 * j] = jnp.broadcast_to(_rowsum(jnp.where(low, pr, 0.0)), (tb, LANES))
            dsb_ref[2 * j + 1] = jnp.broadcast_to(_rowsum(jnp.where(low, 0.0, pr)), (tb, LANES))

    slab = _heads_rows(MIX_SLABS, tb, LANES)
    return pl.pallas_call(
        body, name="mix_out_bwd", grid=(t // tb,),
        in_specs=[_rows(tb, d), slab, slab, _const(ga.shape), _const(gb.shape), _const(wo_a.shape), _const(wo_b.shape)],
        out_specs=[slab, slab, slab, _heads_rows(SWA_HEADS, tb, LANES), _rows(tb, 2 * MIX_WIDTH), _rows(tb, d),
                   _acc(ga.shape), _acc(gb.shape)],
        out_shape=[_sds((MIX_SLABS, t, LANES), _MXU), _sds((MIX_SLABS, t, LANES), jnp.float32),
                   _sds((MIX_SLABS, t, LANES), jnp.float32), _sds((SWA_HEADS, t, LANES), jnp.float32),
                   _sds((t, 2 * MIX_WIDTH), _MXU), _sds((t, d), _MXU),
                   _sds(ga.shape, jnp.float32), _sds(gb.shape, jnp.float32)],
        compiler_params=_cparams(("arbitrary",)),
    )(dx, oa, ob, ga, gb, wo_a, wo_b)


def _loss_head(y, target):
    t, d = y.shape
    tb = min(_TB, t)

    def body(y_ref, t_ref, dy_ref, acc_ref):
        @pl.when(pl.program_id(0) == 0)
        def _():
            acc_ref[...] = jnp.zeros_like(acc_ref)

        err = y_ref[...] - t_ref[...]
        dy_ref[...] = err * (1.0 / d)
        acc_ref[...] += jnp.sum(err * err)

    return pl.pallas_call(
        body, name="loss_head", grid=(t // tb,),
        in_specs=[_rows(tb, d), _rows(tb, d)],
        out_specs=[_rows(tb, d), _acc((8, LANES))],
        out_shape=[_sds((t, d), jnp.float32), _sds((8, LANES), jnp.float32)],
        compiler_params=_cparams(("arbitrary",)),
    )(y, target)


def _is_transposed(name):
    return name not in ROW_SHARDED


def _pack_layer(shards, l, width, names):
    rows = [(shards[n][l].T if _is_transposed(n) else shards[n][l]).reshape(-1, width) for n in names]
    return [jnp.concatenate(rows, axis=0)] if names is OTHER_BIG else rows


def _full_shape(like, name):
    _, a, b = like[name].shape
    return (N_DEV * b, a) if _is_transposed(name) else (N_DEV * a, b)


def _unpack_full(gathered, like, names):
    if names is not OTHER_BIG:
        return {n: g.reshape(_full_shape(like, n)) for n, g in zip(names, gathered)}
    (gathered,), out, off = gathered, {}, 0
    for n in names:
        rows_n = like[n][0].size // gathered.shape[-1]
        out[n] = gathered[:, off:off + rows_n].reshape(_full_shape(like, n))
        off += rows_n
    return out


def _stored(a, name):
    return jnp.swapaxes(a, 1, 2) if _is_transposed(name) else a


def _grads_by_destination(grads, width, names):
    by_dest = lambda n: grads[n].reshape(N_DEV, -1, width)
    if names is OTHER_BIG:
        return [jnp.concatenate([by_dest(n) for n in names], axis=1)]
    return [by_dest(n) for n in names]


def _shards_from_rows(rows, like):
    out = dict(zip(FFN_BIG, rows[:len(FFN_BIG)]))
    rest, off = rows[len(FFN_BIG)], 0
    for n in OTHER_BIG:
        _, a, b = like[n].shape
        rows_n = a * b // rest.shape[-1]
        out[n] = rest[off:off + rows_n].reshape((b, a) if _is_transposed(n) else (a, b))
        off += rows_n
    return out


def _small_rows(n_elems):
    return -(-n_elems // LANES)


def _pack_small(arrays):
    parts = []
    for n in SMALL:
        v = arrays[n]
        depth, width = v.shape
        padded = _small_rows(width) * LANES
        parts.append(jnp.pad(v, ((0, 0), (0, padded - width))).reshape(-1, LANES))
    packed = jnp.concatenate(parts, axis=0)
    return jnp.pad(packed, ((0, (-packed.shape[0]) % 8), (0, 0)))


def _unpack_small(packed, like):
    out, off = {}, 0
    for n in SMALL:
        depth, width = like[n].shape
        rows_n = _small_rows(width)
        seg = packed[off:off + depth * rows_n].reshape(depth, rows_n * LANES)
        out[n] = seg[:, :width]
        off += depth * rows_n
    return out


def _rope_tables(t):
    pos = jnp.arange(t, dtype=jnp.float32)
    inv = 1.0 / (ROPE_THETA ** (jnp.arange(0, MLA_ROPE, 2, dtype=jnp.float32) / MLA_ROPE))
    ang = pos[:, None] * inv[None, :]
    cos, sin = jnp.cos(ang), jnp.sin(ang)
    return jnp.concatenate([cos, cos, cos, cos], axis=1), jnp.concatenate([-sin, sin, -sin, sin], axis=1)


def _pad_lanes(a, width):
    return jnp.pad(a, [(0, 0)] * (a.ndim - 1) + [(0, width - a.shape[-1])])


def _ffn_params(full, small, l, tag):
    return small[tag + "_norm"][l][None, :], full[tag + "_w_gate"], full[tag + "_w_up"], full[tag + "_w_down"]


def _mixer_params(full, small, l):
    w_in = full["w_in"]
    d = w_in.shape[1]
    mla_rows = W_IN_COLS[0]
    w_in_p = jnp.concatenate([w_in[:mla_rows], jnp.zeros((LANES - MLA_ROPE, d), w_in.dtype), w_in[mla_rows:]], axis=0)
    wqb = full["mla_w_q_b"].reshape(MLA_HEADS, MLA_QK, MLA_Q_RANK)
    wqb = jnp.pad(wqb, ((0, 0), (0, MLA_QK_PAD - MLA_QK), (0, 0)))
    row = lambda name: small[name][l][None, :]
    twice = lambda g: jnp.concatenate([g, g], axis=1)
    prep = {
        "mix_g": row("mix_norm"), "w_in": w_in_p,
        "g_qa": row("mla_q_a_norm"), "wqb": wqb,
        "g_kva": row("mla_kv_a_norm"), "w_kvb": full["mla_w_kv_b"],
        "gq_n": row("mla_q_norm")[:, :MLA_NOPE], "gq_r": _pad_lanes(row("mla_q_norm")[:, MLA_NOPE:], LANES),
        "gk_n": row("mla_k_norm")[:, :MLA_NOPE], "gk_r": _pad_lanes(row("mla_k_norm")[:, MLA_NOPE:], LANES),
        "g_sq": twice(row("swa_q_norm")), "g_sk": twice(row("swa_k_norm")),
    }
    return {
        "prep": prep,
        "sinks": jnp.broadcast_to(small["swa_sinks"][l][:, None, None], (SWA_HEADS, 1, LANES)),
        "ga": small["mla_out_norm"][l].reshape(MIX_SLABS, 1, LANES),
        "gb": small["swa_out_norm"][l].reshape(MIX_SLABS, 1, LANES),
        "wo_a": full["w_o"][:MIX_WIDTH].reshape(MIX_SLABS, LANES, d),
        "wo_b": full["w_o"][MIX_WIDTH:].reshape(MIX_SLABS, LANES, d),
    }


def _ffn_backward(x_in, dxo, kept, params, tag, sides=()):
    gain, wg, wu, wd = params
    h, s, fa, fu = kept
    (dxi, da, du, dy, dg), side_out = _ffn_dgrad(x_in, gain, dxo, fa, fu, wg, wu, wd, sides)
    dwg, _ = _tn_matmul(da, h, "wgrad_" + tag + "_gate")
    dwu, _ = _tn_matmul(du, h, "wgrad_" + tag + "_up")
    dwd, _ = _tn_matmul(s, dy, "wgrad_" + tag + "_down")
    return dxi, dg[0], dwg, dwu, dwd, side_out


def _ffn_backward_exchanging(x_in, dxo, kept, params, tag, ready, chip_sums, width):
    gain, wg, wu, wd = params
    h, s, fa, fu = kept
    (dxi, da, du, dy, dg), (ready_sib,) = _ffn_dgrad(x_in, gain, dxo, fa, fu, wg, wu, wd, [_side_sibling(ready)])
    ready_owns, ready_wires = chip_sums(ready, ready_sib)
    dwd, (ready_recv,) = _tn_matmul(s, dy, "wgrad_" + tag + "_down", [_side_chips(ready_wires)])
    down = [dwd.reshape(N_DEV, -1, width)]
    dwg, (down_sib,) = _tn_matmul(da, h, "wgrad_" + tag + "_gate", [_side_sibling(down)])
    down_owns, down_wires = chip_sums(down, down_sib)
    dwu, (down_recv,) = _tn_matmul(du, h, "wgrad_" + tag + "_up", [_side_chips(down_wires)])
    return dxi, dg[0], dwg, dwu, _rs_sum(ready_owns, ready_recv), _rs_sum(down_owns, down_recv)


def kernel(x, ffn1_norm, ffn1_w_gate, ffn1_w_up, ffn1_w_down, mix_norm, w_in, mla_q_a_norm, mla_w_q_b, mla_kv_a_norm, mla_w_kv_b, mla_q_norm, mla_k_norm, swa_q_norm, swa_k_norm, swa_sinks, mla_out_norm, swa_out_norm, w_o, ffn2_norm, ffn2_w_gate, ffn2_w_up, ffn2_w_down, loss_target, m_ffn1_norm, m_ffn1_w_gate, m_ffn1_w_up, m_ffn1_w_down, m_mix_norm, m_w_in, m_mla_q_a_norm, m_mla_w_q_b, m_mla_kv_a_norm, m_mla_w_kv_b, m_mla_q_norm, m_mla_k_norm, m_swa_q_norm, m_swa_k_norm, m_swa_sinks, m_mla_out_norm, m_swa_out_norm, m_w_o, m_ffn2_norm, m_ffn2_w_gate, m_ffn2_w_up, m_ffn2_w_down, v_ffn1_norm, v_ffn1_w_gate, v_ffn1_w_up, v_ffn1_w_down, v_mix_norm, v_w_in, v_mla_q_a_norm, v_mla_w_q_b, v_mla_kv_a_norm, v_mla_w_kv_b, v_mla_q_norm, v_mla_k_norm, v_swa_q_norm, v_swa_k_norm, v_swa_sinks, v_mla_out_norm, v_swa_out_norm, v_w_o, v_ffn2_norm, v_ffn2_w_gate, v_ffn2_w_up, v_ffn2_w_down):
    local = dict(locals())
    w = {n: local[n] for n in WEIGHTS}
    m = {n: local["m_" + n] for n in WEIGHTS}
    v = {n: local["v_" + n] for n in WEIGHTS}
    depth = ffn1_norm.shape[0]
    t, d = x.shape[-2], x.shape[-1]
    x2d = x.reshape(t, d)
    target = loss_target.reshape(t, d)
    bq = min(_BQ, t)

    big = {n: w[n] for n in BIG}
    packed = [[[_mx(a) for a in _pack_layer(big, l, d, names)] for names in GATHER_ORDER] for l in range(depth)]
    cos, sin_s = _rope_tables(t)
    x_i, y_i, c_i = _position()
    dest_idx = jnp.stack([4 * px + 2 * py + c_i for px, py in _relations(x_i, y_i)]).astype(jnp.int32)

    params, saved = [], []
    xc = x2d
    ffn1_full = _unpack_full(_all_gather(packed[0][0]), big, FFN1_BIG)
    for l in range(depth):
        pr = {"ffn1": _ffn_params(ffn1_full, w, l, "ffn1")}
        x0 = xc
        (x1, *kept1), (mixer_gathered,) = _ffn_fwd(x0, *pr["ffn1"], sides=[_side_gather(packed[l][1])])
        pr.update(_mixer_params(_unpack_full(mixer_gathered, big, OTHER_BIG), w, l))
        qa, ka, va, qb, kb, vb = _prep_fwd(x1, cos, sin_s, pr["prep"])
        (oa, lse_a), (ffn2_gathered,) = _mla_fwd(qa, ka, va, sides=[_side_gather(packed[l][2])])
        pr["ffn2"] = _ffn_params(_unpack_full(ffn2_gathered, big, FFN2_BIG), w, l, "ffn2")
        ob, lse_b = _swa_fwd(qb, kb, vb, pr["sinks"])
        x2 = _mix_out_fwd(x1, oa, ob, pr["ga"], pr["gb"], pr["wo_a"], pr["wo_b"])
        (x3, *kept2), next_gathered = _ffn_fwd(
            x2, *pr["ffn2"], sides=[_side_gather(packed[l + 1][0])] if l + 1 < depth else [])
        if next_gathered:
            ffn1_full = _unpack_full(next_gathered[0], big, FFN1_BIG)
        params.append(pr)
        saved.append((x0, kept1, x1, qa, ka, va, qb, kb, vb, oa, lse_a, ob, lse_b, x2, kept2))
        xc = x3

    dx, sq_err = _loss_head(xc, target)
    loss = lax.psum(0.5 / d * sq_err[0, 0], MESH_AXES)

    def chip_sums(arrays, sibling_parts):
        sums = _rs_chip_sums(arrays, sibling_parts, dest_idx)
        return sums[:len(arrays)], sums[len(arrays):]

    grad_shards = [None] * depth
    small_grads = {n: [None] * depth for n in SMALL}
    upper = None
    for l in reversed(range(depth)):
        pr = params[l]
        lowest = l == 0
        x0, kept1, x1, qa, ka, va, qb, kb, vb, oa, lse_a, ob, lse_b, x2, kept2 = saved[l]
        g = {}
        dx, small_grads["ffn2_norm"][l], g["ffn2_w_gate"], g["ffn2_w_up"], g["ffn2_w_down"], side_out = _ffn_backward(
            x2, dx, kept2, pr["ffn2"], "ffn2", [_side_sibling(upper)] if upper else [])
        if upper:
            upper_owns, upper_wires = chip_sums(upper, side_out[0])
        early = _grads_by_destination(g, d, FFN2_BIG) if lowest else None

        doa, dsum_a, dob, dsum_b, mixed, dyb, dga, dgb = _mix_out_bwd(
            dx, oa, ob, pr["ga"], pr["gb"], pr["wo_a"], pr["wo_b"])
        small_grads["mla_out_norm"][l] = dga.reshape(-1)
        small_grads["swa_out_norm"][l] = dgb.reshape(-1)
        g["w_o"], _ = _tn_matmul(mixed, dyb, "wgrad_wo")

        rows_of = lambda s: s[:, :, 0].reshape(MLA_HEADS, t // bq, 1, bq)
        sides = ([_side_chips(upper_wires)] if upper else []) + ([_side_sibling(early)] if lowest else [])
        (dqa, dka, dva), side_out = _mla_bwd(qa, ka, va, doa, rows_of(lse_a), rows_of(dsum_a), sides)
        if upper:
            grad_shards[l + 1] = _shards_from_rows(_rs_sum(upper_owns, side_out[0]), big)
        if lowest:
            early_owns, early_wires = chip_sums(early, side_out[-1])
        dqb, dkb, dvb, dsinks = _swa_bwd(qb, kb, vb, pr["sinks"], dob, lse_b, dsum_b)
        small_grads["swa_sinks"][l] = dsinks[:, 0, 0]

        outs, side_out = _prep_bwd(x1, dx, cos, sin_s, pr["prep"], dqa, dka, dva, dqb, dkb, dvb,
                                   [_side_chips(early_wires)] if lowest else [])
        if lowest:
            early_rows = _rs_sum(early_owns, side_out[0])
        dx = outs[0]
        pg = dict(zip(PREP_WEIGHTS, outs[1:]))
        g["w_in"] = jnp.concatenate([pg["w_in"][:W_IN_COLS[0]], pg["w_in"][C_QS:]], axis=0)
        g["mla_w_q_b"] = pg["wqb"][:, :MLA_QK].reshape(MLA_HEADS * MLA_QK, MLA_Q_RANK)
        g["mla_w_kv_b"] = pg["w_kvb"]
        fold = lambda gg: gg[0, :HALF] + gg[0, HALF:]
        small_grads["mix_norm"][l] = pg["mix_g"][0]
        small_grads["mla_q_a_norm"][l] = pg["g_qa"][0]
        small_grads["mla_kv_a_norm"][l] = pg["g_kva"][0]
        small_grads["mla_q_norm"][l] = jnp.concatenate([pg["gq_n"][0], pg["gq_r"][0, :MLA_ROPE]])
        small_grads["mla_k_norm"][l] = jnp.concatenate([pg["gk_n"][0], pg["gk_r"][0, :MLA_ROPE]])
        small_grads["swa_q_norm"][l] = fold(pg["g_sq"])
        small_grads["swa_k_norm"][l] = fold(pg["g_sk"])

        if lowest:
            other = _grads_by_destination(g, d, OTHER_BIG)
            dx, small_grads["ffn1_norm"][l], g["ffn1_w_gate"], g["ffn1_w_up"], other_rows, down_rows = (
                _ffn_backward_exchanging(x0, dx, kept1, pr["ffn1"], "ffn1", other, chip_sums, d))
            late = _grads_by_destination(g, d, FFN1_BIG[:2])
            late_owns, late_wires = chip_sums(late, _rs_sibling_exchange(late))
            late_rows = _rs_sum(late_owns, _rs_chip_exchange(late_wires))
            grad_shards[l] = _shards_from_rows(late_rows + down_rows + early_rows + other_rows, big)
        else:
            dx, small_grads["ffn1_norm"][l], g["ffn1_w_gate"], g["ffn1_w_up"], g["ffn1_w_down"], _ = _ffn_backward(
                x0, dx, kept1, pr["ffn1"], "ffn1")
            upper = _grads_by_destination(g, d, FFN_BIG) + _grads_by_destination(g, d, OTHER_BIG)

    grad_big, delta_big, new_m_big, new_v_big = {}, {}, {}, {}
    for n in BIG:
        g_st = jnp.stack([grad_shards[l][n] for l in range(depth)])
        d_st, m_st, v_st = _adamw(g_st, _stored(w[n], n), _stored(m[n], n), _stored(v[n], n), n)
        grad_big[n], delta_big[n], new_m_big[n], new_v_big[n] = (_stored(a, n) for a in (g_st, d_st, m_st, v_st))

    small_partial = _pack_small({n: jnp.stack(small_grads[n]) for n in SMALL})
    g_s = _all_reduce_small(small_partial)
    d_s, m_s, v_s = _adamw_small(g_s, _pack_small(w), _pack_small(m), _pack_small(v))
    grad_small, delta_small, new_m_small, new_v_small = (_unpack_small(a, w) for a in (g_s, d_s, m_s, v_s))

    def ordered(big, small):
        return [big[n] if n in big else small[n] for n in WEIGHTS]

    return (loss, dx.reshape(x.shape), *ordered(grad_big, grad_small), *ordered(delta_big, delta_small),
            *ordered(new_m_big, new_m_small), *ordered(new_v_big, new_v_small))
```

```python
import jax
import jax.numpy as jnp
from jax import lax
from jax.experimental import pallas as pl
from jax.experimental.pallas import tpu as pltpu

N_DEV = 8
EPS = 1e-6
ROPE_THETA = 10000.0
MLA_HEADS = 4
MLA_Q_RANK = 256
MLA_KV_RANK = 128
MLA_NOPE = 128
MLA_ROPE = 64
MLA_V = 128
MLA_QK = MLA_NOPE + MLA_ROPE
MLA_QK_PAD = 256
SWA_HEADS = 8
SWA_KV = 2
SWA_GROUP = SWA_HEADS // SWA_KV
SWA_D = 64
SWA_BLOCK = 128
ADAM_LR = 0.001
ADAM_B1 = 0.9
ADAM_B2 = 0.999
ADAM_EPS = 1e-08
ADAM_WD = 0.01
ADAM_STEP = 10

LANES = 128
HALF = LANES // 2
SWA_PAIRS = SWA_HEADS // 2
W_IN_COLS = (MLA_Q_RANK + MLA_KV_RANK + MLA_ROPE, SWA_HEADS * SWA_D + 2 * SWA_KV * SWA_D)
VMEM_LIMIT = 56 * 1024 * 1024

_MXU = jnp.bfloat16
_TB = 256
_TB_MIX = 512
_BQ = 512
_STRIP = 32
_BWD_GROUPS = (4, 2, 1)
_TK = 1024
_SWA_STEP = 4
RS_ROW_BLOCKS = 2

BIG = ("ffn1_w_gate", "ffn1_w_up", "ffn1_w_down", "w_in", "mla_w_q_b", "mla_w_kv_b", "w_o",
       "ffn2_w_gate", "ffn2_w_up", "ffn2_w_down")
ROW_SHARDED = ("ffn1_w_down", "w_o", "ffn2_w_down")
FFN1_BIG = ("ffn1_w_gate", "ffn1_w_up", "ffn1_w_down")
FFN2_BIG = ("ffn2_w_gate", "ffn2_w_up", "ffn2_w_down")
FFN_BIG = FFN1_BIG + FFN2_BIG
OTHER_BIG = ("w_o", "w_in", "mla_w_q_b", "mla_w_kv_b")
GATHER_ORDER = (FFN1_BIG, OTHER_BIG, FFN2_BIG)
SMALL = ("ffn1_norm", "mix_norm", "mla_q_a_norm", "mla_kv_a_norm", "mla_q_norm", "mla_k_norm",
         "swa_q_norm", "swa_k_norm", "swa_sinks", "mla_out_norm", "swa_out_norm", "ffn2_norm")
WEIGHTS = ("ffn1_norm", "ffn1_w_gate", "ffn1_w_up", "ffn1_w_down", "mix_norm", "w_in", "mla_q_a_norm",
           "mla_w_q_b", "mla_kv_a_norm", "mla_w_kv_b", "mla_q_norm", "mla_k_norm", "swa_q_norm",
           "swa_k_norm", "swa_sinks", "mla_out_norm", "swa_out_norm", "w_o", "ffn2_norm",
           "ffn2_w_gate", "ffn2_w_up", "ffn2_w_down")
MESH_AXES = ("x", "y", "c")
MESH = pl.DeviceIdType.MESH
NEG = -1e30
LOG2_E = 1.4426950408889634


def _f32(t):
    return t.astype(jnp.float32)


def _mx(t):
    return t.astype(_MXU)


def _dot(a, b):
    return jnp.dot(a, b, preferred_element_type=jnp.float32)


def _dot_nt(a, b):
    return lax.dot_general(a, b, (((1,), (1,)), ((), ())), preferred_element_type=jnp.float32)


def _dot_tn(a, b):
    return lax.dot_general(a, b, (((0,), (0,)), ((), ())), preferred_element_type=jnp.float32)


def _rsq(ss, n):
    return lax.rsqrt(ss * (1.0 / n) + EPS)


def _sumsq(t):
    return jnp.sum(t * t, axis=-1, keepdims=True)


def _rowsum(t):
    return jnp.sum(t, axis=-1, keepdims=True)


def _rowmax(t):
    return jnp.max(t, axis=-1, keepdims=True)


def _colsum(t):
    return jnp.sum(t, axis=0, keepdims=True)


def _lane():
    return lax.broadcasted_iota(jnp.int32, (1, LANES), 1)


def _low_half():
    return _lane() < HALF


def _swap32(t):
    return jnp.where((_lane() & 32) == 0, pltpu.roll(t, 96, 1), pltpu.roll(t, 32, 1))


def _rope(t, cos, sin_signed):
    return t * cos + _swap32(t) * sin_signed


def _rope_bwd(d, cos, sin_signed):
    return d * cos + _swap32(d * sin_signed)


def _half_sums(t):
    low = _low_half()
    return jnp.where(low, _rowsum(jnp.where(low, t, 0.0)), _rowsum(jnp.where(low, 0.0, t)))


def _dup_halves(pair):
    low = _low_half()
    swapped = pltpu.roll(pair, HALF, 1)
    return jnp.where(low, pair, swapped), jnp.where(low, swapped, pair)


def _undup_halves(d0, d1):
    return jnp.where(_low_half(), d0 + pltpu.roll(d0, HALF, 1), d1 + pltpu.roll(d1, HALF, 1))


def _pick_halves(a, b):
    return jnp.where(_low_half(), a, b)


def _norm_bwd(dn_list, xh_list, r, n):
    c = sum(_rowsum(dn * xh) for dn, xh in zip(dn_list, xh_list)) * (1.0 / n)
    return [r * (dn - xh * c) for dn, xh in zip(dn_list, xh_list)]


def _cparams(semantics):
    return pltpu.CompilerParams(dimension_semantics=semantics, vmem_limit_bytes=VMEM_LIMIT)


def _const(shape):
    nd = len(shape)
    return pl.BlockSpec(shape, lambda *_: (0,) * nd, pipeline_mode=pl.Buffered(1))


def _acc(shape):
    nd = len(shape)
    return pl.BlockSpec(shape, lambda *_: (0,) * nd)


def _rows(tb, width):
    return pl.BlockSpec((tb, width), lambda i: (i, 0))


def _heads_rows(h, tb, width):
    return pl.BlockSpec((h, tb, width), lambda i: (0, i, 0))


def _sds(shape, dtype):
    return jax.ShapeDtypeStruct(shape, dtype)


def _position():
    return lax.axis_index("x"), lax.axis_index("y"), lax.axis_index("c")


def _all_gather(xs):
    n = len(xs)

    def body(*refs):
        x_refs, out_refs, (send_sems, recv_sems, local_sems) = refs[:n], refs[n:2 * n], refs[2 * n:]
        x, y, c = _position()
        me, sibling = (x, y, c), (x, y, 1 - c)
        chips = [(1 - x, y), (x, 1 - y), (1 - x, 1 - y)]

        def rows(i, px, py, pc):
            return out_refs[i].at[4 * px + 2 * py + pc]

        def copy(i, k, block, to, from_input=False):
            return _remote(x_refs[i] if from_input else rows(i, *block), rows(i, *block),
                           send_sems.at[k * n + i], recv_sems.at[k * n + i], to)

        every = range(n)
        mine = [pltpu.make_async_copy(x_refs[i], rows(i, *me), local_sems.at[i]) for i in every]
        first = [copy(i, 0, me, sibling, True) for i in every]
        first += [copy(i, 1 + j, me, (*chip, c), True) for j, chip in enumerate(chips) for i in every]
        for cp in mine + first:
            cp.start()
        passed = [[copy(i, 4 + j, (*chip, c), sibling) for i in every] for j, chip in enumerate(chips)]
        for j, chip in enumerate(chips):
            for i in every:
                copy(i, 1 + j, (*chip, c), me).wait_recv()
                passed[j][i].start()
        for i in every:
            copy(i, 0, sibling, me).wait_recv()
        for j, chip in enumerate(chips):
            for i in every:
                copy(i, 4 + j, (*chip, 1 - c), me).wait_recv()
        for cp in first + [cp for group in passed for cp in group]:
            cp.wait_send()
        for cp in mine:
            cp.wait()

    hbm = pl.BlockSpec(memory_space=pl.ANY)
    dma = pltpu.SemaphoreType.DMA
    return pl.pallas_call(
        body, name="ag_weights",
        out_shape=[_sds((N_DEV,) + a.shape, a.dtype) for a in xs],
        in_specs=[hbm] * n, out_specs=[hbm] * n,
        scratch_shapes=[dma((7 * n,)), dma((7 * n,)), dma((n,))],
    )(*xs)


def _relations(x, y):
    return [(x, y), (1 - x, y), (x, 1 - y), (1 - x, 1 - y)]


def _remote(src, dst, send_sem, recv_sem, device):
    return pltpu.make_async_remote_copy(src_ref=src, dst_ref=dst, send_sem=send_sem, recv_sem=recv_sem,
                                        device_id=device, device_id_type=MESH)


def _sibling_copies(g_refs, out_refs, send, recv):
    x, y, c = _position()
    n = len(g_refs)
    return [_remote(g.at[4 * px + 2 * py + (1 - c)], o.at[k], send.at[k * n + i], recv.at[k * n + i], (x, y, 1 - c))
            for k, (px, py) in enumerate(_relations(x, y)) for i, (g, o) in enumerate(zip(g_refs, out_refs))]


def _chip_copies(w_refs, out_refs, send, recv):
    x, y, c = _position()
    n = len(w_refs)
    return [_remote(w.at[k + 1], o.at[k], send.at[k * n + i], recv.at[k * n + i], (px, py, c))
            for k, (px, py) in enumerate(_relations(x, y)[1:]) for i, (w, o) in enumerate(zip(w_refs, out_refs))]


def _exchange(arrays, lead, relations, copies_fn, name):
    n = len(arrays)

    def body(*refs):
        copies = copies_fn(refs[:n], refs[n:2 * n], refs[2 * n], refs[2 * n + 1])
        for cp in copies:
            cp.start()
        for cp in copies:
            cp.wait()

    hbm = pl.BlockSpec(memory_space=pl.ANY)
    dma = pltpu.SemaphoreType.DMA
    return pl.pallas_call(
        body, name=name, out_shape=[_sds((lead,) + a.shape[1:], a.dtype) for a in arrays],
        in_specs=[hbm] * n, out_specs=[hbm] * n,
        scratch_shapes=[dma((relations * n,)), dma((relations * n,))],
    )(*arrays)


def _rs_sibling_exchange(gs):
    return _exchange(gs, 4, 4, _sibling_copies, "rs_sibling_exchange")


def _rs_chip_exchange(wires):
    return _exchange(wires, 3, 3, _chip_copies, "rs_chip_exchange")


def _side_exchange(arrays, lead, relations, copies_fn):
    shapes = [_sds((lead,) + a.shape[1:], a.dtype) for a in arrays]
    return list(arrays), shapes, relations * len(arrays), lambda ins, outs, send, recv, local: copies_fn(ins, outs, send, recv)


def _side_sibling(gs):
    return _side_exchange(gs, 4, 4, _sibling_copies)


def _side_chips(wires):
    return _side_exchange(wires, 3, 3, _chip_copies)


def _rs_chip_sums(gs, sibs, dest_idx):
    n = len(gs)

    def body(idx_ref, *refs):
        g_refs, s_refs, own_refs, wire_refs = refs[:n], refs[n:2 * n], refs[2 * n:3 * n], refs[3 * n:]
        totals = [g[0] + s[0] for g, s in zip(g_refs, s_refs)]
        for total, wire in zip(totals, wire_refs):
            wire[0] = total.astype(wire.dtype)

        @pl.when(pl.program_id(1) == 0)
        def _():
            for total, own in zip(totals, own_refs):
                own[...] = total

    def blocks(a, index_map, squeeze):
        rb = a.shape[1] // RS_ROW_BLOCKS
        return pl.BlockSpec((rb, a.shape[2]) if squeeze else (1, rb, a.shape[2]), index_map)

    return pl.pallas_call(
        body, name="rs_chip_sums",
        grid_spec=pltpu.PrefetchScalarGridSpec(
            num_scalar_prefetch=1, grid=(RS_ROW_BLOCKS, 4),
            in_specs=[blocks(g, lambda r, k, idx: (idx[k], r, 0), False) for g in gs]
            + [blocks(g, lambda r, k, idx: (k, r, 0), False) for g in gs],
            out_specs=[blocks(g, lambda r, k, idx: (r, 0), True) for g in gs]
            + [blocks(g, lambda r, k, idx: (k, r, 0), False) for g in gs]),
        out_shape=[_sds(g.shape[1:], jnp.float32) for g in gs] + [_sds((4,) + g.shape[1:], _MXU) for g in gs],
        compiler_params=_cparams(("parallel", "arbitrary")),
    )(dest_idx, *gs, *sibs)


def _side_gather(xs):
    n = len(xs)

    def make(ins, outs, send, recv, local):
        x, y, c = _position()
        me = 4 * x + 2 * y + c
        copies = [pltpu.make_async_copy(x_ref, out_ref.at[me], local.at[i])
                  for i, (x_ref, out_ref) in enumerate(zip(ins, outs))]
        for k in range(1, N_DEV):
            peer = (1 - x if k & 4 else x, 1 - y if k & 2 else y, 1 - c if k & 1 else c)
            copies += [_remote(x_ref, out_ref.at[me], send.at[(k - 1) * n + i], recv.at[(k - 1) * n + i], peer)
                       for i, (x_ref, out_ref) in enumerate(zip(ins, outs))]
        return copies

    return list(xs), [_sds((N_DEV,) + a.shape, a.dtype) for a in xs], (N_DEV - 1) * n, make


def _call(body, args, sides, *, name, grid, in_specs, out_specs, out_shape, scratch_shapes=(), semantics):
    in_specs, out_specs, out_shape = list(in_specs), list(out_specs), list(out_shape)
    n_in, n_out, n_scr = len(in_specs), len(out_specs), len(scratch_shapes)
    sides = list(sides or [])
    if not sides:
        outs = pl.pallas_call(body, name=name, grid=grid, in_specs=in_specs, out_specs=out_specs, out_shape=out_shape,
                              scratch_shapes=list(scratch_shapes), compiler_params=_cparams(semantics))(*args)
        return list(outs), []
    arrays = [a for side in sides for a in side[0]]
    shapes = [s for side in sides for s in side[1]]
    n_side_in, n_side_out = len(arrays), len(shapes)
    hbm = pl.BlockSpec(memory_space=pl.ANY)

    def with_copies(*refs):
        main_in, refs = refs[:n_in], refs[n_in:]
        side_in, refs = refs[:n_side_in], refs[n_side_in:]
        main_out, refs = refs[:n_out], refs[n_out:]
        side_out, refs = refs[:n_side_out], refs[n_side_out:]
        main_scr, sems = refs[:n_scr], refs[n_scr:]
        copies = []
        for k, (side_arrays, side_shapes, _, make) in enumerate(sides):
            copies += make(side_in[:len(side_arrays)], side_out[:len(side_shapes)], *sems[3 * k:3 * k + 3])
            side_in, side_out = side_in[len(side_arrays):], side_out[len(side_shapes):]
        ids = [pl.program_id(a) for a in range(len(grid))]
        first, last = ids[0] == 0, ids[0] == grid[0] - 1
        for i, size in zip(ids[1:], grid[1:]):
            first, last = jnp.logical_and(first, i == 0), jnp.logical_and(last, i == size - 1)

        @pl.when(first)
        def _():
            for cp in copies:
                cp.start()

        body(*main_in, *main_out, *main_scr)

        @pl.when(last)
        def _():
            for cp in copies:
                cp.wait()

    dma = pltpu.SemaphoreType.DMA
    sem_shapes = [dma((n,)) for side in sides for n in (side[2], side[2], len(side[0]))]
    outs = pl.pallas_call(
        with_copies, name=name, grid=grid, in_specs=in_specs + [hbm] * n_side_in,
        out_specs=out_specs + [hbm] * n_side_out, out_shape=out_shape + shapes,
        scratch_shapes=list(scratch_shapes) + sem_shapes,
        compiler_params=_cparams(("arbitrary",) * len(grid)),
    )(*args, *arrays)
    side_outs, rest = [], list(outs[n_out:])
    for side in sides:
        side_outs.append(rest[:len(side[1])])
        rest = rest[len(side[1]):]
    return list(outs[:n_out]), side_outs


def _all_reduce_small(v):
    rows_n = v.shape[0]

    def body(v_ref, out_ref, buf, send_sems, recv_sems):
        x, y, c = _position()
        me = 4 * x + 2 * y + c
        buf[me] = v_ref[...]
        copies = []
        for k in range(1, N_DEV):
            px = 1 - x if k & 4 else x
            py = 1 - y if k & 2 else y
            pc = 1 - c if k & 1 else c
            copies.append(pltpu.make_async_remote_copy(
                src_ref=v_ref, dst_ref=buf.at[me],
                send_sem=send_sems.at[k - 1], recv_sem=recv_sems.at[k - 1], device_id=(px, py, pc), device_id_type=MESH))
        for cp in copies:
            cp.start()
        for cp in copies:
            cp.wait()
        total = buf[0]
        for d in range(1, N_DEV):
            total = total + buf[d]
        out_ref[...] = total

    return pl.pallas_call(
        body, name="ar_small",
        out_shape=_sds((rows_n, LANES), jnp.float32),
        in_specs=[pl.BlockSpec(memory_space=pltpu.VMEM)],
        out_specs=pl.BlockSpec(memory_space=pltpu.VMEM),
        scratch_shapes=[pltpu.VMEM((N_DEV, rows_n, LANES), jnp.float32),
                        pltpu.SemaphoreType.DMA((N_DEV - 1,)), pltpu.SemaphoreType.DMA((N_DEV - 1,))],
    )(v)


def _adamw_math(w, g, m, v):
    m = ADAM_B1 * m + (1.0 - ADAM_B1) * g
    v = ADAM_B2 * v + (1.0 - ADAM_B2) * (g * g)
    m_hat = m / (1.0 - ADAM_B1 ** ADAM_STEP)
    v_hat = v / (1.0 - ADAM_B2 ** ADAM_STEP)
    delta = -ADAM_LR * (m_hat / (jnp.sqrt(v_hat) + ADAM_EPS) + ADAM_WD * w)
    return delta, m, v


def _rs_sum(owns, recvs):
    n = len(owns)

    def body(*refs):
        own_refs, recv_refs, out_refs = refs[:n], refs[n:4 * n], refs[4 * n:]
        for i in range(n):
            r0, r1, r2 = recv_refs[3 * i:3 * i + 3]
            out_refs[i][...] = ((own_refs[i][...] + _f32(r0[0])) + _f32(r1[0])) + _f32(r2[0])

    def row(a):
        return pl.BlockSpec((a.shape[0] // RS_ROW_BLOCKS, a.shape[1]), lambda r: (r, 0))

    def slot(a, k):
        return pl.BlockSpec((1, a.shape[0] // RS_ROW_BLOCKS, a.shape[1]), lambda r: (k, r, 0))

    return pl.pallas_call(
        body, name="rs_sum", grid=(RS_ROW_BLOCKS,),
        in_specs=[row(a) for a in owns] + [slot(a, k) for a in owns for k in range(3)],
        out_specs=[row(a) for a in owns],
        out_shape=[_sds(a.shape, jnp.float32) for a in owns],
        compiler_params=_cparams(("parallel",)),
    )(*owns, *[r for r in recvs for _ in range(3)])


def _adamw(g, w, m, v, name):
    depth, a, b = w.shape

    def body(g_ref, w_ref, m_ref, v_ref, d_out, m_out, v_out):
        delta, m2, v2 = _adamw_math(w_ref[...], g_ref[...], m_ref[...], v_ref[...])
        d_out[...] = delta
        m_out[...] = m2
        v_out[...] = v2

    layer = pl.BlockSpec((1, a, b), lambda l: (l, 0, 0))
    return pl.pallas_call(
        body, name="adamw_" + name, grid=(depth,),
        in_specs=[layer] * 4, out_specs=[layer] * 3,
        out_shape=[_sds(w.shape, jnp.float32)] * 3,
        compiler_params=_cparams(("parallel",)),
    )(g, w, m, v)


def _adamw_small(gs, ws, ms, vs):
    n = len(gs)

    def body(*refs):
        g_refs, w_refs, m_refs, v_refs, outs = refs[:n], refs[n:2 * n], refs[2 * n:3 * n], refs[3 * n:4 * n], refs[4 * n:]
        for i in range(n):
            delta, m2, v2 = _adamw_math(w_refs[i][...], g_refs[i][...], m_refs[i][...], v_refs[i][...])
            outs[i][...] = delta
            outs[n + i][...] = m2
            outs[2 * n + i][...] = v2

    vm = pl.BlockSpec(memory_space=pltpu.VMEM)
    return pl.pallas_call(
        body, name="adamw_small",
        in_specs=[vm] * (4 * n), out_specs=[vm] * (3 * n),
        out_shape=[_sds(a.shape, jnp.float32) for a in ws] * 3,
    )(*gs, *ws, *ms, *vs)


def _f_chunk(f):
    for cand in (1408, 1024, 512, 256, 128):
        if f % cand == 0:
            return cand
    return f


def _ffn_fwd(x, gain, wg, wu, wd, sides=()):
    t, d = x.shape
    f = wg.shape[0]
    tb = min(_TB, t)
    fc = _f_chunk(f)

    def body(x_ref, g_ref, wg_ref, wu_ref, wd_ref, xo_ref, h_ref, s_ref, fa_ref, fu_ref):
        xv = x_ref[...]
        hb = _mx(xv * _rsq(_sumsq(xv), d) * g_ref[...])
        h_ref[...] = hb
        y = jnp.zeros((tb, d), jnp.float32)
        for c0 in range(0, f, fc):
            a = _dot_nt(hb, wg_ref[c0:c0 + fc, :])
            u = _dot_nt(hb, wu_ref[c0:c0 + fc, :])
            sig = jax.nn.sigmoid(a)
            silu = a * sig
            s = _mx(silu * u)
            s_ref[:, c0:c0 + fc] = s
            fa_ref[:, c0:c0 + fc] = _mx(u * (sig * (1.0 + a * (1.0 - sig))))
            fu_ref[:, c0:c0 + fc] = _mx(silu)
            y = y + _dot(s, wd_ref[c0:c0 + fc, :])
        xo_ref[...] = xv + 0.5 * y

    return _call(
        body, (x, gain, wg, wu, wd), sides, name="ffn_fwd", grid=(t // tb,),
        in_specs=[_rows(tb, d), _const((1, d)), _const((f, d)), _const((f, d)), _const((f, d))],
        out_specs=[_rows(tb, d), _rows(tb, d), _rows(tb, f), _rows(tb, f), _rows(tb, f)],
        out_shape=[_sds((t, d), jnp.float32), _sds((t, d), _MXU), _sds((t, f), _MXU), _sds((t, f), _MXU),
                   _sds((t, f), _MXU)],
        semantics=("parallel",))


def _ffn_dgrad(x, gain, dxo, fa, fu, wg, wu, wd, sides=()):
    t, d = x.shape
    f = wg.shape[0]
    tb = min(_TB, t)
    fc = _f_chunk(f)

    def body(x_ref, g_ref, dxo_ref, fa_ref, fu_ref, wg_ref, wu_ref, wd_ref, dxi_ref, da_ref, du_ref, dy_ref, dg_ref):
        xv = x_ref[...]
        gv = g_ref[...]
        r = _rsq(_sumsq(xv), d)
        xhat = xv * r
        dxo = dxo_ref[...]
        dyb = _mx(0.5 * dxo)
        dy_ref[...] = dyb
        dh = jnp.zeros((tb, d), jnp.float32)
        for c0 in range(0, f, fc):
            ds = _dot_nt(dyb, wd_ref[c0:c0 + fc, :])
            da = _mx(ds * _f32(fa_ref[:, c0:c0 + fc]))
            du = _mx(ds * _f32(fu_ref[:, c0:c0 + fc]))
            da_ref[:, c0:c0 + fc] = da
            du_ref[:, c0:c0 + fc] = du
            dh = dh + _dot(da, wg_ref[c0:c0 + fc, :]) + _dot(du, wu_ref[c0:c0 + fc, :])

        @pl.when(pl.program_id(0) == 0)
        def _():
            dg_ref[...] = jnp.zeros_like(dg_ref)

        dg_ref[...] += _colsum(dh * xhat)
        dn = dh * gv
        dxi_ref[...] = dxo + r * (dn - xhat * (_rowsum(dn * xhat) * (1.0 / d)))

    return _call(
        body, (x, gain, dxo, fa, fu, wg, wu, wd), sides, name="ffn_dgrad", grid=(t // tb,),
        in_specs=[_rows(tb, d), _const((1, d)), _rows(tb, d), _rows(tb, f), _rows(tb, f),
                  _const((f, d)), _const((f, d)), _const((f, d))],
        out_specs=[_rows(tb, d), _rows(tb, f), _rows(tb, f), _rows(tb, d), _acc((1, d))],
        out_shape=[_sds((t, d), jnp.float32), _sds((t, f), _MXU), _sds((t, f), _MXU), _sds((t, d), _MXU),
                   _sds((1, d), jnp.float32)],
        semantics=("arbitrary",))


def _tn_matmul(a, b, name, sides=()):
    t, m = a.shape
    n = b.shape[1]
    tk = min(_TK, t)
    tn = n
    while m * tn * 4 > 12 * 1024 * 1024 and tn % 256 == 0:
        tn //= 2

    def body(a_ref, b_ref, o_ref):
        @pl.when(pl.program_id(1) == 0)
        def _():
            o_ref[...] = jnp.zeros_like(o_ref)

        o_ref[...] += _dot_tn(a_ref[...], b_ref[...])

    (out,), side_outs = _call(
        body, (a, b), sides, name=name, grid=(n // tn, t // tk),
        in_specs=[pl.BlockSpec((tk, m), lambda j, k: (k, 0)), pl.BlockSpec((tk, tn), lambda j, k: (k, j))],
        out_specs=[pl.BlockSpec((m, tn), lambda j, k: (0, j))],
        out_shape=[_sds((m, n), jnp.float32)],
        semantics=("parallel", "arbitrary"))
    return out, side_outs


PREP_WEIGHTS = ("mix_g", "w_in", "g_qa", "wqb", "g_kva", "w_kvb", "gq_n", "gq_r", "gk_n", "gk_r", "g_sq", "g_sk")
C_CQ, C_CKV, C_KPE, C_QS = 0, MLA_Q_RANK, MLA_Q_RANK + MLA_KV_RANK, MLA_Q_RANK + MLA_KV_RANK + LANES
C_KS = C_QS + SWA_HEADS * SWA_D
C_VS = C_KS + LANES
W_IN_PACKED = C_VS + LANES


def _prep_specs(p):
    return [_const(p[n].shape) for n in PREP_WEIGHTS]


def _pair_norm_rope(t, gain, cos, sin_s):
    return _rope(t * _rsq(_half_sums(t * t), SWA_D) * gain, cos, sin_s)


def _prep_fwd(x, cos, sin_s, p):
    t, d = x.shape
    tb = min(_TB_MIX, t)

    def body(x_ref, cos_ref, sin_ref, mix_g, w_in, g_qa, wqb, g_kva, w_kvb, gq_n, gq_r, gk_n, gk_r, g_sq, g_sk,
             qa_ref, ka_ref, va_ref, qb_ref, kb_ref, vb_ref):
        xv = x_ref[...]
        cos_v, sin_v = cos_ref[...], sin_ref[...]
        hb = _mx(xv * _rsq(_sumsq(xv), d) * mix_g[...])
        proj = _dot_nt(hb, w_in[...])
        cq = proj[:, C_CQ:C_CKV]
        cqn = _mx(cq * _rsq(_sumsq(cq), MLA_Q_RANK) * g_qa[...])
        for h in range(MLA_HEADS):
            qh = _dot_nt(cqn, wqb[h])
            qn, qr = qh[:, :MLA_NOPE], qh[:, MLA_NOPE:]
            rh = _rsq(_sumsq(qn) + _sumsq(qr), MLA_QK)
            qa_ref[h, :, 0:MLA_NOPE] = (qn * rh * gq_n[...]).astype(qa_ref.dtype)
            qa_ref[h, :, MLA_NOPE:MLA_QK_PAD] = _rope(qr * rh * gq_r[...], cos_v, sin_v).astype(qa_ref.dtype)
        ckv = proj[:, C_CKV:C_KPE]
        ckvn = _mx(ckv * _rsq(_sumsq(ckv), MLA_KV_RANK) * g_kva[...])
        kpe = proj[:, C_KPE:C_QS]
        ss_pe = _sumsq(kpe)
        kv = _dot_nt(ckvn, w_kvb[...])
        for h in range(MLA_HEADS):
            c0 = h * (MLA_NOPE + MLA_V)
            kn = kv[:, c0:c0 + MLA_NOPE]
            rh = _rsq(_sumsq(kn) + ss_pe, MLA_QK)
            ka_ref[h, :, 0:MLA_NOPE] = (kn * rh * gk_n[...]).astype(ka_ref.dtype)
            ka_ref[h, :, MLA_NOPE:MLA_QK_PAD] = _rope(kpe * rh * gk_r[...], cos_v, sin_v).astype(ka_ref.dtype)
            va_ref[h] = kv[:, c0 + MLA_NOPE:c0 + MLA_NOPE + MLA_V].astype(va_ref.dtype)
        for j in range(SWA_PAIRS):
            c0 = C_QS + j * LANES
            qb_ref[j] = _pair_norm_rope(proj[:, c0:c0 + LANES], g_sq[...], cos_v, sin_v).astype(qb_ref.dtype)
        k0, k1 = _dup_halves(_pair_norm_rope(proj[:, C_KS:C_VS], g_sk[...], cos_v, sin_v))
        kb_ref[0] = k0.astype(kb_ref.dtype)
        kb_ref[1] = k1.astype(kb_ref.dtype)
        v0, v1 = _dup_halves(proj[:, C_VS:W_IN_PACKED])
        vb_ref[0] = v0.astype(vb_ref.dtype)
        vb_ref[1] = v1.astype(vb_ref.dtype)

    return pl.pallas_call(
        body, name="prep_fwd", grid=(t // tb,),
        in_specs=[_rows(tb, d), _rows(tb, LANES), _rows(tb, LANES)] + _prep_specs(p),
        out_specs=[_heads_rows(MLA_HEADS, tb, MLA_QK_PAD), _heads_rows(MLA_HEADS, tb, MLA_QK_PAD),
                   _heads_rows(MLA_HEADS, tb, MLA_V), _heads_rows(SWA_PAIRS, tb, LANES),
                   _heads_rows(SWA_KV, tb, LANES), _heads_rows(SWA_KV, tb, LANES)],
        out_shape=[_sds((MLA_HEADS, t, MLA_QK_PAD), _MXU), _sds((MLA_HEADS, t, MLA_QK_PAD), _MXU),
                   _sds((MLA_HEADS, t, MLA_V), _MXU), _sds((SWA_PAIRS, t, LANES), _MXU),
                   _sds((SWA_KV, t, LANES), _MXU), _sds((SWA_KV, t, LANES), _MXU)],
        compiler_params=_cparams(("parallel",)),
    )(x, cos, sin_s, *[p[n] for n in PREP_WEIGHTS])


def _prep_bwd(x, dxin, cos, sin_s, p, dqa, dka, dva, dqb, dkb, dvb, sides=()):
    t, d = x.shape
    tb = min(_TB_MIX, t)
    n_w = len(PREP_WEIGHTS)

    def body(*refs):
        x_ref, dxin_ref, cos_ref, sin_ref = refs[:4]
        mix_g, w_in, g_qa, wqb, g_kva, w_kvb, gq_n, gq_r, gk_n, gk_r, g_sq, g_sk = refs[4:4 + n_w]
        dqa_ref, dka_ref, dva_ref, dqb_ref, dkb_ref, dvb_ref = refs[4 + n_w:10 + n_w]
        dx_ref = refs[10 + n_w]
        grads = dict(zip(PREP_WEIGHTS, refs[11 + n_w:11 + 2 * n_w]))
        dproj_ref, dkv_ref, dqh_ref = refs[11 + 2 * n_w:]

        @pl.when(pl.program_id(0) == 0)
        def _():
            for ref in grads.values():
                ref[...] = jnp.zeros_like(ref)

        xv = x_ref[...]
        cos_v, sin_v = cos_ref[...], sin_ref[...]
        r0 = _rsq(_sumsq(xv), d)
        xhat = xv * r0
        hb = _mx(xhat * mix_g[...])
        proj = _dot_nt(hb, w_in[...])

        cq = proj[:, C_CQ:C_CKV]
        rq = _rsq(_sumsq(cq), MLA_Q_RANK)
        cqh = cq * rq
        cqn = _mx(cqh * g_qa[...])
        dcqn = jnp.zeros((tb, MLA_Q_RANK), jnp.float32)
        for h in range(MLA_HEADS):
            qh = _dot_nt(cqn, wqb[h])
            qn, qr = qh[:, :MLA_NOPE], qh[:, MLA_NOPE:]
            rh = _rsq(_sumsq(qn) + _sumsq(qr), MLA_QK)
            xh_n, xh_r = qn * rh, qr * rh
            dy_n = dqa_ref[h, :, 0:MLA_NOPE]
            dy_r = _rope_bwd(dqa_ref[h, :, MLA_NOPE:MLA_QK_PAD], cos_v, sin_v)
            grads["gq_n"][...] += _colsum(dy_n * xh_n)
            grads["gq_r"][...] += _colsum(dy_r * xh_r)
            dqn, dqr = _norm_bwd([dy_n * gq_n[...], dy_r * gq_r[...]], [xh_n, xh_r], rh, MLA_QK)
            dqh_ref[:, 0:MLA_NOPE] = _mx(dqn)
            dqh_ref[:, MLA_NOPE:MLA_QK_PAD] = _mx(dqr)
            dqh = dqh_ref[...]
            grads["wqb"][h] += _dot_tn(dqh, cqn)
            dcqn = dcqn + _dot(dqh, wqb[h])
        grads["g_qa"][...] += _colsum(dcqn * cqh)
        (dcq,) = _norm_bwd([dcqn * g_qa[...]], [cqh], rq, MLA_Q_RANK)
        dproj_ref[:, C_CQ:C_CKV] = _mx(dcq)

        ckv = proj[:, C_CKV:C_KPE]
        rkv = _rsq(_sumsq(ckv), MLA_KV_RANK)
        ckvh = ckv * rkv
        ckvn = _mx(ckvh * g_kva[...])
        kpe = proj[:, C_KPE:C_QS]
        ss_pe = _sumsq(kpe)
        kv = _dot_nt(ckvn, w_kvb[...])
        dkpe = jnp.zeros((tb, LANES), jnp.float32)
        for h in range(MLA_HEADS):
            c0 = h * (MLA_NOPE + MLA_V)
            c1 = c0 + MLA_NOPE
            kn = kv[:, c0:c1]
            rh = _rsq(_sumsq(kn) + ss_pe, MLA_QK)
            xh_n, xh_r = kn * rh, kpe * rh
            dy_n = dka_ref[h, :, 0:MLA_NOPE]
            dy_r = _rope_bwd(dka_ref[h, :, MLA_NOPE:MLA_QK_PAD], cos_v, sin_v)
            grads["gk_n"][...] += _colsum(dy_n * xh_n)
            grads["gk_r"][...] += _colsum(dy_r * xh_r)
            dkn, dkr = _norm_bwd([dy_n * gk_n[...], dy_r * gk_r[...]], [xh_n, xh_r], rh, MLA_QK)
            dkpe = dkpe + dkr
            dkv_ref[:, c0:c1] = _mx(dkn)
            dkv_ref[:, c1:c1 + MLA_V] = _mx(dva_ref[h])
        dkv = dkv_ref[...]
        grads["w_kvb"][...] += _dot_tn(dkv, ckvn)
        dckvn = _dot(dkv, w_kvb[...])
        grads["g_kva"][...] += _colsum(dckvn * ckvh)
        (dckv,) = _norm_bwd([dckvn * g_kva[...]], [ckvh], rkv, MLA_KV_RANK)
        dproj_ref[:, C_CKV:C_KPE] = _mx(dckv)
        dproj_ref[:, C_KPE:C_QS] = _mx(dkpe)

        def pair_bwd(tv, dy, g_ref, gname):
            r = _rsq(_half_sums(tv * tv), SWA_D)
            xh = tv * r
            dpre = _rope_bwd(dy, cos_v, sin_v)
            grads[gname][...] += _colsum(dpre * xh)
            dn = dpre * g_ref[...]
            return r * (dn - xh * (_half_sums(dn * xh) * (1.0 / SWA_D)))

        for j in range(SWA_PAIRS):
            c0 = C_QS + j * LANES
            dproj_ref[:, c0:c0 + LANES] = _mx(pair_bwd(proj[:, c0:c0 + LANES], dqb_ref[j], g_sq, "g_sq"))
        dproj_ref[:, C_KS:C_VS] = _mx(pair_bwd(proj[:, C_KS:C_VS], _undup_halves(dkb_ref[0], dkb_ref[1]), g_sk, "g_sk"))
        dproj_ref[:, C_VS:W_IN_PACKED] = _mx(_undup_halves(dvb_ref[0], dvb_ref[1]))

        dproj = dproj_ref[...]
        grads["w_in"][...] += _dot_tn(dproj, hb)
        dh = _dot(dproj, w_in[...])
        grads["mix_g"][...] += _colsum(dh * xhat)
        (dxv,) = _norm_bwd([dh * mix_g[...]], [xhat], r0, d)
        dx_ref[...] = dxin_ref[...] + dxv

    grad_shapes = [p[n].shape for n in PREP_WEIGHTS]
    args = (x, dxin, cos, sin_s, *[p[n] for n in PREP_WEIGHTS], dqa, dka, dva, dqb, dkb, dvb)
    return _call(
        body, args, sides, name="prep_bwd", grid=(t // tb,),
        in_specs=[_rows(tb, d), _rows(tb, d), _rows(tb, LANES), _rows(tb, LANES)] + _prep_specs(p) + [
            _heads_rows(MLA_HEADS, tb, MLA_QK_PAD), _heads_rows(MLA_HEADS, tb, MLA_QK_PAD),
            _heads_rows(MLA_HEADS, tb, MLA_V), _heads_rows(SWA_PAIRS, tb, LANES),
            _heads_rows(SWA_KV, tb, LANES), _heads_rows(SWA_KV, tb, LANES)],
        out_specs=[_rows(tb, d)] + [_acc(s) for s in grad_shapes],
        out_shape=[_sds((t, d), jnp.float32)] + [_sds(s, jnp.float32) for s in grad_shapes],
        scratch_shapes=[pltpu.VMEM((tb, W_IN_PACKED), _MXU), pltpu.VMEM((tb, MLA_HEADS * (MLA_NOPE + MLA_V)), _MXU),
                        pltpu.VMEM((tb, MLA_QK_PAD), _MXU)],
        semantics=("arbitrary",))


def _strips(n):
    step = min(_STRIP, n)
    return [slice(r, r + step) for r in range(0, n, step)]


def _mla_fwd(q, k, v, sides=()):
    hn, t, dq = q.shape
    dv = v.shape[2]
    bq = min(_BQ, t)
    scale = MLA_QK ** -0.5
    scale2 = scale * LOG2_E

    def body(q_ref, k_ref, v_ref, o_ref, l_ref):
        i = pl.program_id(1)
        qv = q_ref[0]

        def step(first_block, width, carry, masked):
            m, l, acc = carry
            start = pl.multiple_of(first_block * bq, bq)
            s = _dot_nt(qv, k_ref[0, pl.ds(start, width), :])
            if masked:
                row = lax.broadcasted_iota(jnp.int32, (bq, width), 0)
                col = lax.broadcasted_iota(jnp.int32, (bq, width), 1)
                s = jnp.where(col <= row, s, NEG)
            m_new = jnp.maximum(m, _rowmax(s))
            alpha = jnp.exp2((m - m_new) * scale2)
            pv = jnp.exp2((s - m_new) * scale2)
            l = alpha * l + _rowsum(pv)
            acc = alpha * acc + _dot(_mx(pv), v_ref[0, pl.ds(start, width), :])
            return m_new, l, acc

        init = (jnp.full((bq, 1), NEG, jnp.float32), jnp.zeros((bq, 1), jnp.float32), jnp.zeros((bq, dv), jnp.float32))
        carry, done = init, 0
        for group in (4, 2, 1):
            count = (i - done) // group
            carry = lax.fori_loop(0, count, lambda g, c, done=done, group=group: step(done + group * g, group * bq, c, False), carry)
            done = done + group * count
        m, l, acc = step(i, bq, carry, True)
        o_ref[0] = acc / l
        l_ref[0] = jnp.broadcast_to(m * scale + jnp.log(l), (bq, LANES))

    return _call(
        body, (q, k, v), sides, name="mla_fwd", grid=(hn, t // bq),
        in_specs=[pl.BlockSpec((1, bq, dq), lambda h, i: (h, i, 0)),
                  pl.BlockSpec((1, t, dq), lambda h, i: (h, 0, 0)),
                  pl.BlockSpec((1, t, dv), lambda h, i: (h, 0, 0))],
        out_specs=[pl.BlockSpec((1, bq, dv), lambda h, i: (h, i, 0)),
                   pl.BlockSpec((1, bq, LANES), lambda h, i: (h, i, 0))],
        out_shape=[_sds((hn, t, dv), jnp.float32), _sds((hn, t, LANES), jnp.float32)],
        semantics=("parallel", "arbitrary"))


def _mla_bwd(q, k, v, do, lse_rows, dsum_rows, sides=()):
    hn, t, dq_w = q.shape
    dv_w = v.shape[2]
    bq = min(_BQ, t)
    nb = t // bq
    wide = max(_BWD_GROUPS) * bq
    scale = MLA_QK ** -0.5

    def body(q_ref, do_ref, l_ref, d_ref, k_ref, v_ref, dq_ref, dk_ref, dv_ref, st_scr, dpt_scr, p_scr, ds_scr):
        j = pl.program_id(1)

        @pl.when(j == 0)
        def _():
            dq_ref[...] = jnp.zeros_like(dq_ref)

        kv = k_ref[0]
        vv = v_ref[0]
        dk_ref[0] = jnp.zeros((bq, dq_w), jnp.float32)
        dv_ref[0] = jnp.zeros((bq, dv_w), jnp.float32)

        def tile(first_block, n_blk, masked):
            width = n_blk * bq
            start = pl.multiple_of(first_block * bq, bq)
            qv = q_ref[0, pl.ds(start, width), :]
            dov = do_ref[0, pl.ds(start, width), :]
            st_scr[:, :width] = _dot_nt(kv, qv)
            dpt_scr[:, :width] = _dot_nt(vv, dov)
            lse2 = jnp.concatenate([l_ref[0, first_block + b] for b in range(n_blk)], axis=1) * LOG2_E
            dsum = jnp.concatenate([d_ref[0, first_block + b] for b in range(n_blk)], axis=1)
            for rows in _strips(bq):
                pt = jnp.exp2(st_scr[rows, :width] * (scale * LOG2_E) - lse2)
                if masked:
                    n_rows = rows.stop - rows.start
                    row = lax.broadcasted_iota(jnp.int32, (n_rows, width), 0) + rows.start
                    col = lax.broadcasted_iota(jnp.int32, (n_rows, width), 1)
                    pt = jnp.where(row <= col, pt, 0.0)
                p_scr[rows, :width] = _mx(pt)
                ds_scr[rows, :width] = _mx(pt * (dpt_scr[rows, :width] - dsum) * scale)
            ds_t = ds_scr[:, :width]
            dv_ref[0] += _dot(p_scr[:, :width], dov)
            dk_ref[0] += _dot(ds_t, qv)
            dq_ref[0, pl.ds(start, width), :] += _dot_tn(ds_t, kv)

        def group_loop(first_block, n_blk, count):
            def loop_body(g, carry):
                tile(first_block + n_blk * g, n_blk, False)
                return carry

            lax.fori_loop(0, count, loop_body, 0)

        tile(j, 1, True)
        done = j + 1
        for n_blk in _BWD_GROUPS:
            count = (nb - done) // n_blk
            group_loop(done, n_blk, count)
            done = done + n_blk * count

    return _call(
        body, (q, do, lse_rows, dsum_rows, k, v), sides, name="mla_bwd", grid=(hn, nb),
        in_specs=[pl.BlockSpec((1, t, dq_w), lambda h, j: (h, 0, 0)),
                  pl.BlockSpec((1, t, dv_w), lambda h, j: (h, 0, 0)),
                  pl.BlockSpec((1, nb, 1, bq), lambda h, j: (h, 0, 0, 0)),
                  pl.BlockSpec((1, nb, 1, bq), lambda h, j: (h, 0, 0, 0)),
                  pl.BlockSpec((1, bq, dq_w), lambda h, j: (h, j, 0)),
                  pl.BlockSpec((1, bq, dv_w), lambda h, j: (h, j, 0))],
        out_specs=[pl.BlockSpec((1, t, dq_w), lambda h, j: (h, 0, 0)),
                   pl.BlockSpec((1, bq, dq_w), lambda h, j: (h, j, 0)),
                   pl.BlockSpec((1, bq, dv_w), lambda h, j: (h, j, 0))],
        out_shape=[_sds((hn, t, dq_w), jnp.float32), _sds((hn, t, dq_w), jnp.float32), _sds((hn, t, dv_w), jnp.float32)],
        scratch_shapes=[pltpu.VMEM((bq, wide), jnp.float32), pltpu.VMEM((bq, wide), jnp.float32),
                        pltpu.VMEM((bq, wide), _MXU), pltpu.VMEM((bq, wide), _MXU)],
        semantics=("parallel", "arbitrary"))


STACK = SWA_GROUP * SWA_BLOCK


def _swa_stack(ref, c, rows):
    low = _low_half()
    parts = []
    for g in range(SWA_GROUP):
        tv = ref[SWA_GROUP // 2 * c + g // 2, rows, :]
        keep = low if g % 2 == 0 else jnp.logical_not(low)
        parts.append(_mx(jnp.where(keep, tv, jnp.zeros_like(tv))))
    return jnp.concatenate(parts, axis=0)


def _swa_cols(ref, c, rows):
    return jnp.concatenate([ref[SWA_GROUP * c + g, rows, 0:1] for g in range(SWA_GROUP)], axis=0)


def _swa_sink_col(s_ref, c):
    return jnp.concatenate([jnp.broadcast_to(s_ref[SWA_GROUP * c + g][:, 0:1], (SWA_BLOCK, 1))
                            for g in range(SWA_GROUP)], axis=0)


def _swa_band_masks():
    row = lax.broadcasted_iota(jnp.int32, (STACK, SWA_BLOCK), 0) & (SWA_BLOCK - 1)
    col = lax.broadcasted_iota(jnp.int32, (STACK, SWA_BLOCK), 1)
    return col <= row, col > row


def _swa_band(has_previous):
    row = lax.broadcasted_iota(jnp.int32, (STACK, 2 * SWA_BLOCK), 0) & (SWA_BLOCK - 1)
    col = lax.broadcasted_iota(jnp.int32, (STACK, 2 * SWA_BLOCK), 1)
    before = jnp.logical_and(col < SWA_BLOCK, col > row)
    if has_previous is not True:
        before = jnp.logical_and(before, has_previous)
    return jnp.logical_or(before, jnp.logical_and(col >= SWA_BLOCK, col - SWA_BLOCK <= row))


def _swa_keys(ref, prev_ref, c, b):
    if b == 0:
        return jnp.concatenate([prev_ref[c], ref[c, 0:SWA_BLOCK, :]], axis=0)
    return ref[c, (b - 1) * SWA_BLOCK:(b + 1) * SWA_BLOCK, :]


def _swa_unstack_pairs(ref, c, rows, stacked):
    for pr in range(SWA_GROUP // 2):
        r0 = 2 * pr * SWA_BLOCK
        ref[SWA_GROUP // 2 * c + pr, rows, :] = _pick_halves(stacked[r0:r0 + SWA_BLOCK], stacked[r0 + SWA_BLOCK:r0 + 2 * SWA_BLOCK])


def _swa_blocks(t):
    nblk = t // SWA_BLOCK
    bps = min(_SWA_STEP, nblk)
    return nblk, bps, bps * SWA_BLOCK


def _swa_fwd(q, k, v, sinks):
    _, t, _ = q.shape
    nblk, bps, sb = _swa_blocks(t)
    scale = SWA_D ** -0.5

    def body(q_ref, k_ref, kp_ref, v_ref, vp_ref, s_ref, o_ref, l_ref):
        n = pl.program_id(0)
        band_first, band = _swa_band(n > 0), _swa_band(True)
        for c in range(SWA_KV):
            sink = _swa_sink_col(s_ref, c)
            for b in range(bps):
                rows = slice(b * SWA_BLOCK, (b + 1) * SWA_BLOCK)
                qs = _swa_stack(q_ref, c, rows)
                s = jnp.where(band_first if b == 0 else band, _dot_nt(qs, _swa_keys(k_ref, kp_ref, c, b)) * scale, NEG)
                m = jnp.maximum(_rowmax(s), sink)
                e = jnp.exp(s - m)
                denom = _rowsum(e) + jnp.exp(sink - m)
                o = _dot(_mx(e * (1.0 / denom)), _swa_keys(v_ref, vp_ref, c, b))
                lse = m + jnp.log(denom)
                for g in range(SWA_GROUP):
                    l_ref[SWA_GROUP * c + g, rows, :] = jnp.broadcast_to(
                        lse[g * SWA_BLOCK:(g + 1) * SWA_BLOCK], (SWA_BLOCK, LANES))
                _swa_unstack_pairs(o_ref, c, rows, o)

    main = lambda n: (0, n, 0)
    prev = lambda n: (0, jnp.maximum(n * bps - 1, 0), 0)
    return pl.pallas_call(
        body, name="swa_fwd", grid=(nblk // bps,),
        in_specs=[pl.BlockSpec((SWA_PAIRS, sb, LANES), main),
                  pl.BlockSpec((SWA_KV, sb, LANES), main), pl.BlockSpec((SWA_KV, SWA_BLOCK, LANES), prev),
                  pl.BlockSpec((SWA_KV, sb, LANES), main), pl.BlockSpec((SWA_KV, SWA_BLOCK, LANES), prev),
                  _const((SWA_HEADS, 1, LANES))],
        out_specs=[pl.BlockSpec((SWA_PAIRS, sb, LANES), main), pl.BlockSpec((SWA_HEADS, sb, LANES), main)],
        out_shape=[_sds((SWA_PAIRS, t, LANES), jnp.float32), _sds((SWA_HEADS, t, LANES), jnp.float32)],
        compiler_params=_cparams(("parallel",)),
    )(q, k, k, v, v, sinks)


def _swa_bwd(q, k, v, sinks, do, lse, dsum):
    _, t, _ = q.shape
    nblk, bps, sb = _swa_blocks(t)
    steps = nblk // bps
    scale = SWA_D ** -0.5

    def body(q_ref, k_ref, kp_ref, v_ref, vp_ref, s_ref, do_ref, l_ref, d_ref, qn_ref, don_ref, ln_ref, dn_ref,
             dq_ref, dk_ref, dv_ref, ds_ref):
        n = pl.program_id(0)

        @pl.when(n == 0)
        def _():
            ds_ref[...] = jnp.zeros_like(ds_ref)

        _, m_prev = _swa_band_masks()
        band_first, band = _swa_band(n > 0), _swa_band(True)
        everything = slice(0, SWA_BLOCK)

        def probs(qs, keys, mask, lcol):
            return jnp.where(mask, jnp.exp(_dot_nt(qs, keys) * scale - lcol), 0.0)

        def dscores(pm, dos, vals, dcol):
            return _mx(pm * (_dot_nt(dos, vals) - dcol) * scale)

        for c in range(SWA_KV):
            sink = _swa_sink_col(s_ref, c)
            dk_acc = [jnp.zeros((SWA_BLOCK, LANES), jnp.float32) for _ in range(bps)]
            dv_acc = [jnp.zeros((SWA_BLOCK, LANES), jnp.float32) for _ in range(bps)]
            for b in range(bps):
                rows = slice(b * SWA_BLOCK, (b + 1) * SWA_BLOCK)
                keys, vals = _swa_keys(k_ref, kp_ref, c, b), _swa_keys(v_ref, vp_ref, c, b)
                qs = _swa_stack(q_ref, c, rows)
                dos = _swa_stack(do_ref, c, rows)
                lcol = _swa_cols(l_ref, c, rows)
                dcol = _swa_cols(d_ref, c, rows)
                pm = probs(qs, keys, band_first if b == 0 else band, lcol)
                ds = dscores(pm, dos, vals, dcol)
                _swa_unstack_pairs(dq_ref, c, rows, _dot(ds, keys))
                dk_both = _dot_tn(ds, qs)
                dv_both = _dot_tn(_mx(pm), dos)
                dk_acc[b] = dk_acc[b] + dk_both[SWA_BLOCK:]
                dv_acc[b] = dv_acc[b] + dv_both[SWA_BLOCK:]
                if b > 0:
                    dk_acc[b - 1] = dk_acc[b - 1] + dk_both[:SWA_BLOCK]
                    dv_acc[b - 1] = dv_acc[b - 1] + dv_both[:SWA_BLOCK]
                p_sink = jnp.exp(sink - lcol) * dcol
                for g in range(SWA_GROUP):
                    ds_ref[SWA_GROUP * c + g] += -jnp.sum(p_sink[g * SWA_BLOCK:(g + 1) * SWA_BLOCK])
            tail = slice((bps - 1) * SWA_BLOCK, bps * SWA_BLOCK)
            kc, vc = k_ref[c, tail, :], v_ref[c, tail, :]
            qs = _swa_stack(qn_ref, c, everything)
            dos = _swa_stack(don_ref, c, everything)
            lcol = _swa_cols(ln_ref, c, everything)
            dcol = _swa_cols(dn_ref, c, everything)
            p_p = probs(qs, kc, jnp.logical_and(m_prev, n < steps - 1), lcol)
            ds_p = dscores(p_p, dos, vc, dcol)
            dk_acc[bps - 1] = dk_acc[bps - 1] + _dot_tn(ds_p, qs)
            dv_acc[bps - 1] = dv_acc[bps - 1] + _dot_tn(_mx(p_p), dos)
            for b in range(bps):
                rows = slice(b * SWA_BLOCK, (b + 1) * SWA_BLOCK)
                dk_ref[c, rows, :] = dk_acc[b]
                dv_ref[c, rows, :] = dv_acc[b]

    main = lambda n: (0, n, 0)
    prev = lambda n: (0, jnp.maximum(n * bps - 1, 0), 0)
    nxt = lambda n: (0, jnp.minimum((n + 1) * bps, nblk - 1), 0)
    pairs = pl.BlockSpec((SWA_PAIRS, sb, LANES), main)
    kvs = pl.BlockSpec((SWA_KV, sb, LANES), main)
    kv_prev = pl.BlockSpec((SWA_KV, SWA_BLOCK, LANES), prev)
    stats = pl.BlockSpec((SWA_HEADS, sb, LANES), main)
    pairs_next = pl.BlockSpec((SWA_PAIRS, SWA_BLOCK, LANES), nxt)
    stats_next = pl.BlockSpec((SWA_HEADS, SWA_BLOCK, LANES), nxt)
    return pl.pallas_call(
        body, name="swa_bwd", grid=(steps,),
        in_specs=[pairs, kvs, kv_prev, kvs, kv_prev, _const((SWA_HEADS, 1, LANES)), pairs, stats, stats,
                  pairs_next, pairs_next, stats_next, stats_next],
        out_specs=[pairs, kvs, kvs, _acc((SWA_HEADS, 1, LANES))],
        out_shape=[_sds((SWA_PAIRS, t, LANES), jnp.float32), _sds((SWA_KV, t, LANES), jnp.float32),
                   _sds((SWA_KV, t, LANES), jnp.float32), _sds((SWA_HEADS, 1, LANES), jnp.float32)],
        compiler_params=_cparams(("arbitrary",)),
    )(q, k, k, v, v, sinks, do, lse, dsum, q, do, lse, dsum)


MIX_SLABS = 4
MIX_WIDTH = MIX_SLABS * LANES


def _mix_out_fwd(x, oa, ob, ga, gb, wo_a, wo_b):
    t, d = x.shape
    tb = min(_TB, t)

    def body(x_ref, oa_ref, ob_ref, ga_ref, gb_ref, woa_ref, wob_ref, xo_ref):
        y = x_ref[...]
        for o_ref, g_ref, w_ref in ((oa_ref, ga_ref, woa_ref), (ob_ref, gb_ref, wob_ref)):
            r = _rsq(sum(_sumsq(o_ref[h]) for h in range(MIX_SLABS)), MIX_WIDTH)
            for h in range(MIX_SLABS):
                y = y + _dot(_mx(o_ref[h] * r * g_ref[h]), w_ref[h])
        xo_ref[...] = y

    slab = _heads_rows(MIX_SLABS, tb, LANES)
    return pl.pallas_call(
        body, name="mix_out_fwd", grid=(t // tb,),
        in_specs=[_rows(tb, d), slab, slab, _const(ga.shape), _const(gb.shape), _const(wo_a.shape), _const(wo_b.shape)],
        out_specs=_rows(tb, d),
        out_shape=_sds((t, d), jnp.float32),
        compiler_params=_cparams(("parallel",)),
    )(x, oa, ob, ga, gb, wo_a, wo_b)


def _mix_out_bwd(dx, oa, ob, ga, gb, wo_a, wo_b):
    t, d = dx.shape
    tb = min(_TB, t)

    def group(o_ref, g_ref, w_ref, dyb, do_ref, n_ref, col0, dg_ref):
        r = _rsq(sum(_sumsq(o_ref[h]) for h in range(MIX_SLABS)), MIX_WIDTH)
        xh, dn = [], []
        for h in range(MIX_SLABS):
            xh.append(o_ref[h] * r)
            n_ref[:, col0 + h * LANES:col0 + (h + 1) * LANES] = _mx(xh[h] * g_ref[h])
            dm = _dot_nt(dyb, w_ref[h])
            dg_ref[h] += _colsum(dm * xh[h])
            dn.append(dm * g_ref[h])
        c = sum(_rowsum(dn[h] * xh[h]) for h in range(MIX_SLABS)) * (1.0 / MIX_WIDTH)
        prods = []
        for h in range(MIX_SLABS):
            do = r * (dn[h] - xh[h] * c)
            do_ref[h] = do.astype(do_ref.dtype)
            prods.append(do * o_ref[h])
        return prods

    def body(dx_ref, oa_ref, ob_ref, ga_ref, gb_ref, woa_ref, wob_ref,
             doa_ref, dsa_ref, dob_ref, dsb_ref, n_ref, dy_ref, dga_ref, dgb_ref):
        @pl.when(pl.program_id(0) == 0)
        def _():
            dga_ref[...] = jnp.zeros_like(dga_ref)
            dgb_ref[...] = jnp.zeros_like(dgb_ref)

        dyb = _mx(dx_ref[...])
        dy_ref[...] = dyb
        for h, pr in enumerate(group(oa_ref, ga_ref, woa_ref, dyb, doa_ref, n_ref, 0, dga_ref)):
            dsa_ref[h] = jnp.broadcast_to(_rowsum(pr), (tb, LANES))
        low = _low_half()
        for j, pr in enumerate(group(ob_ref, gb_ref, wob_ref, dyb, dob_ref, n_ref, MIX_WIDTH, dgb_ref)):
            dsb_ref[2 * j] = jnp.broadcast_to(_rowsum(jnp.where(low, pr, 0.0)), (tb, LANES))
            dsb_ref[2 * j + 1] = jnp.broadcast_to(_rowsum(jnp.where(low, 0.0, pr)), (tb, LANES))

    slab = _heads_rows(MIX_SLABS, tb, LANES)
    return pl.pallas_call(
        body, name="mix_out_bwd", grid=(t // tb,),
        in_specs=[_rows(tb, d), slab, slab, _const(ga.shape), _const(gb.shape), _const(wo_a.shape), _const(wo_b.shape)],
        out_specs=[slab, slab, slab, _heads_rows(SWA_HEADS, tb, LANES), _rows(tb, 2 * MIX_WIDTH), _rows(tb, d),
                   _acc(ga.shape), _acc(gb.shape)],
        out_shape=[_sds((MIX_SLABS, t, LANES), _MXU), _sds((MIX_SLABS, t, LANES), jnp.float32),
                   _sds((MIX_SLABS, t, LANES), jnp.float32), _sds((SWA_HEADS, t, LANES), jnp.float32),
                   _sds((t, 2 * MIX_WIDTH), _MXU), _sds((t, d), _MXU),
                   _sds(ga.shape, jnp.float32), _sds(gb.shape, jnp.float32)],
        compiler_params=_cparams(("arbitrary",)),
    )(dx, oa, ob, ga, gb, wo_a, wo_b)


def _loss_head(y, target):
    t, d = y.shape
    tb = min(_TB, t)

    def body(y_ref, t_ref, dy_ref, acc_ref):
        @pl.when(pl.program_id(0) == 0)
        def _():
            acc_ref[...] = jnp.zeros_like(acc_ref)

        err = y_ref[...] - t_ref[...]
        dy_ref[...] = err * (1.0 / d)
        acc_ref[...] += jnp.sum(err * err)

    return pl.pallas_call(
        body, name="loss_head", grid=(t // tb,),
        in_specs=[_rows(tb, d), _rows(tb, d)],
        out_specs=[_rows(tb, d), _acc((8, LANES))],
        out_shape=[_sds((t, d), jnp.float32), _sds((8, LANES), jnp.float32)],
        compiler_params=_cparams(("arbitrary",)),
    )(y, target)


def _is_transposed(name):
    return name not in ROW_SHARDED


def _pack_layer(shards, l, width, names):
    rows = [(shards[n][l].T if _is_transposed(n) else shards[n][l]).reshape(-1, width) for n in names]
    return [jnp.concatenate(rows, axis=0)] if names is OTHER_BIG else rows


def _full_shape(like, name):
    _, a, b = like[name].shape
    return (N_DEV * b, a) if _is_transposed(name) else (N_DEV * a, b)


def _unpack_full(gathered, like, names):
    if names is not OTHER_BIG:
        return {n: g.reshape(_full_shape(like, n)) for n, g in zip(names, gathered)}
    (gathered,), out, off = gathered, {}, 0
    for n in names:
        rows_n = like[n][0].size // gathered.shape[-1]
        out[n] = gathered[:, off:off + rows_n].reshape(_full_shape(like, n))
        off += rows_n
    return out


def _stored(a, name):
    return jnp.swapaxes(a, 1, 2) if _is_transposed(name) else a


def _grads_by_destination(grads, width, names):
    by_dest = lambda n: grads[n].reshape(N_DEV, -1, width)
    if names is OTHER_BIG:
        return [jnp.concatenate([by_dest(n) for n in names], axis=1)]
    return [by_dest(n) for n in names]


def _shards_from_rows(rows, like):
    out = dict(zip(FFN_BIG, rows[:len(FFN_BIG)]))
    rest, off = rows[len(FFN_BIG)], 0
    for n in OTHER_BIG:
        _, a, b = like[n].shape
        rows_n = a * b // rest.shape[-1]
        out[n] = rest[off:off + rows_n].reshape((b, a) if _is_transposed(n) else (a, b))
        off += rows_n
    return out


def _small_rows(n_elems):
    return -(-n_elems // LANES)


def _pack_small(arrays):
    parts = []
    for n in SMALL:
        v = arrays[n]
        depth, width = v.shape
        padded = _small_rows(width) * LANES
        parts.append(jnp.pad(v, ((0, 0), (0, padded - width))).reshape(-1, LANES))
    packed = jnp.concatenate(parts, axis=0)
    return jnp.pad(packed, ((0, (-packed.shape[0]) % 8), (0, 0)))


def _unpack_small(packed, like):
    out, off = {}, 0
    for n in SMALL:
        depth, width = like[n].shape
        rows_n = _small_rows(width)
        seg = packed[off:off + depth * rows_n].reshape(depth, rows_n * LANES)
        out[n] = seg[:, :width]
        off += depth * rows_n
    return out


def _rope_tables(t):
    pos = jnp.arange(t, dtype=jnp.float32)
    inv = 1.0 / (ROPE_THETA ** (jnp.arange(0, MLA_ROPE, 2, dtype=jnp.float32) / MLA_ROPE))
    ang = pos[:, None] * inv[None, :]
    cos, sin = jnp.cos(ang), jnp.sin(ang)
    return jnp.concatenate([cos, cos, cos, cos], axis=1), jnp.concatenate([-sin, sin, -sin, sin], axis=1)


def _pad_lanes(a, width):
    return jnp.pad(a, [(0, 0)] * (a.ndim - 1) + [(0, width - a.shape[-1])])


def _ffn_params(full, small, l, tag):
    return small[tag + "_norm"][l][None, :], full[tag + "_w_gate"], full[tag + "_w_up"], full[tag + "_w_down"]


def _mixer_params(full, small, l):
    w_in = full["w_in"]
    d = w_in.shape[1]
    mla_rows = W_IN_COLS[0]
    w_in_p = jnp.concatenate([w_in[:mla_rows], jnp.zeros((LANES - MLA_ROPE, d), w_in.dtype), w_in[mla_rows:]], axis=0)
    wqb = full["mla_w_q_b"].reshape(MLA_HEADS, MLA_QK, MLA_Q_RANK)
    wqb = jnp.pad(wqb, ((0, 0), (0, MLA_QK_PAD - MLA_QK), (0, 0)))
    row = lambda name: small[name][l][None, :]
    twice = lambda g: jnp.concatenate([g, g], axis=1)
    prep = {
        "mix_g": row("mix_norm"), "w_in": w_in_p,
        "g_qa": row("mla_q_a_norm"), "wqb": wqb,
        "g_kva": row("mla_kv_a_norm"), "w_kvb": full["mla_w_kv_b"],
        "gq_n": row("mla_q_norm")[:, :MLA_NOPE], "gq_r": _pad_lanes(row("mla_q_norm")[:, MLA_NOPE:], LANES),
        "gk_n": row("mla_k_norm")[:, :MLA_NOPE], "gk_r": _pad_lanes(row("mla_k_norm")[:, MLA_NOPE:], LANES),
        "g_sq": twice(row("swa_q_norm")), "g_sk": twice(row("swa_k_norm")),
    }
    return {
        "prep": prep,
        "sinks": jnp.broadcast_to(small["swa_sinks"][l][:, None, None], (SWA_HEADS, 1, LANES)),
        "ga": small["mla_out_norm"][l].reshape(MIX_SLABS, 1, LANES),
        "gb": small["swa_out_norm"][l].reshape(MIX_SLABS, 1, LANES),
        "wo_a": full["w_o"][:MIX_WIDTH].reshape(MIX_SLABS, LANES, d),
        "wo_b": full["w_o"][MIX_WIDTH:].reshape(MIX_SLABS, LANES, d),
    }


def _ffn_backward(x_in, dxo, kept, params, tag, sides=()):
    gain, wg, wu, wd = params
    h, s, fa, fu = kept
    (dxi, da, du, dy, dg), side_out = _ffn_dgrad(x_in, gain, dxo, fa, fu, wg, wu, wd, sides)
    dwg, _ = _tn_matmul(da, h, "wgrad_" + tag + "_gate")
    dwu, _ = _tn_matmul(du, h, "wgrad_" + tag + "_up")
    dwd, _ = _tn_matmul(s, dy, "wgrad_" + tag + "_down")
    return dxi, dg[0], dwg, dwu, dwd, side_out


def _ffn_backward_exchanging(x_in, dxo, kept, params, tag, ready, chip_sums, width):
    gain, wg, wu, wd = params
    h, s, fa, fu = kept
    (dxi, da, du, dy, dg), (ready_sib,) = _ffn_dgrad(x_in, gain, dxo, fa, fu, wg, wu, wd, [_side_sibling(ready)])
    ready_owns, ready_wires = chip_sums(ready, ready_sib)
    dwd, (ready_recv,) = _tn_matmul(s, dy, "wgrad_" + tag + "_down", [_side_chips(ready_wires)])
    down = [dwd.reshape(N_DEV, -1, width)]
    dwg, (down_sib,) = _tn_matmul(da, h, "wgrad_" + tag + "_gate", [_side_sibling(down)])
    down_owns, down_wires = chip_sums(down, down_sib)
    dwu, (down_recv,) = _tn_matmul(du, h, "wgrad_" + tag + "_up", [_side_chips(down_wires)])
    return dxi, dg[0], dwg, dwu, _rs_sum(ready_owns, ready_recv), _rs_sum(down_owns, down_recv)


def kernel(x, ffn1_norm, ffn1_w_gate, ffn1_w_up, ffn1_w_down, mix_norm, w_in, mla_q_a_norm, mla_w_q_b, mla_kv_a_norm, mla_w_kv_b, mla_q_norm, mla_k_norm, swa_q_norm, swa_k_norm, swa_sinks, mla_out_norm, swa_out_norm, w_o, ffn2_norm, ffn2_w_gate, ffn2_w_up, ffn2_w_down, loss_target, m_ffn1_norm, m_ffn1_w_gate, m_ffn1_w_up, m_ffn1_w_down, m_mix_norm, m_w_in, m_mla_q_a_norm, m_mla_w_q_b, m_mla_kv_a_norm, m_mla_w_kv_b, m_mla_q_norm, m_mla_k_norm, m_swa_q_norm, m_swa_k_norm, m_swa_sinks, m_mla_out_norm, m_swa_out_norm, m_w_o, m_ffn2_norm, m_ffn2_w_gate, m_ffn2_w_up, m_ffn2_w_down, v_ffn1_norm, v_ffn1_w_gate, v_ffn1_w_up, v_ffn1_w_down, v_mix_norm, v_w_in, v_mla_q_a_norm, v_mla_w_q_b, v_mla_kv_a_norm, v_mla_w_kv_b, v_mla_q_norm, v_mla_k_norm, v_swa_q_norm, v_swa_k_norm, v_swa_sinks, v_mla_out_norm, v_swa_out_norm, v_w_o, v_ffn2_norm, v_ffn2_w_gate, v_ffn2_w_up, v_ffn2_w_down):
    local = dict(locals())
    w = {n: local[n] for n in WEIGHTS}
    m = {n: local["m_" + n] for n in WEIGHTS}
    v = {n: local["v_" + n] for n in WEIGHTS}
    depth = ffn1_norm.shape[0]
    t, d = x.shape[-2], x.shape[-1]
    x2d = x.reshape(t, d)
    target = loss_target.reshape(t, d)
    bq = min(_BQ, t)

    big = {n: w[n] for n in BIG}
    packed = [[[_mx(a) for a in _pack_layer(big, l, d, names)] for names in GATHER_ORDER] for l in range(depth)]
    cos, sin_s = _rope_tables(t)
    x_i, y_i, c_i = _position()
    dest_idx = jnp.stack([4 * px + 2 * py + c_i for px, py in _relations(x_i, y_i)]).astype(jnp.int32)

    params, saved = [], []
    xc = x2d
    ffn1_full = _unpack_full(_all_gather(packed[0][0]), big, FFN1_BIG)
    for l in range(depth):
        pr = {"ffn1": _ffn_params(ffn1_full, w, l, "ffn1")}
        x0 = xc
        (x1, *kept1), (mixer_gathered,) = _ffn_fwd(x0, *pr["ffn1"], sides=[_side_gather(packed[l][1])])
        pr.update(_mixer_params(_unpack_full(mixer_gathered, big, OTHER_BIG), w, l))
        qa, ka, va, qb, kb, vb = _prep_fwd(x1, cos, sin_s, pr["prep"])
        (oa, lse_a), (ffn2_gathered,) = _mla_fwd(qa, ka, va, sides=[_side_gather(packed[l][2])])
        pr["ffn2"] = _ffn_params(_unpack_full(ffn2_gathered, big, FFN2_BIG), w, l, "ffn2")
        ob, lse_b = _swa_fwd(qb, kb, vb, pr["sinks"])
        x2 = _mix_out_fwd(x1, oa, ob, pr["ga"], pr["gb"], pr["wo_a"], pr["wo_b"])
        (x3, *kept2), next_gathered = _ffn_fwd(
            x2, *pr["ffn2"], sides=[_side_gather(packed[l + 1][0])] if l + 1 < depth else [])
        if next_gathered:
            ffn1_full = _unpack_full(next_gathered[0], big, FFN1_BIG)
        params.append(pr)
        saved.append((x0, kept1, x1, qa, ka, va, qb, kb, vb, oa, lse_a, ob, lse_b, x2, kept2))
        xc = x3

    dx, sq_err = _loss_head(xc, target)
    loss = lax.psum(0.5 / d * sq_err[0, 0], MESH_AXES)

    def chip_sums(arrays, sibling_parts):
        sums = _rs_chip_sums(arrays, sibling_parts, dest_idx)
        return sums[:len(arrays)], sums[len(arrays):]

    grad_shards = [None] * depth
    small_grads = {n: [None] * depth for n in SMALL}
    upper = None
    for l in reversed(range(depth)):
        pr = params[l]
        lowest = l == 0
        x0, kept1, x1, qa, ka, va, qb, kb, vb, oa, lse_a, ob, lse_b, x2, kept2 = saved[l]
        g = {}
        dx, small_grads["ffn2_norm"][l], g["ffn2_w_gate"], g["ffn2_w_up"], g["ffn2_w_down"], side_out = _ffn_backward(
            x2, dx, kept2, pr["ffn2"], "ffn2", [_side_sibling(upper)] if upper else [])
        if upper:
            upper_owns, upper_wires = chip_sums(upper, side_out[0])
        early = _grads_by_destination(g, d, FFN2_BIG) if lowest else None

        doa, dsum_a, dob, dsum_b, mixed, dyb, dga, dgb = _mix_out_bwd(
            dx, oa, ob, pr["ga"], pr["gb"], pr["wo_a"], pr["wo_b"])
        small_grads["mla_out_norm"][l] = dga.reshape(-1)
        small_grads["swa_out_norm"][l] = dgb.reshape(-1)
        g["w_o"], _ = _tn_matmul(mixed, dyb, "wgrad_wo")

        rows_of = lambda s: s[:, :, 0].reshape(MLA_HEADS, t // bq, 1, bq)
        sides = ([_side_chips(upper_wires)] if upper else []) + ([_side_sibling(early)] if lowest else [])
        (dqa, dka, dva), side_out = _mla_bwd(qa, ka, va, doa, rows_of(lse_a), rows_of(dsum_a), sides)
        if upper:
            grad_shards[l + 1] = _shards_from_rows(_rs_sum(upper_owns, side_out[0]), big)
        if lowest:
            early_owns, early_wires = chip_sums(early, side_out[-1])
        dqb, dkb, dvb, dsinks = _swa_bwd(qb, kb, vb, pr["sinks"], dob, lse_b, dsum_b)
        small_grads["swa_sinks"][l] = dsinks[:, 0, 0]

        outs, side_out = _prep_bwd(x1, dx, cos, sin_s, pr["prep"], dqa, dka, dva, dqb, dkb, dvb,
                                   [_side_chips(early_wires)] if lowest else [])
        if lowest:
            early_rows = _rs_sum(early_owns, side_out[0])
        dx = outs[0]
        pg = dict(zip(PREP_WEIGHTS, outs[1:]))
        g["w_in"] = jnp.concatenate([pg["w_in"][:W_IN_COLS[0]], pg["w_in"][C_QS:]], axis=0)
        g["mla_w_q_b"] = pg["wqb"][:, :MLA_QK].reshape(MLA_HEADS * MLA_QK, MLA_Q_RANK)
        g["mla_w_kv_b"] = pg["w_kvb"]
        fold = lambda gg: gg[0, :HALF] + gg[0, HALF:]
        small_grads["mix_norm"][l] = pg["mix_g"][0]
        small_grads["mla_q_a_norm"][l] = pg["g_qa"][0]
        small_grads["mla_kv_a_norm"][l] = pg["g_kva"][0]
        small_grads["mla_q_norm"][l] = jnp.concatenate([pg["gq_n"][0], pg["gq_r"][0, :MLA_ROPE]])
        small_grads["mla_k_norm"][l] = jnp.concatenate([pg["gk_n"][0], pg["gk_r"][0, :MLA_ROPE]])
        small_grads["swa_q_norm"][l] = fold(pg["g_sq"])
        small_grads["swa_k_norm"][l] = fold(pg["g_sk"])

        if lowest:
            other = _grads_by_destination(g, d, OTHER_BIG)
            dx, small_grads["ffn1_norm"][l], g["ffn1_w_gate"], g["ffn1_w_up"], other_rows, down_rows = (
                _ffn_backward_exchanging(x0, dx, kept1, pr["ffn1"], "ffn1", other, chip_sums, d))
            late = _grads_by_destination(g, d, FFN1_BIG[:2])
            late_owns, late_wires = chip_sums(late, _rs_sibling_exchange(late))
            late_rows = _rs_sum(late_owns, _rs_chip_exchange(late_wires))
            grad_shards[l] = _shards_from_rows(late_rows + down_rows + early_rows + other_rows, big)
        else:
            dx, small_grads["ffn1_norm"][l], g["ffn1_w_gate"], g["ffn1_w_up"], g["ffn1_w_down"], _ = _ffn_backward(
                x0, dx, kept1, pr["ffn1"], "ffn1")
            upper = _grads_by_destination(g, d, FFN_BIG) + _grads_by_destination(g, d, OTHER_BIG)

    grad_big, delta_big, new_m_big, new_v_big = {}, {}, {}, {}
    for n in BIG:
        g_st = jnp.stack([grad_shards[l][n] for l in range(depth)])
        d_st, m_st, v_st = _adamw(g_st, _stored(w[n], n), _stored(m[n], n), _stored(v[n], n), n)
        grad_big[n], delta_big[n], new_m_big[n], new_v_big[n] = (_stored(a, n) for a in (g_st, d_st, m_st, v_st))

    small_partial = _pack_small({n: jnp.stack(small_grads[n]) for n in SMALL})
    grad_small = _unpack_small(_all_reduce_small(small_partial), w)
    updated = _adamw_small(*[[a[n] for n in SMALL] for a in (grad_small, w, m, v)])
    count = len(SMALL)
    delta_small, new_m_small, new_v_small = (dict(zip(SMALL, updated[k * count:(k + 1) * count])) for k in range(3))

    def ordered(big, small):
        return [big[n] if n in big else small[n] for n in WEIGHTS]

    return (loss, dx.reshape(x.shape), *ordered(grad_big, grad_small), *ordered(delta_big, delta_small),
            *ordered(new_m_big, new_m_small), *ordered(new_v_big, new_v_small))
```

```python
import jax
import jax.numpy as jnp
from jax import lax
from jax.experimental import pallas as pl
from jax.experimental.pallas import tpu as pltpu

N_DEV = 8
EPS = 1e-6
ROPE_THETA = 10000.0
MLA_HEADS = 4
MLA_Q_RANK = 256
MLA_KV_RANK = 128
MLA_NOPE = 128
MLA_ROPE = 64
MLA_V = 128
MLA_QK = MLA_NOPE + MLA_ROPE
MLA_QK_PAD = 256
SWA_HEADS = 8
SWA_KV = 2
SWA_GROUP = SWA_HEADS // SWA_KV
SWA_D = 64
SWA_BLOCK = 128
ADAM_LR = 0.001
ADAM_B1 = 0.9
ADAM_B2 = 0.999
ADAM_EPS = 1e-08
ADAM_WD = 0.01
ADAM_STEP = 10

LANES = 128
HALF = LANES // 2
SWA_PAIRS = SWA_HEADS // 2
W_IN_COLS = (MLA_Q_RANK + MLA_KV_RANK + MLA_ROPE, SWA_HEADS * SWA_D + 2 * SWA_KV * SWA_D)
VMEM_LIMIT = 56 * 1024 * 1024

_MXU = jnp.bfloat16
_TB = 256
_TB_MIX = 512
_BQ = 512
_STRIP = 32
_BWD_GROUPS = (4, 2, 1)
_TK = 1024
_SWA_STEP = 4
RS_ROW_BLOCKS = 2

BIG = ("ffn1_w_gate", "ffn1_w_up", "ffn1_w_down", "w_in", "mla_w_q_b", "mla_w_kv_b", "w_o",
       "ffn2_w_gate", "ffn2_w_up", "ffn2_w_down")
ROW_SHARDED = ("ffn1_w_down", "w_o", "ffn2_w_down")
FFN1_BIG = ("ffn1_w_gate", "ffn1_w_up", "ffn1_w_down")
FFN2_BIG = ("ffn2_w_gate", "ffn2_w_up", "ffn2_w_down")
FFN_BIG = FFN1_BIG + FFN2_BIG
OTHER_BIG = ("w_o", "w_in", "mla_w_q_b", "mla_w_kv_b")
GATHER_ORDER = (FFN1_BIG, OTHER_BIG, FFN2_BIG)
SMALL = ("ffn1_norm", "mix_norm", "mla_q_a_norm", "mla_kv_a_norm", "mla_q_norm", "mla_k_norm",
         "swa_q_norm", "swa_k_norm", "swa_sinks", "mla_out_norm", "swa_out_norm", "ffn2_norm")
WEIGHTS = ("ffn1_norm", "ffn1_w_gate", "ffn1_w_up", "ffn1_w_down", "mix_norm", "w_in", "mla_q_a_norm",
           "mla_w_q_b", "mla_kv_a_norm", "mla_w_kv_b", "mla_q_norm", "mla_k_norm", "swa_q_norm",
           "swa_k_norm", "swa_sinks", "mla_out_norm", "swa_out_norm", "w_o", "ffn2_norm",
           "ffn2_w_gate", "ffn2_w_up", "ffn2_w_down")
MESH_AXES = ("x", "y", "c")
MESH = pl.DeviceIdType.MESH
NEG = -1e30
LOG2_E = 1.4426950408889634


def _f32(t):
    return t.astype(jnp.float32)


def _mx(t):
    return t.astype(_MXU)


def _dot(a, b):
    return jnp.dot(a, b, preferred_element_type=jnp.float32)


def _dot_nt(a, b):
    return lax.dot_general(a, b, (((1,), (1,)), ((), ())), preferred_element_type=jnp.float32)


def _dot_tn(a, b):
    return lax.dot_general(a, b, (((0,), (0,)), ((), ())), preferred_element_type=jnp.float32)


def _rsq(ss, n):
    return lax.rsqrt(ss * (1.0 / n) + EPS)


def _sumsq(t):
    return jnp.sum(t * t, axis=-1, keepdims=True)


def _rowsum(t):
    return jnp.sum(t, axis=-1, keepdims=True)


def _rowmax(t):
    return jnp.max(t, axis=-1, keepdims=True)


def _colsum(t):
    return jnp.sum(t, axis=0, keepdims=True)


def _lane():
    return lax.broadcasted_iota(jnp.int32, (1, LANES), 1)


def _low_half():
    return _lane() < HALF


def _swap32(t):
    return jnp.where((_lane() & 32) == 0, pltpu.roll(t, 96, 1), pltpu.roll(t, 32, 1))


def _rope(t, cos, sin_signed):
    return t * cos + _swap32(t) * sin_signed


def _rope_bwd(d, cos, sin_signed):
    return d * cos + _swap32(d * sin_signed)


def _half_sums(t):
    low = _low_half()
    return jnp.where(low, _rowsum(jnp.where(low, t, 0.0)), _rowsum(jnp.where(low, 0.0, t)))


def _dup_halves(pair):
    low = _low_half()
    swapped = pltpu.roll(pair, HALF, 1)
    return jnp.where(low, pair, swapped), jnp.where(low, swapped, pair)


def _undup_halves(d0, d1):
    return jnp.where(_low_half(), d0 + pltpu.roll(d0, HALF, 1), d1 + pltpu.roll(d1, HALF, 1))


def _pick_halves(a, b):
    return jnp.where(_low_half(), a, b)


def _as_row(col):
    return jnp.transpose(jnp.broadcast_to(col, (col.shape[0], LANES)))[0:1, :]


def _norm_bwd(dn_list, xh_list, r, n):
    c = sum(_rowsum(dn * xh) for dn, xh in zip(dn_list, xh_list)) * (1.0 / n)
    return [r * (dn - xh * c) for dn, xh in zip(dn_list, xh_list)]


def _cparams(semantics):
    return pltpu.CompilerParams(dimension_semantics=semantics, vmem_limit_bytes=VMEM_LIMIT)


def _const(shape):
    nd = len(shape)
    return pl.BlockSpec(shape, lambda *_: (0,) * nd, pipeline_mode=pl.Buffered(1))


def _acc(shape):
    nd = len(shape)
    return pl.BlockSpec(shape, lambda *_: (0,) * nd)


def _rows(tb, width):
    return pl.BlockSpec((tb, width), lambda i: (i, 0))


def _heads_rows(h, tb, width):
    return pl.BlockSpec((h, tb, width), lambda i: (0, i, 0))


def _sds(shape, dtype):
    return jax.ShapeDtypeStruct(shape, dtype)


def _position():
    return lax.axis_index("x"), lax.axis_index("y"), lax.axis_index("c")


def _all_gather(xs):
    n = len(xs)

    def body(*refs):
        x_refs, out_refs, (send_sems, recv_sems, local_sems) = refs[:n], refs[n:2 * n], refs[2 * n:]
        x, y, c = _position()
        me, sibling = (x, y, c), (x, y, 1 - c)
        chips = [(1 - x, y), (x, 1 - y), (1 - x, 1 - y)]

        def rows(i, px, py, pc):
            return out_refs[i].at[4 * px + 2 * py + pc]

        def copy(i, k, block, to, from_input=False):
            return _remote(x_refs[i] if from_input else rows(i, *block), rows(i, *block),
                           send_sems.at[k * n + i], recv_sems.at[k * n + i], to)

        every = range(n)
        mine = [pltpu.make_async_copy(x_refs[i], rows(i, *me), local_sems.at[i]) for i in every]
        first = [copy(i, 0, me, sibling, True) for i in every]
        first += [copy(i, 1 + j, me, (*chip, c), True) for j, chip in enumerate(chips) for i in every]
        for cp in mine + first:
            cp.start()
        passed = [[copy(i, 4 + j, (*chip, c), sibling) for i in every] for j, chip in enumerate(chips)]
        for j, chip in enumerate(chips):
            for i in every:
                copy(i, 1 + j, (*chip, c), me).wait_recv()
                passed[j][i].start()
        for i in every:
            copy(i, 0, sibling, me).wait_recv()
        for j, chip in enumerate(chips):
            for i in every:
                copy(i, 4 + j, (*chip, 1 - c), me).wait_recv()
        for cp in first + [cp for group in passed for cp in group]:
            cp.wait_send()
        for cp in mine:
            cp.wait()

    hbm = pl.BlockSpec(memory_space=pl.ANY)
    dma = pltpu.SemaphoreType.DMA
    return pl.pallas_call(
        body, name="ag_weights",
        out_shape=[_sds((N_DEV,) + a.shape, a.dtype) for a in xs],
        in_specs=[hbm] * n, out_specs=[hbm] * n,
        scratch_shapes=[dma((7 * n,)), dma((7 * n,)), dma((n,))],
    )(*xs)


def _relations(x, y):
    return [(x, y), (1 - x, y), (x, 1 - y), (1 - x, 1 - y)]


def _remote(src, dst, send_sem, recv_sem, device):
    return pltpu.make_async_remote_copy(src_ref=src, dst_ref=dst, send_sem=send_sem, recv_sem=recv_sem,
                                        device_id=device, device_id_type=MESH)


def _sibling_copies(g_refs, out_refs, send, recv):
    x, y, c = _position()
    n = len(g_refs)
    return [_remote(g.at[4 * px + 2 * py + (1 - c)], o.at[k], send.at[k * n + i], recv.at[k * n + i], (x, y, 1 - c))
            for k, (px, py) in enumerate(_relations(x, y)) for i, (g, o) in enumerate(zip(g_refs, out_refs))]


def _chip_copies(w_refs, out_refs, send, recv):
    x, y, c = _position()
    n = len(w_refs)
    return [_remote(w.at[k + 1], o.at[k], send.at[k * n + i], recv.at[k * n + i], (px, py, c))
            for k, (px, py) in enumerate(_relations(x, y)[1:]) for i, (w, o) in enumerate(zip(w_refs, out_refs))]


def _exchange(arrays, lead, relations, copies_fn, name):
    n = len(arrays)

    def body(*refs):
        copies = copies_fn(refs[:n], refs[n:2 * n], refs[2 * n], refs[2 * n + 1])
        for cp in copies:
            cp.start()
        for cp in copies:
            cp.wait()

    hbm = pl.BlockSpec(memory_space=pl.ANY)
    dma = pltpu.SemaphoreType.DMA
    return pl.pallas_call(
        body, name=name, out_shape=[_sds((lead,) + a.shape[1:], a.dtype) for a in arrays],
        in_specs=[hbm] * n, out_specs=[hbm] * n,
        scratch_shapes=[dma((relations * n,)), dma((relations * n,))],
    )(*arrays)


def _rs_sibling_exchange(gs):
    return _exchange(gs, 4, 4, _sibling_copies, "rs_sibling_exchange")


def _rs_chip_exchange(wires):
    return _exchange(wires, 3, 3, _chip_copies, "rs_chip_exchange")


def _side_exchange(arrays, lead, relations, copies_fn):
    shapes = [_sds((lead,) + a.shape[1:], a.dtype) for a in arrays]
    return list(arrays), shapes, relations * len(arrays), lambda ins, outs, send, recv, local: copies_fn(ins, outs, send, recv)


def _side_sibling(gs):
    return _side_exchange(gs, 4, 4, _sibling_copies)


def _side_chips(wires):
    return _side_exchange(wires, 3, 3, _chip_copies)


def _rs_chip_sums(gs, sibs, dest_idx):
    n = len(gs)

    def body(idx_ref, *refs):
        g_refs, s_refs, own_refs, wire_refs = refs[:n], refs[n:2 * n], refs[2 * n:3 * n], refs[3 * n:]
        totals = [g[0] + s[0] for g, s in zip(g_refs, s_refs)]
        for total, wire in zip(totals, wire_refs):
            wire[0] = total.astype(wire.dtype)

        @pl.when(pl.program_id(1) == 0)
        def _():
            for total, own in zip(totals, own_refs):
                own[...] = total

    def blocks(a, index_map, squeeze):
        rb = a.shape[1] // RS_ROW_BLOCKS
        return pl.BlockSpec((rb, a.shape[2]) if squeeze else (1, rb, a.shape[2]), index_map)

    return pl.pallas_call(
        body, name="rs_chip_sums",
        grid_spec=pltpu.PrefetchScalarGridSpec(
            num_scalar_prefetch=1, grid=(RS_ROW_BLOCKS, 4),
            in_specs=[blocks(g, lambda r, k, idx: (idx[k], r, 0), False) for g in gs]
            + [blocks(g, lambda r, k, idx: (k, r, 0), False) for g in gs],
            out_specs=[blocks(g, lambda r, k, idx: (r, 0), True) for g in gs]
            + [blocks(g, lambda r, k, idx: (k, r, 0), False) for g in gs]),
        out_shape=[_sds(g.shape[1:], jnp.float32) for g in gs] + [_sds((4,) + g.shape[1:], _MXU) for g in gs],
        compiler_params=_cparams(("parallel", "arbitrary")),
    )(dest_idx, *gs, *sibs)


def _side_gather(xs):
    n = len(xs)

    def make(ins, outs, send, recv, local):
        x, y, c = _position()
        me = 4 * x + 2 * y + c
        copies = [pltpu.make_async_copy(x_ref, out_ref.at[me], local.at[i])
                  for i, (x_ref, out_ref) in enumerate(zip(ins, outs))]
        for k in range(1, N_DEV):
            peer = (1 - x if k & 4 else x, 1 - y if k & 2 else y, 1 - c if k & 1 else c)
            copies += [_remote(x_ref, out_ref.at[me], send.at[(k - 1) * n + i], recv.at[(k - 1) * n + i], peer)
                       for i, (x_ref, out_ref) in enumerate(zip(ins, outs))]
        return copies

    return list(xs), [_sds((N_DEV,) + a.shape, a.dtype) for a in xs], (N_DEV - 1) * n, make


def _call(body, args, sides, *, name, grid, in_specs, out_specs, out_shape, scratch_shapes=(), semantics):
    in_specs, out_specs, out_shape = list(in_specs), list(out_specs), list(out_shape)
    n_in, n_out, n_scr = len(in_specs), len(out_specs), len(scratch_shapes)
    sides = list(sides or [])
    if not sides:
        outs = pl.pallas_call(body, name=name, grid=grid, in_specs=in_specs, out_specs=out_specs, out_shape=out_shape,
                              scratch_shapes=list(scratch_shapes), compiler_params=_cparams(semantics))(*args)
        return list(outs), []
    arrays = [a for side in sides for a in side[0]]
    shapes = [s for side in sides for s in side[1]]
    n_side_in, n_side_out = len(arrays), len(shapes)
    hbm = pl.BlockSpec(memory_space=pl.ANY)

    def with_copies(*refs):
        main_in, refs = refs[:n_in], refs[n_in:]
        side_in, refs = refs[:n_side_in], refs[n_side_in:]
        main_out, refs = refs[:n_out], refs[n_out:]
        side_out, refs = refs[:n_side_out], refs[n_side_out:]
        main_scr, sems = refs[:n_scr], refs[n_scr:]
        copies = []
        for k, (side_arrays, side_shapes, _, make) in enumerate(sides):
            copies += make(side_in[:len(side_arrays)], side_out[:len(side_shapes)], *sems[3 * k:3 * k + 3])
            side_in, side_out = side_in[len(side_arrays):], side_out[len(side_shapes):]
        ids = [pl.program_id(a) for a in range(len(grid))]
        first, last = ids[0] == 0, ids[0] == grid[0] - 1
        for i, size in zip(ids[1:], grid[1:]):
            first, last = jnp.logical_and(first, i == 0), jnp.logical_and(last, i == size - 1)

        @pl.when(first)
        def _():
            for cp in copies:
                cp.start()

        body(*main_in, *main_out, *main_scr)

        @pl.when(last)
        def _():
            for cp in copies:
                cp.wait()

    dma = pltpu.SemaphoreType.DMA
    sem_shapes = [dma((n,)) for side in sides for n in (side[2], side[2], len(side[0]))]
    outs = pl.pallas_call(
        with_copies, name=name, grid=grid, in_specs=in_specs + [hbm] * n_side_in,
        out_specs=out_specs + [hbm] * n_side_out, out_shape=out_shape + shapes,
        scratch_shapes=list(scratch_shapes) + sem_shapes,
        compiler_params=_cparams(("arbitrary",) * len(grid)),
    )(*args, *arrays)
    side_outs, rest = [], list(outs[n_out:])
    for side in sides:
        side_outs.append(rest[:len(side[1])])
        rest = rest[len(side[1]):]
    return list(outs[:n_out]), side_outs


def _all_reduce_small(v):
    rows_n = v.shape[0]

    def body(v_ref, out_ref, buf, send_sems, recv_sems):
        x, y, c = _position()
        me = 4 * x + 2 * y + c
        buf[me] = v_ref[...]
        copies = []
        for k in range(1, N_DEV):
            px = 1 - x if k & 4 else x
            py = 1 - y if k & 2 else y
            pc = 1 - c if k & 1 else c
            copies.append(pltpu.make_async_remote_copy(
                src_ref=v_ref, dst_ref=buf.at[me],
                send_sem=send_sems.at[k - 1], recv_sem=recv_sems.at[k - 1], device_id=(px, py, pc), device_id_type=MESH))
        for cp in copies:
            cp.start()
        for cp in copies:
            cp.wait()
        total = buf[0]
        for d in range(1, N_DEV):
            total = total + buf[d]
        out_ref[...] = total

    return pl.pallas_call(
        body, name="ar_small",
        out_shape=_sds((rows_n, LANES), jnp.float32),
        in_specs=[pl.BlockSpec(memory_space=pltpu.VMEM)],
        out_specs=pl.BlockSpec(memory_space=pltpu.VMEM),
        scratch_shapes=[pltpu.VMEM((N_DEV, rows_n, LANES), jnp.float32),
                        pltpu.SemaphoreType.DMA((N_DEV - 1,)), pltpu.SemaphoreType.DMA((N_DEV - 1,))],
    )(v)


def _adamw_math(w, g, m, v):
    m = ADAM_B1 * m + (1.0 - ADAM_B1) * g
    v = ADAM_B2 * v + (1.0 - ADAM_B2) * (g * g)
    m_hat = m / (1.0 - ADAM_B1 ** ADAM_STEP)
    v_hat = v / (1.0 - ADAM_B2 ** ADAM_STEP)
    delta = -ADAM_LR * (m_hat / (jnp.sqrt(v_hat) + ADAM_EPS) + ADAM_WD * w)
    return delta, m, v


def _rs_sum(owns, recvs):
    n = len(owns)

    def body(*refs):
        own_refs, recv_refs, out_refs = refs[:n], refs[n:4 * n], refs[4 * n:]
        for i in range(n):
            r0, r1, r2 = recv_refs[3 * i:3 * i + 3]
            out_refs[i][...] = ((own_refs[i][...] + _f32(r0[0])) + _f32(r1[0])) + _f32(r2[0])

    def row(a):
        return pl.BlockSpec((a.shape[0] // RS_ROW_BLOCKS, a.shape[1]), lambda r: (r, 0))

    def slot(a, k):
        return pl.BlockSpec((1, a.shape[0] // RS_ROW_BLOCKS, a.shape[1]), lambda r: (k, r, 0))

    return pl.pallas_call(
        body, name="rs_sum", grid=(RS_ROW_BLOCKS,),
        in_specs=[row(a) for a in owns] + [slot(a, k) for a in owns for k in range(3)],
        out_specs=[row(a) for a in owns],
        out_shape=[_sds(a.shape, jnp.float32) for a in owns],
        compiler_params=_cparams(("parallel",)),
    )(*owns, *[r for r in recvs for _ in range(3)])


def _adamw(g, w, m, v, name):
    depth, a, b = w.shape

    def body(g_ref, w_ref, m_ref, v_ref, d_out, m_out, v_out):
        delta, m2, v2 = _adamw_math(w_ref[...], g_ref[...], m_ref[...], v_ref[...])
        d_out[...] = delta
        m_out[...] = m2
        v_out[...] = v2

    layer = pl.BlockSpec((1, a, b), lambda l: (l, 0, 0))
    return pl.pallas_call(
        body, name="adamw_" + name, grid=(depth,),
        in_specs=[layer] * 4, out_specs=[layer] * 3,
        out_shape=[_sds(w.shape, jnp.float32)] * 3,
        compiler_params=_cparams(("parallel",)),
    )(g, w, m, v)


def _adamw_small(gs, ws, ms, vs):
    n = len(gs)

    def body(*refs):
        g_refs, w_refs, m_refs, v_refs, outs = refs[:n], refs[n:2 * n], refs[2 * n:3 * n], refs[3 * n:4 * n], refs[4 * n:]
        for i in range(n):
            delta, m2, v2 = _adamw_math(w_refs[i][...], g_refs[i][...], m_refs[i][...], v_refs[i][...])
            outs[i][...] = delta
            outs[n + i][...] = m2
            outs[2 * n + i][...] = v2

    vm = pl.BlockSpec(memory_space=pltpu.VMEM)
    return pl.pallas_call(
        body, name="adamw_small",
        in_specs=[vm] * (4 * n), out_specs=[vm] * (3 * n),
        out_shape=[_sds(a.shape, jnp.float32) for a in ws] * 3,
    )(*gs, *ws, *ms, *vs)


def _f_chunk(f):
    for cand in (1408, 1024, 512, 256, 128):
        if f % cand == 0:
            return cand
    return f


def _ffn_fwd(x, gain, wg, wu, wd, sides=()):
    t, d = x.shape
    f = wg.shape[0]
    tb = min(_TB, t)
    fc = _f_chunk(f)

    def body(x_ref, g_ref, wg_ref, wu_ref, wd_ref, xo_ref, h_ref, s_ref, fa_ref, fu_ref):
        xv = x_ref[...]
        hb = _mx(xv * _rsq(_sumsq(xv), d) * g_ref[...])
        h_ref[...] = hb
        y = jnp.zeros((tb, d), jnp.float32)
        for c0 in range(0, f, fc):
            a = _dot_nt(hb, wg_ref[c0:c0 + fc, :])
            u = _dot_nt(hb, wu_ref[c0:c0 + fc, :])
            sig = jax.nn.sigmoid(a)
            silu = a * sig
            s = _mx(silu * u)
            s_ref[:, c0:c0 + fc] = s
            fa_ref[:, c0:c0 + fc] = _mx(u * (sig * (1.0 + a * (1.0 - sig))))
            fu_ref[:, c0:c0 + fc] = _mx(silu)
            y = y + _dot(s, wd_ref[c0:c0 + fc, :])
        xo_ref[...] = xv + 0.5 * y

    return _call(
        body, (x, gain, wg, wu, wd), sides, name="ffn_fwd", grid=(t // tb,),
        in_specs=[_rows(tb, d), _const((1, d)), _const((f, d)), _const((f, d)), _const((f, d))],
        out_specs=[_rows(tb, d), _rows(tb, d), _rows(tb, f), _rows(tb, f), _rows(tb, f)],
        out_shape=[_sds((t, d), jnp.float32), _sds((t, d), _MXU), _sds((t, f), _MXU), _sds((t, f), _MXU),
                   _sds((t, f), _MXU)],
        semantics=("parallel",))


def _ffn_dgrad(x, gain, dxo, fa, fu, wg, wu, wd, sides=()):
    t, d = x.shape
    f = wg.shape[0]
    tb = min(_TB, t)
    fc = _f_chunk(f)

    def body(x_ref, g_ref, dxo_ref, fa_ref, fu_ref, wg_ref, wu_ref, wd_ref, dxi_ref, da_ref, du_ref, dy_ref, dg_ref):
        xv = x_ref[...]
        gv = g_ref[...]
        r = _rsq(_sumsq(xv), d)
        xhat = xv * r
        dxo = dxo_ref[...]
        dyb = _mx(0.5 * dxo)
        dy_ref[...] = dyb
        dh = jnp.zeros((tb, d), jnp.float32)
        for c0 in range(0, f, fc):
            ds = _dot_nt(dyb, wd_ref[c0:c0 + fc, :])
            da = _mx(ds * _f32(fa_ref[:, c0:c0 + fc]))
            du = _mx(ds * _f32(fu_ref[:, c0:c0 + fc]))
            da_ref[:, c0:c0 + fc] = da
            du_ref[:, c0:c0 + fc] = du
            dh = dh + _dot(da, wg_ref[c0:c0 + fc, :]) + _dot(du, wu_ref[c0:c0 + fc, :])

        @pl.when(pl.program_id(0) == 0)
        def _():
            dg_ref[...] = jnp.zeros_like(dg_ref)

        dg_ref[...] += _colsum(dh * xhat)
        dn = dh * gv
        dxi_ref[...] = dxo + r * (dn - xhat * (_rowsum(dn * xhat) * (1.0 / d)))

    return _call(
        body, (x, gain, dxo, fa, fu, wg, wu, wd), sides, name="ffn_dgrad", grid=(t // tb,),
        in_specs=[_rows(tb, d), _const((1, d)), _rows(tb, d), _rows(tb, f), _rows(tb, f),
                  _const((f, d)), _const((f, d)), _const((f, d))],
        out_specs=[_rows(tb, d), _rows(tb, f), _rows(tb, f), _rows(tb, d), _acc((1, d))],
        out_shape=[_sds((t, d), jnp.float32), _sds((t, f), _MXU), _sds((t, f), _MXU), _sds((t, d), _MXU),
                   _sds((1, d), jnp.float32)],
        semantics=("arbitrary",))


def _tn_matmul(a, b, name, sides=()):
    t, m = a.shape
    n = b.shape[1]
    tk = min(_TK, t)
    tn = n
    while m * tn * 4 > 12 * 1024 * 1024 and tn % 256 == 0:
        tn //= 2

    def body(a_ref, b_ref, o_ref):
        @pl.when(pl.program_id(1) == 0)
        def _():
            o_ref[...] = jnp.zeros_like(o_ref)

        o_ref[...] += _dot_tn(a_ref[...], b_ref[...])

    (out,), side_outs = _call(
        body, (a, b), sides, name=name, grid=(n // tn, t // tk),
        in_specs=[pl.BlockSpec((tk, m), lambda j, k: (k, 0)), pl.BlockSpec((tk, tn), lambda j, k: (k, j))],
        out_specs=[pl.BlockSpec((m, tn), lambda j, k: (0, j))],
        out_shape=[_sds((m, n), jnp.float32)],
        semantics=("parallel", "arbitrary"))
    return out, side_outs


PREP_WEIGHTS = ("mix_g", "w_in", "g_qa", "wqb", "g_kva", "w_kvb", "gq_n", "gq_r", "gk_n", "gk_r", "g_sq", "g_sk")
C_CQ, C_CKV, C_KPE, C_QS = 0, MLA_Q_RANK, MLA_Q_RANK + MLA_KV_RANK, MLA_Q_RANK + MLA_KV_RANK + LANES
C_KS = C_QS + SWA_HEADS * SWA_D
C_VS = C_KS + LANES
W_IN_PACKED = C_VS + LANES


def _prep_specs(p):
    return [_const(p[n].shape) for n in PREP_WEIGHTS]


def _pair_norm_rope(t, gain, cos, sin_s):
    return _rope(t * _rsq(_half_sums(t * t), SWA_D) * gain, cos, sin_s)


def _prep_fwd(x, cos, sin_s, p):
    t, d = x.shape
    tb = min(_TB_MIX, t)

    def body(x_ref, cos_ref, sin_ref, mix_g, w_in, g_qa, wqb, g_kva, w_kvb, gq_n, gq_r, gk_n, gk_r, g_sq, g_sk,
             qa_ref, ka_ref, va_ref, qb_ref, kb_ref, vb_ref):
        xv = x_ref[...]
        cos_v, sin_v = cos_ref[...], sin_ref[...]
        hb = _mx(xv * _rsq(_sumsq(xv), d) * mix_g[...])
        proj = _dot_nt(hb, w_in[...])
        cq = proj[:, C_CQ:C_CKV]
        cqn = _mx(cq * _rsq(_sumsq(cq), MLA_Q_RANK) * g_qa[...])
        for h in range(MLA_HEADS):
            qh = _dot_nt(cqn, wqb[h])
            qn, qr = qh[:, :MLA_NOPE], qh[:, MLA_NOPE:]
            rh = _rsq(_sumsq(qn) + _sumsq(qr), MLA_QK)
            qa_ref[h, :, 0:MLA_NOPE] = (qn * rh * gq_n[...]).astype(qa_ref.dtype)
            qa_ref[h, :, MLA_NOPE:MLA_QK_PAD] = _rope(qr * rh * gq_r[...], cos_v, sin_v).astype(qa_ref.dtype)
        ckv = proj[:, C_CKV:C_KPE]
        ckvn = _mx(ckv * _rsq(_sumsq(ckv), MLA_KV_RANK) * g_kva[...])
        kpe = proj[:, C_KPE:C_QS]
        ss_pe = _sumsq(kpe)
        kv = _dot_nt(ckvn, w_kvb[...])
        for h in range(MLA_HEADS):
            c0 = h * (MLA_NOPE + MLA_V)
            kn = kv[:, c0:c0 + MLA_NOPE]
            rh = _rsq(_sumsq(kn) + ss_pe, MLA_QK)
            ka_ref[h, :, 0:MLA_NOPE] = (kn * rh * gk_n[...]).astype(ka_ref.dtype)
            ka_ref[h, :, MLA_NOPE:MLA_QK_PAD] = _rope(kpe * rh * gk_r[...], cos_v, sin_v).astype(ka_ref.dtype)
            va_ref[h] = kv[:, c0 + MLA_NOPE:c0 + MLA_NOPE + MLA_V].astype(va_ref.dtype)
        for j in range(SWA_PAIRS):
            c0 = C_QS + j * LANES
            qb_ref[j] = _pair_norm_rope(proj[:, c0:c0 + LANES], g_sq[...], cos_v, sin_v).astype(qb_ref.dtype)
        k0, k1 = _dup_halves(_pair_norm_rope(proj[:, C_KS:C_VS], g_sk[...], cos_v, sin_v))
        kb_ref[0] = k0.astype(kb_ref.dtype)
        kb_ref[1] = k1.astype(kb_ref.dtype)
        v0, v1 = _dup_halves(proj[:, C_VS:W_IN_PACKED])
        vb_ref[0] = v0.astype(vb_ref.dtype)
        vb_ref[1] = v1.astype(vb_ref.dtype)

    return pl.pallas_call(
        body, name="prep_fwd", grid=(t // tb,),
        in_specs=[_rows(tb, d), _rows(tb, LANES), _rows(tb, LANES)] + _prep_specs(p),
        out_specs=[_heads_rows(MLA_HEADS, tb, MLA_QK_PAD), _heads_rows(MLA_HEADS, tb, MLA_QK_PAD),
                   _heads_rows(MLA_HEADS, tb, MLA_V), _heads_rows(SWA_PAIRS, tb, LANES),
                   _heads_rows(SWA_KV, tb, LANES), _heads_rows(SWA_KV, tb, LANES)],
        out_shape=[_sds((MLA_HEADS, t, MLA_QK_PAD), _MXU), _sds((MLA_HEADS, t, MLA_QK_PAD), _MXU),
                   _sds((MLA_HEADS, t, MLA_V), _MXU), _sds((SWA_PAIRS, t, LANES), _MXU),
                   _sds((SWA_KV, t, LANES), _MXU), _sds((SWA_KV, t, LANES), _MXU)],
        compiler_params=_cparams(("parallel",)),
    )(x, cos, sin_s, *[p[n] for n in PREP_WEIGHTS])


def _prep_bwd(x, dxin, cos, sin_s, p, dqa, dka, dva, dqb, dkb, dvb, sides=()):
    t, d = x.shape
    tb = min(_TB_MIX, t)
    n_w = len(PREP_WEIGHTS)

    def body(*refs):
        x_ref, dxin_ref, cos_ref, sin_ref = refs[:4]
        mix_g, w_in, g_qa, wqb, g_kva, w_kvb, gq_n, gq_r, gk_n, gk_r, g_sq, g_sk = refs[4:4 + n_w]
        dqa_ref, dka_ref, dva_ref, dqb_ref, dkb_ref, dvb_ref = refs[4 + n_w:10 + n_w]
        dx_ref = refs[10 + n_w]
        grads = dict(zip(PREP_WEIGHTS, refs[11 + n_w:11 + 2 * n_w]))
        dproj_ref, dkv_ref, dqh_ref = refs[11 + 2 * n_w:]

        @pl.when(pl.program_id(0) == 0)
        def _():
            for ref in grads.values():
                ref[...] = jnp.zeros_like(ref)

        xv = x_ref[...]
        cos_v, sin_v = cos_ref[...], sin_ref[...]
        r0 = _rsq(_sumsq(xv), d)
        xhat = xv * r0
        hb = _mx(xhat * mix_g[...])
        proj = _dot_nt(hb, w_in[...])

        cq = proj[:, C_CQ:C_CKV]
        rq = _rsq(_sumsq(cq), MLA_Q_RANK)
        cqh = cq * rq
        cqn = _mx(cqh * g_qa[...])
        dcqn = jnp.zeros((tb, MLA_Q_RANK), jnp.float32)
        for h in range(MLA_HEADS):
            qh = _dot_nt(cqn, wqb[h])
            qn, qr = qh[:, :MLA_NOPE], qh[:, MLA_NOPE:]
            rh = _rsq(_sumsq(qn) + _sumsq(qr), MLA_QK)
            xh_n, xh_r = qn * rh, qr * rh
            dy_n = dqa_ref[h, :, 0:MLA_NOPE]
            dy_r = _rope_bwd(dqa_ref[h, :, MLA_NOPE:MLA_QK_PAD], cos_v, sin_v)
            grads["gq_n"][...] += _colsum(dy_n * xh_n)
            grads["gq_r"][...] += _colsum(dy_r * xh_r)
            dqn, dqr = _norm_bwd([dy_n * gq_n[...], dy_r * gq_r[...]], [xh_n, xh_r], rh, MLA_QK)
            dqh_ref[:, 0:MLA_NOPE] = _mx(dqn)
            dqh_ref[:, MLA_NOPE:MLA_QK_PAD] = _mx(dqr)
            dqh = dqh_ref[...]
            grads["wqb"][h] += _dot_tn(dqh, cqn)
            dcqn = dcqn + _dot(dqh, wqb[h])
        grads["g_qa"][...] += _colsum(dcqn * cqh)
        (dcq,) = _norm_bwd([dcqn * g_qa[...]], [cqh], rq, MLA_Q_RANK)
        dproj_ref[:, C_CQ:C_CKV] = _mx(dcq)

        ckv = proj[:, C_CKV:C_KPE]
        rkv = _rsq(_sumsq(ckv), MLA_KV_RANK)
        ckvh = ckv * rkv
        ckvn = _mx(ckvh * g_kva[...])
        kpe = proj[:, C_KPE:C_QS]
        ss_pe = _sumsq(kpe)
        kv = _dot_nt(ckvn, w_kvb[...])
        dkpe = jnp.zeros((tb, LANES), jnp.float32)
        for h in range(MLA_HEADS):
            c0 = h * (MLA_NOPE + MLA_V)
            c1 = c0 + MLA_NOPE
            kn = kv[:, c0:c1]
            rh = _rsq(_sumsq(kn) + ss_pe, MLA_QK)
            xh_n, xh_r = kn * rh, kpe * rh
            dy_n = dka_ref[h, :, 0:MLA_NOPE]
            dy_r = _rope_bwd(dka_ref[h, :, MLA_NOPE:MLA_QK_PAD], cos_v, sin_v)
            grads["gk_n"][...] += _colsum(dy_n * xh_n)
            grads["gk_r"][...] += _colsum(dy_r * xh_r)
            dkn, dkr = _norm_bwd([dy_n * gk_n[...], dy_r * gk_r[...]], [xh_n, xh_r], rh, MLA_QK)
            dkpe = dkpe + dkr
            dkv_ref[:, c0:c1] = _mx(dkn)
            dkv_ref[:, c1:c1 + MLA_V] = _mx(dva_ref[h])
        dkv = dkv_ref[...]
        grads["w_kvb"][...] += _dot_tn(dkv, ckvn)
        dckvn = _dot(dkv, w_kvb[...])
        grads["g_kva"][...] += _colsum(dckvn * ckvh)
        (dckv,) = _norm_bwd([dckvn * g_kva[...]], [ckvh], rkv, MLA_KV_RANK)
        dproj_ref[:, C_CKV:C_KPE] = _mx(dckv)
        dproj_ref[:, C_KPE:C_QS] = _mx(dkpe)

        def pair_bwd(tv, dy, g_ref, gname):
            r = _rsq(_half_sums(tv * tv), SWA_D)
            xh = tv * r
            dpre = _rope_bwd(dy, cos_v, sin_v)
            grads[gname][...] += _colsum(dpre * xh)
            dn = dpre * g_ref[...]
            return r * (dn - xh * (_half_sums(dn * xh) * (1.0 / SWA_D)))

        for j in range(SWA_PAIRS):
            c0 = C_QS + j * LANES
            dproj_ref[:, c0:c0 + LANES] = _mx(pair_bwd(proj[:, c0:c0 + LANES], dqb_ref[j], g_sq, "g_sq"))
        dproj_ref[:, C_KS:C_VS] = _mx(pair_bwd(proj[:, C_KS:C_VS], _undup_halves(dkb_ref[0], dkb_ref[1]), g_sk, "g_sk"))
        dproj_ref[:, C_VS:W_IN_PACKED] = _mx(_undup_halves(dvb_ref[0], dvb_ref[1]))

        dproj = dproj_ref[...]
        grads["w_in"][...] += _dot_tn(dproj, hb)
        dh = _dot(dproj, w_in[...])
        grads["mix_g"][...] += _colsum(dh * xhat)
        (dxv,) = _norm_bwd([dh * mix_g[...]], [xhat], r0, d)
        dx_ref[...] = dxin_ref[...] + dxv

    grad_shapes = [p[n].shape for n in PREP_WEIGHTS]
    args = (x, dxin, cos, sin_s, *[p[n] for n in PREP_WEIGHTS], dqa, dka, dva, dqb, dkb, dvb)
    return _call(
        body, args, sides, name="prep_bwd", grid=(t // tb,),
        in_specs=[_rows(tb, d), _rows(tb, d), _rows(tb, LANES), _rows(tb, LANES)] + _prep_specs(p) + [
            _heads_rows(MLA_HEADS, tb, MLA_QK_PAD), _heads_rows(MLA_HEADS, tb, MLA_QK_PAD),
            _heads_rows(MLA_HEADS, tb, MLA_V), _heads_rows(SWA_PAIRS, tb, LANES),
            _heads_rows(SWA_KV, tb, LANES), _heads_rows(SWA_KV, tb, LANES)],
        out_specs=[_rows(tb, d)] + [_acc(s) for s in grad_shapes],
        out_shape=[_sds((t, d), jnp.float32)] + [_sds(s, jnp.float32) for s in grad_shapes],
        scratch_shapes=[pltpu.VMEM((tb, W_IN_PACKED), _MXU), pltpu.VMEM((tb, MLA_HEADS * (MLA_NOPE + MLA_V)), _MXU),
                        pltpu.VMEM((tb, MLA_QK_PAD), _MXU)],
        semantics=("arbitrary",))


def _strips(n):
    step = min(_STRIP, n)
    return [slice(r, r + step) for r in range(0, n, step)]


def _mla_fwd(q, k, v, sides=()):
    hn, t, dq = q.shape
    dv = v.shape[2]
    bq = min(_BQ, t)
    scale = MLA_QK ** -0.5
    scale2 = scale * LOG2_E

    def body(q_ref, k_ref, v_ref, o_ref, l_ref):
        i = pl.program_id(1)
        qv = q_ref[0]

        def step(first_block, width, carry, masked):
            m, l, acc = carry
            start = pl.multiple_of(first_block * bq, bq)
            s = _dot_nt(qv, k_ref[0, pl.ds(start, width), :])
            if masked:
                row = lax.broadcasted_iota(jnp.int32, (bq, width), 0)
                col = lax.broadcasted_iota(jnp.int32, (bq, width), 1)
                s = jnp.where(col <= row, s, NEG)
            m_new = jnp.maximum(m, _rowmax(s))
            alpha = jnp.exp2((m - m_new) * scale2)
            pv = jnp.exp2((s - m_new) * scale2)
            l = alpha * l + _rowsum(pv)
            acc = alpha * acc + _dot(_mx(pv), v_ref[0, pl.ds(start, width), :])
            return m_new, l, acc

        init = (jnp.full((bq, 1), NEG, jnp.float32), jnp.zeros((bq, 1), jnp.float32), jnp.zeros((bq, dv), jnp.float32))
        carry, done = init, 0
        for group in (4, 2, 1):
            count = (i - done) // group
            carry = lax.fori_loop(0, count, lambda g, c, done=done, group=group: step(done + group * g, group * bq, c, False), carry)
            done = done + group * count
        m, l, acc = step(i, bq, carry, True)
        o_ref[0] = acc / l
        l_ref[0, 0] = _as_row(m * scale + jnp.log(l))

    return _call(
        body, (q, k, v), sides, name="mla_fwd", grid=(hn, t // bq),
        in_specs=[pl.BlockSpec((1, bq, dq), lambda h, i: (h, i, 0)),
                  pl.BlockSpec((1, t, dq), lambda h, i: (h, 0, 0)),
                  pl.BlockSpec((1, t, dv), lambda h, i: (h, 0, 0))],
        out_specs=[pl.BlockSpec((1, bq, dv), lambda h, i: (h, i, 0)),
                   pl.BlockSpec((1, 1, 1, bq), lambda h, i: (h, i, 0, 0))],
        out_shape=[_sds((hn, t, dv), jnp.float32), _sds((hn, t // bq, 1, bq), jnp.float32)],
        semantics=("parallel", "arbitrary"))


def _mla_bwd(q, k, v, do, lse_rows, dsum_rows, sides=()):
    hn, t, dq_w = q.shape
    dv_w = v.shape[2]
    bq = min(_BQ, t)
    nb = t // bq
    wide = max(_BWD_GROUPS) * bq
    scale = MLA_QK ** -0.5

    def body(q_ref, do_ref, l_ref, d_ref, k_ref, v_ref, dq_ref, dk_ref, dv_ref, st_scr, dpt_scr, p_scr, ds_scr):
        j = pl.program_id(1)

        @pl.when(j == 0)
        def _():
            dq_ref[...] = jnp.zeros_like(dq_ref)

        kv = k_ref[0]
        vv = v_ref[0]
        dk_ref[0] = jnp.zeros((bq, dq_w), jnp.float32)
        dv_ref[0] = jnp.zeros((bq, dv_w), jnp.float32)

        def tile(first_block, n_blk, masked):
            width = n_blk * bq
            start = pl.multiple_of(first_block * bq, bq)
            qv = q_ref[0, pl.ds(start, width), :]
            dov = do_ref[0, pl.ds(start, width), :]
            st_scr[:, :width] = _dot_nt(kv, qv)
            dpt_scr[:, :width] = _dot_nt(vv, dov)
            lse2 = jnp.concatenate([l_ref[0, first_block + b] for b in range(n_blk)], axis=1) * LOG2_E
            dsum = jnp.concatenate([d_ref[0, first_block + b] for b in range(n_blk)], axis=1)
            for rows in _strips(bq):
                pt = jnp.exp2(st_scr[rows, :width] * (scale * LOG2_E) - lse2)
                if masked:
                    n_rows = rows.stop - rows.start
                    row = lax.broadcasted_iota(jnp.int32, (n_rows, width), 0) + rows.start
                    col = lax.broadcasted_iota(jnp.int32, (n_rows, width), 1)
                    pt = jnp.where(row <= col, pt, 0.0)
                p_scr[rows, :width] = _mx(pt)
                ds_scr[rows, :width] = _mx(pt * (dpt_scr[rows, :width] - dsum) * scale)
            ds_t = ds_scr[:, :width]
            dv_ref[0] += _dot(p_scr[:, :width], dov)
            dk_ref[0] += _dot(ds_t, qv)
            dq_ref[0, pl.ds(start, width), :] += _dot_tn(ds_t, kv)

        def group_loop(first_block, n_blk, count):
            def loop_body(g, carry):
                tile(first_block + n_blk * g, n_blk, False)
                return carry

            lax.fori_loop(0, count, loop_body, 0)

        tile(j, 1, True)
        done = j + 1
        for n_blk in _BWD_GROUPS:
            count = (nb - done) // n_blk
            group_loop(done, n_blk, count)
            done = done + n_blk * count

    return _call(
        body, (q, do, lse_rows, dsum_rows, k, v), sides, name="mla_bwd", grid=(hn, nb),
        in_specs=[pl.BlockSpec((1, t, dq_w), lambda h, j: (h, 0, 0)),
                  pl.BlockSpec((1, t, dv_w), lambda h, j: (h, 0, 0)),
                  pl.BlockSpec((1, nb, 1, bq), lambda h, j: (h, 0, 0, 0)),
                  pl.BlockSpec((1, nb, 1, bq), lambda h, j: (h, 0, 0, 0)),
                  pl.BlockSpec((1, bq, dq_w), lambda h, j: (h, j, 0)),
                  pl.BlockSpec((1, bq, dv_w), lambda h, j: (h, j, 0))],
        out_specs=[pl.BlockSpec((1, t, dq_w), lambda h, j: (h, 0, 0)),
                   pl.BlockSpec((1, bq, dq_w), lambda h, j: (h, j, 0)),
                   pl.BlockSpec((1, bq, dv_w), lambda h, j: (h, j, 0))],
        out_shape=[_sds((hn, t, dq_w), jnp.float32), _sds((hn, t, dq_w), jnp.float32), _sds((hn, t, dv_w), jnp.float32)],
        scratch_shapes=[pltpu.VMEM((bq, wide), jnp.float32), pltpu.VMEM((bq, wide), jnp.float32),
                        pltpu.VMEM((bq, wide), _MXU), pltpu.VMEM((bq, wide), _MXU)],
        semantics=("parallel", "arbitrary"))


STACK = SWA_GROUP * SWA_BLOCK


def _swa_stack(ref, c, rows):
    low = _low_half()
    parts = []
    for g in range(SWA_GROUP):
        tv = ref[SWA_GROUP // 2 * c + g // 2, rows, :]
        keep = low if g % 2 == 0 else jnp.logical_not(low)
        parts.append(_mx(jnp.where(keep, tv, jnp.zeros_like(tv))))
    return jnp.concatenate(parts, axis=0)


def _swa_cols(ref, c, rows):
    return jnp.concatenate([ref[SWA_GROUP * c + g, rows, 0:1] for g in range(SWA_GROUP)], axis=0)


def _swa_sink_col(s_ref, c):
    return jnp.concatenate([jnp.broadcast_to(s_ref[SWA_GROUP * c + g][:, 0:1], (SWA_BLOCK, 1))
                            for g in range(SWA_GROUP)], axis=0)


def _swa_band_masks():
    row = lax.broadcasted_iota(jnp.int32, (STACK, SWA_BLOCK), 0) & (SWA_BLOCK - 1)
    col = lax.broadcasted_iota(jnp.int32, (STACK, SWA_BLOCK), 1)
    return col <= row, col > row


def _swa_band(has_previous):
    row = lax.broadcasted_iota(jnp.int32, (STACK, 2 * SWA_BLOCK), 0) & (SWA_BLOCK - 1)
    col = lax.broadcasted_iota(jnp.int32, (STACK, 2 * SWA_BLOCK), 1)
    before = jnp.logical_and(col < SWA_BLOCK, col > row)
    if has_previous is not True:
        before = jnp.logical_and(before, has_previous)
    return jnp.logical_or(before, jnp.logical_and(col >= SWA_BLOCK, col - SWA_BLOCK <= row))


def _swa_keys(ref, prev_ref, c, b):
    if b == 0:
        return jnp.concatenate([prev_ref[c], ref[c, 0:SWA_BLOCK, :]], axis=0)
    return ref[c, (b - 1) * SWA_BLOCK:(b + 1) * SWA_BLOCK, :]


def _swa_unstack_pairs(ref, c, rows, stacked):
    for pr in range(SWA_GROUP // 2):
        r0 = 2 * pr * SWA_BLOCK
        ref[SWA_GROUP // 2 * c + pr, rows, :] = _pick_halves(stacked[r0:r0 + SWA_BLOCK], stacked[r0 + SWA_BLOCK:r0 + 2 * SWA_BLOCK])


def _swa_blocks(t):
    nblk = t // SWA_BLOCK
    bps = min(_SWA_STEP, nblk)
    return nblk, bps, bps * SWA_BLOCK


def _swa_fwd(q, k, v, sinks):
    _, t, _ = q.shape
    nblk, bps, sb = _swa_blocks(t)
    scale = SWA_D ** -0.5

    def body(q_ref, k_ref, kp_ref, v_ref, vp_ref, s_ref, o_ref, l_ref):
        n = pl.program_id(0)
        band_first, band = _swa_band(n > 0), _swa_band(True)
        for c in range(SWA_KV):
            sink = _swa_sink_col(s_ref, c)
            for b in range(bps):
                rows = slice(b * SWA_BLOCK, (b + 1) * SWA_BLOCK)
                qs = _swa_stack(q_ref, c, rows)
                s = jnp.where(band_first if b == 0 else band, _dot_nt(qs, _swa_keys(k_ref, kp_ref, c, b)) * scale, NEG)
                m = jnp.maximum(_rowmax(s), sink)
                e = jnp.exp(s - m)
                denom = _rowsum(e) + jnp.exp(sink - m)
                o = _dot(_mx(e * (1.0 / denom)), _swa_keys(v_ref, vp_ref, c, b))
                lse = m + jnp.log(denom)
                for g in range(SWA_GROUP):
                    l_ref[SWA_GROUP * c + g, rows, :] = jnp.broadcast_to(
                        lse[g * SWA_BLOCK:(g + 1) * SWA_BLOCK], (SWA_BLOCK, LANES))
                _swa_unstack_pairs(o_ref, c, rows, o)

    main = lambda n: (0, n, 0)
    prev = lambda n: (0, jnp.maximum(n * bps - 1, 0), 0)
    return pl.pallas_call(
        body, name="swa_fwd", grid=(nblk // bps,),
        in_specs=[pl.BlockSpec((SWA_PAIRS, sb, LANES), main),
                  pl.BlockSpec((SWA_KV, sb, LANES), main), pl.BlockSpec((SWA_KV, SWA_BLOCK, LANES), prev),
                  pl.BlockSpec((SWA_KV, sb, LANES), main), pl.BlockSpec((SWA_KV, SWA_BLOCK, LANES), prev),
                  _const((SWA_HEADS, 1, LANES))],
        out_specs=[pl.BlockSpec((SWA_PAIRS, sb, LANES), main), pl.BlockSpec((SWA_HEADS, sb, LANES), main)],
        out_shape=[_sds((SWA_PAIRS, t, LANES), jnp.float32), _sds((SWA_HEADS, t, LANES), jnp.float32)],
        compiler_params=_cparams(("parallel",)),
    )(q, k, k, v, v, sinks)


def _swa_bwd(q, k, v, sinks, do, lse, dsum):
    _, t, _ = q.shape
    nblk, bps, sb = _swa_blocks(t)
    steps = nblk // bps
    scale = SWA_D ** -0.5

    def body(q_ref, k_ref, kp_ref, v_ref, vp_ref, s_ref, do_ref, l_ref, d_ref, qn_ref, don_ref, ln_ref, dn_ref,
             dq_ref, dk_ref, dv_ref, ds_ref):
        n = pl.program_id(0)

        @pl.when(n == 0)
        def _():
            ds_ref[...] = jnp.zeros_like(ds_ref)

        _, m_prev = _swa_band_masks()
        band_first, band = _swa_band(n > 0), _swa_band(True)
        everything = slice(0, SWA_BLOCK)

        def probs(qs, keys, mask, lcol):
            return jnp.where(mask, jnp.exp(_dot_nt(qs, keys) * scale - lcol), 0.0)

        def dscores(pm, dos, vals, dcol):
            return _mx(pm * (_dot_nt(dos, vals) - dcol) * scale)

        for c in range(SWA_KV):
            sink = _swa_sink_col(s_ref, c)
            dk_acc = [jnp.zeros((SWA_BLOCK, LANES), jnp.float32) for _ in range(bps)]
            dv_acc = [jnp.zeros((SWA_BLOCK, LANES), jnp.float32) for _ in range(bps)]
            for b in range(bps):
                rows = slice(b * SWA_BLOCK, (b + 1) * SWA_BLOCK)
                keys, vals = _swa_keys(k_ref, kp_ref, c, b), _swa_keys(v_ref, vp_ref, c, b)
                qs = _swa_stack(q_ref, c, rows)
                dos = _swa_stack(do_ref, c, rows)
                lcol = _swa_cols(l_ref, c, rows)
                dcol = _swa_cols(d_ref, c, rows)
                pm = probs(qs, keys, band_first if b == 0 else band, lcol)
                ds = dscores(pm, dos, vals, dcol)
                _swa_unstack_pairs(dq_ref, c, rows, _dot(ds, keys))
                dk_both = _dot_tn(ds, qs)
                dv_both = _dot_tn(_mx(pm), dos)
                dk_acc[b] = dk_acc[b] + dk_both[SWA_BLOCK:]
                dv_acc[b] = dv_acc[b] + dv_both[SWA_BLOCK:]
                if b > 0:
                    dk_acc[b - 1] = dk_acc[b - 1] + dk_both[:SWA_BLOCK]
                    dv_acc[b - 1] = dv_acc[b - 1] + dv_both[:SWA_BLOCK]
                p_sink = jnp.exp(sink - lcol) * dcol
                for g in range(SWA_GROUP):
                    ds_ref[SWA_GROUP * c + g] += -jnp.sum(p_sink[g * SWA_BLOCK:(g + 1) * SWA_BLOCK])
            tail = slice((bps - 1) * SWA_BLOCK, bps * SWA_BLOCK)
            kc, vc = k_ref[c, tail, :], v_ref[c, tail, :]
            qs = _swa_stack(qn_ref, c, everything)
            dos = _swa_stack(don_ref, c, everything)
            lcol = _swa_cols(ln_ref, c, everything)
            dcol = _swa_cols(dn_ref, c, everything)
            p_p = probs(qs, kc, jnp.logical_and(m_prev, n < steps - 1), lcol)
            ds_p = dscores(p_p, dos, vc, dcol)
            dk_acc[bps - 1] = dk_acc[bps - 1] + _dot_tn(ds_p, qs)
            dv_acc[bps - 1] = dv_acc[bps - 1] + _dot_tn(_mx(p_p), dos)
            for b in range(bps):
                rows = slice(b * SWA_BLOCK, (b + 1) * SWA_BLOCK)
                dk_ref[c, rows, :] = dk_acc[b]
                dv_ref[c, rows, :] = dv_acc[b]

    main = lambda n: (0, n, 0)
    prev = lambda n: (0, jnp.maximum(n * bps - 1, 0), 0)
    nxt = lambda n: (0, jnp.minimum((n + 1) * bps, nblk - 1), 0)
    pairs = pl.BlockSpec((SWA_PAIRS, sb, LANES), main)
    kvs = pl.BlockSpec((SWA_KV, sb, LANES), main)
    kv_prev = pl.BlockSpec((SWA_KV, SWA_BLOCK, LANES), prev)
    stats = pl.BlockSpec((SWA_HEADS, sb, LANES), main)
    pairs_next = pl.BlockSpec((SWA_PAIRS, SWA_BLOCK, LANES), nxt)
    stats_next = pl.BlockSpec((SWA_HEADS, SWA_BLOCK, LANES), nxt)
    return pl.pallas_call(
        body, name="swa_bwd", grid=(steps,),
        in_specs=[pairs, kvs, kv_prev, kvs, kv_prev, _const((SWA_HEADS, 1, LANES)), pairs, stats, stats,
                  pairs_next, pairs_next, stats_next, stats_next],
        out_specs=[pairs, kvs, kvs, _acc((SWA_HEADS, 1, LANES))],
        out_shape=[_sds((SWA_PAIRS, t, LANES), jnp.float32), _sds((SWA_KV, t, LANES), jnp.float32),
                   _sds((SWA_KV, t, LANES), jnp.float32), _sds((SWA_HEADS, 1, LANES), jnp.float32)],
        compiler_params=_cparams(("arbitrary",)),
    )(q, k, k, v, v, sinks, do, lse, dsum, q, do, lse, dsum)


MIX_SLABS = 4
MIX_WIDTH = MIX_SLABS * LANES


def _mix_out_fwd(x, oa, ob, ga, gb, wo_a, wo_b):
    t, d = x.shape
    tb = min(_TB, t)

    def body(x_ref, oa_ref, ob_ref, ga_ref, gb_ref, woa_ref, wob_ref, xo_ref):
        y = x_ref[...]
        for o_ref, g_ref, w_ref in ((oa_ref, ga_ref, woa_ref), (ob_ref, gb_ref, wob_ref)):
            r = _rsq(sum(_sumsq(o_ref[h]) for h in range(MIX_SLABS)), MIX_WIDTH)
            for h in range(MIX_SLABS):
                y = y + _dot(_mx(o_ref[h] * r * g_ref[h]), w_ref[h])
        xo_ref[...] = y

    slab = _heads_rows(MIX_SLABS, tb, LANES)
    return pl.pallas_call(
        body, name="mix_out_fwd", grid=(t // tb,),
        in_specs=[_rows(tb, d), slab, slab, _const(ga.shape), _const(gb.shape), _const(wo_a.shape), _const(wo_b.shape)],
        out_specs=_rows(tb, d),
        out_shape=_sds((t, d), jnp.float32),
        compiler_params=_cparams(("parallel",)),
    )(x, oa, ob, ga, gb, wo_a, wo_b)


def _mix_out_bwd(dx, oa, ob, ga, gb, wo_a, wo_b):
    t, d = dx.shape
    tb = min(_TB, t)

    def group(o_ref, g_ref, w_ref, dyb, do_ref, n_ref, col0, dg_ref):
        r = _rsq(sum(_sumsq(o_ref[h]) for h in range(MIX_SLABS)), MIX_WIDTH)
        xh, dn = [], []
        for h in range(MIX_SLABS):
            xh.append(o_ref[h] * r)
            n_ref[:, col0 + h * LANES:col0 + (h + 1) * LANES] = _mx(xh[h] * g_ref[h])
            dm = _dot_nt(dyb, w_ref[h])
            dg_ref[h] += _colsum(dm * xh[h])
            dn.append(dm * g_ref[h])
        c = sum(_rowsum(dn[h] * xh[h]) for h in range(MIX_SLABS)) * (1.0 / MIX_WIDTH)
        prods = []
        for h in range(MIX_SLABS):
            do = r * (dn[h] - xh[h] * c)
            do_ref[h] = do.astype(do_ref.dtype)
            prods.append(do * o_ref[h])
        return prods

    def body(dx_ref, oa_ref, ob_ref, ga_ref, gb_ref, woa_ref, wob_ref,
             doa_ref, dsa_ref, dob_ref, dsb_ref, n_ref, dy_ref, dga_ref, dgb_ref):
        @pl.when(pl.program_id(0) == 0)
        def _():
            dga_ref[...] = jnp.zeros_like(dga_ref)
            dgb_ref[...] = jnp.zeros_like(dgb_ref)

        dyb = _mx(dx_ref[...])
        dy_ref[...] = dyb
        for h, pr in enumerate(group(oa_ref, ga_ref, woa_ref, dyb, doa_ref, n_ref, 0, dga_ref)):
            dsa_ref[h, 0] = _as_row(_rowsum(pr))
        low = _low_half()
        for j, pr in enumerate(group(ob_ref, gb_ref, wob_ref, dyb, dob_ref, n_ref, MIX_WIDTH, dgb_ref)):
            dsb_ref[2 * j] = jnp.broadcast_to(_rowsum(jnp.where(low, pr, 0.0)), (tb, LANES))
            dsb_ref[2 * j + 1] = jnp.broadcast_to(_rowsum(jnp.where(low, 0.0, pr)), (tb, LANES))

    slab = _heads_rows(MIX_SLABS, tb, LANES)
    return pl.pallas_call(
        body, name="mix_out_bwd", grid=(t // tb,),
        in_specs=[_rows(tb, d), slab, slab, _const(ga.shape), _const(gb.shape), _const(wo_a.shape), _const(wo_b.shape)],
        out_specs=[slab, pl.BlockSpec((MIX_SLABS, 1, 1, tb), lambda i: (0, i, 0, 0)), slab,
                   _heads_rows(SWA_HEADS, tb, LANES), _rows(tb, 2 * MIX_WIDTH), _rows(tb, d),
                   _acc(ga.shape), _acc(gb.shape)],
        out_shape=[_sds((MIX_SLABS, t, LANES), _MXU), _sds((MIX_SLABS, t // tb, 1, tb), jnp.float32),
                   _sds((MIX_SLABS, t, LANES), jnp.float32), _sds((SWA_HEADS, t, LANES), jnp.float32),
                   _sds((t, 2 * MIX_WIDTH), _MXU), _sds((t, d), _MXU),
                   _sds(ga.shape, jnp.float32), _sds(gb.shape, jnp.float32)],
        compiler_params=_cparams(("arbitrary",)),
    )(dx, oa, ob, ga, gb, wo_a, wo_b)


def _loss_head(y, target):
    t, d = y.shape
    tb = min(_TB, t)

    def body(y_ref, t_ref, dy_ref, acc_ref):
        @pl.when(pl.program_id(0) == 0)
        def _():
            acc_ref[...] = jnp.zeros_like(acc_ref)

        err = y_ref[...] - t_ref[...]
        dy_ref[...] = err * (1.0 / d)
        acc_ref[...] += jnp.sum(err * err)

    return pl.pallas_call(
        body, name="loss_head", grid=(t // tb,),
        in_specs=[_rows(tb, d), _rows(tb, d)],
        out_specs=[_rows(tb, d), _acc((8, LANES))],
        out_shape=[_sds((t, d), jnp.float32), _sds((8, LANES), jnp.float32)],
        compiler_params=_cparams(("arbitrary",)),
    )(y, target)


def _is_transposed(name):
    return name not in ROW_SHARDED


def _pack_layer(shards, l, width, names):
    rows = [(shards[n][l].T if _is_transposed(n) else shards[n][l]).reshape(-1, width) for n in names]
    return [jnp.concatenate(rows, axis=0)] if names is OTHER_BIG else rows


def _full_shape(like, name):
    _, a, b = like[name].shape
    return (N_DEV * b, a) if _is_transposed(name) else (N_DEV * a, b)


def _unpack_full(gathered, like, names):
    if names is not OTHER_BIG:
        return {n: g.reshape(_full_shape(like, n)) for n, g in zip(names, gathered)}
    (gathered,), out, off = gathered, {}, 0
    for n in names:
        rows_n = like[n][0].size // gathered.shape[-1]
        out[n] = gathered[:, off:off + rows_n].reshape(_full_shape(like, n))
        off += rows_n
    return out


def _stored(a, name):
    return jnp.swapaxes(a, 1, 2) if _is_transposed(name) else a


def _grads_by_destination(grads, width, names):
    by_dest = lambda n: grads[n].reshape(N_DEV, -1, width)
    if names is OTHER_BIG:
        return [jnp.concatenate([by_dest(n) for n in names], axis=1)]
    return [by_dest(n) for n in names]


def _shards_from_rows(rows, like):
    out = dict(zip(FFN_BIG, rows[:len(FFN_BIG)]))
    rest, off = rows[len(FFN_BIG)], 0
    for n in OTHER_BIG:
        _, a, b = like[n].shape
        rows_n = a * b // rest.shape[-1]
        out[n] = rest[off:off + rows_n].reshape((b, a) if _is_transposed(n) else (a, b))
        off += rows_n
    return out


def _small_rows(n_elems):
    return -(-n_elems // LANES)


def _pack_small(arrays):
    parts = []
    for n in SMALL:
        v = arrays[n]
        depth, width = v.shape
        padded = _small_rows(width) * LANES
        parts.append(jnp.pad(v, ((0, 0), (0, padded - width))).reshape(-1, LANES))
    packed = jnp.concatenate(parts, axis=0)
    return jnp.pad(packed, ((0, (-packed.shape[0]) % 8), (0, 0)))


def _unpack_small(packed, like):
    out, off = {}, 0
    for n in SMALL:
        depth, width = like[n].shape
        rows_n = _small_rows(width)
        seg = packed[off:off + depth * rows_n].reshape(depth, rows_n * LANES)
        out[n] = seg[:, :width]
        off += depth * rows_n
    return out


def _rope_tables(t):
    pos = jnp.arange(t, dtype=jnp.float32)
    inv = 1.0 / (ROPE_THETA ** (jnp.arange(0, MLA_ROPE, 2, dtype=jnp.float32) / MLA_ROPE))
    ang = pos[:, None] * inv[None, :]
    cos, sin = jnp.cos(ang), jnp.sin(ang)
    return jnp.concatenate([cos, cos, cos, cos], axis=1), jnp.concatenate([-sin, sin, -sin, sin], axis=1)


def _pad_lanes(a, width):
    return jnp.pad(a, [(0, 0)] * (a.ndim - 1) + [(0, width - a.shape[-1])])


def _ffn_params(full, small, l, tag):
    return small[tag + "_norm"][l][None, :], full[tag + "_w_gate"], full[tag + "_w_up"], full[tag + "_w_down"]


def _mixer_params(full, small, l):
    w_in = full["w_in"]
    d = w_in.shape[1]
    mla_rows = W_IN_COLS[0]
    w_in_p = jnp.concatenate([w_in[:mla_rows], jnp.zeros((LANES - MLA_ROPE, d), w_in.dtype), w_in[mla_rows:]], axis=0)
    wqb = full["mla_w_q_b"].reshape(MLA_HEADS, MLA_QK, MLA_Q_RANK)
    wqb = jnp.pad(wqb, ((0, 0), (0, MLA_QK_PAD - MLA_QK), (0, 0)))
    row = lambda name: small[name][l][None, :]
    twice = lambda g: jnp.concatenate([g, g], axis=1)
    prep = {
        "mix_g": row("mix_norm"), "w_in": w_in_p,
        "g_qa": row("mla_q_a_norm"), "wqb": wqb,
        "g_kva": row("mla_kv_a_norm"), "w_kvb": full["mla_w_kv_b"],
        "gq_n": row("mla_q_norm")[:, :MLA_NOPE], "gq_r": _pad_lanes(row("mla_q_norm")[:, MLA_NOPE:], LANES),
        "gk_n": row("mla_k_norm")[:, :MLA_NOPE], "gk_r": _pad_lanes(row("mla_k_norm")[:, MLA_NOPE:], LANES),
        "g_sq": twice(row("swa_q_norm")), "g_sk": twice(row("swa_k_norm")),
    }
    return {
        "prep": prep,
        "sinks": jnp.broadcast_to(small["swa_sinks"][l][:, None, None], (SWA_HEADS, 1, LANES)),
        "ga": small["mla_out_norm"][l].reshape(MIX_SLABS, 1, LANES),
        "gb": small["swa_out_norm"][l].reshape(MIX_SLABS, 1, LANES),
        "wo_a": full["w_o"][:MIX_WIDTH].reshape(MIX_SLABS, LANES, d),
        "wo_b": full["w_o"][MIX_WIDTH:].reshape(MIX_SLABS, LANES, d),
    }


def _ffn_backward(x_in, dxo, kept, params, tag, sides=()):
    gain, wg, wu, wd = params
    h, s, fa, fu = kept
    (dxi, da, du, dy, dg), side_out = _ffn_dgrad(x_in, gain, dxo, fa, fu, wg, wu, wd, sides)
    dwg, _ = _tn_matmul(da, h, "wgrad_" + tag + "_gate")
    dwu, _ = _tn_matmul(du, h, "wgrad_" + tag + "_up")
    dwd, _ = _tn_matmul(s, dy, "wgrad_" + tag + "_down")
    return dxi, dg[0], dwg, dwu, dwd, side_out


def _ffn_backward_exchanging(x_in, dxo, kept, params, tag, ready, chip_sums, width):
    gain, wg, wu, wd = params
    h, s, fa, fu = kept
    (dxi, da, du, dy, dg), (ready_sib,) = _ffn_dgrad(x_in, gain, dxo, fa, fu, wg, wu, wd, [_side_sibling(ready)])
    ready_owns, ready_wires = chip_sums(ready, ready_sib)
    dwd, (ready_recv,) = _tn_matmul(s, dy, "wgrad_" + tag + "_down", [_side_chips(ready_wires)])
    down = [dwd.reshape(N_DEV, -1, width)]
    dwg, (down_sib,) = _tn_matmul(da, h, "wgrad_" + tag + "_gate", [_side_sibling(down)])
    down_owns, down_wires = chip_sums(down, down_sib)
    dwu, (down_recv,) = _tn_matmul(du, h, "wgrad_" + tag + "_up", [_side_chips(down_wires)])
    return dxi, dg[0], dwg, dwu, _rs_sum(ready_owns, ready_recv), _rs_sum(down_owns, down_recv)


def kernel(x, ffn1_norm, ffn1_w_gate, ffn1_w_up, ffn1_w_down, mix_norm, w_in, mla_q_a_norm, mla_w_q_b, mla_kv_a_norm, mla_w_kv_b, mla_q_norm, mla_k_norm, swa_q_norm, swa_k_norm, swa_sinks, mla_out_norm, swa_out_norm, w_o, ffn2_norm, ffn2_w_gate, ffn2_w_up, ffn2_w_down, loss_target, m_ffn1_norm, m_ffn1_w_gate, m_ffn1_w_up, m_ffn1_w_down, m_mix_norm, m_w_in, m_mla_q_a_norm, m_mla_w_q_b, m_mla_kv_a_norm, m_mla_w_kv_b, m_mla_q_norm, m_mla_k_norm, m_swa_q_norm, m_swa_k_norm, m_swa_sinks, m_mla_out_norm, m_swa_out_norm, m_w_o, m_ffn2_norm, m_ffn2_w_gate, m_ffn2_w_up, m_ffn2_w_down, v_ffn1_norm, v_ffn1_w_gate, v_ffn1_w_up, v_ffn1_w_down, v_mix_norm, v_w_in, v_mla_q_a_norm, v_mla_w_q_b, v_mla_kv_a_norm, v_mla_w_kv_b, v_mla_q_norm, v_mla_k_norm, v_swa_q_norm, v_swa_k_norm, v_swa_sinks, v_mla_out_norm, v_swa_out_norm, v_w_o, v_ffn2_norm, v_ffn2_w_gate, v_ffn2_w_up, v_ffn2_w_down):
    local = dict(locals())
    w = {n: local[n] for n in WEIGHTS}
    m = {n: local["m_" + n] for n in WEIGHTS}
    v = {n: local["v_" + n] for n in WEIGHTS}
    depth = ffn1_norm.shape[0]
    t, d = x.shape[-2], x.shape[-1]
    x2d = x.reshape(t, d)
    target = loss_target.reshape(t, d)
    bq = min(_BQ, t)

    big = {n: w[n] for n in BIG}
    packed = [[[_mx(a) for a in _pack_layer(big, l, d, names)] for names in GATHER_ORDER] for l in range(depth)]
    cos, sin_s = _rope_tables(t)
    x_i, y_i, c_i = _position()
    dest_idx = jnp.stack([4 * px + 2 * py + c_i for px, py in _relations(x_i, y_i)]).astype(jnp.int32)

    params, saved = [], []
    xc = x2d
    ffn1_full = _unpack_full(_all_gather(packed[0][0]), big, FFN1_BIG)
    for l in range(depth):
        pr = {"ffn1": _ffn_params(ffn1_full, w, l, "ffn1")}
        x0 = xc
        (x1, *kept1), (mixer_gathered,) = _ffn_fwd(x0, *pr["ffn1"], sides=[_side_gather(packed[l][1])])
        pr.update(_mixer_params(_unpack_full(mixer_gathered, big, OTHER_BIG), w, l))
        qa, ka, va, qb, kb, vb = _prep_fwd(x1, cos, sin_s, pr["prep"])
        (oa, lse_a), (ffn2_gathered,) = _mla_fwd(qa, ka, va, sides=[_side_gather(packed[l][2])])
        pr["ffn2"] = _ffn_params(_unpack_full(ffn2_gathered, big, FFN2_BIG), w, l, "ffn2")
        ob, lse_b = _swa_fwd(qb, kb, vb, pr["sinks"])
        x2 = _mix_out_fwd(x1, oa, ob, pr["ga"], pr["gb"], pr["wo_a"], pr["wo_b"])
        (x3, *kept2), next_gathered = _ffn_fwd(
            x2, *pr["ffn2"], sides=[_side_gather(packed[l + 1][0])] if l + 1 < depth else [])
        if next_gathered:
            ffn1_full = _unpack_full(next_gathered[0], big, FFN1_BIG)
        params.append(pr)
        saved.append((x0, kept1, x1, qa, ka, va, qb, kb, vb, oa, lse_a, ob, lse_b, x2, kept2))
        xc = x3

    dx, sq_err = _loss_head(xc, target)
    loss = lax.psum(0.5 / d * sq_err[0, 0], MESH_AXES)

    def chip_sums(arrays, sibling_parts):
        sums = _rs_chip_sums(arrays, sibling_parts, dest_idx)
        return sums[:len(arrays)], sums[len(arrays):]

    grad_shards = [None] * depth
    small_grads = {n: [None] * depth for n in SMALL}
    upper = None
    for l in reversed(range(depth)):
        pr = params[l]
        lowest = l == 0
        x0, kept1, x1, qa, ka, va, qb, kb, vb, oa, lse_a, ob, lse_b, x2, kept2 = saved[l]
        g = {}
        dx, small_grads["ffn2_norm"][l], g["ffn2_w_gate"], g["ffn2_w_up"], g["ffn2_w_down"], side_out = _ffn_backward(
            x2, dx, kept2, pr["ffn2"], "ffn2", [_side_sibling(upper)] if upper else [])
        if upper:
            upper_owns, upper_wires = chip_sums(upper, side_out[0])
        early = _grads_by_destination(g, d, FFN2_BIG) if lowest else None

        doa, dsum_a, dob, dsum_b, mixed, dyb, dga, dgb = _mix_out_bwd(
            dx, oa, ob, pr["ga"], pr["gb"], pr["wo_a"], pr["wo_b"])
        small_grads["mla_out_norm"][l] = dga.reshape(-1)
        small_grads["swa_out_norm"][l] = dgb.reshape(-1)
        g["w_o"], _ = _tn_matmul(mixed, dyb, "wgrad_wo")

        rows_of = lambda s: s.reshape(MLA_HEADS, t // bq, 1, bq)
        sides = ([_side_chips(upper_wires)] if upper else []) + ([_side_sibling(early)] if lowest else [])
        (dqa, dka, dva), side_out = _mla_bwd(qa, ka, va, doa, rows_of(lse_a), rows_of(dsum_a), sides)
        if upper:
            grad_shards[l + 1] = _shards_from_rows(_rs_sum(upper_owns, side_out[0]), big)
        if lowest:
            early_owns, early_wires = chip_sums(early, side_out[-1])
        dqb, dkb, dvb, dsinks = _swa_bwd(qb, kb, vb, pr["sinks"], dob, lse_b, dsum_b)
        small_grads["swa_sinks"][l] = dsinks[:, 0, 0]

        outs, side_out = _prep_bwd(x1, dx, cos, sin_s, pr["prep"], dqa, dka, dva, dqb, dkb, dvb,
                                   [_side_chips(early_wires)] if lowest else [])
        if lowest:
            early_rows = _rs_sum(early_owns, side_out[0])
        dx = outs[0]
        pg = dict(zip(PREP_WEIGHTS, outs[1:]))
        g["w_in"] = jnp.concatenate([pg["w_in"][:W_IN_COLS[0]], pg["w_in"][C_QS:]], axis=0)
        g["mla_w_q_b"] = pg["wqb"][:, :MLA_QK].reshape(MLA_HEADS * MLA_QK, MLA_Q_RANK)
        g["mla_w_kv_b"] = pg["w_kvb"]
        fold = lambda gg: gg[0, :HALF] + gg[0, HALF:]
        small_grads["mix_norm"][l] = pg["mix_g"][0]
        small_grads["mla_q_a_norm"][l] = pg["g_qa"][0]
        small_grads["mla_kv_a_norm"][l] = pg["g_kva"][0]
        small_grads["mla_q_norm"][l] = jnp.concatenate([pg["gq_n"][0], pg["gq_r"][0, :MLA_ROPE]])
        small_grads["mla_k_norm"][l] = jnp.concatenate([pg["gk_n"][0], pg["gk_r"][0, :MLA_ROPE]])
        small_grads["swa_q_norm"][l] = fold(pg["g_sq"])
        small_grads["swa_k_norm"][l] = fold(pg["g_sk"])

        if lowest:
            other = _grads_by_destination(g, d, OTHER_BIG)
            dx, small_grads["ffn1_norm"][l], g["ffn1_w_gate"], g["ffn1_w_up"], other_rows, down_rows = (
                _ffn_backward_exchanging(x0, dx, kept1, pr["ffn1"], "ffn1", other, chip_sums, d))
            late = _grads_by_destination(g, d, FFN1_BIG[:2])
            late_owns, late_wires = chip_sums(late, _rs_sibling_exchange(late))
            late_rows = _rs_sum(late_owns, _rs_chip_exchange(late_wires))
            grad_shards[l] = _shards_from_rows(late_rows + down_rows + early_rows + other_rows, big)
        else:
            dx, small_grads["ffn1_norm"][l], g["ffn1_w_gate"], g["ffn1_w_up"], g["ffn1_w_down"], _ = _ffn_backward(
                x0, dx, kept1, pr["ffn1"], "ffn1")
            upper = _grads_by_destination(g, d, FFN_BIG) + _grads_by_destination(g, d, OTHER_BIG)

    grad_big, delta_big, new_m_big, new_v_big = {}, {}, {}, {}
    for n in BIG:
        g_st = jnp.stack([grad_shards[l][n] for l in range(depth)])
        d_st, m_st, v_st = _adamw(g_st, _stored(w[n], n), _stored(m[n], n), _stored(v[n], n), n)
        grad_big[n], delta_big[n], new_m_big[n], new_v_big[n] = (_stored(a, n) for a in (g_st, d_st, m_st, v_st))

    small_partial = _pack_small({n: jnp.stack(small_grads[n]) for n in SMALL})
    grad_small = _unpack_small(_all_reduce_small(small_partial), w)
    updated = _adamw_small(*[[a[n] for n in SMALL] for a in (grad_small, w, m, v)])
    count = len(SMALL)
    delta_small, new_m_small, new_v_small = (dict(zip(SMALL, updated[k * count:(k + 1) * count])) for k in range(3))

    def ordered(big, small):
        return [big[n] if n in big else small[n] for n in WEIGHTS]

    return (loss, dx.reshape(x.shape), *ordered(grad_big, grad_small), *ordered(delta_big, delta_small),
            *ordered(new_m_big, new_m_small), *ordered(new_v_big, new_v_small))
```

```python
import jax
import jax.numpy as jnp
from jax import lax
from jax.experimental import pallas as pl
from jax.experimental.pallas import tpu as pltpu

N_DEV = 8
EPS = 1e-6
ROPE_THETA = 10000.0
MLA_HEADS = 4
MLA_Q_RANK = 256
MLA_KV_RANK = 128
MLA_NOPE = 128
MLA_ROPE = 64
MLA_V = 128
MLA_QK = MLA_NOPE + MLA_ROPE
MLA_QK_PAD = 256
SWA_HEADS = 8
SWA_KV = 2
SWA_GROUP = SWA_HEADS // SWA_KV
SWA_D = 64
SWA_BLOCK = 128
ADAM_LR = 0.001
ADAM_B1 = 0.9
ADAM_B2 = 0.999
ADAM_EPS = 1e-08
ADAM_WD = 0.01
ADAM_STEP = 10

LANES = 128
HALF = LANES // 2
SWA_PAIRS = SWA_HEADS // 2
W_IN_COLS = (MLA_Q_RANK + MLA_KV_RANK + MLA_ROPE, SWA_HEADS * SWA_D + 2 * SWA_KV * SWA_D)
VMEM_LIMIT = 56 * 1024 * 1024

_MXU = jnp.bfloat16
_TB = 256
_TB_MIX = 512
_BQ = 512
_STRIP = 32
_BWD_GROUPS = (4, 2, 1)
_TK = 1024
_SWA_STEP = 4
RS_ROW_BLOCKS = 2

BIG = ("ffn1_w_gate", "ffn1_w_up", "ffn1_w_down", "w_in", "mla_w_q_b", "mla_w_kv_b", "w_o",
       "ffn2_w_gate", "ffn2_w_up", "ffn2_w_down")
ROW_SHARDED = ("ffn1_w_down", "w_o", "ffn2_w_down")
FFN1_BIG = ("ffn1_w_gate", "ffn1_w_up", "ffn1_w_down")
FFN2_BIG = ("ffn2_w_gate", "ffn2_w_up", "ffn2_w_down")
FFN_BIG = FFN1_BIG + FFN2_BIG
OTHER_BIG = ("w_o", "w_in", "mla_w_q_b", "mla_w_kv_b")
GATHER_ORDER = (FFN1_BIG, OTHER_BIG, FFN2_BIG)
SMALL = ("ffn1_norm", "mix_norm", "mla_q_a_norm", "mla_kv_a_norm", "mla_q_norm", "mla_k_norm",
         "swa_q_norm", "swa_k_norm", "swa_sinks", "mla_out_norm", "swa_out_norm", "ffn2_norm")
WEIGHTS = ("ffn1_norm", "ffn1_w_gate", "ffn1_w_up", "ffn1_w_down", "mix_norm", "w_in", "mla_q_a_norm",
           "mla_w_q_b", "mla_kv_a_norm", "mla_w_kv_b", "mla_q_norm", "mla_k_norm", "swa_q_norm",
           "swa_k_norm", "swa_sinks", "mla_out_norm", "swa_out_norm", "w_o", "ffn2_norm",
           "ffn2_w_gate", "ffn2_w_up", "ffn2_w_down")
MESH_AXES = ("x", "y", "c")
MESH = pl.DeviceIdType.MESH
NEG = -1e30
LOG2_E = 1.4426950408889634


def _f32(t):
    return t.astype(jnp.float32)


def _mx(t):
    return t.astype(_MXU)


def _dot(a, b):
    return jnp.dot(a, b, preferred_element_type=jnp.float32)


def _dot_nt(a, b):
    return lax.dot_general(a, b, (((1,), (1,)), ((), ())), preferred_element_type=jnp.float32)


def _dot_tn(a, b):
    return lax.dot_general(a, b, (((0,), (0,)), ((), ())), preferred_element_type=jnp.float32)


def _rsq(ss, n):
    return lax.rsqrt(ss * (1.0 / n) + EPS)


def _sumsq(t):
    return jnp.sum(t * t, axis=-1, keepdims=True)


def _rowsum(t):
    return jnp.sum(t, axis=-1, keepdims=True)


def _rowmax(t):
    return jnp.max(t, axis=-1, keepdims=True)


def _colsum(t):
    return jnp.sum(t, axis=0, keepdims=True)


def _lane():
    return lax.broadcasted_iota(jnp.int32, (1, LANES), 1)


def _low_half():
    return _lane() < HALF


def _swap32(t):
    return jnp.where((_lane() & 32) == 0, pltpu.roll(t, 96, 1), pltpu.roll(t, 32, 1))


def _rope(t, cos, sin_signed):
    return t * cos + _swap32(t) * sin_signed


def _rope_bwd(d, cos, sin_signed):
    return d * cos + _swap32(d * sin_signed)


def _half_sums(t):
    low = _low_half()
    return jnp.where(low, _rowsum(jnp.where(low, t, 0.0)), _rowsum(jnp.where(low, 0.0, t)))


def _dup_halves(pair):
    low = _low_half()
    swapped = pltpu.roll(pair, HALF, 1)
    return jnp.where(low, pair, swapped), jnp.where(low, swapped, pair)


def _undup_halves(d0, d1):
    return jnp.where(_low_half(), d0 + pltpu.roll(d0, HALF, 1), d1 + pltpu.roll(d1, HALF, 1))


def _pick_halves(a, b):
    return jnp.where(_low_half(), a, b)


def _as_row(col):
    return jnp.transpose(jnp.broadcast_to(col, (col.shape[0], LANES)))[0:1, :]


def _norm_bwd(dn_list, xh_list, r, n):
    c = sum(_rowsum(dn * xh) for dn, xh in zip(dn_list, xh_list)) * (1.0 / n)
    return [r * (dn - xh * c) for dn, xh in zip(dn_list, xh_list)]


def _cparams(semantics):
    return pltpu.CompilerParams(dimension_semantics=semantics, vmem_limit_bytes=VMEM_LIMIT)


def _const(shape):
    nd = len(shape)
    return pl.BlockSpec(shape, lambda *_: (0,) * nd, pipeline_mode=pl.Buffered(1))


def _acc(shape):
    nd = len(shape)
    return pl.BlockSpec(shape, lambda *_: (0,) * nd)


def _rows(tb, width):
    return pl.BlockSpec((tb, width), lambda i: (i, 0))


def _heads_rows(h, tb, width):
    return pl.BlockSpec((h, tb, width), lambda i: (0, i, 0))


def _sds(shape, dtype):
    return jax.ShapeDtypeStruct(shape, dtype)


def _position():
    return lax.axis_index("x"), lax.axis_index("y"), lax.axis_index("c")


def _all_gather(xs):
    n = len(xs)

    def body(*refs):
        x_refs, out_refs, (send_sems, recv_sems, local_sems) = refs[:n], refs[n:2 * n], refs[2 * n:]
        x, y, c = _position()
        me, sibling = (x, y, c), (x, y, 1 - c)
        chips = [(1 - x, y), (x, 1 - y), (1 - x, 1 - y)]

        def rows(i, px, py, pc):
            return out_refs[i].at[4 * px + 2 * py + pc]

        def copy(i, k, block, to, from_input=False):
            return _remote(x_refs[i] if from_input else rows(i, *block), rows(i, *block),
                           send_sems.at[k * n + i], recv_sems.at[k * n + i], to)

        every = range(n)
        mine = [pltpu.make_async_copy(x_refs[i], rows(i, *me), local_sems.at[i]) for i in every]
        first = [copy(i, 0, me, sibling, True) for i in every]
        first += [copy(i, 1 + j, me, (*chip, c), True) for j, chip in enumerate(chips) for i in every]
        for cp in mine + first:
            cp.start()
        passed = [[copy(i, 4 + j, (*chip, c), sibling) for i in every] for j, chip in enumerate(chips)]
        for j, chip in enumerate(chips):
            for i in every:
                copy(i, 1 + j, (*chip, c), me).wait_recv()
                passed[j][i].start()
        for i in every:
            copy(i, 0, sibling, me).wait_recv()
        for j, chip in enumerate(chips):
            for i in every:
                copy(i, 4 + j, (*chip, 1 - c), me).wait_recv()
        for cp in first + [cp for group in passed for cp in group]:
            cp.wait_send()
        for cp in mine:
            cp.wait()

    hbm = pl.BlockSpec(memory_space=pl.ANY)
    dma = pltpu.SemaphoreType.DMA
    return pl.pallas_call(
        body, name="ag_weights",
        out_shape=[_sds((N_DEV,) + a.shape, a.dtype) for a in xs],
        in_specs=[hbm] * n, out_specs=[hbm] * n,
        scratch_shapes=[dma((7 * n,)), dma((7 * n,)), dma((n,))],
    )(*xs)


def _relations(x, y):
    return [(x, y), (1 - x, y), (x, 1 - y), (1 - x, 1 - y)]


def _remote(src, dst, send_sem, recv_sem, device):
    return pltpu.make_async_remote_copy(src_ref=src, dst_ref=dst, send_sem=send_sem, recv_sem=recv_sem,
                                        device_id=device, device_id_type=MESH)


def _sibling_copies(g_refs, out_refs, send, recv):
    x, y, c = _position()
    n = len(g_refs)
    return [_remote(g.at[4 * px + 2 * py + (1 - c)], o.at[k], send.at[k * n + i], recv.at[k * n + i], (x, y, 1 - c))
            for k, (px, py) in enumerate(_relations(x, y)) for i, (g, o) in enumerate(zip(g_refs, out_refs))]


def _chip_copies(w_refs, out_refs, send, recv):
    x, y, c = _position()
    n = len(w_refs)
    return [_remote(w.at[k + 1], o.at[k], send.at[k * n + i], recv.at[k * n + i], (px, py, c))
            for k, (px, py) in enumerate(_relations(x, y)[1:]) for i, (w, o) in enumerate(zip(w_refs, out_refs))]


def _exchange(arrays, lead, relations, copies_fn, name):
    n = len(arrays)

    def body(*refs):
        copies = copies_fn(refs[:n], refs[n:2 * n], refs[2 * n], refs[2 * n + 1])
        for cp in copies:
            cp.start()
        for cp in copies:
            cp.wait()

    hbm = pl.BlockSpec(memory_space=pl.ANY)
    dma = pltpu.SemaphoreType.DMA
    return pl.pallas_call(
        body, name=name, out_shape=[_sds((lead,) + a.shape[1:], a.dtype) for a in arrays],
        in_specs=[hbm] * n, out_specs=[hbm] * n,
        scratch_shapes=[dma((relations * n,)), dma((relations * n,))],
    )(*arrays)


def _rs_sibling_exchange(gs):
    return _exchange(gs, 4, 4, _sibling_copies, "rs_sibling_exchange")


def _rs_chip_exchange(wires):
    return _exchange(wires, 3, 3, _chip_copies, "rs_chip_exchange")


def _side_exchange(arrays, lead, relations, copies_fn):
    shapes = [_sds((lead,) + a.shape[1:], a.dtype) for a in arrays]
    return list(arrays), shapes, relations * len(arrays), lambda ins, outs, send, recv, local: copies_fn(ins, outs, send, recv)


def _side_sibling(gs):
    return _side_exchange(gs, 4, 4, _sibling_copies)


def _side_chips(wires):
    return _side_exchange(wires, 3, 3, _chip_copies)


def _rs_chip_sums(gs, sibs, dest_idx):
    n = len(gs)

    def body(idx_ref, *refs):
        g_refs, s_refs, own_refs, wire_refs = refs[:n], refs[n:2 * n], refs[2 * n:3 * n], refs[3 * n:]
        totals = [g[0] + s[0] for g, s in zip(g_refs, s_refs)]
        for total, wire in zip(totals, wire_refs):
            wire[0] = total.astype(wire.dtype)

        @pl.when(pl.program_id(1) == 0)
        def _():
            for total, own in zip(totals, own_refs):
                own[...] = total

    def blocks(a, index_map, squeeze):
        rb = a.shape[1] // RS_ROW_BLOCKS
        return pl.BlockSpec((rb, a.shape[2]) if squeeze else (1, rb, a.shape[2]), index_map)

    return pl.pallas_call(
        body, name="rs_chip_sums",
        grid_spec=pltpu.PrefetchScalarGridSpec(
            num_scalar_prefetch=1, grid=(RS_ROW_BLOCKS, 4),
            in_specs=[blocks(g, lambda r, k, idx: (idx[k], r, 0), False) for g in gs]
            + [blocks(g, lambda r, k, idx: (k, r, 0), False) for g in gs],
            out_specs=[blocks(g, lambda r, k, idx: (r, 0), True) for g in gs]
            + [blocks(g, lambda r, k, idx: (k, r, 0), False) for g in gs]),
        out_shape=[_sds(g.shape[1:], jnp.float32) for g in gs] + [_sds((4,) + g.shape[1:], _MXU) for g in gs],
        compiler_params=_cparams(("parallel", "arbitrary")),
    )(dest_idx, *gs, *sibs)


def _side_gather(xs):
    n = len(xs)

    def make(ins, outs, send, recv, local):
        x, y, c = _position()
        me = 4 * x + 2 * y + c
        copies = [pltpu.make_async_copy(x_ref, out_ref.at[me], local.at[i])
                  for i, (x_ref, out_ref) in enumerate(zip(ins, outs))]
        for k in range(1, N_DEV):
            peer = (1 - x if k & 4 else x, 1 - y if k & 2 else y, 1 - c if k & 1 else c)
            copies += [_remote(x_ref, out_ref.at[me], send.at[(k - 1) * n + i], recv.at[(k - 1) * n + i], peer)
                       for i, (x_ref, out_ref) in enumerate(zip(ins, outs))]
        return copies

    return list(xs), [_sds((N_DEV,) + a.shape, a.dtype) for a in xs], (N_DEV - 1) * n, make


def _call(body, args, sides, *, name, grid, in_specs, out_specs, out_shape, scratch_shapes=(), semantics):
    in_specs, out_specs, out_shape = list(in_specs), list(out_specs), list(out_shape)
    n_in, n_out, n_scr = len(in_specs), len(out_specs), len(scratch_shapes)
    sides = list(sides or [])
    if not sides:
        outs = pl.pallas_call(body, name=name, grid=grid, in_specs=in_specs, out_specs=out_specs, out_shape=out_shape,
                              scratch_shapes=list(scratch_shapes), compiler_params=_cparams(semantics))(*args)
        return list(outs), []
    arrays = [a for side in sides for a in side[0]]
    shapes = [s for side in sides for s in side[1]]
    n_side_in, n_side_out = len(arrays), len(shapes)
    hbm = pl.BlockSpec(memory_space=pl.ANY)

    def with_copies(*refs):
        main_in, refs = refs[:n_in], refs[n_in:]
        side_in, refs = refs[:n_side_in], refs[n_side_in:]
        main_out, refs = refs[:n_out], refs[n_out:]
        side_out, refs = refs[:n_side_out], refs[n_side_out:]
        main_scr, sems = refs[:n_scr], refs[n_scr:]
        copies = []
        for k, (side_arrays, side_shapes, _, make) in enumerate(sides):
            copies += make(side_in[:len(side_arrays)], side_out[:len(side_shapes)], *sems[3 * k:3 * k + 3])
            side_in, side_out = side_in[len(side_arrays):], side_out[len(side_shapes):]
        ids = [pl.program_id(a) for a in range(len(grid))]
        first, last = ids[0] == 0, ids[0] == grid[0] - 1
        for i, size in zip(ids[1:], grid[1:]):
            first, last = jnp.logical_and(first, i == 0), jnp.logical_and(last, i == size - 1)

        @pl.when(first)
        def _():
            for cp in copies:
                cp.start()

        body(*main_in, *main_out, *main_scr)

        @pl.when(last)
        def _():
            for cp in copies:
                cp.wait()

    dma = pltpu.SemaphoreType.DMA
    sem_shapes = [dma((n,)) for side in sides for n in (side[2], side[2], len(side[0]))]
    outs = pl.pallas_call(
        with_copies, name=name, grid=grid, in_specs=in_specs + [hbm] * n_side_in,
        out_specs=out_specs + [hbm] * n_side_out, out_shape=out_shape + shapes,
        scratch_shapes=list(scratch_shapes) + sem_shapes,
        compiler_params=_cparams(("arbitrary",) * len(grid)),
    )(*args, *arrays)
    side_outs, rest = [], list(outs[n_out:])
    for side in sides:
        side_outs.append(rest[:len(side[1])])
        rest = rest[len(side[1]):]
    return list(outs[:n_out]), side_outs


def _all_reduce_small(v):
    rows_n = v.shape[0]

    def body(v_ref, out_ref, buf, send_sems, recv_sems):
        x, y, c = _position()
        me = 4 * x + 2 * y + c
        buf[me] = v_ref[...]
        copies = []
        for k in range(1, N_DEV):
            px = 1 - x if k & 4 else x
            py = 1 - y if k & 2 else y
            pc = 1 - c if k & 1 else c
            copies.append(pltpu.make_async_remote_copy(
                src_ref=v_ref, dst_ref=buf.at[me],
                send_sem=send_sems.at[k - 1], recv_sem=recv_sems.at[k - 1], device_id=(px, py, pc), device_id_type=MESH))
        for cp in copies:
            cp.start()
        for cp in copies:
            cp.wait()
        total = buf[0]
        for d in range(1, N_DEV):
            total = total + buf[d]
        out_ref[...] = total

    return pl.pallas_call(
        body, name="ar_small",
        out_shape=_sds((rows_n, LANES), jnp.float32),
        in_specs=[pl.BlockSpec(memory_space=pltpu.VMEM)],
        out_specs=pl.BlockSpec(memory_space=pltpu.VMEM),
        scratch_shapes=[pltpu.VMEM((N_DEV, rows_n, LANES), jnp.float32),
                        pltpu.SemaphoreType.DMA((N_DEV - 1,)), pltpu.SemaphoreType.DMA((N_DEV - 1,))],
    )(v)


def _adamw_math(w, g, m, v):
    m = ADAM_B1 * m + (1.0 - ADAM_B1) * g
    v = ADAM_B2 * v + (1.0 - ADAM_B2) * (g * g)
    m_hat = m / (1.0 - ADAM_B1 ** ADAM_STEP)
    v_hat = v / (1.0 - ADAM_B2 ** ADAM_STEP)
    delta = -ADAM_LR * (m_hat / (jnp.sqrt(v_hat) + ADAM_EPS) + ADAM_WD * w)
    return delta, m, v


def _rs_sum(owns, recvs):
    n = len(owns)

    def body(*refs):
        own_refs, recv_refs, out_refs = refs[:n], refs[n:4 * n], refs[4 * n:]
        for i in range(n):
            r0, r1, r2 = recv_refs[3 * i:3 * i + 3]
            out_refs[i][...] = ((own_refs[i][...] + _f32(r0[0])) + _f32(r1[0])) + _f32(r2[0])

    def row(a):
        return pl.BlockSpec((a.shape[0] // RS_ROW_BLOCKS, a.shape[1]), lambda r: (r, 0))

    def slot(a, k):
        return pl.BlockSpec((1, a.shape[0] // RS_ROW_BLOCKS, a.shape[1]), lambda r: (k, r, 0))

    return pl.pallas_call(
        body, name="rs_sum", grid=(RS_ROW_BLOCKS,),
        in_specs=[row(a) for a in owns] + [slot(a, k) for a in owns for k in range(3)],
        out_specs=[row(a) for a in owns],
        out_shape=[_sds(a.shape, jnp.float32) for a in owns],
        compiler_params=_cparams(("parallel",)),
    )(*owns, *[r for r in recvs for _ in range(3)])


def _adamw(g, w, m, v, name):
    depth, a, b = w.shape

    def body(g_ref, w_ref, m_ref, v_ref, d_out, m_out, v_out):
        delta, m2, v2 = _adamw_math(w_ref[...], g_ref[...], m_ref[...], v_ref[...])
        d_out[...] = delta
        m_out[...] = m2
        v_out[...] = v2

    layer = pl.BlockSpec((1, a, b), lambda l: (l, 0, 0))
    return pl.pallas_call(
        body, name="adamw_" + name, grid=(depth,),
        in_specs=[layer] * 4, out_specs=[layer] * 3,
        out_shape=[_sds(w.shape, jnp.float32)] * 3,
        compiler_params=_cparams(("parallel",)),
    )(g, w, m, v)


def _adamw_small(gs, ws, ms, vs):
    n = len(gs)

    def body(*refs):
        g_refs, w_refs, m_refs, v_refs, outs = refs[:n], refs[n:2 * n], refs[2 * n:3 * n], refs[3 * n:4 * n], refs[4 * n:]
        for i in range(n):
            delta, m2, v2 = _adamw_math(w_refs[i][...], g_refs[i][...], m_refs[i][...], v_refs[i][...])
            outs[i][...] = delta
            outs[n + i][...] = m2
            outs[2 * n + i][...] = v2

    vm = pl.BlockSpec(memory_space=pltpu.VMEM)
    return pl.pallas_call(
        body, name="adamw_small",
        in_specs=[vm] * (4 * n), out_specs=[vm] * (3 * n),
        out_shape=[_sds(a.shape, jnp.float32) for a in ws] * 3,
    )(*gs, *ws, *ms, *vs)


def _f_chunk(f):
    for cand in (1408, 1024, 512, 256, 128):
        if f % cand == 0:
            return cand
    return f


def _ffn_fwd(x, gain, wg, wu, wd, sides=(), target=None):
    t, d = x.shape
    f = wg.shape[0]
    tb = min(_TB, t)
    fc = _f_chunk(f)
    with_loss = target is not None

    def body(*refs):
        x_ref, g_ref, wg_ref, wu_ref, wd_ref = refs[:5]
        h_ref, s_ref, fa_ref, fu_ref = refs[-4:]
        xv = x_ref[...]
        hb = _mx(xv * _rsq(_sumsq(xv), d) * g_ref[...])
        h_ref[...] = hb
        y = jnp.zeros((tb, d), jnp.float32)
        for c0 in range(0, f, fc):
            a = _dot_nt(hb, wg_ref[c0:c0 + fc, :])
            u = _dot_nt(hb, wu_ref[c0:c0 + fc, :])
            sig = jax.nn.sigmoid(a)
            silu = a * sig
            s = _mx(silu * u)
            s_ref[:, c0:c0 + fc] = s
            fa_ref[:, c0:c0 + fc] = _mx(u * (sig * (1.0 + a * (1.0 - sig))))
            fu_ref[:, c0:c0 + fc] = _mx(silu)
            y = y + _dot(s, wd_ref[c0:c0 + fc, :])
        out = xv + 0.5 * y
        if not with_loss:
            refs[5][...] = out
            return
        t_ref, dy_ref, acc_ref = refs[5:8]

        @pl.when(pl.program_id(0) == 0)
        def _():
            acc_ref[...] = jnp.zeros_like(acc_ref)

        err = out - t_ref[...]
        dy_ref[...] = err * (1.0 / d)
        acc_ref[...] += jnp.sum(err * err)

    kept_specs = [_rows(tb, d), _rows(tb, f), _rows(tb, f), _rows(tb, f)]
    kept_shapes = [_sds((t, d), _MXU), _sds((t, f), _MXU), _sds((t, f), _MXU), _sds((t, f), _MXU)]
    weights = [_const((1, d)), _const((f, d)), _const((f, d)), _const((f, d))]
    if with_loss:
        return _call(
            body, (x, gain, wg, wu, wd, target), sides, name="ffn_fwd_loss", grid=(t // tb,),
            in_specs=[_rows(tb, d)] + weights + [_rows(tb, d)],
            out_specs=[_rows(tb, d), _acc((8, LANES))] + kept_specs,
            out_shape=[_sds((t, d), jnp.float32), _sds((8, LANES), jnp.float32)] + kept_shapes,
            semantics=("arbitrary",))
    return _call(
        body, (x, gain, wg, wu, wd), sides, name="ffn_fwd", grid=(t // tb,),
        in_specs=[_rows(tb, d)] + weights, out_specs=[_rows(tb, d)] + kept_specs,
        out_shape=[_sds((t, d), jnp.float32)] + kept_shapes,
        semantics=("parallel",))


def _ffn_dgrad(x, gain, dxo, fa, fu, wg, wu, wd, sides=()):
    t, d = x.shape
    f = wg.shape[0]
    tb = min(_TB, t)
    fc = _f_chunk(f)

    def body(x_ref, g_ref, dxo_ref, fa_ref, fu_ref, wg_ref, wu_ref, wd_ref, dxi_ref, da_ref, du_ref, dy_ref, dg_ref):
        xv = x_ref[...]
        gv = g_ref[...]
        r = _rsq(_sumsq(xv), d)
        xhat = xv * r
        dxo = dxo_ref[...]
        dyb = _mx(0.5 * dxo)
        dy_ref[...] = dyb
        dh = jnp.zeros((tb, d), jnp.float32)
        for c0 in range(0, f, fc):
            ds = _dot_nt(dyb, wd_ref[c0:c0 + fc, :])
            da = _mx(ds * _f32(fa_ref[:, c0:c0 + fc]))
            du = _mx(ds * _f32(fu_ref[:, c0:c0 + fc]))
            da_ref[:, c0:c0 + fc] = da
            du_ref[:, c0:c0 + fc] = du
            dh = dh + _dot(da, wg_ref[c0:c0 + fc, :]) + _dot(du, wu_ref[c0:c0 + fc, :])

        @pl.when(pl.program_id(0) == 0)
        def _():
            dg_ref[...] = jnp.zeros_like(dg_ref)

        dg_ref[...] += _colsum(dh * xhat)
        dn = dh * gv
        dxi_ref[...] = dxo + r * (dn - xhat * (_rowsum(dn * xhat) * (1.0 / d)))

    return _call(
        body, (x, gain, dxo, fa, fu, wg, wu, wd), sides, name="ffn_dgrad", grid=(t // tb,),
        in_specs=[_rows(tb, d), _const((1, d)), _rows(tb, d), _rows(tb, f), _rows(tb, f),
                  _const((f, d)), _const((f, d)), _const((f, d))],
        out_specs=[_rows(tb, d), _rows(tb, f), _rows(tb, f), _rows(tb, d), _acc((1, d))],
        out_shape=[_sds((t, d), jnp.float32), _sds((t, f), _MXU), _sds((t, f), _MXU), _sds((t, d), _MXU),
                   _sds((1, d), jnp.float32)],
        semantics=("arbitrary",))


def _tn_matmul(a, b, name, sides=()):
    t, m = a.shape
    n = b.shape[1]
    tk = min(_TK, t)
    tn = n
    while m * tn * 4 > 12 * 1024 * 1024 and tn % 256 == 0:
        tn //= 2

    def body(a_ref, b_ref, o_ref):
        @pl.when(pl.program_id(1) == 0)
        def _():
            o_ref[...] = jnp.zeros_like(o_ref)

        o_ref[...] += _dot_tn(a_ref[...], b_ref[...])

    (out,), side_outs = _call(
        body, (a, b), sides, name=name, grid=(n // tn, t // tk),
        in_specs=[pl.BlockSpec((tk, m), lambda j, k: (k, 0)), pl.BlockSpec((tk, tn), lambda j, k: (k, j))],
        out_specs=[pl.BlockSpec((m, tn), lambda j, k: (0, j))],
        out_shape=[_sds((m, n), jnp.float32)],
        semantics=("parallel", "arbitrary"))
    return out, side_outs


PREP_WEIGHTS = ("mix_g", "w_in", "g_qa", "wqb", "g_kva", "w_kvb", "gq_n", "gq_r", "gk_n", "gk_r", "g_sq", "g_sk")
C_CQ, C_CKV, C_KPE, C_QS = 0, MLA_Q_RANK, MLA_Q_RANK + MLA_KV_RANK, MLA_Q_RANK + MLA_KV_RANK + LANES
C_KS = C_QS + SWA_HEADS * SWA_D
C_VS = C_KS + LANES
W_IN_PACKED = C_VS + LANES


def _prep_specs(p):
    return [_const(p[n].shape) for n in PREP_WEIGHTS]


def _pair_norm_rope(t, gain, cos, sin_s):
    return _rope(t * _rsq(_half_sums(t * t), SWA_D) * gain, cos, sin_s)


def _prep_fwd(x, cos, sin_s, p):
    t, d = x.shape
    tb = min(_TB_MIX, t)

    def body(x_ref, cos_ref, sin_ref, mix_g, w_in, g_qa, wqb, g_kva, w_kvb, gq_n, gq_r, gk_n, gk_r, g_sq, g_sk,
             qa_ref, ka_ref, va_ref, qb_ref, kb_ref, vb_ref):
        xv = x_ref[...]
        cos_v, sin_v = cos_ref[...], sin_ref[...]
        hb = _mx(xv * _rsq(_sumsq(xv), d) * mix_g[...])
        proj = _dot_nt(hb, w_in[...])
        cq = proj[:, C_CQ:C_CKV]
        cqn = _mx(cq * _rsq(_sumsq(cq), MLA_Q_RANK) * g_qa[...])
        for h in range(MLA_HEADS):
            qh = _dot_nt(cqn, wqb[h])
            qn, qr = qh[:, :MLA_NOPE], qh[:, MLA_NOPE:]
            rh = _rsq(_sumsq(qn) + _sumsq(qr), MLA_QK)
            qa_ref[h, :, 0:MLA_NOPE] = (qn * rh * gq_n[...]).astype(qa_ref.dtype)
            qa_ref[h, :, MLA_NOPE:MLA_QK_PAD] = _rope(qr * rh * gq_r[...], cos_v, sin_v).astype(qa_ref.dtype)
        ckv = proj[:, C_CKV:C_KPE]
        ckvn = _mx(ckv * _rsq(_sumsq(ckv), MLA_KV_RANK) * g_kva[...])
        kpe = proj[:, C_KPE:C_QS]
        ss_pe = _sumsq(kpe)
        kv = _dot_nt(ckvn, w_kvb[...])
        for h in range(MLA_HEADS):
            c0 = h * (MLA_NOPE + MLA_V)
            kn = kv[:, c0:c0 + MLA_NOPE]
            rh = _rsq(_sumsq(kn) + ss_pe, MLA_QK)
            ka_ref[h, :, 0:MLA_NOPE] = (kn * rh * gk_n[...]).astype(ka_ref.dtype)
            ka_ref[h, :, MLA_NOPE:MLA_QK_PAD] = _rope(kpe * rh * gk_r[...], cos_v, sin_v).astype(ka_ref.dtype)
            va_ref[h] = kv[:, c0 + MLA_NOPE:c0 + MLA_NOPE + MLA_V].astype(va_ref.dtype)
        for j in range(SWA_PAIRS):
            c0 = C_QS + j * LANES
            qb_ref[j] = _pair_norm_rope(proj[:, c0:c0 + LANES], g_sq[...], cos_v, sin_v).astype(qb_ref.dtype)
        k0, k1 = _dup_halves(_pair_norm_rope(proj[:, C_KS:C_VS], g_sk[...], cos_v, sin_v))
        kb_ref[0] = k0.astype(kb_ref.dtype)
        kb_ref[1] = k1.astype(kb_ref.dtype)
        v0, v1 = _dup_halves(proj[:, C_VS:W_IN_PACKED])
        vb_ref[0] = v0.astype(vb_ref.dtype)
        vb_ref[1] = v1.astype(vb_ref.dtype)

    return pl.pallas_call(
        body, name="prep_fwd", grid=(t // tb,),
        in_specs=[_rows(tb, d), _rows(tb, LANES), _rows(tb, LANES)] + _prep_specs(p),
        out_specs=[_heads_rows(MLA_HEADS, tb, MLA_QK_PAD), _heads_rows(MLA_HEADS, tb, MLA_QK_PAD),
                   _heads_rows(MLA_HEADS, tb, MLA_V), _heads_rows(SWA_PAIRS, tb, LANES),
                   _heads_rows(SWA_KV, tb, LANES), _heads_rows(SWA_KV, tb, LANES)],
        out_shape=[_sds((MLA_HEADS, t, MLA_QK_PAD), _MXU), _sds((MLA_HEADS, t, MLA_QK_PAD), _MXU),
                   _sds((MLA_HEADS, t, MLA_V), _MXU), _sds((SWA_PAIRS, t, LANES), _MXU),
                   _sds((SWA_KV, t, LANES), _MXU), _sds((SWA_KV, t, LANES), _MXU)],
        compiler_params=_cparams(("parallel",)),
    )(x, cos, sin_s, *[p[n] for n in PREP_WEIGHTS])


def _prep_bwd(x, dxin, cos, sin_s, p, dqa, dka, dva, dqb, dkb, dvb, sides=()):
    t, d = x.shape
    tb = min(_TB_MIX, t)
    n_w = len(PREP_WEIGHTS)

    def body(*refs):
        x_ref, dxin_ref, cos_ref, sin_ref = refs[:4]
        mix_g, w_in, g_qa, wqb, g_kva, w_kvb, gq_n, gq_r, gk_n, gk_r, g_sq, g_sk = refs[4:4 + n_w]
        dqa_ref, dka_ref, dva_ref, dqb_ref, dkb_ref, dvb_ref = refs[4 + n_w:10 + n_w]
        dx_ref = refs[10 + n_w]
        grads = dict(zip(PREP_WEIGHTS, refs[11 + n_w:11 + 2 * n_w]))
        dproj_ref, dkv_ref, dqh_ref = refs[11 + 2 * n_w:]

        @pl.when(pl.program_id(0) == 0)
        def _():
            for ref in grads.values():
                ref[...] = jnp.zeros_like(ref)

        xv = x_ref[...]
        cos_v, sin_v = cos_ref[...], sin_ref[...]
        r0 = _rsq(_sumsq(xv), d)
        xhat = xv * r0
        hb = _mx(xhat * mix_g[...])
        proj = _dot_nt(hb, w_in[...])

        cq = proj[:, C_CQ:C_CKV]
        rq = _rsq(_sumsq(cq), MLA_Q_RANK)
        cqh = cq * rq
        cqn = _mx(cqh * g_qa[...])
        dcqn = jnp.zeros((tb, MLA_Q_RANK), jnp.float32)
        for h in range(MLA_HEADS):
            qh = _dot_nt(cqn, wqb[h])
            qn, qr = qh[:, :MLA_NOPE], qh[:, MLA_NOPE:]
            rh = _rsq(_sumsq(qn) + _sumsq(qr), MLA_QK)
            xh_n, xh_r = qn * rh, qr * rh
            dy_n = dqa_ref[h, :, 0:MLA_NOPE]
            dy_r = _rope_bwd(dqa_ref[h, :, MLA_NOPE:MLA_QK_PAD], cos_v, sin_v)
            grads["gq_n"][...] += _colsum(dy_n * xh_n)
            grads["gq_r"][...] += _colsum(dy_r * xh_r)
            dqn, dqr = _norm_bwd([dy_n * gq_n[...], dy_r * gq_r[...]], [xh_n, xh_r], rh, MLA_QK)
            dqh_ref[:, 0:MLA_NOPE] = _mx(dqn)
            dqh_ref[:, MLA_NOPE:MLA_QK_PAD] = _mx(dqr)
            dqh = dqh_ref[...]
            grads["wqb"][h] += _dot_tn(dqh, cqn)
            dcqn = dcqn + _dot(dqh, wqb[h])
        grads["g_qa"][...] += _colsum(dcqn * cqh)
        (dcq,) = _norm_bwd([dcqn * g_qa[...]], [cqh], rq, MLA_Q_RANK)
        dproj_ref[:, C_CQ:C_CKV] = _mx(dcq)

        ckv = proj[:, C_CKV:C_KPE]
        rkv = _rsq(_sumsq(ckv), MLA_KV_RANK)
        ckvh = ckv * rkv
        ckvn = _mx(ckvh * g_kva[...])
        kpe = proj[:, C_KPE:C_QS]
        ss_pe = _sumsq(kpe)
        kv = _dot_nt(ckvn, w_kvb[...])
        dkpe = jnp.zeros((tb, LANES), jnp.float32)
        for h in range(MLA_HEADS):
            c0 = h * (MLA_NOPE + MLA_V)
            c1 = c0 + MLA_NOPE
            kn = kv[:, c0:c1]
            rh = _rsq(_sumsq(kn) + ss_pe, MLA_QK)
            xh_n, xh_r = kn * rh, kpe * rh
            dy_n = dka_ref[h, :, 0:MLA_NOPE]
            dy_r = _rope_bwd(dka_ref[h, :, MLA_NOPE:MLA_QK_PAD], cos_v, sin_v)
            grads["gk_n"][...] += _colsum(dy_n * xh_n)
            grads["gk_r"][...] += _colsum(dy_r * xh_r)
            dkn, dkr = _norm_bwd([dy_n * gk_n[...], dy_r * gk_r[...]], [xh_n, xh_r], rh, MLA_QK)
            dkpe = dkpe + dkr
            dkv_ref[:, c0:c1] = _mx(dkn)
            dkv_ref[:, c1:c1 + MLA_V] = _mx(dva_ref[h])
        dkv = dkv_ref[...]
        grads["w_kvb"][...] += _dot_tn(dkv, ckvn)
        dckvn = _dot(dkv, w_kvb[...])
        grads["g_kva"][...] += _colsum(dckvn * ckvh)
        (dckv,) = _norm_bwd([dckvn * g_kva[...]], [ckvh], rkv, MLA_KV_RANK)
        dproj_ref[:, C_CKV:C_KPE] = _mx(dckv)
        dproj_ref[:, C_KPE:C_QS] = _mx(dkpe)

        def pair_bwd(tv, dy, g_ref, gname):
            r = _rsq(_half_sums(tv * tv), SWA_D)
            xh = tv * r
            dpre = _rope_bwd(dy, cos_v, sin_v)
            grads[gname][...] += _colsum(dpre * xh)
            dn = dpre * g_ref[...]
            return r * (dn - xh * (_half_sums(dn * xh) * (1.0 / SWA_D)))

        for j in range(SWA_PAIRS):
            c0 = C_QS + j * LANES
            dproj_ref[:, c0:c0 + LANES] = _mx(pair_bwd(proj[:, c0:c0 + LANES], dqb_ref[j], g_sq, "g_sq"))
        dproj_ref[:, C_KS:C_VS] = _mx(pair_bwd(proj[:, C_KS:C_VS], _undup_halves(dkb_ref[0], dkb_ref[1]), g_sk, "g_sk"))
        dproj_ref[:, C_VS:W_IN_PACKED] = _mx(_undup_halves(dvb_ref[0], dvb_ref[1]))

        dproj = dproj_ref[...]
        grads["w_in"][...] += _dot_tn(dproj, hb)
        dh = _dot(dproj, w_in[...])
        grads["mix_g"][...] += _colsum(dh * xhat)
        (dxv,) = _norm_bwd([dh * mix_g[...]], [xhat], r0, d)
        dx_ref[...] = dxin_ref[...] + dxv

    grad_shapes = [p[n].shape for n in PREP_WEIGHTS]
    args = (x, dxin, cos, sin_s, *[p[n] for n in PREP_WEIGHTS], dqa, dka, dva, dqb, dkb, dvb)
    return _call(
        body, args, sides, name="prep_bwd", grid=(t // tb,),
        in_specs=[_rows(tb, d), _rows(tb, d), _rows(tb, LANES), _rows(tb, LANES)] + _prep_specs(p) + [
            _heads_rows(MLA_HEADS, tb, MLA_QK_PAD), _heads_rows(MLA_HEADS, tb, MLA_QK_PAD),
            _heads_rows(MLA_HEADS, tb, MLA_V), _heads_rows(SWA_PAIRS, tb, LANES),
            _heads_rows(SWA_KV, tb, LANES), _heads_rows(SWA_KV, tb, LANES)],
        out_specs=[_rows(tb, d)] + [_acc(s) for s in grad_shapes],
        out_shape=[_sds((t, d), jnp.float32)] + [_sds(s, jnp.float32) for s in grad_shapes],
        scratch_shapes=[pltpu.VMEM((tb, W_IN_PACKED), _MXU), pltpu.VMEM((tb, MLA_HEADS * (MLA_NOPE + MLA_V)), _MXU),
                        pltpu.VMEM((tb, MLA_QK_PAD), _MXU)],
        semantics=("arbitrary",))


def _strips(n):
    step = min(_STRIP, n)
    return [slice(r, r + step) for r in range(0, n, step)]


def _mla_fwd(q, k, v, sides=()):
    hn, t, dq = q.shape
    dv = v.shape[2]
    bq = min(_BQ, t)
    scale = MLA_QK ** -0.5
    scale2 = scale * LOG2_E

    def body(q_ref, k_ref, v_ref, o_ref, l_ref):
        i = pl.program_id(1)
        qv = q_ref[0]

        def step(first_block, width, carry, masked):
            m, l, acc = carry
            start = pl.multiple_of(first_block * bq, bq)
            s = _dot_nt(qv, k_ref[0, pl.ds(start, width), :])
            if masked:
                row = lax.broadcasted_iota(jnp.int32, (bq, width), 0)
                col = lax.broadcasted_iota(jnp.int32, (bq, width), 1)
                s = jnp.where(col <= row, s, NEG)
            m_new = jnp.maximum(m, _rowmax(s))
            alpha = jnp.exp2((m - m_new) * scale2)
            pv = jnp.exp2((s - m_new) * scale2)
            l = alpha * l + _rowsum(pv)
            acc = alpha * acc + _dot(_mx(pv), v_ref[0, pl.ds(start, width), :])
            return m_new, l, acc

        init = (jnp.full((bq, 1), NEG, jnp.float32), jnp.zeros((bq, 1), jnp.float32), jnp.zeros((bq, dv), jnp.float32))
        carry, done = init, 0
        for group in (4, 2, 1):
            count = (i - done) // group
            carry = lax.fori_loop(0, count, lambda g, c, done=done, group=group: step(done + group * g, group * bq, c, False), carry)
            done = done + group * count
        m, l, acc = step(i, bq, carry, True)
        o_ref[0] = acc / l
        l_ref[0, 0] = _as_row(m * scale + jnp.log(l))

    return _call(
        body, (q, k, v), sides, name="mla_fwd", grid=(hn, t // bq),
        in_specs=[pl.BlockSpec((1, bq, dq), lambda h, i: (h, i, 0)),
                  pl.BlockSpec((1, t, dq), lambda h, i: (h, 0, 0)),
                  pl.BlockSpec((1, t, dv), lambda h, i: (h, 0, 0))],
        out_specs=[pl.BlockSpec((1, bq, dv), lambda h, i: (h, i, 0)),
                   pl.BlockSpec((1, 1, 1, bq), lambda h, i: (h, i, 0, 0))],
        out_shape=[_sds((hn, t, dv), jnp.float32), _sds((hn, t // bq, 1, bq), jnp.float32)],
        semantics=("parallel", "arbitrary"))


def _mla_bwd(q, k, v, do, lse_rows, dsum_rows, sides=()):
    hn, t, dq_w = q.shape
    dv_w = v.shape[2]
    bq = min(_BQ, t)
    nb = t // bq
    wide = max(_BWD_GROUPS) * bq
    scale = MLA_QK ** -0.5

    def body(q_ref, do_ref, l_ref, d_ref, k_ref, v_ref, dq_ref, dk_ref, dv_ref, st_scr, dpt_scr, p_scr, ds_scr):
        j = pl.program_id(1)

        @pl.when(j == 0)
        def _():
            dq_ref[...] = jnp.zeros_like(dq_ref)

        kv = k_ref[0]
        vv = v_ref[0]
        dk_ref[0] = jnp.zeros((bq, dq_w), jnp.float32)
        dv_ref[0] = jnp.zeros((bq, dv_w), jnp.float32)

        def tile(first_block, n_blk, masked):
            width = n_blk * bq
            start = pl.multiple_of(first_block * bq, bq)
            qv = q_ref[0, pl.ds(start, width), :]
            dov = do_ref[0, pl.ds(start, width), :]
            st_scr[:, :width] = _dot_nt(kv, qv)
            dpt_scr[:, :width] = _dot_nt(vv, dov)
            lse2 = jnp.concatenate([l_ref[0, first_block + b] for b in range(n_blk)], axis=1) * LOG2_E
            dsum = jnp.concatenate([d_ref[0, first_block + b] for b in range(n_blk)], axis=1)
            for rows in _strips(bq):
                pt = jnp.exp2(st_scr[rows, :width] * (scale * LOG2_E) - lse2)
                if masked:
                    n_rows = rows.stop - rows.start
                    row = lax.broadcasted_iota(jnp.int32, (n_rows, width), 0) + rows.start
                    col = lax.broadcasted_iota(jnp.int32, (n_rows, width), 1)
                    pt = jnp.where(row <= col, pt, 0.0)
                p_scr[rows, :width] = _mx(pt)
                ds_scr[rows, :width] = _mx(pt * (dpt_scr[rows, :width] - dsum) * scale)
            ds_t = ds_scr[:, :width]
            dv_ref[0] += _dot(p_scr[:, :width], dov)
            dk_ref[0] += _dot(ds_t, qv)
            dq_ref[0, pl.ds(start, width), :] += _dot_tn(ds_t, kv)

        def group_loop(first_block, n_blk, count):
            def loop_body(g, carry):
                tile(first_block + n_blk * g, n_blk, False)
                return carry

            lax.fori_loop(0, count, loop_body, 0)

        tile(j, 1, True)
        done = j + 1
        for n_blk in _BWD_GROUPS:
            count = (nb - done) // n_blk
            group_loop(done, n_blk, count)
            done = done + n_blk * count

    return _call(
        body, (q, do, lse_rows, dsum_rows, k, v), sides, name="mla_bwd", grid=(hn, nb),
        in_specs=[pl.BlockSpec((1, t, dq_w), lambda h, j: (h, 0, 0)),
                  pl.BlockSpec((1, t, dv_w), lambda h, j: (h, 0, 0)),
                  pl.BlockSpec((1, nb, 1, bq), lambda h, j: (h, 0, 0, 0)),
                  pl.BlockSpec((1, nb, 1, bq), lambda h, j: (h, 0, 0, 0)),
                  pl.BlockSpec((1, bq, dq_w), lambda h, j: (h, j, 0)),
                  pl.BlockSpec((1, bq, dv_w), lambda h, j: (h, j, 0))],
        out_specs=[pl.BlockSpec((1, t, dq_w), lambda h, j: (h, 0, 0)),
                   pl.BlockSpec((1, bq, dq_w), lambda h, j: (h, j, 0)),
                   pl.BlockSpec((1, bq, dv_w), lambda h, j: (h, j, 0))],
        out_shape=[_sds((hn, t, dq_w), jnp.float32), _sds((hn, t, dq_w), jnp.float32), _sds((hn, t, dv_w), jnp.float32)],
        scratch_shapes=[pltpu.VMEM((bq, wide), jnp.float32), pltpu.VMEM((bq, wide), jnp.float32),
                        pltpu.VMEM((bq, wide), _MXU), pltpu.VMEM((bq, wide), _MXU)],
        semantics=("parallel", "arbitrary"))


STACK = SWA_GROUP * SWA_BLOCK


def _swa_stack(ref, c, rows):
    low = _low_half()
    parts = []
    for g in range(SWA_GROUP):
        tv = ref[SWA_GROUP // 2 * c + g // 2, rows, :]
        keep = low if g % 2 == 0 else jnp.logical_not(low)
        parts.append(_mx(jnp.where(keep, tv, jnp.zeros_like(tv))))
    return jnp.concatenate(parts, axis=0)


def _swa_cols(ref, c, rows):
    return jnp.concatenate([ref[SWA_GROUP * c + g, rows, 0:1] for g in range(SWA_GROUP)], axis=0)


def _swa_sink_col(s_ref, c):
    return jnp.concatenate([jnp.broadcast_to(s_ref[SWA_GROUP * c + g][:, 0:1], (SWA_BLOCK, 1))
                            for g in range(SWA_GROUP)], axis=0)


def _swa_band_masks():
    row = lax.broadcasted_iota(jnp.int32, (STACK, SWA_BLOCK), 0) & (SWA_BLOCK - 1)
    col = lax.broadcasted_iota(jnp.int32, (STACK, SWA_BLOCK), 1)
    return col <= row, col > row


def _swa_band(has_previous):
    row = lax.broadcasted_iota(jnp.int32, (STACK, 2 * SWA_BLOCK), 0) & (SWA_BLOCK - 1)
    col = lax.broadcasted_iota(jnp.int32, (STACK, 2 * SWA_BLOCK), 1)
    before = jnp.logical_and(col < SWA_BLOCK, col > row)
    if has_previous is not True:
        before = jnp.logical_and(before, has_previous)
    return jnp.logical_or(before, jnp.logical_and(col >= SWA_BLOCK, col - SWA_BLOCK <= row))


def _swa_keys(ref, prev_ref, c, b):
    if b == 0:
        return jnp.concatenate([prev_ref[c], ref[c, 0:SWA_BLOCK, :]], axis=0)
    return ref[c, (b - 1) * SWA_BLOCK:(b + 1) * SWA_BLOCK, :]


def _swa_unstack_pairs(ref, c, rows, stacked):
    for pr in range(SWA_GROUP // 2):
        r0 = 2 * pr * SWA_BLOCK
        ref[SWA_GROUP // 2 * c + pr, rows, :] = _pick_halves(stacked[r0:r0 + SWA_BLOCK], stacked[r0 + SWA_BLOCK:r0 + 2 * SWA_BLOCK])


def _swa_blocks(t):
    nblk = t // SWA_BLOCK
    bps = min(_SWA_STEP, nblk)
    return nblk, bps, bps * SWA_BLOCK


def _swa_fwd(q, k, v, sinks):
    _, t, _ = q.shape
    nblk, bps, sb = _swa_blocks(t)
    scale = SWA_D ** -0.5

    def body(q_ref, k_ref, kp_ref, v_ref, vp_ref, s_ref, o_ref, l_ref):
        n = pl.program_id(0)
        band_first, band = _swa_band(n > 0), _swa_band(True)
        for c in range(SWA_KV):
            sink = _swa_sink_col(s_ref, c)
            for b in range(bps):
                rows = slice(b * SWA_BLOCK, (b + 1) * SWA_BLOCK)
                qs = _swa_stack(q_ref, c, rows)
                s = jnp.where(band_first if b == 0 else band, _dot_nt(qs, _swa_keys(k_ref, kp_ref, c, b)) * scale, NEG)
                m = jnp.maximum(_rowmax(s), sink)
                e = jnp.exp(s - m)
                denom = _rowsum(e) + jnp.exp(sink - m)
                o = _dot(_mx(e * (1.0 / denom)), _swa_keys(v_ref, vp_ref, c, b))
                lse = m + jnp.log(denom)
                for g in range(SWA_GROUP):
                    l_ref[SWA_GROUP * c + g, rows, :] = jnp.broadcast_to(
                        lse[g * SWA_BLOCK:(g + 1) * SWA_BLOCK], (SWA_BLOCK, LANES))
                _swa_unstack_pairs(o_ref, c, rows, o)

    main = lambda n: (0, n, 0)
    prev = lambda n: (0, jnp.maximum(n * bps - 1, 0), 0)
    return pl.pallas_call(
        body, name="swa_fwd", grid=(nblk // bps,),
        in_specs=[pl.BlockSpec((SWA_PAIRS, sb, LANES), main),
                  pl.BlockSpec((SWA_KV, sb, LANES), main), pl.BlockSpec((SWA_KV, SWA_BLOCK, LANES), prev),
                  pl.BlockSpec((SWA_KV, sb, LANES), main), pl.BlockSpec((SWA_KV, SWA_BLOCK, LANES), prev),
                  _const((SWA_HEADS, 1, LANES))],
        out_specs=[pl.BlockSpec((SWA_PAIRS, sb, LANES), main), pl.BlockSpec((SWA_HEADS, sb, LANES), main)],
        out_shape=[_sds((SWA_PAIRS, t, LANES), jnp.float32), _sds((SWA_HEADS, t, LANES), jnp.float32)],
        compiler_params=_cparams(("parallel",)),
    )(q, k, k, v, v, sinks)


def _swa_bwd(q, k, v, sinks, do, lse, dsum):
    _, t, _ = q.shape
    nblk, bps, sb = _swa_blocks(t)
    steps = nblk // bps
    scale = SWA_D ** -0.5

    def body(q_ref, k_ref, kp_ref, v_ref, vp_ref, s_ref, do_ref, l_ref, d_ref, qn_ref, don_ref, ln_ref, dn_ref,
             dq_ref, dk_ref, dv_ref, ds_ref):
        n = pl.program_id(0)

        @pl.when(n == 0)
        def _():
            ds_ref[...] = jnp.zeros_like(ds_ref)

        _, m_prev = _swa_band_masks()
        band_first, band = _swa_band(n > 0), _swa_band(True)
        everything = slice(0, SWA_BLOCK)

        def probs(qs, keys, mask, lcol):
            return jnp.where(mask, jnp.exp(_dot_nt(qs, keys) * scale - lcol), 0.0)

        def dscores(pm, dos, vals, dcol):
            return _mx(pm * (_dot_nt(dos, vals) - dcol) * scale)

        for c in range(SWA_KV):
            sink = _swa_sink_col(s_ref, c)
            dk_acc = [jnp.zeros((SWA_BLOCK, LANES), jnp.float32) for _ in range(bps)]
            dv_acc = [jnp.zeros((SWA_BLOCK, LANES), jnp.float32) for _ in range(bps)]
            for b in range(bps):
                rows = slice(b * SWA_BLOCK, (b + 1) * SWA_BLOCK)
                keys, vals = _swa_keys(k_ref, kp_ref, c, b), _swa_keys(v_ref, vp_ref, c, b)
                qs = _swa_stack(q_ref, c, rows)
                dos = _swa_stack(do_ref, c, rows)
                lcol = _swa_cols(l_ref, c, rows)
                dcol = _swa_cols(d_ref, c, rows)
                pm = probs(qs, keys, band_first if b == 0 else band, lcol)
                ds = dscores(pm, dos, vals, dcol)
                _swa_unstack_pairs(dq_ref, c, rows, _dot(ds, keys))
                dk_both = _dot_tn(ds, qs)
                dv_both = _dot_tn(_mx(pm), dos)
                dk_acc[b] = dk_acc[b] + dk_both[SWA_BLOCK:]
                dv_acc[b] = dv_acc[b] + dv_both[SWA_BLOCK:]
                if b > 0:
                    dk_acc[b - 1] = dk_acc[b - 1] + dk_both[:SWA_BLOCK]
                    dv_acc[b - 1] = dv_acc[b - 1] + dv_both[:SWA_BLOCK]
                p_sink = jnp.exp(sink - lcol) * dcol
                for g in range(SWA_GROUP):
                    ds_ref[SWA_GROUP * c + g] += -jnp.sum(p_sink[g * SWA_BLOCK:(g + 1) * SWA_BLOCK])
            tail = slice((bps - 1) * SWA_BLOCK, bps * SWA_BLOCK)
            kc, vc = k_ref[c, tail, :], v_ref[c, tail, :]
            qs = _swa_stack(qn_ref, c, everything)
            dos = _swa_stack(don_ref, c, everything)
            lcol = _swa_cols(ln_ref, c, everything)
            dcol = _swa_cols(dn_ref, c, everything)
            p_p = probs(qs, kc, jnp.logical_and(m_prev, n < steps - 1), lcol)
            ds_p = dscores(p_p, dos, vc, dcol)
            dk_acc[bps - 1] = dk_acc[bps - 1] + _dot_tn(ds_p, qs)
            dv_acc[bps - 1] = dv_acc[bps - 1] + _dot_tn(_mx(p_p), dos)
            for b in range(bps):
                rows = slice(b * SWA_BLOCK, (b + 1) * SWA_BLOCK)
                dk_ref[c, rows, :] = dk_acc[b]
                dv_ref[c, rows, :] = dv_acc[b]

    main = lambda n: (0, n, 0)
    prev = lambda n: (0, jnp.maximum(n * bps - 1, 0), 0)
    nxt = lambda n: (0, jnp.minimum((n + 1) * bps, nblk - 1), 0)
    pairs = pl.BlockSpec((SWA_PAIRS, sb, LANES), main)
    kvs = pl.BlockSpec((SWA_KV, sb, LANES), main)
    kv_prev = pl.BlockSpec((SWA_KV, SWA_BLOCK, LANES), prev)
    stats = pl.BlockSpec((SWA_HEADS, sb, LANES), main)
    pairs_next = pl.BlockSpec((SWA_PAIRS, SWA_BLOCK, LANES), nxt)
    stats_next = pl.BlockSpec((SWA_HEADS, SWA_BLOCK, LANES), nxt)
    return pl.pallas_call(
        body, name="swa_bwd", grid=(steps,),
        in_specs=[pairs, kvs, kv_prev, kvs, kv_prev, _const((SWA_HEADS, 1, LANES)), pairs, stats, stats,
                  pairs_next, pairs_next, stats_next, stats_next],
        out_specs=[pairs, kvs, kvs, _acc((SWA_HEADS, 1, LANES))],
        out_shape=[_sds((SWA_PAIRS, t, LANES), jnp.float32), _sds((SWA_KV, t, LANES), jnp.float32),
                   _sds((SWA_KV, t, LANES), jnp.float32), _sds((SWA_HEADS, 1, LANES), jnp.float32)],
        compiler_params=_cparams(("arbitrary",)),
    )(q, k, k, v, v, sinks, do, lse, dsum, q, do, lse, dsum)


MIX_SLABS = 4
MIX_WIDTH = MIX_SLABS * LANES


def _mix_out_fwd(x, oa, ob, ga, gb, wo_a, wo_b):
    t, d = x.shape
    tb = min(_TB, t)

    def body(x_ref, oa_ref, ob_ref, ga_ref, gb_ref, woa_ref, wob_ref, xo_ref):
        y = x_ref[...]
        for o_ref, g_ref, w_ref in ((oa_ref, ga_ref, woa_ref), (ob_ref, gb_ref, wob_ref)):
            r = _rsq(sum(_sumsq(o_ref[h]) for h in range(MIX_SLABS)), MIX_WIDTH)
            for h in range(MIX_SLABS):
                y = y + _dot(_mx(o_ref[h] * r * g_ref[h]), w_ref[h])
        xo_ref[...] = y

    slab = _heads_rows(MIX_SLABS, tb, LANES)
    return pl.pallas_call(
        body, name="mix_out_fwd", grid=(t // tb,),
        in_specs=[_rows(tb, d), slab, slab, _const(ga.shape), _const(gb.shape), _const(wo_a.shape), _const(wo_b.shape)],
        out_specs=_rows(tb, d),
        out_shape=_sds((t, d), jnp.float32),
        compiler_params=_cparams(("parallel",)),
    )(x, oa, ob, ga, gb, wo_a, wo_b)


def _mix_out_bwd(dx, oa, ob, ga, gb, wo_a, wo_b):
    t, d = dx.shape
    tb = min(_TB, t)

    def group(o_ref, g_ref, w_ref, dyb, do_ref, n_ref, col0, dg_ref):
        r = _rsq(sum(_sumsq(o_ref[h]) for h in range(MIX_SLABS)), MIX_WIDTH)
        xh, dn = [], []
        for h in range(MIX_SLABS):
            xh.append(o_ref[h] * r)
            n_ref[:, col0 + h * LANES:col0 + (h + 1) * LANES] = _mx(xh[h] * g_ref[h])
            dm = _dot_nt(dyb, w_ref[h])
            dg_ref[h] += _colsum(dm * xh[h])
            dn.append(dm * g_ref[h])
        c = sum(_rowsum(dn[h] * xh[h]) for h in range(MIX_SLABS)) * (1.0 / MIX_WIDTH)
        prods = []
        for h in range(MIX_SLABS):
            do = r * (dn[h] - xh[h] * c)
            do_ref[h] = do.astype(do_ref.dtype)
            prods.append(do * o_ref[h])
        return prods

    def body(dx_ref, oa_ref, ob_ref, ga_ref, gb_ref, woa_ref, wob_ref,
             doa_ref, dsa_ref, dob_ref, dsb_ref, n_ref, dy_ref, dga_ref, dgb_ref):
        @pl.when(pl.program_id(0) == 0)
        def _():
            dga_ref[...] = jnp.zeros_like(dga_ref)
            dgb_ref[...] = jnp.zeros_like(dgb_ref)

        dyb = _mx(dx_ref[...])
        dy_ref[...] = dyb
        for h, pr in enumerate(group(oa_ref, ga_ref, woa_ref, dyb, doa_ref, n_ref, 0, dga_ref)):
            dsa_ref[h, 0] = _as_row(_rowsum(pr))
        low = _low_half()
        for j, pr in enumerate(group(ob_ref, gb_ref, wob_ref, dyb, dob_ref, n_ref, MIX_WIDTH, dgb_ref)):
            dsb_ref[2 * j] = jnp.broadcast_to(_rowsum(jnp.where(low, pr, 0.0)), (tb, LANES))
            dsb_ref[2 * j + 1] = jnp.broadcast_to(_rowsum(jnp.where(low, 0.0, pr)), (tb, LANES))

    slab = _heads_rows(MIX_SLABS, tb, LANES)
    return pl.pallas_call(
        body, name="mix_out_bwd", grid=(t // tb,),
        in_specs=[_rows(tb, d), slab, slab, _const(ga.shape), _const(gb.shape), _const(wo_a.shape), _const(wo_b.shape)],
        out_specs=[slab, pl.BlockSpec((MIX_SLABS, 1, 1, tb), lambda i: (0, i, 0, 0)), slab,
                   _heads_rows(SWA_HEADS, tb, LANES), _rows(tb, 2 * MIX_WIDTH), _rows(tb, d),
                   _acc(ga.shape), _acc(gb.shape)],
        out_shape=[_sds((MIX_SLABS, t, LANES), _MXU), _sds((MIX_SLABS, t // tb, 1, tb), jnp.float32),
                   _sds((MIX_SLABS, t, LANES), jnp.float32), _sds((SWA_HEADS, t, LANES), jnp.float32),
                   _sds((t, 2 * MIX_WIDTH), _MXU), _sds((t, d), _MXU),
                   _sds(ga.shape, jnp.float32), _sds(gb.shape, jnp.float32)],
        compiler_params=_cparams(("arbitrary",)),
    )(dx, oa, ob, ga, gb, wo_a, wo_b)


def _is_transposed(name):
    return name not in ROW_SHARDED


def _pack_layer(shards, l, width, names):
    rows = [(shards[n][l].T if _is_transposed(n) else shards[n][l]).reshape(-1, width) for n in names]
    return [jnp.concatenate(rows, axis=0)] if names is OTHER_BIG else rows


def _full_shape(like, name):
    _, a, b = like[name].shape
    return (N_DEV * b, a) if _is_transposed(name) else (N_DEV * a, b)


def _unpack_full(gathered, like, names):
    if names is not OTHER_BIG:
        return {n: g.reshape(_full_shape(like, n)) for n, g in zip(names, gathered)}
    (gathered,), out, off = gathered, {}, 0
    for n in names:
        rows_n = like[n][0].size // gathered.shape[-1]
        out[n] = gathered[:, off:off + rows_n].reshape(_full_shape(like, n))
        off += rows_n
    return out


def _stored(a, name):
    return jnp.swapaxes(a, 1, 2) if _is_transposed(name) else a


def _grads_by_destination(grads, width, names):
    by_dest = lambda n: grads[n].reshape(N_DEV, -1, width)
    if names is OTHER_BIG:
        return [jnp.concatenate([by_dest(n) for n in names], axis=1)]
    return [by_dest(n) for n in names]


def _shards_from_rows(rows, like):
    out = dict(zip(FFN_BIG, rows[:len(FFN_BIG)]))
    rest, off = rows[len(FFN_BIG)], 0
    for n in OTHER_BIG:
        _, a, b = like[n].shape
        rows_n = a * b // rest.shape[-1]
        out[n] = rest[off:off + rows_n].reshape((b, a) if _is_transposed(n) else (a, b))
        off += rows_n
    return out


def _small_rows(n_elems):
    return -(-n_elems // LANES)


def _pack_small(arrays):
    parts = []
    for n in SMALL:
        v = arrays[n]
        depth, width = v.shape
        padded = _small_rows(width) * LANES
        parts.append(jnp.pad(v, ((0, 0), (0, padded - width))).reshape(-1, LANES))
    packed = jnp.concatenate(parts, axis=0)
    return jnp.pad(packed, ((0, (-packed.shape[0]) % 8), (0, 0)))


def _unpack_small(packed, like):
    out, off = {}, 0
    for n in SMALL:
        depth, width = like[n].shape
        rows_n = _small_rows(width)
        seg = packed[off:off + depth * rows_n].reshape(depth, rows_n * LANES)
        out[n] = seg[:, :width]
        off += depth * rows_n
    return out


def _rope_tables(t):
    pos = jnp.arange(t, dtype=jnp.float32)
    inv = 1.0 / (ROPE_THETA ** (jnp.arange(0, MLA_ROPE, 2, dtype=jnp.float32) / MLA_ROPE))
    ang = pos[:, None] * inv[None, :]
    cos, sin = jnp.cos(ang), jnp.sin(ang)
    return jnp.concatenate([cos, cos, cos, cos], axis=1), jnp.concatenate([-sin, sin, -sin, sin], axis=1)


def _pad_lanes(a, width):
    return jnp.pad(a, [(0, 0)] * (a.ndim - 1) + [(0, width - a.shape[-1])])


def _ffn_params(full, small, l, tag):
    return small[tag + "_norm"][l][None, :], full[tag + "_w_gate"], full[tag + "_w_up"], full[tag + "_w_down"]


def _mixer_params(full, small, l):
    w_in = full["w_in"]
    d = w_in.shape[1]
    mla_rows = W_IN_COLS[0]
    w_in_p = jnp.concatenate([w_in[:mla_rows], jnp.zeros((LANES - MLA_ROPE, d), w_in.dtype), w_in[mla_rows:]], axis=0)
    wqb = full["mla_w_q_b"].reshape(MLA_HEADS, MLA_QK, MLA_Q_RANK)
    wqb = jnp.pad(wqb, ((0, 0), (0, MLA_QK_PAD - MLA_QK), (0, 0)))
    row = lambda name: small[name][l][None, :]
    twice = lambda g: jnp.concatenate([g, g], axis=1)
    prep = {
        "mix_g": row("mix_norm"), "w_in": w_in_p,
        "g_qa": row("mla_q_a_norm"), "wqb": wqb,
        "g_kva": row("mla_kv_a_norm"), "w_kvb": full["mla_w_kv_b"],
        "gq_n": row("mla_q_norm")[:, :MLA_NOPE], "gq_r": _pad_lanes(row("mla_q_norm")[:, MLA_NOPE:], LANES),
        "gk_n": row("mla_k_norm")[:, :MLA_NOPE], "gk_r": _pad_lanes(row("mla_k_norm")[:, MLA_NOPE:], LANES),
        "g_sq": twice(row("swa_q_norm")), "g_sk": twice(row("swa_k_norm")),
    }
    return {
        "prep": prep,
        "sinks": jnp.broadcast_to(small["swa_sinks"][l][:, None, None], (SWA_HEADS, 1, LANES)),
        "ga": small["mla_out_norm"][l].reshape(MIX_SLABS, 1, LANES),
        "gb": small["swa_out_norm"][l].reshape(MIX_SLABS, 1, LANES),
        "wo_a": full["w_o"][:MIX_WIDTH].reshape(MIX_SLABS, LANES, d),
        "wo_b": full["w_o"][MIX_WIDTH:].reshape(MIX_SLABS, LANES, d),
    }


def _ffn_backward(x_in, dxo, kept, params, tag, sides=()):
    gain, wg, wu, wd = params
    h, s, fa, fu = kept
    (dxi, da, du, dy, dg), side_out = _ffn_dgrad(x_in, gain, dxo, fa, fu, wg, wu, wd, sides)
    dwg, _ = _tn_matmul(da, h, "wgrad_" + tag + "_gate")
    dwu, _ = _tn_matmul(du, h, "wgrad_" + tag + "_up")
    dwd, _ = _tn_matmul(s, dy, "wgrad_" + tag + "_down")
    return dxi, dg[0], dwg, dwu, dwd, side_out


def _ffn_backward_exchanging(x_in, dxo, kept, params, tag, ready, chip_sums, width):
    gain, wg, wu, wd = params
    h, s, fa, fu = kept
    (dxi, da, du, dy, dg), (ready_sib,) = _ffn_dgrad(x_in, gain, dxo, fa, fu, wg, wu, wd, [_side_sibling(ready)])
    ready_owns, ready_wires = chip_sums(ready, ready_sib)
    dwd, (ready_recv,) = _tn_matmul(s, dy, "wgrad_" + tag + "_down", [_side_chips(ready_wires)])
    down = [dwd.reshape(N_DEV, -1, width)]
    dwg, (down_sib,) = _tn_matmul(da, h, "wgrad_" + tag + "_gate", [_side_sibling(down)])
    down_owns, down_wires = chip_sums(down, down_sib)
    dwu, (down_recv,) = _tn_matmul(du, h, "wgrad_" + tag + "_up", [_side_chips(down_wires)])
    return dxi, dg[0], dwg, dwu, _rs_sum(ready_owns, ready_recv), _rs_sum(down_owns, down_recv)


def kernel(x, ffn1_norm, ffn1_w_gate, ffn1_w_up, ffn1_w_down, mix_norm, w_in, mla_q_a_norm, mla_w_q_b, mla_kv_a_norm, mla_w_kv_b, mla_q_norm, mla_k_norm, swa_q_norm, swa_k_norm, swa_sinks, mla_out_norm, swa_out_norm, w_o, ffn2_norm, ffn2_w_gate, ffn2_w_up, ffn2_w_down, loss_target, m_ffn1_norm, m_ffn1_w_gate, m_ffn1_w_up, m_ffn1_w_down, m_mix_norm, m_w_in, m_mla_q_a_norm, m_mla_w_q_b, m_mla_kv_a_norm, m_mla_w_kv_b, m_mla_q_norm, m_mla_k_norm, m_swa_q_norm, m_swa_k_norm, m_swa_sinks, m_mla_out_norm, m_swa_out_norm, m_w_o, m_ffn2_norm, m_ffn2_w_gate, m_ffn2_w_up, m_ffn2_w_down, v_ffn1_norm, v_ffn1_w_gate, v_ffn1_w_up, v_ffn1_w_down, v_mix_norm, v_w_in, v_mla_q_a_norm, v_mla_w_q_b, v_mla_kv_a_norm, v_mla_w_kv_b, v_mla_q_norm, v_mla_k_norm, v_swa_q_norm, v_swa_k_norm, v_swa_sinks, v_mla_out_norm, v_swa_out_norm, v_w_o, v_ffn2_norm, v_ffn2_w_gate, v_ffn2_w_up, v_ffn2_w_down):
    local = dict(locals())
    w = {n: local[n] for n in WEIGHTS}
    m = {n: local["m_" + n] for n in WEIGHTS}
    v = {n: local["v_" + n] for n in WEIGHTS}
    depth = ffn1_norm.shape[0]
    t, d = x.shape[-2], x.shape[-1]
    x2d = x.reshape(t, d)
    target = loss_target.reshape(t, d)
    bq = min(_BQ, t)

    big = {n: w[n] for n in BIG}
    packed = [[[_mx(a) for a in _pack_layer(big, l, d, names)] for names in GATHER_ORDER] for l in range(depth)]
    cos, sin_s = _rope_tables(t)
    x_i, y_i, c_i = _position()
    dest_idx = jnp.stack([4 * px + 2 * py + c_i for px, py in _relations(x_i, y_i)]).astype(jnp.int32)

    params, saved = [], []
    xc = x2d
    ffn1_full = _unpack_full(_all_gather(packed[0][0]), big, FFN1_BIG)
    for l in range(depth):
        pr = {"ffn1": _ffn_params(ffn1_full, w, l, "ffn1")}
        x0 = xc
        (x1, *kept1), (mixer_gathered,) = _ffn_fwd(x0, *pr["ffn1"], sides=[_side_gather(packed[l][1])])
        pr.update(_mixer_params(_unpack_full(mixer_gathered, big, OTHER_BIG), w, l))
        qa, ka, va, qb, kb, vb = _prep_fwd(x1, cos, sin_s, pr["prep"])
        (oa, lse_a), (ffn2_gathered,) = _mla_fwd(qa, ka, va, sides=[_side_gather(packed[l][2])])
        pr["ffn2"] = _ffn_params(_unpack_full(ffn2_gathered, big, FFN2_BIG), w, l, "ffn2")
        ob, lse_b = _swa_fwd(qb, kb, vb, pr["sinks"])
        x2 = _mix_out_fwd(x1, oa, ob, pr["ga"], pr["gb"], pr["wo_a"], pr["wo_b"])
        if l + 1 < depth:
            (xc, *kept2), (next_gathered,) = _ffn_fwd(x2, *pr["ffn2"], sides=[_side_gather(packed[l + 1][0])])
            ffn1_full = _unpack_full(next_gathered, big, FFN1_BIG)
        else:
            (dx, sq_err, *kept2), _ = _ffn_fwd(x2, *pr["ffn2"], target=target)
        params.append(pr)
        saved.append((x0, kept1, x1, qa, ka, va, qb, kb, vb, oa, lse_a, ob, lse_b, x2, kept2))

    loss = lax.psum(0.5 / d * sq_err[0, 0], MESH_AXES)

    def chip_sums(arrays, sibling_parts):
        sums = _rs_chip_sums(arrays, sibling_parts, dest_idx)
        return sums[:len(arrays)], sums[len(arrays):]

    grad_shards = [None] * depth
    small_grads = {n: [None] * depth for n in SMALL}
    upper = None
    for l in reversed(range(depth)):
        pr = params[l]
        lowest = l == 0
        x0, kept1, x1, qa, ka, va, qb, kb, vb, oa, lse_a, ob, lse_b, x2, kept2 = saved[l]
        g = {}
        dx, small_grads["ffn2_norm"][l], g["ffn2_w_gate"], g["ffn2_w_up"], g["ffn2_w_down"], side_out = _ffn_backward(
            x2, dx, kept2, pr["ffn2"], "ffn2", [_side_sibling(upper)] if upper else [])
        if upper:
            upper_owns, upper_wires = chip_sums(upper, side_out[0])
        early = _grads_by_destination(g, d, FFN2_BIG) if lowest else None

        doa, dsum_a, dob, dsum_b, mixed, dyb, dga, dgb = _mix_out_bwd(
            dx, oa, ob, pr["ga"], pr["gb"], pr["wo_a"], pr["wo_b"])
        small_grads["mla_out_norm"][l] = dga.reshape(-1)
        small_grads["swa_out_norm"][l] = dgb.reshape(-1)
        g["w_o"], _ = _tn_matmul(mixed, dyb, "wgrad_wo")

        rows_of = lambda s: s.reshape(MLA_HEADS, t // bq, 1, bq)
        sides = ([_side_chips(upper_wires)] if upper else []) + ([_side_sibling(early)] if lowest else [])
        (dqa, dka, dva), side_out = _mla_bwd(qa, ka, va, doa, rows_of(lse_a), rows_of(dsum_a), sides)
        if upper:
            grad_shards[l + 1] = _shards_from_rows(_rs_sum(upper_owns, side_out[0]), big)
        if lowest:
            early_owns, early_wires = chip_sums(early, side_out[-1])
        dqb, dkb, dvb, dsinks = _swa_bwd(qb, kb, vb, pr["sinks"], dob, lse_b, dsum_b)
        small_grads["swa_sinks"][l] = dsinks[:, 0, 0]

        outs, side_out = _prep_bwd(x1, dx, cos, sin_s, pr["prep"], dqa, dka, dva, dqb, dkb, dvb,
                                   [_side_chips(early_wires)] if lowest else [])
        if lowest:
            early_rows = _rs_sum(early_owns, side_out[0])
        dx = outs[0]
        pg = dict(zip(PREP_WEIGHTS, outs[1:]))
        g["w_in"] = jnp.concatenate([pg["w_in"][:W_IN_COLS[0]], pg["w_in"][C_QS:]], axis=0)
        g["mla_w_q_b"] = pg["wqb"][:, :MLA_QK].reshape(MLA_HEADS * MLA_QK, MLA_Q_RANK)
        g["mla_w_kv_b"] = pg["w_kvb"]
        fold = lambda gg: gg[0, :HALF] + gg[0, HALF:]
        small_grads["mix_norm"][l] = pg["mix_g"][0]
        small_grads["mla_q_a_norm"][l] = pg["g_qa"][0]
        small_grads["mla_kv_a_norm"][l] = pg["g_kva"][0]
        small_grads["mla_q_norm"][l] = jnp.concatenate([pg["gq_n"][0], pg["gq_r"][0, :MLA_ROPE]])
        small_grads["mla_k_norm"][l] = jnp.concatenate([pg["gk_n"][0], pg["gk_r"][0, :MLA_ROPE]])
        small_grads["swa_q_norm"][l] = fold(pg["g_sq"])
        small_grads["swa_k_norm"][l] = fold(pg["g_sk"])

        if lowest:
            other = _grads_by_destination(g, d, OTHER_BIG)
            dx, small_grads["ffn1_norm"][l], g["ffn1_w_gate"], g["ffn1_w_up"], other_rows, down_rows = (
                _ffn_backward_exchanging(x0, dx, kept1, pr["ffn1"], "ffn1", other, chip_sums, d))
            late = _grads_by_destination(g, d, FFN1_BIG[:2])
            late_owns, late_wires = chip_sums(late, _rs_sibling_exchange(late))
            late_rows = _rs_sum(late_owns, _rs_chip_exchange(late_wires))
            grad_shards[l] = _shards_from_rows(late_rows + down_rows + early_rows + other_rows, big)
        else:
            dx, small_grads["ffn1_norm"][l], g["ffn1_w_gate"], g["ffn1_w_up"], g["ffn1_w_down"], _ = _ffn_backward(
                x0, dx, kept1, pr["ffn1"], "ffn1")
            upper = _grads_by_destination(g, d, FFN_BIG) + _grads_by_destination(g, d, OTHER_BIG)

    grad_big, delta_big, new_m_big, new_v_big = {}, {}, {}, {}
    for n in BIG:
        g_st = jnp.stack([grad_shards[l][n] for l in range(depth)])
        d_st, m_st, v_st = _adamw(g_st, _stored(w[n], n), _stored(m[n], n), _stored(v[n], n), n)
        grad_big[n], delta_big[n], new_m_big[n], new_v_big[n] = (_stored(a, n) for a in (g_st, d_st, m_st, v_st))

    small_partial = _pack_small({n: jnp.stack(small_grads[n]) for n in SMALL})
    grad_small = _unpack_small(_all_reduce_small(small_partial), w)
    updated = _adamw_small(*[[a[n] for n in SMALL] for a in (grad_small, w, m, v)])
    count = len(SMALL)
    delta_small, new_m_small, new_v_small = (dict(zip(SMALL, updated[k * count:(k + 1) * count])) for k in range(3))

    def ordered(big, small):
        return [big[n] if n in big else small[n] for n in WEIGHTS]

    return (loss, dx.reshape(x.shape), *ordered(grad_big, grad_small), *ordered(delta_big, delta_small),
            *ordered(new_m_big, new_m_small), *ordered(new_v_big, new_v_small))
```

```python
import jax
import jax.numpy as jnp
from jax import lax
from jax.experimental import pallas as pl
from jax.experimental.pallas import tpu as pltpu

N_DEV = 8
EPS = 1e-6
ROPE_THETA = 10000.0
MLA_HEADS = 4
MLA_Q_RANK = 256
MLA_KV_RANK = 128
MLA_NOPE = 128
MLA_ROPE = 64
MLA_V = 128
MLA_QK = MLA_NOPE + MLA_ROPE
MLA_QK_PAD = 256
SWA_HEADS = 8
SWA_KV = 2
SWA_GROUP = SWA_HEADS // SWA_KV
SWA_D = 64
SWA_BLOCK = 128
ADAM_LR = 0.001
ADAM_B1 = 0.9
ADAM_B2 = 0.999
ADAM_EPS = 1e-08
ADAM_WD = 0.01
ADAM_STEP = 10

LANES = 128
HALF = LANES // 2
SWA_PAIRS = SWA_HEADS // 2
W_IN_COLS = (MLA_Q_RANK + MLA_KV_RANK + MLA_ROPE, SWA_HEADS * SWA_D + 2 * SWA_KV * SWA_D)
VMEM_LIMIT = 56 * 1024 * 1024

_MXU = jnp.bfloat16
_TB = 256
_TB_MIX = 512
_BQ = 512
_STRIP = 32
_BWD_GROUPS = (4, 2, 1)
_TK = 1024
_SWA_STEP = 4
RS_ROW_BLOCKS = 2

BIG = ("ffn1_w_gate", "ffn1_w_up", "ffn1_w_down", "w_in", "mla_w_q_b", "mla_w_kv_b", "w_o",
       "ffn2_w_gate", "ffn2_w_up", "ffn2_w_down")
ROW_SHARDED = ("ffn1_w_down", "w_o", "ffn2_w_down")
FFN1_BIG = ("ffn1_w_gate", "ffn1_w_up", "ffn1_w_down")
FFN2_BIG = ("ffn2_w_gate", "ffn2_w_up", "ffn2_w_down")
FFN_BIG = FFN1_BIG + FFN2_BIG
OTHER_BIG = ("w_o", "w_in", "mla_w_q_b", "mla_w_kv_b")
GATHER_ORDER = (FFN1_BIG, OTHER_BIG, FFN2_BIG)
SMALL = ("ffn1_norm", "mix_norm", "mla_q_a_norm", "mla_kv_a_norm", "mla_q_norm", "mla_k_norm",
         "swa_q_norm", "swa_k_norm", "swa_sinks", "mla_out_norm", "swa_out_norm", "ffn2_norm")
WEIGHTS = ("ffn1_norm", "ffn1_w_gate", "ffn1_w_up", "ffn1_w_down", "mix_norm", "w_in", "mla_q_a_norm",
           "mla_w_q_b", "mla_kv_a_norm", "mla_w_kv_b", "mla_q_norm", "mla_k_norm", "swa_q_norm",
           "swa_k_norm", "swa_sinks", "mla_out_norm", "swa_out_norm", "w_o", "ffn2_norm",
           "ffn2_w_gate", "ffn2_w_up", "ffn2_w_down")
MESH_AXES = ("x", "y", "c")
MESH = pl.DeviceIdType.MESH
NEG = -1e30
LOG2_E = 1.4426950408889634


def _f32(t):
    return t.astype(jnp.float32)


def _mx(t):
    return t.astype(_MXU)


def _dot(a, b):
    return jnp.dot(a, b, preferred_element_type=jnp.float32)


def _dot_nt(a, b):
    return lax.dot_general(a, b, (((1,), (1,)), ((), ())), preferred_element_type=jnp.float32)


def _dot_tn(a, b):
    return lax.dot_general(a, b, (((0,), (0,)), ((), ())), preferred_element_type=jnp.float32)


def _rsq(ss, n):
    return lax.rsqrt(ss * (1.0 / n) + EPS)


def _sumsq(t):
    return jnp.sum(t * t, axis=-1, keepdims=True)


def _rowsum(t):
    return jnp.sum(t, axis=-1, keepdims=True)


def _rowmax(t):
    return jnp.max(t, axis=-1, keepdims=True)


def _colsum(t):
    return jnp.sum(t, axis=0, keepdims=True)


def _lane():
    return lax.broadcasted_iota(jnp.int32, (1, LANES), 1)


def _low_half():
    return _lane() < HALF


def _swap32(t):
    return jnp.where((_lane() & 32) == 0, pltpu.roll(t, 96, 1), pltpu.roll(t, 32, 1))


def _rope(t, cos, sin_signed):
    return t * cos + _swap32(t) * sin_signed


def _rope_bwd(d, cos, sin_signed):
    return d * cos + _swap32(d * sin_signed)


def _half_sums(t):
    low = _low_half()
    return jnp.where(low, _rowsum(jnp.where(low, t, 0.0)), _rowsum(jnp.where(low, 0.0, t)))


def _dup_halves(pair):
    low = _low_half()
    swapped = pltpu.roll(pair, HALF, 1)
    return jnp.where(low, pair, swapped), jnp.where(low, swapped, pair)


def _undup_halves(d0, d1):
    return jnp.where(_low_half(), d0 + pltpu.roll(d0, HALF, 1), d1 + pltpu.roll(d1, HALF, 1))


def _pick_halves(a, b):
    return jnp.where(_low_half(), a, b)


def _as_row(col):
    return jnp.transpose(jnp.broadcast_to(col, (col.shape[0], LANES)))[0:1, :]


def _norm_bwd(dn_list, xh_list, r, n):
    c = sum(_rowsum(dn * xh) for dn, xh in zip(dn_list, xh_list)) * (1.0 / n)
    return [r * (dn - xh * c) for dn, xh in zip(dn_list, xh_list)]


def _cparams(semantics):
    return pltpu.CompilerParams(dimension_semantics=semantics, vmem_limit_bytes=VMEM_LIMIT)


def _const(shape):
    nd = len(shape)
    return pl.BlockSpec(shape, lambda *_: (0,) * nd, pipeline_mode=pl.Buffered(1))


def _acc(shape):
    nd = len(shape)
    return pl.BlockSpec(shape, lambda *_: (0,) * nd)


def _rows(tb, width):
    return pl.BlockSpec((tb, width), lambda i: (i, 0))


def _heads_rows(h, tb, width):
    return pl.BlockSpec((h, tb, width), lambda i: (0, i, 0))


def _sds(shape, dtype):
    return jax.ShapeDtypeStruct(shape, dtype)


def _position():
    return lax.axis_index("x"), lax.axis_index("y"), lax.axis_index("c")


def _all_gather(xs):
    n = len(xs)

    def body(*refs):
        x_refs, out_refs, (send_sems, recv_sems, local_sems) = refs[:n], refs[n:2 * n], refs[2 * n:]
        x, y, c = _position()
        me, sibling = (x, y, c), (x, y, 1 - c)
        chips = [(1 - x, y), (x, 1 - y), (1 - x, 1 - y)]

        def rows(i, px, py, pc):
            return out_refs[i].at[4 * px + 2 * py + pc]

        def copy(i, k, block, to, from_input=False):
            return _remote(x_refs[i] if from_input else rows(i, *block), rows(i, *block),
                           send_sems.at[k * n + i], recv_sems.at[k * n + i], to)

        every = range(n)
        mine = [pltpu.make_async_copy(x_refs[i], rows(i, *me), local_sems.at[i]) for i in every]
        first = [copy(i, 0, me, sibling, True) for i in every]
        first += [copy(i, 1 + j, me, (*chip, c), True) for j, chip in enumerate(chips) for i in every]
        for cp in mine + first:
            cp.start()
        passed = [[copy(i, 4 + j, (*chip, c), sibling) for i in every] for j, chip in enumerate(chips)]
        for j, chip in enumerate(chips):
            for i in every:
                copy(i, 1 + j, (*chip, c), me).wait_recv()
                passed[j][i].start()
        for i in every:
            copy(i, 0, sibling, me).wait_recv()
        for j, chip in enumerate(chips):
            for i in every:
                copy(i, 4 + j, (*chip, 1 - c), me).wait_recv()
        for cp in first + [cp for group in passed for cp in group]:
            cp.wait_send()
        for cp in mine:
            cp.wait()

    hbm = pl.BlockSpec(memory_space=pl.ANY)
    dma = pltpu.SemaphoreType.DMA
    return pl.pallas_call(
        body, name="ag_weights",
        out_shape=[_sds((N_DEV,) + a.shape, a.dtype) for a in xs],
        in_specs=[hbm] * n, out_specs=[hbm] * n,
        scratch_shapes=[dma((7 * n,)), dma((7 * n,)), dma((n,))],
    )(*xs)


def _relations(x, y):
    return [(x, y), (1 - x, y), (x, 1 - y), (1 - x, 1 - y)]


def _remote(src, dst, send_sem, recv_sem, device):
    return pltpu.make_async_remote_copy(src_ref=src, dst_ref=dst, send_sem=send_sem, recv_sem=recv_sem,
                                        device_id=device, device_id_type=MESH)


def _sibling_copies(g_refs, out_refs, send, recv):
    x, y, c = _position()
    n = len(g_refs)
    return [_remote(g.at[4 * px + 2 * py + (1 - c)], o.at[k], send.at[k * n + i], recv.at[k * n + i], (x, y, 1 - c))
            for k, (px, py) in enumerate(_relations(x, y)) for i, (g, o) in enumerate(zip(g_refs, out_refs))]


def _chip_copies(w_refs, out_refs, send, recv):
    x, y, c = _position()
    n = len(w_refs)
    return [_remote(w.at[k + 1], o.at[k], send.at[k * n + i], recv.at[k * n + i], (px, py, c))
            for k, (px, py) in enumerate(_relations(x, y)[1:]) for i, (w, o) in enumerate(zip(w_refs, out_refs))]


def _exchange(arrays, lead, relations, copies_fn, name):
    n = len(arrays)

    def body(*refs):
        copies = copies_fn(refs[:n], refs[n:2 * n], refs[2 * n], refs[2 * n + 1])
        for cp in copies:
            cp.start()
        for cp in copies:
            cp.wait()

    hbm = pl.BlockSpec(memory_space=pl.ANY)
    dma = pltpu.SemaphoreType.DMA
    return pl.pallas_call(
        body, name=name, out_shape=[_sds((lead,) + a.shape[1:], a.dtype) for a in arrays],
        in_specs=[hbm] * n, out_specs=[hbm] * n,
        scratch_shapes=[dma((relations * n,)), dma((relations * n,))],
    )(*arrays)


def _rs_sibling_exchange(gs):
    return _exchange(gs, 4, 4, _sibling_copies, "rs_sibling_exchange")


def _rs_chip_exchange(wires):
    return _exchange(wires, 3, 3, _chip_copies, "rs_chip_exchange")


def _side_exchange(arrays, lead, relations, copies_fn):
    shapes = [_sds((lead,) + a.shape[1:], a.dtype) for a in arrays]
    return list(arrays), shapes, relations * len(arrays), lambda ins, outs, send, recv, local: copies_fn(ins, outs, send, recv)


def _side_sibling(gs):
    return _side_exchange(gs, 4, 4, _sibling_copies)


def _side_chips(wires):
    return _side_exchange(wires, 3, 3, _chip_copies)


def _rs_chip_sums(gs, sibs, dest_idx):
    n = len(gs)

    def body(idx_ref, *refs):
        g_refs, s_refs, own_refs, wire_refs = refs[:n], refs[n:2 * n], refs[2 * n:3 * n], refs[3 * n:]
        totals = [g[0] + s[0] for g, s in zip(g_refs, s_refs)]
        for total, wire in zip(totals, wire_refs):
            wire[0] = total.astype(wire.dtype)

        @pl.when(pl.program_id(1) == 0)
        def _():
            for total, own in zip(totals, own_refs):
                own[...] = total

    def blocks(a, index_map, squeeze):
        rb = a.shape[1] // RS_ROW_BLOCKS
        return pl.BlockSpec((rb, a.shape[2]) if squeeze else (1, rb, a.shape[2]), index_map)

    return pl.pallas_call(
        body, name="rs_chip_sums",
        grid_spec=pltpu.PrefetchScalarGridSpec(
            num_scalar_prefetch=1, grid=(RS_ROW_BLOCKS, 4),
            in_specs=[blocks(g, lambda r, k, idx: (idx[k], r, 0), False) for g in gs]
            + [blocks(g, lambda r, k, idx: (k, r, 0), False) for g in gs],
            out_specs=[blocks(g, lambda r, k, idx: (r, 0), True) for g in gs]
            + [blocks(g, lambda r, k, idx: (k, r, 0), False) for g in gs]),
        out_shape=[_sds(g.shape[1:], jnp.float32) for g in gs] + [_sds((4,) + g.shape[1:], _MXU) for g in gs],
        compiler_params=_cparams(("parallel", "arbitrary")),
    )(dest_idx, *gs, *sibs)


def _side_gather(xs):
    n = len(xs)

    def make(ins, outs, send, recv, local):
        x, y, c = _position()
        me = 4 * x + 2 * y + c
        copies = [pltpu.make_async_copy(x_ref, out_ref.at[me], local.at[i])
                  for i, (x_ref, out_ref) in enumerate(zip(ins, outs))]
        for k in range(1, N_DEV):
            peer = (1 - x if k & 4 else x, 1 - y if k & 2 else y, 1 - c if k & 1 else c)
            copies += [_remote(x_ref, out_ref.at[me], send.at[(k - 1) * n + i], recv.at[(k - 1) * n + i], peer)
                       for i, (x_ref, out_ref) in enumerate(zip(ins, outs))]
        return copies

    return list(xs), [_sds((N_DEV,) + a.shape, a.dtype) for a in xs], (N_DEV - 1) * n, make


def _call(body, args, sides, *, name, grid, in_specs, out_specs, out_shape, scratch_shapes=(), semantics):
    in_specs, out_specs, out_shape = list(in_specs), list(out_specs), list(out_shape)
    n_in, n_out, n_scr = len(in_specs), len(out_specs), len(scratch_shapes)
    sides = list(sides or [])
    if not sides:
        outs = pl.pallas_call(body, name=name, grid=grid, in_specs=in_specs, out_specs=out_specs, out_shape=out_shape,
                              scratch_shapes=list(scratch_shapes), compiler_params=_cparams(semantics))(*args)
        return list(outs), []
    arrays = [a for side in sides for a in side[0]]
    shapes = [s for side in sides for s in side[1]]
    n_side_in, n_side_out = len(arrays), len(shapes)
    hbm = pl.BlockSpec(memory_space=pl.ANY)

    def with_copies(*refs):
        main_in, refs = refs[:n_in], refs[n_in:]
        side_in, refs = refs[:n_side_in], refs[n_side_in:]
        main_out, refs = refs[:n_out], refs[n_out:]
        side_out, refs = refs[:n_side_out], refs[n_side_out:]
        main_scr, sems = refs[:n_scr], refs[n_scr:]
        copies = []
        for k, (side_arrays, side_shapes, _, make) in enumerate(sides):
            copies += make(side_in[:len(side_arrays)], side_out[:len(side_shapes)], *sems[3 * k:3 * k + 3])
            side_in, side_out = side_in[len(side_arrays):], side_out[len(side_shapes):]
        ids = [pl.program_id(a) for a in range(len(grid))]
        first, last = ids[0] == 0, ids[0] == grid[0] - 1
        for i, size in zip(ids[1:], grid[1:]):
            first, last = jnp.logical_and(first, i == 0), jnp.logical_and(last, i == size - 1)

        @pl.when(first)
        def _():
            for cp in copies:
                cp.start()

        body(*main_in, *main_out, *main_scr)

        @pl.when(last)
        def _():
            for cp in copies:
                cp.wait()

    dma = pltpu.SemaphoreType.DMA
    sem_shapes = [dma((n,)) for side in sides for n in (side[2], side[2], len(side[0]))]
    outs = pl.pallas_call(
        with_copies, name=name, grid=grid, in_specs=in_specs + [hbm] * n_side_in,
        out_specs=out_specs + [hbm] * n_side_out, out_shape=out_shape + shapes,
        scratch_shapes=list(scratch_shapes) + sem_shapes,
        compiler_params=_cparams(("arbitrary",) * len(grid)),
    )(*args, *arrays)
    side_outs, rest = [], list(outs[n_out:])
    for side in sides:
        side_outs.append(rest[:len(side[1])])
        rest = rest[len(side[1]):]
    return list(outs[:n_out]), side_outs


def _all_reduce_small(v):
    rows_n = v.shape[0]

    def body(v_ref, out_ref, buf, send_sems, recv_sems):
        x, y, c = _position()
        me = 4 * x + 2 * y + c
        buf[me] = v_ref[...]
        copies = []
        for k in range(1, N_DEV):
            px = 1 - x if k & 4 else x
            py = 1 - y if k & 2 else y
            pc = 1 - c if k & 1 else c
            copies.append(pltpu.make_async_remote_copy(
                src_ref=v_ref, dst_ref=buf.at[me],
                send_sem=send_sems.at[k - 1], recv_sem=recv_sems.at[k - 1], device_id=(px, py, pc), device_id_type=MESH))
        for cp in copies:
            cp.start()
        for cp in copies:
            cp.wait()
        total = buf[0]
        for d in range(1, N_DEV):
            total = total + buf[d]
        out_ref[...] = total

    return pl.pallas_call(
        body, name="ar_small",
        out_shape=_sds((rows_n, LANES), jnp.float32),
        in_specs=[pl.BlockSpec(memory_space=pltpu.VMEM)],
        out_specs=pl.BlockSpec(memory_space=pltpu.VMEM),
        scratch_shapes=[pltpu.VMEM((N_DEV, rows_n, LANES), jnp.float32),
                        pltpu.SemaphoreType.DMA((N_DEV - 1,)), pltpu.SemaphoreType.DMA((N_DEV - 1,))],
    )(v)


def _adamw_math(w, g, m, v):
    m = ADAM_B1 * m + (1.0 - ADAM_B1) * g
    v = ADAM_B2 * v + (1.0 - ADAM_B2) * (g * g)
    m_hat = m / (1.0 - ADAM_B1 ** ADAM_STEP)
    v_hat = v / (1.0 - ADAM_B2 ** ADAM_STEP)
    delta = -ADAM_LR * (m_hat / (jnp.sqrt(v_hat) + ADAM_EPS) + ADAM_WD * w)
    return delta, m, v


def _rs_sum(owns, recvs):
    n = len(owns)

    def body(*refs):
        own_refs, recv_refs, out_refs = refs[:n], refs[n:4 * n], refs[4 * n:]
        for i in range(n):
            r0, r1, r2 = recv_refs[3 * i:3 * i + 3]
            out_refs[i][...] = ((own_refs[i][...] + _f32(r0[0])) + _f32(r1[0])) + _f32(r2[0])

    def row(a):
        return pl.BlockSpec((a.shape[0] // RS_ROW_BLOCKS, a.shape[1]), lambda r: (r, 0))

    def slot(a, k):
        return pl.BlockSpec((1, a.shape[0] // RS_ROW_BLOCKS, a.shape[1]), lambda r: (k, r, 0))

    return pl.pallas_call(
        body, name="rs_sum", grid=(RS_ROW_BLOCKS,),
        in_specs=[row(a) for a in owns] + [slot(a, k) for a in owns for k in range(3)],
        out_specs=[row(a) for a in owns],
        out_shape=[_sds(a.shape, jnp.float32) for a in owns],
        compiler_params=_cparams(("parallel",)),
    )(*owns, *[r for r in recvs for _ in range(3)])


def _adamw(g, w, m, v, name):
    depth, a, b = w.shape

    def body(g_ref, w_ref, m_ref, v_ref, d_out, m_out, v_out):
        delta, m2, v2 = _adamw_math(w_ref[...], g_ref[...], m_ref[...], v_ref[...])
        d_out[...] = delta
        m_out[...] = m2
        v_out[...] = v2

    layer = pl.BlockSpec((1, a, b), lambda l: (l, 0, 0))
    return pl.pallas_call(
        body, name="adamw_" + name, grid=(depth,),
        in_specs=[layer] * 4, out_specs=[layer] * 3,
        out_shape=[_sds(w.shape, jnp.float32)] * 3,
        compiler_params=_cparams(("parallel",)),
    )(g, w, m, v)


def _adamw_small(gs, ws, ms, vs):
    n = len(gs)

    def body(*refs):
        g_refs, w_refs, m_refs, v_refs, outs = refs[:n], refs[n:2 * n], refs[2 * n:3 * n], refs[3 * n:4 * n], refs[4 * n:]
        for i in range(n):
            delta, m2, v2 = _adamw_math(w_refs[i][...], g_refs[i][...], m_refs[i][...], v_refs[i][...])
            outs[i][...] = delta
            outs[n + i][...] = m2
            outs[2 * n + i][...] = v2

    vm = pl.BlockSpec(memory_space=pltpu.VMEM)
    return pl.pallas_call(
        body, name="adamw_small",
        in_specs=[vm] * (4 * n), out_specs=[vm] * (3 * n),
        out_shape=[_sds(a.shape, jnp.float32) for a in ws] * 3,
    )(*gs, *ws, *ms, *vs)


def _f_chunk(f):
    for cand in (1408, 1024, 512, 256, 128):
        if f % cand == 0:
            return cand
    return f


def _ffn_fwd(x, gain, wg, wu, wd, sides=(), target=None):
    t, d = x.shape
    f = wg.shape[0]
    tb = min(_TB, t)
    fc = _f_chunk(f)
    with_loss = target is not None

    def body(*refs):
        x_ref, g_ref, wg_ref, wu_ref, wd_ref = refs[:5]
        h_ref, s_ref, fa_ref, fu_ref = refs[-4:]
        xv = x_ref[...]
        hb = _mx(xv * _rsq(_sumsq(xv), d) * g_ref[...])
        h_ref[...] = hb
        y = jnp.zeros((tb, d), jnp.float32)
        for c0 in range(0, f, fc):
            a = _dot_nt(hb, wg_ref[c0:c0 + fc, :])
            u = _dot_nt(hb, wu_ref[c0:c0 + fc, :])
            sig = jax.nn.sigmoid(a)
            silu = a * sig
            s = _mx(silu * u)
            s_ref[:, c0:c0 + fc] = s
            fa_ref[:, c0:c0 + fc] = _mx(u * (sig * (1.0 + a * (1.0 - sig))))
            fu_ref[:, c0:c0 + fc] = _mx(silu)
            y = y + _dot(s, wd_ref[c0:c0 + fc, :])
        out = xv + 0.5 * y
        if not with_loss:
            refs[5][...] = out
            return
        t_ref, dy_ref, acc_ref = refs[5:8]

        @pl.when(pl.program_id(0) == 0)
        def _():
            acc_ref[...] = jnp.zeros_like(acc_ref)

        err = out - t_ref[...]
        dy_ref[...] = err * (1.0 / d)
        acc_ref[...] += jnp.sum(err * err)

    kept_specs = [_rows(tb, d), _rows(tb, f), _rows(tb, f), _rows(tb, f)]
    kept_shapes = [_sds((t, d), _MXU), _sds((t, f), _MXU), _sds((t, f), _MXU), _sds((t, f), _MXU)]
    weights = [_const((1, d)), _const((f, d)), _const((f, d)), _const((f, d))]
    if with_loss:
        return _call(
            body, (x, gain, wg, wu, wd, target), sides, name="ffn_fwd_loss", grid=(t // tb,),
            in_specs=[_rows(tb, d)] + weights + [_rows(tb, d)],
            out_specs=[_rows(tb, d), _acc((8, LANES))] + kept_specs,
            out_shape=[_sds((t, d), jnp.float32), _sds((8, LANES), jnp.float32)] + kept_shapes,
            semantics=("arbitrary",))
    return _call(
        body, (x, gain, wg, wu, wd), sides, name="ffn_fwd", grid=(t // tb,),
        in_specs=[_rows(tb, d)] + weights, out_specs=[_rows(tb, d)] + kept_specs,
        out_shape=[_sds((t, d), jnp.float32)] + kept_shapes,
        semantics=("parallel",))


def _ffn_dgrad(x, gain, dxo, fa, fu, wg, wu, wd, sides=()):
    t, d = x.shape
    f = wg.shape[0]
    tb = min(_TB, t)
    fc = _f_chunk(f)

    def body(x_ref, g_ref, dxo_ref, fa_ref, fu_ref, wg_ref, wu_ref, wd_ref, dxi_ref, da_ref, du_ref, dy_ref, dg_ref):
        xv = x_ref[...]
        gv = g_ref[...]
        r = _rsq(_sumsq(xv), d)
        xhat = xv * r
        dxo = dxo_ref[...]
        dyb = _mx(0.5 * dxo)
        dy_ref[...] = dyb
        dh = jnp.zeros((tb, d), jnp.float32)
        for c0 in range(0, f, fc):
            ds = _dot_nt(dyb, wd_ref[c0:c0 + fc, :])
            da = _mx(ds * _f32(fa_ref[:, c0:c0 + fc]))
            du = _mx(ds * _f32(fu_ref[:, c0:c0 + fc]))
            da_ref[:, c0:c0 + fc] = da
            du_ref[:, c0:c0 + fc] = du
            dh = dh + _dot(da, wg_ref[c0:c0 + fc, :]) + _dot(du, wu_ref[c0:c0 + fc, :])

        @pl.when(pl.program_id(0) == 0)
        def _():
            dg_ref[...] = jnp.zeros_like(dg_ref)

        dg_ref[...] += _colsum(dh * xhat)
        dn = dh * gv
        dxi_ref[...] = dxo + r * (dn - xhat * (_rowsum(dn * xhat) * (1.0 / d)))

    return _call(
        body, (x, gain, dxo, fa, fu, wg, wu, wd), sides, name="ffn_dgrad", grid=(t // tb,),
        in_specs=[_rows(tb, d), _const((1, d)), _rows(tb, d), _rows(tb, f), _rows(tb, f),
                  _const((f, d)), _const((f, d)), _const((f, d))],
        out_specs=[_rows(tb, d), _rows(tb, f), _rows(tb, f), _rows(tb, d), _acc((1, d))],
        out_shape=[_sds((t, d), jnp.float32), _sds((t, f), _MXU), _sds((t, f), _MXU), _sds((t, d), _MXU),
                   _sds((1, d), jnp.float32)],
        semantics=("arbitrary",))


def _tn_matmul(a, b, name, sides=()):
    t, m = a.shape
    n = b.shape[1]
    tk = min(_TK, t)
    tn = n
    while m * tn * 4 > 12 * 1024 * 1024 and tn % 256 == 0:
        tn //= 2

    def body(a_ref, b_ref, o_ref):
        @pl.when(pl.program_id(1) == 0)
        def _():
            o_ref[...] = jnp.zeros_like(o_ref)

        o_ref[...] += _dot_tn(a_ref[...], b_ref[...])

    (out,), side_outs = _call(
        body, (a, b), sides, name=name, grid=(n // tn, t // tk),
        in_specs=[pl.BlockSpec((tk, m), lambda j, k: (k, 0)), pl.BlockSpec((tk, tn), lambda j, k: (k, j))],
        out_specs=[pl.BlockSpec((m, tn), lambda j, k: (0, j))],
        out_shape=[_sds((m, n), jnp.float32)],
        semantics=("parallel", "arbitrary"))
    return out, side_outs


PREP_WEIGHTS = ("mix_g", "w_in", "g_qa", "wqb", "g_kva", "w_kvb", "gq_n", "gq_r", "gk_n", "gk_r", "g_sq", "g_sk")
C_CQ, C_CKV, C_KPE, C_QS = 0, MLA_Q_RANK, MLA_Q_RANK + MLA_KV_RANK, MLA_Q_RANK + MLA_KV_RANK + LANES
C_KS = C_QS + SWA_HEADS * SWA_D
C_VS = C_KS + LANES
W_IN_PACKED = C_VS + LANES


def _prep_specs(p):
    return [_const(p[n].shape) for n in PREP_WEIGHTS]


def _pair_norm_rope(t, gain, cos, sin_s):
    return _rope(t * _rsq(_half_sums(t * t), SWA_D) * gain, cos, sin_s)


def _prep_fwd(x, cos, sin_s, p):
    t, d = x.shape
    tb = min(_TB_MIX, t)

    def body(x_ref, cos_ref, sin_ref, mix_g, w_in, g_qa, wqb, g_kva, w_kvb, gq_n, gq_r, gk_n, gk_r, g_sq, g_sk,
             qa_ref, ka_ref, va_ref, qb_ref, kb_ref, vb_ref):
        xv = x_ref[...]
        cos_v, sin_v = cos_ref[...], sin_ref[...]
        hb = _mx(xv * _rsq(_sumsq(xv), d) * mix_g[...])
        proj = _dot_nt(hb, w_in[...])
        cq = proj[:, C_CQ:C_CKV]
        cqn = _mx(cq * _rsq(_sumsq(cq), MLA_Q_RANK) * g_qa[...])
        for h in range(MLA_HEADS):
            qh = _dot_nt(cqn, wqb[h])
            qn, qr = qh[:, :MLA_NOPE], qh[:, MLA_NOPE:]
            rh = _rsq(_sumsq(qn) + _sumsq(qr), MLA_QK)
            qa_ref[h, :, 0:MLA_NOPE] = (qn * rh * gq_n[...]).astype(qa_ref.dtype)
            qa_ref[h, :, MLA_NOPE:MLA_QK_PAD] = _rope(qr * rh * gq_r[...], cos_v, sin_v).astype(qa_ref.dtype)
        ckv = proj[:, C_CKV:C_KPE]
        ckvn = _mx(ckv * _rsq(_sumsq(ckv), MLA_KV_RANK) * g_kva[...])
        kpe = proj[:, C_KPE:C_QS]
        ss_pe = _sumsq(kpe)
        kv = _dot_nt(ckvn, w_kvb[...])
        for h in range(MLA_HEADS):
            c0 = h * (MLA_NOPE + MLA_V)
            kn = kv[:, c0:c0 + MLA_NOPE]
            rh = _rsq(_sumsq(kn) + ss_pe, MLA_QK)
            ka_ref[h, :, 0:MLA_NOPE] = (kn * rh * gk_n[...]).astype(ka_ref.dtype)
            ka_ref[h, :, MLA_NOPE:MLA_QK_PAD] = _rope(kpe * rh * gk_r[...], cos_v, sin_v).astype(ka_ref.dtype)
            va_ref[h] = kv[:, c0 + MLA_NOPE:c0 + MLA_NOPE + MLA_V].astype(va_ref.dtype)
        for j in range(SWA_PAIRS):
            c0 = C_QS + j * LANES
            qb_ref[j] = _pair_norm_rope(proj[:, c0:c0 + LANES], g_sq[...], cos_v, sin_v).astype(qb_ref.dtype)
        k0, k1 = _dup_halves(_pair_norm_rope(proj[:, C_KS:C_VS], g_sk[...], cos_v, sin_v))
        kb_ref[0] = k0.astype(kb_ref.dtype)
        kb_ref[1] = k1.astype(kb_ref.dtype)
        v0, v1 = _dup_halves(proj[:, C_VS:W_IN_PACKED])
        vb_ref[0] = v0.astype(vb_ref.dtype)
        vb_ref[1] = v1.astype(vb_ref.dtype)

    return pl.pallas_call(
        body, name="prep_fwd", grid=(t // tb,),
        in_specs=[_rows(tb, d), _rows(tb, LANES), _rows(tb, LANES)] + _prep_specs(p),
        out_specs=[_heads_rows(MLA_HEADS, tb, MLA_QK_PAD), _heads_rows(MLA_HEADS, tb, MLA_QK_PAD),
                   _heads_rows(MLA_HEADS, tb, MLA_V), _heads_rows(SWA_PAIRS, tb, LANES),
                   _heads_rows(SWA_KV, tb, LANES), _heads_rows(SWA_KV, tb, LANES)],
        out_shape=[_sds((MLA_HEADS, t, MLA_QK_PAD), _MXU), _sds((MLA_HEADS, t, MLA_QK_PAD), _MXU),
                   _sds((MLA_HEADS, t, MLA_V), _MXU), _sds((SWA_PAIRS, t, LANES), _MXU),
                   _sds((SWA_KV, t, LANES), _MXU), _sds((SWA_KV, t, LANES), _MXU)],
        compiler_params=_cparams(("parallel",)),
    )(x, cos, sin_s, *[p[n] for n in PREP_WEIGHTS])


def _prep_bwd(x, dxin, cos, sin_s, p, dqa, dka, dva, dqb, dkb, dvb, sides=()):
    t, d = x.shape
    tb = min(_TB_MIX, t)
    n_w = len(PREP_WEIGHTS)

    def body(*refs):
        x_ref, dxin_ref, cos_ref, sin_ref = refs[:4]
        mix_g, w_in, g_qa, wqb, g_kva, w_kvb, gq_n, gq_r, gk_n, gk_r, g_sq, g_sk = refs[4:4 + n_w]
        dqa_ref, dka_ref, dva_ref, dqb_ref, dkb_ref, dvb_ref = refs[4 + n_w:10 + n_w]
        dx_ref = refs[10 + n_w]
        grads = dict(zip(PREP_WEIGHTS, refs[11 + n_w:11 + 2 * n_w]))
        dproj_ref, dkv_ref, dqh_ref = refs[11 + 2 * n_w:]

        @pl.when(pl.program_id(0) == 0)
        def _():
            for ref in grads.values():
                ref[...] = jnp.zeros_like(ref)

        xv = x_ref[...]
        cos_v, sin_v = cos_ref[...], sin_ref[...]
        r0 = _rsq(_sumsq(xv), d)
        xhat = xv * r0
        hb = _mx(xhat * mix_g[...])
        proj = _dot_nt(hb, w_in[...])

        cq = proj[:, C_CQ:C_CKV]
        rq = _rsq(_sumsq(cq), MLA_Q_RANK)
        cqh = cq * rq
        cqn = _mx(cqh * g_qa[...])
        dcqn = jnp.zeros((tb, MLA_Q_RANK), jnp.float32)
        for h in range(MLA_HEADS):
            qh = _dot_nt(cqn, wqb[h])
            qn, qr = qh[:, :MLA_NOPE], qh[:, MLA_NOPE:]
            rh = _rsq(_sumsq(qn) + _sumsq(qr), MLA_QK)
            xh_n, xh_r = qn * rh, qr * rh
            dy_n = dqa_ref[h, :, 0:MLA_NOPE]
            dy_r = _rope_bwd(dqa_ref[h, :, MLA_NOPE:MLA_QK_PAD], cos_v, sin_v)
            grads["gq_n"][...] += _colsum(dy_n * xh_n)
            grads["gq_r"][...] += _colsum(dy_r * xh_r)
            dqn, dqr = _norm_bwd([dy_n * gq_n[...], dy_r * gq_r[...]], [xh_n, xh_r], rh, MLA_QK)
            dqh_ref[:, 0:MLA_NOPE] = _mx(dqn)
            dqh_ref[:, MLA_NOPE:MLA_QK_PAD] = _mx(dqr)
            dqh = dqh_ref[...]
            grads["wqb"][h] += _dot_tn(dqh, cqn)
            dcqn = dcqn + _dot(dqh, wqb[h])
        grads["g_qa"][...] += _colsum(dcqn * cqh)
        (dcq,) = _norm_bwd([dcqn * g_qa[...]], [cqh], rq, MLA_Q_RANK)
        dproj_ref[:, C_CQ:C_CKV] = _mx(dcq)

        ckv = proj[:, C_CKV:C_KPE]
        rkv = _rsq(_sumsq(ckv), MLA_KV_RANK)
        ckvh = ckv * rkv
        ckvn = _mx(ckvh * g_kva[...])
        kpe = proj[:, C_KPE:C_QS]
        ss_pe = _sumsq(kpe)
        kv = _dot_nt(ckvn, w_kvb[...])
        dkpe = jnp.zeros((tb, LANES), jnp.float32)
        for h in range(MLA_HEADS):
            c0 = h * (MLA_NOPE + MLA_V)
            c1 = c0 + MLA_NOPE
            kn = kv[:, c0:c1]
            rh = _rsq(_sumsq(kn) + ss_pe, MLA_QK)
            xh_n, xh_r = kn * rh, kpe * rh
            dy_n = dka_ref[h, :, 0:MLA_NOPE]
            dy_r = _rope_bwd(dka_ref[h, :, MLA_NOPE:MLA_QK_PAD], cos_v, sin_v)
            grads["gk_n"][...] += _colsum(dy_n * xh_n)
            grads["gk_r"][...] += _colsum(dy_r * xh_r)
            dkn, dkr = _norm_bwd([dy_n * gk_n[...], dy_r * gk_r[...]], [xh_n, xh_r], rh, MLA_QK)
            dkpe = dkpe + dkr
            dkv_ref[:, c0:c1] = _mx(dkn)
            dkv_ref[:, c1:c1 + MLA_V] = _mx(dva_ref[h])
        dkv = dkv_ref[...]
        grads["w_kvb"][...] += _dot_tn(dkv, ckvn)
        dckvn = _dot(dkv, w_kvb[...])
        grads["g_kva"][...] += _colsum(dckvn * ckvh)
        (dckv,) = _norm_bwd([dckvn * g_kva[...]], [ckvh], rkv, MLA_KV_RANK)
        dproj_ref[:, C_CKV:C_KPE] = _mx(dckv)
        dproj_ref[:, C_KPE:C_QS] = _mx(dkpe)

        def pair_bwd(tv, dy, g_ref, gname):
            r = _rsq(_half_sums(tv * tv), SWA_D)
            xh = tv * r
            dpre = _rope_bwd(dy, cos_v, sin_v)
            grads[gname][...] += _colsum(dpre * xh)
            dn = dpre * g_ref[...]
            return r * (dn - xh * (_half_sums(dn * xh) * (1.0 / SWA_D)))

        for j in range(SWA_PAIRS):
            c0 = C_QS + j * LANES
            dproj_ref[:, c0:c0 + LANES] = _mx(pair_bwd(proj[:, c0:c0 + LANES], dqb_ref[j], g_sq, "g_sq"))
        dproj_ref[:, C_KS:C_VS] = _mx(pair_bwd(proj[:, C_KS:C_VS], _undup_halves(dkb_ref[0], dkb_ref[1]), g_sk, "g_sk"))
        dproj_ref[:, C_VS:W_IN_PACKED] = _mx(_undup_halves(dvb_ref[0], dvb_ref[1]))

        dproj = dproj_ref[...]
        grads["w_in"][...] += _dot_tn(dproj, hb)
        dh = _dot(dproj, w_in[...])
        grads["mix_g"][...] += _colsum(dh * xhat)
        (dxv,) = _norm_bwd([dh * mix_g[...]], [xhat], r0, d)
        dx_ref[...] = dxin_ref[...] + dxv

    grad_shapes = [p[n].shape for n in PREP_WEIGHTS]
    args = (x, dxin, cos, sin_s, *[p[n] for n in PREP_WEIGHTS], dqa, dka, dva, dqb, dkb, dvb)
    return _call(
        body, args, sides, name="prep_bwd", grid=(t // tb,),
        in_specs=[_rows(tb, d), _rows(tb, d), _rows(tb, LANES), _rows(tb, LANES)] + _prep_specs(p) + [
            _heads_rows(MLA_HEADS, tb, MLA_QK_PAD), _heads_rows(MLA_HEADS, tb, MLA_QK_PAD),
            _heads_rows(MLA_HEADS, tb, MLA_V), _heads_rows(SWA_PAIRS, tb, LANES),
            _heads_rows(SWA_KV, tb, LANES), _heads_rows(SWA_KV, tb, LANES)],
        out_specs=[_rows(tb, d)] + [_acc(s) for s in grad_shapes],
        out_shape=[_sds((t, d), jnp.float32)] + [_sds(s, jnp.float32) for s in grad_shapes],
        scratch_shapes=[pltpu.VMEM((tb, W_IN_PACKED), _MXU), pltpu.VMEM((tb, MLA_HEADS * (MLA_NOPE + MLA_V)), _MXU),
                        pltpu.VMEM((tb, MLA_QK_PAD), _MXU)],
        semantics=("arbitrary",))


def _strips(n):
    step = min(_STRIP, n)
    return [slice(r, r + step) for r in range(0, n, step)]


def _mla_fwd(q, k, v, sides=()):
    hn, t, dq = q.shape
    dv = v.shape[2]
    bq = min(_BQ, t)
    scale = MLA_QK ** -0.5
    scale2 = scale * LOG2_E

    def body(q_ref, k_ref, v_ref, o_ref, l_ref):
        i = pl.program_id(1)
        qv = q_ref[0]

        def step(first_block, width, carry, masked):
            m, l, acc = carry
            start = pl.multiple_of(first_block * bq, bq)
            s = _dot_nt(qv, k_ref[0, pl.ds(start, width), :])
            if masked:
                row = lax.broadcasted_iota(jnp.int32, (bq, width), 0)
                col = lax.broadcasted_iota(jnp.int32, (bq, width), 1)
                s = jnp.where(col <= row, s, NEG)
            m_new = jnp.maximum(m, _rowmax(s))
            alpha = jnp.exp2((m - m_new) * scale2)
            pv = jnp.exp2((s - m_new) * scale2)
            l = alpha * l + _rowsum(pv)
            acc = alpha * acc + _dot(_mx(pv), v_ref[0, pl.ds(start, width), :])
            return m_new, l, acc

        init = (jnp.full((bq, 1), NEG, jnp.float32), jnp.zeros((bq, 1), jnp.float32), jnp.zeros((bq, dv), jnp.float32))
        carry, done = init, 0
        for group in (4, 2, 1):
            count = (i - done) // group
            carry = lax.fori_loop(0, count, lambda g, c, done=done, group=group: step(done + group * g, group * bq, c, False), carry)
            done = done + group * count
        m, l, acc = step(i, bq, carry, True)
        o_ref[0] = acc / l
        l_ref[0, 0] = _as_row(m * scale + jnp.log(l))

    return _call(
        body, (q, k, v), sides, name="mla_fwd", grid=(hn, t // bq),
        in_specs=[pl.BlockSpec((1, bq, dq), lambda h, i: (h, i, 0)),
                  pl.BlockSpec((1, t, dq), lambda h, i: (h, 0, 0)),
                  pl.BlockSpec((1, t, dv), lambda h, i: (h, 0, 0))],
        out_specs=[pl.BlockSpec((1, bq, dv), lambda h, i: (h, i, 0)),
                   pl.BlockSpec((1, 1, 1, bq), lambda h, i: (h, i, 0, 0))],
        out_shape=[_sds((hn, t, dv), jnp.float32), _sds((hn, t // bq, 1, bq), jnp.float32)],
        semantics=("parallel", "arbitrary"))


def _mla_bwd(q, k, v, do, lse_rows, dsum_rows, sides=()):
    hn, t, dq_w = q.shape
    dv_w = v.shape[2]
    bq = min(_BQ, t)
    nb = t // bq
    wide = max(_BWD_GROUPS) * bq
    scale = MLA_QK ** -0.5

    def body(q_ref, do_ref, l_ref, d_ref, k_ref, v_ref, dq_ref, dk_ref, dv_ref, st_scr, dpt_scr, p_scr, ds_scr):
        j = pl.program_id(1)

        @pl.when(j == 0)
        def _():
            dq_ref[...] = jnp.zeros_like(dq_ref)

        kv = k_ref[0]
        vv = v_ref[0]
        dk_ref[0] = jnp.zeros((bq, dq_w), jnp.float32)
        dv_ref[0] = jnp.zeros((bq, dv_w), jnp.float32)

        def tile(first_block, n_blk, masked):
            width = n_blk * bq
            start = pl.multiple_of(first_block * bq, bq)
            qv = q_ref[0, pl.ds(start, width), :]
            dov = do_ref[0, pl.ds(start, width), :]
            st_scr[:, :width] = _dot_nt(kv, qv)
            dpt_scr[:, :width] = _dot_nt(vv, dov)
            lse2 = jnp.concatenate([l_ref[0, first_block + b] for b in range(n_blk)], axis=1) * LOG2_E
            dsum = jnp.concatenate([d_ref[0, first_block + b] for b in range(n_blk)], axis=1)
            for rows in _strips(bq):
                pt = jnp.exp2(st_scr[rows, :width] * (scale * LOG2_E) - lse2)
                if masked:
                    n_rows = rows.stop - rows.start
                    row = lax.broadcasted_iota(jnp.int32, (n_rows, width), 0) + rows.start
                    col = lax.broadcasted_iota(jnp.int32, (n_rows, width), 1)
                    pt = jnp.where(row <= col, pt, 0.0)
                p_scr[rows, :width] = _mx(pt)
                ds_scr[rows, :width] = _mx(pt * (dpt_scr[rows, :width] - dsum) * scale)
            ds_t = ds_scr[:, :width]
            dv_ref[0] += _dot(p_scr[:, :width], dov)
            dk_ref[0] += _dot(ds_t, qv)
            dq_ref[0, pl.ds(start, width), :] += _dot_tn(ds_t, kv)

        def group_loop(first_block, n_blk, count):
            def loop_body(g, carry):
                tile(first_block + n_blk * g, n_blk, False)
                return carry

            lax.fori_loop(0, count, loop_body, 0)

        tile(j, 1, True)
        done = j + 1
        for n_blk in _BWD_GROUPS:
            count = (nb - done) // n_blk
            group_loop(done, n_blk, count)
            done = done + n_blk * count

    return _call(
        body, (q, do, lse_rows, dsum_rows, k, v), sides, name="mla_bwd", grid=(hn, nb),
        in_specs=[pl.BlockSpec((1, t, dq_w), lambda h, j: (h, 0, 0)),
                  pl.BlockSpec((1, t, dv_w), lambda h, j: (h, 0, 0)),
                  pl.BlockSpec((1, nb, 1, bq), lambda h, j: (h, 0, 0, 0)),
                  pl.BlockSpec((1, nb, 1, bq), lambda h, j: (h, 0, 0, 0)),
                  pl.BlockSpec((1, bq, dq_w), lambda h, j: (h, j, 0)),
                  pl.BlockSpec((1, bq, dv_w), lambda h, j: (h, j, 0))],
        out_specs=[pl.BlockSpec((1, t, dq_w), lambda h, j: (h, 0, 0)),
                   pl.BlockSpec((1, bq, dq_w), lambda h, j: (h, j, 0)),
                   pl.BlockSpec((1, bq, dv_w), lambda h, j: (h, j, 0))],
        out_shape=[_sds((hn, t, dq_w), jnp.float32), _sds((hn, t, dq_w), jnp.float32), _sds((hn, t, dv_w), jnp.float32)],
        scratch_shapes=[pltpu.VMEM((bq, wide), jnp.float32), pltpu.VMEM((bq, wide), jnp.float32),
                        pltpu.VMEM((bq, wide), _MXU), pltpu.VMEM((bq, wide), _MXU)],
        semantics=("parallel", "arbitrary"))


STACK = SWA_GROUP * SWA_BLOCK


def _swa_stack(ref, c, rows):
    low = _low_half()
    parts = []
    for g in range(SWA_GROUP):
        tv = ref[SWA_GROUP // 2 * c + g // 2, rows, :]
        keep = low if g % 2 == 0 else jnp.logical_not(low)
        parts.append(_mx(jnp.where(keep, tv, jnp.zeros_like(tv))))
    return jnp.concatenate(parts, axis=0)


def _swa_cols(ref, c, rows):
    return jnp.concatenate([ref[SWA_GROUP * c + g, rows, 0:1] for g in range(SWA_GROUP)], axis=0)


def _swa_sink_col(s_ref, c):
    return jnp.concatenate([jnp.broadcast_to(s_ref[SWA_GROUP * c + g][:, 0:1], (SWA_BLOCK, 1))
                            for g in range(SWA_GROUP)], axis=0)


def _swa_band_masks():
    row = lax.broadcasted_iota(jnp.int32, (STACK, SWA_BLOCK), 0) & (SWA_BLOCK - 1)
    col = lax.broadcasted_iota(jnp.int32, (STACK, SWA_BLOCK), 1)
    return col <= row, col > row


def _swa_band(has_previous):
    row = lax.broadcasted_iota(jnp.int32, (STACK, 2 * SWA_BLOCK), 0) & (SWA_BLOCK - 1)
    col = lax.broadcasted_iota(jnp.int32, (STACK, 2 * SWA_BLOCK), 1)
    before = jnp.logical_and(col < SWA_BLOCK, col > row)
    if has_previous is not True:
        before = jnp.logical_and(before, has_previous)
    return jnp.logical_or(before, jnp.logical_and(col >= SWA_BLOCK, col - SWA_BLOCK <= row))


def _swa_keys(ref, prev_ref, c, b):
    if b == 0:
        return jnp.concatenate([prev_ref[c], ref[c, 0:SWA_BLOCK, :]], axis=0)
    return ref[c, (b - 1) * SWA_BLOCK:(b + 1) * SWA_BLOCK, :]


def _swa_unstack_pairs(ref, c, rows, stacked):
    for pr in range(SWA_GROUP // 2):
        r0 = 2 * pr * SWA_BLOCK
        ref[SWA_GROUP // 2 * c + pr, rows, :] = _pick_halves(stacked[r0:r0 + SWA_BLOCK], stacked[r0 + SWA_BLOCK:r0 + 2 * SWA_BLOCK])


def _swa_blocks(t):
    nblk = t // SWA_BLOCK
    bps = min(_SWA_STEP, nblk)
    return nblk, bps, bps * SWA_BLOCK


def _swa_fwd(q, k, v, sinks):
    _, t, _ = q.shape
    nblk, bps, sb = _swa_blocks(t)
    scale = SWA_D ** -0.5

    def body(q_ref, k_ref, kp_ref, v_ref, vp_ref, s_ref, o_ref, l_ref):
        n = pl.program_id(0)
        band_first, band = _swa_band(n > 0), _swa_band(True)
        for c in range(SWA_KV):
            sink = _swa_sink_col(s_ref, c)
            for b in range(bps):
                rows = slice(b * SWA_BLOCK, (b + 1) * SWA_BLOCK)
                qs = _swa_stack(q_ref, c, rows)
                s = jnp.where(band_first if b == 0 else band, _dot_nt(qs, _swa_keys(k_ref, kp_ref, c, b)) * scale, NEG)
                m = jnp.maximum(_rowmax(s), sink)
                e = jnp.exp(s - m)
                denom = _rowsum(e) + jnp.exp(sink - m)
                o = _dot(_mx(e * (1.0 / denom)), _swa_keys(v_ref, vp_ref, c, b))
                lse = m + jnp.log(denom)
                for g in range(SWA_GROUP):
                    l_ref[SWA_GROUP * c + g, rows, :] = jnp.broadcast_to(
                        lse[g * SWA_BLOCK:(g + 1) * SWA_BLOCK], (SWA_BLOCK, LANES))
                _swa_unstack_pairs(o_ref, c, rows, o)

    main = lambda n: (0, n, 0)
    prev = lambda n: (0, jnp.maximum(n * bps - 1, 0), 0)
    return pl.pallas_call(
        body, name="swa_fwd", grid=(nblk // bps,),
        in_specs=[pl.BlockSpec((SWA_PAIRS, sb, LANES), main),
                  pl.BlockSpec((SWA_KV, sb, LANES), main), pl.BlockSpec((SWA_KV, SWA_BLOCK, LANES), prev),
                  pl.BlockSpec((SWA_KV, sb, LANES), main), pl.BlockSpec((SWA_KV, SWA_BLOCK, LANES), prev),
                  _const((SWA_HEADS, 1, LANES))],
        out_specs=[pl.BlockSpec((SWA_PAIRS, sb, LANES), main), pl.BlockSpec((SWA_HEADS, sb, LANES), main)],
        out_shape=[_sds((SWA_PAIRS, t, LANES), jnp.float32), _sds((SWA_HEADS, t, LANES), jnp.float32)],
        compiler_params=_cparams(("parallel",)),
    )(q, k, k, v, v, sinks)


def _swa_bwd(q, k, v, sinks, do, lse, dsum):
    _, t, _ = q.shape
    nblk, bps, sb = _swa_blocks(t)
    steps = nblk // bps
    scale = SWA_D ** -0.5

    def body(q_ref, k_ref, kp_ref, v_ref, vp_ref, s_ref, do_ref, l_ref, d_ref, qn_ref, don_ref, ln_ref, dn_ref,
             dq_ref, dk_ref, dv_ref, ds_ref):
        n = pl.program_id(0)

        @pl.when(n == 0)
        def _():
            ds_ref[...] = jnp.zeros_like(ds_ref)

        _, m_prev = _swa_band_masks()
        band_first, band = _swa_band(n > 0), _swa_band(True)
        everything = slice(0, SWA_BLOCK)

        def probs(qs, keys, mask, lcol):
            return jnp.where(mask, jnp.exp(_dot_nt(qs, keys) * scale - lcol), 0.0)

        def dscores(pm, dos, vals, dcol):
            return _mx(pm * (_dot_nt(dos, vals) - dcol) * scale)

        for c in range(SWA_KV):
            sink = _swa_sink_col(s_ref, c)
            dk_acc = [jnp.zeros((SWA_BLOCK, LANES), jnp.float32) for _ in range(bps)]
            dv_acc = [jnp.zeros((SWA_BLOCK, LANES), jnp.float32) for _ in range(bps)]
            for b in range(bps):
                rows = slice(b * SWA_BLOCK, (b + 1) * SWA_BLOCK)
                keys, vals = _swa_keys(k_ref, kp_ref, c, b), _swa_keys(v_ref, vp_ref, c, b)
                qs = _swa_stack(q_ref, c, rows)
                dos = _swa_stack(do_ref, c, rows)
                lcol = _swa_cols(l_ref, c, rows)
                dcol = _swa_cols(d_ref, c, rows)
                pm = probs(qs, keys, band_first if b == 0 else band, lcol)
                ds = dscores(pm, dos, vals, dcol)
                _swa_unstack_pairs(dq_ref, c, rows, _dot(ds, keys))
                dk_both = _dot_tn(ds, qs)
                dv_both = _dot_tn(_mx(pm), dos)
                dk_acc[b] = dk_acc[b] + dk_both[SWA_BLOCK:]
                dv_acc[b] = dv_acc[b] + dv_both[SWA_BLOCK:]
                if b > 0:
                    dk_acc[b - 1] = dk_acc[b - 1] + dk_both[:SWA_BLOCK]
                    dv_acc[b - 1] = dv_acc[b - 1] + dv_both[:SWA_BLOCK]
                p_sink = jnp.exp(sink - lcol) * dcol
                for g in range(SWA_GROUP):
                    ds_ref[SWA_GROUP * c + g] += -jnp.sum(p_sink[g * SWA_BLOCK:(g + 1) * SWA_BLOCK])
            tail = slice((bps - 1) * SWA_BLOCK, bps * SWA_BLOCK)
            kc, vc = k_ref[c, tail, :], v_ref[c, tail, :]
            qs = _swa_stack(qn_ref, c, everything)
            dos = _swa_stack(don_ref, c, everything)
            lcol = _swa_cols(ln_ref, c, everything)
            dcol = _swa_cols(dn_ref, c, everything)
            p_p = probs(qs, kc, jnp.logical_and(m_prev, n < steps - 1), lcol)
            ds_p = dscores(p_p, dos, vc, dcol)
            dk_acc[bps - 1] = dk_acc[bps - 1] + _dot_tn(ds_p, qs)
            dv_acc[bps - 1] = dv_acc[bps - 1] + _dot_tn(_mx(p_p), dos)
            for b in range(bps):
                rows = slice(b * SWA_BLOCK, (b + 1) * SWA_BLOCK)
                dk_ref[c, rows, :] = dk_acc[b]
                dv_ref[c, rows, :] = dv_acc[b]

    main = lambda n: (0, n, 0)
    prev = lambda n: (0, jnp.maximum(n * bps - 1, 0), 0)
    nxt = lambda n: (0, jnp.minimum((n + 1) * bps, nblk - 1), 0)
    pairs = pl.BlockSpec((SWA_PAIRS, sb, LANES), main)
    kvs = pl.BlockSpec((SWA_KV, sb, LANES), main)
    kv_prev = pl.BlockSpec((SWA_KV, SWA_BLOCK, LANES), prev)
    stats = pl.BlockSpec((SWA_HEADS, sb, LANES), main)
    pairs_next = pl.BlockSpec((SWA_PAIRS, SWA_BLOCK, LANES), nxt)
    stats_next = pl.BlockSpec((SWA_HEADS, SWA_BLOCK, LANES), nxt)
    return pl.pallas_call(
        body, name="swa_bwd", grid=(steps,),
        in_specs=[pairs, kvs, kv_prev, kvs, kv_prev, _const((SWA_HEADS, 1, LANES)), pairs, stats, stats,
                  pairs_next, pairs_next, stats_next, stats_next],
        out_specs=[pairs, kvs, kvs, _acc((SWA_HEADS, 1, LANES))],
        out_shape=[_sds((SWA_PAIRS, t, LANES), jnp.float32), _sds((SWA_KV, t, LANES), jnp.float32),
                   _sds((SWA_KV, t, LANES), jnp.float32), _sds((SWA_HEADS, 1, LANES), jnp.float32)],
        compiler_params=_cparams(("arbitrary",)),
    )(q, k, k, v, v, sinks, do, lse, dsum, q, do, lse, dsum)


MIX_SLABS = 4
MIX_WIDTH = MIX_SLABS * LANES


def _mix_out_fwd(x, oa, ob, ga, gb, wo_a, wo_b):
    t, d = x.shape
    tb = min(_TB_MIX, t)

    def body(x_ref, oa_ref, ob_ref, ga_ref, gb_ref, woa_ref, wob_ref, xo_ref):
        y = x_ref[...]
        for o_ref, g_ref, w_ref in ((oa_ref, ga_ref, woa_ref), (ob_ref, gb_ref, wob_ref)):
            r = _rsq(sum(_sumsq(o_ref[h]) for h in range(MIX_SLABS)), MIX_WIDTH)
            for h in range(MIX_SLABS):
                y = y + _dot(_mx(o_ref[h] * r * g_ref[h]), w_ref[h])
        xo_ref[...] = y

    slab = _heads_rows(MIX_SLABS, tb, LANES)
    return pl.pallas_call(
        body, name="mix_out_fwd", grid=(t // tb,),
        in_specs=[_rows(tb, d), slab, slab, _const(ga.shape), _const(gb.shape), _const(wo_a.shape), _const(wo_b.shape)],
        out_specs=_rows(tb, d),
        out_shape=_sds((t, d), jnp.float32),
        compiler_params=_cparams(("parallel",)),
    )(x, oa, ob, ga, gb, wo_a, wo_b)


def _mix_out_bwd(dx, oa, ob, ga, gb, wo_a, wo_b):
    t, d = dx.shape
    tb = min(_TB_MIX, t)

    def group(o_ref, g_ref, w_ref, dyb, do_ref, n_ref, col0, dg_ref):
        r = _rsq(sum(_sumsq(o_ref[h]) for h in range(MIX_SLABS)), MIX_WIDTH)
        xh, dn = [], []
        for h in range(MIX_SLABS):
            xh.append(o_ref[h] * r)
            n_ref[:, col0 + h * LANES:col0 + (h + 1) * LANES] = _mx(xh[h] * g_ref[h])
            dm = _dot_nt(dyb, w_ref[h])
            dg_ref[h] += _colsum(dm * xh[h])
            dn.append(dm * g_ref[h])
        c = sum(_rowsum(dn[h] * xh[h]) for h in range(MIX_SLABS)) * (1.0 / MIX_WIDTH)
        prods = []
        for h in range(MIX_SLABS):
            do = r * (dn[h] - xh[h] * c)
            do_ref[h] = do.astype(do_ref.dtype)
            prods.append(do * o_ref[h])
        return prods

    def body(dx_ref, oa_ref, ob_ref, ga_ref, gb_ref, woa_ref, wob_ref,
             doa_ref, dsa_ref, dob_ref, dsb_ref, n_ref, dy_ref, dga_ref, dgb_ref):
        @pl.when(pl.program_id(0) == 0)
        def _():
            dga_ref[...] = jnp.zeros_like(dga_ref)
            dgb_ref[...] = jnp.zeros_like(dgb_ref)

        dyb = _mx(dx_ref[...])
        dy_ref[...] = dyb
        for h, pr in enumerate(group(oa_ref, ga_ref, woa_ref, dyb, doa_ref, n_ref, 0, dga_ref)):
            dsa_ref[h, 0] = _as_row(_rowsum(pr))
        low = _low_half()
        for j, pr in enumerate(group(ob_ref, gb_ref, wob_ref, dyb, dob_ref, n_ref, MIX_WIDTH, dgb_ref)):
            dsb_ref[2 * j] = jnp.broadcast_to(_rowsum(jnp.where(low, pr, 0.0)), (tb, LANES))
            dsb_ref[2 * j + 1] = jnp.broadcast_to(_rowsum(jnp.where(low, 0.0, pr)), (tb, LANES))

    slab = _heads_rows(MIX_SLABS, tb, LANES)
    return pl.pallas_call(
        body, name="mix_out_bwd", grid=(t // tb,),
        in_specs=[_rows(tb, d), slab, slab, _const(ga.shape), _const(gb.shape), _const(wo_a.shape), _const(wo_b.shape)],
        out_specs=[slab, pl.BlockSpec((MIX_SLABS, 1, 1, tb), lambda i: (0, i, 0, 0)), slab,
                   _heads_rows(SWA_HEADS, tb, LANES), _rows(tb, 2 * MIX_WIDTH), _rows(tb, d),
                   _acc(ga.shape), _acc(gb.shape)],
        out_shape=[_sds((MIX_SLABS, t, LANES), _MXU), _sds((MIX_SLABS, t // tb, 1, tb), jnp.float32),
                   _sds((MIX_SLABS, t, LANES), jnp.float32), _sds((SWA_HEADS, t, LANES), jnp.float32),
                   _sds((t, 2 * MIX_WIDTH), _MXU), _sds((t, d), _MXU),
                   _sds(ga.shape, jnp.float32), _sds(gb.shape, jnp.float32)],
        compiler_params=_cparams(("arbitrary",)),
    )(dx, oa, ob, ga, gb, wo_a, wo_b)


def _is_transposed(name):
    return name not in ROW_SHARDED


def _pack_layer(shards, l, width, names):
    rows = [(shards[n][l].T if _is_transposed(n) else shards[n][l]).reshape(-1, width) for n in names]
    return [jnp.concatenate(rows, axis=0)] if names is OTHER_BIG else rows


def _full_shape(like, name):
    _, a, b = like[name].shape
    return (N_DEV * b, a) if _is_transposed(name) else (N_DEV * a, b)


def _unpack_full(gathered, like, names):
    if names is not OTHER_BIG:
        return {n: g.reshape(_full_shape(like, n)) for n, g in zip(names, gathered)}
    (gathered,), out, off = gathered, {}, 0
    for n in names:
        rows_n = like[n][0].size // gathered.shape[-1]
        out[n] = gathered[:, off:off + rows_n].reshape(_full_shape(like, n))
        off += rows_n
    return out


def _stored(a, name):
    return jnp.swapaxes(a, 1, 2) if _is_transposed(name) else a


def _grads_by_destination(grads, width, names):
    by_dest = lambda n: grads[n].reshape(N_DEV, -1, width)
    if names is OTHER_BIG:
        return [jnp.concatenate([by_dest(n) for n in names], axis=1)]
    return [by_dest(n) for n in names]


def _shards_from_rows(rows, like):
    out = dict(zip(FFN_BIG, rows[:len(FFN_BIG)]))
    rest, off = rows[len(FFN_BIG)], 0
    for n in OTHER_BIG:
        _, a, b = like[n].shape
        rows_n = a * b // rest.shape[-1]
        out[n] = rest[off:off + rows_n].reshape((b, a) if _is_transposed(n) else (a, b))
        off += rows_n
    return out


def _small_rows(n_elems):
    return -(-n_elems // LANES)


def _pack_small(arrays):
    parts = []
    for n in SMALL:
        v = arrays[n]
        depth, width = v.shape
        padded = _small_rows(width) * LANES
        parts.append(jnp.pad(v, ((0, 0), (0, padded - width))).reshape(-1, LANES))
    packed = jnp.concatenate(parts, axis=0)
    return jnp.pad(packed, ((0, (-packed.shape[0]) % 8), (0, 0)))


def _unpack_small(packed, like):
    out, off = {}, 0
    for n in SMALL:
        depth, width = like[n].shape
        rows_n = _small_rows(width)
        seg = packed[off:off + depth * rows_n].reshape(depth, rows_n * LANES)
        out[n] = seg[:, :width]
        off += depth * rows_n
    return out


def _rope_tables(t):
    pos = jnp.arange(t, dtype=jnp.float32)
    inv = 1.0 / (ROPE_THETA ** (jnp.arange(0, MLA_ROPE, 2, dtype=jnp.float32) / MLA_ROPE))
    ang = pos[:, None] * inv[None, :]
    cos, sin = jnp.cos(ang), jnp.sin(ang)
    return jnp.concatenate([cos, cos, cos, cos], axis=1), jnp.concatenate([-sin, sin, -sin, sin], axis=1)


def _pad_lanes(a, width):
    return jnp.pad(a, [(0, 0)] * (a.ndim - 1) + [(0, width - a.shape[-1])])


def _ffn_params(full, small, l, tag):
    return small[tag + "_norm"][l][None, :], full[tag + "_w_gate"], full[tag + "_w_up"], full[tag + "_w_down"]


def _mixer_params(full, small, l):
    w_in = full["w_in"]
    d = w_in.shape[1]
    mla_rows = W_IN_COLS[0]
    w_in_p = jnp.concatenate([w_in[:mla_rows], jnp.zeros((LANES - MLA_ROPE, d), w_in.dtype), w_in[mla_rows:]], axis=0)
    wqb = full["mla_w_q_b"].reshape(MLA_HEADS, MLA_QK, MLA_Q_RANK)
    wqb = jnp.pad(wqb, ((0, 0), (0, MLA_QK_PAD - MLA_QK), (0, 0)))
    row = lambda name: small[name][l][None, :]
    twice = lambda g: jnp.concatenate([g, g], axis=1)
    prep = {
        "mix_g": row("mix_norm"), "w_in": w_in_p,
        "g_qa": row("mla_q_a_norm"), "wqb": wqb,
        "g_kva": row("mla_kv_a_norm"), "w_kvb": full["mla_w_kv_b"],
        "gq_n": row("mla_q_norm")[:, :MLA_NOPE], "gq_r": _pad_lanes(row("mla_q_norm")[:, MLA_NOPE:], LANES),
        "gk_n": row("mla_k_norm")[:, :MLA_NOPE], "gk_r": _pad_lanes(row("mla_k_norm")[:, MLA_NOPE:], LANES),
        "g_sq": twice(row("swa_q_norm")), "g_sk": twice(row("swa_k_norm")),
    }
    return {
        "prep": prep,
        "sinks": jnp.broadcast_to(small["swa_sinks"][l][:, None, None], (SWA_HEADS, 1, LANES)),
        "ga": small["mla_out_norm"][l].reshape(MIX_SLABS, 1, LANES),
        "gb": small["swa_out_norm"][l].reshape(MIX_SLABS, 1, LANES),
        "wo_a": full["w_o"][:MIX_WIDTH].reshape(MIX_SLABS, LANES, d),
        "wo_b": full["w_o"][MIX_WIDTH:].reshape(MIX_SLABS, LANES, d),
    }


def _ffn_backward(x_in, dxo, kept, params, tag, sides=()):
    gain, wg, wu, wd = params
    h, s, fa, fu = kept
    (dxi, da, du, dy, dg), side_out = _ffn_dgrad(x_in, gain, dxo, fa, fu, wg, wu, wd, sides)
    dwg, _ = _tn_matmul(da, h, "wgrad_" + tag + "_gate")
    dwu, _ = _tn_matmul(du, h, "wgrad_" + tag + "_up")
    dwd, _ = _tn_matmul(s, dy, "wgrad_" + tag + "_down")
    return dxi, dg[0], dwg, dwu, dwd, side_out


def _ffn_backward_exchanging(x_in, dxo, kept, params, tag, ready, chip_sums, width):
    gain, wg, wu, wd = params
    h, s, fa, fu = kept
    (dxi, da, du, dy, dg), (ready_sib,) = _ffn_dgrad(x_in, gain, dxo, fa, fu, wg, wu, wd, [_side_sibling(ready)])
    ready_owns, ready_wires = chip_sums(ready, ready_sib)
    dwd, (ready_recv,) = _tn_matmul(s, dy, "wgrad_" + tag + "_down", [_side_chips(ready_wires)])
    down = [dwd.reshape(N_DEV, -1, width)]
    dwg, (down_sib,) = _tn_matmul(da, h, "wgrad_" + tag + "_gate", [_side_sibling(down)])
    down_owns, down_wires = chip_sums(down, down_sib)
    dwu, (down_recv,) = _tn_matmul(du, h, "wgrad_" + tag + "_up", [_side_chips(down_wires)])
    return dxi, dg[0], dwg, dwu, _rs_sum(ready_owns, ready_recv), _rs_sum(down_owns, down_recv)


def kernel(x, ffn1_norm, ffn1_w_gate, ffn1_w_up, ffn1_w_down, mix_norm, w_in, mla_q_a_norm, mla_w_q_b, mla_kv_a_norm, mla_w_kv_b, mla_q_norm, mla_k_norm, swa_q_norm, swa_k_norm, swa_sinks, mla_out_norm, swa_out_norm, w_o, ffn2_norm, ffn2_w_gate, ffn2_w_up, ffn2_w_down, loss_target, m_ffn1_norm, m_ffn1_w_gate, m_ffn1_w_up, m_ffn1_w_down, m_mix_norm, m_w_in, m_mla_q_a_norm, m_mla_w_q_b, m_mla_kv_a_norm, m_mla_w_kv_b, m_mla_q_norm, m_mla_k_norm, m_swa_q_norm, m_swa_k_norm, m_swa_sinks, m_mla_out_norm, m_swa_out_norm, m_w_o, m_ffn2_norm, m_ffn2_w_gate, m_ffn2_w_up, m_ffn2_w_down, v_ffn1_norm, v_ffn1_w_gate, v_ffn1_w_up, v_ffn1_w_down, v_mix_norm, v_w_in, v_mla_q_a_norm, v_mla_w_q_b, v_mla_kv_a_norm, v_mla_w_kv_b, v_mla_q_norm, v_mla_k_norm, v_swa_q_norm, v_swa_k_norm, v_swa_sinks, v_mla_out_norm, v_swa_out_norm, v_w_o, v_ffn2_norm, v_ffn2_w_gate, v_ffn2_w_up, v_ffn2_w_down):
    local = dict(locals())
    w = {n: local[n] for n in WEIGHTS}
    m = {n: local["m_" + n] for n in WEIGHTS}
    v = {n: local["v_" + n] for n in WEIGHTS}
    depth = ffn1_norm.shape[0]
    t, d = x.shape[-2], x.shape[-1]
    x2d = x.reshape(t, d)
    target = loss_target.reshape(t, d)
    bq = min(_BQ, t)

    big = {n: w[n] for n in BIG}
    packed = [[[_mx(a) for a in _pack_layer(big, l, d, names)] for names in GATHER_ORDER] for l in range(depth)]
    cos, sin_s = _rope_tables(t)
    x_i, y_i, c_i = _position()
    dest_idx = jnp.stack([4 * px + 2 * py + c_i for px, py in _relations(x_i, y_i)]).astype(jnp.int32)

    params, saved = [], []
    xc = x2d
    ffn1_full = _unpack_full(_all_gather(packed[0][0]), big, FFN1_BIG)
    for l in range(depth):
        pr = {"ffn1": _ffn_params(ffn1_full, w, l, "ffn1")}
        x0 = xc
        (x1, *kept1), (mixer_gathered,) = _ffn_fwd(x0, *pr["ffn1"], sides=[_side_gather(packed[l][1])])
        pr.update(_mixer_params(_unpack_full(mixer_gathered, big, OTHER_BIG), w, l))
        qa, ka, va, qb, kb, vb = _prep_fwd(x1, cos, sin_s, pr["prep"])
        (oa, lse_a), (ffn2_gathered,) = _mla_fwd(qa, ka, va, sides=[_side_gather(packed[l][2])])
        pr["ffn2"] = _ffn_params(_unpack_full(ffn2_gathered, big, FFN2_BIG), w, l, "ffn2")
        ob, lse_b = _swa_fwd(qb, kb, vb, pr["sinks"])
        x2 = _mix_out_fwd(x1, oa, ob, pr["ga"], pr["gb"], pr["wo_a"], pr["wo_b"])
        if l + 1 < depth:
            (xc, *kept2), (next_gathered,) = _ffn_fwd(x2, *pr["ffn2"], sides=[_side_gather(packed[l + 1][0])])
            ffn1_full = _unpack_full(next_gathered, big, FFN1_BIG)
        else:
            (dx, sq_err, *kept2), _ = _ffn_fwd(x2, *pr["ffn2"], target=target)
        params.append(pr)
        saved.append((x0, kept1, x1, qa, ka, va, qb, kb, vb, oa, lse_a, ob, lse_b, x2, kept2))

    loss = lax.psum(0.5 / d * sq_err[0, 0], MESH_AXES)

    def chip_sums(arrays, sibling_parts):
        sums = _rs_chip_sums(arrays, sibling_parts, dest_idx)
        return sums[:len(arrays)], sums[len(arrays):]

    grad_shards = [None] * depth
    small_grads = {n: [None] * depth for n in SMALL}
    upper = None
    for l in reversed(range(depth)):
        pr = params[l]
        lowest = l == 0
        x0, kept1, x1, qa, ka, va, qb, kb, vb, oa, lse_a, ob, lse_b, x2, kept2 = saved[l]
        g = {}
        dx, small_grads["ffn2_norm"][l], g["ffn2_w_gate"], g["ffn2_w_up"], g["ffn2_w_down"], side_out = _ffn_backward(
            x2, dx, kept2, pr["ffn2"], "ffn2", [_side_sibling(upper)] if upper else [])
        if upper:
            upper_owns, upper_wires = chip_sums(upper, side_out[0])
        early = _grads_by_destination(g, d, FFN2_BIG) if lowest else None

        doa, dsum_a, dob, dsum_b, mixed, dyb, dga, dgb = _mix_out_bwd(
            dx, oa, ob, pr["ga"], pr["gb"], pr["wo_a"], pr["wo_b"])
        small_grads["mla_out_norm"][l] = dga.reshape(-1)
        small_grads["swa_out_norm"][l] = dgb.reshape(-1)
        g["w_o"], _ = _tn_matmul(mixed, dyb, "wgrad_wo")

        rows_of = lambda s: s.reshape(MLA_HEADS, t // bq, 1, bq)
        sides = ([_side_chips(upper_wires)] if upper else []) + ([_side_sibling(early)] if lowest else [])
        (dqa, dka, dva), side_out = _mla_bwd(qa, ka, va, doa, rows_of(lse_a), rows_of(dsum_a), sides)
        if upper:
            grad_shards[l + 1] = _shards_from_rows(_rs_sum(upper_owns, side_out[0]), big)
        if lowest:
            early_owns, early_wires = chip_sums(early, side_out[-1])
        dqb, dkb, dvb, dsinks = _swa_bwd(qb, kb, vb, pr["sinks"], dob, lse_b, dsum_b)
        small_grads["swa_sinks"][l] = dsinks[:, 0, 0]

        outs, side_out = _prep_bwd(x1, dx, cos, sin_s, pr["prep"], dqa, dka, dva, dqb, dkb, dvb,
                                   [_side_chips(early_wires)] if lowest else [])
        if lowest:
            early_rows = _rs_sum(early_owns, side_out[0])
        dx = outs[0]
        pg = dict(zip(PREP_WEIGHTS, outs[1:]))
        g["w_in"] = jnp.concatenate([pg["w_in"][:W_IN_COLS[0]], pg["w_in"][C_QS:]], axis=0)
        g["mla_w_q_b"] = pg["wqb"][:, :MLA_QK].reshape(MLA_HEADS * MLA_QK, MLA_Q_RANK)
        g["mla_w_kv_b"] = pg["w_kvb"]
        fold = lambda gg: gg[0, :HALF] + gg[0, HALF:]
        small_grads["mix_norm"][l] = pg["mix_g"][0]
        small_grads["mla_q_a_norm"][l] = pg["g_qa"][0]
        small_grads["mla_kv_a_norm"][l] = pg["g_kva"][0]
        small_grads["mla_q_norm"][l] = jnp.concatenate([pg["gq_n"][0], pg["gq_r"][0, :MLA_ROPE]])
        small_grads["mla_k_norm"][l] = jnp.concatenate([pg["gk_n"][0], pg["gk_r"][0, :MLA_ROPE]])
        small_grads["swa_q_norm"][l] = fold(pg["g_sq"])
        small_grads["swa_k_norm"][l] = fold(pg["g_sk"])

        if lowest:
            other = _grads_by_destination(g, d, OTHER_BIG)
            dx, small_grads["ffn1_norm"][l], g["ffn1_w_gate"], g["ffn1_w_up"], other_rows, down_rows = (
                _ffn_backward_exchanging(x0, dx, kept1, pr["ffn1"], "ffn1", other, chip_sums, d))
            late = _grads_by_destination(g, d, FFN1_BIG[:2])
            late_owns, late_wires = chip_sums(late, _rs_sibling_exchange(late))
            late_rows = _rs_sum(late_owns, _rs_chip_exchange(late_wires))
            grad_shards[l] = _shards_from_rows(late_rows + down_rows + early_rows + other_rows, big)
        else:
            dx, small_grads["ffn1_norm"][l], g["ffn1_w_gate"], g["ffn1_w_up"], g["ffn1_w_down"], _ = _ffn_backward(
                x0, dx, kept1, pr["ffn1"], "ffn1")
            upper = _grads_by_destination(g, d, FFN_BIG) + _grads_by_destination(g, d, OTHER_BIG)

    grad_big, delta_big, new_m_big, new_v_big = {}, {}, {}, {}
    for n in BIG:
        g_st = jnp.stack([grad_shards[l][n] for l in range(depth)])
        d_st, m_st, v_st = _adamw(g_st, _stored(w[n], n), _stored(m[n], n), _stored(v[n], n), n)
        grad_big[n], delta_big[n], new_m_big[n], new_v_big[n] = (_stored(a, n) for a in (g_st, d_st, m_st, v_st))

    small_partial = _pack_small({n: jnp.stack(small_grads[n]) for n in SMALL})
    grad_small = _unpack_small(_all_reduce_small(small_partial), w)
    updated = _adamw_small(*[[a[n] for n in SMALL] for a in (grad_small, w, m, v)])
    count = len(SMALL)
    delta_small, new_m_small, new_v_small = (dict(zip(SMALL, updated[k * count:(k + 1) * count])) for k in range(3))

    def ordered(big, small):
        return [big[n] if n in big else small[n] for n in WEIGHTS]

    return (loss, dx.reshape(x.shape), *ordered(grad_big, grad_small), *ordered(delta_big, delta_small),
            *ordered(new_m_big, new_m_small), *ordered(new_v_big, new_v_small))
```
